```python
import math
import jax, jax.numpy as jnp
from jax import lax
import numpy as np

D_MODEL = 1024
BATCH = 8
SEQ = 2048
DEPTH = 2
DEC_BATCH = 32
DEC_SEQ = 8
PAST_LEN = 16384
PAGE_SIZE = 128

N_EVEN = (DEPTH + 1) // 2
N_ODD = DEPTH // 2
NORM_EPS = 1e-6

MLA_HEADS = 8
MLA_NOPE = 64
MLA_ROPE = 32
MLA_V = 64
MLA_Q_RANK = 384
MLA_KV_RANK = 256
MLA_LAT = MLA_KV_RANK + MLA_ROPE
MLA_SCALE = (MLA_NOPE + MLA_ROPE) ** -0.5
ROPE_THETA = 10000.0
Q_BLOCK = 128

RWKV_HEADS = 8
RWKV_HEAD = 64
RWKV_DIM = RWKV_HEADS * RWKV_HEAD
RWKV_W_LORA = 64
RWKV_A_LORA = 64
RWKV_G_LORA = 128
RWKV_PROJ = 3 * RWKV_DIM + RWKV_W_LORA + RWKV_A_LORA + RWKV_G_LORA
RWKV_SPLITS = [RWKV_DIM, 2 * RWKV_DIM, 3 * RWKV_DIM, 3 * RWKV_DIM + RWKV_W_LORA, 3 * RWKV_DIM + RWKV_W_LORA + RWKV_A_LORA]
RWKV_LN_EPS = 64e-5

EVEN_IN = MLA_Q_RANK + MLA_LAT + RWKV_PROJ
EVEN_MIX = MLA_HEADS * MLA_V + RWKV_DIM

GLA_HEADS = 4
GLA_DK = 128
GLA_DV = 256
GLA_KDIM = GLA_HEADS * GLA_DK
GLA_VDIM = GLA_HEADS * GLA_DV
GLA_GATE_RANK = 16
GLA_GATE_NORM = 16.0
GLA_CHUNK = 64
GLA_SPLITS = [GLA_KDIM, 2 * GLA_KDIM, 2 * GLA_KDIM + GLA_VDIM, 2 * GLA_KDIM + 2 * GLA_VDIM]
ODD_IN = 2 * GLA_KDIM + 2 * GLA_VDIM + GLA_GATE_RANK

D_FF = 2816
N_EXPERTS = 8
TOP_K = 2
D_FF_EXPERT = 3584

kernel_name = 'mla_rwkv7_gla_moe_decode_step'


def rmsnorm(x, g, eps=NORM_EPS):
    xf = x.astype(jnp.float32)
    y = xf * lax.rsqrt(jnp.mean(xf * xf, axis=-1, keepdims=True) + eps)
    return (y * g.astype(jnp.float32)).astype(x.dtype)


def rope_angles(pos):
    inv = ROPE_THETA ** (-jnp.arange(0, MLA_ROPE, 2, dtype=jnp.float32) / MLA_ROPE)
    ang = pos.astype(jnp.float32)[:, None] * inv[None, :]
    return jnp.cos(ang), jnp.sin(ang)


def apply_rope(x, cos, sin):
    x1, x2 = jnp.split(x.astype(jnp.float32), 2, axis=-1)
    return jnp.concatenate([x1 * cos - x2 * sin, x1 * sin + x2 * cos], axis=-1).astype(x.dtype)


def swiglu(x, w_gu, w_down):
    gate, up = jnp.split(x @ w_gu, 2, axis=-1)
    return (jax.nn.silu(gate) * up) @ w_down


def mla_attend(q_lat, q_pe, q_pos, c_kv, k_pe, k_pos, w_uv):
    s = (jnp.einsum('bthc,bsc->bhts', q_lat, c_kv) + jnp.einsum('bthr,bsr->bhts', q_pe, k_pe)).astype(jnp.float32) * MLA_SCALE
    s = jnp.where(k_pos[None, :] <= q_pos[:, None], s, -jnp.inf)
    p = jax.nn.softmax(s, axis=-1).astype(c_kv.dtype)
    o_lat = jnp.einsum('bhts,bsc->bthc', p, c_kv)
    return jnp.einsum('bthc,chv->bthv', o_lat, w_uv)


def mla_prompt(q_lat, q_pe, c_kv, k_pe, w_uv):
    B, T, H, C = q_lat.shape
    nb = T // Q_BLOCK
    k_pos = jnp.arange(T)

    def blk(args):
        ql, qp, qpos = args
        return mla_attend(ql, qp, qpos, c_kv, k_pe, k_pos, w_uv)

    ql = q_lat.reshape(B, nb, Q_BLOCK, H, C).swapaxes(0, 1)
    qp = q_pe.reshape(B, nb, Q_BLOCK, H, MLA_ROPE).swapaxes(0, 1)
    qpos = jnp.arange(T).reshape(nb, Q_BLOCK)
    o = lax.map(blk, (ql, qp, qpos))
    return o.swapaxes(0, 1).reshape(B, T, H, MLA_V)


def rwkv7_scan(r, w, k, v, a, b, S0):
    xs = tuple(jnp.moveaxis(t.astype(jnp.float32), 1, 0) for t in (r, w, k, v, a, b))

    def step(S, inp):
        rt, wt, kt, vt, at, bt = inp
        sa = jnp.einsum('bhvk,bhk->bhv', S, at)
        S = S * wt[:, :, None, :] + sa[..., None] * bt[:, :, None, :] + vt[..., None] * kt[:, :, None, :]
        return S, jnp.einsum('bhvk,bhk->bhv', S, rt)

    S, y = lax.scan(step, S0.astype(jnp.float32), xs)
    return jnp.moveaxis(y, 0, 1).astype(r.dtype), S.astype(S0.dtype)


def gla_chunked(q, k, v, log_a, S0):
    B, T, H, DK = q.shape
    DV = v.shape[-1]
    C = min(GLA_CHUNK, T)
    n = -(-T // C)
    pad = n * C - T

    def blocks(t):
        t = jnp.pad(t.astype(jnp.float32), ((0, 0), (0, pad), (0, 0), (0, 0)))
        return jnp.transpose(t.reshape(B, n, C, H, t.shape[-1]), (1, 0, 3, 2, 4))

    causal = jnp.tril(jnp.ones((C, C), dtype=bool))

    def step(S, inp):
        qc, kc, vc, gc = inp
        b = jnp.cumsum(gc, axis=-2)
        inter = jnp.einsum('bhtd,bhdv->bhtv', qc * jnp.exp(b), S)
        diff = b[:, :, :, None, :] - b[:, :, None, :, :]
        decay = jnp.exp(jnp.where(causal[:, :, None], diff, -jnp.inf))
        A = jnp.einsum('bhtd,bhsd,bhtsd->bhts', qc, kc, decay)
        intra = jnp.einsum('bhts,bhsv->bhtv', A, vc)
        b_end = b[:, :, -1:, :]
        S = S * jnp.exp(b_end)[:, :, 0, :, None] + jnp.einsum('bhsd,bhsv->bhdv', kc * jnp.exp(b_end - b), vc)
        return S, inter + intra

    S, o = lax.scan(step, S0.astype(jnp.float32), (blocks(q), blocks(k), blocks(v), blocks(log_a)))
    o = jnp.transpose(o, (1, 0, 3, 2, 4)).reshape(B, n * C, H, DV)[:, :T]
    return o.astype(v.dtype), S.astype(S0.dtype)


def even_mixer(h, pos, S0, shift0, past_lat, p):
    B, T, _ = h.shape
    proj = h @ p['w_in']
    q_c, kv_c, rw = jnp.split(proj, [MLA_Q_RANK, MLA_Q_RANK + MLA_LAT], axis=-1)
    c_q = rmsnorm(q_c, p['q_norm'])
    q = (c_q @ p['w_uq']).reshape(B, T, MLA_HEADS, MLA_NOPE + MLA_ROPE)
    q_nope, q_pe = jnp.split(q, [MLA_NOPE], axis=-1)
    cos, sin = rope_angles(pos)
    q_pe = apply_rope(q_pe, cos[None, :, None, :], sin[None, :, None, :])
    c_kv = rmsnorm(kv_c[..., :MLA_KV_RANK], p['kv_norm'])
    k_pe = apply_rope(kv_c[..., MLA_KV_RANK:], cos[None], sin[None])
    new_lat = jnp.concatenate([c_kv, k_pe], axis=-1)
    q_lat = jnp.einsum('bthn,chn->bthc', q_nope, p['w_uk'])
    if past_lat is None:
        o_a = mla_prompt(q_lat, q_pe, c_kv, k_pe, p['w_uv'])
    else:
        lat = jnp.concatenate([past_lat.astype(new_lat.dtype), new_lat], axis=1)
        o_a = mla_attend(q_lat, q_pe, pos, lat[..., :MLA_KV_RANK], lat[..., MLA_KV_RANK:], jnp.arange(lat.shape[1]), p['w_uv'])
    prev = jnp.concatenate([shift0[:, None, :].astype(rw.dtype), rw[:, :-1]], axis=1)
    xs = rw + (prev - rw) * p['mu']
    r, k, v, xw, xa, xg = jnp.split(xs, RWKV_SPLITS, axis=-1)
    w_log = -jax.nn.softplus(-(p['w0'] + jnp.tanh(xw) @ p['w2']).astype(jnp.float32)) - 0.5
    decay = jnp.exp(-jnp.exp(w_log))
    a = jax.nn.sigmoid(p['a0'] + xa @ p['a2'])
    g = jax.nn.sigmoid(xg) @ p['g2']
    shp = (B, T, RWKV_HEADS, RWKV_HEAD)
    r, k, v, a, decay = r.reshape(shp), k.reshape(shp), v.reshape(shp), a.reshape(shp), decay.reshape(shp)
    kk = (k * p['k_k']).astype(jnp.float32)
    kk = (kk / jnp.maximum(jnp.sqrt(jnp.sum(kk * kk, axis=-1, keepdims=True)), 1e-12)).astype(k.dtype)
    k = k * (1 + (a - 1) * p['k_a'])
    y, S = rwkv7_scan(r, decay, k, v, -kk, kk * a, S0)
    yf = y.astype(jnp.float32)
    mean = jnp.mean(yf, axis=-1, keepdims=True)
    var = jnp.mean(jnp.square(yf - mean), axis=-1, keepdims=True)
    yn = ((yf - mean) * lax.rsqrt(var + RWKV_LN_EPS) * p['ln_g'] + p['ln_b']).astype(v.dtype)
    yn = yn + jnp.sum(r * k * p['r_k'], axis=-1, keepdims=True) * v
    o_b = yn.reshape(B, T, RWKV_DIM) * g
    mix = jnp.concatenate([o_a.reshape(B, T, MLA_HEADS * MLA_V), o_b], axis=-1) @ p['w_out']
    return mix, new_lat, S, rw[:, -1]


def odd_mixer(h, S0, p):
    B, T, _ = h.shape
    proj = h @ p['w_in']
    q, k, v, g, xa = jnp.split(proj, GLA_SPLITS, axis=-1)
    log_a = jax.nn.log_sigmoid((xa @ p['a2'] + p['ab']).astype(jnp.float32)) / GLA_GATE_NORM
    q = q.reshape(B, T, GLA_HEADS, GLA_DK) * (GLA_DK ** -0.5)
    k = k.reshape(B, T, GLA_HEADS, GLA_DK)
    v = v.reshape(B, T, GLA_HEADS, GLA_DV)
    log_a = log_a.reshape(B, T, GLA_HEADS, GLA_DK)
    o, S = gla_chunked(q, k, v, log_a, S0)
    o = rmsnorm(o, p['norm']).reshape(B, T, GLA_VDIM) * jax.nn.silu(g)
    return o @ p['w_out'], S


def moe_ffn(h, router, w_gu, w_down):
    B, T, D = h.shape
    xf = h.reshape(-1, D)
    logits = (xf @ router).astype(jnp.float32)
    top_v, top_i = lax.top_k(logits, TOP_K)
    gates = jax.nn.softmax(top_v, axis=-1)
    comb = jnp.einsum('nk,nke->ne', gates, jax.nn.one_hot(top_i, N_EXPERTS, dtype=jnp.float32)).astype(h.dtype)
    out = jnp.zeros_like(xf)
    for e in range(N_EXPERTS):
        out = out + comb[:, e:e + 1] * swiglu(xf, w_gu[e], w_down[e])
    return out.reshape(B, T, D)


def even_layer(x, pos, S0, shift0, past_lat, p):
    mix, lat, S, sh = even_mixer(rmsnorm(x, p['norm_mix']), pos, S0, shift0, past_lat, p)
    x = x + mix
    x = x + swiglu(rmsnorm(x, p['norm_ffn']), p['ffn_gu'], p['ffn_down'])
    return x, lat, S, sh


def odd_layer(x, S0, p):
    mix, S = odd_mixer(rmsnorm(x, p['norm_mix']), S0, p)
    x = x + mix
    x = x + moe_ffn(rmsnorm(x, p['norm_ffn']), p['router'], p['moe_gu'], p['moe_down'])
    return x, S


def setup_inputs(seed: int = 0) -> dict:
    key = jax.random.key(seed)
    ks = iter(jax.random.split(key, 48))
    f32 = jnp.float32

    def nrm(shape, scale):
        return scale * jax.random.normal(next(ks), shape, f32)

    def gain(shape):
        return 1.0 + 0.05 * jax.random.normal(next(ks), shape, f32)

    def unif(shape, lo, hi):
        return jax.random.uniform(next(ks), shape, f32, lo, hi)

    n_pages = PAST_LEN // PAGE_SIZE
    n_used = DEC_BATCH * n_pages
    n_pool = n_used + n_used // 4
    perm = jax.random.permutation(next(ks), n_pool)
    page_table = perm[:n_used].reshape(DEC_BATCH, n_pages).astype(jnp.int32)
    D = D_MODEL
    return {
        'x_prompt': nrm((BATCH, SEQ, D), 1.0),
        'x_sample': nrm((DEC_BATCH, DEC_SEQ, D), 1.0),
        'cache_mla': nrm((N_EVEN, n_pool, PAGE_SIZE, MLA_LAT), 1.0),
        'state_rwkv': nrm((N_EVEN, DEC_BATCH, RWKV_HEADS, RWKV_HEAD, RWKV_HEAD), 0.1),
        'state_rwkv_shift': nrm((N_EVEN, DEC_BATCH, RWKV_PROJ), 1.0),
        'state_gla': nrm((N_ODD, DEC_BATCH, GLA_HEADS, GLA_DK, GLA_DV), 0.1),
        'page_table': page_table,
        'norm_mix_even': gain((N_EVEN, D)),
        'norm_ffn_even': gain((N_EVEN, D)),
        'w_in_even': nrm((N_EVEN, D, EVEN_IN), D ** -0.5),
        'mla_q_norm': gain((N_EVEN, MLA_Q_RANK)),
        'mla_kv_norm': gain((N_EVEN, MLA_KV_RANK)),
        'mla_w_uq': nrm((N_EVEN, MLA_Q_RANK, MLA_HEADS * (MLA_NOPE + MLA_ROPE)), MLA_Q_RANK ** -0.5),
        'mla_w_uk': nrm((N_EVEN, MLA_KV_RANK, MLA_HEADS, MLA_NOPE), MLA_KV_RANK ** -0.5),
        'mla_w_uv': nrm((N_EVEN, MLA_KV_RANK, MLA_HEADS, MLA_V), MLA_KV_RANK ** -0.5),
        'rwkv_mu': unif((N_EVEN, RWKV_PROJ), 0.0, 1.0),
        'rwkv_w0': unif((N_EVEN, RWKV_DIM), -2.5, 0.5),
        'rwkv_w2': nrm((N_EVEN, RWKV_W_LORA, RWKV_DIM), 0.1),
        'rwkv_a0': nrm((N_EVEN, RWKV_DIM), 0.5),
        'rwkv_a2': nrm((N_EVEN, RWKV_A_LORA, RWKV_DIM), 0.5 * RWKV_A_LORA ** -0.5),
        'rwkv_g2': nrm((N_EVEN, RWKV_G_LORA, RWKV_DIM), RWKV_G_LORA ** -0.5),
        'rwkv_k_k': gain((N_EVEN, RWKV_HEADS, RWKV_HEAD)),
        'rwkv_k_a': gain((N_EVEN, RWKV_HEADS, RWKV_HEAD)),
        'rwkv_r_k': nrm((N_EVEN, RWKV_HEADS, RWKV_HEAD), 0.1),
        'rwkv_ln_g': gain((N_EVEN, RWKV_HEADS, RWKV_HEAD)),
        'rwkv_ln_b': nrm((N_EVEN, RWKV_HEADS, RWKV_HEAD), 0.02),
        'w_out_even': nrm((N_EVEN, EVEN_MIX, D), EVEN_MIX ** -0.5),
        'ffn_w_gu_even': nrm((N_EVEN, D, 2 * D_FF), D ** -0.5),
        'ffn_w_down_even': nrm((N_EVEN, D_FF, D), D_FF ** -0.5),
        'norm_mix_odd': gain((N_ODD, D)),
        'norm_ffn_odd': gain((N_ODD, D)),
        'w_in_odd': nrm((N_ODD, D, ODD_IN), D ** -0.5),
        'gla_a2': nrm((N_ODD, GLA_GATE_RANK, GLA_KDIM), GLA_GATE_RANK ** -0.5),
        'gla_ab': nrm((N_ODD, GLA_KDIM), 0.5),
        'gla_norm': gain((N_ODD, GLA_HEADS, GLA_DV)),
        'w_out_odd': nrm((N_ODD, GLA_VDIM, D), GLA_VDIM ** -0.5),
        'moe_router': nrm((N_ODD, D, N_EXPERTS), D ** -0.5),
        'moe_w_gu': nrm((N_ODD, N_EXPERTS, D, 2 * D_FF_EXPERT), D ** -0.5),
        'moe_w_down': nrm((N_ODD, N_EXPERTS, D_FF_EXPERT, D), D_FF_EXPERT ** -0.5),
        'final_norm': gain((D,)),
    }


def reference(x_prompt, x_sample, cache_mla, state_rwkv, state_rwkv_shift, state_gla, page_table,
              norm_mix_even, norm_ffn_even, w_in_even, mla_q_norm, mla_kv_norm, mla_w_uq, mla_w_uk, mla_w_uv,
              rwkv_mu, rwkv_w0, rwkv_w2, rwkv_a0, rwkv_a2, rwkv_g2, rwkv_k_k, rwkv_k_a, rwkv_r_k, rwkv_ln_g, rwkv_ln_b,
              w_out_even, ffn_w_gu_even, ffn_w_down_even,
              norm_mix_odd, norm_ffn_odd, w_in_odd, gla_a2, gla_ab, gla_norm, w_out_odd,
              moe_router, moe_w_gu, moe_w_down, final_norm):
    bp, tp, _ = x_prompt.shape
    pos_p = jnp.arange(tp)
    pos_s = PAST_LEN + jnp.arange(x_sample.shape[1])
    db, n_pages = page_table.shape
    hp, hs = x_prompt, x_sample
    lat_p, lat_s, rs_p, rs_s, sh_p, sh_s, gs_p, gs_s = [], [], [], [], [], [], [], []
    for layer in range(DEPTH):
        i = layer // 2
        if layer % 2 == 0:
            p = dict(norm_mix=norm_mix_even[i], norm_ffn=norm_ffn_even[i], w_in=w_in_even[i],
                     q_norm=mla_q_norm[i], kv_norm=mla_kv_norm[i], w_uq=mla_w_uq[i], w_uk=mla_w_uk[i], w_uv=mla_w_uv[i],
                     mu=rwkv_mu[i], w0=rwkv_w0[i], w2=rwkv_w2[i], a0=rwkv_a0[i], a2=rwkv_a2[i], g2=rwkv_g2[i],
                     k_k=rwkv_k_k[i], k_a=rwkv_k_a[i], r_k=rwkv_r_k[i], ln_g=rwkv_ln_g[i], ln_b=rwkv_ln_b[i],
                     w_out=w_out_even[i], ffn_gu=ffn_w_gu_even[i], ffn_down=ffn_w_down_even[i])
            S0_p = jnp.zeros((bp, RWKV_HEADS, RWKV_HEAD, RWKV_HEAD), x_prompt.dtype)
            shift0_p = jnp.zeros((bp, RWKV_PROJ), x_prompt.dtype)
            hp, lp, sp, shp = even_layer(hp, pos_p, S0_p, shift0_p, None, p)
            past = cache_mla[i][page_table].reshape(db, n_pages * PAGE_SIZE, MLA_LAT)
            hs, ls, ss, shs = even_layer(hs, pos_s, state_rwkv[i], state_rwkv_shift[i], past, p)
            lat_p.append(lp); lat_s.append(ls); rs_p.append(sp); rs_s.append(ss); sh_p.append(shp); sh_s.append(shs)
        else:
            p = dict(norm_mix=norm_mix_odd[i], norm_ffn=norm_ffn_odd[i], w_in=w_in_odd[i], a2=gla_a2[i], ab=gla_ab[i],
                     norm=gla_norm[i], w_out=w_out_odd[i], router=moe_router[i], moe_gu=moe_w_gu[i], moe_down=moe_w_down[i])
            G0_p = jnp.zeros((bp, GLA_HEADS, GLA_DK, GLA_DV), x_prompt.dtype)
            hp, gp = odd_layer(hp, G0_p, p)
            hs, gsa = odd_layer(hs, state_gla[i], p)
            gs_p.append(gp); gs_s.append(gsa)
    y_prompt = rmsnorm(hp, final_norm)
    y_sample = rmsnorm(hs, final_norm)
    new_mla_prompt = jnp.stack(lat_p)
    new_mla_sample = jnp.stack(lat_s)
    new_rwkv_prompt = jnp.stack(rs_p)
    new_rwkv_sample = jnp.stack(rs_s)
    new_shift_prompt = jnp.stack(sh_p)
    new_shift_sample = jnp.stack(sh_s)
    new_gla_prompt = jnp.stack(gs_p)
    new_gla_sample = jnp.stack(gs_s)
    return (y_prompt, y_sample, new_mla_prompt, new_mla_sample, new_rwkv_prompt, new_rwkv_sample,
            new_shift_prompt, new_shift_sample, new_gla_prompt, new_gla_sample)
```

```python
import functools

import jax
import jax.numpy as jnp
from jax import lax
from jax.experimental import pallas as pl
from jax.experimental.pallas import tpu as pltpu

F32 = jnp.float32
BF16 = jnp.bfloat16

D_MODEL = 1024
PAGE_SIZE = 128
NORM_EPS = 1e-6

MLA_HEADS = 8
MLA_NOPE = 64
MLA_ROPE = 32
MLA_V = 64
MLA_Q_RANK = 384
MLA_KV_RANK = 256
MLA_LAT = MLA_KV_RANK + MLA_ROPE
MLA_SCALE = (MLA_NOPE + MLA_ROPE) ** -0.5
ROPE_THETA = 10000.0

RWKV_HEADS = 8
RWKV_HEAD = 64
RWKV_DIM = RWKV_HEADS * RWKV_HEAD
RWKV_W_LORA = 64
RWKV_A_LORA = 64
RWKV_G_LORA = 128
RWKV_PROJ = 3 * RWKV_DIM + RWKV_W_LORA + RWKV_A_LORA + RWKV_G_LORA
RWKV_LN_EPS = 64e-5

GLA_HEADS = 4
GLA_DK = 128
GLA_DV = 256
GLA_KDIM = GLA_HEADS * GLA_DK
GLA_VDIM = GLA_HEADS * GLA_DV
GLA_GATE_RANK = 16
GLA_GATE_NORM = 16.0
GLA_CHUNK = 128

D_FF = 2816
N_EXPERTS = 8
D_FF_EXPERT = 3584

LANES = 128
VMEM_LIMIT = 56 * 1024 * 1024
NEG_BIG = -1e30


def _cparams(sem):
    return pltpu.CompilerParams(dimension_semantics=sem, vmem_limit_bytes=VMEM_LIMIT)


def _const_spec(shape):
    nd = len(shape)
    return pl.BlockSpec(shape, lambda *_: (0,) * nd)


def _row_spec(tm, width):
    return pl.BlockSpec((tm, width), lambda i: (i, 0))


def _dot(a, b):
    return jnp.dot(a.astype(BF16), b.astype(BF16), preferred_element_type=F32)


def _dot_nt(a, b):
    return lax.dot_general(a.astype(BF16), b.astype(BF16), (((1,), (1,)), ((), ())),
                           preferred_element_type=F32)


def _split2(x):
    hi = x.astype(BF16)
    lo = (x - hi.astype(F32)).astype(BF16)
    return hi, lo


def _split3(x):
    hi = x.astype(BF16)
    r1 = x - hi.astype(F32)
    mid = r1.astype(BF16)
    lo = (r1 - mid.astype(F32)).astype(BF16)
    return hi, mid, lo


def _dot_exact_rhs(x, e):
    hi, mid, lo = _split3(x)
    return (jnp.dot(hi, e, preferred_element_type=F32) + jnp.dot(mid, e, preferred_element_type=F32)
            + jnp.dot(lo, e, preferred_element_type=F32))


def _dot_exact_lhs(e, x):
    hi, mid, lo = _split3(x)
    return (jnp.dot(e, hi, preferred_element_type=F32) + jnp.dot(e, mid, preferred_element_type=F32)
            + jnp.dot(e, lo, preferred_element_type=F32))


def _dot_f32ish(a, b):
    ah, al = _split2(a)
    bh, bl = _split2(b)
    return (jnp.dot(ah, bh, preferred_element_type=F32) + jnp.dot(ah, bl, preferred_element_type=F32)
            + jnp.dot(al, bh, preferred_element_type=F32))


def _rms(x, g, eps=NORM_EPS):
    return x * lax.rsqrt(jnp.mean(x * x, axis=-1, keepdims=True) + eps) * g


def _sigmoid(x):
    return 1.0 / (1.0 + jnp.exp(-x))


def _softplus(x):
    return jnp.maximum(x, 0.0) + jnp.log(1.0 + jnp.exp(-jnp.abs(x)))


def _fold_qlat_kernel(uq_ref, uk_ref, o_ref):
    a = uq_ref[...]
    b = uk_ref[...]
    ah, al = _split2(a)
    bh, bl = _split2(b)
    dn = (((1,), (1,)), ((), ()))
    o = (lax.dot_general(ah, bh, dn, preferred_element_type=F32)
         + lax.dot_general(ah, bl, dn, preferred_element_type=F32)
         + lax.dot_general(al, bh, dn, preferred_element_type=F32))
    o_ref[...] = o.astype(BF16)


def _fold_qlat(uq_nope, uk):
    return pl.pallas_call(
        _fold_qlat_kernel,
        grid=(MLA_HEADS,),
        in_specs=[pl.BlockSpec((None, MLA_Q_RANK, MLA_NOPE), lambda h: (h, 0, 0)),
                  pl.BlockSpec((None, MLA_KV_RANK, MLA_NOPE), lambda h: (h, 0, 0))],
        out_specs=pl.BlockSpec((MLA_Q_RANK, MLA_KV_RANK), lambda h: (0, h)),
        out_shape=jax.ShapeDtypeStruct((MLA_Q_RANK, MLA_HEADS * MLA_KV_RANK), BF16),
        compiler_params=_cparams(("arbitrary",)),
    )(uq_nope, uk)


def _even_in_kernel(x_ref, g_ref, wq_ref, wckv_ref, wpa_ref, wpb_ref, wrw_ref, qn_ref, kvn_ref,
                    cs_ref, sn_ref, wql_ref, wqa_ref, wqb_ref, cs8_ref, sn8_ref,
                    lat_ref, latb_ref, ql_ref, qpe_ref, rw_ref):
    xn = _rms(x_ref[...], g_ref[...]).astype(BF16)
    cq = _rms(_dot(xn, wq_ref[...]), qn_ref[...]).astype(BF16)
    ql_ref[...] = _dot(cq, wql_ref[...]).astype(BF16)
    qpe = _dot(cq, wqa_ref[...]) * cs8_ref[...] + _dot(cq, wqb_ref[...]) * sn8_ref[...]
    qpe_ref[...] = qpe.astype(BF16)
    ckv = _rms(_dot(xn, wckv_ref[...]), kvn_ref[...])
    kpe = _dot(xn, wpa_ref[...]) * cs_ref[...] + _dot(xn, wpb_ref[...]) * sn_ref[...]
    lat_ref[:, :MLA_KV_RANK] = ckv
    lat_ref[:, MLA_KV_RANK:] = kpe
    latb_ref[:, :MLA_KV_RANK] = ckv.astype(BF16)
    latb_ref[:, MLA_KV_RANK:] = kpe.astype(BF16)
    rw_ref[...] = _dot(xn, wrw_ref[...])


def _even_in(x, w, tabs, tm):
    n = x.shape[0]
    nt = tabs["cs"].shape[0] // tm
    tab = lambda width: pl.BlockSpec((tm, width), lambda i: (i % nt, 0))
    hq = MLA_HEADS * MLA_KV_RANK
    hr = MLA_HEADS * MLA_ROPE
    return pl.pallas_call(
        _even_in_kernel,
        grid=(n // tm,),
        in_specs=[_row_spec(tm, D_MODEL), _const_spec((1, D_MODEL)),
                  _const_spec((D_MODEL, MLA_Q_RANK)), _const_spec((D_MODEL, MLA_KV_RANK)),
                  _const_spec((D_MODEL, MLA_ROPE)), _const_spec((D_MODEL, MLA_ROPE)),
                  _const_spec((D_MODEL, RWKV_PROJ)), _const_spec((1, MLA_Q_RANK)),
                  _const_spec((1, MLA_KV_RANK)), tab(MLA_ROPE), tab(MLA_ROPE),
                  _const_spec((MLA_Q_RANK, hq)), _const_spec((MLA_Q_RANK, hr)),
                  _const_spec((MLA_Q_RANK, hr)), tab(hr), tab(hr)],
        out_specs=[_row_spec(tm, MLA_LAT), _row_spec(tm, MLA_LAT), _row_spec(tm, hq),
                   _row_spec(tm, hr), _row_spec(tm, RWKV_PROJ)],
        out_shape=[jax.ShapeDtypeStruct((n, MLA_LAT), F32), jax.ShapeDtypeStruct((n, MLA_LAT), BF16),
                   jax.ShapeDtypeStruct((n, hq), BF16), jax.ShapeDtypeStruct((n, hr), BF16),
                   jax.ShapeDtypeStruct((n, RWKV_PROJ), F32)],
        compiler_params=_cparams(("parallel",)),
    )(x, w["norm_mix"], w["w_q"], w["w_ckv"], w["w_pe_a"], w["w_pe_b"], w["w_rw"], w["q_norm"],
      w["kv_norm"], tabs["cs"], tabs["sn"], w["w_qlat"], w["w_qpe_a"], w["w_qpe_b"],
      tabs["cs8"], tabs["sn8"])


ATT_TQ = 256
ATT_SUB = 256


def _mla_prompt_kernel(q_ref, qpe_ref, lat_ref, o_ref, m_sc, l_sc, acc_sc):
    i = pl.program_id(1)
    j = pl.program_id(2)
    rows = ATT_TQ * MLA_HEADS

    @pl.when(j == 0)
    def _():
        m_sc[...] = jnp.full(m_sc.shape, NEG_BIG, F32)
        l_sc[...] = jnp.zeros(l_sc.shape, F32)
        acc_sc[...] = jnp.zeros(acc_sc.shape, F32)

    def tile(masked):
        ckv = lat_ref[:, :MLA_KV_RANK]
        kpe = lat_ref[:, MLA_KV_RANK:]

        def sub(r, carry):
            rs = pl.ds(pl.multiple_of(r * ATT_SUB, ATT_SUB), ATT_SUB)
            s = (_dot_nt(q_ref[rs, :], ckv) + _dot_nt(qpe_ref[rs, :], kpe)) * MLA_SCALE
            if masked:
                tok = (r * ATT_SUB + lax.broadcasted_iota(jnp.int32, s.shape, 0)) >> 3
                key = lax.broadcasted_iota(jnp.int32, s.shape, 1)
                s = jnp.where(key <= tok, s, NEG_BIG)
            m_prev = m_sc[rs, :]
            m_new = jnp.maximum(m_prev, jnp.max(s, axis=-1, keepdims=True))
            alpha = jnp.exp(m_prev - m_new)
            p = jnp.exp(s - m_new)
            l_sc[rs, :] = alpha * l_sc[rs, :] + jnp.sum(p, axis=-1, keepdims=True)
            acc_sc[rs, :] = alpha * acc_sc[rs, :] + _dot(p, ckv)
            m_sc[rs, :] = m_new
            return carry

        lax.fori_loop(0, rows // ATT_SUB, sub, 0)

    @pl.when(j < i)
    def _():
        tile(False)

    @pl.when(j == i)
    def _():
        tile(True)
        o_ref[...] = (acc_sc[...] / l_sc[...]).astype(BF16)


def _mla_prompt(q_lat, q_pe, lat_b, batch, seq):
    nq = seq // ATT_TQ
    rows = ATT_TQ * MLA_HEADS
    return pl.pallas_call(
        _mla_prompt_kernel,
        grid=(batch, nq, nq),
        in_specs=[pl.BlockSpec((rows, MLA_KV_RANK), lambda b, i, j: (b * nq + i, 0)),
                  pl.BlockSpec((rows, MLA_ROPE), lambda b, i, j: (b * nq + i, 0)),
                  pl.BlockSpec((ATT_TQ, MLA_LAT), lambda b, i, j: (b * nq + jnp.minimum(i, j), 0))],
        out_specs=pl.BlockSpec((rows, MLA_KV_RANK), lambda b, i, j: (b * nq + i, 0)),
        out_shape=jax.ShapeDtypeStruct(q_lat.shape, BF16),
        scratch_shapes=[pltpu.VMEM((rows, 1), F32), pltpu.VMEM((rows, 1), F32),
                        pltpu.VMEM((rows, MLA_KV_RANK), F32)],
        compiler_params=_cparams(("parallel", "parallel", "arbitrary")),
    )(q_lat, q_pe, lat_b)


PAGES_PER_STEP = 8


def _mla_decode_kernel(pt_ref, q_ref, new_ref, *rest):
    page_refs = rest[:PAGES_PER_STEP]
    o_ref, m_sc, l_sc, acc_sc = rest[PAGES_PER_STEP:]
    j = pl.program_id(1)
    q = q_ref[0]

    @pl.when(j == 0)
    def _():
        m_sc[...] = jnp.full(m_sc.shape, NEG_BIG, F32)
        l_sc[...] = jnp.zeros(l_sc.shape, F32)
        acc_sc[...] = jnp.zeros(acc_sc.shape, F32)

    def update(s, values):
        m_prev = m_sc[...]
        m_new = jnp.maximum(m_prev, jnp.max(s, axis=-1, keepdims=True))
        alpha = jnp.exp(m_prev - m_new)
        p = jnp.exp(s - m_new)
        l_sc[...] = alpha * l_sc[...] + jnp.sum(p, axis=-1, keepdims=True)
        pv = _dot(p[:, :PAGE_SIZE], values[0])
        for c in range(1, len(values)):
            pv = pv + _dot(p[:, c * PAGE_SIZE:(c + 1) * PAGE_SIZE], values[c])
        acc_sc[...] = alpha * acc_sc[...] + pv
        m_sc[...] = m_new

    pages = [pr[...].astype(BF16) for pr in page_refs]
    s = jnp.concatenate([_dot_nt(q, pg) for pg in pages], axis=1) * MLA_SCALE
    update(s, [pg[:, :MLA_KV_RANK] for pg in pages])

    @pl.when(j == pl.num_programs(1) - 1)
    def _():
        new = new_ref[0]
        sn = _dot_nt(q, new) * MLA_SCALE
        tok = lax.broadcasted_iota(jnp.int32, sn.shape, 0) >> 3
        key = lax.broadcasted_iota(jnp.int32, sn.shape, 1)
        sn = jnp.where(key <= tok, sn, NEG_BIG)
        update(sn, [new[:, :MLA_KV_RANK]])
        o_ref[0] = (acc_sc[...] / l_sc[...]).astype(BF16)


def _mla_decode(page_table, q_full, new_pad, cache):
    db, n_pages = page_table.shape
    rows = q_full.shape[1]
    steps = n_pages // PAGES_PER_STEP

    def page_spec(p):
        return pl.BlockSpec((None, PAGE_SIZE, MLA_LAT),
                            lambda b, j, pt: (pt[b, j * PAGES_PER_STEP + p], 0, 0))

    grid_spec = pltpu.PrefetchScalarGridSpec(
        num_scalar_prefetch=1,
        grid=(db, steps),
        in_specs=[pl.BlockSpec((1, rows, MLA_LAT), lambda b, j, pt: (b, 0, 0)),
                  pl.BlockSpec((1, PAGE_SIZE, MLA_LAT), lambda b, j, pt: (b, 0, 0))]
        + [page_spec(p) for p in range(PAGES_PER_STEP)],
        out_specs=pl.BlockSpec((1, rows, MLA_KV_RANK), lambda b, j, pt: (b, 0, 0)),
        scratch_shapes=[pltpu.VMEM((rows, 1), F32), pltpu.VMEM((rows, 1), F32),
                        pltpu.VMEM((rows, MLA_KV_RANK), F32)],
    )
    return pl.pallas_call(
        _mla_decode_kernel,
        grid_spec=grid_spec,
        out_shape=jax.ShapeDtypeStruct((db, rows, MLA_KV_RANK), BF16),
        compiler_params=_cparams(("parallel", "arbitrary")),
    )(page_table, q_full, new_pad, *([cache] * PAGES_PER_STEP))


def _rwkv_prep_kernel(rw_ref, prev_ref, mu_ref, w0_ref, w2_ref, a0_ref, a2_ref, g2_ref, kk_ref,
                      ka_ref, rk_ref, ones_ref,
                      r_ref, w_ref, k_ref, v_ref, na_ref, kb_ref, g_ref, rkv_ref):
    rw = rw_ref[...]
    xs = rw + (prev_ref[...] - rw) * mu_ref[...]
    d = RWKV_DIM
    r = xs[:, :d]
    k = xs[:, d:2 * d]
    v = xs[:, 2 * d:3 * d]
    xwa = xs[:, 3 * d:3 * d + LANES]
    xg = xs[:, 3 * d + LANES:]
    ones = ones_ref[...]
    w_log = -_softplus(-(w0_ref[...] + _dot(jnp.tanh(xwa), w2_ref[...]))) - 0.5
    w_ref[...] = jnp.exp(-jnp.exp(w_log))
    a = _sigmoid(a0_ref[...] + _dot(xwa, a2_ref[...]))
    g_ref[...] = _dot(_sigmoid(xg), g2_ref[...])
    kk = k * kk_ref[...]
    ss = _dot_exact_rhs(kk * kk, ones)
    kk = kk / jnp.maximum(jnp.sqrt(ss), 1e-12)
    k2 = k * (1.0 + (a - 1.0) * ka_ref[...])
    r_ref[...] = r
    k_ref[...] = k2
    v_ref[...] = v
    na_ref[...] = -kk
    kb_ref[...] = kk * a
    rkv_ref[...] = _dot_exact_rhs(r * k2 * rk_ref[...], ones) * v


def _rwkv_prep(rw, prev, w, tm):
    n = rw.shape[0]
    d = RWKV_DIM
    vec = _const_spec((1, d))
    return pl.pallas_call(
        _rwkv_prep_kernel,
        grid=(n // tm,),
        in_specs=[_row_spec(tm, RWKV_PROJ), _row_spec(tm, RWKV_PROJ), _const_spec((1, RWKV_PROJ)),
                  vec, _const_spec((LANES, d)), vec, _const_spec((LANES, d)),
                  _const_spec((RWKV_G_LORA, d)), vec, vec, vec, _const_spec((d, d))],
        out_specs=[_row_spec(tm, d)] * 8,
        out_shape=[jax.ShapeDtypeStruct((n, d), F32)] * 8,
        compiler_params=_cparams(("parallel",)),
    )(rw, prev, w["mu"], w["w0"], w["w2p"], w["a0"], w["a2p"], w["g2"], w["k_k"], w["k_a"],
      w["r_k"], w["ones_bd"])


SCAN_VS = 32


def _rwkv_scan_kernel(a_ref, w_ref, b_ref, k_ref, r_ref, v_ref, s0_ref, y_ref, s_ref, *, tc):
    @pl.when(pl.program_id(1) == 0)
    def _():
        s_ref[...] = s0_ref[...]

    def step(t, carry):
        sa0 = jnp.zeros((SCAN_VS, LANES), F32)
        sa1 = jnp.zeros((SCAN_VS, LANES), F32)
        for k in range(0, RWKV_HEAD, 2):
            sa0 = sa0 + s_ref[0, k] * a_ref[0, t, k:k + 1, :]
            sa1 = sa1 + s_ref[0, k + 1] * a_ref[0, t, k + 1:k + 2, :]
        sa = sa0 + sa1
        v = v_ref[0, t]
        y0 = jnp.zeros((SCAN_VS, LANES), F32)
        y1 = jnp.zeros((SCAN_VS, LANES), F32)
        for k in range(RWKV_HEAD):
            sn = (s_ref[0, k] * w_ref[0, t, k:k + 1, :] + sa * b_ref[0, t, k:k + 1, :]
                  + v * k_ref[0, t, k:k + 1, :])
            s_ref[0, k] = sn
            if k % 2 == 0:
                y0 = y0 + sn * r_ref[0, t, k:k + 1, :]
            else:
                y1 = y1 + sn * r_ref[0, t, k:k + 1, :]
        y_ref[0, t] = y0 + y1
        return carry

    lax.fori_loop(0, tc, step, 0)


def _rwkv_scan(xs, v, s0, tc):
    nbl, t, _, _ = xs[0].shape
    nb = v.shape[0]
    nvb = nb // nbl
    xspec = pl.BlockSpec((1, tc, RWKV_HEAD, LANES), lambda n, c: (n // nvb, c, 0, 0))
    vspec = pl.BlockSpec((1, tc, SCAN_VS, LANES), lambda n, c: (n, c, 0, 0))
    sspec = pl.BlockSpec((1, RWKV_HEAD, SCAN_VS, LANES), lambda n, c: (n, 0, 0, 0))
    return pl.pallas_call(
        functools.partial(_rwkv_scan_kernel, tc=tc),
        grid=(nb, t // tc),
        in_specs=[xspec] * 5 + [vspec, sspec],
        out_specs=[vspec, sspec],
        out_shape=[jax.ShapeDtypeStruct(v.shape, F32), jax.ShapeDtypeStruct(s0.shape, F32)],
        compiler_params=_cparams(("parallel", "arbitrary")),
    )(*xs, v, s0)


def _even_out_kernel(y_ref, rkv_ref, g_ref, lng_ref, lnb_ref, ones_ref, ol_ref, wuv_ref, woa_ref,
                     wob_ref, x_ref, o_ref):
    ones = ones_ref[...]
    y = y_ref[...]
    inv = 1.0 / RWKV_HEAD
    mean = _dot_exact_rhs(y, ones) * inv
    dlt = y - mean
    var = _dot_exact_rhs(dlt * dlt, ones) * inv
    yn = dlt * lax.rsqrt(var + RWKV_LN_EPS) * lng_ref[...] + lnb_ref[...] + rkv_ref[...]
    ob = (yn * g_ref[...]).astype(BF16)
    pair = 2 * MLA_KV_RANK
    oa = jnp.concatenate(
        [_dot(ol_ref[:, p * pair:(p + 1) * pair], wuv_ref[p]) for p in range(MLA_HEADS // 2)], axis=1)
    o_ref[...] = x_ref[...] + _dot(oa, woa_ref[...]) + _dot(ob, wob_ref[...])


def _even_out(y, rkv, g, o_lat, x, w, tm):
    n = x.shape[0]
    d = RWKV_DIM
    hq = MLA_HEADS * MLA_KV_RANK
    return pl.pallas_call(
        _even_out_kernel,
        grid=(n // tm,),
        in_specs=[_row_spec(tm, d), _row_spec(tm, d), _row_spec(tm, d), _const_spec((1, d)),
                  _const_spec((1, d)), _const_spec((d, d)), _row_spec(tm, hq),
                  _const_spec((MLA_HEADS // 2, 2 * MLA_KV_RANK, 2 * MLA_V)),
                  _const_spec((MLA_HEADS * MLA_V, D_MODEL)), _const_spec((d, D_MODEL)),
                  _row_spec(tm, D_MODEL)],
        out_specs=_row_spec(tm, D_MODEL),
        out_shape=jax.ShapeDtypeStruct((n, D_MODEL), F32),
        compiler_params=_cparams(("parallel",)),
    )(y, rkv, g, w["ln_g"], w["ln_b"], w["ones_bd"], o_lat, w["w_uv_bd"], w["w_out_a"],
      w["w_out_b"], x)


FFN_TF = 256


def _ffn_kernel(x_ref, g_ref, wg_ref, wu_ref, wd_ref, o_ref, xn_sc, acc_sc):
    f = pl.program_id(1)

    @pl.when(f == 0)
    def _():
        xn_sc[...] = _rms(x_ref[...], g_ref[...]).astype(BF16)
        acc_sc[...] = jnp.zeros(acc_sc.shape, F32)

    xn = xn_sc[...]
    gate = _dot(xn, wg_ref[...])
    up = _dot(xn, wu_ref[...])
    acc_sc[...] += _dot(gate * _sigmoid(gate) * up, wd_ref[...])

    @pl.when(f == pl.num_programs(1) - 1)
    def _():
        o_ref[...] = x_ref[...] + acc_sc[...]


def _ffn(x, g, w_gu, w_down, tm):
    n = x.shape[0]
    nf = D_FF // FFN_TF
    return pl.pallas_call(
        _ffn_kernel,
        grid=(n // tm, nf),
        in_specs=[pl.BlockSpec((tm, D_MODEL), lambda i, f: (i, 0)),
                  pl.BlockSpec((1, D_MODEL), lambda i, f: (0, 0)),
                  pl.BlockSpec((D_MODEL, FFN_TF), lambda i, f: (0, f)),
                  pl.BlockSpec((D_MODEL, FFN_TF), lambda i, f: (0, nf + f)),
                  pl.BlockSpec((FFN_TF, D_MODEL), lambda i, f: (f, 0))],
        out_specs=pl.BlockSpec((tm, D_MODEL), lambda i, f: (i, 0)),
        out_shape=jax.ShapeDtypeStruct((n, D_MODEL), F32),
        scratch_shapes=[pltpu.VMEM((tm, D_MODEL), BF16), pltpu.VMEM((tm, D_MODEL), F32)],
        compiler_params=_cparams(("parallel", "arbitrary")),
    )(x, g, w_gu, w_gu, w_down)


def _odd_in_kernel(x_ref, g_ref, wq_ref, wk_ref, wv_ref, wg_ref, wxa_ref, a2_ref, ab_ref,
                   q_ref, k_ref, v_ref, gate_ref, la_ref):
    xn = _rms(x_ref[...], g_ref[...]).astype(BF16)
    q_ref[...] = _dot(xn, wq_ref[...]) * (GLA_DK ** -0.5)
    k_ref[...] = _dot(xn, wk_ref[...])
    v_ref[...] = _dot(xn, wv_ref[...])
    gate_ref[...] = _dot(xn, wg_ref[...])
    z = _dot(_dot(xn, wxa_ref[...]), a2_ref[...]) + ab_ref[...]
    la_ref[...] = -_softplus(-z) * (1.0 / GLA_GATE_NORM)


def _odd_in(x, w, tm):
    n = x.shape[0]
    return pl.pallas_call(
        _odd_in_kernel,
        grid=(n // tm,),
        in_specs=[_row_spec(tm, D_MODEL), _const_spec((1, D_MODEL)),
                  _const_spec((D_MODEL, GLA_KDIM)), _const_spec((D_MODEL, GLA_KDIM)),
                  _const_spec((D_MODEL, GLA_VDIM)), _const_spec((D_MODEL, GLA_VDIM)),
                  _const_spec((D_MODEL, LANES)), _const_spec((LANES, GLA_KDIM)),
                  _const_spec((1, GLA_KDIM))],
        out_specs=[_row_spec(tm, GLA_KDIM), _row_spec(tm, GLA_KDIM), _row_spec(tm, GLA_VDIM),
                   _row_spec(tm, GLA_VDIM), _row_spec(tm, GLA_KDIM)],
        out_shape=[jax.ShapeDtypeStruct((n, GLA_KDIM), F32), jax.ShapeDtypeStruct((n, GLA_KDIM), F32),
                   jax.ShapeDtypeStruct((n, GLA_VDIM), F32), jax.ShapeDtypeStruct((n, GLA_VDIM), F32),
                   jax.ShapeDtypeStruct((n, GLA_KDIM), F32)],
        compiler_params=_cparams(("parallel",)),
    )(x, w["norm_mix"], w["w_q"], w["w_k"], w["w_v"], w["w_g"], w["w_xa"], w["a2p"], w["ab"])


def _gla_kernel(q_ref, k_ref, v_ref, la_ref, s0_ref, o_ref, st_ref):
    c = GLA_CHUNK

    @pl.when(pl.program_id(1) == 0)
    def _():
        st_ref[...] = s0_ref[...]

    row = lax.broadcasted_iota(jnp.int32, (c, c), 0)
    col = lax.broadcasted_iota(jnp.int32, (c, c), 1)
    tri = row >= col
    tri_b = jnp.where(tri, 1.0, 0.0).astype(BF16)
    for h in range(GLA_HEADS):
        ks = slice(h * GLA_DK, (h + 1) * GLA_DK)
        vs = slice(h * GLA_DV, (h + 1) * GLA_DV)
        b = _dot_exact_lhs(tri_b, la_ref[:, ks])
        q = q_ref[:, ks]
        k = k_ref[:, ks]
        v = v_ref[:, vs]
        b_end = b[c - 1:c, :]
        qe = (q * jnp.exp(b)).astype(BF16)
        ke = (k * jnp.exp(-b)).astype(BF16)
        a_mat = jnp.where(tri, _dot_nt(qe, ke), 0.0)
        st = st_ref[0, h]
        o_ref[:, vs] = _dot_nt(qe, st) + _dot(a_mat, v)
        k_end = k * jnp.exp(b_end - b)
        st_ref[0, h] = st * jnp.exp(b_end) + _dot(v.T, k_end)


def _gla(q, k, v, la, s0t, batch, seq):
    nc = seq // GLA_CHUNK
    rspec = lambda width: pl.BlockSpec((GLA_CHUNK, width), lambda b, c: (b * nc + c, 0))
    sspec = pl.BlockSpec((1, GLA_HEADS, GLA_DV, GLA_DK), lambda b, c: (b, 0, 0, 0))
    return pl.pallas_call(
        _gla_kernel,
        grid=(batch, nc),
        in_specs=[rspec(GLA_KDIM), rspec(GLA_KDIM), rspec(GLA_VDIM), rspec(GLA_KDIM), sspec],
        out_specs=[rspec(GLA_VDIM), sspec],
        out_shape=[jax.ShapeDtypeStruct(v.shape, F32), jax.ShapeDtypeStruct(s0t.shape, F32)],
        compiler_params=_cparams(("parallel", "arbitrary")),
    )(q, k, v, la, s0t)


def _odd_out_kernel(o_ref, gate_ref, gn_ref, wo_ref, x_ref, y_ref):
    parts = []
    for h in range(GLA_HEADS):
        vs = slice(h * GLA_DV, (h + 1) * GLA_DV)
        parts.append(_rms(o_ref[:, vs], gn_ref[:, vs]))
    gate = gate_ref[...]
    on = jnp.concatenate(parts, axis=1) * (gate * _sigmoid(gate))
    y_ref[...] = x_ref[...] + _dot(on, wo_ref[...])


def _odd_out(o, gate, x, w, tm):
    n = x.shape[0]
    return pl.pallas_call(
        _odd_out_kernel,
        grid=(n // tm,),
        in_specs=[_row_spec(tm, GLA_VDIM), _row_spec(tm, GLA_VDIM), _const_spec((1, GLA_VDIM)),
                  _const_spec((GLA_VDIM, D_MODEL)), _row_spec(tm, D_MODEL)],
        out_specs=_row_spec(tm, D_MODEL),
        out_shape=jax.ShapeDtypeStruct((n, D_MODEL), F32),
        compiler_params=_cparams(("parallel",)),
    )(o, gate, w["gla_norm"], w["w_out"], x)


def _router_kernel(x_ref, g_ref, wr_ref, xn_ref, comb_ref):
    xn = _rms(x_ref[...], g_ref[...])
    xn_ref[...] = xn.astype(BF16)
    logits = _dot_f32ish(xn, wr_ref[...])
    lane = lax.broadcasted_iota(jnp.int32, logits.shape, 1)
    logits = jnp.where(lane < N_EXPERTS, logits, NEG_BIG)
    m1 = jnp.max(logits, axis=-1, keepdims=True)
    i1 = jnp.min(jnp.where(logits == m1, lane, LANES), axis=-1, keepdims=True)
    rest = jnp.where(lane == i1, NEG_BIG, logits)
    m2 = jnp.max(rest, axis=-1, keepdims=True)
    i2 = jnp.min(jnp.where(rest == m2, lane, LANES), axis=-1, keepdims=True)
    e2 = jnp.exp(m2 - m1)
    g1 = 1.0 / (1.0 + e2)
    g2 = e2 / (1.0 + e2)
    comb_ref[...] = jnp.where(lane == i1, g1, 0.0) + jnp.where(lane == i2, g2, 0.0)


def _router(x, g, wr, tm):
    n = x.shape[0]
    return pl.pallas_call(
        _router_kernel,
        grid=(n // tm,),
        in_specs=[_row_spec(tm, D_MODEL), _const_spec((1, D_MODEL)), _const_spec((D_MODEL, LANES))],
        out_specs=[_row_spec(tm, D_MODEL), _row_spec(tm, LANES)],
        out_shape=[jax.ShapeDtypeStruct((n, D_MODEL), BF16), jax.ShapeDtypeStruct((n, LANES), F32)],
        compiler_params=_cparams(("parallel",)),
    )(x, g, wr)


MOE_TF = 512


def _moe_kernel(xn_ref, comb_ref, wg_ref, wu_ref, wd_ref, x_ref, fn_ref, o_ref, acc_sc):
    e = pl.program_id(1)
    f = pl.program_id(2)

    @pl.when((e == 0) & (f == 0))
    def _():
        acc_sc[...] = jnp.zeros(acc_sc.shape, F32)

    xn = xn_ref[...]
    comb = comb_ref[...]
    lane = lax.broadcasted_iota(jnp.int32, comb.shape, 1)
    ce = jnp.sum(jnp.where(lane == e, comb, 0.0), axis=-1, keepdims=True)
    gate = _dot(xn, wg_ref[...])
    up = _dot(xn, wu_ref[...])
    acc_sc[...] += _dot(gate * _sigmoid(gate) * up * ce, wd_ref[...])

    @pl.when((e == pl.num_programs(1) - 1) & (f == pl.num_programs(2) - 1))
    def _():
        o_ref[...] = _rms(x_ref[...] + acc_sc[...], fn_ref[...])


def _moe(xn, comb, w_gu, w_down, x, fn, tm):
    n = x.shape[0]
    nf = D_FF_EXPERT // MOE_TF
    return pl.pallas_call(
        _moe_kernel,
        grid=(n // tm, N_EXPERTS, nf),
        in_specs=[pl.BlockSpec((tm, D_MODEL), lambda i, e, f: (i, 0)),
                  pl.BlockSpec((tm, LANES), lambda i, e, f: (i, 0)),
                  pl.BlockSpec((None, D_MODEL, MOE_TF), lambda i, e, f: (e, 0, f)),
                  pl.BlockSpec((None, D_MODEL, MOE_TF), lambda i, e, f: (e, 0, nf + f)),
                  pl.BlockSpec((None, MOE_TF, D_MODEL), lambda i, e, f: (e, f, 0)),
                  pl.BlockSpec((tm, D_MODEL), lambda i, e, f: (i, 0)),
                  pl.BlockSpec((1, D_MODEL), lambda i, e, f: (0, 0))],
        out_specs=pl.BlockSpec((tm, D_MODEL), lambda i, e, f: (i, 0)),
        out_shape=jax.ShapeDtypeStruct((n, D_MODEL), F32),
        scratch_shapes=[pltpu.VMEM((tm, D_MODEL), F32)],
        compiler_params=_cparams(("parallel", "arbitrary", "arbitrary")),
    )(xn, comb, w_gu, w_gu, w_down, x, fn)


def _scan_vec_layout(x, batch, seq):
    bh = batch * RWKV_HEADS
    x4 = x.reshape(batch, seq, RWKV_HEADS, RWKV_HEAD).transpose(1, 3, 0, 2).reshape(seq, RWKV_HEAD, bh)
    if bh < LANES:
        return jnp.concatenate([x4, x4], axis=-1)[None]
    return x4.reshape(seq, RWKV_HEAD, bh // LANES, LANES).transpose(2, 0, 1, 3)


def _scan_val_layout(v, batch, seq):
    bh = batch * RWKV_HEADS
    v4 = v.reshape(batch, seq, RWKV_HEADS, RWKV_HEAD).transpose(1, 3, 0, 2).reshape(seq, RWKV_HEAD, bh)
    nvb = RWKV_HEAD // SCAN_VS
    if bh < LANES:
        return v4.reshape(seq, nvb, SCAN_VS, bh).transpose(0, 2, 1, 3).reshape(seq, SCAN_VS, LANES)[None]
    nbl = bh // LANES
    return (v4.reshape(seq, nvb, SCAN_VS, nbl, LANES).transpose(3, 1, 0, 2, 4)
            .reshape(nbl * nvb, seq, SCAN_VS, LANES))


def _scan_val_unlayout(y, batch, seq):
    bh = batch * RWKV_HEADS
    nvb = RWKV_HEAD // SCAN_VS
    if bh < LANES:
        v4 = y[0].reshape(seq, SCAN_VS, nvb, bh).transpose(0, 2, 1, 3).reshape(seq, RWKV_HEAD, bh)
    else:
        nbl = bh // LANES
        v4 = (y.reshape(nbl, nvb, seq, SCAN_VS, LANES).transpose(2, 1, 3, 0, 4)
              .reshape(seq, RWKV_HEAD, bh))
    return (v4.reshape(seq, RWKV_HEAD, batch, RWKV_HEADS).transpose(2, 0, 3, 1)
            .reshape(batch * seq, RWKV_DIM))


def _scan_state_layout(s, batch):
    bh = batch * RWKV_HEADS
    nvb = RWKV_HEAD // SCAN_VS
    s4 = s.transpose(3, 2, 0, 1).reshape(RWKV_HEAD, RWKV_HEAD, bh)
    if bh < LANES:
        return (s4.reshape(RWKV_HEAD, nvb, SCAN_VS, bh).transpose(0, 2, 1, 3)
                .reshape(RWKV_HEAD, SCAN_VS, LANES)[None])
    nbl = bh // LANES
    return (s4.reshape(RWKV_HEAD, nvb, SCAN_VS, nbl, LANES).transpose(3, 1, 0, 2, 4)
            .reshape(nbl * nvb, RWKV_HEAD, SCAN_VS, LANES))


def _scan_state_unlayout(arr, batch):
    bh = batch * RWKV_HEADS
    nvb = RWKV_HEAD // SCAN_VS
    if bh < LANES:
        s = arr[0].reshape(RWKV_HEAD, SCAN_VS, nvb, bh).transpose(3, 2, 1, 0)
    else:
        nbl = bh // LANES
        s = arr.reshape(nbl, nvb, RWKV_HEAD, SCAN_VS, LANES).transpose(0, 4, 1, 3, 2)
    return s.reshape(batch, RWKV_HEADS, RWKV_HEAD, RWKV_HEAD)


def _swap_halves(w):
    half = w.shape[-1] // 2
    return jnp.concatenate([w[..., half:], w[..., :half]], axis=-1)


def _prep_even(i, norm_mix, norm_ffn, w_in, q_norm, kv_norm, w_uq, w_uk, w_uv, mu, w0, w2, a0, a2,
               g2, k_k, k_a, r_k, ln_g, ln_b, w_out, ffn_gu, ffn_down):
    w = {}
    row = lambda v: v[i].reshape(1, -1)
    w_in = w_in[i]
    w["norm_mix"] = row(norm_mix)
    w["norm_ffn"] = row(norm_ffn)
    w["w_q"] = w_in[:, :MLA_Q_RANK].astype(BF16)
    w_kv = w_in[:, MLA_Q_RANK:MLA_Q_RANK + MLA_LAT]
    w["w_ckv"] = w_kv[:, :MLA_KV_RANK].astype(BF16)
    w["w_pe_a"] = w_kv[:, MLA_KV_RANK:].astype(BF16)
    w["w_pe_b"] = _swap_halves(w_kv[:, MLA_KV_RANK:]).astype(BF16)
    w["w_rw"] = w_in[:, MLA_Q_RANK + MLA_LAT:].astype(BF16)
    w["q_norm"] = row(q_norm)
    w["kv_norm"] = row(kv_norm)
    uq = w_uq[i].reshape(MLA_Q_RANK, MLA_HEADS, MLA_NOPE + MLA_ROPE)
    uq_pe = uq[:, :, MLA_NOPE:]
    w["w_qpe_a"] = uq_pe.reshape(MLA_Q_RANK, -1).astype(BF16)
    w["w_qpe_b"] = _swap_halves(uq_pe).reshape(MLA_Q_RANK, -1).astype(BF16)
    w["w_qlat"] = _fold_qlat(uq[:, :, :MLA_NOPE].transpose(1, 0, 2), w_uk[i].transpose(1, 0, 2))
    uv = w_uv[i].transpose(1, 0, 2).reshape(MLA_HEADS // 2, 2, MLA_KV_RANK, MLA_V)
    zero = jnp.zeros_like(uv[:, 0])
    w["w_uv_bd"] = jnp.concatenate(
        [jnp.concatenate([uv[:, 0], zero], axis=-1), jnp.concatenate([zero, uv[:, 1]], axis=-1)],
        axis=1).astype(BF16)
    w["mu"] = row(mu)
    w["w0"] = row(w0)
    pad = lambda m, before: jnp.pad(m, ((before, LANES - before - m.shape[0]), (0, 0))).astype(BF16)
    w["w2p"] = pad(w2[i], 0)
    w["a2p"] = pad(a2[i], RWKV_W_LORA)
    w["a0"] = row(a0)
    w["g2"] = g2[i].astype(BF16)
    w["k_k"] = row(k_k)
    w["k_a"] = row(k_a)
    w["r_k"] = row(r_k)
    w["ln_g"] = row(ln_g)
    w["ln_b"] = row(ln_b)
    head = jnp.arange(RWKV_DIM) // RWKV_HEAD
    w["ones_bd"] = (head[:, None] == head[None, :]).astype(BF16)
    w["w_out_a"] = w_out[i][:MLA_HEADS * MLA_V].astype(BF16)
    w["w_out_b"] = w_out[i][MLA_HEADS * MLA_V:].astype(BF16)
    w["ffn_gu"] = ffn_gu[i].astype(BF16)
    w["ffn_down"] = ffn_down[i].astype(BF16)
    return w


def _prep_odd(i, norm_mix, norm_ffn, w_in, a2, ab, gla_norm, w_out, router, moe_gu, moe_down):
    w = {}
    row = lambda v: v[i].reshape(1, -1)
    w_in = w_in[i]
    w["norm_mix"] = row(norm_mix)
    w["norm_ffn"] = row(norm_ffn)
    w["w_q"] = w_in[:, :GLA_KDIM].astype(BF16)
    w["w_k"] = w_in[:, GLA_KDIM:2 * GLA_KDIM].astype(BF16)
    w["w_v"] = w_in[:, 2 * GLA_KDIM:2 * GLA_KDIM + GLA_VDIM].astype(BF16)
    w["w_g"] = w_in[:, 2 * GLA_KDIM + GLA_VDIM:2 * GLA_KDIM + 2 * GLA_VDIM].astype(BF16)
    w["w_xa"] = jnp.pad(w_in[:, 2 * GLA_KDIM + 2 * GLA_VDIM:],
                        ((0, 0), (0, LANES - GLA_GATE_RANK))).astype(BF16)
    w["a2p"] = jnp.pad(a2[i], ((0, LANES - GLA_GATE_RANK), (0, 0))).astype(BF16)
    w["ab"] = row(ab)
    w["gla_norm"] = row(gla_norm)
    w["w_out"] = w_out[i].astype(BF16)
    w["router"] = jnp.pad(router[i], ((0, 0), (0, LANES - N_EXPERTS)))
    w["moe_gu"] = moe_gu[i].astype(BF16)
    w["moe_down"] = moe_down[i].astype(BF16)
    return w


def _rope_tables(pos, reps):
    inv = ROPE_THETA ** (-jnp.arange(0, MLA_ROPE, 2, dtype=F32) / MLA_ROPE)
    ang = pos.astype(F32)[:, None] * inv[None, :]
    cos, sin = jnp.cos(ang), jnp.sin(ang)
    cs = jnp.tile(jnp.concatenate([cos, cos], axis=-1), (reps, 1))
    sn = jnp.tile(jnp.concatenate([-sin, sin], axis=-1), (reps, 1))
    return {"cs": cs, "sn": sn, "cs8": jnp.tile(cs, (1, MLA_HEADS)), "sn8": jnp.tile(sn, (1, MLA_HEADS))}


def _even_layer(x, batch, seq, tabs, state, shift0, past, w, tm, tc):
    n = batch * seq
    lat, lat_b, q_lat, q_pe, rw = _even_in(x, w, tabs, tm)
    if past is None:
        o_lat = _mla_prompt(q_lat.reshape(n * MLA_HEADS, MLA_KV_RANK),
                            q_pe.reshape(n * MLA_HEADS, MLA_ROPE), lat_b, batch, seq)
    else:
        cache, page_table = past
        rows = seq * MLA_HEADS
        q_full = jnp.concatenate([q_lat.reshape(batch, rows, MLA_KV_RANK),
                                  q_pe.reshape(batch, rows, MLA_ROPE)], axis=-1)
        new_pad = jnp.pad(lat_b.reshape(batch, seq, MLA_LAT), ((0, 0), (0, PAGE_SIZE - seq), (0, 0)))
        o_lat = _mla_decode(page_table, q_full, new_pad, cache)
    o_lat = o_lat.reshape(n, MLA_HEADS * MLA_KV_RANK)

    rw3 = rw.reshape(batch, seq, RWKV_PROJ)
    prev = jnp.concatenate([shift0[:, None, :], rw3[:, :-1]], axis=1).reshape(n, RWKV_PROJ)
    r, dec, k2, v, na, kb, g, rkv = _rwkv_prep(rw, prev, w, tm)
    xs = [_scan_vec_layout(t, batch, seq) for t in (na, dec, kb, k2, r)]
    y_l, s_l = _rwkv_scan(xs, _scan_val_layout(v, batch, seq), _scan_state_layout(state, batch), tc)
    y = _scan_val_unlayout(y_l, batch, seq)
    new_state = _scan_state_unlayout(s_l, batch)

    x = _even_out(y, rkv, g, o_lat, x, w, tm)
    x = _ffn(x, w["norm_ffn"], w["ffn_gu"], w["ffn_down"], tm)
    return x, lat.reshape(batch, seq, MLA_LAT), new_state, rw3[:, -1]


def _odd_layer(x, batch, seq, state, w, final_norm, tm):
    q, k, v, gate, la = _odd_in(x, w, tm)
    seq_p = -(-seq // GLA_CHUNK) * GLA_CHUNK
    if seq_p != seq:
        padr = lambda t: jnp.pad(t.reshape(batch, seq, -1), ((0, 0), (0, seq_p - seq), (0, 0))
                                 ).reshape(batch * seq_p, -1)
        qp, kp, vp, lap = padr(q), padr(k), padr(v), padr(la)
    else:
        qp, kp, vp, lap = q, k, v, la
    o, st = _gla(qp, kp, vp, lap, state.transpose(0, 1, 3, 2), batch, seq_p)
    if seq_p != seq:
        o = o.reshape(batch, seq_p, GLA_VDIM)[:, :seq].reshape(batch * seq, GLA_VDIM)
    x = _odd_out(o, gate, x, w, tm)
    xn, comb = _router(x, w["norm_ffn"], w["router"], tm)
    y = _moe(xn, comb, w["moe_gu"], w["moe_down"], x, final_norm, tm)
    return y, st.transpose(0, 1, 3, 2)


def kernel(x_prompt, x_sample, cache_mla, state_rwkv, state_rwkv_shift, state_gla, page_table, norm_mix_even, norm_ffn_even, w_in_even, mla_q_norm, mla_kv_norm, mla_w_uq, mla_w_uk, mla_w_uv, rwkv_mu, rwkv_w0, rwkv_w2, rwkv_a0, rwkv_a2, rwkv_g2, rwkv_k_k, rwkv_k_a, rwkv_r_k, rwkv_ln_g, rwkv_ln_b, w_out_even, ffn_w_gu_even, ffn_w_down_even, norm_mix_odd, norm_ffn_odd, w_in_odd, gla_a2, gla_ab, gla_norm, w_out_odd, moe_router, moe_w_gu, moe_w_down, final_norm):
    bp, tp, _ = x_prompt.shape
    bs, ts, _ = x_sample.shape
    past_len = page_table.shape[1] * PAGE_SIZE
    tm_p, tm_s = 512, bs * ts
    we = _prep_even(0, norm_mix_even, norm_ffn_even, w_in_even, mla_q_norm, mla_kv_norm, mla_w_uq,
                    mla_w_uk, mla_w_uv, rwkv_mu, rwkv_w0, rwkv_w2, rwkv_a0, rwkv_a2, rwkv_g2,
                    rwkv_k_k, rwkv_k_a, rwkv_r_k, rwkv_ln_g, rwkv_ln_b, w_out_even, ffn_w_gu_even,
                    ffn_w_down_even)
    wo = _prep_odd(0, norm_mix_odd, norm_ffn_odd, w_in_odd, gla_a2, gla_ab, gla_norm, w_out_odd,
                   moe_router, moe_w_gu, moe_w_down)
    fn = final_norm.reshape(1, -1)
    tabs_p = _rope_tables(jnp.arange(tp), 1)
    tabs_s = _rope_tables(past_len + jnp.arange(ts), bs)

    hp = x_prompt.reshape(bp * tp, D_MODEL)
    hs = x_sample.reshape(bs * ts, D_MODEL)
    zeros_state = jnp.zeros((bp, RWKV_HEADS, RWKV_HEAD, RWKV_HEAD), F32)
    zeros_shift = jnp.zeros((bp, RWKV_PROJ), F32)
    hp, lat_p, rs_p, sh_p = _even_layer(hp, bp, tp, tabs_p, zeros_state, zeros_shift, None, we,
                                        tm_p, 32)
    hs, lat_s, rs_s, sh_s = _even_layer(hs, bs, ts, tabs_s, state_rwkv[0], state_rwkv_shift[0],
                                        (cache_mla[0], page_table), we, tm_s, ts)
    zeros_gla = jnp.zeros((bp, GLA_HEADS, GLA_DK, GLA_DV), F32)
    yp, gs_p = _odd_layer(hp, bp, tp, zeros_gla, wo, fn, tm_p)
    ys, gs_s = _odd_layer(hs, bs, ts, state_gla[0], wo, fn, tm_s)
    return (yp.reshape(bp, tp, D_MODEL), ys.reshape(bs, ts, D_MODEL), lat_p[None], lat_s[None],
            rs_p[None], rs_s[None], sh_p[None], sh_s[None], gs_p[None], gs_s[None])
```

```python
import functools

import jax
import jax.numpy as jnp
from jax import lax
from jax.experimental import pallas as pl
from jax.experimental.pallas import tpu as pltpu

F32 = jnp.float32
BF16 = jnp.bfloat16

D_MODEL = 1024
PAGE_SIZE = 128
NORM_EPS = 1e-6

MLA_HEADS = 8
MLA_NOPE = 64
MLA_ROPE = 32
MLA_V = 64
MLA_Q_RANK = 384
MLA_KV_RANK = 256
MLA_LAT = MLA_KV_RANK + MLA_ROPE
MLA_SCALE = (MLA_NOPE + MLA_ROPE) ** -0.5
ROPE_THETA = 10000.0

RWKV_HEADS = 8
RWKV_HEAD = 64
RWKV_DIM = RWKV_HEADS * RWKV_HEAD
RWKV_W_LORA = 64
RWKV_A_LORA = 64
RWKV_G_LORA = 128
RWKV_PROJ = 3 * RWKV_DIM + RWKV_W_LORA + RWKV_A_LORA + RWKV_G_LORA
RWKV_LN_EPS = 64e-5

GLA_HEADS = 4
GLA_DK = 128
GLA_DV = 256
GLA_KDIM = GLA_HEADS * GLA_DK
GLA_VDIM = GLA_HEADS * GLA_DV
GLA_GATE_RANK = 16
GLA_GATE_NORM = 16.0
GLA_CHUNK = 128

D_FF = 2816
N_EXPERTS = 8
D_FF_EXPERT = 3584

LANES = 128
VMEM_LIMIT = 56 * 1024 * 1024
NEG_BIG = -1e30
LOG2_E = 1.4426950408889634
Q_PRESCALE = MLA_SCALE * LOG2_E


def _cparams(sem):
    return pltpu.CompilerParams(dimension_semantics=sem, vmem_limit_bytes=VMEM_LIMIT)


def _const_spec(shape):
    nd = len(shape)
    return pl.BlockSpec(shape, lambda *_: (0,) * nd)


def _row_spec(tm, width):
    return pl.BlockSpec((tm, width), lambda i: (i, 0))


def _dot(a, b):
    return jnp.dot(a.astype(BF16), b.astype(BF16), preferred_element_type=F32)


def _dot_nt(a, b):
    return lax.dot_general(a.astype(BF16), b.astype(BF16), (((1,), (1,)), ((), ())),
                           preferred_element_type=F32)


def _split2(x):
    hi = x.astype(BF16)
    lo = (x - hi.astype(F32)).astype(BF16)
    return hi, lo


def _split3(x):
    hi = x.astype(BF16)
    r1 = x - hi.astype(F32)
    mid = r1.astype(BF16)
    lo = (r1 - mid.astype(F32)).astype(BF16)
    return hi, mid, lo


def _dot_exact_rhs(x, e):
    hi, mid, lo = _split3(x)
    return (jnp.dot(hi, e, preferred_element_type=F32) + jnp.dot(mid, e, preferred_element_type=F32)
            + jnp.dot(lo, e, preferred_element_type=F32))


def _dot_exact_lhs(e, x):
    hi, mid, lo = _split3(x)
    return (jnp.dot(e, hi, preferred_element_type=F32) + jnp.dot(e, mid, preferred_element_type=F32)
            + jnp.dot(e, lo, preferred_element_type=F32))


def _dot_f32ish(a, b):
    ah, al = _split2(a)
    bh, bl = _split2(b)
    return (jnp.dot(ah, bh, preferred_element_type=F32) + jnp.dot(ah, bl, preferred_element_type=F32)
            + jnp.dot(al, bh, preferred_element_type=F32))


def _lane_tile(x, width):
    return x if width == LANES else pltpu.repeat(x, width // LANES, axis=1)


def _rms(x, g, eps=NORM_EPS):
    return x * lax.rsqrt(jnp.mean(x * x, axis=-1, keepdims=True) + eps) * g


def _sigmoid(x):
    return 1.0 / (1.0 + jnp.exp(-x))


def _softplus(x):
    return jnp.maximum(x, 0.0) + jnp.log(1.0 + jnp.exp(-jnp.abs(x)))


def _fold_qlat_kernel(uq_ref, uk_ref, o_ref):
    a = uq_ref[...]
    b = uk_ref[...]
    ah, al = _split2(a)
    bh, bl = _split2(b)
    dn = (((1,), (1,)), ((), ()))
    o = (lax.dot_general(ah, bh, dn, preferred_element_type=F32)
         + lax.dot_general(ah, bl, dn, preferred_element_type=F32)
         + lax.dot_general(al, bh, dn, preferred_element_type=F32))
    o_ref[...] = o.astype(BF16)


def _fold_qlat(uq_nope, uk):
    return pl.pallas_call(
        _fold_qlat_kernel,
        grid=(MLA_HEADS,),
        in_specs=[pl.BlockSpec((None, MLA_Q_RANK, MLA_NOPE), lambda h: (h, 0, 0)),
                  pl.BlockSpec((None, MLA_KV_RANK, MLA_NOPE), lambda h: (h, 0, 0))],
        out_specs=pl.BlockSpec((MLA_Q_RANK, MLA_KV_RANK), lambda h: (0, h)),
        out_shape=jax.ShapeDtypeStruct((MLA_Q_RANK, MLA_HEADS * MLA_KV_RANK), BF16),
        compiler_params=_cparams(("arbitrary",)),
    )(uq_nope, uk)


def _even_in_kernel(x_ref, g_ref, wq_ref, wckv_ref, wpa_ref, wpb_ref, wrw_ref, qn_ref, kvn_ref,
                    cs_ref, sn_ref, wql_ref, wqa_ref, wqb_ref, cs8_ref, sn8_ref,
                    lat_ref, latb_ref, ql_ref, qpe_ref, rw_ref):
    xn = _rms(x_ref[...], g_ref[...]).astype(BF16)
    cq = _rms(_dot(xn, wq_ref[...]), qn_ref[...]).astype(BF16)
    ql_ref[...] = (_dot(cq, wql_ref[...]) * Q_PRESCALE).astype(BF16)
    qpe = _dot(cq, wqa_ref[...]) * cs8_ref[...] + _dot(cq, wqb_ref[...]) * sn8_ref[...]
    qpe_ref[...] = (qpe * Q_PRESCALE).astype(BF16)
    ckv = _rms(_dot(xn, wckv_ref[...]), kvn_ref[...])
    kpe = _dot(xn, wpa_ref[...]) * cs_ref[...] + _dot(xn, wpb_ref[...]) * sn_ref[...]
    lat_ref[:, :MLA_KV_RANK] = ckv
    lat_ref[:, MLA_KV_RANK:] = kpe
    latb_ref[:, :MLA_KV_RANK] = ckv.astype(BF16)
    latb_ref[:, MLA_KV_RANK:] = kpe.astype(BF16)
    rw_ref[...] = _dot(xn, wrw_ref[...])


def _even_in(x, w, tabs, tm):
    n = x.shape[0]
    nt = tabs["cs"].shape[0] // tm
    tab = lambda width: pl.BlockSpec((tm, width), lambda i: (i % nt, 0))
    hq = MLA_HEADS * MLA_KV_RANK
    hr = MLA_HEADS * MLA_ROPE
    return pl.pallas_call(
        _even_in_kernel,
        grid=(n // tm,),
        in_specs=[_row_spec(tm, D_MODEL), _const_spec((1, D_MODEL)),
                  _const_spec((D_MODEL, MLA_Q_RANK)), _const_spec((D_MODEL, MLA_KV_RANK)),
                  _const_spec((D_MODEL, MLA_ROPE)), _const_spec((D_MODEL, MLA_ROPE)),
                  _const_spec((D_MODEL, RWKV_PROJ)), _const_spec((1, MLA_Q_RANK)),
                  _const_spec((1, MLA_KV_RANK)), tab(MLA_ROPE), tab(MLA_ROPE),
                  _const_spec((MLA_Q_RANK, hq)), _const_spec((MLA_Q_RANK, hr)),
                  _const_spec((MLA_Q_RANK, hr)), tab(hr), tab(hr)],
        out_specs=[_row_spec(tm, MLA_LAT), _row_spec(tm, MLA_LAT), _row_spec(tm, hq),
                   _row_spec(tm, hr), _row_spec(tm, RWKV_PROJ)],
        out_shape=[jax.ShapeDtypeStruct((n, MLA_LAT), F32), jax.ShapeDtypeStruct((n, MLA_LAT), BF16),
                   jax.ShapeDtypeStruct((n, hq), BF16), jax.ShapeDtypeStruct((n, hr), BF16),
                   jax.ShapeDtypeStruct((n, RWKV_PROJ), F32)],
        compiler_params=_cparams(("parallel",)),
    )(x, w["norm_mix"], w["w_q"], w["w_ckv"], w["w_pe_a"], w["w_pe_b"], w["w_rw"], w["q_norm"],
      w["kv_norm"], tabs["cs"], tabs["sn"], w["w_qlat"], w["w_qpe_a"], w["w_qpe_b"],
      tabs["cs8"], tabs["sn8"])


ATT_TQ = 256
ATT_SUB = 256


def _mla_prompt_kernel(q_ref, qpe_ref, lat_ref, o_ref, m_sc, l_sc, acc_sc):
    i = pl.program_id(1)
    j = pl.program_id(2)
    rows = ATT_TQ * MLA_HEADS

    @pl.when(j == 0)
    def _():
        m_sc[...] = jnp.full(m_sc.shape, NEG_BIG, F32)
        l_sc[...] = jnp.zeros(l_sc.shape, F32)
        acc_sc[...] = jnp.zeros(acc_sc.shape, F32)

    def tile(masked):
        ckv = lat_ref[:, :MLA_KV_RANK]
        kpe = lat_ref[:, MLA_KV_RANK:]

        def sub(r, carry):
            rs = pl.ds(pl.multiple_of(r * ATT_SUB, ATT_SUB), ATT_SUB)
            s = _dot_nt(q_ref[rs, :], ckv) + _dot_nt(qpe_ref[rs, :], kpe)
            if masked:
                tok = (r * ATT_SUB + lax.broadcasted_iota(jnp.int32, s.shape, 0)) >> 3
                key = lax.broadcasted_iota(jnp.int32, s.shape, 1)
                s = jnp.where(key <= tok, s, NEG_BIG)
            m_prev = m_sc[rs, :]
            m_new = jnp.maximum(m_prev, jnp.max(s, axis=-1, keepdims=True))
            alpha = jnp.exp2(m_prev - m_new)
            p = jnp.exp2(s - _lane_tile(m_new, ATT_TQ))
            l_sc[rs, :] = alpha * l_sc[rs, :] + jnp.sum(p, axis=-1, keepdims=True)
            acc_sc[rs, :] = _lane_tile(alpha, MLA_KV_RANK) * acc_sc[rs, :] + _dot(p, ckv)
            m_sc[rs, :] = m_new
            return carry

        lax.fori_loop(0, rows // ATT_SUB, sub, 0, unroll=2)

    @pl.when(j < i)
    def _():
        tile(False)

    @pl.when(j == i)
    def _():
        tile(True)
        o_ref[...] = (acc_sc[...] / _lane_tile(l_sc[...], MLA_KV_RANK)).astype(BF16)


def _mla_prompt(q_lat, q_pe, lat_b, batch, seq):
    nq = seq // ATT_TQ
    rows = ATT_TQ * MLA_HEADS
    return pl.pallas_call(
        _mla_prompt_kernel,
        grid=(batch, nq, nq),
        in_specs=[pl.BlockSpec((rows, MLA_KV_RANK), lambda b, i, j: (b * nq + i, 0)),
                  pl.BlockSpec((rows, MLA_ROPE), lambda b, i, j: (b * nq + i, 0)),
                  pl.BlockSpec((ATT_TQ, MLA_LAT), lambda b, i, j: (b * nq + jnp.minimum(i, j), 0))],
        out_specs=pl.BlockSpec((rows, MLA_KV_RANK), lambda b, i, j: (b * nq + i, 0)),
        out_shape=jax.ShapeDtypeStruct(q_lat.shape, BF16),
        scratch_shapes=[pltpu.VMEM((rows, LANES), F32), pltpu.VMEM((rows, LANES), F32),
                        pltpu.VMEM((rows, MLA_KV_RANK), F32)],
        compiler_params=_cparams(("parallel", "parallel", "arbitrary")),
    )(q_lat, q_pe, lat_b)


PAGES_PER_STEP = 8


def _mla_decode_kernel(pt_ref, q_ref, new_ref, *rest):
    page_refs = rest[:PAGES_PER_STEP]
    o_ref, m_sc, l_sc, acc_sc = rest[PAGES_PER_STEP:]
    j = pl.program_id(1)
    q = q_ref[0]

    @pl.when(j == 0)
    def _():
        m_sc[...] = jnp.full(m_sc.shape, NEG_BIG, F32)
        l_sc[...] = jnp.zeros(l_sc.shape, F32)
        acc_sc[...] = jnp.zeros(acc_sc.shape, F32)

    def update(s, values):
        m_prev = m_sc[...]
        m_new = jnp.maximum(m_prev, jnp.max(s, axis=-1, keepdims=True))
        alpha = jnp.exp2(m_prev - m_new)
        p = jnp.exp2(s - _lane_tile(m_new, s.shape[1]))
        l_sc[...] = alpha * l_sc[...] + jnp.sum(p, axis=-1, keepdims=True)
        pv = _dot(p[:, :PAGE_SIZE], values[0])
        for c in range(1, len(values)):
            pv = pv + _dot(p[:, c * PAGE_SIZE:(c + 1) * PAGE_SIZE], values[c])
        acc_sc[...] = _lane_tile(alpha, MLA_KV_RANK) * acc_sc[...] + pv
        m_sc[...] = m_new

    pages = [pr[...].astype(BF16) for pr in page_refs]
    s = jnp.concatenate([_dot_nt(q, pg) for pg in pages], axis=1)
    update(s, [pg[:, :MLA_KV_RANK] for pg in pages])

    @pl.when(j == pl.num_programs(1) - 1)
    def _():
        new = new_ref[0]
        sn = _dot_nt(q, new)
        tok = lax.broadcasted_iota(jnp.int32, sn.shape, 0) >> 3
        key = lax.broadcasted_iota(jnp.int32, sn.shape, 1)
        sn = jnp.where(key <= tok, sn, NEG_BIG)
        update(sn, [new[:, :MLA_KV_RANK]])
        o_ref[0] = (acc_sc[...] / _lane_tile(l_sc[...], MLA_KV_RANK)).astype(BF16)


def _mla_decode(page_table, q_full, new_pad, cache, layer):
    db, n_pages = page_table.shape
    rows = q_full.shape[1]
    steps = n_pages // PAGES_PER_STEP

    def page_spec(p):
        return pl.BlockSpec((None, None, PAGE_SIZE, MLA_LAT),
                            lambda b, j, pt: (layer, pt[b, j * PAGES_PER_STEP + p], 0, 0))

    grid_spec = pltpu.PrefetchScalarGridSpec(
        num_scalar_prefetch=1,
        grid=(db, steps),
        in_specs=[pl.BlockSpec((1, rows, MLA_LAT), lambda b, j, pt: (b, 0, 0)),
                  pl.BlockSpec((1, PAGE_SIZE, MLA_LAT), lambda b, j, pt: (b, 0, 0))]
        + [page_spec(p) for p in range(PAGES_PER_STEP)],
        out_specs=pl.BlockSpec((1, rows, MLA_KV_RANK), lambda b, j, pt: (b, 0, 0)),
        scratch_shapes=[pltpu.VMEM((rows, LANES), F32), pltpu.VMEM((rows, LANES), F32),
                        pltpu.VMEM((rows, MLA_KV_RANK), F32)],
    )
    return pl.pallas_call(
        _mla_decode_kernel,
        grid_spec=grid_spec,
        out_shape=jax.ShapeDtypeStruct((db, rows, MLA_KV_RANK), BF16),
        compiler_params=_cparams(("parallel", "arbitrary")),
    )(page_table, q_full, new_pad, *([cache] * PAGES_PER_STEP))


def _rwkv_prep_kernel(rw_ref, prev_ref, mu_ref, w0_ref, w2_ref, a0_ref, a2_ref, g2_ref, kk_ref,
                      ka_ref, rk_ref, ones_ref,
                      r_ref, w_ref, k_ref, v_ref, na_ref, kb_ref, g_ref, rkv_ref):
    rw = rw_ref[...]
    xs = rw + (prev_ref[...] - rw) * mu_ref[...]
    d = RWKV_DIM
    r = xs[:, :d]
    k = xs[:, d:2 * d]
    v = xs[:, 2 * d:3 * d]
    xwa = xs[:, 3 * d:3 * d + LANES]
    xg = xs[:, 3 * d + LANES:]
    ones = ones_ref[...]
    w_log = -_softplus(-(w0_ref[...] + _dot(jnp.tanh(xwa), w2_ref[...]))) - 0.5
    w_ref[...] = jnp.exp(-jnp.exp(w_log))
    a = _sigmoid(a0_ref[...] + _dot(xwa, a2_ref[...]))
    g_ref[...] = _dot(_sigmoid(xg), g2_ref[...])
    kk = k * kk_ref[...]
    ss = _dot_exact_rhs(kk * kk, ones)
    kk = kk / jnp.maximum(jnp.sqrt(ss), 1e-12)
    k2 = k * (1.0 + (a - 1.0) * ka_ref[...])
    r_ref[...] = r
    k_ref[...] = k2
    v_ref[...] = v
    na_ref[...] = -kk
    kb_ref[...] = kk * a
    rkv_ref[...] = _dot_exact_rhs(r * k2 * rk_ref[...], ones) * v


def _rwkv_prep(rw, prev, w, tm):
    n = rw.shape[0]
    d = RWKV_DIM
    vec = _const_spec((1, d))
    return pl.pallas_call(
        _rwkv_prep_kernel,
        grid=(n // tm,),
        in_specs=[_row_spec(tm, RWKV_PROJ), _row_spec(tm, RWKV_PROJ), _const_spec((1, RWKV_PROJ)),
                  vec, _const_spec((LANES, d)), vec, _const_spec((LANES, d)),
                  _const_spec((RWKV_G_LORA, d)), vec, vec, vec, _const_spec((d, d))],
        out_specs=[_row_spec(tm, d)] * 8,
        out_shape=[jax.ShapeDtypeStruct((n, d), F32)] * 8,
        compiler_params=_cparams(("parallel",)),
    )(rw, prev, w["mu"], w["w0"], w["w2p"], w["a0"], w["a2p"], w["g2"], w["k_k"], w["k_a"],
      w["r_k"], w["ones_bd"])


SCAN_VS = 32


def _rwkv_scan_kernel(a_ref, w_ref, b_ref, k_ref, r_ref, v_ref, s0_ref, y_ref, s_ref, *, tc):
    @pl.when(pl.program_id(1) == 0)
    def _():
        s_ref[...] = s0_ref[...]

    def step(t, carry):
        sa0 = jnp.zeros((SCAN_VS, LANES), F32)
        sa1 = jnp.zeros((SCAN_VS, LANES), F32)
        for k in range(0, RWKV_HEAD, 2):
            sa0 = sa0 + s_ref[0, k] * a_ref[0, t, k:k + 1, :]
            sa1 = sa1 + s_ref[0, k + 1] * a_ref[0, t, k + 1:k + 2, :]
        sa = sa0 + sa1
        v = v_ref[0, t]
        y0 = jnp.zeros((SCAN_VS, LANES), F32)
        y1 = jnp.zeros((SCAN_VS, LANES), F32)
        for k in range(RWKV_HEAD):
            sn = (s_ref[0, k] * w_ref[0, t, k:k + 1, :] + sa * b_ref[0, t, k:k + 1, :]
                  + v * k_ref[0, t, k:k + 1, :])
            s_ref[0, k] = sn
            if k % 2 == 0:
                y0 = y0 + sn * r_ref[0, t, k:k + 1, :]
            else:
                y1 = y1 + sn * r_ref[0, t, k:k + 1, :]
        y_ref[0, t] = y0 + y1
        return carry

    lax.fori_loop(0, tc, step, 0)


def _rwkv_scan(xs, v, s0, tc):
    nbl, t, _, _ = xs[0].shape
    nb = v.shape[0]
    nvb = nb // nbl
    xspec = pl.BlockSpec((1, tc, RWKV_HEAD, LANES), lambda n, c: (n // nvb, c, 0, 0))
    vspec = pl.BlockSpec((1, tc, SCAN_VS, LANES), lambda n, c: (n, c, 0, 0))
    sspec = pl.BlockSpec((1, RWKV_HEAD, SCAN_VS, LANES), lambda n, c: (n, 0, 0, 0))
    return pl.pallas_call(
        functools.partial(_rwkv_scan_kernel, tc=tc),
        grid=(nb, t // tc),
        in_specs=[xspec] * 5 + [vspec, sspec],
        out_specs=[vspec, sspec],
        out_shape=[jax.ShapeDtypeStruct(v.shape, F32), jax.ShapeDtypeStruct(s0.shape, F32)],
        compiler_params=_cparams(("parallel", "arbitrary")),
    )(*xs, v, s0)


def _even_out_kernel(y_ref, rkv_ref, g_ref, lng_ref, lnb_ref, ones_ref, ol_ref, wuv_ref, woa_ref,
                     wob_ref, x_ref, o_ref):
    ones = ones_ref[...]
    y = y_ref[...]
    inv = 1.0 / RWKV_HEAD
    mean = _dot_exact_rhs(y, ones) * inv
    dlt = y - mean
    var = _dot_exact_rhs(dlt * dlt, ones) * inv
    yn = dlt * lax.rsqrt(var + RWKV_LN_EPS) * lng_ref[...] + lnb_ref[...] + rkv_ref[...]
    ob = (yn * g_ref[...]).astype(BF16)
    pair = 2 * MLA_KV_RANK
    oa = jnp.concatenate(
        [_dot(ol_ref[:, p * pair:(p + 1) * pair], wuv_ref[p]) for p in range(MLA_HEADS // 2)], axis=1)
    o_ref[...] = x_ref[...] + _dot(oa, woa_ref[...]) + _dot(ob, wob_ref[...])


def _even_out(y, rkv, g, o_lat, x, w, tm):
    n = x.shape[0]
    d = RWKV_DIM
    hq = MLA_HEADS * MLA_KV_RANK
    return pl.pallas_call(
        _even_out_kernel,
        grid=(n // tm,),
        in_specs=[_row_spec(tm, d), _row_spec(tm, d), _row_spec(tm, d), _const_spec((1, d)),
                  _const_spec((1, d)), _const_spec((d, d)), _row_spec(tm, hq),
                  _const_spec((MLA_HEADS // 2, 2 * MLA_KV_RANK, 2 * MLA_V)),
                  _const_spec((MLA_HEADS * MLA_V, D_MODEL)), _const_spec((d, D_MODEL)),
                  _row_spec(tm, D_MODEL)],
        out_specs=_row_spec(tm, D_MODEL),
        out_shape=jax.ShapeDtypeStruct((n, D_MODEL), F32),
        compiler_params=_cparams(("parallel",)),
    )(y, rkv, g, w["ln_g"], w["ln_b"], w["ones_bd"], o_lat, w["w_uv_bd"], w["w_out_a"],
      w["w_out_b"], x)


FFN_TF = 256


def _ffn_kernel(x_ref, g_ref, wg_ref, wu_ref, wd_ref, o_ref, xn_sc, acc_sc):
    f = pl.program_id(1)

    @pl.when(f == 0)
    def _():
        xn_sc[...] = _rms(x_ref[...], g_ref[...]).astype(BF16)
        acc_sc[...] = jnp.zeros(acc_sc.shape, F32)

    xn = xn_sc[...]
    gate = _dot(xn, wg_ref[...])
    up = _dot(xn, wu_ref[...])
    acc_sc[...] += _dot(gate * _sigmoid(gate) * up, wd_ref[...])

    @pl.when(f == pl.num_programs(1) - 1)
    def _():
        o_ref[...] = x_ref[...] + acc_sc[...]


def _ffn(x, g, w_gu, w_down, tm):
    n = x.shape[0]
    nf = D_FF // FFN_TF
    return pl.pallas_call(
        _ffn_kernel,
        grid=(n // tm, nf),
        in_specs=[pl.BlockSpec((tm, D_MODEL), lambda i, f: (i, 0)),
                  pl.BlockSpec((1, D_MODEL), lambda i, f: (0, 0)),
                  pl.BlockSpec((D_MODEL, FFN_TF), lambda i, f: (0, f)),
                  pl.BlockSpec((D_MODEL, FFN_TF), lambda i, f: (0, nf + f)),
                  pl.BlockSpec((FFN_TF, D_MODEL), lambda i, f: (f, 0))],
        out_specs=pl.BlockSpec((tm, D_MODEL), lambda i, f: (i, 0)),
        out_shape=jax.ShapeDtypeStruct((n, D_MODEL), F32),
        scratch_shapes=[pltpu.VMEM((tm, D_MODEL), BF16), pltpu.VMEM((tm, D_MODEL), F32)],
        compiler_params=_cparams(("parallel", "arbitrary")),
    )(x, g, w_gu, w_gu, w_down)


def _odd_in_kernel(x_ref, g_ref, wq_ref, wk_ref, wv_ref, wg_ref, wxa_ref, a2_ref, ab_ref,
                   q_ref, k_ref, v_ref, gate_ref, la_ref):
    xn = _rms(x_ref[...], g_ref[...]).astype(BF16)
    q_ref[...] = _dot(xn, wq_ref[...]) * (GLA_DK ** -0.5)
    k_ref[...] = _dot(xn, wk_ref[...])
    v_ref[...] = _dot(xn, wv_ref[...])
    gate_ref[...] = _dot(xn, wg_ref[...])
    z = _dot(_dot(xn, wxa_ref[...]), a2_ref[...]) + ab_ref[...]
    la_ref[...] = -_softplus(-z) * (1.0 / GLA_GATE_NORM)


def _odd_in(x, w, tm):
    n = x.shape[0]
    return pl.pallas_call(
        _odd_in_kernel,
        grid=(n // tm,),
        in_specs=[_row_spec(tm, D_MODEL), _const_spec((1, D_MODEL)),
                  _const_spec((D_MODEL, GLA_KDIM)), _const_spec((D_MODEL, GLA_KDIM)),
                  _const_spec((D_MODEL, GLA_VDIM)), _const_spec((D_MODEL, GLA_VDIM)),
                  _const_spec((D_MODEL, LANES)), _const_spec((LANES, GLA_KDIM)),
                  _const_spec((1, GLA_KDIM))],
        out_specs=[_row_spec(tm, GLA_KDIM), _row_spec(tm, GLA_KDIM), _row_spec(tm, GLA_VDIM),
                   _row_spec(tm, GLA_VDIM), _row_spec(tm, GLA_KDIM)],
        out_shape=[jax.ShapeDtypeStruct((n, GLA_KDIM), F32), jax.ShapeDtypeStruct((n, GLA_KDIM), F32),
                   jax.ShapeDtypeStruct((n, GLA_VDIM), F32), jax.ShapeDtypeStruct((n, GLA_VDIM), F32),
                   jax.ShapeDtypeStruct((n, GLA_KDIM), F32)],
        compiler_params=_cparams(("parallel",)),
    )(x, w["norm_mix"], w["w_q"], w["w_k"], w["w_v"], w["w_g"], w["w_xa"], w["a2p"], w["ab"])


def _gla_kernel(q_ref, k_ref, v_ref, la_ref, s0_ref, o_ref, st_ref):
    c = GLA_CHUNK

    @pl.when(pl.program_id(1) == 0)
    def _():
        st_ref[...] = s0_ref[...]

    row = lax.broadcasted_iota(jnp.int32, (c, c), 0)
    col = lax.broadcasted_iota(jnp.int32, (c, c), 1)
    tri = row >= col
    tri_b = jnp.where(tri, 1.0, 0.0).astype(BF16)
    for h in range(GLA_HEADS):
        ks = slice(h * GLA_DK, (h + 1) * GLA_DK)
        vs = slice(h * GLA_DV, (h + 1) * GLA_DV)
        b = _dot_exact_lhs(tri_b, la_ref[:, ks])
        q = q_ref[:, ks]
        k = k_ref[:, ks]
        v = v_ref[:, vs]
        b_end = b[c - 1:c, :]
        qe = (q * jnp.exp(b)).astype(BF16)
        ke = (k * jnp.exp(-b)).astype(BF16)
        a_mat = jnp.where(tri, _dot_nt(qe, ke), 0.0)
        st = st_ref[0, h]
        o_ref[:, vs] = _dot_nt(qe, st) + _dot(a_mat, v)
        k_end = k * jnp.exp(b_end - b)
        st_ref[0, h] = st * jnp.exp(b_end) + _dot(v.T, k_end)


def _gla(q, k, v, la, s0t, batch, seq):
    nc = seq // GLA_CHUNK
    rspec = lambda width: pl.BlockSpec((GLA_CHUNK, width), lambda b, c: (b * nc + c, 0))
    sspec = pl.BlockSpec((1, GLA_HEADS, GLA_DV, GLA_DK), lambda b, c: (b, 0, 0, 0))
    return pl.pallas_call(
        _gla_kernel,
        grid=(batch, nc),
        in_specs=[rspec(GLA_KDIM), rspec(GLA_KDIM), rspec(GLA_VDIM), rspec(GLA_KDIM), sspec],
        out_specs=[rspec(GLA_VDIM), sspec],
        out_shape=[jax.ShapeDtypeStruct(v.shape, F32), jax.ShapeDtypeStruct(s0t.shape, F32)],
        compiler_params=_cparams(("parallel", "arbitrary")),
    )(q, k, v, la, s0t)


def _odd_out_kernel(o_ref, gate_ref, gn_ref, wo_ref, x_ref, y_ref):
    parts = []
    for h in range(GLA_HEADS):
        vs = slice(h * GLA_DV, (h + 1) * GLA_DV)
        parts.append(_rms(o_ref[:, vs], gn_ref[:, vs]))
    gate = gate_ref[...]
    on = jnp.concatenate(parts, axis=1) * (gate * _sigmoid(gate))
    y_ref[...] = x_ref[...] + _dot(on, wo_ref[...])


def _odd_out(o, gate, x, w, tm):
    n = x.shape[0]
    return pl.pallas_call(
        _odd_out_kernel,
        grid=(n // tm,),
        in_specs=[_row_spec(tm, GLA_VDIM), _row_spec(tm, GLA_VDIM), _const_spec((1, GLA_VDIM)),
                  _const_spec((GLA_VDIM, D_MODEL)), _row_spec(tm, D_MODEL)],
        out_specs=_row_spec(tm, D_MODEL),
        out_shape=jax.ShapeDtypeStruct((n, D_MODEL), F32),
        compiler_params=_cparams(("parallel",)),
    )(o, gate, w["gla_norm"], w["w_out"], x)


def _pack_bf16_pairs(lo, hi):
    lo_bits = pltpu.bitcast(lo.astype(BF16).astype(F32), jnp.uint32)
    hi_bits = pltpu.bitcast(hi.astype(BF16).astype(F32), jnp.uint32)
    return (lo_bits >> 16) | (hi_bits & jnp.uint32(0xFFFF0000))


def _unpack_bf16_pairs(u):
    lo = pltpu.bitcast(u << 16, F32)
    hi = pltpu.bitcast(u & jnp.uint32(0xFFFF0000), F32)
    return lo, hi


def _router_kernel(x_ref, g_ref, wr_ref, xp_ref, idx_ref, gate_ref):
    xn = _rms(x_ref[...], g_ref[...])
    half = D_MODEL // 2
    xp_ref[...] = _pack_bf16_pairs(xn[:, :half], xn[:, half:])
    logits = _dot_f32ish(xn, wr_ref[...])
    lane = lax.broadcasted_iota(jnp.int32, logits.shape, 1)
    logits = jnp.where(lane < N_EXPERTS, logits, NEG_BIG)
    m1 = jnp.max(logits, axis=-1, keepdims=True)
    i1 = jnp.min(jnp.where(logits == m1, lane, LANES), axis=-1, keepdims=True)
    rest = jnp.where(lane == i1, NEG_BIG, logits)
    m2 = jnp.max(rest, axis=-1, keepdims=True)
    i2 = jnp.min(jnp.where(rest == m2, lane, LANES), axis=-1, keepdims=True)
    e2 = jnp.exp(m2 - m1)
    g1 = 1.0 / (1.0 + e2)
    g2 = e2 / (1.0 + e2)
    idx_ref[...] = jnp.where(lane == 0, i1, jnp.where(lane == 1, i2, 0))
    gate_ref[...] = jnp.where(lane == 0, g1, jnp.where(lane == 1, g2, 0.0))


def _router(x, g, wr, tm):
    n = x.shape[0]
    return pl.pallas_call(
        _router_kernel,
        grid=(n // tm,),
        in_specs=[_row_spec(tm, D_MODEL), _const_spec((1, D_MODEL)), _const_spec((D_MODEL, LANES))],
        out_specs=[_row_spec(tm, D_MODEL // 2), _row_spec(tm, LANES), _row_spec(tm, LANES)],
        out_shape=[jax.ShapeDtypeStruct((n, D_MODEL // 2), jnp.uint32),
                   jax.ShapeDtypeStruct((n, LANES), jnp.int32), jax.ShapeDtypeStruct((n, LANES), F32)],
        compiler_params=_cparams(("parallel",)),
    )(x, g, wr)


MOE_TF = 512
MOE_CHUNK = 2 * LANES


def _route(top_i, tm):
    n = top_i.shape[0]
    slots = 2 * n
    n_tiles = -(-(slots + N_EXPERTS * (tm - 1)) // tm)
    e_flat = top_i.reshape(-1)
    onehot = (e_flat[:, None] == jnp.arange(N_EXPERTS, dtype=jnp.int32)[None, :]).astype(jnp.int32)
    csum = jnp.cumsum(onehot, axis=0)
    rank = jnp.sum(onehot * csum, axis=1) - 1
    counts = csum[-1]
    padded = ((counts + tm - 1) // tm) * tm
    ends = jnp.cumsum(padded)
    starts = ends - padded
    dest = (jnp.sum(onehot * starts[None, :], axis=1) + rank).astype(jnp.int32)
    tile_start = jnp.arange(n_tiles, dtype=jnp.int32) * tm
    tile_expert = jnp.minimum(jnp.sum((tile_start[:, None] >= ends[None, :]).astype(jnp.int32), axis=1),
                              N_EXPERTS - 1).astype(jnp.int32)
    tile_valid = (tile_start < ends[-1]).astype(jnp.int32)
    src = jnp.zeros((n_tiles * tm,), jnp.int32).at[dest].set(jnp.arange(slots, dtype=jnp.int32) // 2)
    return dest, src, tile_expert, tile_valid


def _moe_gather_kernel(src_ref, x_ref, o_ref, *, tg):
    base = pl.program_id(0) * tg

    def body(r, carry):
        o_ref[pl.ds(r, 1), :] = x_ref[pl.ds(src_ref[base + r], 1), :]
        return carry

    lax.fori_loop(0, tg, body, 0, unroll=8)


def _moe_gather(src, xp, tg):
    rows = src.shape[0]
    width = xp.shape[1]
    grid_spec = pltpu.PrefetchScalarGridSpec(
        num_scalar_prefetch=1,
        grid=(rows // tg,),
        in_specs=[pl.BlockSpec(memory_space=pltpu.VMEM)],
        out_specs=pl.BlockSpec((tg, width), lambda i, s: (i, 0)),
    )
    return pl.pallas_call(
        functools.partial(_moe_gather_kernel, tg=tg),
        grid_spec=grid_spec,
        out_shape=jax.ShapeDtypeStruct((rows, width), jnp.uint32),
        compiler_params=_cparams(("arbitrary",)),
    )(src, xp)


def _moe_up_kernel(te_ref, tv_ref, xs_ref, wg_ref, wu_ref, h_ref):
    @pl.when(tv_ref[pl.program_id(1)] != 0)
    def _():
        lo, hi = _unpack_bf16_pairs(xs_ref[...])
        half = D_MODEL // 2
        gate = _dot(lo, wg_ref[:half, :]) + _dot(hi, wg_ref[half:, :])
        up = _dot(lo, wu_ref[:half, :]) + _dot(hi, wu_ref[half:, :])
        h_ref[...] = (gate * _sigmoid(gate) * up).astype(BF16)


def _moe_up(te, tv, xs, w_gu, tm):
    rows = xs.shape[0]
    nf = D_FF_EXPERT // MOE_TF
    grid_spec = pltpu.PrefetchScalarGridSpec(
        num_scalar_prefetch=2,
        grid=(nf, rows // tm),
        in_specs=[pl.BlockSpec((tm, D_MODEL // 2), lambda f, t, te, tv: (t, 0)),
                  pl.BlockSpec((None, D_MODEL, MOE_TF), lambda f, t, te, tv: (te[t], 0, f)),
                  pl.BlockSpec((None, D_MODEL, MOE_TF), lambda f, t, te, tv: (te[t], 0, nf + f))],
        out_specs=pl.BlockSpec((tm, MOE_TF), lambda f, t, te, tv: (t, f)),
    )
    return pl.pallas_call(
        _moe_up_kernel,
        grid_spec=grid_spec,
        out_shape=jax.ShapeDtypeStruct((rows, D_FF_EXPERT), BF16),
        compiler_params=_cparams(("arbitrary", "arbitrary")),
    )(te, tv, xs, w_gu, w_gu)


def _moe_down_kernel(te_ref, tv_ref, h_ref, wd_ref, y_ref):
    @pl.when(tv_ref[pl.program_id(0)] != 0)
    def _():
        y = _dot(h_ref[...], wd_ref[...])
        parts = []
        for c in range(D_MODEL // MOE_CHUNK):
            lo = y[:, c * MOE_CHUNK:c * MOE_CHUNK + LANES]
            hi = y[:, c * MOE_CHUNK + LANES:(c + 1) * MOE_CHUNK]
            parts.append(_pack_bf16_pairs(lo, hi))
        y_ref[...] = jnp.concatenate(parts, axis=1)


def _moe_down(te, tv, h, w_down, tm):
    rows = h.shape[0]
    grid_spec = pltpu.PrefetchScalarGridSpec(
        num_scalar_prefetch=2,
        grid=(rows // tm,),
        in_specs=[pl.BlockSpec((tm, D_FF_EXPERT), lambda t, te, tv: (t, 0)),
                  pl.BlockSpec((None, D_FF_EXPERT, D_MODEL), lambda t, te, tv: (te[t], 0, 0))],
        out_specs=pl.BlockSpec((tm, D_MODEL // 2), lambda t, te, tv: (t, 0)),
    )
    return pl.pallas_call(
        _moe_down_kernel,
        grid_spec=grid_spec,
        out_shape=jax.ShapeDtypeStruct((rows, D_MODEL // 2), jnp.uint32),
        compiler_params=_cparams(("arbitrary",)),
    )(te, tv, h, w_down)


def _moe_combine_kernel(dest_ref, ys_ref, g1_ref, g2_ref, x_ref, o_ref, *, tmc):
    base = 2 * pl.program_id(1) * tmc

    def body(r, carry):
        row = pl.ds(r, 1)
        lo1, hi1 = _unpack_bf16_pairs(ys_ref[pl.ds(dest_ref[base + 2 * r], 1), :])
        lo2, hi2 = _unpack_bf16_pairs(ys_ref[pl.ds(dest_ref[base + 2 * r + 1], 1), :])
        g1 = g1_ref[row, :]
        g2 = g2_ref[row, :]
        moe = jnp.concatenate([g1 * lo1 + g2 * lo2, g1 * hi1 + g2 * hi2], axis=1)
        o_ref[row, :] = x_ref[row, :] + moe
        return carry

    lax.fori_loop(0, tmc, body, 0, unroll=8)


def _moe_combine(dest, ys, g1b, g2b, x, tmc):
    n = x.shape[0]
    rows = ys.shape[0]
    grid_spec = pltpu.PrefetchScalarGridSpec(
        num_scalar_prefetch=1,
        grid=(D_MODEL // MOE_CHUNK, n // tmc),
        in_specs=[pl.BlockSpec((rows, LANES), lambda c, i, d: (0, c)),
                  pl.BlockSpec((tmc, LANES), lambda c, i, d: (i, 0)),
                  pl.BlockSpec((tmc, LANES), lambda c, i, d: (i, 0)),
                  pl.BlockSpec((tmc, MOE_CHUNK), lambda c, i, d: (i, c))],
        out_specs=pl.BlockSpec((tmc, MOE_CHUNK), lambda c, i, d: (i, c)),
    )
    return pl.pallas_call(
        functools.partial(_moe_combine_kernel, tmc=tmc),
        grid_spec=grid_spec,
        out_shape=jax.ShapeDtypeStruct((n, D_MODEL), F32),
        compiler_params=_cparams(("arbitrary", "arbitrary")),
    )(dest, ys, g1b, g2b, x)


def _final_norm_kernel(x_ref, g_ref, o_ref):
    o_ref[...] = _rms(x_ref[...], g_ref[...])


def _final_norm(x, g, tm):
    n = x.shape[0]
    return pl.pallas_call(
        _final_norm_kernel,
        grid=(n // tm,),
        in_specs=[_row_spec(tm, D_MODEL), _const_spec((1, D_MODEL))],
        out_specs=_row_spec(tm, D_MODEL),
        out_shape=jax.ShapeDtypeStruct((n, D_MODEL), F32),
        compiler_params=_cparams(("parallel",)),
    )(x, g)


def _scan_vec_layout(x, batch, seq):
    bh = batch * RWKV_HEADS
    x4 = x.reshape(batch, seq, RWKV_HEADS, RWKV_HEAD).transpose(1, 3, 0, 2).reshape(seq, RWKV_HEAD, bh)
    if bh < LANES:
        return jnp.concatenate([x4, x4], axis=-1)[None]
    return x4.reshape(seq, RWKV_HEAD, bh // LANES, LANES).transpose(2, 0, 1, 3)


def _scan_val_layout(v, batch, seq):
    bh = batch * RWKV_HEADS
    v4 = v.reshape(batch, seq, RWKV_HEADS, RWKV_HEAD).transpose(1, 3, 0, 2).reshape(seq, RWKV_HEAD, bh)
    nvb = RWKV_HEAD // SCAN_VS
    if bh < LANES:
        return v4.reshape(seq, nvb, SCAN_VS, bh).transpose(0, 2, 1, 3).reshape(seq, SCAN_VS, LANES)[None]
    nbl = bh // LANES
    return (v4.reshape(seq, nvb, SCAN_VS, nbl, LANES).transpose(3, 1, 0, 2, 4)
            .reshape(nbl * nvb, seq, SCAN_VS, LANES))


def _scan_val_unlayout(y, batch, seq):
    bh = batch * RWKV_HEADS
    nvb = RWKV_HEAD // SCAN_VS
    if bh < LANES:
        v4 = y[0].reshape(seq, SCAN_VS, nvb, bh).transpose(0, 2, 1, 3).reshape(seq, RWKV_HEAD, bh)
    else:
        nbl = bh // LANES
        v4 = (y.reshape(nbl, nvb, seq, SCAN_VS, LANES).transpose(2, 1, 3, 0, 4)
              .reshape(seq, RWKV_HEAD, bh))
    return (v4.reshape(seq, RWKV_HEAD, batch, RWKV_HEADS).transpose(2, 0, 3, 1)
            .reshape(batch * seq, RWKV_DIM))


def _scan_state_layout(s, batch):
    bh = batch * RWKV_HEADS
    nvb = RWKV_HEAD // SCAN_VS
    s4 = s.transpose(3, 2, 0, 1).reshape(RWKV_HEAD, RWKV_HEAD, bh)
    if bh < LANES:
        return (s4.reshape(RWKV_HEAD, nvb, SCAN_VS, bh).transpose(0, 2, 1, 3)
                .reshape(RWKV_HEAD, SCAN_VS, LANES)[None])
    nbl = bh // LANES
    return (s4.reshape(RWKV_HEAD, nvb, SCAN_VS, nbl, LANES).transpose(3, 1, 0, 2, 4)
            .reshape(nbl * nvb, RWKV_HEAD, SCAN_VS, LANES))


def _scan_state_unlayout(arr, batch):
    bh = batch * RWKV_HEADS
    nvb = RWKV_HEAD // SCAN_VS
    if bh < LANES:
        s = arr[0].reshape(RWKV_HEAD, SCAN_VS, nvb, bh).transpose(3, 2, 1, 0)
    else:
        nbl = bh // LANES
        s = arr.reshape(nbl, nvb, RWKV_HEAD, SCAN_VS, LANES).transpose(0, 4, 1, 3, 2)
    return s.reshape(batch, RWKV_HEADS, RWKV_HEAD, RWKV_HEAD)


def _swap_halves(w):
    half = w.shape[-1] // 2
    return jnp.concatenate([w[..., half:], w[..., :half]], axis=-1)


def _prep_even(i, norm_mix, norm_ffn, w_in, q_norm, kv_norm, w_uq, w_uk, w_uv, mu, w0, w2, a0, a2,
               g2, k_k, k_a, r_k, ln_g, ln_b, w_out, ffn_gu, ffn_down):
    w = {}
    row = lambda v: v[i].reshape(1, -1)
    w_in = w_in[i]
    w["norm_mix"] = row(norm_mix)
    w["norm_ffn"] = row(norm_ffn)
    w["w_q"] = w_in[:, :MLA_Q_RANK].astype(BF16)
    w_kv = w_in[:, MLA_Q_RANK:MLA_Q_RANK + MLA_LAT]
    w["w_ckv"] = w_kv[:, :MLA_KV_RANK].astype(BF16)
    w["w_pe_a"] = w_kv[:, MLA_KV_RANK:].astype(BF16)
    w["w_pe_b"] = _swap_halves(w_kv[:, MLA_KV_RANK:]).astype(BF16)
    w["w_rw"] = w_in[:, MLA_Q_RANK + MLA_LAT:].astype(BF16)
    w["q_norm"] = row(q_norm)
    w["kv_norm"] = row(kv_norm)
    uq = w_uq[i].reshape(MLA_Q_RANK, MLA_HEADS, MLA_NOPE + MLA_ROPE)
    uq_pe = uq[:, :, MLA_NOPE:]
    w["w_qpe_a"] = uq_pe.reshape(MLA_Q_RANK, -1).astype(BF16)
    w["w_qpe_b"] = _swap_halves(uq_pe).reshape(MLA_Q_RANK, -1).astype(BF16)
    w["w_qlat"] = _fold_qlat(uq[:, :, :MLA_NOPE].transpose(1, 0, 2), w_uk[i].transpose(1, 0, 2))
    uv = w_uv[i].transpose(1, 0, 2).reshape(MLA_HEADS // 2, 2, MLA_KV_RANK, MLA_V)
    zero = jnp.zeros_like(uv[:, 0])
    w["w_uv_bd"] = jnp.concatenate(
        [jnp.concatenate([uv[:, 0], zero], axis=-1), jnp.concatenate([zero, uv[:, 1]], axis=-1)],
        axis=1).astype(BF16)
    w["mu"] = row(mu)
    w["w0"] = row(w0)
    pad = lambda m, before: jnp.pad(m, ((before, LANES - before - m.shape[0]), (0, 0))).astype(BF16)
    w["w2p"] = pad(w2[i], 0)
    w["a2p"] = pad(a2[i], RWKV_W_LORA)
    w["a0"] = row(a0)
    w["g2"] = g2[i].astype(BF16)
    w["k_k"] = row(k_k)
    w["k_a"] = row(k_a)
    w["r_k"] = row(r_k)
    w["ln_g"] = row(ln_g)
    w["ln_b"] = row(ln_b)
    head = jnp.arange(RWKV_DIM) // RWKV_HEAD
    w["ones_bd"] = (head[:, None] == head[None, :]).astype(BF16)
    w["w_out_a"] = w_out[i][:MLA_HEADS * MLA_V].astype(BF16)
    w["w_out_b"] = w_out[i][MLA_HEADS * MLA_V:].astype(BF16)
    w["ffn_gu"] = ffn_gu[i].astype(BF16)
    w["ffn_down"] = ffn_down[i].astype(BF16)
    return w


def _prep_odd(i, norm_mix, norm_ffn, w_in, a2, ab, gla_norm, w_out, router, moe_gu, moe_down):
    w = {}
    row = lambda v: v[i].reshape(1, -1)
    w_in = w_in[i]
    w["norm_mix"] = row(norm_mix)
    w["norm_ffn"] = row(norm_ffn)
    w["w_q"] = w_in[:, :GLA_KDIM].astype(BF16)
    w["w_k"] = w_in[:, GLA_KDIM:2 * GLA_KDIM].astype(BF16)
    w["w_v"] = w_in[:, 2 * GLA_KDIM:2 * GLA_KDIM + GLA_VDIM].astype(BF16)
    w["w_g"] = w_in[:, 2 * GLA_KDIM + GLA_VDIM:2 * GLA_KDIM + 2 * GLA_VDIM].astype(BF16)
    w["w_xa"] = jnp.pad(w_in[:, 2 * GLA_KDIM + 2 * GLA_VDIM:],
                        ((0, 0), (0, LANES - GLA_GATE_RANK))).astype(BF16)
    w["a2p"] = jnp.pad(a2[i], ((0, LANES - GLA_GATE_RANK), (0, 0))).astype(BF16)
    w["ab"] = row(ab)
    w["gla_norm"] = row(gla_norm)
    w["w_out"] = w_out[i].astype(BF16)
    w["router"] = jnp.pad(router[i], ((0, 0), (0, LANES - N_EXPERTS)))
    w["moe_gu"] = moe_gu[i].astype(BF16)
    w["moe_down"] = moe_down[i].astype(BF16)
    return w


def _rope_tables(pos, reps):
    inv = ROPE_THETA ** (-jnp.arange(0, MLA_ROPE, 2, dtype=F32) / MLA_ROPE)
    ang = pos.astype(F32)[:, None] * inv[None, :]
    cos, sin = jnp.cos(ang), jnp.sin(ang)
    cs = jnp.tile(jnp.concatenate([cos, cos], axis=-1), (reps, 1))
    sn = jnp.tile(jnp.concatenate([-sin, sin], axis=-1), (reps, 1))
    return {"cs": cs, "sn": sn, "cs8": jnp.tile(cs, (1, MLA_HEADS)), "sn8": jnp.tile(sn, (1, MLA_HEADS))}


def _even_layer(x, batch, seq, tabs, state, shift0, past, w, tm, tc):
    n = batch * seq
    lat, lat_b, q_lat, q_pe, rw = _even_in(x, w, tabs, tm)
    if past is None:
        o_lat = _mla_prompt(q_lat.reshape(n * MLA_HEADS, MLA_KV_RANK),
                            q_pe.reshape(n * MLA_HEADS, MLA_ROPE), lat_b, batch, seq)
    else:
        cache, layer, page_table = past
        rows = seq * MLA_HEADS
        q_full = jnp.concatenate([q_lat.reshape(batch, rows, MLA_KV_RANK),
                                  q_pe.reshape(batch, rows, MLA_ROPE)], axis=-1)
        new_pad = jnp.pad(lat_b.reshape(batch, seq, MLA_LAT), ((0, 0), (0, PAGE_SIZE - seq), (0, 0)))
        o_lat = _mla_decode(page_table, q_full, new_pad, cache, layer)
    o_lat = o_lat.reshape(n, MLA_HEADS * MLA_KV_RANK)

    rw3 = rw.reshape(batch, seq, RWKV_PROJ)
    prev = jnp.concatenate([shift0[:, None, :], rw3[:, :-1]], axis=1).reshape(n, RWKV_PROJ)
    r, dec, k2, v, na, kb, g, rkv = _rwkv_prep(rw, prev, w, tm)
    xs = [_scan_vec_layout(t, batch, seq) for t in (na, dec, kb, k2, r)]
    y_l, s_l = _rwkv_scan(xs, _scan_val_layout(v, batch, seq), _scan_state_layout(state, batch), tc)
    y = _scan_val_unlayout(y_l, batch, seq)
    new_state = _scan_state_unlayout(s_l, batch)

    x = _even_out(y, rkv, g, o_lat, x, w, tm)
    x = _ffn(x, w["norm_ffn"], w["ffn_gu"], w["ffn_down"], tm)
    return x, lat.reshape(batch, seq, MLA_LAT), new_state, rw3[:, -1]


def _odd_layer(x, batch, seq, state, w, final_norm, tm, tm_moe):
    q, k, v, gate, la = _odd_in(x, w, tm)
    seq_p = -(-seq // GLA_CHUNK) * GLA_CHUNK
    if seq_p != seq:
        padr = lambda t: jnp.pad(t.reshape(batch, seq, -1), ((0, 0), (0, seq_p - seq), (0, 0))
                                 ).reshape(batch * seq_p, -1)
        qp, kp, vp, lap = padr(q), padr(k), padr(v), padr(la)
    else:
        qp, kp, vp, lap = q, k, v, la
    o, st = _gla(qp, kp, vp, lap, state.transpose(0, 1, 3, 2), batch, seq_p)
    if seq_p != seq:
        o = o.reshape(batch, seq_p, GLA_VDIM)[:, :seq].reshape(batch * seq, GLA_VDIM)
    x = _odd_out(o, gate, x, w, tm)
    xp, idx, gates = _router(x, w["norm_ffn"], w["router"], tm)
    dest, src, tile_expert, tile_valid = _route(idx[:, :2], tm_moe)
    xs = _moe_gather(src, xp, tm_moe)
    h = _moe_up(tile_expert, tile_valid, xs, w["moe_gu"], tm_moe)
    ys = _moe_down(tile_expert, tile_valid, h, w["moe_down"], tm_moe)
    g1b = jnp.broadcast_to(gates[:, 0:1], (x.shape[0], LANES))
    g2b = jnp.broadcast_to(gates[:, 1:2], (x.shape[0], LANES))
    z = _moe_combine(dest, ys, g1b, g2b, x, tm)
    y = _final_norm(z, final_norm, tm)
    return y, st.transpose(0, 1, 3, 2)


def kernel(x_prompt, x_sample, cache_mla, state_rwkv, state_rwkv_shift, state_gla, page_table, norm_mix_even, norm_ffn_even, w_in_even, mla_q_norm, mla_kv_norm, mla_w_uq, mla_w_uk, mla_w_uv, rwkv_mu, rwkv_w0, rwkv_w2, rwkv_a0, rwkv_a2, rwkv_g2, rwkv_k_k, rwkv_k_a, rwkv_r_k, rwkv_ln_g, rwkv_ln_b, w_out_even, ffn_w_gu_even, ffn_w_down_even, norm_mix_odd, norm_ffn_odd, w_in_odd, gla_a2, gla_ab, gla_norm, w_out_odd, moe_router, moe_w_gu, moe_w_down, final_norm):
    bp, tp, _ = x_prompt.shape
    bs, ts, _ = x_sample.shape
    past_len = page_table.shape[1] * PAGE_SIZE
    tm_p, tm_s = 512, bs * ts
    we = _prep_even(0, norm_mix_even, norm_ffn_even, w_in_even, mla_q_norm, mla_kv_norm, mla_w_uq,
                    mla_w_uk, mla_w_uv, rwkv_mu, rwkv_w0, rwkv_w2, rwkv_a0, rwkv_a2, rwkv_g2,
                    rwkv_k_k, rwkv_k_a, rwkv_r_k, rwkv_ln_g, rwkv_ln_b, w_out_even, ffn_w_gu_even,
                    ffn_w_down_even)
    wo = _prep_odd(0, norm_mix_odd, norm_ffn_odd, w_in_odd, gla_a2, gla_ab, gla_norm, w_out_odd,
                   moe_router, moe_w_gu, moe_w_down)
    fn = final_norm.reshape(1, -1)
    tabs_p = _rope_tables(jnp.arange(tp), 1)
    tabs_s = _rope_tables(past_len + jnp.arange(ts), bs)

    hp = x_prompt.reshape(bp * tp, D_MODEL)
    hs = x_sample.reshape(bs * ts, D_MODEL)
    zeros_state = jnp.zeros((bp, RWKV_HEADS, RWKV_HEAD, RWKV_HEAD), F32)
    zeros_shift = jnp.zeros((bp, RWKV_PROJ), F32)
    hp, lat_p, rs_p, sh_p = _even_layer(hp, bp, tp, tabs_p, zeros_state, zeros_shift, None, we,
                                        tm_p, 32)
    hs, lat_s, rs_s, sh_s = _even_layer(hs, bs, ts, tabs_s, state_rwkv[0], state_rwkv_shift[0],
                                        (cache_mla, 0, page_table), we, tm_s, ts)
    zeros_gla = jnp.zeros((bp, GLA_HEADS, GLA_DK, GLA_DV), F32)
    yp, gs_p = _odd_layer(hp, bp, tp, zeros_gla, wo, fn, tm_p, 512)
    ys, gs_s = _odd_layer(hs, bs, ts, state_gla[0], wo, fn, tm_s, 128)
    return (yp.reshape(bp, tp, D_MODEL), ys.reshape(bs, ts, D_MODEL), lat_p[None], lat_s[None],
            rs_p[None], rs_s[None], sh_p[None], sh_s[None], gs_p[None], gs_s[None])
```

```python
import functools

import jax
import jax.numpy as jnp
from jax import lax
from jax.experimental import pallas as pl
from jax.experimental.pallas import tpu as pltpu

F32 = jnp.float32
BF16 = jnp.bfloat16

D_MODEL = 1024
PAGE_SIZE = 128
NORM_EPS = 1e-6

MLA_HEADS = 8
MLA_NOPE = 64
MLA_ROPE = 32
MLA_V = 64
MLA_Q_RANK = 384
MLA_KV_RANK = 256
MLA_LAT = MLA_KV_RANK + MLA_ROPE
MLA_SCALE = (MLA_NOPE + MLA_ROPE) ** -0.5
ROPE_THETA = 10000.0

RWKV_HEADS = 8
RWKV_HEAD = 64
RWKV_DIM = RWKV_HEADS * RWKV_HEAD
RWKV_W_LORA = 64
RWKV_A_LORA = 64
RWKV_G_LORA = 128
RWKV_PROJ = 3 * RWKV_DIM + RWKV_W_LORA + RWKV_A_LORA + RWKV_G_LORA
RWKV_LN_EPS = 64e-5

GLA_HEADS = 4
GLA_DK = 128
GLA_DV = 256
GLA_KDIM = GLA_HEADS * GLA_DK
GLA_VDIM = GLA_HEADS * GLA_DV
GLA_GATE_RANK = 16
GLA_GATE_NORM = 16.0
GLA_CHUNK = 128

D_FF = 2816
N_EXPERTS = 8
D_FF_EXPERT = 3584

LANES = 128
VMEM_LIMIT = 56 * 1024 * 1024
NEG_BIG = -1e30
LOG2_E = 1.4426950408889634
Q_PRESCALE = MLA_SCALE * LOG2_E


def _cparams(sem):
    return pltpu.CompilerParams(dimension_semantics=sem, vmem_limit_bytes=VMEM_LIMIT)


def _const_spec(shape):
    nd = len(shape)
    return pl.BlockSpec(shape, lambda *_: (0,) * nd)


def _row_spec(tm, width):
    return pl.BlockSpec((tm, width), lambda i: (i, 0))


def _dot(a, b):
    return jnp.dot(a.astype(BF16), b.astype(BF16), preferred_element_type=F32)


def _dot_nt(a, b):
    return lax.dot_general(a.astype(BF16), b.astype(BF16), (((1,), (1,)), ((), ())),
                           preferred_element_type=F32)


def _split2(x):
    hi = x.astype(BF16)
    lo = (x - hi.astype(F32)).astype(BF16)
    return hi, lo


def _split3(x):
    hi = x.astype(BF16)
    r1 = x - hi.astype(F32)
    mid = r1.astype(BF16)
    lo = (r1 - mid.astype(F32)).astype(BF16)
    return hi, mid, lo


def _dot_exact_rhs(x, e):
    hi, mid, lo = _split3(x)
    return (jnp.dot(hi, e, preferred_element_type=F32) + jnp.dot(mid, e, preferred_element_type=F32)
            + jnp.dot(lo, e, preferred_element_type=F32))


def _dot_exact_lhs(e, x):
    hi, mid, lo = _split3(x)
    return (jnp.dot(e, hi, preferred_element_type=F32) + jnp.dot(e, mid, preferred_element_type=F32)
            + jnp.dot(e, lo, preferred_element_type=F32))


def _dot_f32ish(a, b):
    ah, al = _split2(a)
    bh, bl = _split2(b)
    return (jnp.dot(ah, bh, preferred_element_type=F32) + jnp.dot(ah, bl, preferred_element_type=F32)
            + jnp.dot(al, bh, preferred_element_type=F32))


def _lane_tile(x, width):
    return x if width == LANES else pltpu.repeat(x, width // LANES, axis=1)


def _rms(x, g, eps=NORM_EPS):
    return x * lax.rsqrt(jnp.mean(x * x, axis=-1, keepdims=True) + eps) * g


def _sigmoid(x):
    return 1.0 / (1.0 + jnp.exp(-x))


def _softplus(x):
    return jnp.maximum(x, 0.0) + jnp.log(1.0 + jnp.exp(-jnp.abs(x)))


def _fold_qlat_kernel(uq_ref, uk_ref, o_ref):
    a = uq_ref[...]
    b = uk_ref[...]
    ah, al = _split2(a)
    bh, bl = _split2(b)
    dn = (((1,), (1,)), ((), ()))
    o = (lax.dot_general(ah, bh, dn, preferred_element_type=F32)
         + lax.dot_general(ah, bl, dn, preferred_element_type=F32)
         + lax.dot_general(al, bh, dn, preferred_element_type=F32))
    o_ref[...] = o.astype(BF16)


def _fold_qlat(uq_nope, uk):
    return pl.pallas_call(
        _fold_qlat_kernel,
        grid=(MLA_HEADS,),
        in_specs=[pl.BlockSpec((None, MLA_Q_RANK, MLA_NOPE), lambda h: (h, 0, 0)),
                  pl.BlockSpec((None, MLA_KV_RANK, MLA_NOPE), lambda h: (h, 0, 0))],
        out_specs=pl.BlockSpec((MLA_Q_RANK, MLA_KV_RANK), lambda h: (0, h)),
        out_shape=jax.ShapeDtypeStruct((MLA_Q_RANK, MLA_HEADS * MLA_KV_RANK), BF16),
        compiler_params=_cparams(("arbitrary",)),
    )(uq_nope, uk)


def _even_in_kernel(x_ref, g_ref, wq_ref, wckv_ref, wpa_ref, wpb_ref, wrw_ref, qn_ref, kvn_ref,
                    cs_ref, sn_ref, wql_ref, wqa_ref, wqb_ref, cs8_ref, sn8_ref,
                    lat_ref, latb_ref, ql_ref, qpe_ref, rw_ref):
    xn = _rms(x_ref[...], g_ref[...]).astype(BF16)
    cq = _rms(_dot(xn, wq_ref[...]), qn_ref[...]).astype(BF16)
    ql_ref[...] = (_dot(cq, wql_ref[...]) * Q_PRESCALE).astype(BF16)
    qpe = _dot(cq, wqa_ref[...]) * cs8_ref[...] + _dot(cq, wqb_ref[...]) * sn8_ref[...]
    qpe_ref[...] = (qpe * Q_PRESCALE).astype(BF16)
    ckv = _rms(_dot(xn, wckv_ref[...]), kvn_ref[...])
    kpe = _dot(xn, wpa_ref[...]) * cs_ref[...] + _dot(xn, wpb_ref[...]) * sn_ref[...]
    lat_ref[:, :MLA_KV_RANK] = ckv
    lat_ref[:, MLA_KV_RANK:] = kpe
    latb_ref[:, :MLA_KV_RANK] = ckv.astype(BF16)
    latb_ref[:, MLA_KV_RANK:] = kpe.astype(BF16)
    rw_ref[...] = _dot(xn, wrw_ref[...])


def _even_in(x, w, tabs, tm):
    n = x.shape[0]
    nt = tabs["cs"].shape[0] // tm
    tab = lambda width: pl.BlockSpec((tm, width), lambda i: (i % nt, 0))
    hq = MLA_HEADS * MLA_KV_RANK
    hr = MLA_HEADS * MLA_ROPE
    return pl.pallas_call(
        _even_in_kernel,
        grid=(n // tm,),
        in_specs=[_row_spec(tm, D_MODEL), _const_spec((1, D_MODEL)),
                  _const_spec((D_MODEL, MLA_Q_RANK)), _const_spec((D_MODEL, MLA_KV_RANK)),
                  _const_spec((D_MODEL, MLA_ROPE)), _const_spec((D_MODEL, MLA_ROPE)),
                  _const_spec((D_MODEL, RWKV_PROJ)), _const_spec((1, MLA_Q_RANK)),
                  _const_spec((1, MLA_KV_RANK)), tab(MLA_ROPE), tab(MLA_ROPE),
                  _const_spec((MLA_Q_RANK, hq)), _const_spec((MLA_Q_RANK, hr)),
                  _const_spec((MLA_Q_RANK, hr)), tab(hr), tab(hr)],
        out_specs=[_row_spec(tm, MLA_LAT), _row_spec(tm, MLA_LAT), _row_spec(tm, hq),
                   _row_spec(tm, hr), _row_spec(tm, RWKV_PROJ)],
        out_shape=[jax.ShapeDtypeStruct((n, MLA_LAT), F32), jax.ShapeDtypeStruct((n, MLA_LAT), BF16),
                   jax.ShapeDtypeStruct((n, hq), BF16), jax.ShapeDtypeStruct((n, hr), BF16),
                   jax.ShapeDtypeStruct((n, RWKV_PROJ), F32)],
        compiler_params=_cparams(("parallel",)),
    )(x, w["norm_mix"], w["w_q"], w["w_ckv"], w["w_pe_a"], w["w_pe_b"], w["w_rw"], w["q_norm"],
      w["kv_norm"], tabs["cs"], tabs["sn"], w["w_qlat"], w["w_qpe_a"], w["w_qpe_b"],
      tabs["cs8"], tabs["sn8"])


ATT_TQ = 256
ATT_SUB = 256


def _mla_prompt_kernel(q_ref, qpe_ref, lat_ref, o_ref, m_sc, l_sc, acc_sc):
    i = pl.program_id(1)
    j = pl.program_id(2)
    rows = ATT_TQ * MLA_HEADS

    @pl.when(j == 0)
    def _():
        m_sc[...] = jnp.full(m_sc.shape, NEG_BIG, F32)
        l_sc[...] = jnp.zeros(l_sc.shape, F32)
        acc_sc[...] = jnp.zeros(acc_sc.shape, F32)

    def tile(masked):
        ckv = lat_ref[:, :MLA_KV_RANK]
        kpe = lat_ref[:, MLA_KV_RANK:]

        def sub(r):
            rs = slice(r * ATT_SUB, (r + 1) * ATT_SUB)
            s = _dot_nt(q_ref[rs, :], ckv) + _dot_nt(qpe_ref[rs, :], kpe)
            if masked:
                tok = (r * ATT_SUB + lax.broadcasted_iota(jnp.int32, s.shape, 0)) >> 3
                key = lax.broadcasted_iota(jnp.int32, s.shape, 1)
                s = jnp.where(key <= tok, s, NEG_BIG)
            m_prev = m_sc[rs, :]
            m_new = jnp.maximum(m_prev, jnp.max(s, axis=-1, keepdims=True))
            alpha = jnp.exp2(m_prev - m_new)
            p = jnp.exp2(s - _lane_tile(m_new, ATT_TQ))
            l_sc[rs, :] = alpha * l_sc[rs, :] + jnp.sum(p, axis=-1, keepdims=True)
            acc_sc[rs, :] = _lane_tile(alpha, MLA_KV_RANK) * acc_sc[rs, :] + _dot(p, ckv)
            m_sc[rs, :] = m_new

        for r in range(rows // ATT_SUB):
            sub(r)

    @pl.when(j < i)
    def _():
        tile(False)

    @pl.when(j == i)
    def _():
        tile(True)
        o_ref[...] = (acc_sc[...] / _lane_tile(l_sc[...], MLA_KV_RANK)).astype(BF16)


def _mla_prompt(q_lat, q_pe, lat_b, batch, seq):
    nq = seq // ATT_TQ
    rows = ATT_TQ * MLA_HEADS
    return pl.pallas_call(
        _mla_prompt_kernel,
        grid=(batch, nq, nq),
        in_specs=[pl.BlockSpec((rows, MLA_KV_RANK), lambda b, i, j: (b * nq + i, 0)),
                  pl.BlockSpec((rows, MLA_ROPE), lambda b, i, j: (b * nq + i, 0)),
                  pl.BlockSpec((ATT_TQ, MLA_LAT), lambda b, i, j: (b * nq + jnp.minimum(i, j), 0))],
        out_specs=pl.BlockSpec((rows, MLA_KV_RANK), lambda b, i, j: (b * nq + i, 0)),
        out_shape=jax.ShapeDtypeStruct(q_lat.shape, BF16),
        scratch_shapes=[pltpu.VMEM((rows, LANES), F32), pltpu.VMEM((rows, LANES), F32),
                        pltpu.VMEM((rows, MLA_KV_RANK), F32)],
        compiler_params=_cparams(("parallel", "parallel", "arbitrary")),
    )(q_lat, q_pe, lat_b)


PAGES_PER_STEP = 16


def _mla_decode_kernel(pt_ref, q_ref, new_ref, *rest):
    page_refs = rest[:PAGES_PER_STEP]
    o_ref, m_sc, l_sc, acc_sc = rest[PAGES_PER_STEP:]
    j = pl.program_id(1)
    q = q_ref[0]

    @pl.when(j == 0)
    def _():
        m_sc[...] = jnp.full(m_sc.shape, NEG_BIG, F32)
        l_sc[...] = jnp.zeros(l_sc.shape, F32)
        acc_sc[...] = jnp.zeros(acc_sc.shape, F32)

    def update(s, values_t):
        m_prev = m_sc[...]
        m_new = jnp.maximum(m_prev, jnp.max(s, axis=-1, keepdims=True))
        alpha = jnp.exp2(m_prev - m_new)
        p = jnp.exp2(s - _lane_tile(m_new, s.shape[1]))
        l_sc[...] = alpha * l_sc[...] + jnp.sum(p, axis=-1, keepdims=True)
        acc_sc[...] = _lane_tile(alpha, MLA_KV_RANK) * acc_sc[...] + _dot_nt(p, values_t)
        m_sc[...] = m_new

    keys_t = jnp.concatenate([pr[...].astype(BF16) for pr in page_refs], axis=1)
    update(_dot(q, keys_t), keys_t[:MLA_KV_RANK, :])

    @pl.when(j == pl.num_programs(1) - 1)
    def _():
        new_t = new_ref[0]
        sn = _dot(q, new_t)
        tok = lax.broadcasted_iota(jnp.int32, sn.shape, 0) >> 3
        key = lax.broadcasted_iota(jnp.int32, sn.shape, 1)
        sn = jnp.where(key <= tok, sn, NEG_BIG)
        update(sn, new_t[:MLA_KV_RANK, :])
        o_ref[0] = (acc_sc[...] / _lane_tile(l_sc[...], MLA_KV_RANK)).astype(BF16)


def _mla_decode(page_table, q_full, new_pad_t, cache_t, layer):
    db, n_pages = page_table.shape
    rows = q_full.shape[1]
    steps = n_pages // PAGES_PER_STEP

    def page_spec(p):
        return pl.BlockSpec((None, None, MLA_LAT, PAGE_SIZE),
                            lambda b, j, pt: (layer, pt[b, j * PAGES_PER_STEP + p], 0, 0))

    grid_spec = pltpu.PrefetchScalarGridSpec(
        num_scalar_prefetch=1,
        grid=(db, steps),
        in_specs=[pl.BlockSpec((1, rows, MLA_LAT), lambda b, j, pt: (b, 0, 0)),
                  pl.BlockSpec((1, MLA_LAT, PAGE_SIZE), lambda b, j, pt: (b, 0, 0))]
        + [page_spec(p) for p in range(PAGES_PER_STEP)],
        out_specs=pl.BlockSpec((1, rows, MLA_KV_RANK), lambda b, j, pt: (b, 0, 0)),
        scratch_shapes=[pltpu.VMEM((rows, LANES), F32), pltpu.VMEM((rows, LANES), F32),
                        pltpu.VMEM((rows, MLA_KV_RANK), F32)],
    )
    return pl.pallas_call(
        _mla_decode_kernel,
        grid_spec=grid_spec,
        out_shape=jax.ShapeDtypeStruct((db, rows, MLA_KV_RANK), BF16),
        compiler_params=_cparams(("parallel", "arbitrary")),
    )(page_table, q_full, new_pad_t, *([cache_t] * PAGES_PER_STEP))


def _rwkv_prep_kernel(rw_ref, prev_ref, mu_ref, w0_ref, w2_ref, a0_ref, a2_ref, g2_ref, kk_ref,
                      ka_ref, rk_ref, ones_ref,
                      r_ref, w_ref, k_ref, v_ref, na_ref, kb_ref, g_ref, rkv_ref):
    rw = rw_ref[...]
    xs = rw + (prev_ref[...] - rw) * mu_ref[...]
    d = RWKV_DIM
    r = xs[:, :d]
    k = xs[:, d:2 * d]
    v = xs[:, 2 * d:3 * d]
    xwa = xs[:, 3 * d:3 * d + LANES]
    xg = xs[:, 3 * d + LANES:]
    ones = ones_ref[...]
    w_log = -_softplus(-(w0_ref[...] + _dot(jnp.tanh(xwa), w2_ref[...]))) - 0.5
    w_ref[...] = jnp.exp(-jnp.exp(w_log))
    a = _sigmoid(a0_ref[...] + _dot(xwa, a2_ref[...]))
    g_ref[...] = _dot(_sigmoid(xg), g2_ref[...])
    kk = k * kk_ref[...]
    ss = _dot_exact_rhs(kk * kk, ones)
    kk = kk / jnp.maximum(jnp.sqrt(ss), 1e-12)
    k2 = k * (1.0 + (a - 1.0) * ka_ref[...])
    r_ref[...] = r
    k_ref[...] = k2
    v_ref[...] = v
    na_ref[...] = -kk
    kb_ref[...] = kk * a
    rkv_ref[...] = _dot_exact_rhs(r * k2 * rk_ref[...], ones) * v


def _rwkv_prep(rw, prev, w, tm):
    n = rw.shape[0]
    d = RWKV_DIM
    vec = _const_spec((1, d))
    return pl.pallas_call(
        _rwkv_prep_kernel,
        grid=(n // tm,),
        in_specs=[_row_spec(tm, RWKV_PROJ), _row_spec(tm, RWKV_PROJ), _const_spec((1, RWKV_PROJ)),
                  vec, _const_spec((LANES, d)), vec, _const_spec((LANES, d)),
                  _const_spec((RWKV_G_LORA, d)), vec, vec, vec, _const_spec((d, d))],
        out_specs=[_row_spec(tm, d)] * 8,
        out_shape=[jax.ShapeDtypeStruct((n, d), F32)] * 8,
        compiler_params=_cparams(("parallel",)),
    )(rw, prev, w["mu"], w["w0"], w["w2p"], w["a0"], w["a2p"], w["g2"], w["k_k"], w["k_a"],
      w["r_k"], w["ones_bd"])


SCAN_VS = 32


def _rwkv_scan_kernel(a_ref, w_ref, b_ref, k_ref, r_ref, v_ref, s0_ref, y_ref, s_ref, *, tc):
    @pl.when(pl.program_id(1) == 0)
    def _():
        s_ref[...] = s0_ref[...]

    def step(t, carry):
        sa0 = jnp.zeros((SCAN_VS, LANES), F32)
        sa1 = jnp.zeros((SCAN_VS, LANES), F32)
        for k in range(0, RWKV_HEAD, 2):
            sa0 = sa0 + s_ref[0, k] * a_ref[0, t, k:k + 1, :]
            sa1 = sa1 + s_ref[0, k + 1] * a_ref[0, t, k + 1:k + 2, :]
        sa = sa0 + sa1
        v = v_ref[0, t]
        y0 = jnp.zeros((SCAN_VS, LANES), F32)
        y1 = jnp.zeros((SCAN_VS, LANES), F32)
        for k in range(RWKV_HEAD):
            sn = (s_ref[0, k] * w_ref[0, t, k:k + 1, :] + sa * b_ref[0, t, k:k + 1, :]
                  + v * k_ref[0, t, k:k + 1, :])
            s_ref[0, k] = sn
            if k % 2 == 0:
                y0 = y0 + sn * r_ref[0, t, k:k + 1, :]
            else:
                y1 = y1 + sn * r_ref[0, t, k:k + 1, :]
        y_ref[0, t] = y0 + y1
        return carry

    lax.fori_loop(0, tc, step, 0)


def _rwkv_scan(xs, v, s0, tc):
    nbl, t, _, _ = xs[0].shape
    nb = v.shape[0]
    nvb = nb // nbl
    xspec = pl.BlockSpec((1, tc, RWKV_HEAD, LANES), lambda n, c: (n // nvb, c, 0, 0))
    vspec = pl.BlockSpec((1, tc, SCAN_VS, LANES), lambda n, c: (n, c, 0, 0))
    sspec = pl.BlockSpec((1, RWKV_HEAD, SCAN_VS, LANES), lambda n, c: (n, 0, 0, 0))
    return pl.pallas_call(
        functools.partial(_rwkv_scan_kernel, tc=tc),
        grid=(nb, t // tc),
        in_specs=[xspec] * 5 + [vspec, sspec],
        out_specs=[vspec, sspec],
        out_shape=[jax.ShapeDtypeStruct(v.shape, F32), jax.ShapeDtypeStruct(s0.shape, F32)],
        compiler_params=_cparams(("parallel", "arbitrary")),
    )(*xs, v, s0)


def _even_out_kernel(y_ref, rkv_ref, g_ref, lng_ref, lnb_ref, ones_ref, ol_ref, wuv_ref, woa_ref,
                     wob_ref, x_ref, o_ref):
    ones = ones_ref[...]
    y = y_ref[...]
    inv = 1.0 / RWKV_HEAD
    mean = _dot_exact_rhs(y, ones) * inv
    dlt = y - mean
    var = _dot_exact_rhs(dlt * dlt, ones) * inv
    yn = dlt * lax.rsqrt(var + RWKV_LN_EPS) * lng_ref[...] + lnb_ref[...] + rkv_ref[...]
    ob = (yn * g_ref[...]).astype(BF16)
    pair = 2 * MLA_KV_RANK
    oa = jnp.concatenate(
        [_dot(ol_ref[:, p * pair:(p + 1) * pair], wuv_ref[p]) for p in range(MLA_HEADS // 2)], axis=1)
    o_ref[...] = x_ref[...] + _dot(oa, woa_ref[...]) + _dot(ob, wob_ref[...])


def _even_out(y, rkv, g, o_lat, x, w, tm):
    n = x.shape[0]
    d = RWKV_DIM
    hq = MLA_HEADS * MLA_KV_RANK
    return pl.pallas_call(
        _even_out_kernel,
        grid=(n // tm,),
        in_specs=[_row_spec(tm, d), _row_spec(tm, d), _row_spec(tm, d), _const_spec((1, d)),
                  _const_spec((1, d)), _const_spec((d, d)), _row_spec(tm, hq),
                  _const_spec((MLA_HEADS // 2, 2 * MLA_KV_RANK, 2 * MLA_V)),
                  _const_spec((MLA_HEADS * MLA_V, D_MODEL)), _const_spec((d, D_MODEL)),
                  _row_spec(tm, D_MODEL)],
        out_specs=_row_spec(tm, D_MODEL),
        out_shape=jax.ShapeDtypeStruct((n, D_MODEL), F32),
        compiler_params=_cparams(("parallel",)),
    )(y, rkv, g, w["ln_g"], w["ln_b"], w["ones_bd"], o_lat, w["w_uv_bd"], w["w_out_a"],
      w["w_out_b"], x)


FFN_TF = 1408


def _ffn_kernel(x_ref, g_ref, wg_ref, wu_ref, wd_ref, o_ref, xn_sc, acc_sc):
    f = pl.program_id(1)

    @pl.when(f == 0)
    def _():
        xn_sc[...] = _rms(x_ref[...], g_ref[...]).astype(BF16)
        acc_sc[...] = jnp.zeros(acc_sc.shape, F32)

    xn = xn_sc[...]
    gate = _dot(xn, wg_ref[...])
    up = _dot(xn, wu_ref[...])
    acc_sc[...] += _dot(gate * _sigmoid(gate) * up, wd_ref[...])

    @pl.when(f == pl.num_programs(1) - 1)
    def _():
        o_ref[...] = x_ref[...] + acc_sc[...]


def _ffn(x, g, w_gu, w_down, tm):
    n = x.shape[0]
    nf = D_FF // FFN_TF
    return pl.pallas_call(
        _ffn_kernel,
        grid=(n // tm, nf),
        in_specs=[pl.BlockSpec((tm, D_MODEL), lambda i, f: (i, 0)),
                  pl.BlockSpec((1, D_MODEL), lambda i, f: (0, 0)),
                  pl.BlockSpec((D_MODEL, FFN_TF), lambda i, f: (0, f)),
                  pl.BlockSpec((D_MODEL, FFN_TF), lambda i, f: (0, nf + f)),
                  pl.BlockSpec((FFN_TF, D_MODEL), lambda i, f: (f, 0))],
        out_specs=pl.BlockSpec((tm, D_MODEL), lambda i, f: (i, 0)),
        out_shape=jax.ShapeDtypeStruct((n, D_MODEL), F32),
        scratch_shapes=[pltpu.VMEM((tm, D_MODEL), BF16), pltpu.VMEM((tm, D_MODEL), F32)],
        compiler_params=_cparams(("parallel", "arbitrary")),
    )(x, g, w_gu, w_gu, w_down)


def _odd_in_kernel(x_ref, g_ref, wq_ref, wk_ref, wv_ref, wg_ref, wxa_ref, a2_ref, ab_ref,
                   q_ref, k_ref, v_ref, gate_ref, la_ref):
    xn = _rms(x_ref[...], g_ref[...]).astype(BF16)
    q_ref[...] = _dot(xn, wq_ref[...]) * (GLA_DK ** -0.5)
    k_ref[...] = _dot(xn, wk_ref[...])
    v_ref[...] = _dot(xn, wv_ref[...])
    gate_ref[...] = _dot(xn, wg_ref[...])
    z = _dot(_dot(xn, wxa_ref[...]), a2_ref[...]) + ab_ref[...]
    la_ref[...] = -_softplus(-z) * (1.0 / GLA_GATE_NORM)


def _odd_in(x, w, tm):
    n = x.shape[0]
    return pl.pallas_call(
        _odd_in_kernel,
        grid=(n // tm,),
        in_specs=[_row_spec(tm, D_MODEL), _const_spec((1, D_MODEL)),
                  _const_spec((D_MODEL, GLA_KDIM)), _const_spec((D_MODEL, GLA_KDIM)),
                  _const_spec((D_MODEL, GLA_VDIM)), _const_spec((D_MODEL, GLA_VDIM)),
                  _const_spec((D_MODEL, LANES)), _const_spec((LANES, GLA_KDIM)),
                  _const_spec((1, GLA_KDIM))],
        out_specs=[_row_spec(tm, GLA_KDIM), _row_spec(tm, GLA_KDIM), _row_spec(tm, GLA_VDIM),
                   _row_spec(tm, GLA_VDIM), _row_spec(tm, GLA_KDIM)],
        out_shape=[jax.ShapeDtypeStruct((n, GLA_KDIM), F32), jax.ShapeDtypeStruct((n, GLA_KDIM), F32),
                   jax.ShapeDtypeStruct((n, GLA_VDIM), F32), jax.ShapeDtypeStruct((n, GLA_VDIM), F32),
                   jax.ShapeDtypeStruct((n, GLA_KDIM), F32)],
        compiler_params=_cparams(("parallel",)),
    )(x, w["norm_mix"], w["w_q"], w["w_k"], w["w_v"], w["w_g"], w["w_xa"], w["a2p"], w["ab"])


def _gla_kernel(q_ref, k_ref, v_ref, la_ref, s0_ref, o_ref, st_ref):
    c = GLA_CHUNK

    @pl.when(pl.program_id(1) == 0)
    def _():
        st_ref[...] = s0_ref[...]

    row = lax.broadcasted_iota(jnp.int32, (c, c), 0)
    col = lax.broadcasted_iota(jnp.int32, (c, c), 1)
    tri = row >= col
    tri_b = jnp.where(tri, 1.0, 0.0).astype(BF16)
    for h in range(GLA_HEADS):
        ks = slice(h * GLA_DK, (h + 1) * GLA_DK)
        vs = slice(h * GLA_DV, (h + 1) * GLA_DV)
        b = _dot_exact_lhs(tri_b, la_ref[:, ks])
        q = q_ref[:, ks]
        k = k_ref[:, ks]
        v = v_ref[:, vs]
        b_end = b[c - 1:c, :]
        qe = (q * jnp.exp(b)).astype(BF16)
        ke = (k * jnp.exp(-b)).astype(BF16)
        a_mat = jnp.where(tri, _dot_nt(qe, ke), 0.0)
        st = st_ref[0, h]
        o_ref[:, vs] = _dot_nt(qe, st) + _dot(a_mat, v)
        k_end = k * jnp.exp(b_end - b)
        st_ref[0, h] = st * jnp.exp(b_end) + _dot(v.T, k_end)


def _gla(q, k, v, la, s0t, batch, seq):
    nc = seq // GLA_CHUNK
    rspec = lambda width: pl.BlockSpec((GLA_CHUNK, width), lambda b, c: (b * nc + c, 0))
    sspec = pl.BlockSpec((1, GLA_HEADS, GLA_DV, GLA_DK), lambda b, c: (b, 0, 0, 0))
    return pl.pallas_call(
        _gla_kernel,
        grid=(batch, nc),
        in_specs=[rspec(GLA_KDIM), rspec(GLA_KDIM), rspec(GLA_VDIM), rspec(GLA_KDIM), sspec],
        out_specs=[rspec(GLA_VDIM), sspec],
        out_shape=[jax.ShapeDtypeStruct(v.shape, F32), jax.ShapeDtypeStruct(s0t.shape, F32)],
        compiler_params=_cparams(("parallel", "arbitrary")),
    )(q, k, v, la, s0t)


def _odd_out_kernel(o_ref, gate_ref, gn_ref, wo_ref, x_ref, y_ref):
    parts = []
    for h in range(GLA_HEADS):
        vs = slice(h * GLA_DV, (h + 1) * GLA_DV)
        parts.append(_rms(o_ref[:, vs], gn_ref[:, vs]))
    gate = gate_ref[...]
    on = jnp.concatenate(parts, axis=1) * (gate * _sigmoid(gate))
    y_ref[...] = x_ref[...] + _dot(on, wo_ref[...])


def _odd_out(o, gate, x, w, tm):
    n = x.shape[0]
    return pl.pallas_call(
        _odd_out_kernel,
        grid=(n // tm,),
        in_specs=[_row_spec(tm, GLA_VDIM), _row_spec(tm, GLA_VDIM), _const_spec((1, GLA_VDIM)),
                  _const_spec((GLA_VDIM, D_MODEL)), _row_spec(tm, D_MODEL)],
        out_specs=_row_spec(tm, D_MODEL),
        out_shape=jax.ShapeDtypeStruct((n, D_MODEL), F32),
        compiler_params=_cparams(("parallel",)),
    )(o, gate, w["gla_norm"], w["w_out"], x)


def _pack_bf16_pairs(lo, hi):
    lo_bits = pltpu.bitcast(lo.astype(BF16).astype(F32), jnp.uint32)
    hi_bits = pltpu.bitcast(hi.astype(BF16).astype(F32), jnp.uint32)
    return (lo_bits >> 16) | (hi_bits & jnp.uint32(0xFFFF0000))


def _unpack_bf16_pairs(u):
    lo = pltpu.bitcast(u << 16, F32)
    hi = pltpu.bitcast(u & jnp.uint32(0xFFFF0000), F32)
    return lo, hi


def _router_kernel(x_ref, g_ref, wr_ref, xp_ref, idx_ref, gate_ref):
    xn = _rms(x_ref[...], g_ref[...])
    half = D_MODEL // 2
    xp_ref[...] = _pack_bf16_pairs(xn[:, :half], xn[:, half:])
    logits = _dot_f32ish(xn, wr_ref[...])
    lane = lax.broadcasted_iota(jnp.int32, logits.shape, 1)
    logits = jnp.where(lane < N_EXPERTS, logits, NEG_BIG)
    m1 = jnp.max(logits, axis=-1, keepdims=True)
    i1 = jnp.min(jnp.where(logits == m1, lane, LANES), axis=-1, keepdims=True)
    rest = jnp.where(lane == i1, NEG_BIG, logits)
    m2 = jnp.max(rest, axis=-1, keepdims=True)
    i2 = jnp.min(jnp.where(rest == m2, lane, LANES), axis=-1, keepdims=True)
    e2 = jnp.exp(m2 - m1)
    g1 = 1.0 / (1.0 + e2)
    g2 = e2 / (1.0 + e2)
    idx_ref[...] = jnp.where(lane == 0, i1, jnp.where(lane == 1, i2, 0))
    gate_ref[...] = jnp.where(lane == 0, g1, jnp.where(lane == 1, g2, 0.0))


def _router(x, g, wr, tm):
    n = x.shape[0]
    return pl.pallas_call(
        _router_kernel,
        grid=(n // tm,),
        in_specs=[_row_spec(tm, D_MODEL), _const_spec((1, D_MODEL)), _const_spec((D_MODEL, LANES))],
        out_specs=[_row_spec(tm, D_MODEL // 2), _row_spec(tm, LANES), _row_spec(tm, LANES)],
        out_shape=[jax.ShapeDtypeStruct((n, D_MODEL // 2), jnp.uint32),
                   jax.ShapeDtypeStruct((n, LANES), jnp.int32), jax.ShapeDtypeStruct((n, LANES), F32)],
        compiler_params=_cparams(("parallel",)),
    )(x, g, wr)


MOE_TF = 1792
MOE_CHUNK = 2 * LANES


def _route(top_i, tm):
    n = top_i.shape[0]
    slots = 2 * n
    n_tiles = -(-(slots + N_EXPERTS * (tm - 1)) // tm)
    e_flat = top_i.reshape(-1)
    onehot = (e_flat[:, None] == jnp.arange(N_EXPERTS, dtype=jnp.int32)[None, :]).astype(jnp.int32)
    csum = jnp.cumsum(onehot, axis=0)
    rank = jnp.sum(onehot * csum, axis=1) - 1
    counts = csum[-1]
    padded = ((counts + tm - 1) // tm) * tm
    ends = jnp.cumsum(padded)
    starts = ends - padded
    dest = (jnp.sum(onehot * starts[None, :], axis=1) + rank).astype(jnp.int32)
    tile_start = jnp.arange(n_tiles, dtype=jnp.int32) * tm
    tile_expert = jnp.minimum(jnp.sum((tile_start[:, None] >= ends[None, :]).astype(jnp.int32), axis=1),
                              N_EXPERTS - 1).astype(jnp.int32)
    tile_valid = (tile_start < ends[-1]).astype(jnp.int32)
    src = jnp.zeros((n_tiles * tm,), jnp.int32).at[dest].set(jnp.arange(slots, dtype=jnp.int32) // 2)
    return dest, src, tile_expert, tile_valid


def _moe_gather_kernel(src_ref, x_ref, o_ref, *, tg):
    base = pl.program_id(0) * tg

    def body(r, carry):
        o_ref[pl.ds(r, 1), :] = x_ref[pl.ds(src_ref[base + r], 1), :]
        return carry

    lax.fori_loop(0, tg, body, 0, unroll=8)


def _moe_gather(src, xp, tg):
    rows = src.shape[0]
    width = xp.shape[1]
    grid_spec = pltpu.PrefetchScalarGridSpec(
        num_scalar_prefetch=1,
        grid=(rows // tg,),
        in_specs=[pl.BlockSpec(memory_space=pltpu.VMEM)],
        out_specs=pl.BlockSpec((tg, width), lambda i, s: (i, 0)),
    )
    return pl.pallas_call(
        functools.partial(_moe_gather_kernel, tg=tg),
        grid_spec=grid_spec,
        out_shape=jax.ShapeDtypeStruct((rows, width), jnp.uint32),
        compiler_params=_cparams(("arbitrary",)),
    )(src, xp)


def _moe_up_kernel(te_ref, tv_ref, xs_ref, wg_ref, wu_ref, h_ref):
    @pl.when(tv_ref[pl.program_id(1)] != 0)
    def _():
        lo, hi = _unpack_bf16_pairs(xs_ref[...])
        half = D_MODEL // 2
        gate = _dot(lo, wg_ref[:half, :]) + _dot(hi, wg_ref[half:, :])
        up = _dot(lo, wu_ref[:half, :]) + _dot(hi, wu_ref[half:, :])
        h_ref[...] = (gate * _sigmoid(gate) * up).astype(BF16)


def _moe_up(te, tv, xs, w_gu, tm):
    rows = xs.shape[0]
    nf = D_FF_EXPERT // MOE_TF
    grid_spec = pltpu.PrefetchScalarGridSpec(
        num_scalar_prefetch=2,
        grid=(nf, rows // tm),
        in_specs=[pl.BlockSpec((tm, D_MODEL // 2), lambda f, t, te, tv: (t, 0)),
                  pl.BlockSpec((None, D_MODEL, MOE_TF), lambda f, t, te, tv: (te[t], 0, f)),
                  pl.BlockSpec((None, D_MODEL, MOE_TF), lambda f, t, te, tv: (te[t], 0, nf + f))],
        out_specs=pl.BlockSpec((tm, MOE_TF), lambda f, t, te, tv: (t, f)),
    )
    return pl.pallas_call(
        _moe_up_kernel,
        grid_spec=grid_spec,
        out_shape=jax.ShapeDtypeStruct((rows, D_FF_EXPERT), BF16),
        compiler_params=_cparams(("arbitrary", "arbitrary")),
    )(te, tv, xs, w_gu, w_gu)


def _moe_down_kernel(te_ref, tv_ref, h_ref, wd_ref, y_ref):
    @pl.when(tv_ref[pl.program_id(0)] != 0)
    def _():
        y = _dot(h_ref[...], wd_ref[...])
        parts = []
        for c in range(D_MODEL // MOE_CHUNK):
            lo = y[:, c * MOE_CHUNK:c * MOE_CHUNK + LANES]
            hi = y[:, c * MOE_CHUNK + LANES:(c + 1) * MOE_CHUNK]
            parts.append(_pack_bf16_pairs(lo, hi))
        y_ref[...] = jnp.concatenate(parts, axis=1)


def _moe_down(te, tv, h, w_down, tm):
    rows = h.shape[0]
    grid_spec = pltpu.PrefetchScalarGridSpec(
        num_scalar_prefetch=2,
        grid=(rows // tm,),
        in_specs=[pl.BlockSpec((tm, D_FF_EXPERT), lambda t, te, tv: (t, 0)),
                  pl.BlockSpec((None, D_FF_EXPERT, D_MODEL), lambda t, te, tv: (te[t], 0, 0))],
        out_specs=pl.BlockSpec((tm, D_MODEL // 2), lambda t, te, tv: (t, 0)),
    )
    return pl.pallas_call(
        _moe_down_kernel,
        grid_spec=grid_spec,
        out_shape=jax.ShapeDtypeStruct((rows, D_MODEL // 2), jnp.uint32),
        compiler_params=_cparams(("arbitrary",)),
    )(te, tv, h, w_down)


def _moe_combine_kernel(dest_ref, ys_ref, g1_ref, g2_ref, x_ref, o_ref, *, tmc):
    base = 2 * pl.program_id(1) * tmc

    def body(r, carry):
        row = pl.ds(r, 1)
        lo1, hi1 = _unpack_bf16_pairs(ys_ref[pl.ds(dest_ref[base + 2 * r], 1), :])
        lo2, hi2 = _unpack_bf16_pairs(ys_ref[pl.ds(dest_ref[base + 2 * r + 1], 1), :])
        g1 = g1_ref[row, :]
        g2 = g2_ref[row, :]
        moe = jnp.concatenate([g1 * lo1 + g2 * lo2, g1 * hi1 + g2 * hi2], axis=1)
        o_ref[row, :] = x_ref[row, :] + moe
        return carry

    lax.fori_loop(0, tmc, body, 0, unroll=8)


def _moe_combine(dest, ys, g1b, g2b, x, tmc):
    n = x.shape[0]
    rows = ys.shape[0]
    grid_spec = pltpu.PrefetchScalarGridSpec(
        num_scalar_prefetch=1,
        grid=(D_MODEL // MOE_CHUNK, n // tmc),
        in_specs=[pl.BlockSpec((rows, LANES), lambda c, i, d: (0, c)),
                  pl.BlockSpec((tmc, LANES), lambda c, i, d: (i, 0)),
                  pl.BlockSpec((tmc, LANES), lambda c, i, d: (i, 0)),
                  pl.BlockSpec((tmc, MOE_CHUNK), lambda c, i, d: (i, c))],
        out_specs=pl.BlockSpec((tmc, MOE_CHUNK), lambda c, i, d: (i, c)),
    )
    return pl.pallas_call(
        functools.partial(_moe_combine_kernel, tmc=tmc),
        grid_spec=grid_spec,
        out_shape=jax.ShapeDtypeStruct((n, D_MODEL), F32),
        compiler_params=_cparams(("arbitrary", "arbitrary")),
    )(dest, ys, g1b, g2b, x)


def _final_norm_kernel(x_ref, g_ref, o_ref):
    o_ref[...] = _rms(x_ref[...], g_ref[...])


def _final_norm(x, g, tm):
    n = x.shape[0]
    return pl.pallas_call(
        _final_norm_kernel,
        grid=(n // tm,),
        in_specs=[_row_spec(tm, D_MODEL), _const_spec((1, D_MODEL))],
        out_specs=_row_spec(tm, D_MODEL),
        out_shape=jax.ShapeDtypeStruct((n, D_MODEL), F32),
        compiler_params=_cparams(("parallel",)),
    )(x, g)


def _scan_vec_layout(x, batch, seq):
    bh = batch * RWKV_HEADS
    x4 = x.reshape(batch, seq, RWKV_HEADS, RWKV_HEAD).transpose(1, 3, 0, 2).reshape(seq, RWKV_HEAD, bh)
    if bh < LANES:
        return jnp.concatenate([x4, x4], axis=-1)[None]
    return x4.reshape(seq, RWKV_HEAD, bh // LANES, LANES).transpose(2, 0, 1, 3)


def _scan_val_layout(v, batch, seq):
    bh = batch * RWKV_HEADS
    v4 = v.reshape(batch, seq, RWKV_HEADS, RWKV_HEAD).transpose(1, 3, 0, 2).reshape(seq, RWKV_HEAD, bh)
    nvb = RWKV_HEAD // SCAN_VS
    if bh < LANES:
        return v4.reshape(seq, nvb, SCAN_VS, bh).transpose(0, 2, 1, 3).reshape(seq, SCAN_VS, LANES)[None]
    nbl = bh // LANES
    return (v4.reshape(seq, nvb, SCAN_VS, nbl, LANES).transpose(3, 1, 0, 2, 4)
            .reshape(nbl * nvb, seq, SCAN_VS, LANES))


def _scan_val_unlayout(y, batch, seq):
    bh = batch * RWKV_HEADS
    nvb = RWKV_HEAD // SCAN_VS
    if bh < LANES:
        v4 = y[0].reshape(seq, SCAN_VS, nvb, bh).transpose(0, 2, 1, 3).reshape(seq, RWKV_HEAD, bh)
    else:
        nbl = bh // LANES
        v4 = (y.reshape(nbl, nvb, seq, SCAN_VS, LANES).transpose(2, 1, 3, 0, 4)
              .reshape(seq, RWKV_HEAD, bh))
    return (v4.reshape(seq, RWKV_HEAD, batch, RWKV_HEADS).transpose(2, 0, 3, 1)
            .reshape(batch * seq, RWKV_DIM))


def _scan_state_layout(s, batch):
    bh = batch * RWKV_HEADS
    nvb = RWKV_HEAD // SCAN_VS
    s4 = s.transpose(3, 2, 0, 1).reshape(RWKV_HEAD, RWKV_HEAD, bh)
    if bh < LANES:
        return (s4.reshape(RWKV_HEAD, nvb, SCAN_VS, bh).transpose(0, 2, 1, 3)
                .reshape(RWKV_HEAD, SCAN_VS, LANES)[None])
    nbl = bh // LANES
    return (s4.reshape(RWKV_HEAD, nvb, SCAN_VS, nbl, LANES).transpose(3, 1, 0, 2, 4)
            .reshape(nbl * nvb, RWKV_HEAD, SCAN_VS, LANES))


def _scan_state_unlayout(arr, batch):
    bh = batch * RWKV_HEADS
    nvb = RWKV_HEAD // SCAN_VS
    if bh < LANES:
        s = arr[0].reshape(RWKV_HEAD, SCAN_VS, nvb, bh).transpose(3, 2, 1, 0)
    else:
        nbl = bh // LANES
        s = arr.reshape(nbl, nvb, RWKV_HEAD, SCAN_VS, LANES).transpose(0, 4, 1, 3, 2)
    return s.reshape(batch, RWKV_HEADS, RWKV_HEAD, RWKV_HEAD)


def _swap_halves(w):
    half = w.shape[-1] // 2
    return jnp.concatenate([w[..., half:], w[..., :half]], axis=-1)


def _prep_even(i, norm_mix, norm_ffn, w_in, q_norm, kv_norm, w_uq, w_uk, w_uv, mu, w0, w2, a0, a2,
               g2, k_k, k_a, r_k, ln_g, ln_b, w_out, ffn_gu, ffn_down):
    w = {}
    row = lambda v: v[i].reshape(1, -1)
    w_in = w_in[i]
    w["norm_mix"] = row(norm_mix)
    w["norm_ffn"] = row(norm_ffn)
    w["w_q"] = w_in[:, :MLA_Q_RANK].astype(BF16)
    w_kv = w_in[:, MLA_Q_RANK:MLA_Q_RANK + MLA_LAT]
    w["w_ckv"] = w_kv[:, :MLA_KV_RANK].astype(BF16)
    w["w_pe_a"] = w_kv[:, MLA_KV_RANK:].astype(BF16)
    w["w_pe_b"] = _swap_halves(w_kv[:, MLA_KV_RANK:]).astype(BF16)
    w["w_rw"] = w_in[:, MLA_Q_RANK + MLA_LAT:].astype(BF16)
    w["q_norm"] = row(q_norm)
    w["kv_norm"] = row(kv_norm)
    uq = w_uq[i].reshape(MLA_Q_RANK, MLA_HEADS, MLA_NOPE + MLA_ROPE)
    uq_pe = uq[:, :, MLA_NOPE:]
    w["w_qpe_a"] = uq_pe.reshape(MLA_Q_RANK, -1).astype(BF16)
    w["w_qpe_b"] = _swap_halves(uq_pe).reshape(MLA_Q_RANK, -1).astype(BF16)
    w["w_qlat"] = _fold_qlat(uq[:, :, :MLA_NOPE].transpose(1, 0, 2), w_uk[i].transpose(1, 0, 2))
    uv = w_uv[i].transpose(1, 0, 2).reshape(MLA_HEADS // 2, 2, MLA_KV_RANK, MLA_V)
    zero = jnp.zeros_like(uv[:, 0])
    w["w_uv_bd"] = jnp.concatenate(
        [jnp.concatenate([uv[:, 0], zero], axis=-1), jnp.concatenate([zero, uv[:, 1]], axis=-1)],
        axis=1).astype(BF16)
    w["mu"] = row(mu)
    w["w0"] = row(w0)
    pad = lambda m, before: jnp.pad(m, ((before, LANES - before - m.shape[0]), (0, 0))).astype(BF16)
    w["w2p"] = pad(w2[i], 0)
    w["a2p"] = pad(a2[i], RWKV_W_LORA)
    w["a0"] = row(a0)
    w["g2"] = g2[i].astype(BF16)
    w["k_k"] = row(k_k)
    w["k_a"] = row(k_a)
    w["r_k"] = row(r_k)
    w["ln_g"] = row(ln_g)
    w["ln_b"] = row(ln_b)
    head = jnp.arange(RWKV_DIM) // RWKV_HEAD
    w["ones_bd"] = (head[:, None] == head[None, :]).astype(BF16)
    w["w_out_a"] = w_out[i][:MLA_HEADS * MLA_V].astype(BF16)
    w["w_out_b"] = w_out[i][MLA_HEADS * MLA_V:].astype(BF16)
    w["ffn_gu"] = ffn_gu[i].astype(BF16)
    w["ffn_down"] = ffn_down[i].astype(BF16)
    return w


def _prep_odd(i, norm_mix, norm_ffn, w_in, a2, ab, gla_norm, w_out, router, moe_gu, moe_down):
    w = {}
    row = lambda v: v[i].reshape(1, -1)
    w_in = w_in[i]
    w["norm_mix"] = row(norm_mix)
    w["norm_ffn"] = row(norm_ffn)
    w["w_q"] = w_in[:, :GLA_KDIM].astype(BF16)
    w["w_k"] = w_in[:, GLA_KDIM:2 * GLA_KDIM].astype(BF16)
    w["w_v"] = w_in[:, 2 * GLA_KDIM:2 * GLA_KDIM + GLA_VDIM].astype(BF16)
    w["w_g"] = w_in[:, 2 * GLA_KDIM + GLA_VDIM:2 * GLA_KDIM + 2 * GLA_VDIM].astype(BF16)
    w["w_xa"] = jnp.pad(w_in[:, 2 * GLA_KDIM + 2 * GLA_VDIM:],
                        ((0, 0), (0, LANES - GLA_GATE_RANK))).astype(BF16)
    w["a2p"] = jnp.pad(a2[i], ((0, LANES - GLA_GATE_RANK), (0, 0))).astype(BF16)
    w["ab"] = row(ab)
    w["gla_norm"] = row(gla_norm)
    w["w_out"] = w_out[i].astype(BF16)
    w["router"] = jnp.pad(router[i], ((0, 0), (0, LANES - N_EXPERTS)))
    w["moe_gu"] = moe_gu[i].astype(BF16)
    w["moe_down"] = moe_down[i].astype(BF16)
    return w


def _rope_tables(pos, reps):
    inv = ROPE_THETA ** (-jnp.arange(0, MLA_ROPE, 2, dtype=F32) / MLA_ROPE)
    ang = pos.astype(F32)[:, None] * inv[None, :]
    cos, sin = jnp.cos(ang), jnp.sin(ang)
    cs = jnp.tile(jnp.concatenate([cos, cos], axis=-1), (reps, 1))
    sn = jnp.tile(jnp.concatenate([-sin, sin], axis=-1), (reps, 1))
    return {"cs": cs, "sn": sn, "cs8": jnp.tile(cs, (1, MLA_HEADS)), "sn8": jnp.tile(sn, (1, MLA_HEADS))}


def _even_layer(x, batch, seq, tabs, state, shift0, past, w, tm, tc):
    n = batch * seq
    lat, lat_b, q_lat, q_pe, rw = _even_in(x, w, tabs, tm)
    if past is None:
        o_lat = _mla_prompt(q_lat.reshape(n * MLA_HEADS, MLA_KV_RANK),
                            q_pe.reshape(n * MLA_HEADS, MLA_ROPE), lat_b, batch, seq)
    else:
        cache, layer, page_table = past
        rows = seq * MLA_HEADS
        q_full = jnp.concatenate([q_lat.reshape(batch, rows, MLA_KV_RANK),
                                  q_pe.reshape(batch, rows, MLA_ROPE)], axis=-1)
        new_pad_t = jnp.pad(lat_b.reshape(batch, seq, MLA_LAT),
                            ((0, 0), (0, PAGE_SIZE - seq), (0, 0))).transpose(0, 2, 1)
        o_lat = _mla_decode(page_table, q_full, new_pad_t, cache.transpose(0, 1, 3, 2), layer)
    o_lat = o_lat.reshape(n, MLA_HEADS * MLA_KV_RANK)

    rw3 = rw.reshape(batch, seq, RWKV_PROJ)
    prev = jnp.concatenate([shift0[:, None, :], rw3[:, :-1]], axis=1).reshape(n, RWKV_PROJ)
    r, dec, k2, v, na, kb, g, rkv = _rwkv_prep(rw, prev, w, tm)
    xs = [_scan_vec_layout(t, batch, seq) for t in (na, dec, kb, k2, r)]
    y_l, s_l = _rwkv_scan(xs, _scan_val_layout(v, batch, seq), _scan_state_layout(state, batch), tc)
    y = _scan_val_unlayout(y_l, batch, seq)
    new_state = _scan_state_unlayout(s_l, batch)

    x = _even_out(y, rkv, g, o_lat, x, w, tm)
    x = _ffn(x, w["norm_ffn"], w["ffn_gu"], w["ffn_down"], tm)
    return x, lat.reshape(batch, seq, MLA_LAT), new_state, rw3[:, -1]


def _odd_layer(x, batch, seq, state, w, final_norm, tm, tm_moe):
    q, k, v, gate, la = _odd_in(x, w, tm)
    seq_p = -(-seq // GLA_CHUNK) * GLA_CHUNK
    if seq_p != seq:
        padr = lambda t: jnp.pad(t.reshape(batch, seq, -1), ((0, 0), (0, seq_p - seq), (0, 0))
                                 ).reshape(batch * seq_p, -1)
        qp, kp, vp, lap = padr(q), padr(k), padr(v), padr(la)
    else:
        qp, kp, vp, lap = q, k, v, la
    o, st = _gla(qp, kp, vp, lap, state.transpose(0, 1, 3, 2), batch, seq_p)
    if seq_p != seq:
        o = o.reshape(batch, seq_p, GLA_VDIM)[:, :seq].reshape(batch * seq, GLA_VDIM)
    x = _odd_out(o, gate, x, w, tm)
    xp, idx, gates = _router(x, w["norm_ffn"], w["router"], tm)
    dest, src, tile_expert, tile_valid = _route(idx[:, :2], tm_moe)
    xs = _moe_gather(src, xp, tm_moe)
    h = _moe_up(tile_expert, tile_valid, xs, w["moe_gu"], tm_moe)
    ys = _moe_down(tile_expert, tile_valid, h, w["moe_down"], tm_moe)
    g1b = jnp.broadcast_to(gates[:, 0:1], (x.shape[0], LANES))
    g2b = jnp.broadcast_to(gates[:, 1:2], (x.shape[0], LANES))
    z = _moe_combine(dest, ys, g1b, g2b, x, tm)
    y = _final_norm(z, final_norm, tm)
    return y, st.transpose(0, 1, 3, 2)


def kernel(x_prompt, x_sample, cache_mla, state_rwkv, state_rwkv_shift, state_gla, page_table, norm_mix_even, norm_ffn_even, w_in_even, mla_q_norm, mla_kv_norm, mla_w_uq, mla_w_uk, mla_w_uv, rwkv_mu, rwkv_w0, rwkv_w2, rwkv_a0, rwkv_a2, rwkv_g2, rwkv_k_k, rwkv_k_a, rwkv_r_k, rwkv_ln_g, rwkv_ln_b, w_out_even, ffn_w_gu_even, ffn_w_down_even, norm_mix_odd, norm_ffn_odd, w_in_odd, gla_a2, gla_ab, gla_norm, w_out_odd, moe_router, moe_w_gu, moe_w_down, final_norm):
    bp, tp, _ = x_prompt.shape
    bs, ts, _ = x_sample.shape
    past_len = page_table.shape[1] * PAGE_SIZE
    tm_p, tm_s = 512, bs * ts
    we = _prep_even(0, norm_mix_even, norm_ffn_even, w_in_even, mla_q_norm, mla_kv_norm, mla_w_uq,
                    mla_w_uk, mla_w_uv, rwkv_mu, rwkv_w0, rwkv_w2, rwkv_a0, rwkv_a2, rwkv_g2,
                    rwkv_k_k, rwkv_k_a, rwkv_r_k, rwkv_ln_g, rwkv_ln_b, w_out_even, ffn_w_gu_even,
                    ffn_w_down_even)
    wo = _prep_odd(0, norm_mix_odd, norm_ffn_odd, w_in_odd, gla_a2, gla_ab, gla_norm, w_out_odd,
                   moe_router, moe_w_gu, moe_w_down)
    fn = final_norm.reshape(1, -1)
    tabs_p = _rope_tables(jnp.arange(tp), 1)
    tabs_s = _rope_tables(past_len + jnp.arange(ts), bs)

    hp = x_prompt.reshape(bp * tp, D_MODEL)
    hs = x_sample.reshape(bs * ts, D_MODEL)
    zeros_state = jnp.zeros((bp, RWKV_HEADS, RWKV_HEAD, RWKV_HEAD), F32)
    zeros_shift = jnp.zeros((bp, RWKV_PROJ), F32)
    hp, lat_p, rs_p, sh_p = _even_layer(hp, bp, tp, tabs_p, zeros_state, zeros_shift, None, we,
                                        tm_p, 32)
    hs, lat_s, rs_s, sh_s = _even_layer(hs, bs, ts, tabs_s, state_rwkv[0], state_rwkv_shift[0],
                                        (cache_mla, 0, page_table), we, tm_s, ts)
    zeros_gla = jnp.zeros((bp, GLA_HEADS, GLA_DK, GLA_DV), F32)
    yp, gs_p = _odd_layer(hp, bp, tp, zeros_gla, wo, fn, tm_p, 512)
    ys, gs_s = _odd_layer(hs, bs, ts, state_gla[0], wo, fn, tm_s, 128)
    return (yp.reshape(bp, tp, D_MODEL), ys.reshape(bs, ts, D_MODEL), lat_p[None], lat_s[None],
            rs_p[None], rs_s[None], sh_p[None], sh_s[None], gs_p[None], gs_s[None])
```

```python
import functools

import jax
import jax.numpy as jnp
from jax import lax
from jax.experimental import pallas as pl
from jax.experimental.pallas import tpu as pltpu

F32 = jnp.float32
BF16 = jnp.bfloat16

D_MODEL = 1024
PAGE_SIZE = 128
NORM_EPS = 1e-6

MLA_HEADS = 8
MLA_NOPE = 64
MLA_ROPE = 32
MLA_V = 64
MLA_Q_RANK = 384
MLA_KV_RANK = 256
MLA_LAT = MLA_KV_RANK + MLA_ROPE
MLA_LATB = MLA_KV_RANK + 128
MLA_SCALE = (MLA_NOPE + MLA_ROPE) ** -0.5
ROPE_THETA = 10000.0

RWKV_HEADS = 8
RWKV_HEAD = 64
RWKV_DIM = RWKV_HEADS * RWKV_HEAD
RWKV_W_LORA = 64
RWKV_A_LORA = 64
RWKV_G_LORA = 128
RWKV_PROJ = 3 * RWKV_DIM + RWKV_W_LORA + RWKV_A_LORA + RWKV_G_LORA
RWKV_LN_EPS = 64e-5

GLA_HEADS = 4
GLA_DK = 128
GLA_DV = 256
GLA_KDIM = GLA_HEADS * GLA_DK
GLA_VDIM = GLA_HEADS * GLA_DV
GLA_GATE_RANK = 16
GLA_GATE_NORM = 16.0
GLA_CHUNK = 128

D_FF = 2816
N_EXPERTS = 8
D_FF_EXPERT = 3584

LANES = 128
VMEM_LIMIT = 56 * 1024 * 1024
NEG_BIG = -1e30
LOG2_E = 1.4426950408889634
Q_PRESCALE = MLA_SCALE * LOG2_E


def _cparams(sem):
    return pltpu.CompilerParams(dimension_semantics=sem, vmem_limit_bytes=VMEM_LIMIT)


def _const_spec(shape):
    nd = len(shape)
    return pl.BlockSpec(shape, lambda *_: (0,) * nd)


def _row_spec(tm, width):
    return pl.BlockSpec((tm, width), lambda i: (i, 0))


def _dot(a, b):
    return jnp.dot(a.astype(BF16), b.astype(BF16), preferred_element_type=F32)


def _dot_nt(a, b):
    return lax.dot_general(a.astype(BF16), b.astype(BF16), (((1,), (1,)), ((), ())),
                           preferred_element_type=F32)


def _split2(x):
    hi = x.astype(BF16)
    lo = (x - hi.astype(F32)).astype(BF16)
    return hi, lo


def _split3(x):
    hi = x.astype(BF16)
    r1 = x - hi.astype(F32)
    mid = r1.astype(BF16)
    lo = (r1 - mid.astype(F32)).astype(BF16)
    return hi, mid, lo


def _dot_exact_rhs(x, e):
    hi, mid, lo = _split3(x)
    return (jnp.dot(hi, e, preferred_element_type=F32) + jnp.dot(mid, e, preferred_element_type=F32)
            + jnp.dot(lo, e, preferred_element_type=F32))


def _dot_exact_lhs(e, x):
    hi, mid, lo = _split3(x)
    return (jnp.dot(e, hi, preferred_element_type=F32) + jnp.dot(e, mid, preferred_element_type=F32)
            + jnp.dot(e, lo, preferred_element_type=F32))


def _dot_f32ish(a, b):
    ah, al = _split2(a)
    bh, bl = _split2(b)
    return (jnp.dot(ah, bh, preferred_element_type=F32) + jnp.dot(ah, bl, preferred_element_type=F32)
            + jnp.dot(al, bh, preferred_element_type=F32))


def _lane_tile(x, width):
    return x if width == LANES else jnp.concatenate([x] * (width // LANES), axis=1)


def _rms(x, g, eps=NORM_EPS):
    return x * lax.rsqrt(jnp.mean(x * x, axis=-1, keepdims=True) + eps) * g


def _sigmoid(x):
    return 1.0 / (1.0 + jnp.exp(-x))


def _softplus(x):
    return jnp.maximum(x, 0.0) + jnp.log(1.0 + jnp.exp(-jnp.abs(x)))


def _fold_qlat_kernel(uq_ref, uk_ref, o_ref):
    a = uq_ref[...]
    b = uk_ref[...]
    ah, al = _split2(a)
    bh, bl = _split2(b)
    dn = (((1,), (1,)), ((), ()))
    o = (lax.dot_general(ah, bh, dn, preferred_element_type=F32)
         + lax.dot_general(ah, bl, dn, preferred_element_type=F32)
         + lax.dot_general(al, bh, dn, preferred_element_type=F32))
    o_ref[...] = o.astype(BF16)


def _fold_qlat(uq_nope, uk):
    return pl.pallas_call(
        _fold_qlat_kernel,
        grid=(MLA_HEADS,),
        in_specs=[pl.BlockSpec((None, MLA_Q_RANK, MLA_NOPE), lambda h: (h, 0, 0)),
                  pl.BlockSpec((None, MLA_KV_RANK, MLA_NOPE), lambda h: (h, 0, 0))],
        out_specs=pl.BlockSpec((MLA_Q_RANK, MLA_KV_RANK), lambda h: (0, h)),
        out_shape=jax.ShapeDtypeStruct((MLA_Q_RANK, MLA_HEADS * MLA_KV_RANK), BF16),
        compiler_params=_cparams(("arbitrary",)),
    )(uq_nope, uk)


def _even_in_kernel(x_ref, g_ref, wq_ref, wckv_ref, wpa_ref, wpb_ref, wrw_ref, qn_ref, kvn_ref,
                    cs_ref, sn_ref, wql_ref, wqa_ref, wqb_ref, cs8_ref, sn8_ref,
                    lat_ref, latb_ref, ql_ref, qpe_ref, rw_ref):
    xn = _rms(x_ref[...], g_ref[...]).astype(BF16)
    cq = _rms(_dot(xn, wq_ref[...]), qn_ref[...]).astype(BF16)
    ql_ref[...] = (_dot(cq, wql_ref[...]) * Q_PRESCALE).astype(BF16)
    qpe = _dot(cq, wqa_ref[...]) * cs8_ref[...] + _dot(cq, wqb_ref[...]) * sn8_ref[...]
    qpe_ref[...] = (qpe * Q_PRESCALE).astype(BF16)
    ckv = _rms(_dot(xn, wckv_ref[...]), kvn_ref[...])
    kpe = _dot(xn, wpa_ref[...]) * cs_ref[...] + _dot(xn, wpb_ref[...]) * sn_ref[...]
    lat_ref[:, :MLA_KV_RANK] = ckv
    lat_ref[:, MLA_KV_RANK:] = kpe[:, :MLA_ROPE]
    latb_ref[:, :MLA_KV_RANK] = ckv.astype(BF16)
    latb_ref[:, MLA_KV_RANK:] = kpe.astype(BF16)
    rw_ref[...] = _dot(xn, wrw_ref[...])


def _even_in(x, w, tabs, tm):
    n = x.shape[0]
    nt = tabs["cs"].shape[0] // tm
    tab = lambda width: pl.BlockSpec((tm, width), lambda i: (i % nt, 0))
    hq = MLA_HEADS * MLA_KV_RANK
    hr = MLA_HEADS * LANES
    return pl.pallas_call(
        _even_in_kernel,
        grid=(n // tm,),
        in_specs=[_row_spec(tm, D_MODEL), _const_spec((1, D_MODEL)),
                  _const_spec((D_MODEL, MLA_Q_RANK)), _const_spec((D_MODEL, MLA_KV_RANK)),
                  _const_spec((D_MODEL, LANES)), _const_spec((D_MODEL, LANES)),
                  _const_spec((D_MODEL, RWKV_PROJ)), _const_spec((1, MLA_Q_RANK)),
                  _const_spec((1, MLA_KV_RANK)), tab(LANES), tab(LANES),
                  _const_spec((MLA_Q_RANK, hq)), _const_spec((MLA_Q_RANK, hr)),
                  _const_spec((MLA_Q_RANK, hr)), tab(hr), tab(hr)],
        out_specs=[_row_spec(tm, MLA_LAT), _row_spec(tm, MLA_LATB), _row_spec(tm, hq),
                   _row_spec(tm, hr), _row_spec(tm, RWKV_PROJ)],
        out_shape=[jax.ShapeDtypeStruct((n, MLA_LAT), F32), jax.ShapeDtypeStruct((n, MLA_LATB), BF16),
                   jax.ShapeDtypeStruct((n, hq), BF16), jax.ShapeDtypeStruct((n, hr), BF16),
                   jax.ShapeDtypeStruct((n, RWKV_PROJ), F32)],
        compiler_params=_cparams(("parallel",)),
    )(x, w["norm_mix"], w["w_q"], w["w_ckv"], w["w_pe_a"], w["w_pe_b"], w["w_rw"], w["q_norm"],
      w["kv_norm"], tabs["cs"], tabs["sn"], w["w_qlat"], w["w_qpe_a"], w["w_qpe_b"],
      tabs["cs8"], tabs["sn8"])


ATT_TQ = 256


def _mla_prompt_kernel(qi_ref, kj_ref, ql_ref, qpe_ref, lat_ref, o_ref,
                       m_sc, l_sc, a_sc, acc_sc, s_sc, p_sc):
    step = pl.program_id(1)
    i = qi_ref[step]
    j = kj_ref[step]
    heads = range(MLA_HEADS)

    @pl.when(j == 0)
    def _():
        m_sc[...] = jnp.full(m_sc.shape, NEG_BIG, F32)
        l_sc[...] = jnp.zeros(l_sc.shape, F32)
        acc_sc[...] = jnp.zeros(acc_sc.shape, F32)

    def tile(masked):
        ckv = lat_ref[:, :MLA_KV_RANK]
        kpe = lat_ref[:, MLA_KV_RANK:]
        for h in heads:
            s_sc[h] = (_dot_nt(ql_ref[:, h * MLA_KV_RANK:(h + 1) * MLA_KV_RANK], ckv)
                       + _dot_nt(qpe_ref[:, h * LANES:(h + 1) * LANES], kpe))
        for h in heads:
            s = s_sc[h]
            if masked:
                tok = lax.broadcasted_iota(jnp.int32, s.shape, 0)
                key = lax.broadcasted_iota(jnp.int32, s.shape, 1)
                s = jnp.where(key <= tok, s, NEG_BIG)
            m_prev = m_sc[h]
            m_new = jnp.maximum(m_prev, jnp.max(s, axis=-1, keepdims=True))
            alpha = jnp.exp2(m_prev - m_new)
            p = jnp.exp2(s - _lane_tile(m_new, ATT_TQ))
            l_sc[h] = alpha * l_sc[h] + jnp.sum(p, axis=-1, keepdims=True)
            m_sc[h] = m_new
            a_sc[h] = alpha
            p_sc[h] = p.astype(BF16)
        for h in heads:
            acc_sc[h] = _lane_tile(a_sc[h], MLA_KV_RANK) * acc_sc[h] + _dot(p_sc[h], ckv)

    @pl.when(j < i)
    def _():
        tile(False)

    @pl.when(j == i)
    def _():
        tile(True)
        for h in heads:
            o_ref[:, h * MLA_KV_RANK:(h + 1) * MLA_KV_RANK] = (
                acc_sc[h] / _lane_tile(l_sc[h], MLA_KV_RANK)).astype(BF16)


def _mla_prompt(q_lat, q_pe, lat_b, batch, seq):
    nq = seq // ATT_TQ
    pairs = [(i, j) for i in range(nq) for j in range(i + 1)]
    qi = jnp.array([p[0] for p in pairs], jnp.int32)
    kj = jnp.array([p[1] for p in pairs], jnp.int32)
    hq = MLA_HEADS * MLA_KV_RANK
    grid_spec = pltpu.PrefetchScalarGridSpec(
        num_scalar_prefetch=2,
        grid=(batch, len(pairs)),
        in_specs=[pl.BlockSpec((ATT_TQ, hq), lambda b, s, qi, kj: (b * nq + qi[s], 0)),
                  pl.BlockSpec((ATT_TQ, MLA_HEADS * LANES), lambda b, s, qi, kj: (b * nq + qi[s], 0)),
                  pl.BlockSpec((ATT_TQ, MLA_LATB), lambda b, s, qi, kj: (b * nq + kj[s], 0))],
        out_specs=pl.BlockSpec((ATT_TQ, hq), lambda b, s, qi, kj: (b * nq + qi[s], 0)),
        scratch_shapes=[pltpu.VMEM((MLA_HEADS, ATT_TQ, LANES), F32),
                        pltpu.VMEM((MLA_HEADS, ATT_TQ, LANES), F32),
                        pltpu.VMEM((MLA_HEADS, ATT_TQ, LANES), F32),
                        pltpu.VMEM((MLA_HEADS, ATT_TQ, MLA_KV_RANK), F32),
                        pltpu.VMEM((MLA_HEADS, ATT_TQ, ATT_TQ), F32),
                        pltpu.VMEM((MLA_HEADS, ATT_TQ, ATT_TQ), BF16)],
    )
    return pl.pallas_call(
        _mla_prompt_kernel,
        grid_spec=grid_spec,
        out_shape=jax.ShapeDtypeStruct(q_lat.shape, BF16),
        compiler_params=_cparams(("parallel", "arbitrary")),
    )(qi, kj, q_lat, q_pe, lat_b)


PAGES_PER_STEP = 16


def _mla_decode_kernel(pt_ref, q_ref, new_ref, *rest):
    page_refs = rest[:PAGES_PER_STEP]
    o_ref, m_sc, l_sc, acc_sc = rest[PAGES_PER_STEP:]
    j = pl.program_id(1)
    q = q_ref[0]

    @pl.when(j == 0)
    def _():
        m_sc[...] = jnp.full(m_sc.shape, NEG_BIG, F32)
        l_sc[...] = jnp.zeros(l_sc.shape, F32)
        acc_sc[...] = jnp.zeros(acc_sc.shape, F32)

    def update(s, values_t):
        m_prev = m_sc[...]
        m_new = jnp.maximum(m_prev, jnp.max(s, axis=-1, keepdims=True))
        alpha = jnp.exp2(m_prev - m_new)
        p = jnp.exp2(s - _lane_tile(m_new, s.shape[1]))
        l_sc[...] = alpha * l_sc[...] + jnp.sum(p, axis=-1, keepdims=True)
        acc_sc[...] = _lane_tile(alpha, MLA_KV_RANK) * acc_sc[...] + _dot_nt(p, values_t)
        m_sc[...] = m_new

    keys_t = jnp.concatenate([pr[...].astype(BF16) for pr in page_refs], axis=1)
    update(_dot(q, keys_t), keys_t[:MLA_KV_RANK, :])

    @pl.when(j == pl.num_programs(1) - 1)
    def _():
        new_t = new_ref[0]
        sn = _dot(q, new_t)
        tok = lax.broadcasted_iota(jnp.int32, sn.shape, 0) >> 3
        key = lax.broadcasted_iota(jnp.int32, sn.shape, 1)
        sn = jnp.where(key <= tok, sn, NEG_BIG)
        update(sn, new_t[:MLA_KV_RANK, :])
        o_ref[0] = (acc_sc[...] / _lane_tile(l_sc[...], MLA_KV_RANK)).astype(BF16)


def _mla_decode(page_table, q_full, new_pad_t, cache_t, layer):
    db, n_pages = page_table.shape
    rows = q_full.shape[1]
    steps = n_pages // PAGES_PER_STEP

    def page_spec(p):
        return pl.BlockSpec((None, None, MLA_LAT, PAGE_SIZE),
                            lambda b, j, pt: (layer, pt[b, j * PAGES_PER_STEP + p], 0, 0))

    grid_spec = pltpu.PrefetchScalarGridSpec(
        num_scalar_prefetch=1,
        grid=(db, steps),
        in_specs=[pl.BlockSpec((1, rows, MLA_LAT), lambda b, j, pt: (b, 0, 0)),
                  pl.BlockSpec((1, MLA_LAT, PAGE_SIZE), lambda b, j, pt: (b, 0, 0))]
        + [page_spec(p) for p in range(PAGES_PER_STEP)],
        out_specs=pl.BlockSpec((1, rows, MLA_KV_RANK), lambda b, j, pt: (b, 0, 0)),
        scratch_shapes=[pltpu.VMEM((rows, LANES), F32), pltpu.VMEM((rows, LANES), F32),
                        pltpu.VMEM((rows, MLA_KV_RANK), F32)],
    )
    return pl.pallas_call(
        _mla_decode_kernel,
        grid_spec=grid_spec,
        out_shape=jax.ShapeDtypeStruct((db, rows, MLA_KV_RANK), BF16),
        compiler_params=_cparams(("parallel", "arbitrary")),
    )(page_table, q_full, new_pad_t, *([cache_t] * PAGES_PER_STEP))


def _rwkv_prep_kernel(rw_ref, before_ref, sh_ref, mu_ref, w0_ref, w2_ref, a0_ref, a2_ref, g2_ref,
                      kk_ref, ka_ref, rk_ref, ones_ref, xs_ref, v_ref, g_ref, rkv_ref, *, tm, seq):
    rw = rw_ref[...]
    rolled = pltpu.roll(rw, 1, axis=0)
    row = lax.broadcasted_iota(jnp.int32, rw.shape, 0)
    if seq >= tm:
        at_start = pl.program_id(0) % (seq // tm) == 0
        first = jnp.where(at_start, sh_ref[...], before_ref[7:8, :])
        prev = jnp.where(row == 0, first, rolled)
    else:
        prev = jnp.where((row & (seq - 1)) == 0, sh_ref[...], rolled)
    xs = rw + (prev - rw) * mu_ref[...]
    d = RWKV_DIM
    r = xs[:, :d]
    k = xs[:, d:2 * d]
    v = xs[:, 2 * d:3 * d]
    xwa = xs[:, 3 * d:3 * d + LANES]
    xg = xs[:, 3 * d + LANES:]
    ones = ones_ref[...]
    w_log = -_softplus(-(w0_ref[...] + _dot(jnp.tanh(xwa), w2_ref[...]))) - 0.5
    a = _sigmoid(a0_ref[...] + _dot(xwa, a2_ref[...]))
    g_ref[...] = _dot(_sigmoid(xg), g2_ref[...])
    kk = k * kk_ref[...]
    ss = _dot_exact_rhs(kk * kk, ones)
    kk = kk / jnp.maximum(jnp.sqrt(ss), 1e-12)
    k2 = k * (1.0 + (a - 1.0) * ka_ref[...])
    xs_ref[0] = -kk
    xs_ref[1] = jnp.exp(-jnp.exp(w_log))
    xs_ref[2] = kk * a
    xs_ref[3] = k2
    xs_ref[4] = r
    v_ref[...] = v
    rkv_ref[...] = _dot_exact_rhs(r * k2 * rk_ref[...], ones) * v


def _rwkv_prep(rw, shift0, w, tm, seq):
    n = rw.shape[0]
    d = RWKV_DIM
    vec = _const_spec((1, d))
    if seq >= tm:
        tiles = seq // tm
        sh = shift0.reshape(-1, 1, RWKV_PROJ)
        sh_spec = pl.BlockSpec((None, 1, RWKV_PROJ), lambda i: (i // tiles, 0, 0))
    else:
        sh = jnp.repeat(shift0, seq, axis=0)
        sh_spec = _row_spec(tm, RWKV_PROJ)
    before_spec = pl.BlockSpec((8, RWKV_PROJ), lambda i: (jnp.maximum(i * (tm // 8) - 1, 0), 0))
    return pl.pallas_call(
        functools.partial(_rwkv_prep_kernel, tm=tm, seq=seq),
        grid=(n // tm,),
        in_specs=[_row_spec(tm, RWKV_PROJ), before_spec, sh_spec, _const_spec((1, RWKV_PROJ)),
                  vec, _const_spec((LANES, d)), vec, _const_spec((LANES, d)),
                  _const_spec((RWKV_G_LORA, d)), vec, vec, vec, _const_spec((d, d))],
        out_specs=[pl.BlockSpec((5, tm, d), lambda i: (0, i, 0))] + [_row_spec(tm, d)] * 3,
        out_shape=[jax.ShapeDtypeStruct((5, n, d), F32)] + [jax.ShapeDtypeStruct((n, d), F32)] * 3,
        compiler_params=_cparams(("parallel",)),
    )(rw, rw, sh, w["mu"], w["w0"], w["w2p"], w["a0"], w["a2p"], w["g2"], w["k_k"], w["k_a"],
      w["r_k"], w["ones_bd"])


SCAN_KH = RWKV_HEAD // 2
SCAN_PAIRS = LANES // 2


def _rwkv_scan_kernel(x_ref, v_ref, s0_ref, y_ref, s_ref, *, tc):
    @pl.when(pl.program_id(1) == 0)
    def _():
        s_ref[...] = s0_ref[...]

    def both_halves(p):
        return p + pltpu.roll(p, SCAN_PAIRS, axis=1)

    def step(t, carry):
        p0 = jnp.zeros((RWKV_HEAD, LANES), F32)
        p1 = jnp.zeros((RWKV_HEAD, LANES), F32)
        for k in range(0, SCAN_KH, 2):
            p0 = p0 + s_ref[0, k] * x_ref[0, 0, t, k:k + 1, :]
            p1 = p1 + s_ref[0, k + 1] * x_ref[0, 0, t, k + 1:k + 2, :]
        sa = both_halves(p0 + p1)
        v = v_ref[0, t]
        y0 = jnp.zeros((RWKV_HEAD, LANES), F32)
        y1 = jnp.zeros((RWKV_HEAD, LANES), F32)
        for k in range(SCAN_KH):
            sn = (s_ref[0, k] * x_ref[1, 0, t, k:k + 1, :] + sa * x_ref[2, 0, t, k:k + 1, :]
                  + v * x_ref[3, 0, t, k:k + 1, :])
            s_ref[0, k] = sn
            if k % 2 == 0:
                y0 = y0 + sn * x_ref[4, 0, t, k:k + 1, :]
            else:
                y1 = y1 + sn * x_ref[4, 0, t, k:k + 1, :]
        y_ref[0, t] = both_halves(y0 + y1)
        return carry

    lax.fori_loop(0, tc, step, 0)


def _rwkv_scan(xs, v, s0, tc):
    _, nb, t, _, _ = xs.shape
    xspec = pl.BlockSpec((5, 1, tc, SCAN_KH, LANES), lambda n, c: (0, n, c, 0, 0))
    vspec = pl.BlockSpec((1, tc, RWKV_HEAD, LANES), lambda n, c: (n, c, 0, 0))
    sspec = pl.BlockSpec((1, SCAN_KH, RWKV_HEAD, LANES), lambda n, c: (n, 0, 0, 0))
    return pl.pallas_call(
        functools.partial(_rwkv_scan_kernel, tc=tc),
        grid=(nb, t // tc),
        in_specs=[xspec, vspec, sspec],
        out_specs=[vspec, sspec],
        out_shape=[jax.ShapeDtypeStruct(v.shape, F32), jax.ShapeDtypeStruct(s0.shape, F32)],
        compiler_params=_cparams(("parallel", "arbitrary")),
    )(xs, v, s0)


def _even_out_kernel(y_ref, rkv_ref, g_ref, lng_ref, lnb_ref, ones_ref, ol_ref, wuv_ref, woa_ref,
                     wob_ref, x_ref, o_ref):
    ones = ones_ref[...]
    y = y_ref[...]
    inv = 1.0 / RWKV_HEAD
    mean = _dot_exact_rhs(y, ones) * inv
    dlt = y - mean
    var = _dot_exact_rhs(dlt * dlt, ones) * inv
    yn = dlt * lax.rsqrt(var + RWKV_LN_EPS) * lng_ref[...] + lnb_ref[...] + rkv_ref[...]
    ob = (yn * g_ref[...]).astype(BF16)
    pair = 2 * MLA_KV_RANK
    oa = jnp.concatenate(
        [_dot(ol_ref[:, p * pair:(p + 1) * pair], wuv_ref[p]) for p in range(MLA_HEADS // 2)], axis=1)
    o_ref[...] = x_ref[...] + _dot(oa, woa_ref[...]) + _dot(ob, wob_ref[...])


def _even_out(y, rkv, g, o_lat, x, w, tm):
    n = x.shape[0]
    d = RWKV_DIM
    hq = MLA_HEADS * MLA_KV_RANK
    return pl.pallas_call(
        _even_out_kernel,
        grid=(n // tm,),
        in_specs=[_row_spec(tm, d), _row_spec(tm, d), _row_spec(tm, d), _const_spec((1, d)),
                  _const_spec((1, d)), _const_spec((d, d)), _row_spec(tm, hq),
                  _const_spec((MLA_HEADS // 2, 2 * MLA_KV_RANK, 2 * MLA_V)),
                  _const_spec((MLA_HEADS * MLA_V, D_MODEL)), _const_spec((d, D_MODEL)),
                  _row_spec(tm, D_MODEL)],
        out_specs=_row_spec(tm, D_MODEL),
        out_shape=jax.ShapeDtypeStruct((n, D_MODEL), F32),
        compiler_params=_cparams(("parallel",)),
    )(y, rkv, g, w["ln_g"], w["ln_b"], w["ones_bd"], o_lat, w["w_uv_bd"], w["w_out_a"],
      w["w_out_b"], x)


FFN_TF = 1408


def _ffn_kernel(x_ref, g_ref, wg_ref, wu_ref, wd_ref, o_ref, xn_sc, acc_sc):
    f = pl.program_id(1)

    @pl.when(f == 0)
    def _():
        xn_sc[...] = _rms(x_ref[...], g_ref[...]).astype(BF16)
        acc_sc[...] = jnp.zeros(acc_sc.shape, F32)

    xn = xn_sc[...]
    gate = _dot(xn, wg_ref[...])
    up = _dot(xn, wu_ref[...])
    acc_sc[...] += _dot(gate * _sigmoid(gate) * up, wd_ref[...])

    @pl.when(f == pl.num_programs(1) - 1)
    def _():
        o_ref[...] = x_ref[...] + acc_sc[...]


def _ffn(x, g, w_gu, w_down, tm):
    n = x.shape[0]
    nf = D_FF // FFN_TF
    return pl.pallas_call(
        _ffn_kernel,
        grid=(n // tm, nf),
        in_specs=[pl.BlockSpec((tm, D_MODEL), lambda i, f: (i, 0)),
                  pl.BlockSpec((1, D_MODEL), lambda i, f: (0, 0)),
                  pl.BlockSpec((D_MODEL, FFN_TF), lambda i, f: (0, f)),
                  pl.BlockSpec((D_MODEL, FFN_TF), lambda i, f: (0, nf + f)),
                  pl.BlockSpec((FFN_TF, D_MODEL), lambda i, f: (f, 0))],
        out_specs=pl.BlockSpec((tm, D_MODEL), lambda i, f: (i, 0)),
        out_shape=jax.ShapeDtypeStruct((n, D_MODEL), F32),
        scratch_shapes=[pltpu.VMEM((tm, D_MODEL), BF16), pltpu.VMEM((tm, D_MODEL), F32)],
        compiler_params=_cparams(("parallel", "arbitrary")),
    )(x, g, w_gu, w_gu, w_down)


def _odd_in_kernel(x_ref, g_ref, wq_ref, wk_ref, wv_ref, wg_ref, wxa_ref, a2_ref, ab_ref,
                   q_ref, k_ref, v_ref, gate_ref, la_ref):
    xn = _rms(x_ref[...], g_ref[...]).astype(BF16)
    q_ref[...] = _dot(xn, wq_ref[...]) * (GLA_DK ** -0.5)
    k_ref[...] = _dot(xn, wk_ref[...])
    v_ref[...] = _dot(xn, wv_ref[...])
    gate_ref[...] = _dot(xn, wg_ref[...])
    z = _dot(_dot(xn, wxa_ref[...]), a2_ref[...]) + ab_ref[...]
    la_ref[...] = -_softplus(-z) * (1.0 / GLA_GATE_NORM)


def _odd_in(x, w, tm):
    n = x.shape[0]
    return pl.pallas_call(
        _odd_in_kernel,
        grid=(n // tm,),
        in_specs=[_row_spec(tm, D_MODEL), _const_spec((1, D_MODEL)),
                  _const_spec((D_MODEL, GLA_KDIM)), _const_spec((D_MODEL, GLA_KDIM)),
                  _const_spec((D_MODEL, GLA_VDIM)), _const_spec((D_MODEL, GLA_VDIM)),
                  _const_spec((D_MODEL, LANES)), _const_spec((LANES, GLA_KDIM)),
                  _const_spec((1, GLA_KDIM))],
        out_specs=[_row_spec(tm, GLA_KDIM), _row_spec(tm, GLA_KDIM), _row_spec(tm, GLA_VDIM),
                   _row_spec(tm, GLA_VDIM), _row_spec(tm, GLA_KDIM)],
        out_shape=[jax.ShapeDtypeStruct((n, GLA_KDIM), F32), jax.ShapeDtypeStruct((n, GLA_KDIM), F32),
                   jax.ShapeDtypeStruct((n, GLA_VDIM), F32), jax.ShapeDtypeStruct((n, GLA_VDIM), F32),
                   jax.ShapeDtypeStruct((n, GLA_KDIM), F32)],
        compiler_params=_cparams(("parallel",)),
    )(x, w["norm_mix"], w["w_q"], w["w_k"], w["w_v"], w["w_g"], w["w_xa"], w["a2p"], w["ab"])


def _gla_kernel(q_ref, k_ref, v_ref, la_ref, s0_ref, o_ref, st_ref):
    c = GLA_CHUNK

    @pl.when(pl.program_id(1) == 0)
    def _():
        st_ref[...] = s0_ref[...]

    row = lax.broadcasted_iota(jnp.int32, (c, c), 0)
    col = lax.broadcasted_iota(jnp.int32, (c, c), 1)
    tri = row >= col
    tri_b = jnp.where(tri, 1.0, 0.0).astype(BF16)
    for h in range(GLA_HEADS):
        ks = slice(h * GLA_DK, (h + 1) * GLA_DK)
        vs = slice(h * GLA_DV, (h + 1) * GLA_DV)
        b = _dot_exact_lhs(tri_b, la_ref[:, ks])
        q = q_ref[:, ks]
        k = k_ref[:, ks]
        v = v_ref[:, vs]
        b_end = b[c - 1:c, :]
        qe = (q * jnp.exp(b)).astype(BF16)
        ke = (k * jnp.exp(-b)).astype(BF16)
        a_mat = jnp.where(tri, _dot_nt(qe, ke), 0.0)
        st = st_ref[0, h]
        o_ref[:, vs] = _dot_nt(qe, st) + _dot(a_mat, v)
        k_end = k * jnp.exp(b_end - b)
        st_ref[0, h] = st * jnp.exp(b_end) + _dot(v.T, k_end)


def _gla(q, k, v, la, s0t, batch, seq):
    nc = seq // GLA_CHUNK
    rspec = lambda width: pl.BlockSpec((GLA_CHUNK, width), lambda b, c: (b * nc + c, 0))
    sspec = pl.BlockSpec((1, GLA_HEADS, GLA_DV, GLA_DK), lambda b, c: (b, 0, 0, 0))
    return pl.pallas_call(
        _gla_kernel,
        grid=(batch, nc),
        in_specs=[rspec(GLA_KDIM), rspec(GLA_KDIM), rspec(GLA_VDIM), rspec(GLA_KDIM), sspec],
        out_specs=[rspec(GLA_VDIM), sspec],
        out_shape=[jax.ShapeDtypeStruct(v.shape, F32), jax.ShapeDtypeStruct(s0t.shape, F32)],
        compiler_params=_cparams(("parallel", "arbitrary")),
    )(q, k, v, la, s0t)


def _odd_out_kernel(o_ref, gate_ref, gn_ref, wo_ref, x_ref, y_ref):
    parts = []
    for h in range(GLA_HEADS):
        vs = slice(h * GLA_DV, (h + 1) * GLA_DV)
        parts.append(_rms(o_ref[:, vs], gn_ref[:, vs]))
    gate = gate_ref[...]
    on = jnp.concatenate(parts, axis=1) * (gate * _sigmoid(gate))
    y_ref[...] = x_ref[...] + _dot(on, wo_ref[...])


def _odd_out(o, gate, x, w, tm):
    n = x.shape[0]
    return pl.pallas_call(
        _odd_out_kernel,
        grid=(n // tm,),
        in_specs=[_row_spec(tm, GLA_VDIM), _row_spec(tm, GLA_VDIM), _const_spec((1, GLA_VDIM)),
                  _const_spec((GLA_VDIM, D_MODEL)), _row_spec(tm, D_MODEL)],
        out_specs=_row_spec(tm, D_MODEL),
        out_shape=jax.ShapeDtypeStruct((n, D_MODEL), F32),
        compiler_params=_cparams(("parallel",)),
    )(o, gate, w["gla_norm"], w["w_out"], x)


def _pack_bf16_pairs(lo, hi):
    lo_bits = pltpu.bitcast(lo.astype(BF16).astype(F32), jnp.uint32)
    hi_bits = pltpu.bitcast(hi.astype(BF16).astype(F32), jnp.uint32)
    return (lo_bits >> 16) | (hi_bits & jnp.uint32(0xFFFF0000))


def _unpack_bf16_pairs(u):
    lo = pltpu.bitcast(u << 16, F32)
    hi = pltpu.bitcast(u & jnp.uint32(0xFFFF0000), F32)
    return lo, hi


def _router_kernel(x_ref, g_ref, wr_ref, xp_ref, idx_ref, gate_ref):
    xn = _rms(x_ref[...], g_ref[...])
    half = D_MODEL // 2
    xp_ref[...] = _pack_bf16_pairs(xn[:, :half], xn[:, half:])
    logits = _dot_f32ish(xn, wr_ref[...])
    lane = lax.broadcasted_iota(jnp.int32, logits.shape, 1)
    logits = jnp.where(lane < N_EXPERTS, logits, NEG_BIG)
    m1 = jnp.max(logits, axis=-1, keepdims=True)
    i1 = jnp.min(jnp.where(logits == m1, lane, LANES), axis=-1, keepdims=True)
    rest = jnp.where(lane == i1, NEG_BIG, logits)
    m2 = jnp.max(rest, axis=-1, keepdims=True)
    i2 = jnp.min(jnp.where(rest == m2, lane, LANES), axis=-1, keepdims=True)
    e2 = jnp.exp(m2 - m1)
    g1 = 1.0 / (1.0 + e2)
    g2 = e2 / (1.0 + e2)
    idx_ref[...] = jnp.where(lane == 0, i1, jnp.where(lane == 1, i2, 0))
    gate_ref[...] = jnp.where(lane == 0, g1, jnp.where(lane == 1, g2, 0.0))


def _router(x, g, wr, tm):
    n = x.shape[0]
    return pl.pallas_call(
        _router_kernel,
        grid=(n // tm,),
        in_specs=[_row_spec(tm, D_MODEL), _const_spec((1, D_MODEL)), _const_spec((D_MODEL, LANES))],
        out_specs=[_row_spec(tm, D_MODEL // 2), _row_spec(tm, LANES), _row_spec(tm, LANES)],
        out_shape=[jax.ShapeDtypeStruct((n, D_MODEL // 2), jnp.uint32),
                   jax.ShapeDtypeStruct((n, LANES), jnp.int32), jax.ShapeDtypeStruct((n, LANES), F32)],
        compiler_params=_cparams(("parallel",)),
    )(x, g, wr)


MOE_TF = 1792
MOE_CHUNK = 2 * LANES


def _route(top_i, tm):
    n = top_i.shape[0]
    slots = 2 * n
    n_tiles = -(-(slots + N_EXPERTS * (tm - 1)) // tm)
    e_flat = top_i.reshape(-1)
    onehot = (e_flat[:, None] == jnp.arange(N_EXPERTS, dtype=jnp.int32)[None, :]).astype(jnp.int32)
    csum = jnp.cumsum(onehot, axis=0)
    rank = jnp.sum(onehot * csum, axis=1) - 1
    counts = csum[-1]
    padded = ((counts + tm - 1) // tm) * tm
    ends = jnp.cumsum(padded)
    starts = ends - padded
    dest = (jnp.sum(onehot * starts[None, :], axis=1) + rank).astype(jnp.int32)
    tile_start = jnp.arange(n_tiles, dtype=jnp.int32) * tm
    tile_expert = jnp.minimum(jnp.sum((tile_start[:, None] >= ends[None, :]).astype(jnp.int32), axis=1),
                              N_EXPERTS - 1).astype(jnp.int32)
    tile_valid = (tile_start < ends[-1]).astype(jnp.int32)
    src = jnp.zeros((n_tiles * tm,), jnp.int32).at[dest].set(jnp.arange(slots, dtype=jnp.int32) // 2)
    return dest, src, tile_expert, tile_valid


def _moe_gather_kernel(src_ref, x_ref, o_ref, *, tg):
    base = pl.program_id(0) * tg

    def body(r, carry):
        o_ref[pl.ds(r, 1), :] = x_ref[pl.ds(src_ref[base + r], 1), :]
        return carry

    lax.fori_loop(0, tg, body, 0, unroll=8)


def _moe_gather(src, xp, tg):
    rows = src.shape[0]
    width = xp.shape[1]
    grid_spec = pltpu.PrefetchScalarGridSpec(
        num_scalar_prefetch=1,
        grid=(rows // tg,),
        in_specs=[pl.BlockSpec(memory_space=pltpu.VMEM)],
        out_specs=pl.BlockSpec((tg, width), lambda i, s: (i, 0)),
    )
    return pl.pallas_call(
        functools.partial(_moe_gather_kernel, tg=tg),
        grid_spec=grid_spec,
        out_shape=jax.ShapeDtypeStruct((rows, width), jnp.uint32),
        compiler_params=_cparams(("arbitrary",)),
    )(src, xp)


def _moe_up_kernel(te_ref, tv_ref, xs_ref, wg_ref, wu_ref, h_ref):
    @pl.when(tv_ref[pl.program_id(1)] != 0)
    def _():
        lo, hi = _unpack_bf16_pairs(xs_ref[...])
        half = D_MODEL // 2
        gate = _dot(lo, wg_ref[:half, :]) + _dot(hi, wg_ref[half:, :])
        up = _dot(lo, wu_ref[:half, :]) + _dot(hi, wu_ref[half:, :])
        h_ref[...] = (gate * _sigmoid(gate) * up).astype(BF16)


def _moe_up(te, tv, xs, w_gu, tm):
    rows = xs.shape[0]
    nf = D_FF_EXPERT // MOE_TF
    grid_spec = pltpu.PrefetchScalarGridSpec(
        num_scalar_prefetch=2,
        grid=(nf, rows // tm),
        in_specs=[pl.BlockSpec((tm, D_MODEL // 2), lambda f, t, te, tv: (t, 0)),
                  pl.BlockSpec((None, D_MODEL, MOE_TF), lambda f, t, te, tv: (te[t], 0, f)),
                  pl.BlockSpec((None, D_MODEL, MOE_TF), lambda f, t, te, tv: (te[t], 0, nf + f))],
        out_specs=pl.BlockSpec((tm, MOE_TF), lambda f, t, te, tv: (t, f)),
    )
    return pl.pallas_call(
        _moe_up_kernel,
        grid_spec=grid_spec,
        out_shape=jax.ShapeDtypeStruct((rows, D_FF_EXPERT), BF16),
        compiler_params=_cparams(("arbitrary", "arbitrary")),
    )(te, tv, xs, w_gu, w_gu)


def _moe_down_kernel(te_ref, tv_ref, h_ref, wd_ref, y_ref):
    @pl.when(tv_ref[pl.program_id(0)] != 0)
    def _():
        y = _dot(h_ref[...], wd_ref[...])
        parts = []
        for c in range(D_MODEL // MOE_CHUNK):
            lo = y[:, c * MOE_CHUNK:c * MOE_CHUNK + LANES]
            hi = y[:, c * MOE_CHUNK + LANES:(c + 1) * MOE_CHUNK]
            parts.append(_pack_bf16_pairs(lo, hi))
        y_ref[...] = jnp.concatenate(parts, axis=1)


def _moe_down(te, tv, h, w_down, tm):
    rows = h.shape[0]
    grid_spec = pltpu.PrefetchScalarGridSpec(
        num_scalar_prefetch=2,
        grid=(rows // tm,),
        in_specs=[pl.BlockSpec((tm, D_FF_EXPERT), lambda t, te, tv: (t, 0)),
                  pl.BlockSpec((None, D_FF_EXPERT, D_MODEL), lambda t, te, tv: (te[t], 0, 0))],
        out_specs=pl.BlockSpec((tm, D_MODEL // 2), lambda t, te, tv: (t, 0)),
    )
    return pl.pallas_call(
        _moe_down_kernel,
        grid_spec=grid_spec,
        out_shape=jax.ShapeDtypeStruct((rows, D_MODEL // 2), jnp.uint32),
        compiler_params=_cparams(("arbitrary",)),
    )(te, tv, h, w_down)


def _moe_combine_kernel(dest_ref, ys_ref, g1_ref, g2_ref, x_ref, o_ref, *, tmc):
    base = 2 * pl.program_id(1) * tmc

    def body(r, carry):
        row = pl.ds(r, 1)
        lo1, hi1 = _unpack_bf16_pairs(ys_ref[pl.ds(dest_ref[base + 2 * r], 1), :])
        lo2, hi2 = _unpack_bf16_pairs(ys_ref[pl.ds(dest_ref[base + 2 * r + 1], 1), :])
        g1 = g1_ref[row, :]
        g2 = g2_ref[row, :]
        moe = jnp.concatenate([g1 * lo1 + g2 * lo2, g1 * hi1 + g2 * hi2], axis=1)
        o_ref[row, :] = x_ref[row, :] + moe
        return carry

    lax.fori_loop(0, tmc, body, 0, unroll=8)


def _moe_combine(dest, ys, g1b, g2b, x, tmc):
    n = x.shape[0]
    rows = ys.shape[0]
    grid_spec = pltpu.PrefetchScalarGridSpec(
        num_scalar_prefetch=1,
        grid=(D_MODEL // MOE_CHUNK, n // tmc),
        in_specs=[pl.BlockSpec((rows, LANES), lambda c, i, d: (0, c)),
                  pl.BlockSpec((tmc, LANES), lambda c, i, d: (i, 0)),
                  pl.BlockSpec((tmc, LANES), lambda c, i, d: (i, 0)),
                  pl.BlockSpec((tmc, MOE_CHUNK), lambda c, i, d: (i, c))],
        out_specs=pl.BlockSpec((tmc, MOE_CHUNK), lambda c, i, d: (i, c)),
    )
    return pl.pallas_call(
        functools.partial(_moe_combine_kernel, tmc=tmc),
        grid_spec=grid_spec,
        out_shape=jax.ShapeDtypeStruct((n, D_MODEL), F32),
        compiler_params=_cparams(("arbitrary", "arbitrary")),
    )(dest, ys, g1b, g2b, x)


def _final_norm_kernel(x_ref, g_ref, o_ref):
    o_ref[...] = _rms(x_ref[...], g_ref[...])


def _final_norm(x, g, tm):
    n = x.shape[0]
    return pl.pallas_call(
        _final_norm_kernel,
        grid=(n // tm,),
        in_specs=[_row_spec(tm, D_MODEL), _const_spec((1, D_MODEL))],
        out_specs=_row_spec(tm, D_MODEL),
        out_shape=jax.ShapeDtypeStruct((n, D_MODEL), F32),
        compiler_params=_cparams(("parallel",)),
    )(x, g)


def _scan_vec_layout(xs, batch, seq):
    nb = batch * RWKV_HEADS // SCAN_PAIRS
    x = xs.reshape(5, batch, seq, RWKV_HEADS, 2, SCAN_KH).transpose(0, 2, 5, 4, 1, 3)
    x = x.reshape(5, seq, SCAN_KH, 2, nb, SCAN_PAIRS).transpose(0, 4, 1, 2, 3, 5)
    return x.reshape(5, nb, seq, SCAN_KH, LANES)


def _scan_val_layout(v, batch, seq):
    nb = batch * RWKV_HEADS // SCAN_PAIRS
    v4 = v.reshape(batch, seq, RWKV_HEADS, RWKV_HEAD).transpose(1, 3, 0, 2)
    v4 = v4.reshape(seq, RWKV_HEAD, nb, 1, SCAN_PAIRS)
    v4 = jnp.broadcast_to(v4, (seq, RWKV_HEAD, nb, 2, SCAN_PAIRS)).transpose(2, 0, 1, 3, 4)
    return v4.reshape(nb, seq, RWKV_HEAD, LANES)


def _scan_val_unlayout(y, batch, seq):
    nb = y.shape[0]
    v4 = y[..., :SCAN_PAIRS].transpose(1, 2, 0, 3).reshape(seq, RWKV_HEAD, batch, RWKV_HEADS)
    return v4.transpose(2, 0, 3, 1).reshape(batch * seq, RWKV_DIM)


def _scan_state_layout(s, batch):
    nb = batch * RWKV_HEADS // SCAN_PAIRS
    s6 = s.reshape(batch, RWKV_HEADS, RWKV_HEAD, 2, SCAN_KH).transpose(4, 2, 3, 0, 1)
    s6 = s6.reshape(SCAN_KH, RWKV_HEAD, 2, nb, SCAN_PAIRS).transpose(3, 0, 1, 2, 4)
    return s6.reshape(nb, SCAN_KH, RWKV_HEAD, LANES)


def _scan_state_unlayout(arr, batch):
    nb = arr.shape[0]
    s = arr.reshape(nb, SCAN_KH, RWKV_HEAD, 2, SCAN_PAIRS).transpose(0, 4, 2, 3, 1)
    return s.reshape(batch, RWKV_HEADS, RWKV_HEAD, RWKV_HEAD)


def _swap_halves(w):
    half = w.shape[-1] // 2
    return jnp.concatenate([w[..., half:], w[..., :half]], axis=-1)


def _prep_even(i, norm_mix, norm_ffn, w_in, q_norm, kv_norm, w_uq, w_uk, w_uv, mu, w0, w2, a0, a2,
               g2, k_k, k_a, r_k, ln_g, ln_b, w_out, ffn_gu, ffn_down):
    w = {}
    row = lambda v: v[i].reshape(1, -1)
    w_in = w_in[i]
    w["norm_mix"] = row(norm_mix)
    w["norm_ffn"] = row(norm_ffn)
    w["w_q"] = w_in[:, :MLA_Q_RANK].astype(BF16)
    w_kv = w_in[:, MLA_Q_RANK:MLA_Q_RANK + MLA_LAT]
    w["w_ckv"] = w_kv[:, :MLA_KV_RANK].astype(BF16)
    lane_pad = lambda m: jnp.pad(m, [(0, 0)] * (m.ndim - 1) + [(0, LANES - m.shape[-1])])
    w["w_pe_a"] = lane_pad(w_kv[:, MLA_KV_RANK:]).astype(BF16)
    w["w_pe_b"] = lane_pad(_swap_halves(w_kv[:, MLA_KV_RANK:])).astype(BF16)
    w["w_rw"] = w_in[:, MLA_Q_RANK + MLA_LAT:].astype(BF16)
    w["q_norm"] = row(q_norm)
    w["kv_norm"] = row(kv_norm)
    uq = w_uq[i].reshape(MLA_Q_RANK, MLA_HEADS, MLA_NOPE + MLA_ROPE)
    uq_pe = uq[:, :, MLA_NOPE:]
    w["w_qpe_a"] = lane_pad(uq_pe).reshape(MLA_Q_RANK, -1).astype(BF16)
    w["w_qpe_b"] = lane_pad(_swap_halves(uq_pe)).reshape(MLA_Q_RANK, -1).astype(BF16)
    w["w_qlat"] = _fold_qlat(uq[:, :, :MLA_NOPE].transpose(1, 0, 2), w_uk[i].transpose(1, 0, 2))
    uv = w_uv[i].transpose(1, 0, 2).reshape(MLA_HEADS // 2, 2, MLA_KV_RANK, MLA_V)
    zero = jnp.zeros_like(uv[:, 0])
    w["w_uv_bd"] = jnp.concatenate(
        [jnp.concatenate([uv[:, 0], zero], axis=-1), jnp.concatenate([zero, uv[:, 1]], axis=-1)],
        axis=1).astype(BF16)
    w["mu"] = row(mu)
    w["w0"] = row(w0)
    pad = lambda m, before: jnp.pad(m, ((before, LANES - before - m.shape[0]), (0, 0))).astype(BF16)
    w["w2p"] = pad(w2[i], 0)
    w["a2p"] = pad(a2[i], RWKV_W_LORA)
    w["a0"] = row(a0)
    w["g2"] = g2[i].astype(BF16)
    w["k_k"] = row(k_k)
    w["k_a"] = row(k_a)
    w["r_k"] = row(r_k)
    w["ln_g"] = row(ln_g)
    w["ln_b"] = row(ln_b)
    head = jnp.arange(RWKV_DIM) // RWKV_HEAD
    w["ones_bd"] = (head[:, None] == head[None, :]).astype(BF16)
    w["w_out_a"] = w_out[i][:MLA_HEADS * MLA_V].astype(BF16)
    w["w_out_b"] = w_out[i][MLA_HEADS * MLA_V:].astype(BF16)
    w["ffn_gu"] = ffn_gu[i].astype(BF16)
    w["ffn_down"] = ffn_down[i].astype(BF16)
    return w


def _prep_odd(i, norm_mix, norm_ffn, w_in, a2, ab, gla_norm, w_out, router, moe_gu, moe_down):
    w = {}
    row = lambda v: v[i].reshape(1, -1)
    w_in = w_in[i]
    w["norm_mix"] = row(norm_mix)
    w["norm_ffn"] = row(norm_ffn)
    w["w_q"] = w_in[:, :GLA_KDIM].astype(BF16)
    w["w_k"] = w_in[:, GLA_KDIM:2 * GLA_KDIM].astype(BF16)
    w["w_v"] = w_in[:, 2 * GLA_KDIM:2 * GLA_KDIM + GLA_VDIM].astype(BF16)
    w["w_g"] = w_in[:, 2 * GLA_KDIM + GLA_VDIM:2 * GLA_KDIM + 2 * GLA_VDIM].astype(BF16)
    w["w_xa"] = jnp.pad(w_in[:, 2 * GLA_KDIM + 2 * GLA_VDIM:],
                        ((0, 0), (0, LANES - GLA_GATE_RANK))).astype(BF16)
    w["a2p"] = jnp.pad(a2[i], ((0, LANES - GLA_GATE_RANK), (0, 0))).astype(BF16)
    w["ab"] = row(ab)
    w["gla_norm"] = row(gla_norm)
    w["w_out"] = w_out[i].astype(BF16)
    w["router"] = jnp.pad(router[i], ((0, 0), (0, LANES - N_EXPERTS)))
    w["moe_gu"] = moe_gu[i].astype(BF16)
    w["moe_down"] = moe_down[i].astype(BF16)
    return w


def _rope_tables(pos, reps):
    inv = ROPE_THETA ** (-jnp.arange(0, MLA_ROPE, 2, dtype=F32) / MLA_ROPE)
    ang = pos.astype(F32)[:, None] * inv[None, :]
    cos, sin = jnp.cos(ang), jnp.sin(ang)
    pad = ((0, 0), (0, LANES - MLA_ROPE))
    cs = jnp.tile(jnp.pad(jnp.concatenate([cos, cos], axis=-1), pad), (reps, 1))
    sn = jnp.tile(jnp.pad(jnp.concatenate([-sin, sin], axis=-1), pad), (reps, 1))
    return {"cs": cs, "sn": sn, "cs8": jnp.tile(cs, (1, MLA_HEADS)), "sn8": jnp.tile(sn, (1, MLA_HEADS))}


def _even_layer(x, batch, seq, tabs, state, shift0, past, w, tm, tc):
    n = batch * seq
    lat, lat_b, q_lat, q_pe, rw = _even_in(x, w, tabs, tm)
    if past is None:
        o_lat = _mla_prompt(q_lat, q_pe, lat_b, batch, seq)
    else:
        cache, layer, page_table = past
        rows = seq * MLA_HEADS
        q_full = jnp.concatenate([q_lat.reshape(batch, rows, MLA_KV_RANK),
                                  q_pe.reshape(batch, rows, LANES)[:, :, :MLA_ROPE]], axis=-1)
        new_pad_t = jnp.pad(lat_b.reshape(batch, seq, MLA_LATB)[:, :, :MLA_LAT],
                            ((0, 0), (0, PAGE_SIZE - seq), (0, 0))).transpose(0, 2, 1)
        o_lat = _mla_decode(page_table, q_full, new_pad_t, cache.transpose(0, 1, 3, 2), layer)
        o_lat = o_lat.reshape(n, MLA_HEADS * MLA_KV_RANK)

    rw3 = rw.reshape(batch, seq, RWKV_PROJ)
    xs5, v, g, rkv = _rwkv_prep(rw, shift0, w, tm, seq)
    y_l, s_l = _rwkv_scan(_scan_vec_layout(xs5, batch, seq), _scan_val_layout(v, batch, seq),
                          _scan_state_layout(state, batch), tc)
    y = _scan_val_unlayout(y_l, batch, seq)
    new_state = _scan_state_unlayout(s_l, batch)

    x = _even_out(y, rkv, g, o_lat, x, w, tm)
    x = _ffn(x, w["norm_ffn"], w["ffn_gu"], w["ffn_down"], tm)
    return x, lat.reshape(batch, seq, MLA_LAT), new_state, rw3[:, -1]


def _odd_layer(x, batch, seq, state, w, final_norm, tm, tm_moe):
    q, k, v, gate, la = _odd_in(x, w, tm)
    seq_p = -(-seq // GLA_CHUNK) * GLA_CHUNK
    if seq_p != seq:
        padr = lambda t: jnp.pad(t.reshape(batch, seq, -1), ((0, 0), (0, seq_p - seq), (0, 0))
                                 ).reshape(batch * seq_p, -1)
        qp, kp, vp, lap = padr(q), padr(k), padr(v), padr(la)
    else:
        qp, kp, vp, lap = q, k, v, la
    o, st = _gla(qp, kp, vp, lap, state.transpose(0, 1, 3, 2), batch, seq_p)
    if seq_p != seq:
        o = o.reshape(batch, seq_p, GLA_VDIM)[:, :seq].reshape(batch * seq, GLA_VDIM)
    x = _odd_out(o, gate, x, w, tm)
    xp, idx, gates = _router(x, w["norm_ffn"], w["router"], tm)
    dest, src, tile_expert, tile_valid = _route(idx[:, :2], tm_moe)
    xs = _moe_gather(src, xp, tm_moe)
    h = _moe_up(tile_expert, tile_valid, xs, w["moe_gu"], tm_moe)
    ys = _moe_down(tile_expert, tile_valid, h, w["moe_down"], tm_moe)
    g1b = jnp.broadcast_to(gates[:, 0:1], (x.shape[0], LANES))
    g2b = jnp.broadcast_to(gates[:, 1:2], (x.shape[0], LANES))
    z = _moe_combine(dest, ys, g1b, g2b, x, tm)
    y = _final_norm(z, final_norm, tm)
    return y, st.transpose(0, 1, 3, 2)


def kernel(x_prompt, x_sample, cache_mla, state_rwkv, state_rwkv_shift, state_gla, page_table, norm_mix_even, norm_ffn_even, w_in_even, mla_q_norm, mla_kv_norm, mla_w_uq, mla_w_uk, mla_w_uv, rwkv_mu, rwkv_w0, rwkv_w2, rwkv_a0, rwkv_a2, rwkv_g2, rwkv_k_k, rwkv_k_a, rwkv_r_k, rwkv_ln_g, rwkv_ln_b, w_out_even, ffn_w_gu_even, ffn_w_down_even, norm_mix_odd, norm_ffn_odd, w_in_odd, gla_a2, gla_ab, gla_norm, w_out_odd, moe_router, moe_w_gu, moe_w_down, final_norm):
    bp, tp, _ = x_prompt.shape
    bs, ts, _ = x_sample.shape
    past_len = page_table.shape[1] * PAGE_SIZE
    tm_p, tm_s = 512, bs * ts
    we = _prep_even(0, norm_mix_even, norm_ffn_even, w_in_even, mla_q_norm, mla_kv_norm, mla_w_uq,
                    mla_w_uk, mla_w_uv, rwkv_mu, rwkv_w0, rwkv_w2, rwkv_a0, rwkv_a2, rwkv_g2,
                    rwkv_k_k, rwkv_k_a, rwkv_r_k, rwkv_ln_g, rwkv_ln_b, w_out_even, ffn_w_gu_even,
                    ffn_w_down_even)
    wo = _prep_odd(0, norm_mix_odd, norm_ffn_odd, w_in_odd, gla_a2, gla_ab, gla_norm, w_out_odd,
                   moe_router, moe_w_gu, moe_w_down)
    fn = final_norm.reshape(1, -1)
    tabs_p = _rope_tables(jnp.arange(tp), 1)
    tabs_s = _rope_tables(past_len + jnp.arange(ts), bs)

    hp = x_prompt.reshape(bp * tp, D_MODEL)
    hs = x_sample.reshape(bs * ts, D_MODEL)
    zeros_state = jnp.zeros((bp, RWKV_HEADS, RWKV_HEAD, RWKV_HEAD), F32)
    zeros_shift = jnp.zeros((bp, RWKV_PROJ), F32)
    hp, lat_p, rs_p, sh_p = _even_layer(hp, bp, tp, tabs_p, zeros_state, zeros_shift, None, we,
                                        tm_p, 32)
    hs, lat_s, rs_s, sh_s = _even_layer(hs, bs, ts, tabs_s, state_rwkv[0], state_rwkv_shift[0],
                                        (cache_mla, 0, page_table), we, tm_s, ts)
    zeros_gla = jnp.zeros((bp, GLA_HEADS, GLA_DK, GLA_DV), F32)
    yp, gs_p = _odd_layer(hp, bp, tp, zeros_gla, wo, fn, tm_p, 512)
    ys, gs_s = _odd_layer(hs, bs, ts, state_gla[0], wo, fn, tm_s, 128)
    return (yp.reshape(bp, tp, D_MODEL), ys.reshape(bs, ts, D_MODEL), lat_p[None], lat_s[None],
            rs_p[None], rs_s[None], sh_p[None], sh_s[None], gs_p[None], gs_s[None])
```

```python
import functools

import jax
import jax.numpy as jnp
from jax import lax
from jax.experimental import pallas as pl
from jax.experimental.pallas import tpu as pltpu

F32 = jnp.float32
BF16 = jnp.bfloat16

D_MODEL = 1024
PAGE_SIZE = 128
NORM_EPS = 1e-6

MLA_HEADS = 8
MLA_NOPE = 64
MLA_ROPE = 32
MLA_V = 64
MLA_Q_RANK = 384
MLA_KV_RANK = 256
MLA_LAT = MLA_KV_RANK + MLA_ROPE
MLA_LATB = MLA_KV_RANK + 128
MLA_SCALE = (MLA_NOPE + MLA_ROPE) ** -0.5
ROPE_THETA = 10000.0

RWKV_HEADS = 8
RWKV_HEAD = 64
RWKV_DIM = RWKV_HEADS * RWKV_HEAD
RWKV_W_LORA = 64
RWKV_A_LORA = 64
RWKV_G_LORA = 128
RWKV_PROJ = 3 * RWKV_DIM + RWKV_W_LORA + RWKV_A_LORA + RWKV_G_LORA
RWKV_LN_EPS = 64e-5

GLA_HEADS = 4
GLA_DK = 128
GLA_DV = 256
GLA_KDIM = GLA_HEADS * GLA_DK
GLA_VDIM = GLA_HEADS * GLA_DV
GLA_GATE_RANK = 16
GLA_GATE_NORM = 16.0
GLA_CHUNK = 128

D_FF = 2816
N_EXPERTS = 8
D_FF_EXPERT = 3584

LANES = 128
VMEM_LIMIT = 56 * 1024 * 1024
NEG_BIG = -1e30
LOG2_E = 1.4426950408889634
Q_PRESCALE = MLA_SCALE * LOG2_E


def _cparams(sem):
    return pltpu.CompilerParams(dimension_semantics=sem, vmem_limit_bytes=VMEM_LIMIT)


def _const_spec(shape):
    nd = len(shape)
    return pl.BlockSpec(shape, lambda *_: (0,) * nd)


def _row_spec(tm, width):
    return pl.BlockSpec((tm, width), lambda i: (i, 0))


def _dot(a, b):
    return jnp.dot(a.astype(BF16), b.astype(BF16), preferred_element_type=F32)


def _dot_nt(a, b):
    return lax.dot_general(a.astype(BF16), b.astype(BF16), (((1,), (1,)), ((), ())),
                           preferred_element_type=F32)


def _split2(x):
    hi = x.astype(BF16)
    lo = (x - hi.astype(F32)).astype(BF16)
    return hi, lo


def _split3(x):
    hi = x.astype(BF16)
    r1 = x - hi.astype(F32)
    mid = r1.astype(BF16)
    lo = (r1 - mid.astype(F32)).astype(BF16)
    return hi, mid, lo


def _dot_exact_rhs(x, e):
    hi, mid, lo = _split3(x)
    return (jnp.dot(hi, e, preferred_element_type=F32) + jnp.dot(mid, e, preferred_element_type=F32)
            + jnp.dot(lo, e, preferred_element_type=F32))


def _dot_exact_lhs(e, x):
    hi, mid, lo = _split3(x)
    return (jnp.dot(e, hi, preferred_element_type=F32) + jnp.dot(e, mid, preferred_element_type=F32)
            + jnp.dot(e, lo, preferred_element_type=F32))


def _dot_f32ish(a, b):
    ah, al = _split2(a)
    bh, bl = _split2(b)
    return (jnp.dot(ah, bh, preferred_element_type=F32) + jnp.dot(ah, bl, preferred_element_type=F32)
            + jnp.dot(al, bh, preferred_element_type=F32))


def _lane_tile(x, width):
    return x if width == LANES else jnp.concatenate([x] * (width // LANES), axis=1)


def _rms(x, g, eps=NORM_EPS):
    return x * lax.rsqrt(jnp.mean(x * x, axis=-1, keepdims=True) + eps) * g


def _sigmoid(x):
    return 1.0 / (1.0 + jnp.exp(-x))


def _softplus(x):
    return jnp.maximum(x, 0.0) + jnp.log(1.0 + jnp.exp(-jnp.abs(x)))


def _fold_qlat_kernel(uq_ref, uk_ref, o_ref):
    a = uq_ref[...]
    b = uk_ref[...]
    ah, al = _split2(a)
    bh, bl = _split2(b)
    dn = (((1,), (1,)), ((), ()))
    o = (lax.dot_general(ah, bh, dn, preferred_element_type=F32)
         + lax.dot_general(ah, bl, dn, preferred_element_type=F32)
         + lax.dot_general(al, bh, dn, preferred_element_type=F32))
    o_ref[...] = o.astype(BF16)


def _fold_qlat(uq_nope, uk):
    return pl.pallas_call(
        _fold_qlat_kernel,
        grid=(MLA_HEADS,),
        in_specs=[pl.BlockSpec((None, MLA_Q_RANK, MLA_NOPE), lambda h: (h, 0, 0)),
                  pl.BlockSpec((None, MLA_KV_RANK, MLA_NOPE), lambda h: (h, 0, 0))],
        out_specs=pl.BlockSpec((MLA_Q_RANK, MLA_KV_RANK), lambda h: (0, h)),
        out_shape=jax.ShapeDtypeStruct((MLA_Q_RANK, MLA_HEADS * MLA_KV_RANK), BF16),
        compiler_params=_cparams(("arbitrary",)),
    )(uq_nope, uk)


def _even_in_kernel(x_ref, g_ref, wq_ref, wckv_ref, wpa_ref, wpb_ref, wrw_ref, qn_ref, kvn_ref,
                    cs_ref, sn_ref, wql_ref, wqa_ref, wqb_ref, cs8_ref, sn8_ref,
                    lat_ref, latb_ref, ql_ref, qpe_ref, rw_ref):
    xn = _rms(x_ref[...], g_ref[...]).astype(BF16)
    cq = _rms(_dot(xn, wq_ref[...]), qn_ref[...]).astype(BF16)
    ql_ref[...] = (_dot(cq, wql_ref[...]) * Q_PRESCALE).astype(BF16)
    qpe = _dot(cq, wqa_ref[...]) * cs8_ref[...] + _dot(cq, wqb_ref[...]) * sn8_ref[...]
    qpe_ref[...] = (qpe * Q_PRESCALE).astype(BF16)
    ckv = _rms(_dot(xn, wckv_ref[...]), kvn_ref[...])
    kpe = _dot(xn, wpa_ref[...]) * cs_ref[...] + _dot(xn, wpb_ref[...]) * sn_ref[...]
    lat_ref[:, :MLA_KV_RANK] = ckv
    lat_ref[:, MLA_KV_RANK:] = kpe[:, :MLA_ROPE]
    latb_ref[:, :MLA_KV_RANK] = ckv.astype(BF16)
    latb_ref[:, MLA_KV_RANK:] = kpe.astype(BF16)
    rw_ref[...] = _dot(xn, wrw_ref[...])


def _even_in(x, w, tabs, tm):
    n = x.shape[0]
    nt = tabs["cs"].shape[0] // tm
    tab = lambda width: pl.BlockSpec((tm, width), lambda i: (i % nt, 0))
    hq = MLA_HEADS * MLA_KV_RANK
    hr = MLA_HEADS * LANES
    return pl.pallas_call(
        _even_in_kernel,
        grid=(n // tm,),
        in_specs=[_row_spec(tm, D_MODEL), _const_spec((1, D_MODEL)),
                  _const_spec((D_MODEL, MLA_Q_RANK)), _const_spec((D_MODEL, MLA_KV_RANK)),
                  _const_spec((D_MODEL, LANES)), _const_spec((D_MODEL, LANES)),
                  _const_spec((D_MODEL, RWKV_PROJ)), _const_spec((1, MLA_Q_RANK)),
                  _const_spec((1, MLA_KV_RANK)), tab(LANES), tab(LANES),
                  _const_spec((MLA_Q_RANK, hq)), _const_spec((MLA_Q_RANK, hr)),
                  _const_spec((MLA_Q_RANK, hr)), tab(hr), tab(hr)],
        out_specs=[_row_spec(tm, MLA_LAT), _row_spec(tm, MLA_LATB), _row_spec(tm, hq),
                   _row_spec(tm, hr), _row_spec(tm, RWKV_PROJ)],
        out_shape=[jax.ShapeDtypeStruct((n, MLA_LAT), F32), jax.ShapeDtypeStruct((n, MLA_LATB), BF16),
                   jax.ShapeDtypeStruct((n, hq), BF16), jax.ShapeDtypeStruct((n, hr), BF16),
                   jax.ShapeDtypeStruct((n, RWKV_PROJ), F32)],
        compiler_params=_cparams(("parallel",)),
    )(x, w["norm_mix"], w["w_q"], w["w_ckv"], w["w_pe_a"], w["w_pe_b"], w["w_rw"], w["q_norm"],
      w["kv_norm"], tabs["cs"], tabs["sn"], w["w_qlat"], w["w_qpe_a"], w["w_qpe_b"],
      tabs["cs8"], tabs["sn8"])


ATT_TQ = 256


def _mla_prompt_kernel(qi_ref, kj_ref, ql_ref, qpe_ref, lat_ref, o_ref,
                       m_sc, l_sc, a_sc, acc_sc, s_sc, p_sc):
    step = pl.program_id(1)
    i = qi_ref[step]
    j = kj_ref[step]
    heads = range(MLA_HEADS)

    @pl.when(j == 0)
    def _():
        m_sc[...] = jnp.full(m_sc.shape, NEG_BIG, F32)
        l_sc[...] = jnp.zeros(l_sc.shape, F32)
        acc_sc[...] = jnp.zeros(acc_sc.shape, F32)

    def tile(masked):
        ckv = lat_ref[:, :MLA_KV_RANK]
        kpe = lat_ref[:, MLA_KV_RANK:]
        for h in heads:
            s_sc[h] = (_dot_nt(ql_ref[:, h * MLA_KV_RANK:(h + 1) * MLA_KV_RANK], ckv)
                       + _dot_nt(qpe_ref[:, h * LANES:(h + 1) * LANES], kpe))
        for h in heads:
            s = s_sc[h]
            if masked:
                tok = lax.broadcasted_iota(jnp.int32, s.shape, 0)
                key = lax.broadcasted_iota(jnp.int32, s.shape, 1)
                s = jnp.where(key <= tok, s, NEG_BIG)
            m_prev = m_sc[h]
            m_new = jnp.maximum(m_prev, jnp.max(s, axis=-1, keepdims=True))
            alpha = jnp.exp2(m_prev - m_new)
            p = jnp.exp2(s - _lane_tile(m_new, ATT_TQ))
            l_sc[h] = alpha * l_sc[h] + jnp.sum(p, axis=-1, keepdims=True)
            m_sc[h] = m_new
            a_sc[h] = alpha
            p_sc[h] = p.astype(BF16)
        for h in heads:
            acc_sc[h] = _lane_tile(a_sc[h], MLA_KV_RANK) * acc_sc[h] + _dot(p_sc[h], ckv)

    @pl.when(j < i)
    def _():
        tile(False)

    @pl.when(j == i)
    def _():
        tile(True)
        for h in heads:
            o_ref[:, h * MLA_KV_RANK:(h + 1) * MLA_KV_RANK] = (
                acc_sc[h] / _lane_tile(l_sc[h], MLA_KV_RANK)).astype(BF16)


def _mla_prompt(q_lat, q_pe, lat_b, batch, seq):
    nq = seq // ATT_TQ
    pairs = [(i, j) for i in range(nq) for j in range(i + 1)]
    qi = jnp.array([p[0] for p in pairs], jnp.int32)
    kj = jnp.array([p[1] for p in pairs], jnp.int32)
    hq = MLA_HEADS * MLA_KV_RANK
    grid_spec = pltpu.PrefetchScalarGridSpec(
        num_scalar_prefetch=2,
        grid=(batch, len(pairs)),
        in_specs=[pl.BlockSpec((ATT_TQ, hq), lambda b, s, qi, kj: (b * nq + qi[s], 0)),
                  pl.BlockSpec((ATT_TQ, MLA_HEADS * LANES), lambda b, s, qi, kj: (b * nq + qi[s], 0)),
                  pl.BlockSpec((ATT_TQ, MLA_LATB), lambda b, s, qi, kj: (b * nq + kj[s], 0))],
        out_specs=pl.BlockSpec((ATT_TQ, hq), lambda b, s, qi, kj: (b * nq + qi[s], 0)),
        scratch_shapes=[pltpu.VMEM((MLA_HEADS, ATT_TQ, LANES), F32),
                        pltpu.VMEM((MLA_HEADS, ATT_TQ, LANES), F32),
                        pltpu.VMEM((MLA_HEADS, ATT_TQ, LANES), F32),
                        pltpu.VMEM((MLA_HEADS, ATT_TQ, MLA_KV_RANK), F32),
                        pltpu.VMEM((MLA_HEADS, ATT_TQ, ATT_TQ), F32),
                        pltpu.VMEM((MLA_HEADS, ATT_TQ, ATT_TQ), BF16)],
    )
    return pl.pallas_call(
        _mla_prompt_kernel,
        grid_spec=grid_spec,
        out_shape=jax.ShapeDtypeStruct(q_lat.shape, BF16),
        compiler_params=_cparams(("parallel", "arbitrary")),
    )(qi, kj, q_lat, q_pe, lat_b)


PAGES_PER_STEP = 16


def _mla_decode_kernel(pt_ref, q_ref, new_ref, *rest):
    page_refs = rest[:PAGES_PER_STEP]
    o_ref, m_sc, l_sc, acc_sc = rest[PAGES_PER_STEP:]
    j = pl.program_id(1)
    q = q_ref[0]

    @pl.when(j == 0)
    def _():
        m_sc[...] = jnp.full(m_sc.shape, NEG_BIG, F32)
        l_sc[...] = jnp.zeros(l_sc.shape, F32)
        acc_sc[...] = jnp.zeros(acc_sc.shape, F32)

    def update(s, values_t):
        m_prev = m_sc[...]
        m_new = jnp.maximum(m_prev, jnp.max(s, axis=-1, keepdims=True))
        alpha = jnp.exp2(m_prev - m_new)
        p = jnp.exp2(s - _lane_tile(m_new, s.shape[1]))
        l_sc[...] = alpha * l_sc[...] + jnp.sum(p, axis=-1, keepdims=True)
        acc_sc[...] = _lane_tile(alpha, MLA_KV_RANK) * acc_sc[...] + _dot_nt(p, values_t)
        m_sc[...] = m_new

    keys_t = jnp.concatenate([pr[...].astype(BF16) for pr in page_refs], axis=1)
    update(_dot(q, keys_t), keys_t[:MLA_KV_RANK, :])

    @pl.when(j == pl.num_programs(1) - 1)
    def _():
        new_t = new_ref[0]
        sn = _dot(q, new_t)
        tok = lax.broadcasted_iota(jnp.int32, sn.shape, 0) >> 3
        key = lax.broadcasted_iota(jnp.int32, sn.shape, 1)
        sn = jnp.where(key <= tok, sn, NEG_BIG)
        update(sn, new_t[:MLA_KV_RANK, :])
        o_ref[0] = (acc_sc[...] / _lane_tile(l_sc[...], MLA_KV_RANK)).astype(BF16)


def _mla_decode(page_table, q_full, new_pad_t, cache_t, layer):
    db, n_pages = page_table.shape
    rows = q_full.shape[1]
    steps = n_pages // PAGES_PER_STEP

    def page_spec(p):
        return pl.BlockSpec((None, None, MLA_LAT, PAGE_SIZE),
                            lambda b, j, pt: (layer, pt[b, j * PAGES_PER_STEP + p], 0, 0))

    grid_spec = pltpu.PrefetchScalarGridSpec(
        num_scalar_prefetch=1,
        grid=(db, steps),
        in_specs=[pl.BlockSpec((1, rows, MLA_LAT), lambda b, j, pt: (b, 0, 0)),
                  pl.BlockSpec((1, MLA_LAT, PAGE_SIZE), lambda b, j, pt: (b, 0, 0))]
        + [page_spec(p) for p in range(PAGES_PER_STEP)],
        out_specs=pl.BlockSpec((1, rows, MLA_KV_RANK), lambda b, j, pt: (b, 0, 0)),
        scratch_shapes=[pltpu.VMEM((rows, LANES), F32), pltpu.VMEM((rows, LANES), F32),
                        pltpu.VMEM((rows, MLA_KV_RANK), F32)],
    )
    return pl.pallas_call(
        _mla_decode_kernel,
        grid_spec=grid_spec,
        out_shape=jax.ShapeDtypeStruct((db, rows, MLA_KV_RANK), BF16),
        compiler_params=_cparams(("parallel", "arbitrary")),
    )(page_table, q_full, new_pad_t, *([cache_t] * PAGES_PER_STEP))


def _rwkv_prep_kernel(rw_ref, before_ref, sh_ref, mu_ref, w0_ref, w2_ref, a0_ref, a2_ref, g2_ref,
                      kk_ref, ka_ref, rk_ref, ones_ref, xs_ref, v_ref, g_ref, rkv_ref, *, tm, seq):
    rw = rw_ref[...]
    rolled = pltpu.roll(rw, 1, axis=0)
    row = lax.broadcasted_iota(jnp.int32, rw.shape, 0)
    if seq >= tm:
        at_start = pl.program_id(0) % (seq // tm) == 0
        first = jnp.where(at_start, sh_ref[...], before_ref[7:8, :])
        prev = jnp.where(row == 0, first, rolled)
    else:
        prev = jnp.where((row & (seq - 1)) == 0, sh_ref[...], rolled)
    xs = rw + (prev - rw) * mu_ref[...]
    d = RWKV_DIM
    r = xs[:, :d]
    k = xs[:, d:2 * d]
    v = xs[:, 2 * d:3 * d]
    xwa = xs[:, 3 * d:3 * d + LANES]
    xg = xs[:, 3 * d + LANES:]
    ones = ones_ref[...]
    w_log = -_softplus(-(w0_ref[...] + _dot(jnp.tanh(xwa), w2_ref[...]))) - 0.5
    a = _sigmoid(a0_ref[...] + _dot(xwa, a2_ref[...]))
    g_ref[...] = _dot(_sigmoid(xg), g2_ref[...])
    kk = k * kk_ref[...]
    ss = _dot_exact_rhs(kk * kk, ones)
    kk = kk / jnp.maximum(jnp.sqrt(ss), 1e-12)
    k2 = k * (1.0 + (a - 1.0) * ka_ref[...])
    xs_ref[0] = -kk
    xs_ref[1] = jnp.exp(-jnp.exp(w_log))
    xs_ref[2] = kk * a
    xs_ref[3] = k2
    xs_ref[4] = r
    v_ref[...] = v
    rkv_ref[...] = _dot_exact_rhs(r * k2 * rk_ref[...], ones) * v


def _rwkv_prep(rw, shift0, w, tm, seq):
    n = rw.shape[0]
    d = RWKV_DIM
    vec = _const_spec((1, d))
    if seq >= tm:
        tiles = seq // tm
        sh = shift0.reshape(-1, 1, RWKV_PROJ)
        sh_spec = pl.BlockSpec((None, 1, RWKV_PROJ), lambda i: (i // tiles, 0, 0))
    else:
        sh = jnp.repeat(shift0, seq, axis=0)
        sh_spec = _row_spec(tm, RWKV_PROJ)
    before_spec = pl.BlockSpec((8, RWKV_PROJ), lambda i: (jnp.maximum(i * (tm // 8) - 1, 0), 0))
    return pl.pallas_call(
        functools.partial(_rwkv_prep_kernel, tm=tm, seq=seq),
        grid=(n // tm,),
        in_specs=[_row_spec(tm, RWKV_PROJ), before_spec, sh_spec, _const_spec((1, RWKV_PROJ)),
                  vec, _const_spec((LANES, d)), vec, _const_spec((LANES, d)),
                  _const_spec((RWKV_G_LORA, d)), vec, vec, vec, _const_spec((d, d))],
        out_specs=[pl.BlockSpec((5, tm, d), lambda i: (0, i, 0))] + [_row_spec(tm, d)] * 3,
        out_shape=[jax.ShapeDtypeStruct((5, n, d), F32)] + [jax.ShapeDtypeStruct((n, d), F32)] * 3,
        compiler_params=_cparams(("parallel",)),
    )(rw, rw, sh, w["mu"], w["w0"], w["w2p"], w["a0"], w["a2p"], w["g2"], w["k_k"], w["k_a"],
      w["r_k"], w["ones_bd"])


SCAN_KH = RWKV_HEAD // 2
SCAN_PAIRS = LANES // 2
SCAN_VR = RWKV_HEAD // 2


def _rwkv_scan_kernel(x_ref, v_ref, s0_ref, y_ref, s_ref, *, tc):
    @pl.when(pl.program_id(1) == 0)
    def _():
        s_ref[...] = s0_ref[...]

    def both_halves(p):
        return p + pltpu.roll(p, SCAN_PAIRS, axis=1)

    def step(t, carry):
        for rows in (slice(0, RWKV_HEAD),):
            p = s_ref[0, 0, rows, :] * x_ref[0, 0, t, 0:1, :]
            for k in range(1, SCAN_KH):
                p = p + s_ref[0, k, rows, :] * x_ref[0, 0, t, k:k + 1, :]
            sa = both_halves(p)
            v_half = v_ref[0, t, rows, :]
            v = jnp.concatenate([v_half, v_half], axis=1)
            y = None
            for k in range(SCAN_KH):
                sn = (s_ref[0, k, rows, :] * x_ref[1, 0, t, k:k + 1, :]
                      + sa * x_ref[2, 0, t, k:k + 1, :] + v * x_ref[3, 0, t, k:k + 1, :])
                s_ref[0, k, rows, :] = sn
                yk = sn * x_ref[4, 0, t, k:k + 1, :]
                y = yk if y is None else y + yk
            y_ref[0, t, rows, :] = both_halves(y)[:, :SCAN_PAIRS]
        return carry

    lax.fori_loop(0, tc, step, 0)


def _rwkv_scan(xs, v, s0, tc):
    _, nb, t, _, _ = xs.shape
    xspec = pl.BlockSpec((5, 1, tc, SCAN_KH, LANES), lambda n, c: (0, n, c, 0, 0))
    vspec = pl.BlockSpec((1, tc, RWKV_HEAD, SCAN_PAIRS), lambda n, c: (n, c, 0, 0))
    sspec = pl.BlockSpec((1, SCAN_KH, RWKV_HEAD, LANES), lambda n, c: (n, 0, 0, 0))
    return pl.pallas_call(
        functools.partial(_rwkv_scan_kernel, tc=tc),
        grid=(nb, t // tc),
        in_specs=[xspec, vspec, sspec],
        out_specs=[vspec, sspec],
        out_shape=[jax.ShapeDtypeStruct(v.shape, F32), jax.ShapeDtypeStruct(s0.shape, F32)],
        compiler_params=_cparams(("parallel", "arbitrary")),
    )(xs, v, s0)


def _even_out_kernel(y_ref, rkv_ref, g_ref, lng_ref, lnb_ref, ones_ref, ol_ref, wuv_ref, woa_ref,
                     wob_ref, x_ref, o_ref):
    ones = ones_ref[...]
    y = y_ref[...]
    inv = 1.0 / RWKV_HEAD
    mean = _dot_exact_rhs(y, ones) * inv
    dlt = y - mean
    var = _dot_exact_rhs(dlt * dlt, ones) * inv
    yn = dlt * lax.rsqrt(var + RWKV_LN_EPS) * lng_ref[...] + lnb_ref[...] + rkv_ref[...]
    ob = (yn * g_ref[...]).astype(BF16)
    pair = 2 * MLA_KV_RANK
    oa = jnp.concatenate(
        [_dot(ol_ref[:, p * pair:(p + 1) * pair], wuv_ref[p]) for p in range(MLA_HEADS // 2)], axis=1)
    o_ref[...] = x_ref[...] + _dot(oa, woa_ref[...]) + _dot(ob, wob_ref[...])


def _even_out(y, rkv, g, o_lat, x, w, tm):
    n = x.shape[0]
    d = RWKV_DIM
    hq = MLA_HEADS * MLA_KV_RANK
    return pl.pallas_call(
        _even_out_kernel,
        grid=(n // tm,),
        in_specs=[_row_spec(tm, d), _row_spec(tm, d), _row_spec(tm, d), _const_spec((1, d)),
                  _const_spec((1, d)), _const_spec((d, d)), _row_spec(tm, hq),
                  _const_spec((MLA_HEADS // 2, 2 * MLA_KV_RANK, 2 * MLA_V)),
                  _const_spec((MLA_HEADS * MLA_V, D_MODEL)), _const_spec((d, D_MODEL)),
                  _row_spec(tm, D_MODEL)],
        out_specs=_row_spec(tm, D_MODEL),
        out_shape=jax.ShapeDtypeStruct((n, D_MODEL), F32),
        compiler_params=_cparams(("parallel",)),
    )(y, rkv, g, w["ln_g"], w["ln_b"], w["ones_bd"], o_lat, w["w_uv_bd"], w["w_out_a"],
      w["w_out_b"], x)


FFN_TF = 1408


def _ffn_kernel(x_ref, g_ref, wg_ref, wu_ref, wd_ref, o_ref, xn_sc, acc_sc):
    f = pl.program_id(1)

    @pl.when(f == 0)
    def _():
        xn_sc[...] = _rms(x_ref[...], g_ref[...]).astype(BF16)
        acc_sc[...] = jnp.zeros(acc_sc.shape, F32)

    xn = xn_sc[...]
    gate = _dot(xn, wg_ref[...])
    up = _dot(xn, wu_ref[...])
    acc_sc[...] += _dot(gate * _sigmoid(gate) * up, wd_ref[...])

    @pl.when(f == pl.num_programs(1) - 1)
    def _():
        o_ref[...] = x_ref[...] + acc_sc[...]


def _ffn(x, g, w_gu, w_down, tm):
    n = x.shape[0]
    nf = D_FF // FFN_TF
    return pl.pallas_call(
        _ffn_kernel,
        grid=(n // tm, nf),
        in_specs=[pl.BlockSpec((tm, D_MODEL), lambda i, f: (i, 0)),
                  pl.BlockSpec((1, D_MODEL), lambda i, f: (0, 0)),
                  pl.BlockSpec((D_MODEL, FFN_TF), lambda i, f: (0, f)),
                  pl.BlockSpec((D_MODEL, FFN_TF), lambda i, f: (0, nf + f)),
                  pl.BlockSpec((FFN_TF, D_MODEL), lambda i, f: (f, 0))],
        out_specs=pl.BlockSpec((tm, D_MODEL), lambda i, f: (i, 0)),
        out_shape=jax.ShapeDtypeStruct((n, D_MODEL), F32),
        scratch_shapes=[pltpu.VMEM((tm, D_MODEL), BF16), pltpu.VMEM((tm, D_MODEL), F32)],
        compiler_params=_cparams(("parallel", "arbitrary")),
    )(x, g, w_gu, w_gu, w_down)


def _odd_in_kernel(x_ref, g_ref, wq_ref, wk_ref, wv_ref, wg_ref, wxa_ref, a2_ref, ab_ref,
                   q_ref, k_ref, v_ref, gate_ref, la_ref):
    xn = _rms(x_ref[...], g_ref[...]).astype(BF16)
    q_ref[...] = _dot(xn, wq_ref[...]) * (GLA_DK ** -0.5)
    k_ref[...] = _dot(xn, wk_ref[...])
    v_ref[...] = _dot(xn, wv_ref[...])
    gate_ref[...] = _dot(xn, wg_ref[...])
    z = _dot(_dot(xn, wxa_ref[...]), a2_ref[...]) + ab_ref[...]
    la_ref[...] = -_softplus(-z) * (1.0 / GLA_GATE_NORM)


def _odd_in(x, w, tm):
    n = x.shape[0]
    return pl.pallas_call(
        _odd_in_kernel,
        grid=(n // tm,),
        in_specs=[_row_spec(tm, D_MODEL), _const_spec((1, D_MODEL)),
                  _const_spec((D_MODEL, GLA_KDIM)), _const_spec((D_MODEL, GLA_KDIM)),
                  _const_spec((D_MODEL, GLA_VDIM)), _const_spec((D_MODEL, GLA_VDIM)),
                  _const_spec((D_MODEL, LANES)), _const_spec((LANES, GLA_KDIM)),
                  _const_spec((1, GLA_KDIM))],
        out_specs=[_row_spec(tm, GLA_KDIM), _row_spec(tm, GLA_KDIM), _row_spec(tm, GLA_VDIM),
                   _row_spec(tm, GLA_VDIM), _row_spec(tm, GLA_KDIM)],
        out_shape=[jax.ShapeDtypeStruct((n, GLA_KDIM), F32), jax.ShapeDtypeStruct((n, GLA_KDIM), F32),
                   jax.ShapeDtypeStruct((n, GLA_VDIM), F32), jax.ShapeDtypeStruct((n, GLA_VDIM), F32),
                   jax.ShapeDtypeStruct((n, GLA_KDIM), F32)],
        compiler_params=_cparams(("parallel",)),
    )(x, w["norm_mix"], w["w_q"], w["w_k"], w["w_v"], w["w_g"], w["w_xa"], w["a2p"], w["ab"])


def _gla_kernel(q_ref, k_ref, v_ref, la_ref, s0_ref, o_ref, st_ref):
    c = GLA_CHUNK

    @pl.when(pl.program_id(1) == 0)
    def _():
        st_ref[...] = s0_ref[...]

    row = lax.broadcasted_iota(jnp.int32, (c, c), 0)
    col = lax.broadcasted_iota(jnp.int32, (c, c), 1)
    tri = row >= col
    tri_b = jnp.where(tri, 1.0, 0.0).astype(BF16)
    for h in range(GLA_HEADS):
        ks = slice(h * GLA_DK, (h + 1) * GLA_DK)
        vs = slice(h * GLA_DV, (h + 1) * GLA_DV)
        b = _dot_exact_lhs(tri_b, la_ref[:, ks])
        q = q_ref[:, ks]
        k = k_ref[:, ks]
        v = v_ref[:, vs]
        b_end = b[c - 1:c, :]
        qe = (q * jnp.exp(b)).astype(BF16)
        ke = (k * jnp.exp(-b)).astype(BF16)
        a_mat = jnp.where(tri, _dot_nt(qe, ke), 0.0)
        st = st_ref[0, h]
        o_ref[:, vs] = _dot_nt(qe, st) + _dot(a_mat, v)
        k_end = k * jnp.exp(b_end - b)
        st_ref[0, h] = st * jnp.exp(b_end) + _dot(v.T, k_end)


def _gla(q, k, v, la, s0t, batch, seq):
    nc = seq // GLA_CHUNK
    rspec = lambda width: pl.BlockSpec((GLA_CHUNK, width), lambda b, c: (b * nc + c, 0))
    sspec = pl.BlockSpec((1, GLA_HEADS, GLA_DV, GLA_DK), lambda b, c: (b, 0, 0, 0))
    return pl.pallas_call(
        _gla_kernel,
        grid=(batch, nc),
        in_specs=[rspec(GLA_KDIM), rspec(GLA_KDIM), rspec(GLA_VDIM), rspec(GLA_KDIM), sspec],
        out_specs=[rspec(GLA_VDIM), sspec],
        out_shape=[jax.ShapeDtypeStruct(v.shape, F32), jax.ShapeDtypeStruct(s0t.shape, F32)],
        compiler_params=_cparams(("parallel", "arbitrary")),
    )(q, k, v, la, s0t)


def _odd_out_kernel(o_ref, gate_ref, gn_ref, wo_ref, x_ref, y_ref):
    parts = []
    for h in range(GLA_HEADS):
        vs = slice(h * GLA_DV, (h + 1) * GLA_DV)
        parts.append(_rms(o_ref[:, vs], gn_ref[:, vs]))
    gate = gate_ref[...]
    on = jnp.concatenate(parts, axis=1) * (gate * _sigmoid(gate))
    y_ref[...] = x_ref[...] + _dot(on, wo_ref[...])


def _odd_out(o, gate, x, w, tm):
    n = x.shape[0]
    return pl.pallas_call(
        _odd_out_kernel,
        grid=(n // tm,),
        in_specs=[_row_spec(tm, GLA_VDIM), _row_spec(tm, GLA_VDIM), _const_spec((1, GLA_VDIM)),
                  _const_spec((GLA_VDIM, D_MODEL)), _row_spec(tm, D_MODEL)],
        out_specs=_row_spec(tm, D_MODEL),
        out_shape=jax.ShapeDtypeStruct((n, D_MODEL), F32),
        compiler_params=_cparams(("parallel",)),
    )(o, gate, w["gla_norm"], w["w_out"], x)


def _pack_bf16_pairs(lo, hi):
    lo_bits = pltpu.bitcast(lo.astype(BF16).astype(F32), jnp.uint32)
    hi_bits = pltpu.bitcast(hi.astype(BF16).astype(F32), jnp.uint32)
    return (lo_bits >> 16) | (hi_bits & jnp.uint32(0xFFFF0000))


def _unpack_bf16_pairs(u):
    lo = pltpu.bitcast(u << 16, F32)
    hi = pltpu.bitcast(u & jnp.uint32(0xFFFF0000), F32)
    return lo, hi


def _router_kernel(x_ref, g_ref, wr_ref, xp_ref, idx_ref, gate_ref):
    xn = _rms(x_ref[...], g_ref[...])
    half = D_MODEL // 2
    xp_ref[...] = _pack_bf16_pairs(xn[:, :half], xn[:, half:])
    logits = _dot_f32ish(xn, wr_ref[...])
    lane = lax.broadcasted_iota(jnp.int32, logits.shape, 1)
    logits = jnp.where(lane < N_EXPERTS, logits, NEG_BIG)
    m1 = jnp.max(logits, axis=-1, keepdims=True)
    i1 = jnp.min(jnp.where(logits == m1, lane, LANES), axis=-1, keepdims=True)
    rest = jnp.where(lane == i1, NEG_BIG, logits)
    m2 = jnp.max(rest, axis=-1, keepdims=True)
    i2 = jnp.min(jnp.where(rest == m2, lane, LANES), axis=-1, keepdims=True)
    e2 = jnp.exp(m2 - m1)
    g1 = 1.0 / (1.0 + e2)
    g2 = e2 / (1.0 + e2)
    idx_ref[...] = jnp.where(lane == 0, i1, jnp.where(lane == 1, i2, 0))
    gate_ref[...] = jnp.where(lane == 0, g1, jnp.where(lane == 1, g2, 0.0))


def _router(x, g, wr, tm):
    n = x.shape[0]
    return pl.pallas_call(
        _router_kernel,
        grid=(n // tm,),
        in_specs=[_row_spec(tm, D_MODEL), _const_spec((1, D_MODEL)), _const_spec((D_MODEL, LANES))],
        out_specs=[_row_spec(tm, D_MODEL // 2), _row_spec(tm, LANES), _row_spec(tm, LANES)],
        out_shape=[jax.ShapeDtypeStruct((n, D_MODEL // 2), jnp.uint32),
                   jax.ShapeDtypeStruct((n, LANES), jnp.int32), jax.ShapeDtypeStruct((n, LANES), F32)],
        compiler_params=_cparams(("parallel",)),
    )(x, g, wr)


MOE_TF = 1792
MOE_CHUNK = 2 * LANES


def _route(top_i, tm):
    n = top_i.shape[0]
    slots = 2 * n
    n_tiles = -(-(slots + N_EXPERTS * (tm - 1)) // tm)
    e_flat = top_i.reshape(-1)
    onehot = (e_flat[:, None] == jnp.arange(N_EXPERTS, dtype=jnp.int32)[None, :]).astype(jnp.int32)
    csum = jnp.cumsum(onehot, axis=0)
    rank = jnp.sum(onehot * csum, axis=1) - 1
    counts = csum[-1]
    padded = ((counts + tm - 1) // tm) * tm
    ends = jnp.cumsum(padded)
    starts = ends - padded
    dest = (jnp.sum(onehot * starts[None, :], axis=1) + rank).astype(jnp.int32)
    tile_start = jnp.arange(n_tiles, dtype=jnp.int32) * tm
    tile_expert = jnp.minimum(jnp.sum((tile_start[:, None] >= ends[None, :]).astype(jnp.int32), axis=1),
                              N_EXPERTS - 1).astype(jnp.int32)
    tile_valid = (tile_start < ends[-1]).astype(jnp.int32)
    src = jnp.zeros((n_tiles * tm,), jnp.int32).at[dest].set(jnp.arange(slots, dtype=jnp.int32) // 2)
    return dest, src, tile_expert, tile_valid


def _moe_gather_kernel(src_ref, x_ref, o_ref, *, tg):
    base = pl.program_id(0) * tg

    def body(r, carry):
        o_ref[pl.ds(r, 1), :] = x_ref[pl.ds(src_ref[base + r], 1), :]
        return carry

    lax.fori_loop(0, tg, body, 0, unroll=8)


def _moe_gather(src, xp, tg):
    rows = src.shape[0]
    width = xp.shape[1]
    grid_spec = pltpu.PrefetchScalarGridSpec(
        num_scalar_prefetch=1,
        grid=(rows // tg,),
        in_specs=[pl.BlockSpec(memory_space=pltpu.VMEM)],
        out_specs=pl.BlockSpec((tg, width), lambda i, s: (i, 0)),
    )
    return pl.pallas_call(
        functools.partial(_moe_gather_kernel, tg=tg),
        grid_spec=grid_spec,
        out_shape=jax.ShapeDtypeStruct((rows, width), jnp.uint32),
        compiler_params=_cparams(("arbitrary",)),
    )(src, xp)


def _moe_up_kernel(te_ref, tv_ref, xs_ref, wg_ref, wu_ref, h_ref):
    @pl.when(tv_ref[pl.program_id(1)] != 0)
    def _():
        lo, hi = _unpack_bf16_pairs(xs_ref[...])
        half = D_MODEL // 2
        gate = _dot(lo, wg_ref[:half, :]) + _dot(hi, wg_ref[half:, :])
        up = _dot(lo, wu_ref[:half, :]) + _dot(hi, wu_ref[half:, :])
        h_ref[...] = (gate * _sigmoid(gate) * up).astype(BF16)


def _moe_up(te, tv, xs, w_gu, layer, tm):
    rows = xs.shape[0]
    nf = D_FF_EXPERT // MOE_TF
    wspec = lambda off: pl.BlockSpec((None, None, D_MODEL, MOE_TF),
                                     lambda f, t, te, tv: (layer, te[t], 0, off + f))
    grid_spec = pltpu.PrefetchScalarGridSpec(
        num_scalar_prefetch=2,
        grid=(nf, rows // tm),
        in_specs=[pl.BlockSpec((tm, D_MODEL // 2), lambda f, t, te, tv: (t, 0)), wspec(0), wspec(nf)],
        out_specs=pl.BlockSpec((tm, MOE_TF), lambda f, t, te, tv: (t, f)),
    )
    return pl.pallas_call(
        _moe_up_kernel,
        grid_spec=grid_spec,
        out_shape=jax.ShapeDtypeStruct((rows, D_FF_EXPERT), BF16),
        compiler_params=_cparams(("arbitrary", "arbitrary")),
    )(te, tv, xs, w_gu, w_gu)


def _moe_down_kernel(te_ref, tv_ref, h_ref, wd_ref, y_ref):
    @pl.when(tv_ref[pl.program_id(0)] != 0)
    def _():
        y = _dot(h_ref[...], wd_ref[...])
        parts = []
        for c in range(D_MODEL // MOE_CHUNK):
            lo = y[:, c * MOE_CHUNK:c * MOE_CHUNK + LANES]
            hi = y[:, c * MOE_CHUNK + LANES:(c + 1) * MOE_CHUNK]
            parts.append(_pack_bf16_pairs(lo, hi))
        y_ref[...] = jnp.concatenate(parts, axis=1)


def _moe_down(te, tv, h, w_down, layer, tm):
    rows = h.shape[0]
    grid_spec = pltpu.PrefetchScalarGridSpec(
        num_scalar_prefetch=2,
        grid=(rows // tm,),
        in_specs=[pl.BlockSpec((tm, D_FF_EXPERT), lambda t, te, tv: (t, 0)),
                  pl.BlockSpec((None, None, D_FF_EXPERT, D_MODEL),
                               lambda t, te, tv: (layer, te[t], 0, 0))],
        out_specs=pl.BlockSpec((tm, D_MODEL // 2), lambda t, te, tv: (t, 0)),
    )
    return pl.pallas_call(
        _moe_down_kernel,
        grid_spec=grid_spec,
        out_shape=jax.ShapeDtypeStruct((rows, D_MODEL // 2), jnp.uint32),
        compiler_params=_cparams(("arbitrary",)),
    )(te, tv, h, w_down)


def _moe_combine_kernel(dest_ref, ys_ref, g1_ref, g2_ref, x_ref, o_ref, *, tmc):
    base = 2 * pl.program_id(1) * tmc

    def body(r, carry):
        row = pl.ds(r, 1)
        lo1, hi1 = _unpack_bf16_pairs(ys_ref[pl.ds(dest_ref[base + 2 * r], 1), :])
        lo2, hi2 = _unpack_bf16_pairs(ys_ref[pl.ds(dest_ref[base + 2 * r + 1], 1), :])
        g1 = g1_ref[row, :]
        g2 = g2_ref[row, :]
        moe = jnp.concatenate([g1 * lo1 + g2 * lo2, g1 * hi1 + g2 * hi2], axis=1)
        o_ref[row, :] = x_ref[row, :] + moe
        return carry

    lax.fori_loop(0, tmc, body, 0, unroll=8)


def _moe_combine(dest, ys, g1b, g2b, x, tmc):
    n = x.shape[0]
    rows = ys.shape[0]
    grid_spec = pltpu.PrefetchScalarGridSpec(
        num_scalar_prefetch=1,
        grid=(D_MODEL // MOE_CHUNK, n // tmc),
        in_specs=[pl.BlockSpec((rows, LANES), lambda c, i, d: (0, c)),
                  pl.BlockSpec((tmc, LANES), lambda c, i, d: (i, 0)),
                  pl.BlockSpec((tmc, LANES), lambda c, i, d: (i, 0)),
                  pl.BlockSpec((tmc, MOE_CHUNK), lambda c, i, d: (i, c))],
        out_specs=pl.BlockSpec((tmc, MOE_CHUNK), lambda c, i, d: (i, c)),
    )
    return pl.pallas_call(
        functools.partial(_moe_combine_kernel, tmc=tmc),
        grid_spec=grid_spec,
        out_shape=jax.ShapeDtypeStruct((n, D_MODEL), F32),
        compiler_params=_cparams(("arbitrary", "arbitrary")),
    )(dest, ys, g1b, g2b, x)


def _final_norm_kernel(x_ref, g_ref, o_ref):
    o_ref[...] = _rms(x_ref[...], g_ref[...])


def _final_norm(x, g, tm):
    n = x.shape[0]
    return pl.pallas_call(
        _final_norm_kernel,
        grid=(n // tm,),
        in_specs=[_row_spec(tm, D_MODEL), _const_spec((1, D_MODEL))],
        out_specs=_row_spec(tm, D_MODEL),
        out_shape=jax.ShapeDtypeStruct((n, D_MODEL), F32),
        compiler_params=_cparams(("parallel",)),
    )(x, g)


def _scan_vec_layout(xs, batch, seq):
    nb = batch * RWKV_HEADS // SCAN_PAIRS
    x = xs.reshape(5, batch, seq, RWKV_HEADS, 2, SCAN_KH).transpose(0, 2, 5, 4, 1, 3)
    x = x.reshape(5, seq, SCAN_KH, 2, nb, SCAN_PAIRS).transpose(0, 4, 1, 2, 3, 5)
    return x.reshape(5, nb, seq, SCAN_KH, LANES)


def _scan_val_layout(v, batch, seq):
    nb = batch * RWKV_HEADS // SCAN_PAIRS
    v4 = v.reshape(batch, seq, RWKV_HEADS, RWKV_HEAD).transpose(1, 3, 0, 2)
    return v4.reshape(seq, RWKV_HEAD, nb, SCAN_PAIRS).transpose(2, 0, 1, 3)


def _scan_val_unlayout(y, batch, seq):
    v4 = y.transpose(1, 2, 0, 3).reshape(seq, RWKV_HEAD, batch, RWKV_HEADS)
    return v4.transpose(2, 0, 3, 1).reshape(batch * seq, RWKV_DIM)


def _scan_state_layout(s, batch):
    nb = batch * RWKV_HEADS // SCAN_PAIRS
    s6 = s.reshape(batch, RWKV_HEADS, RWKV_HEAD, 2, SCAN_KH).transpose(4, 2, 3, 0, 1)
    s6 = s6.reshape(SCAN_KH, RWKV_HEAD, 2, nb, SCAN_PAIRS).transpose(3, 0, 1, 2, 4)
    return s6.reshape(nb, SCAN_KH, RWKV_HEAD, LANES)


def _scan_state_unlayout(arr, batch):
    nb = arr.shape[0]
    s = arr.reshape(nb, SCAN_KH, RWKV_HEAD, 2, SCAN_PAIRS).transpose(0, 4, 2, 3, 1)
    return s.reshape(batch, RWKV_HEADS, RWKV_HEAD, RWKV_HEAD)


def _swap_halves(w):
    half = w.shape[-1] // 2
    return jnp.concatenate([w[..., half:], w[..., :half]], axis=-1)


def _prep_even(i, norm_mix, norm_ffn, w_in, q_norm, kv_norm, w_uq, w_uk, w_uv, mu, w0, w2, a0, a2,
               g2, k_k, k_a, r_k, ln_g, ln_b, w_out, ffn_gu, ffn_down):
    w = {}
    row = lambda v: v[i].reshape(1, -1)
    w_in = w_in[i]
    w["norm_mix"] = row(norm_mix)
    w["norm_ffn"] = row(norm_ffn)
    w["w_q"] = w_in[:, :MLA_Q_RANK].astype(BF16)
    w_kv = w_in[:, MLA_Q_RANK:MLA_Q_RANK + MLA_LAT]
    w["w_ckv"] = w_kv[:, :MLA_KV_RANK].astype(BF16)
    lane_pad = lambda m: jnp.pad(m, [(0, 0)] * (m.ndim - 1) + [(0, LANES - m.shape[-1])])
    w["w_pe_a"] = lane_pad(w_kv[:, MLA_KV_RANK:]).astype(BF16)
    w["w_pe_b"] = lane_pad(_swap_halves(w_kv[:, MLA_KV_RANK:])).astype(BF16)
    w["w_rw"] = w_in[:, MLA_Q_RANK + MLA_LAT:].astype(BF16)
    w["q_norm"] = row(q_norm)
    w["kv_norm"] = row(kv_norm)
    uq = w_uq[i].reshape(MLA_Q_RANK, MLA_HEADS, MLA_NOPE + MLA_ROPE)
    uq_pe = uq[:, :, MLA_NOPE:]
    w["w_qpe_a"] = lane_pad(uq_pe).reshape(MLA_Q_RANK, -1).astype(BF16)
    w["w_qpe_b"] = lane_pad(_swap_halves(uq_pe)).reshape(MLA_Q_RANK, -1).astype(BF16)
    w["w_qlat"] = _fold_qlat(uq[:, :, :MLA_NOPE].transpose(1, 0, 2), w_uk[i].transpose(1, 0, 2))
    uv = w_uv[i].transpose(1, 0, 2).reshape(MLA_HEADS // 2, 2, MLA_KV_RANK, MLA_V)
    zero = jnp.zeros_like(uv[:, 0])
    w["w_uv_bd"] = jnp.concatenate(
        [jnp.concatenate([uv[:, 0], zero], axis=-1), jnp.concatenate([zero, uv[:, 1]], axis=-1)],
        axis=1).astype(BF16)
    w["mu"] = row(mu)
    w["w0"] = row(w0)
    pad = lambda m, before: jnp.pad(m, ((before, LANES - before - m.shape[0]), (0, 0))).astype(BF16)
    w["w2p"] = pad(w2[i], 0)
    w["a2p"] = pad(a2[i], RWKV_W_LORA)
    w["a0"] = row(a0)
    w["g2"] = g2[i].astype(BF16)
    w["k_k"] = row(k_k)
    w["k_a"] = row(k_a)
    w["r_k"] = row(r_k)
    w["ln_g"] = row(ln_g)
    w["ln_b"] = row(ln_b)
    head = jnp.arange(RWKV_DIM) // RWKV_HEAD
    w["ones_bd"] = (head[:, None] == head[None, :]).astype(BF16)
    w["w_out_a"] = w_out[i][:MLA_HEADS * MLA_V].astype(BF16)
    w["w_out_b"] = w_out[i][MLA_HEADS * MLA_V:].astype(BF16)
    w["ffn_gu"] = ffn_gu[i].astype(BF16)
    w["ffn_down"] = ffn_down[i].astype(BF16)
    return w


def _prep_odd(i, norm_mix, norm_ffn, w_in, a2, ab, gla_norm, w_out, router, moe_gu, moe_down):
    w = {}
    row = lambda v: v[i].reshape(1, -1)
    w_in = w_in[i]
    w["norm_mix"] = row(norm_mix)
    w["norm_ffn"] = row(norm_ffn)
    w["w_q"] = w_in[:, :GLA_KDIM].astype(BF16)
    w["w_k"] = w_in[:, GLA_KDIM:2 * GLA_KDIM].astype(BF16)
    w["w_v"] = w_in[:, 2 * GLA_KDIM:2 * GLA_KDIM + GLA_VDIM].astype(BF16)
    w["w_g"] = w_in[:, 2 * GLA_KDIM + GLA_VDIM:2 * GLA_KDIM + 2 * GLA_VDIM].astype(BF16)
    w["w_xa"] = jnp.pad(w_in[:, 2 * GLA_KDIM + 2 * GLA_VDIM:],
                        ((0, 0), (0, LANES - GLA_GATE_RANK))).astype(BF16)
    w["a2p"] = jnp.pad(a2[i], ((0, LANES - GLA_GATE_RANK), (0, 0))).astype(BF16)
    w["ab"] = row(ab)
    w["gla_norm"] = row(gla_norm)
    w["w_out"] = w_out[i].astype(BF16)
    w["router"] = jnp.pad(router[i], ((0, 0), (0, LANES - N_EXPERTS)))
    w["layer"] = i
    w["moe_gu"] = moe_gu
    w["moe_down"] = moe_down
    return w


def _rope_tables(pos, reps):
    inv = ROPE_THETA ** (-jnp.arange(0, MLA_ROPE, 2, dtype=F32) / MLA_ROPE)
    ang = pos.astype(F32)[:, None] * inv[None, :]
    cos, sin = jnp.cos(ang), jnp.sin(ang)
    pad = ((0, 0), (0, LANES - MLA_ROPE))
    cs = jnp.tile(jnp.pad(jnp.concatenate([cos, cos], axis=-1), pad), (reps, 1))
    sn = jnp.tile(jnp.pad(jnp.concatenate([-sin, sin], axis=-1), pad), (reps, 1))
    return {"cs": cs, "sn": sn, "cs8": jnp.tile(cs, (1, MLA_HEADS)), "sn8": jnp.tile(sn, (1, MLA_HEADS))}


def _even_layer(x, batch, seq, tabs, state, shift0, past, w, tm, tc):
    n = batch * seq
    lat, lat_b, q_lat, q_pe, rw = _even_in(x, w, tabs, tm)
    if past is None:
        o_lat = _mla_prompt(q_lat, q_pe, lat_b, batch, seq)
    else:
        cache, layer, page_table = past
        rows = seq * MLA_HEADS
        q_full = jnp.concatenate([q_lat.reshape(batch, rows, MLA_KV_RANK),
                                  q_pe.reshape(batch, rows, LANES)[:, :, :MLA_ROPE]], axis=-1)
        new_pad_t = jnp.pad(lat_b.reshape(batch, seq, MLA_LATB)[:, :, :MLA_LAT],
                            ((0, 0), (0, PAGE_SIZE - seq), (0, 0))).transpose(0, 2, 1)
        o_lat = _mla_decode(page_table, q_full, new_pad_t, cache.transpose(0, 1, 3, 2), layer)
        o_lat = o_lat.reshape(n, MLA_HEADS * MLA_KV_RANK)

    rw3 = rw.reshape(batch, seq, RWKV_PROJ)
    xs5, v, g, rkv = _rwkv_prep(rw, shift0, w, tm, seq)
    y_l, s_l = _rwkv_scan(_scan_vec_layout(xs5, batch, seq), _scan_val_layout(v, batch, seq),
                          _scan_state_layout(state, batch), tc)
    y = _scan_val_unlayout(y_l, batch, seq)
    new_state = _scan_state_unlayout(s_l, batch)

    x = _even_out(y, rkv, g, o_lat, x, w, tm)
    x = _ffn(x, w["norm_ffn"], w["ffn_gu"], w["ffn_down"], tm)
    return x, lat.reshape(batch, seq, MLA_LAT), new_state, rw3[:, -1]


def _odd_layer(x, batch, seq, state, w, final_norm, tm, tm_moe):
    q, k, v, gate, la = _odd_in(x, w, tm)
    seq_p = -(-seq // GLA_CHUNK) * GLA_CHUNK
    if seq_p != seq:
        padr = lambda t: jnp.pad(t.reshape(batch, seq, -1), ((0, 0), (0, seq_p - seq), (0, 0))
                                 ).reshape(batch * seq_p, -1)
        qp, kp, vp, lap = padr(q), padr(k), padr(v), padr(la)
    else:
        qp, kp, vp, lap = q, k, v, la
    o, st = _gla(qp, kp, vp, lap, state.transpose(0, 1, 3, 2), batch, seq_p)
    if seq_p != seq:
        o = o.reshape(batch, seq_p, GLA_VDIM)[:, :seq].reshape(batch * seq, GLA_VDIM)
    x = _odd_out(o, gate, x, w, tm)
    xp, idx, gates = _router(x, w["norm_ffn"], w["router"], tm)
    dest, src, tile_expert, tile_valid = _route(idx[:, :2], tm_moe)
    xs = _moe_gather(src, xp, tm_moe)
    h = _moe_up(tile_expert, tile_valid, xs, w["moe_gu"], w["layer"], tm_moe)
    ys = _moe_down(tile_expert, tile_valid, h, w["moe_down"], w["layer"], tm_moe)
    g1b = jnp.broadcast_to(gates[:, 0:1], (x.shape[0], LANES))
    g2b = jnp.broadcast_to(gates[:, 1:2], (x.shape[0], LANES))
    z = _moe_combine(dest, ys, g1b, g2b, x, tm)
    y = _final_norm(z, final_norm, tm)
    return y, st.transpose(0, 1, 3, 2)


def kernel(x_prompt, x_sample, cache_mla, state_rwkv, state_rwkv_shift, state_gla, page_table, norm_mix_even, norm_ffn_even, w_in_even, mla_q_norm, mla_kv_norm, mla_w_uq, mla_w_uk, mla_w_uv, rwkv_mu, rwkv_w0, rwkv_w2, rwkv_a0, rwkv_a2, rwkv_g2, rwkv_k_k, rwkv_k_a, rwkv_r_k, rwkv_ln_g, rwkv_ln_b, w_out_even, ffn_w_gu_even, ffn_w_down_even, norm_mix_odd, norm_ffn_odd, w_in_odd, gla_a2, gla_ab, gla_norm, w_out_odd, moe_router, moe_w_gu, moe_w_down, final_norm):
    bp, tp, _ = x_prompt.shape
    bs, ts, _ = x_sample.shape
    past_len = page_table.shape[1] * PAGE_SIZE
    tm_p, tm_s = 512, bs * ts
    we = _prep_even(0, norm_mix_even, norm_ffn_even, w_in_even, mla_q_norm, mla_kv_norm, mla_w_uq,
                    mla_w_uk, mla_w_uv, rwkv_mu, rwkv_w0, rwkv_w2, rwkv_a0, rwkv_a2, rwkv_g2,
                    rwkv_k_k, rwkv_k_a, rwkv_r_k, rwkv_ln_g, rwkv_ln_b, w_out_even, ffn_w_gu_even,
                    ffn_w_down_even)
    wo = _prep_odd(0, norm_mix_odd, norm_ffn_odd, w_in_odd, gla_a2, gla_ab, gla_norm, w_out_odd,
                   moe_router, moe_w_gu, moe_w_down)
    fn = final_norm.reshape(1, -1)
    tabs_p = _rope_tables(jnp.arange(tp), 1)
    tabs_s = _rope_tables(past_len + jnp.arange(ts), bs)

    hp = x_prompt.reshape(bp * tp, D_MODEL)
    hs = x_sample.reshape(bs * ts, D_MODEL)
    zeros_state = jnp.zeros((bp, RWKV_HEADS, RWKV_HEAD, RWKV_HEAD), F32)
    zeros_shift = jnp.zeros((bp, RWKV_PROJ), F32)
    hp, lat_p, rs_p, sh_p = _even_layer(hp, bp, tp, tabs_p, zeros_state, zeros_shift, None, we,
                                        tm_p, 32)
    hs, lat_s, rs_s, sh_s = _even_layer(hs, bs, ts, tabs_s, state_rwkv[0], state_rwkv_shift[0],
                                        (cache_mla, 0, page_table), we, tm_s, ts)
    zeros_gla = jnp.zeros((bp, GLA_HEADS, GLA_DK, GLA_DV), F32)
    yp, gs_p = _odd_layer(hp, bp, tp, zeros_gla, wo, fn, tm_p, 512)
    ys, gs_s = _odd_layer(hs, bs, ts, state_gla[0], wo, fn, tm_s, 128)
    return (yp.reshape(bp, tp, D_MODEL), ys.reshape(bs, ts, D_MODEL), lat_p[None], lat_s[None],
            rs_p[None], rs_s[None], sh_p[None], sh_s[None], gs_p[None], gs_s[None])
```

```python
import functools

import jax
import jax.numpy as jnp
from jax import lax
from jax.experimental import pallas as pl
from jax.experimental.pallas import tpu as pltpu

F32 = jnp.float32
BF16 = jnp.bfloat16

D_MODEL = 1024
PAGE_SIZE = 128
NORM_EPS = 1e-6

MLA_HEADS = 8
MLA_NOPE = 64
MLA_ROPE = 32
MLA_V = 64
MLA_Q_RANK = 384
MLA_KV_RANK = 256
MLA_LAT = MLA_KV_RANK + MLA_ROPE
MLA_LATB = MLA_KV_RANK + 128
MLA_SCALE = (MLA_NOPE + MLA_ROPE) ** -0.5
ROPE_THETA = 10000.0

RWKV_HEADS = 8
RWKV_HEAD = 64
RWKV_DIM = RWKV_HEADS * RWKV_HEAD
RWKV_W_LORA = 64
RWKV_A_LORA = 64
RWKV_G_LORA = 128
RWKV_PROJ = 3 * RWKV_DIM + RWKV_W_LORA + RWKV_A_LORA + RWKV_G_LORA
RWKV_LN_EPS = 64e-5

GLA_HEADS = 4
GLA_DK = 128
GLA_DV = 256
GLA_KDIM = GLA_HEADS * GLA_DK
GLA_VDIM = GLA_HEADS * GLA_DV
GLA_GATE_RANK = 16
GLA_GATE_NORM = 16.0
GLA_CHUNK = 128

D_FF = 2816
N_EXPERTS = 8
D_FF_EXPERT = 3584

LANES = 128
VMEM_LIMIT = 56 * 1024 * 1024
NEG_BIG = -1e30
LOG2_E = 1.4426950408889634
Q_PRESCALE = MLA_SCALE * LOG2_E


def _cparams(sem):
    return pltpu.CompilerParams(dimension_semantics=sem, vmem_limit_bytes=VMEM_LIMIT)


def _const_spec(shape):
    nd = len(shape)
    return pl.BlockSpec(shape, lambda *_: (0,) * nd)


def _row_spec(tm, width):
    return pl.BlockSpec((tm, width), lambda i: (i, 0))


def _dot(a, b):
    return jnp.dot(a.astype(BF16), b.astype(BF16), preferred_element_type=F32)


def _dot_nt(a, b):
    return lax.dot_general(a.astype(BF16), b.astype(BF16), (((1,), (1,)), ((), ())),
                           preferred_element_type=F32)


def _split2(x):
    hi = x.astype(BF16)
    lo = (x - hi.astype(F32)).astype(BF16)
    return hi, lo


def _split3(x):
    hi = x.astype(BF16)
    r1 = x - hi.astype(F32)
    mid = r1.astype(BF16)
    lo = (r1 - mid.astype(F32)).astype(BF16)
    return hi, mid, lo


def _dot_exact_rhs(x, e):
    hi, mid, lo = _split3(x)
    return (jnp.dot(hi, e, preferred_element_type=F32) + jnp.dot(mid, e, preferred_element_type=F32)
            + jnp.dot(lo, e, preferred_element_type=F32))


def _dot_exact_lhs(e, x):
    hi, mid, lo = _split3(x)
    return (jnp.dot(e, hi, preferred_element_type=F32) + jnp.dot(e, mid, preferred_element_type=F32)
            + jnp.dot(e, lo, preferred_element_type=F32))


def _dot_f32ish(a, b):
    ah, al = _split2(a)
    bh, bl = _split2(b)
    return (jnp.dot(ah, bh, preferred_element_type=F32) + jnp.dot(ah, bl, preferred_element_type=F32)
            + jnp.dot(al, bh, preferred_element_type=F32))


def _lane_tile(x, width):
    return x if width == LANES else jnp.concatenate([x] * (width // LANES), axis=1)


def _rms(x, g, eps=NORM_EPS):
    return x * lax.rsqrt(jnp.mean(x * x, axis=-1, keepdims=True) + eps) * g


def _sigmoid(x):
    return 1.0 / (1.0 + jnp.exp(-x))


def _softplus(x):
    return jnp.maximum(x, 0.0) + jnp.log(1.0 + jnp.exp(-jnp.abs(x)))


def _fold_qlat_kernel(uq_ref, uk_ref, o_ref):
    a = uq_ref[...]
    b = uk_ref[...]
    ah, al = _split2(a)
    bh, bl = _split2(b)
    dn = (((1,), (1,)), ((), ()))
    o = (lax.dot_general(ah, bh, dn, preferred_element_type=F32)
         + lax.dot_general(ah, bl, dn, preferred_element_type=F32)
         + lax.dot_general(al, bh, dn, preferred_element_type=F32))
    o_ref[...] = o.astype(BF16)


def _fold_qlat(uq_nope, uk):
    return pl.pallas_call(
        _fold_qlat_kernel,
        grid=(MLA_HEADS,),
        in_specs=[pl.BlockSpec((None, MLA_Q_RANK, MLA_NOPE), lambda h: (h, 0, 0)),
                  pl.BlockSpec((None, MLA_KV_RANK, MLA_NOPE), lambda h: (h, 0, 0))],
        out_specs=pl.BlockSpec((MLA_Q_RANK, MLA_KV_RANK), lambda h: (0, h)),
        out_shape=jax.ShapeDtypeStruct((MLA_Q_RANK, MLA_HEADS * MLA_KV_RANK), BF16),
        compiler_params=_cparams(("arbitrary",)),
    )(uq_nope, uk)


def _even_in_kernel(x_ref, g_ref, wq_ref, wckv_ref, wpa_ref, wpb_ref, wrw_ref, qn_ref, kvn_ref,
                    cs_ref, sn_ref, wql_ref, wqa_ref, wqb_ref, cs8_ref, sn8_ref,
                    lat_ref, latb_ref, ql_ref, qpe_ref, rw_ref):
    xn = _rms(x_ref[...], g_ref[...]).astype(BF16)
    cq = _rms(_dot(xn, wq_ref[...]), qn_ref[...]).astype(BF16)
    ql_ref[...] = (_dot(cq, wql_ref[...]) * Q_PRESCALE).astype(BF16)
    qpe = _dot(cq, wqa_ref[...]) * cs8_ref[...] + _dot(cq, wqb_ref[...]) * sn8_ref[...]
    qpe_ref[...] = (qpe * Q_PRESCALE).astype(BF16)
    ckv = _rms(_dot(xn, wckv_ref[...]), kvn_ref[...])
    kpe = _dot(xn, wpa_ref[...]) * cs_ref[...] + _dot(xn, wpb_ref[...]) * sn_ref[...]
    lat_ref[:, :MLA_KV_RANK] = ckv
    lat_ref[:, MLA_KV_RANK:] = kpe[:, :MLA_ROPE]
    latb_ref[:, :MLA_KV_RANK] = ckv.astype(BF16)
    latb_ref[:, MLA_KV_RANK:] = kpe.astype(BF16)
    rw_ref[...] = _dot(xn, wrw_ref[...])


def _even_in(x, w, tabs, tm):
    n = x.shape[0]
    nt = tabs["cs"].shape[0] // tm
    tab = lambda width: pl.BlockSpec((tm, width), lambda i: (i % nt, 0))
    hq = MLA_HEADS * MLA_KV_RANK
    hr = MLA_HEADS * LANES
    return pl.pallas_call(
        _even_in_kernel,
        grid=(n // tm,),
        in_specs=[_row_spec(tm, D_MODEL), _const_spec((1, D_MODEL)),
                  _const_spec((D_MODEL, MLA_Q_RANK)), _const_spec((D_MODEL, MLA_KV_RANK)),
                  _const_spec((D_MODEL, LANES)), _const_spec((D_MODEL, LANES)),
                  _const_spec((D_MODEL, RWKV_PROJ)), _const_spec((1, MLA_Q_RANK)),
                  _const_spec((1, MLA_KV_RANK)), tab(LANES), tab(LANES),
                  _const_spec((MLA_Q_RANK, hq)), _const_spec((MLA_Q_RANK, hr)),
                  _const_spec((MLA_Q_RANK, hr)), tab(hr), tab(hr)],
        out_specs=[_row_spec(tm, MLA_LAT), _row_spec(tm, MLA_LATB), _row_spec(tm, hq),
                   _row_spec(tm, hr), _row_spec(tm, RWKV_PROJ)],
        out_shape=[jax.ShapeDtypeStruct((n, MLA_LAT), F32), jax.ShapeDtypeStruct((n, MLA_LATB), BF16),
                   jax.ShapeDtypeStruct((n, hq), BF16), jax.ShapeDtypeStruct((n, hr), BF16),
                   jax.ShapeDtypeStruct((n, RWKV_PROJ), F32)],
        compiler_params=_cparams(("parallel",)),
    )(x, w["norm_mix"], w["w_q"], w["w_ckv"], w["w_pe_a"], w["w_pe_b"], w["w_rw"], w["q_norm"],
      w["kv_norm"], tabs["cs"], tabs["sn"], w["w_qlat"], w["w_qpe_a"], w["w_qpe_b"],
      tabs["cs8"], tabs["sn8"])


ATT_TQ = 256


def _mla_prompt_kernel(qi_ref, kj_ref, ql_ref, qpe_ref, lat_ref, o_ref,
                       m_sc, l_sc, a_sc, acc_sc, s_sc, p_sc):
    step = pl.program_id(1)
    i = qi_ref[step]
    j = kj_ref[step]
    heads = range(MLA_HEADS)

    @pl.when(j == 0)
    def _():
        m_sc[...] = jnp.full(m_sc.shape, NEG_BIG, F32)
        l_sc[...] = jnp.zeros(l_sc.shape, F32)
        acc_sc[...] = jnp.zeros(acc_sc.shape, F32)

    def tile(masked):
        ckv = lat_ref[:, :MLA_KV_RANK]
        kpe = lat_ref[:, MLA_KV_RANK:]
        for h in heads:
            s_sc[h] = (_dot_nt(ql_ref[:, h * MLA_KV_RANK:(h + 1) * MLA_KV_RANK], ckv)
                       + _dot_nt(qpe_ref[:, h * LANES:(h + 1) * LANES], kpe))
        for h in heads:
            s = s_sc[h]
            if masked:
                tok = lax.broadcasted_iota(jnp.int32, s.shape, 0)
                key = lax.broadcasted_iota(jnp.int32, s.shape, 1)
                s = jnp.where(key <= tok, s, NEG_BIG)
            m_prev = m_sc[h]
            m_new = jnp.maximum(m_prev, jnp.max(s, axis=-1, keepdims=True))
            alpha = jnp.exp2(m_prev - m_new)
            p = jnp.exp2(s - _lane_tile(m_new, ATT_TQ))
            l_sc[h] = alpha * l_sc[h] + jnp.sum(p, axis=-1, keepdims=True)
            m_sc[h] = m_new
            a_sc[h] = alpha
            p_sc[h] = p.astype(BF16)
        for h in heads:
            acc_sc[h] = _lane_tile(a_sc[h], MLA_KV_RANK) * acc_sc[h] + _dot(p_sc[h], ckv)

    @pl.when(j < i)
    def _():
        tile(False)

    @pl.when(j == i)
    def _():
        tile(True)
        for h in heads:
            o_ref[:, h * MLA_KV_RANK:(h + 1) * MLA_KV_RANK] = (
                acc_sc[h] / _lane_tile(l_sc[h], MLA_KV_RANK)).astype(BF16)


def _mla_prompt(q_lat, q_pe, lat_b, batch, seq):
    nq = seq // ATT_TQ
    pairs = [(i, j) for i in range(nq) for j in range(i + 1)]
    qi = jnp.array([p[0] for p in pairs], jnp.int32)
    kj = jnp.array([p[1] for p in pairs], jnp.int32)
    hq = MLA_HEADS * MLA_KV_RANK
    grid_spec = pltpu.PrefetchScalarGridSpec(
        num_scalar_prefetch=2,
        grid=(batch, len(pairs)),
        in_specs=[pl.BlockSpec((ATT_TQ, hq), lambda b, s, qi, kj: (b * nq + qi[s], 0)),
                  pl.BlockSpec((ATT_TQ, MLA_HEADS * LANES), lambda b, s, qi, kj: (b * nq + qi[s], 0)),
                  pl.BlockSpec((ATT_TQ, MLA_LATB), lambda b, s, qi, kj: (b * nq + kj[s], 0))],
        out_specs=pl.BlockSpec((ATT_TQ, hq), lambda b, s, qi, kj: (b * nq + qi[s], 0)),
        scratch_shapes=[pltpu.VMEM((MLA_HEADS, ATT_TQ, LANES), F32),
                        pltpu.VMEM((MLA_HEADS, ATT_TQ, LANES), F32),
                        pltpu.VMEM((MLA_HEADS, ATT_TQ, LANES), F32),
                        pltpu.VMEM((MLA_HEADS, ATT_TQ, MLA_KV_RANK), F32),
                        pltpu.VMEM((MLA_HEADS, ATT_TQ, ATT_TQ), F32),
                        pltpu.VMEM((MLA_HEADS, ATT_TQ, ATT_TQ), BF16)],
    )
    return pl.pallas_call(
        _mla_prompt_kernel,
        grid_spec=grid_spec,
        out_shape=jax.ShapeDtypeStruct(q_lat.shape, BF16),
        compiler_params=_cparams(("parallel", "arbitrary")),
    )(qi, kj, q_lat, q_pe, lat_b)


PAGES_PER_STEP = 16


def _mla_decode_kernel(pt_ref, q_ref, new_ref, *rest):
    page_refs = rest[:PAGES_PER_STEP]
    o_ref, m_sc, l_sc, acc_sc = rest[PAGES_PER_STEP:]
    j = pl.program_id(1)
    q = q_ref[0]

    @pl.when(j == 0)
    def _():
        m_sc[...] = jnp.full(m_sc.shape, NEG_BIG, F32)
        l_sc[...] = jnp.zeros(l_sc.shape, F32)
        acc_sc[...] = jnp.zeros(acc_sc.shape, F32)

    def update(s, values_t):
        m_prev = m_sc[...]
        m_new = jnp.maximum(m_prev, jnp.max(s, axis=-1, keepdims=True))
        alpha = jnp.exp2(m_prev - m_new)
        p = jnp.exp2(s - _lane_tile(m_new, s.shape[1]))
        l_sc[...] = alpha * l_sc[...] + jnp.sum(p, axis=-1, keepdims=True)
        acc_sc[...] = _lane_tile(alpha, MLA_KV_RANK) * acc_sc[...] + _dot_nt(p, values_t)
        m_sc[...] = m_new

    keys_t = jnp.concatenate([pr[...].astype(BF16) for pr in page_refs], axis=1)
    update(_dot(q, keys_t), keys_t[:MLA_KV_RANK, :])

    @pl.when(j == pl.num_programs(1) - 1)
    def _():
        new_t = new_ref[0]
        sn = _dot(q, new_t)
        tok = lax.broadcasted_iota(jnp.int32, sn.shape, 0) >> 3
        key = lax.broadcasted_iota(jnp.int32, sn.shape, 1)
        sn = jnp.where(key <= tok, sn, NEG_BIG)
        update(sn, new_t[:MLA_KV_RANK, :])
        o_ref[0] = (acc_sc[...] / _lane_tile(l_sc[...], MLA_KV_RANK)).astype(BF16)


def _mla_decode(page_table, q_full, new_pad_t, cache_t, layer):
    db, n_pages = page_table.shape
    rows = q_full.shape[1]
    steps = n_pages // PAGES_PER_STEP

    def page_spec(p):
        return pl.BlockSpec((None, None, MLA_LAT, PAGE_SIZE),
                            lambda b, j, pt: (layer, pt[b, j * PAGES_PER_STEP + p], 0, 0))

    grid_spec = pltpu.PrefetchScalarGridSpec(
        num_scalar_prefetch=1,
        grid=(db, steps),
        in_specs=[pl.BlockSpec((1, rows, MLA_LAT), lambda b, j, pt: (b, 0, 0)),
                  pl.BlockSpec((1, MLA_LAT, PAGE_SIZE), lambda b, j, pt: (b, 0, 0))]
        + [page_spec(p) for p in range(PAGES_PER_STEP)],
        out_specs=pl.BlockSpec((1, rows, MLA_KV_RANK), lambda b, j, pt: (b, 0, 0)),
        scratch_shapes=[pltpu.VMEM((rows, LANES), F32), pltpu.VMEM((rows, LANES), F32),
                        pltpu.VMEM((rows, MLA_KV_RANK), F32)],
    )
    return pl.pallas_call(
        _mla_decode_kernel,
        grid_spec=grid_spec,
        out_shape=jax.ShapeDtypeStruct((db, rows, MLA_KV_RANK), BF16),
        compiler_params=_cparams(("parallel", "arbitrary")),
    )(page_table, q_full, new_pad_t, *([cache_t] * PAGES_PER_STEP))


def _rwkv_prep_kernel(rw_ref, before_ref, sh_ref, mu_ref, w0_ref, w2_ref, a0_ref, a2_ref, g2_ref,
                      kk_ref, ka_ref, rk_ref, ones_ref, xs_ref, v_ref, g_ref, rkv_ref, *, tm, seq):
    rw = rw_ref[...]
    rolled = pltpu.roll(rw, 1, axis=0)
    row = lax.broadcasted_iota(jnp.int32, rw.shape, 0)
    if seq >= tm:
        at_start = pl.program_id(0) % (seq // tm) == 0
        first = jnp.where(at_start, sh_ref[...], before_ref[7:8, :])
        prev = jnp.where(row == 0, first, rolled)
    else:
        prev = jnp.where((row & (seq - 1)) == 0, sh_ref[...], rolled)
    xs = rw + (prev - rw) * mu_ref[...]
    d = RWKV_DIM
    r = xs[:, :d]
    k = xs[:, d:2 * d]
    v = xs[:, 2 * d:3 * d]
    xwa = xs[:, 3 * d:3 * d + LANES]
    xg = xs[:, 3 * d + LANES:]
    ones = ones_ref[...]
    w_log = -_softplus(-(w0_ref[...] + _dot(jnp.tanh(xwa), w2_ref[...]))) - 0.5
    a = _sigmoid(a0_ref[...] + _dot(xwa, a2_ref[...]))
    g_ref[...] = _dot(_sigmoid(xg), g2_ref[...])
    kk = k * kk_ref[...]
    ss = _dot_exact_rhs(kk * kk, ones)
    kk = kk / jnp.maximum(jnp.sqrt(ss), 1e-12)
    k2 = k * (1.0 + (a - 1.0) * ka_ref[...])
    xs_ref[0] = -kk
    xs_ref[1] = jnp.exp(-jnp.exp(w_log))
    xs_ref[2] = kk * a
    xs_ref[3] = k2
    xs_ref[4] = r
    v_ref[...] = v
    rkv_ref[...] = _dot_exact_rhs(r * k2 * rk_ref[...], ones) * v


def _rwkv_prep(rw, shift0, w, tm, seq):
    n = rw.shape[0]
    d = RWKV_DIM
    vec = _const_spec((1, d))
    if seq >= tm:
        tiles = seq // tm
        sh = shift0.reshape(-1, 1, RWKV_PROJ)
        sh_spec = pl.BlockSpec((None, 1, RWKV_PROJ), lambda i: (i // tiles, 0, 0))
    else:
        sh = jnp.repeat(shift0, seq, axis=0)
        sh_spec = _row_spec(tm, RWKV_PROJ)
    before_spec = pl.BlockSpec((8, RWKV_PROJ), lambda i: (jnp.maximum(i * (tm // 8) - 1, 0), 0))
    return pl.pallas_call(
        functools.partial(_rwkv_prep_kernel, tm=tm, seq=seq),
        grid=(n // tm,),
        in_specs=[_row_spec(tm, RWKV_PROJ), before_spec, sh_spec, _const_spec((1, RWKV_PROJ)),
                  vec, _const_spec((LANES, d)), vec, _const_spec((LANES, d)),
                  _const_spec((RWKV_G_LORA, d)), vec, vec, vec, _const_spec((d, d))],
        out_specs=[pl.BlockSpec((5, tm, d), lambda i: (0, i, 0))] + [_row_spec(tm, d)] * 3,
        out_shape=[jax.ShapeDtypeStruct((5, n, d), F32)] + [jax.ShapeDtypeStruct((n, d), F32)] * 3,
        compiler_params=_cparams(("parallel",)),
    )(rw, rw, sh, w["mu"], w["w0"], w["w2p"], w["a0"], w["a2p"], w["g2"], w["k_k"], w["k_a"],
      w["r_k"], w["ones_bd"])


SCAN_KH = RWKV_HEAD // 2
SCAN_PAIRS = LANES // 2
SCAN_VR = RWKV_HEAD // 2


def _rwkv_scan_kernel(x_ref, v_ref, s0_ref, y_ref, s_ref, c_sc, d_sc, *, tc):
    @pl.when(pl.program_id(1) == 0)
    def _():
        s_ref[...] = s0_ref[...]

    half_a = slice(0, SCAN_VR)
    half_b = slice(SCAN_VR, RWKV_HEAD)

    def both_halves(p):
        return p + pltpu.roll(p, SCAN_PAIRS, axis=1)

    def key_dot(u, w):
        return both_halves(jnp.sum(u * w, axis=0, keepdims=True))

    def first_partial(rows):
        p = s_ref[0, 0, rows, :] * x_ref[0, 0, 0, 0:1, :]
        for k in range(1, SCAN_KH):
            p = p + s_ref[0, k, rows, :] * x_ref[0, 0, 0, k:k + 1, :]
        return p

    def half_step(t, rows, sa):
        v_half = v_ref[0, t, rows, :]
        v = jnp.concatenate([v_half, v_half], axis=1)
        q = None
        y = None
        for k in range(SCAN_KH):
            s_old = s_ref[0, k, rows, :]
            qk = s_old * c_sc[k:k + 1, :]
            sn = (s_old * x_ref[1, 0, t, k:k + 1, :] + sa * x_ref[2, 0, t, k:k + 1, :]
                  + v * x_ref[3, 0, t, k:k + 1, :])
            s_ref[0, k, rows, :] = sn
            yk = sn * x_ref[4, 0, t, k:k + 1, :]
            q = qk if q is None else q + qk
            y = yk if y is None else y + yk
        y_ref[0, t, rows, :] = y
        return q, sa * d_sc[0:1, :] + v * d_sc[1:2, :]

    def step(t, carry):
        sa_a, q_b, corr_b = carry
        a_next = x_ref[0, 0, jnp.minimum(t + 1, tc - 1)]
        c_sc[...] = x_ref[1, 0, t] * a_next
        d_sc[0:1, :] = key_dot(x_ref[2, 0, t], a_next)
        d_sc[1:2, :] = key_dot(x_ref[3, 0, t], a_next)
        sa_b = both_halves(q_b) + corr_b
        q_a, corr_a = half_step(t, half_a, sa_a)
        sa_a_next = both_halves(q_a) + corr_a
        q_b_next, corr_b_next = half_step(t, half_b, sa_b)
        return sa_a_next, q_b_next, corr_b_next

    init = (both_halves(first_partial(half_a)), first_partial(half_b),
            jnp.zeros((SCAN_VR, LANES), F32))
    lax.fori_loop(0, tc, step, init)


def _rwkv_scan(xs, v, s0, tc):
    _, nb, t, _, _ = xs.shape
    xspec = pl.BlockSpec((5, 1, tc, SCAN_KH, LANES), lambda n, c: (0, n, c, 0, 0))
    vspec = pl.BlockSpec((1, tc, RWKV_HEAD, SCAN_PAIRS), lambda n, c: (n, c, 0, 0))
    yspec = pl.BlockSpec((1, tc, RWKV_HEAD, LANES), lambda n, c: (n, c, 0, 0))
    sspec = pl.BlockSpec((1, SCAN_KH, RWKV_HEAD, LANES), lambda n, c: (n, 0, 0, 0))
    return pl.pallas_call(
        functools.partial(_rwkv_scan_kernel, tc=tc),
        grid=(nb, t // tc),
        in_specs=[xspec, vspec, sspec],
        out_specs=[yspec, sspec],
        out_shape=[jax.ShapeDtypeStruct((nb, t, RWKV_HEAD, LANES), F32),
                   jax.ShapeDtypeStruct(s0.shape, F32)],
        scratch_shapes=[pltpu.VMEM((SCAN_KH, LANES), F32), pltpu.VMEM((8, LANES), F32)],
        compiler_params=_cparams(("parallel", "arbitrary")),
    )(xs, v, s0)


def _even_out_kernel(y_ref, rkv_ref, g_ref, lng_ref, lnb_ref, ones_ref, ol_ref, wuv_ref, woa_ref,
                     wob_ref, x_ref, o_ref):
    ones = ones_ref[...]
    y = y_ref[...]
    inv = 1.0 / RWKV_HEAD
    mean = _dot_exact_rhs(y, ones) * inv
    dlt = y - mean
    var = _dot_exact_rhs(dlt * dlt, ones) * inv
    yn = dlt * lax.rsqrt(var + RWKV_LN_EPS) * lng_ref[...] + lnb_ref[...] + rkv_ref[...]
    ob = (yn * g_ref[...]).astype(BF16)
    pair = 2 * MLA_KV_RANK
    oa = jnp.concatenate(
        [_dot(ol_ref[:, p * pair:(p + 1) * pair], wuv_ref[p]) for p in range(MLA_HEADS // 2)], axis=1)
    o_ref[...] = x_ref[...] + _dot(oa, woa_ref[...]) + _dot(ob, wob_ref[...])


def _even_out(y, rkv, g, o_lat, x, w, tm):
    n = x.shape[0]
    d = RWKV_DIM
    hq = MLA_HEADS * MLA_KV_RANK
    return pl.pallas_call(
        _even_out_kernel,
        grid=(n // tm,),
        in_specs=[_row_spec(tm, d), _row_spec(tm, d), _row_spec(tm, d), _const_spec((1, d)),
                  _const_spec((1, d)), _const_spec((d, d)), _row_spec(tm, hq),
                  _const_spec((MLA_HEADS // 2, 2 * MLA_KV_RANK, 2 * MLA_V)),
                  _const_spec((MLA_HEADS * MLA_V, D_MODEL)), _const_spec((d, D_MODEL)),
                  _row_spec(tm, D_MODEL)],
        out_specs=_row_spec(tm, D_MODEL),
        out_shape=jax.ShapeDtypeStruct((n, D_MODEL), F32),
        compiler_params=_cparams(("parallel",)),
    )(y, rkv, g, w["ln_g"], w["ln_b"], w["ones_bd"], o_lat, w["w_uv_bd"], w["w_out_a"],
      w["w_out_b"], x)


FFN_TF = 1408


def _ffn_kernel(x_ref, g_ref, wg_ref, wu_ref, wd_ref, o_ref, xn_sc, acc_sc):
    f = pl.program_id(1)

    @pl.when(f == 0)
    def _():
        xn_sc[...] = _rms(x_ref[...], g_ref[...]).astype(BF16)
        acc_sc[...] = jnp.zeros(acc_sc.shape, F32)

    xn = xn_sc[...]
    gate = _dot(xn, wg_ref[...])
    up = _dot(xn, wu_ref[...])
    acc_sc[...] += _dot(gate * _sigmoid(gate) * up, wd_ref[...])

    @pl.when(f == pl.num_programs(1) - 1)
    def _():
        o_ref[...] = x_ref[...] + acc_sc[...]


def _ffn(x, g, w_gu, w_down, tm):
    n = x.shape[0]
    nf = D_FF // FFN_TF
    return pl.pallas_call(
        _ffn_kernel,
        grid=(n // tm, nf),
        in_specs=[pl.BlockSpec((tm, D_MODEL), lambda i, f: (i, 0)),
                  pl.BlockSpec((1, D_MODEL), lambda i, f: (0, 0)),
                  pl.BlockSpec((D_MODEL, FFN_TF), lambda i, f: (0, f)),
                  pl.BlockSpec((D_MODEL, FFN_TF), lambda i, f: (0, nf + f)),
                  pl.BlockSpec((FFN_TF, D_MODEL), lambda i, f: (f, 0))],
        out_specs=pl.BlockSpec((tm, D_MODEL), lambda i, f: (i, 0)),
        out_shape=jax.ShapeDtypeStruct((n, D_MODEL), F32),
        scratch_shapes=[pltpu.VMEM((tm, D_MODEL), BF16), pltpu.VMEM((tm, D_MODEL), F32)],
        compiler_params=_cparams(("parallel", "arbitrary")),
    )(x, g, w_gu, w_gu, w_down)


def _odd_in_kernel(x_ref, g_ref, wq_ref, wk_ref, wv_ref, wg_ref, wxa_ref, a2_ref, ab_ref,
                   q_ref, k_ref, v_ref, gate_ref, la_ref):
    xn = _rms(x_ref[...], g_ref[...]).astype(BF16)
    q_ref[...] = _dot(xn, wq_ref[...]) * (GLA_DK ** -0.5)
    k_ref[...] = _dot(xn, wk_ref[...])
    v_ref[...] = _dot(xn, wv_ref[...])
    gate_ref[...] = _dot(xn, wg_ref[...])
    z = _dot(_dot(xn, wxa_ref[...]), a2_ref[...]) + ab_ref[...]
    la_ref[...] = -_softplus(-z) * (1.0 / GLA_GATE_NORM)


def _odd_in(x, w, tm):
    n = x.shape[0]
    return pl.pallas_call(
        _odd_in_kernel,
        grid=(n // tm,),
        in_specs=[_row_spec(tm, D_MODEL), _const_spec((1, D_MODEL)),
                  _const_spec((D_MODEL, GLA_KDIM)), _const_spec((D_MODEL, GLA_KDIM)),
                  _const_spec((D_MODEL, GLA_VDIM)), _const_spec((D_MODEL, GLA_VDIM)),
                  _const_spec((D_MODEL, LANES)), _const_spec((LANES, GLA_KDIM)),
                  _const_spec((1, GLA_KDIM))],
        out_specs=[_row_spec(tm, GLA_KDIM), _row_spec(tm, GLA_KDIM), _row_spec(tm, GLA_VDIM),
                   _row_spec(tm, GLA_VDIM), _row_spec(tm, GLA_KDIM)],
        out_shape=[jax.ShapeDtypeStruct((n, GLA_KDIM), F32), jax.ShapeDtypeStruct((n, GLA_KDIM), F32),
                   jax.ShapeDtypeStruct((n, GLA_VDIM), F32), jax.ShapeDtypeStruct((n, GLA_VDIM), F32),
                   jax.ShapeDtypeStruct((n, GLA_KDIM), F32)],
        compiler_params=_cparams(("parallel",)),
    )(x, w["norm_mix"], w["w_q"], w["w_k"], w["w_v"], w["w_g"], w["w_xa"], w["a2p"], w["ab"])


def _gla_kernel(q_ref, k_ref, v_ref, la_ref, s0_ref, o_ref, st_ref):
    c = GLA_CHUNK

    @pl.when(pl.program_id(1) == 0)
    def _():
        st_ref[...] = s0_ref[...]

    row = lax.broadcasted_iota(jnp.int32, (c, c), 0)
    col = lax.broadcasted_iota(jnp.int32, (c, c), 1)
    tri = row >= col
    tri_b = jnp.where(tri, 1.0, 0.0).astype(BF16)
    for h in range(GLA_HEADS):
        ks = slice(h * GLA_DK, (h + 1) * GLA_DK)
        vs = slice(h * GLA_DV, (h + 1) * GLA_DV)
        b = _dot_exact_lhs(tri_b, la_ref[:, ks])
        q = q_ref[:, ks]
        k = k_ref[:, ks]
        v = v_ref[:, vs]
        b_end = b[c - 1:c, :]
        qe = (q * jnp.exp(b)).astype(BF16)
        ke = (k * jnp.exp(-b)).astype(BF16)
        a_mat = jnp.where(tri, _dot_nt(qe, ke), 0.0)
        st = st_ref[0, h]
        o_ref[:, vs] = _dot_nt(qe, st) + _dot(a_mat, v)
        k_end = k * jnp.exp(b_end - b)
        st_ref[0, h] = st * jnp.exp(b_end) + _dot(v.T, k_end)


def _gla(q, k, v, la, s0t, batch, seq):
    nc = seq // GLA_CHUNK
    rspec = lambda width: pl.BlockSpec((GLA_CHUNK, width), lambda b, c: (b * nc + c, 0))
    sspec = pl.BlockSpec((1, GLA_HEADS, GLA_DV, GLA_DK), lambda b, c: (b, 0, 0, 0))
    return pl.pallas_call(
        _gla_kernel,
        grid=(batch, nc),
        in_specs=[rspec(GLA_KDIM), rspec(GLA_KDIM), rspec(GLA_VDIM), rspec(GLA_KDIM), sspec],
        out_specs=[rspec(GLA_VDIM), sspec],
        out_shape=[jax.ShapeDtypeStruct(v.shape, F32), jax.ShapeDtypeStruct(s0t.shape, F32)],
        compiler_params=_cparams(("parallel", "arbitrary")),
    )(q, k, v, la, s0t)


def _odd_out_kernel(o_ref, gate_ref, gn_ref, wo_ref, x_ref, y_ref):
    parts = []
    for h in range(GLA_HEADS):
        vs = slice(h * GLA_DV, (h + 1) * GLA_DV)
        parts.append(_rms(o_ref[:, vs], gn_ref[:, vs]))
    gate = gate_ref[...]
    on = jnp.concatenate(parts, axis=1) * (gate * _sigmoid(gate))
    y_ref[...] = x_ref[...] + _dot(on, wo_ref[...])


def _odd_out(o, gate, x, w, tm):
    n = x.shape[0]
    return pl.pallas_call(
        _odd_out_kernel,
        grid=(n // tm,),
        in_specs=[_row_spec(tm, GLA_VDIM), _row_spec(tm, GLA_VDIM), _const_spec((1, GLA_VDIM)),
                  _const_spec((GLA_VDIM, D_MODEL)), _row_spec(tm, D_MODEL)],
        out_specs=_row_spec(tm, D_MODEL),
        out_shape=jax.ShapeDtypeStruct((n, D_MODEL), F32),
        compiler_params=_cparams(("parallel",)),
    )(o, gate, w["gla_norm"], w["w_out"], x)


def _pack_bf16_pairs(lo, hi):
    lo_bits = pltpu.bitcast(lo.astype(BF16).astype(F32), jnp.uint32)
    hi_bits = pltpu.bitcast(hi.astype(BF16).astype(F32), jnp.uint32)
    return (lo_bits >> 16) | (hi_bits & jnp.uint32(0xFFFF0000))


def _unpack_bf16_pairs(u):
    lo = pltpu.bitcast(u << 16, F32)
    hi = pltpu.bitcast(u & jnp.uint32(0xFFFF0000), F32)
    return lo, hi


def _router_kernel(x_ref, g_ref, wr_ref, xp_ref, idx_ref, gate_ref):
    xn = _rms(x_ref[...], g_ref[...])
    half = D_MODEL // 2
    xp_ref[...] = _pack_bf16_pairs(xn[:, :half], xn[:, half:])
    logits = _dot_f32ish(xn, wr_ref[...])
    lane = lax.broadcasted_iota(jnp.int32, logits.shape, 1)
    logits = jnp.where(lane < N_EXPERTS, logits, NEG_BIG)
    m1 = jnp.max(logits, axis=-1, keepdims=True)
    i1 = jnp.min(jnp.where(logits == m1, lane, LANES), axis=-1, keepdims=True)
    rest = jnp.where(lane == i1, NEG_BIG, logits)
    m2 = jnp.max(rest, axis=-1, keepdims=True)
    i2 = jnp.min(jnp.where(rest == m2, lane, LANES), axis=-1, keepdims=True)
    e2 = jnp.exp(m2 - m1)
    g1 = 1.0 / (1.0 + e2)
    g2 = e2 / (1.0 + e2)
    idx_ref[...] = jnp.where(lane == 0, i1, jnp.where(lane == 1, i2, 0))
    gate_ref[...] = jnp.where(lane == 0, g1, jnp.where(lane == 1, g2, 0.0))


def _router(x, g, wr, tm):
    n = x.shape[0]
    return pl.pallas_call(
        _router_kernel,
        grid=(n // tm,),
        in_specs=[_row_spec(tm, D_MODEL), _const_spec((1, D_MODEL)), _const_spec((D_MODEL, LANES))],
        out_specs=[_row_spec(tm, D_MODEL // 2), _row_spec(tm, LANES), _row_spec(tm, LANES)],
        out_shape=[jax.ShapeDtypeStruct((n, D_MODEL // 2), jnp.uint32),
                   jax.ShapeDtypeStruct((n, LANES), jnp.int32), jax.ShapeDtypeStruct((n, LANES), F32)],
        compiler_params=_cparams(("parallel",)),
    )(x, g, wr)


MOE_TF = 1792
MOE_CHUNK = 2 * LANES


def _route(top_i, tm):
    n = top_i.shape[0]
    slots = 2 * n
    n_tiles = -(-(slots + N_EXPERTS * (tm - 1)) // tm)
    e_flat = top_i.reshape(-1)
    onehot = (e_flat[:, None] == jnp.arange(N_EXPERTS, dtype=jnp.int32)[None, :]).astype(jnp.int32)
    csum = jnp.cumsum(onehot, axis=0)
    rank = jnp.sum(onehot * csum, axis=1) - 1
    counts = csum[-1]
    padded = ((counts + tm - 1) // tm) * tm
    ends = jnp.cumsum(padded)
    starts = ends - padded
    dest = (jnp.sum(onehot * starts[None, :], axis=1) + rank).astype(jnp.int32)
    tile_start = jnp.arange(n_tiles, dtype=jnp.int32) * tm
    tile_expert = jnp.minimum(jnp.sum((tile_start[:, None] >= ends[None, :]).astype(jnp.int32), axis=1),
                              N_EXPERTS - 1).astype(jnp.int32)
    tile_valid = (tile_start < ends[-1]).astype(jnp.int32)
    src = jnp.zeros((n_tiles * tm,), jnp.int32).at[dest].set(jnp.arange(slots, dtype=jnp.int32) // 2)
    return dest, src, tile_expert, tile_valid


def _moe_gather_kernel(src_ref, x_ref, o_ref, *, tg):
    base = pl.program_id(0) * tg

    def body(r, carry):
        o_ref[pl.ds(r, 1), :] = x_ref[pl.ds(src_ref[base + r], 1), :]
        return carry

    lax.fori_loop(0, tg, body, 0, unroll=8)


def _moe_gather(src, xp, tg):
    rows = src.shape[0]
    width = xp.shape[1]
    grid_spec = pltpu.PrefetchScalarGridSpec(
        num_scalar_prefetch=1,
        grid=(rows // tg,),
        in_specs=[pl.BlockSpec(memory_space=pltpu.VMEM)],
        out_specs=pl.BlockSpec((tg, width), lambda i, s: (i, 0)),
    )
    return pl.pallas_call(
        functools.partial(_moe_gather_kernel, tg=tg),
        grid_spec=grid_spec,
        out_shape=jax.ShapeDtypeStruct((rows, width), jnp.uint32),
        compiler_params=_cparams(("arbitrary",)),
    )(src, xp)


def _moe_up_kernel(te_ref, tv_ref, xs_ref, wg_ref, wu_ref, h_ref):
    @pl.when(tv_ref[pl.program_id(1)] != 0)
    def _():
        lo, hi = _unpack_bf16_pairs(xs_ref[...])
        half = D_MODEL // 2
        gate = _dot(lo, wg_ref[:half, :]) + _dot(hi, wg_ref[half:, :])
        up = _dot(lo, wu_ref[:half, :]) + _dot(hi, wu_ref[half:, :])
        h_ref[...] = (gate * _sigmoid(gate) * up).astype(BF16)


def _moe_up(te, tv, xs, w_gu, layer, tm):
    rows = xs.shape[0]
    nf = D_FF_EXPERT // MOE_TF
    wspec = lambda off: pl.BlockSpec((None, None, D_MODEL, MOE_TF),
                                     lambda f, t, te, tv: (layer, te[t], 0, off + f))
    grid_spec = pltpu.PrefetchScalarGridSpec(
        num_scalar_prefetch=2,
        grid=(nf, rows // tm),
        in_specs=[pl.BlockSpec((tm, D_MODEL // 2), lambda f, t, te, tv: (t, 0)), wspec(0), wspec(nf)],
        out_specs=pl.BlockSpec((tm, MOE_TF), lambda f, t, te, tv: (t, f)),
    )
    return pl.pallas_call(
        _moe_up_kernel,
        grid_spec=grid_spec,
        out_shape=jax.ShapeDtypeStruct((rows, D_FF_EXPERT), BF16),
        compiler_params=_cparams(("arbitrary", "arbitrary")),
    )(te, tv, xs, w_gu, w_gu)


def _moe_down_kernel(te_ref, tv_ref, h_ref, wd_ref, y_ref):
    @pl.when(tv_ref[pl.program_id(0)] != 0)
    def _():
        y = _dot(h_ref[...], wd_ref[...])
        parts = []
        for c in range(D_MODEL // MOE_CHUNK):
            lo = y[:, c * MOE_CHUNK:c * MOE_CHUNK + LANES]
            hi = y[:, c * MOE_CHUNK + LANES:(c + 1) * MOE_CHUNK]
            parts.append(_pack_bf16_pairs(lo, hi))
        y_ref[...] = jnp.concatenate(parts, axis=1)


def _moe_down(te, tv, h, w_down, layer, tm):
    rows = h.shape[0]
    grid_spec = pltpu.PrefetchScalarGridSpec(
        num_scalar_prefetch=2,
        grid=(rows // tm,),
        in_specs=[pl.BlockSpec((tm, D_FF_EXPERT), lambda t, te, tv: (t, 0)),
                  pl.BlockSpec((None, None, D_FF_EXPERT, D_MODEL),
                               lambda t, te, tv: (layer, te[t], 0, 0))],
        out_specs=pl.BlockSpec((tm, D_MODEL // 2), lambda t, te, tv: (t, 0)),
    )
    return pl.pallas_call(
        _moe_down_kernel,
        grid_spec=grid_spec,
        out_shape=jax.ShapeDtypeStruct((rows, D_MODEL // 2), jnp.uint32),
        compiler_params=_cparams(("arbitrary",)),
    )(te, tv, h, w_down)


def _moe_combine_kernel(dest_ref, ys_ref, g1_ref, g2_ref, x_ref, o_ref, *, tmc):
    base = 2 * pl.program_id(1) * tmc

    def body(r, carry):
        row = pl.ds(r, 1)
        lo1, hi1 = _unpack_bf16_pairs(ys_ref[pl.ds(dest_ref[base + 2 * r], 1), :])
        lo2, hi2 = _unpack_bf16_pairs(ys_ref[pl.ds(dest_ref[base + 2 * r + 1], 1), :])
        g1 = g1_ref[row, :]
        g2 = g2_ref[row, :]
        moe = jnp.concatenate([g1 * lo1 + g2 * lo2, g1 * hi1 + g2 * hi2], axis=1)
        o_ref[row, :] = x_ref[row, :] + moe
        return carry

    lax.fori_loop(0, tmc, body, 0, unroll=8)


def _moe_combine(dest, ys, g1b, g2b, x, tmc):
    n = x.shape[0]
    rows = ys.shape[0]
    grid_spec = pltpu.PrefetchScalarGridSpec(
        num_scalar_prefetch=1,
        grid=(D_MODEL // MOE_CHUNK, n // tmc),
        in_specs=[pl.BlockSpec((rows, LANES), lambda c, i, d: (0, c)),
                  pl.BlockSpec((tmc, LANES), lambda c, i, d: (i, 0)),
                  pl.BlockSpec((tmc, LANES), lambda c, i, d: (i, 0)),
                  pl.BlockSpec((tmc, MOE_CHUNK), lambda c, i, d: (i, c))],
        out_specs=pl.BlockSpec((tmc, MOE_CHUNK), lambda c, i, d: (i, c)),
    )
    return pl.pallas_call(
        functools.partial(_moe_combine_kernel, tmc=tmc),
        grid_spec=grid_spec,
        out_shape=jax.ShapeDtypeStruct((n, D_MODEL), F32),
        compiler_params=_cparams(("arbitrary", "arbitrary")),
    )(dest, ys, g1b, g2b, x)


def _final_norm_kernel(x_ref, g_ref, o_ref):
    o_ref[...] = _rms(x_ref[...], g_ref[...])


def _final_norm(x, g, tm):
    n = x.shape[0]
    return pl.pallas_call(
        _final_norm_kernel,
        grid=(n // tm,),
        in_specs=[_row_spec(tm, D_MODEL), _const_spec((1, D_MODEL))],
        out_specs=_row_spec(tm, D_MODEL),
        out_shape=jax.ShapeDtypeStruct((n, D_MODEL), F32),
        compiler_params=_cparams(("parallel",)),
    )(x, g)


def _scan_vec_layout(xs, batch, seq):
    nb = batch * RWKV_HEADS // SCAN_PAIRS
    x = xs.reshape(5, batch, seq, RWKV_HEADS, 2, SCAN_KH).transpose(0, 2, 5, 4, 1, 3)
    x = x.reshape(5, seq, SCAN_KH, 2, nb, SCAN_PAIRS).transpose(0, 4, 1, 2, 3, 5)
    return x.reshape(5, nb, seq, SCAN_KH, LANES)


def _scan_val_layout(v, batch, seq):
    nb = batch * RWKV_HEADS // SCAN_PAIRS
    v4 = v.reshape(batch, seq, RWKV_HEADS, RWKV_HEAD).transpose(1, 3, 0, 2)
    return v4.reshape(seq, RWKV_HEAD, nb, SCAN_PAIRS).transpose(2, 0, 1, 3)


def _scan_val_unlayout(y, batch, seq):
    y = y[..., :SCAN_PAIRS] + y[..., SCAN_PAIRS:]
    v4 = y.transpose(1, 2, 0, 3).reshape(seq, RWKV_HEAD, batch, RWKV_HEADS)
    return v4.transpose(2, 0, 3, 1).reshape(batch * seq, RWKV_DIM)


def _scan_state_layout(s, batch):
    nb = batch * RWKV_HEADS // SCAN_PAIRS
    s6 = s.reshape(batch, RWKV_HEADS, RWKV_HEAD, 2, SCAN_KH).transpose(4, 2, 3, 0, 1)
    s6 = s6.reshape(SCAN_KH, RWKV_HEAD, 2, nb, SCAN_PAIRS).transpose(3, 0, 1, 2, 4)
    return s6.reshape(nb, SCAN_KH, RWKV_HEAD, LANES)


def _scan_state_unlayout(arr, batch):
    nb = arr.shape[0]
    s = arr.reshape(nb, SCAN_KH, RWKV_HEAD, 2, SCAN_PAIRS).transpose(0, 4, 2, 3, 1)
    return s.reshape(batch, RWKV_HEADS, RWKV_HEAD, RWKV_HEAD)


def _swap_halves(w):
    half = w.shape[-1] // 2
    return jnp.concatenate([w[..., half:], w[..., :half]], axis=-1)


def _prep_even(i, norm_mix, norm_ffn, w_in, q_norm, kv_norm, w_uq, w_uk, w_uv, mu, w0, w2, a0, a2,
               g2, k_k, k_a, r_k, ln_g, ln_b, w_out, ffn_gu, ffn_down):
    w = {}
    row = lambda v: v[i].reshape(1, -1)
    w_in = w_in[i]
    w["norm_mix"] = row(norm_mix)
    w["norm_ffn"] = row(norm_ffn)
    w["w_q"] = w_in[:, :MLA_Q_RANK].astype(BF16)
    w_kv = w_in[:, MLA_Q_RANK:MLA_Q_RANK + MLA_LAT]
    w["w_ckv"] = w_kv[:, :MLA_KV_RANK].astype(BF16)
    lane_pad = lambda m: jnp.pad(m, [(0, 0)] * (m.ndim - 1) + [(0, LANES - m.shape[-1])])
    w["w_pe_a"] = lane_pad(w_kv[:, MLA_KV_RANK:]).astype(BF16)
    w["w_pe_b"] = lane_pad(_swap_halves(w_kv[:, MLA_KV_RANK:])).astype(BF16)
    w["w_rw"] = w_in[:, MLA_Q_RANK + MLA_LAT:].astype(BF16)
    w["q_norm"] = row(q_norm)
    w["kv_norm"] = row(kv_norm)
    uq = w_uq[i].reshape(MLA_Q_RANK, MLA_HEADS, MLA_NOPE + MLA_ROPE)
    uq_pe = uq[:, :, MLA_NOPE:]
    w["w_qpe_a"] = lane_pad(uq_pe).reshape(MLA_Q_RANK, -1).astype(BF16)
    w["w_qpe_b"] = lane_pad(_swap_halves(uq_pe)).reshape(MLA_Q_RANK, -1).astype(BF16)
    w["w_qlat"] = _fold_qlat(uq[:, :, :MLA_NOPE].transpose(1, 0, 2), w_uk[i].transpose(1, 0, 2))
    uv = w_uv[i].transpose(1, 0, 2).reshape(MLA_HEADS // 2, 2, MLA_KV_RANK, MLA_V)
    zero = jnp.zeros_like(uv[:, 0])
    w["w_uv_bd"] = jnp.concatenate(
        [jnp.concatenate([uv[:, 0], zero], axis=-1), jnp.concatenate([zero, uv[:, 1]], axis=-1)],
        axis=1).astype(BF16)
    w["mu"] = row(mu)
    w["w0"] = row(w0)
    pad = lambda m, before: jnp.pad(m, ((before, LANES - before - m.shape[0]), (0, 0))).astype(BF16)
    w["w2p"] = pad(w2[i], 0)
    w["a2p"] = pad(a2[i], RWKV_W_LORA)
    w["a0"] = row(a0)
    w["g2"] = g2[i].astype(BF16)
    w["k_k"] = row(k_k)
    w["k_a"] = row(k_a)
    w["r_k"] = row(r_k)
    w["ln_g"] = row(ln_g)
    w["ln_b"] = row(ln_b)
    head = jnp.arange(RWKV_DIM) // RWKV_HEAD
    w["ones_bd"] = (head[:, None] == head[None, :]).astype(BF16)
    w["w_out_a"] = w_out[i][:MLA_HEADS * MLA_V].astype(BF16)
    w["w_out_b"] = w_out[i][MLA_HEADS * MLA_V:].astype(BF16)
    w["ffn_gu"] = ffn_gu[i].astype(BF16)
    w["ffn_down"] = ffn_down[i].astype(BF16)
    return w


def _prep_odd(i, norm_mix, norm_ffn, w_in, a2, ab, gla_norm, w_out, router, moe_gu, moe_down):
    w = {}
    row = lambda v: v[i].reshape(1, -1)
    w_in = w_in[i]
    w["norm_mix"] = row(norm_mix)
    w["norm_ffn"] = row(norm_ffn)
    w["w_q"] = w_in[:, :GLA_KDIM].astype(BF16)
    w["w_k"] = w_in[:, GLA_KDIM:2 * GLA_KDIM].astype(BF16)
    w["w_v"] = w_in[:, 2 * GLA_KDIM:2 * GLA_KDIM + GLA_VDIM].astype(BF16)
    w["w_g"] = w_in[:, 2 * GLA_KDIM + GLA_VDIM:2 * GLA_KDIM + 2 * GLA_VDIM].astype(BF16)
    w["w_xa"] = jnp.pad(w_in[:, 2 * GLA_KDIM + 2 * GLA_VDIM:],
                        ((0, 0), (0, LANES - GLA_GATE_RANK))).astype(BF16)
    w["a2p"] = jnp.pad(a2[i], ((0, LANES - GLA_GATE_RANK), (0, 0))).astype(BF16)
    w["ab"] = row(ab)
    w["gla_norm"] = row(gla_norm)
    w["w_out"] = w_out[i].astype(BF16)
    w["router"] = jnp.pad(router[i], ((0, 0), (0, LANES - N_EXPERTS)))
    w["layer"] = i
    w["moe_gu"] = moe_gu
    w["moe_down"] = moe_down
    return w


def _rope_tables(pos, reps):
    inv = ROPE_THETA ** (-jnp.arange(0, MLA_ROPE, 2, dtype=F32) / MLA_ROPE)
    ang = pos.astype(F32)[:, None] * inv[None, :]
    cos, sin = jnp.cos(ang), jnp.sin(ang)
    pad = ((0, 0), (0, LANES - MLA_ROPE))
    cs = jnp.tile(jnp.pad(jnp.concatenate([cos, cos], axis=-1), pad), (reps, 1))
    sn = jnp.tile(jnp.pad(jnp.concatenate([-sin, sin], axis=-1), pad), (reps, 1))
    return {"cs": cs, "sn": sn, "cs8": jnp.tile(cs, (1, MLA_HEADS)), "sn8": jnp.tile(sn, (1, MLA_HEADS))}


def _even_layer(x, batch, seq, tabs, state, shift0, past, w, tm, tc):
    n = batch * seq
    lat, lat_b, q_lat, q_pe, rw = _even_in(x, w, tabs, tm)
    if past is None:
        o_lat = _mla_prompt(q_lat, q_pe, lat_b, batch, seq)
    else:
        cache, layer, page_table = past
        rows = seq * MLA_HEADS
        q_full = jnp.concatenate([q_lat.reshape(batch, rows, MLA_KV_RANK),
                                  q_pe.reshape(batch, rows, LANES)[:, :, :MLA_ROPE]], axis=-1)
        new_pad_t = jnp.pad(lat_b.reshape(batch, seq, MLA_LATB)[:, :, :MLA_LAT],
                            ((0, 0), (0, PAGE_SIZE - seq), (0, 0))).transpose(0, 2, 1)
        o_lat = _mla_decode(page_table, q_full, new_pad_t, cache.transpose(0, 1, 3, 2), layer)
        o_lat = o_lat.reshape(n, MLA_HEADS * MLA_KV_RANK)

    rw3 = rw.reshape(batch, seq, RWKV_PROJ)
    xs5, v, g, rkv = _rwkv_prep(rw, shift0, w, tm, seq)
    y_l, s_l = _rwkv_scan(_scan_vec_layout(xs5, batch, seq), _scan_val_layout(v, batch, seq),
                          _scan_state_layout(state, batch), tc)
    y = _scan_val_unlayout(y_l, batch, seq)
    new_state = _scan_state_unlayout(s_l, batch)

    x = _even_out(y, rkv, g, o_lat, x, w, tm)
    x = _ffn(x, w["norm_ffn"], w["ffn_gu"], w["ffn_down"], tm)
    return x, lat.reshape(batch, seq, MLA_LAT), new_state, rw3[:, -1]


def _odd_layer(x, batch, seq, state, w, final_norm, tm, tm_moe):
    q, k, v, gate, la = _odd_in(x, w, tm)
    seq_p = -(-seq // GLA_CHUNK) * GLA_CHUNK
    if seq_p != seq:
        padr = lambda t: jnp.pad(t.reshape(batch, seq, -1), ((0, 0), (0, seq_p - seq), (0, 0))
                                 ).reshape(batch * seq_p, -1)
        qp, kp, vp, lap = padr(q), padr(k), padr(v), padr(la)
    else:
        qp, kp, vp, lap = q, k, v, la
    o, st = _gla(qp, kp, vp, lap, state.transpose(0, 1, 3, 2), batch, seq_p)
    if seq_p != seq:
        o = o.reshape(batch, seq_p, GLA_VDIM)[:, :seq].reshape(batch * seq, GLA_VDIM)
    x = _odd_out(o, gate, x, w, tm)
    xp, idx, gates = _router(x, w["norm_ffn"], w["router"], tm)
    dest, src, tile_expert, tile_valid = _route(idx[:, :2], tm_moe)
    xs = _moe_gather(src, xp, tm_moe)
    h = _moe_up(tile_expert, tile_valid, xs, w["moe_gu"], w["layer"], tm_moe)
    ys = _moe_down(tile_expert, tile_valid, h, w["moe_down"], w["layer"], tm_moe)
    g1b = jnp.broadcast_to(gates[:, 0:1], (x.shape[0], LANES))
    g2b = jnp.broadcast_to(gates[:, 1:2], (x.shape[0], LANES))
    z = _moe_combine(dest, ys, g1b, g2b, x, tm)
    y = _final_norm(z, final_norm, tm)
    return y, st.transpose(0, 1, 3, 2)


def kernel(x_prompt, x_sample, cache_mla, state_rwkv, state_rwkv_shift, state_gla, page_table, norm_mix_even, norm_ffn_even, w_in_even, mla_q_norm, mla_kv_norm, mla_w_uq, mla_w_uk, mla_w_uv, rwkv_mu, rwkv_w0, rwkv_w2, rwkv_a0, rwkv_a2, rwkv_g2, rwkv_k_k, rwkv_k_a, rwkv_r_k, rwkv_ln_g, rwkv_ln_b, w_out_even, ffn_w_gu_even, ffn_w_down_even, norm_mix_odd, norm_ffn_odd, w_in_odd, gla_a2, gla_ab, gla_norm, w_out_odd, moe_router, moe_w_gu, moe_w_down, final_norm):
    bp, tp, _ = x_prompt.shape
    bs, ts, _ = x_sample.shape
    past_len = page_table.shape[1] * PAGE_SIZE
    tm_p, tm_s = 512, bs * ts
    we = _prep_even(0, norm_mix_even, norm_ffn_even, w_in_even, mla_q_norm, mla_kv_norm, mla_w_uq,
                    mla_w_uk, mla_w_uv, rwkv_mu, rwkv_w0, rwkv_w2, rwkv_a0, rwkv_a2, rwkv_g2,
                    rwkv_k_k, rwkv_k_a, rwkv_r_k, rwkv_ln_g, rwkv_ln_b, w_out_even, ffn_w_gu_even,
                    ffn_w_down_even)
    wo = _prep_odd(0, norm_mix_odd, norm_ffn_odd, w_in_odd, gla_a2, gla_ab, gla_norm, w_out_odd,
                   moe_router, moe_w_gu, moe_w_down)
    fn = final_norm.reshape(1, -1)
    tabs_p = _rope_tables(jnp.arange(tp), 1)
    tabs_s = _rope_tables(past_len + jnp.arange(ts), bs)

    hp = x_prompt.reshape(bp * tp, D_MODEL)
    hs = x_sample.reshape(bs * ts, D_MODEL)
    zeros_state = jnp.zeros((bp, RWKV_HEADS, RWKV_HEAD, RWKV_HEAD), F32)
    zeros_shift = jnp.zeros((bp, RWKV_PROJ), F32)
    hp, lat_p, rs_p, sh_p = _even_layer(hp, bp, tp, tabs_p, zeros_state, zeros_shift, None, we,
                                        tm_p, 64)
    hs, lat_s, rs_s, sh_s = _even_layer(hs, bs, ts, tabs_s, state_rwkv[0], state_rwkv_shift[0],
                                        (cache_mla, 0, page_table), we, tm_s, ts)
    zeros_gla = jnp.zeros((bp, GLA_HEADS, GLA_DK, GLA_DV), F32)
    yp, gs_p = _odd_layer(hp, bp, tp, zeros_gla, wo, fn, tm_p, 512)
    ys, gs_s = _odd_layer(hs, bs, ts, state_gla[0], wo, fn, tm_s, 128)
    return (yp.reshape(bp, tp, D_MODEL), ys.reshape(bs, ts, D_MODEL), lat_p[None], lat_s[None],
            rs_p[None], rs_s[None], sh_p[None], sh_s[None], gs_p[None], gs_s[None])
```

```python
import functools

import jax
import jax.numpy as jnp
from jax import lax
from jax.experimental import pallas as pl
from jax.experimental.pallas import tpu as pltpu

F32 = jnp.float32
BF16 = jnp.bfloat16

D_MODEL = 1024
PAGE_SIZE = 128
NORM_EPS = 1e-6

MLA_HEADS = 8
MLA_NOPE = 64
MLA_ROPE = 32
MLA_V = 64
MLA_Q_RANK = 384
MLA_KV_RANK = 256
MLA_LAT = MLA_KV_RANK + MLA_ROPE
MLA_LATB = MLA_KV_RANK + 128
MLA_SCALE = (MLA_NOPE + MLA_ROPE) ** -0.5
ROPE_THETA = 10000.0

RWKV_HEADS = 8
RWKV_HEAD = 64
RWKV_DIM = RWKV_HEADS * RWKV_HEAD
RWKV_W_LORA = 64
RWKV_A_LORA = 64
RWKV_G_LORA = 128
RWKV_PROJ = 3 * RWKV_DIM + RWKV_W_LORA + RWKV_A_LORA + RWKV_G_LORA
RWKV_LN_EPS = 64e-5

GLA_HEADS = 4
GLA_DK = 128
GLA_DV = 256
GLA_KDIM = GLA_HEADS * GLA_DK
GLA_VDIM = GLA_HEADS * GLA_DV
GLA_GATE_RANK = 16
GLA_GATE_NORM = 16.0
GLA_CHUNK = 128

D_FF = 2816
N_EXPERTS = 8
D_FF_EXPERT = 3584

LANES = 128
VMEM_LIMIT = 56 * 1024 * 1024
NEG_BIG = -1e30
LOG2_E = 1.4426950408889634
Q_PRESCALE = MLA_SCALE * LOG2_E


def _cparams(sem):
    return pltpu.CompilerParams(dimension_semantics=sem, vmem_limit_bytes=VMEM_LIMIT)


def _const_spec(shape):
    nd = len(shape)
    return pl.BlockSpec(shape, lambda *_: (0,) * nd)


def _row_spec(tm, width):
    return pl.BlockSpec((tm, width), lambda i: (i, 0))


def _dot(a, b):
    return jnp.dot(a.astype(BF16), b.astype(BF16), preferred_element_type=F32)


def _dot_nt(a, b):
    return lax.dot_general(a.astype(BF16), b.astype(BF16), (((1,), (1,)), ((), ())),
                           preferred_element_type=F32)


def _split2(x):
    hi = x.astype(BF16)
    lo = (x - hi.astype(F32)).astype(BF16)
    return hi, lo


def _split3(x):
    hi = x.astype(BF16)
    r1 = x - hi.astype(F32)
    mid = r1.astype(BF16)
    lo = (r1 - mid.astype(F32)).astype(BF16)
    return hi, mid, lo


def _dot_exact_rhs(x, e):
    hi, mid, lo = _split3(x)
    return (jnp.dot(hi, e, preferred_element_type=F32) + jnp.dot(mid, e, preferred_element_type=F32)
            + jnp.dot(lo, e, preferred_element_type=F32))


def _dot_exact_lhs(e, x):
    hi, mid, lo = _split3(x)
    return (jnp.dot(e, hi, preferred_element_type=F32) + jnp.dot(e, mid, preferred_element_type=F32)
            + jnp.dot(e, lo, preferred_element_type=F32))


def _dot_f32ish(a, b):
    ah, al = _split2(a)
    bh, bl = _split2(b)
    return (jnp.dot(ah, bh, preferred_element_type=F32) + jnp.dot(ah, bl, preferred_element_type=F32)
            + jnp.dot(al, bh, preferred_element_type=F32))


def _lane_tile(x, width):
    return x if width == LANES else jnp.concatenate([x] * (width // LANES), axis=1)


def _rms(x, g, eps=NORM_EPS):
    return x * lax.rsqrt(jnp.mean(x * x, axis=-1, keepdims=True) + eps) * g


def _sigmoid(x):
    return 1.0 / (1.0 + jnp.exp(-x))


def _softplus(x):
    return jnp.maximum(x, 0.0) + jnp.log(1.0 + jnp.exp(-jnp.abs(x)))


def _fold_qlat_kernel(uq_ref, uk_ref, o_ref):
    a = uq_ref[...]
    b = uk_ref[...]
    ah, al = _split2(a)
    bh, bl = _split2(b)
    dn = (((1,), (1,)), ((), ()))
    o = (lax.dot_general(ah, bh, dn, preferred_element_type=F32)
         + lax.dot_general(ah, bl, dn, preferred_element_type=F32)
         + lax.dot_general(al, bh, dn, preferred_element_type=F32))
    o_ref[...] = o.astype(BF16)


def _fold_qlat(uq_nope, uk):
    return pl.pallas_call(
        _fold_qlat_kernel,
        grid=(MLA_HEADS,),
        in_specs=[pl.BlockSpec((None, MLA_Q_RANK, MLA_NOPE), lambda h: (h, 0, 0)),
                  pl.BlockSpec((None, MLA_KV_RANK, MLA_NOPE), lambda h: (h, 0, 0))],
        out_specs=pl.BlockSpec((MLA_Q_RANK, MLA_KV_RANK), lambda h: (0, h)),
        out_shape=jax.ShapeDtypeStruct((MLA_Q_RANK, MLA_HEADS * MLA_KV_RANK), BF16),
        compiler_params=_cparams(("arbitrary",)),
    )(uq_nope, uk)


def _even_in_kernel(x_ref, g_ref, wq_ref, wckv_ref, wpa_ref, wpb_ref, wrw_ref, qn_ref, kvn_ref,
                    cs_ref, sn_ref, wql_ref, wqa_ref, wqb_ref, cs8_ref, sn8_ref,
                    lat_ref, latb_ref, ql_ref, qpe_ref, rw_ref):
    xn = _rms(x_ref[...], g_ref[...]).astype(BF16)
    cq = _rms(_dot(xn, wq_ref[...]), qn_ref[...]).astype(BF16)
    ql_ref[...] = (_dot(cq, wql_ref[...]) * Q_PRESCALE).astype(BF16)
    qpe = _dot(cq, wqa_ref[...]) * cs8_ref[...] + _dot(cq, wqb_ref[...]) * sn8_ref[...]
    qpe_ref[...] = (qpe * Q_PRESCALE).astype(BF16)
    ckv = _rms(_dot(xn, wckv_ref[...]), kvn_ref[...])
    kpe = _dot(xn, wpa_ref[...]) * cs_ref[...] + _dot(xn, wpb_ref[...]) * sn_ref[...]
    lat_ref[:, :MLA_KV_RANK] = ckv
    lat_ref[:, MLA_KV_RANK:] = kpe[:, :MLA_ROPE]
    latb_ref[:, :MLA_KV_RANK] = ckv.astype(BF16)
    latb_ref[:, MLA_KV_RANK:] = kpe.astype(BF16)
    rw_ref[...] = _dot(xn, wrw_ref[...])


def _even_in(x, w, tabs, tm):
    n = x.shape[0]
    nt = tabs["cs"].shape[0] // tm
    tab = lambda width: pl.BlockSpec((tm, width), lambda i: (i % nt, 0))
    hq = MLA_HEADS * MLA_KV_RANK
    hr = MLA_HEADS * LANES
    return pl.pallas_call(
        _even_in_kernel,
        grid=(n // tm,),
        in_specs=[_row_spec(tm, D_MODEL), _const_spec((1, D_MODEL)),
                  _const_spec((D_MODEL, MLA_Q_RANK)), _const_spec((D_MODEL, MLA_KV_RANK)),
                  _const_spec((D_MODEL, LANES)), _const_spec((D_MODEL, LANES)),
                  _const_spec((D_MODEL, RWKV_PROJ)), _const_spec((1, MLA_Q_RANK)),
                  _const_spec((1, MLA_KV_RANK)), tab(LANES), tab(LANES),
                  _const_spec((MLA_Q_RANK, hq)), _const_spec((MLA_Q_RANK, hr)),
                  _const_spec((MLA_Q_RANK, hr)), tab(hr), tab(hr)],
        out_specs=[_row_spec(tm, MLA_LAT), _row_spec(tm, MLA_LATB), _row_spec(tm, hq),
                   _row_spec(tm, hr), _row_spec(tm, RWKV_PROJ)],
        out_shape=[jax.ShapeDtypeStruct((n, MLA_LAT), F32), jax.ShapeDtypeStruct((n, MLA_LATB), BF16),
                   jax.ShapeDtypeStruct((n, hq), BF16), jax.ShapeDtypeStruct((n, hr), BF16),
                   jax.ShapeDtypeStruct((n, RWKV_PROJ), F32)],
        compiler_params=_cparams(("parallel",)),
    )(x, w["norm_mix"], w["w_q"], w["w_ckv"], w["w_pe_a"], w["w_pe_b"], w["w_rw"], w["q_norm"],
      w["kv_norm"], tabs["cs"], tabs["sn"], w["w_qlat"], w["w_qpe_a"], w["w_qpe_b"],
      tabs["cs8"], tabs["sn8"])


ATT_TQ = 256


def _mla_prompt_kernel(qi_ref, kj_ref, ql_ref, qpe_ref, lat_ref, o_ref,
                       m_sc, l_sc, a_sc, acc_sc, s_sc, p_sc):
    step = pl.program_id(1)
    i = qi_ref[step]
    j = kj_ref[step]
    heads = range(MLA_HEADS)

    @pl.when(j == 0)
    def _():
        m_sc[...] = jnp.full(m_sc.shape, NEG_BIG, F32)
        l_sc[...] = jnp.zeros(l_sc.shape, F32)
        acc_sc[...] = jnp.zeros(acc_sc.shape, F32)

    def tile(masked):
        ckv = lat_ref[:, :MLA_KV_RANK]
        kpe = lat_ref[:, MLA_KV_RANK:]
        for h in heads:
            s_sc[h] = (_dot_nt(ql_ref[:, h * MLA_KV_RANK:(h + 1) * MLA_KV_RANK], ckv)
                       + _dot_nt(qpe_ref[:, h * LANES:(h + 1) * LANES], kpe))
        for h in heads:
            s = s_sc[h]
            if masked:
                tok = lax.broadcasted_iota(jnp.int32, s.shape, 0)
                key = lax.broadcasted_iota(jnp.int32, s.shape, 1)
                s = jnp.where(key <= tok, s, NEG_BIG)
            m_prev = m_sc[h]
            m_new = jnp.maximum(m_prev, jnp.max(s, axis=-1, keepdims=True))
            alpha = jnp.exp2(m_prev - m_new)
            p = jnp.exp2(s - _lane_tile(m_new, ATT_TQ))
            l_sc[h] = alpha * l_sc[h] + jnp.sum(p, axis=-1, keepdims=True)
            m_sc[h] = m_new
            a_sc[h] = alpha
            p_sc[h] = p.astype(BF16)
        for h in heads:
            acc_sc[h] = _lane_tile(a_sc[h], MLA_KV_RANK) * acc_sc[h] + _dot(p_sc[h], ckv)

    @pl.when(j < i)
    def _():
        tile(False)

    @pl.when(j == i)
    def _():
        tile(True)
        for h in heads:
            o_ref[:, h * MLA_KV_RANK:(h + 1) * MLA_KV_RANK] = (
                acc_sc[h] / _lane_tile(l_sc[h], MLA_KV_RANK)).astype(BF16)


def _mla_prompt(q_lat, q_pe, lat_b, batch, seq):
    nq = seq // ATT_TQ
    pairs = [(i, j) for i in range(nq) for j in range(i + 1)]
    qi = jnp.array([p[0] for p in pairs], jnp.int32)
    kj = jnp.array([p[1] for p in pairs], jnp.int32)
    hq = MLA_HEADS * MLA_KV_RANK
    grid_spec = pltpu.PrefetchScalarGridSpec(
        num_scalar_prefetch=2,
        grid=(batch, len(pairs)),
        in_specs=[pl.BlockSpec((ATT_TQ, hq), lambda b, s, qi, kj: (b * nq + qi[s], 0)),
                  pl.BlockSpec((ATT_TQ, MLA_HEADS * LANES), lambda b, s, qi, kj: (b * nq + qi[s], 0)),
                  pl.BlockSpec((ATT_TQ, MLA_LATB), lambda b, s, qi, kj: (b * nq + kj[s], 0))],
        out_specs=pl.BlockSpec((ATT_TQ, hq), lambda b, s, qi, kj: (b * nq + qi[s], 0)),
        scratch_shapes=[pltpu.VMEM((MLA_HEADS, ATT_TQ, LANES), F32),
                        pltpu.VMEM((MLA_HEADS, ATT_TQ, LANES), F32),
                        pltpu.VMEM((MLA_HEADS, ATT_TQ, LANES), F32),
                        pltpu.VMEM((MLA_HEADS, ATT_TQ, MLA_KV_RANK), F32),
                        pltpu.VMEM((MLA_HEADS, ATT_TQ, ATT_TQ), F32),
                        pltpu.VMEM((MLA_HEADS, ATT_TQ, ATT_TQ), BF16)],
    )
    return pl.pallas_call(
        _mla_prompt_kernel,
        grid_spec=grid_spec,
        out_shape=jax.ShapeDtypeStruct(q_lat.shape, BF16),
        compiler_params=_cparams(("parallel", "arbitrary")),
    )(qi, kj, q_lat, q_pe, lat_b)


PAGES_PER_STEP = 16


def _mla_decode_kernel(pt_ref, q_ref, new_ref, *rest):
    page_refs = rest[:PAGES_PER_STEP]
    o_ref, m_sc, l_sc, acc_sc = rest[PAGES_PER_STEP:]
    j = pl.program_id(1)
    q = q_ref[0]

    @pl.when(j == 0)
    def _():
        m_sc[...] = jnp.full(m_sc.shape, NEG_BIG, F32)
        l_sc[...] = jnp.zeros(l_sc.shape, F32)
        acc_sc[...] = jnp.zeros(acc_sc.shape, F32)

    def update(s, values_t):
        m_prev = m_sc[...]
        m_new = jnp.maximum(m_prev, jnp.max(s, axis=-1, keepdims=True))
        alpha = jnp.exp2(m_prev - m_new)
        p = jnp.exp2(s - _lane_tile(m_new, s.shape[1]))
        l_sc[...] = alpha * l_sc[...] + jnp.sum(p, axis=-1, keepdims=True)
        acc_sc[...] = _lane_tile(alpha, MLA_KV_RANK) * acc_sc[...] + _dot_nt(p, values_t)
        m_sc[...] = m_new

    keys_t = jnp.concatenate([pr[...].astype(BF16) for pr in page_refs], axis=1)
    update(_dot(q, keys_t), keys_t[:MLA_KV_RANK, :])

    @pl.when(j == pl.num_programs(1) - 1)
    def _():
        new_t = new_ref[0]
        sn = _dot(q, new_t)
        tok = lax.broadcasted_iota(jnp.int32, sn.shape, 0) >> 3
        key = lax.broadcasted_iota(jnp.int32, sn.shape, 1)
        sn = jnp.where(key <= tok, sn, NEG_BIG)
        update(sn, new_t[:MLA_KV_RANK, :])
        o_ref[0] = (acc_sc[...] / _lane_tile(l_sc[...], MLA_KV_RANK)).astype(BF16)


def _mla_decode(page_table, q_full, new_pad_t, cache_t, layer):
    db, n_pages = page_table.shape
    rows = q_full.shape[1]
    steps = n_pages // PAGES_PER_STEP

    def page_spec(p):
        return pl.BlockSpec((None, None, MLA_LAT, PAGE_SIZE),
                            lambda b, j, pt: (layer, pt[b, j * PAGES_PER_STEP + p], 0, 0))

    grid_spec = pltpu.PrefetchScalarGridSpec(
        num_scalar_prefetch=1,
        grid=(db, steps),
        in_specs=[pl.BlockSpec((1, rows, MLA_LAT), lambda b, j, pt: (b, 0, 0)),
                  pl.BlockSpec((1, MLA_LAT, PAGE_SIZE), lambda b, j, pt: (b, 0, 0))]
        + [page_spec(p) for p in range(PAGES_PER_STEP)],
        out_specs=pl.BlockSpec((1, rows, MLA_KV_RANK), lambda b, j, pt: (b, 0, 0)),
        scratch_shapes=[pltpu.VMEM((rows, LANES), F32), pltpu.VMEM((rows, LANES), F32),
                        pltpu.VMEM((rows, MLA_KV_RANK), F32)],
    )
    return pl.pallas_call(
        _mla_decode_kernel,
        grid_spec=grid_spec,
        out_shape=jax.ShapeDtypeStruct((db, rows, MLA_KV_RANK), BF16),
        compiler_params=_cparams(("parallel", "arbitrary")),
    )(page_table, q_full, new_pad_t, *([cache_t] * PAGES_PER_STEP))


def _rwkv_prep_kernel(rw_ref, before_ref, sh_ref, mu_ref, w0_ref, w2_ref, a0_ref, a2_ref, g2_ref,
                      kk_ref, ka_ref, rk_ref, ones_ref, xs_ref, v_ref, g_ref, rkv_ref, *, tm, seq):
    rw = rw_ref[...]
    rolled = pltpu.roll(rw, 1, axis=0)
    row = lax.broadcasted_iota(jnp.int32, rw.shape, 0)
    if seq >= tm:
        at_start = pl.program_id(0) % (seq // tm) == 0
        first = jnp.where(at_start, sh_ref[...], before_ref[7:8, :])
        prev = jnp.where(row == 0, first, rolled)
    else:
        prev = jnp.where((row & (seq - 1)) == 0, sh_ref[...], rolled)
    xs = rw + (prev - rw) * mu_ref[...]
    d = RWKV_DIM
    r = xs[:, :d]
    k = xs[:, d:2 * d]
    v = xs[:, 2 * d:3 * d]
    xwa = xs[:, 3 * d:3 * d + LANES]
    xg = xs[:, 3 * d + LANES:]
    ones = ones_ref[...]
    w_log = -_softplus(-(w0_ref[...] + _dot(jnp.tanh(xwa), w2_ref[...]))) - 0.5
    a = _sigmoid(a0_ref[...] + _dot(xwa, a2_ref[...]))
    g_ref[...] = _dot(_sigmoid(xg), g2_ref[...])
    kk = k * kk_ref[...]
    ss = _dot_exact_rhs(kk * kk, ones)
    kk = kk / jnp.maximum(jnp.sqrt(ss), 1e-12)
    k2 = k * (1.0 + (a - 1.0) * ka_ref[...])
    xs_ref[0] = -kk
    xs_ref[1] = jnp.exp(-jnp.exp(w_log))
    xs_ref[2] = kk * a
    xs_ref[3] = k2
    xs_ref[4] = r
    v_ref[...] = v
    rkv_ref[...] = _dot_exact_rhs(r * k2 * rk_ref[...], ones) * v


def _rwkv_prep(rw, shift0, w, tm, seq):
    n = rw.shape[0]
    d = RWKV_DIM
    vec = _const_spec((1, d))
    if seq >= tm:
        tiles = seq // tm
        sh = shift0.reshape(-1, 1, RWKV_PROJ)
        sh_spec = pl.BlockSpec((None, 1, RWKV_PROJ), lambda i: (i // tiles, 0, 0))
    else:
        sh = jnp.repeat(shift0, seq, axis=0)
        sh_spec = _row_spec(tm, RWKV_PROJ)
    before_spec = pl.BlockSpec((8, RWKV_PROJ), lambda i: (jnp.maximum(i * (tm // 8) - 1, 0), 0))
    return pl.pallas_call(
        functools.partial(_rwkv_prep_kernel, tm=tm, seq=seq),
        grid=(n // tm,),
        in_specs=[_row_spec(tm, RWKV_PROJ), before_spec, sh_spec, _const_spec((1, RWKV_PROJ)),
                  vec, _const_spec((LANES, d)), vec, _const_spec((LANES, d)),
                  _const_spec((RWKV_G_LORA, d)), vec, vec, vec, _const_spec((d, d))],
        out_specs=[pl.BlockSpec((5, tm, d), lambda i: (0, i, 0))] + [_row_spec(tm, d)] * 3,
        out_shape=[jax.ShapeDtypeStruct((5, n, d), F32)] + [jax.ShapeDtypeStruct((n, d), F32)] * 3,
        compiler_params=_cparams(("parallel",)),
    )(rw, rw, sh, w["mu"], w["w0"], w["w2p"], w["a0"], w["a2p"], w["g2"], w["k_k"], w["k_a"],
      w["r_k"], w["ones_bd"])


SCAN_KH = RWKV_HEAD // 2
SCAN_PAIRS = LANES // 2
SCAN_VR = RWKV_HEAD // 2


def _rwkv_scan_kernel(x_ref, v_ref, s0_ref, y_ref, s_ref, c_sc, d_sc, *, tc):
    @pl.when(pl.program_id(1) == 0)
    def _():
        s_ref[...] = s0_ref[...]

    half_a = slice(0, SCAN_VR)
    half_b = slice(SCAN_VR, RWKV_HEAD)

    def both_halves(p):
        return p + pltpu.roll(p, SCAN_PAIRS, axis=1)

    def key_dot(u, w):
        return both_halves(jnp.sum(u * w, axis=0, keepdims=True))

    def first_partial(rows):
        p = s_ref[0, 0, rows, :] * x_ref[0, 0, 0, 0:1, :]
        for k in range(1, SCAN_KH):
            p = p + s_ref[0, k, rows, :] * x_ref[0, 0, 0, k:k + 1, :]
        return p

    def half_step(t, rows, sa):
        v_half = v_ref[0, t, rows, :]
        v = jnp.concatenate([v_half, v_half], axis=1)
        q = None
        y = None
        for k in range(SCAN_KH):
            s_old = s_ref[0, k, rows, :]
            qk = s_old * c_sc[k:k + 1, :]
            sn = (s_old * x_ref[1, 0, t, k:k + 1, :] + sa * x_ref[2, 0, t, k:k + 1, :]
                  + v * x_ref[3, 0, t, k:k + 1, :])
            s_ref[0, k, rows, :] = sn
            yk = sn * x_ref[4, 0, t, k:k + 1, :]
            q = qk if q is None else q + qk
            y = yk if y is None else y + yk
        y_ref[0, t, rows, :] = y
        return q, sa * d_sc[0:1, :] + v * d_sc[1:2, :]

    def step(t, carry):
        sa_a, q_b, corr_b = carry
        a_next = x_ref[0, 0, jnp.minimum(t + 1, tc - 1)]
        c_sc[...] = x_ref[1, 0, t] * a_next
        d_sc[0:1, :] = key_dot(x_ref[2, 0, t], a_next)
        d_sc[1:2, :] = key_dot(x_ref[3, 0, t], a_next)
        sa_b = both_halves(q_b) + corr_b
        q_a, corr_a = half_step(t, half_a, sa_a)
        sa_a_next = both_halves(q_a) + corr_a
        q_b_next, corr_b_next = half_step(t, half_b, sa_b)
        return sa_a_next, q_b_next, corr_b_next

    init = (both_halves(first_partial(half_a)), first_partial(half_b),
            jnp.zeros((SCAN_VR, LANES), F32))
    lax.fori_loop(0, tc, step, init)


def _rwkv_scan(xs, v, s0, tc):
    _, nb, t, _, _ = xs.shape
    xspec = pl.BlockSpec((5, 1, tc, SCAN_KH, LANES), lambda n, c: (0, n, c, 0, 0))
    vspec = pl.BlockSpec((1, tc, RWKV_HEAD, SCAN_PAIRS), lambda n, c: (n, c, 0, 0))
    yspec = pl.BlockSpec((1, tc, RWKV_HEAD, LANES), lambda n, c: (n, c, 0, 0))
    sspec = pl.BlockSpec((1, SCAN_KH, RWKV_HEAD, LANES), lambda n, c: (n, 0, 0, 0))
    return pl.pallas_call(
        functools.partial(_rwkv_scan_kernel, tc=tc),
        grid=(nb, t // tc),
        in_specs=[xspec, vspec, sspec],
        out_specs=[yspec, sspec],
        out_shape=[jax.ShapeDtypeStruct((nb, t, RWKV_HEAD, LANES), F32),
                   jax.ShapeDtypeStruct(s0.shape, F32)],
        scratch_shapes=[pltpu.VMEM((SCAN_KH, LANES), F32), pltpu.VMEM((8, LANES), F32)],
        compiler_params=_cparams(("parallel", "arbitrary")),
    )(xs, v, s0)


def _even_out_kernel(y_ref, rkv_ref, g_ref, lng_ref, lnb_ref, ones_ref, ol_ref, wuv_ref, woa_ref,
                     wob_ref, x_ref, o_ref):
    ones = ones_ref[...]
    y = y_ref[...]
    inv = 1.0 / RWKV_HEAD
    mean = _dot_exact_rhs(y, ones) * inv
    dlt = y - mean
    var = _dot_exact_rhs(dlt * dlt, ones) * inv
    yn = dlt * lax.rsqrt(var + RWKV_LN_EPS) * lng_ref[...] + lnb_ref[...] + rkv_ref[...]
    ob = (yn * g_ref[...]).astype(BF16)
    pair = 2 * MLA_KV_RANK
    oa = jnp.concatenate(
        [_dot(ol_ref[:, p * pair:(p + 1) * pair], wuv_ref[p]) for p in range(MLA_HEADS // 2)], axis=1)
    o_ref[...] = x_ref[...] + _dot(oa, woa_ref[...]) + _dot(ob, wob_ref[...])


def _even_out(y, rkv, g, o_lat, x, w, tm):
    n = x.shape[0]
    d = RWKV_DIM
    hq = MLA_HEADS * MLA_KV_RANK
    return pl.pallas_call(
        _even_out_kernel,
        grid=(n // tm,),
        in_specs=[_row_spec(tm, d), _row_spec(tm, d), _row_spec(tm, d), _const_spec((1, d)),
                  _const_spec((1, d)), _const_spec((d, d)), _row_spec(tm, hq),
                  _const_spec((MLA_HEADS // 2, 2 * MLA_KV_RANK, 2 * MLA_V)),
                  _const_spec((MLA_HEADS * MLA_V, D_MODEL)), _const_spec((d, D_MODEL)),
                  _row_spec(tm, D_MODEL)],
        out_specs=_row_spec(tm, D_MODEL),
        out_shape=jax.ShapeDtypeStruct((n, D_MODEL), F32),
        compiler_params=_cparams(("parallel",)),
    )(y, rkv, g, w["ln_g"], w["ln_b"], w["ones_bd"], o_lat, w["w_uv_bd"], w["w_out_a"],
      w["w_out_b"], x)


FFN_TF = 1408


def _ffn_kernel(x_ref, g_ref, wg_ref, wu_ref, wd_ref, o_ref, xn_sc, acc_sc):
    f = pl.program_id(1)

    @pl.when(f == 0)
    def _():
        xn_sc[...] = _rms(x_ref[...], g_ref[...]).astype(BF16)
        acc_sc[...] = jnp.zeros(acc_sc.shape, F32)

    xn = xn_sc[...]
    gate = _dot(xn, wg_ref[...])
    up = _dot(xn, wu_ref[...])
    acc_sc[...] += _dot(gate * _sigmoid(gate) * up, wd_ref[...])

    @pl.when(f == pl.num_programs(1) - 1)
    def _():
        o_ref[...] = x_ref[...] + acc_sc[...]


def _ffn(x, g, w_gu, w_down, tm):
    n = x.shape[0]
    nf = D_FF // FFN_TF
    return pl.pallas_call(
        _ffn_kernel,
        grid=(n // tm, nf),
        in_specs=[pl.BlockSpec((tm, D_MODEL), lambda i, f: (i, 0)),
                  pl.BlockSpec((1, D_MODEL), lambda i, f: (0, 0)),
                  pl.BlockSpec((D_MODEL, FFN_TF), lambda i, f: (0, f)),
                  pl.BlockSpec((D_MODEL, FFN_TF), lambda i, f: (0, nf + f)),
                  pl.BlockSpec((FFN_TF, D_MODEL), lambda i, f: (f, 0))],
        out_specs=pl.BlockSpec((tm, D_MODEL), lambda i, f: (i, 0)),
        out_shape=jax.ShapeDtypeStruct((n, D_MODEL), F32),
        scratch_shapes=[pltpu.VMEM((tm, D_MODEL), BF16), pltpu.VMEM((tm, D_MODEL), F32)],
        compiler_params=_cparams(("parallel", "arbitrary")),
    )(x, g, w_gu, w_gu, w_down)


def _odd_in_kernel(x_ref, g_ref, wq_ref, wk_ref, wv_ref, wg_ref, wxa_ref, a2_ref, ab_ref,
                   q_ref, k_ref, v_ref, gate_ref, la_ref):
    xn = _rms(x_ref[...], g_ref[...]).astype(BF16)
    q_ref[...] = _dot(xn, wq_ref[...]) * (GLA_DK ** -0.5)
    k_ref[...] = _dot(xn, wk_ref[...])
    v_ref[...] = _dot(xn, wv_ref[...])
    gate_ref[...] = _dot(xn, wg_ref[...])
    z = _dot(_dot(xn, wxa_ref[...]), a2_ref[...]) + ab_ref[...]
    la_ref[...] = -_softplus(-z) * (1.0 / GLA_GATE_NORM)


def _odd_in(x, w, tm):
    n = x.shape[0]
    return pl.pallas_call(
        _odd_in_kernel,
        grid=(n // tm,),
        in_specs=[_row_spec(tm, D_MODEL), _const_spec((1, D_MODEL)),
                  _const_spec((D_MODEL, GLA_KDIM)), _const_spec((D_MODEL, GLA_KDIM)),
                  _const_spec((D_MODEL, GLA_VDIM)), _const_spec((D_MODEL, GLA_VDIM)),
                  _const_spec((D_MODEL, LANES)), _const_spec((LANES, GLA_KDIM)),
                  _const_spec((1, GLA_KDIM))],
        out_specs=[_row_spec(tm, GLA_KDIM), _row_spec(tm, GLA_KDIM), _row_spec(tm, GLA_VDIM),
                   _row_spec(tm, GLA_VDIM), _row_spec(tm, GLA_KDIM)],
        out_shape=[jax.ShapeDtypeStruct((n, GLA_KDIM), F32), jax.ShapeDtypeStruct((n, GLA_KDIM), F32),
                   jax.ShapeDtypeStruct((n, GLA_VDIM), F32), jax.ShapeDtypeStruct((n, GLA_VDIM), F32),
                   jax.ShapeDtypeStruct((n, GLA_KDIM), F32)],
        compiler_params=_cparams(("parallel",)),
    )(x, w["norm_mix"], w["w_q"], w["w_k"], w["w_v"], w["w_g"], w["w_xa"], w["a2p"], w["ab"])


def _gla_kernel(q_ref, k_ref, v_ref, la_ref, s0_ref, o_ref, st_ref):
    c = GLA_CHUNK

    @pl.when(pl.program_id(1) == 0)
    def _():
        st_ref[...] = s0_ref[...]

    row = lax.broadcasted_iota(jnp.int32, (c, c), 0)
    col = lax.broadcasted_iota(jnp.int32, (c, c), 1)
    tri = row >= col
    tri_b = jnp.where(tri, 1.0, 0.0).astype(BF16)
    for h in range(GLA_HEADS):
        ks = slice(h * GLA_DK, (h + 1) * GLA_DK)
        vs = slice(h * GLA_DV, (h + 1) * GLA_DV)
        b = _dot_exact_lhs(tri_b, la_ref[:, ks])
        q = q_ref[:, ks]
        k = k_ref[:, ks]
        v = v_ref[:, vs]
        b_end = b[c - 1:c, :]
        qe = (q * jnp.exp(b)).astype(BF16)
        ke = (k * jnp.exp(-b)).astype(BF16)
        a_mat = jnp.where(tri, _dot_nt(qe, ke), 0.0)
        st = st_ref[0, h]
        o_ref[:, vs] = _dot_nt(qe, st) + _dot(a_mat, v)
        k_end = k * jnp.exp(b_end - b)
        st_ref[0, h] = st * jnp.exp(b_end) + _dot(v.T, k_end)


def _gla(q, k, v, la, s0t, batch, seq):
    nc = seq // GLA_CHUNK
    rspec = lambda width: pl.BlockSpec((GLA_CHUNK, width), lambda b, c: (b * nc + c, 0))
    sspec = pl.BlockSpec((1, GLA_HEADS, GLA_DV, GLA_DK), lambda b, c: (b, 0, 0, 0))
    return pl.pallas_call(
        _gla_kernel,
        grid=(batch, nc),
        in_specs=[rspec(GLA_KDIM), rspec(GLA_KDIM), rspec(GLA_VDIM), rspec(GLA_KDIM), sspec],
        out_specs=[rspec(GLA_VDIM), sspec],
        out_shape=[jax.ShapeDtypeStruct(v.shape, F32), jax.ShapeDtypeStruct(s0t.shape, F32)],
        compiler_params=_cparams(("parallel", "arbitrary")),
    )(q, k, v, la, s0t)


def _odd_out_kernel(o_ref, gate_ref, gn_ref, wo_ref, x_ref, y_ref):
    parts = []
    for h in range(GLA_HEADS):
        vs = slice(h * GLA_DV, (h + 1) * GLA_DV)
        parts.append(_rms(o_ref[:, vs], gn_ref[:, vs]))
    gate = gate_ref[...]
    on = jnp.concatenate(parts, axis=1) * (gate * _sigmoid(gate))
    y_ref[...] = x_ref[...] + _dot(on, wo_ref[...])


def _odd_out(o, gate, x, w, tm):
    n = x.shape[0]
    return pl.pallas_call(
        _odd_out_kernel,
        grid=(n // tm,),
        in_specs=[_row_spec(tm, GLA_VDIM), _row_spec(tm, GLA_VDIM), _const_spec((1, GLA_VDIM)),
                  _const_spec((GLA_VDIM, D_MODEL)), _row_spec(tm, D_MODEL)],
        out_specs=_row_spec(tm, D_MODEL),
        out_shape=jax.ShapeDtypeStruct((n, D_MODEL), F32),
        compiler_params=_cparams(("parallel",)),
    )(o, gate, w["gla_norm"], w["w_out"], x)


def _router_kernel(x_ref, g_ref, wr_ref, xn_ref, idx_ref, gate_ref):
    xn = _rms(x_ref[...], g_ref[...])
    half = D_MODEL // 2
    xn_ref[0] = xn[:, :half]
    xn_ref[1] = xn[:, half:]
    logits = _dot_f32ish(xn, wr_ref[...])
    lane = lax.broadcasted_iota(jnp.int32, logits.shape, 1)
    logits = jnp.where(lane < N_EXPERTS, logits, NEG_BIG)
    m1 = jnp.max(logits, axis=-1, keepdims=True)
    i1 = jnp.min(jnp.where(logits == m1, lane, LANES), axis=-1, keepdims=True)
    rest = jnp.where(lane == i1, NEG_BIG, logits)
    m2 = jnp.max(rest, axis=-1, keepdims=True)
    i2 = jnp.min(jnp.where(rest == m2, lane, LANES), axis=-1, keepdims=True)
    e2 = jnp.exp(m2 - m1)
    g1 = 1.0 / (1.0 + e2)
    g2 = e2 / (1.0 + e2)
    idx_ref[...] = jnp.where(lane == 0, i1, jnp.where(lane == 1, i2, 0))
    gate_ref[...] = jnp.where(lane == 0, g1, jnp.where(lane == 1, g2, 0.0))


def _router(x, g, wr, tm):
    n = x.shape[0]
    half = D_MODEL // 2
    return pl.pallas_call(
        _router_kernel,
        grid=(n // tm,),
        in_specs=[_row_spec(tm, D_MODEL), _const_spec((1, D_MODEL)), _const_spec((D_MODEL, LANES))],
        out_specs=[pl.BlockSpec((2, tm, half), lambda i: (0, i, 0)), _row_spec(tm, LANES),
                   _row_spec(tm, LANES)],
        out_shape=[jax.ShapeDtypeStruct((2, n, half), F32),
                   jax.ShapeDtypeStruct((n, LANES), jnp.int32), jax.ShapeDtypeStruct((n, LANES), F32)],
        compiler_params=_cparams(("parallel",)),
    )(x, g, wr)


MOE_TF = 1792
MOE_TMC = 256


def _route(top_i, tm, tmc):
    n = top_i.shape[0]
    slots = 2 * n
    n_tiles = -(-(slots + N_EXPERTS * (tm - 1)) // tm)
    win = tmc + 8
    e_flat = top_i.reshape(-1)
    onehot = (e_flat[:, None] == jnp.arange(N_EXPERTS, dtype=jnp.int32)[None, :]).astype(jnp.int32)
    csum = jnp.cumsum(onehot, axis=0)
    rank = jnp.sum(onehot * csum, axis=1) - 1
    counts = csum[-1]
    padded = ((counts + tm - 1) // tm) * tm
    ends = jnp.cumsum(padded)
    starts = ends - padded
    dest = (jnp.sum(onehot * starts[None, :], axis=1) + rank).astype(jnp.int32)
    tile_start = jnp.arange(n_tiles, dtype=jnp.int32) * tm
    tile_expert = jnp.minimum(jnp.sum((tile_start[:, None] >= ends[None, :]).astype(jnp.int32), axis=1),
                              N_EXPERTS - 1).astype(jnp.int32)
    tile_valid = (tile_start < ends[-1]).astype(jnp.int32)
    src = jnp.zeros((n_tiles * tm,), jnp.int32).at[dest].set(jnp.arange(slots, dtype=jnp.int32) // 2)
    before = jnp.concatenate([jnp.zeros((1, N_EXPERTS), jnp.int32), csum[2 * tmc - 1:-1:2 * tmc]], axis=0)
    wstart = jnp.clip(((starts[None, :] + before) // 8) * 8, 0, n_tiles * tm - win).astype(jnp.int32)
    ws_slot = jnp.sum(onehot * jnp.repeat(wstart, 2 * tmc, axis=0), axis=1)
    local = (e_flat * win + dest - ws_slot).astype(jnp.int32)
    return local, wstart.reshape(-1), src, tile_expert, tile_valid


def _moe_gather_kernel(src_ref, x_ref, o_ref, *, tg):
    base = pl.program_id(1) * tg

    def body(r, carry):
        o_ref[pl.ds(r, 1), :] = x_ref[pl.ds(src_ref[base + r], 1), :]
        return carry

    lax.fori_loop(0, tg, body, 0, unroll=8)


def _moe_gather(src, xn2, tg):
    rows = src.shape[0]
    _, n, half = xn2.shape
    grid_spec = pltpu.PrefetchScalarGridSpec(
        num_scalar_prefetch=1,
        grid=(2, rows // tg),
        in_specs=[pl.BlockSpec((None, n, half), lambda h, i, s: (h, 0, 0),
                               pipeline_mode=pl.Buffered(1))],
        out_specs=pl.BlockSpec((tg, half), lambda h, i, s: (i, h)),
    )
    return pl.pallas_call(
        functools.partial(_moe_gather_kernel, tg=tg),
        grid_spec=grid_spec,
        out_shape=jax.ShapeDtypeStruct((rows, 2 * half), F32),
        compiler_params=_cparams(("arbitrary", "arbitrary")),
    )(src, xn2)


def _moe_up_kernel(te_ref, tv_ref, xs_ref, wg_ref, wu_ref, h_ref):
    @pl.when(tv_ref[pl.program_id(1)] != 0)
    def _():
        xs = xs_ref[...].astype(BF16)
        gate = _dot(xs, wg_ref[...])
        up = _dot(xs, wu_ref[...])
        h_ref[...] = (gate * _sigmoid(gate) * up).astype(BF16)


def _moe_up(te, tv, xs, w_gu, layer, tm):
    rows = xs.shape[0]
    nf = D_FF_EXPERT // MOE_TF
    wspec = lambda off: pl.BlockSpec((None, None, D_MODEL, MOE_TF),
                                     lambda f, t, te, tv: (layer, te[t], 0, off + f))
    grid_spec = pltpu.PrefetchScalarGridSpec(
        num_scalar_prefetch=2,
        grid=(nf, rows // tm),
        in_specs=[pl.BlockSpec((tm, D_MODEL), lambda f, t, te, tv: (t, 0)), wspec(0), wspec(nf)],
        out_specs=pl.BlockSpec((tm, MOE_TF), lambda f, t, te, tv: (t, f)),
    )
    return pl.pallas_call(
        _moe_up_kernel,
        grid_spec=grid_spec,
        out_shape=jax.ShapeDtypeStruct((rows, D_FF_EXPERT), BF16),
        compiler_params=_cparams(("arbitrary", "arbitrary")),
    )(te, tv, xs, w_gu, w_gu)


def _moe_down_kernel(te_ref, tv_ref, h_ref, wd_ref, y_ref):
    @pl.when(tv_ref[pl.program_id(0)] != 0)
    def _():
        y_ref[...] = _dot(h_ref[...], wd_ref[...])


def _moe_down(te, tv, h, w_down, layer, tm):
    rows = h.shape[0]
    grid_spec = pltpu.PrefetchScalarGridSpec(
        num_scalar_prefetch=2,
        grid=(rows // tm,),
        in_specs=[pl.BlockSpec((tm, D_FF_EXPERT), lambda t, te, tv: (t, 0)),
                  pl.BlockSpec((None, None, D_FF_EXPERT, D_MODEL),
                               lambda t, te, tv: (layer, te[t], 0, 0))],
        out_specs=pl.BlockSpec((tm, D_MODEL), lambda t, te, tv: (t, 0)),
    )
    return pl.pallas_call(
        _moe_down_kernel,
        grid_spec=grid_spec,
        out_shape=jax.ShapeDtypeStruct((rows, D_MODEL), F32),
        compiler_params=_cparams(("arbitrary",)),
    )(te, tv, h, w_down)


def _moe_combine_kernel(ws_ref, local_ref, *refs, tmc, win):
    win_refs = refs[:N_EXPERTS]
    g1_ref, g2_ref, x_ref, fn_ref, o_ref, buf = refs[N_EXPERTS:]
    for e in range(N_EXPERTS):
        buf[e * win:(e + 1) * win, :] = win_refs[e][...]
    base = 2 * pl.program_id(0) * tmc

    def body(r, carry):
        row = pl.ds(r, 1)
        y1 = buf[pl.ds(local_ref[base + 2 * r], 1), :]
        y2 = buf[pl.ds(local_ref[base + 2 * r + 1], 1), :]
        g1 = g1_ref[row, :]
        g2 = g2_ref[row, :]
        parts = []
        for c in range(D_MODEL // LANES):
            cs = slice(c * LANES, (c + 1) * LANES)
            parts.append(g1 * y1[:, cs] + g2 * y2[:, cs])
        o_ref[row, :] = x_ref[row, :] + jnp.concatenate(parts, axis=1)
        return carry

    lax.fori_loop(0, tmc, body, 0, unroll=4)
    o_ref[...] = _rms(o_ref[...], fn_ref[...])


def _moe_combine(local, wstart, ys, g1b, g2b, x, fn, tmc):
    n = x.shape[0]
    win = tmc + 8

    def win_spec(e):
        return pl.BlockSpec((pl.Element(win), pl.Element(D_MODEL)),
                            lambda i, ws, lo: (pl.multiple_of(ws[i * N_EXPERTS + e], 8), 0))

    grid_spec = pltpu.PrefetchScalarGridSpec(
        num_scalar_prefetch=2,
        grid=(n // tmc,),
        in_specs=[win_spec(e) for e in range(N_EXPERTS)]
        + [pl.BlockSpec((tmc, LANES), lambda i, ws, lo: (i, 0)),
           pl.BlockSpec((tmc, LANES), lambda i, ws, lo: (i, 0)),
           pl.BlockSpec((tmc, D_MODEL), lambda i, ws, lo: (i, 0)),
           pl.BlockSpec((1, D_MODEL), lambda i, ws, lo: (0, 0))],
        out_specs=pl.BlockSpec((tmc, D_MODEL), lambda i, ws, lo: (i, 0)),
        scratch_shapes=[pltpu.VMEM((N_EXPERTS * win, D_MODEL), F32)],
    )
    return pl.pallas_call(
        functools.partial(_moe_combine_kernel, tmc=tmc, win=win),
        grid_spec=grid_spec,
        out_shape=jax.ShapeDtypeStruct((n, D_MODEL), F32),
        compiler_params=_cparams(("arbitrary",)),
    )(wstart, local, *([ys] * N_EXPERTS), g1b, g2b, x, fn)


def _scan_vec_layout(xs, batch, seq):
    nb = batch * RWKV_HEADS // SCAN_PAIRS
    x = xs.reshape(5, batch, seq, RWKV_HEADS, 2, SCAN_KH).transpose(0, 2, 5, 4, 1, 3)
    x = x.reshape(5, seq, SCAN_KH, 2, nb, SCAN_PAIRS).transpose(0, 4, 1, 2, 3, 5)
    return x.reshape(5, nb, seq, SCAN_KH, LANES)


def _scan_val_layout(v, batch, seq):
    nb = batch * RWKV_HEADS // SCAN_PAIRS
    v4 = v.reshape(batch, seq, RWKV_HEADS, RWKV_HEAD).transpose(1, 3, 0, 2)
    return v4.reshape(seq, RWKV_HEAD, nb, SCAN_PAIRS).transpose(2, 0, 1, 3)


def _scan_val_unlayout(y, batch, seq):
    y = y[..., :SCAN_PAIRS] + y[..., SCAN_PAIRS:]
    v4 = y.transpose(1, 2, 0, 3).reshape(seq, RWKV_HEAD, batch, RWKV_HEADS)
    return v4.transpose(2, 0, 3, 1).reshape(batch * seq, RWKV_DIM)


def _scan_state_layout(s, batch):
    nb = batch * RWKV_HEADS // SCAN_PAIRS
    s6 = s.reshape(batch, RWKV_HEADS, RWKV_HEAD, 2, SCAN_KH).transpose(4, 2, 3, 0, 1)
    s6 = s6.reshape(SCAN_KH, RWKV_HEAD, 2, nb, SCAN_PAIRS).transpose(3, 0, 1, 2, 4)
    return s6.reshape(nb, SCAN_KH, RWKV_HEAD, LANES)


def _scan_state_unlayout(arr, batch):
    nb = arr.shape[0]
    s = arr.reshape(nb, SCAN_KH, RWKV_HEAD, 2, SCAN_PAIRS).transpose(0, 4, 2, 3, 1)
    return s.reshape(batch, RWKV_HEADS, RWKV_HEAD, RWKV_HEAD)


def _swap_halves(w):
    half = w.shape[-1] // 2
    return jnp.concatenate([w[..., half:], w[..., :half]], axis=-1)


def _prep_even(i, norm_mix, norm_ffn, w_in, q_norm, kv_norm, w_uq, w_uk, w_uv, mu, w0, w2, a0, a2,
               g2, k_k, k_a, r_k, ln_g, ln_b, w_out, ffn_gu, ffn_down):
    w = {}
    row = lambda v: v[i].reshape(1, -1)
    w_in = w_in[i]
    w["norm_mix"] = row(norm_mix)
    w["norm_ffn"] = row(norm_ffn)
    w["w_q"] = w_in[:, :MLA_Q_RANK].astype(BF16)
    w_kv = w_in[:, MLA_Q_RANK:MLA_Q_RANK + MLA_LAT]
    w["w_ckv"] = w_kv[:, :MLA_KV_RANK].astype(BF16)
    lane_pad = lambda m: jnp.pad(m, [(0, 0)] * (m.ndim - 1) + [(0, LANES - m.shape[-1])])
    w["w_pe_a"] = lane_pad(w_kv[:, MLA_KV_RANK:]).astype(BF16)
    w["w_pe_b"] = lane_pad(_swap_halves(w_kv[:, MLA_KV_RANK:])).astype(BF16)
    w["w_rw"] = w_in[:, MLA_Q_RANK + MLA_LAT:].astype(BF16)
    w["q_norm"] = row(q_norm)
    w["kv_norm"] = row(kv_norm)
    uq = w_uq[i].reshape(MLA_Q_RANK, MLA_HEADS, MLA_NOPE + MLA_ROPE)
    uq_pe = uq[:, :, MLA_NOPE:]
    w["w_qpe_a"] = lane_pad(uq_pe).reshape(MLA_Q_RANK, -1).astype(BF16)
    w["w_qpe_b"] = lane_pad(_swap_halves(uq_pe)).reshape(MLA_Q_RANK, -1).astype(BF16)
    w["w_qlat"] = _fold_qlat(uq[:, :, :MLA_NOPE].transpose(1, 0, 2), w_uk[i].transpose(1, 0, 2))
    uv = w_uv[i].transpose(1, 0, 2).reshape(MLA_HEADS // 2, 2, MLA_KV_RANK, MLA_V)
    zero = jnp.zeros_like(uv[:, 0])
    w["w_uv_bd"] = jnp.concatenate(
        [jnp.concatenate([uv[:, 0], zero], axis=-1), jnp.concatenate([zero, uv[:, 1]], axis=-1)],
        axis=1).astype(BF16)
    w["mu"] = row(mu)
    w["w0"] = row(w0)
    pad = lambda m, before: jnp.pad(m, ((before, LANES - before - m.shape[0]), (0, 0))).astype(BF16)
    w["w2p"] = pad(w2[i], 0)
    w["a2p"] = pad(a2[i], RWKV_W_LORA)
    w["a0"] = row(a0)
    w["g2"] = g2[i].astype(BF16)
    w["k_k"] = row(k_k)
    w["k_a"] = row(k_a)
    w["r_k"] = row(r_k)
    w["ln_g"] = row(ln_g)
    w["ln_b"] = row(ln_b)
    head = jnp.arange(RWKV_DIM) // RWKV_HEAD
    w["ones_bd"] = (head[:, None] == head[None, :]).astype(BF16)
    w["w_out_a"] = w_out[i][:MLA_HEADS * MLA_V].astype(BF16)
    w["w_out_b"] = w_out[i][MLA_HEADS * MLA_V:].astype(BF16)
    w["ffn_gu"] = ffn_gu[i].astype(BF16)
    w["ffn_down"] = ffn_down[i].astype(BF16)
    return w


def _prep_odd(i, norm_mix, norm_ffn, w_in, a2, ab, gla_norm, w_out, router, moe_gu, moe_down):
    w = {}
    row = lambda v: v[i].reshape(1, -1)
    w_in = w_in[i]
    w["norm_mix"] = row(norm_mix)
    w["norm_ffn"] = row(norm_ffn)
    w["w_q"] = w_in[:, :GLA_KDIM].astype(BF16)
    w["w_k"] = w_in[:, GLA_KDIM:2 * GLA_KDIM].astype(BF16)
    w["w_v"] = w_in[:, 2 * GLA_KDIM:2 * GLA_KDIM + GLA_VDIM].astype(BF16)
    w["w_g"] = w_in[:, 2 * GLA_KDIM + GLA_VDIM:2 * GLA_KDIM + 2 * GLA_VDIM].astype(BF16)
    w["w_xa"] = jnp.pad(w_in[:, 2 * GLA_KDIM + 2 * GLA_VDIM:],
                        ((0, 0), (0, LANES - GLA_GATE_RANK))).astype(BF16)
    w["a2p"] = jnp.pad(a2[i], ((0, LANES - GLA_GATE_RANK), (0, 0))).astype(BF16)
    w["ab"] = row(ab)
    w["gla_norm"] = row(gla_norm)
    w["w_out"] = w_out[i].astype(BF16)
    w["router"] = jnp.pad(router[i], ((0, 0), (0, LANES - N_EXPERTS)))
    w["layer"] = i
    w["moe_gu"] = moe_gu
    w["moe_down"] = moe_down
    return w


def _rope_tables(pos, reps):
    inv = ROPE_THETA ** (-jnp.arange(0, MLA_ROPE, 2, dtype=F32) / MLA_ROPE)
    ang = pos.astype(F32)[:, None] * inv[None, :]
    cos, sin = jnp.cos(ang), jnp.sin(ang)
    pad = ((0, 0), (0, LANES - MLA_ROPE))
    cs = jnp.tile(jnp.pad(jnp.concatenate([cos, cos], axis=-1), pad), (reps, 1))
    sn = jnp.tile(jnp.pad(jnp.concatenate([-sin, sin], axis=-1), pad), (reps, 1))
    return {"cs": cs, "sn": sn, "cs8": jnp.tile(cs, (1, MLA_HEADS)), "sn8": jnp.tile(sn, (1, MLA_HEADS))}


def _even_layer(x, batch, seq, tabs, state, shift0, past, w, tm, tc):
    n = batch * seq
    lat, lat_b, q_lat, q_pe, rw = _even_in(x, w, tabs, tm)
    if past is None:
        o_lat = _mla_prompt(q_lat, q_pe, lat_b, batch, seq)
    else:
        cache, layer, page_table = past
        rows = seq * MLA_HEADS
        q_full = jnp.concatenate([q_lat.reshape(batch, rows, MLA_KV_RANK),
                                  q_pe.reshape(batch, rows, LANES)[:, :, :MLA_ROPE]], axis=-1)
        new_pad_t = jnp.pad(lat_b.reshape(batch, seq, MLA_LATB)[:, :, :MLA_LAT],
                            ((0, 0), (0, PAGE_SIZE - seq), (0, 0))).transpose(0, 2, 1)
        o_lat = _mla_decode(page_table, q_full, new_pad_t, cache.transpose(0, 1, 3, 2), layer)
        o_lat = o_lat.reshape(n, MLA_HEADS * MLA_KV_RANK)

    rw3 = rw.reshape(batch, seq, RWKV_PROJ)
    xs5, v, g, rkv = _rwkv_prep(rw, shift0, w, tm, seq)
    y_l, s_l = _rwkv_scan(_scan_vec_layout(xs5, batch, seq), _scan_val_layout(v, batch, seq),
                          _scan_state_layout(state, batch), tc)
    y = _scan_val_unlayout(y_l, batch, seq)
    new_state = _scan_state_unlayout(s_l, batch)

    x = _even_out(y, rkv, g, o_lat, x, w, tm)
    x = _ffn(x, w["norm_ffn"], w["ffn_gu"], w["ffn_down"], tm)
    return x, lat.reshape(batch, seq, MLA_LAT), new_state, rw3[:, -1]


def _odd_layer(x, batch, seq, state, w, final_norm, tm, tm_moe):
    q, k, v, gate, la = _odd_in(x, w, tm)
    seq_p = -(-seq // GLA_CHUNK) * GLA_CHUNK
    if seq_p != seq:
        padr = lambda t: jnp.pad(t.reshape(batch, seq, -1), ((0, 0), (0, seq_p - seq), (0, 0))
                                 ).reshape(batch * seq_p, -1)
        qp, kp, vp, lap = padr(q), padr(k), padr(v), padr(la)
    else:
        qp, kp, vp, lap = q, k, v, la
    o, st = _gla(qp, kp, vp, lap, state.transpose(0, 1, 3, 2), batch, seq_p)
    if seq_p != seq:
        o = o.reshape(batch, seq_p, GLA_VDIM)[:, :seq].reshape(batch * seq, GLA_VDIM)
    x = _odd_out(o, gate, x, w, tm)
    xn2, idx, gates = _router(x, w["norm_ffn"], w["router"], tm)
    local, wstart, src, tile_expert, tile_valid = _route(idx[:, :2], tm_moe, MOE_TMC)
    xs = _moe_gather(src, xn2, tm_moe)
    h = _moe_up(tile_expert, tile_valid, xs, w["moe_gu"], w["layer"], tm_moe)
    ys = _moe_down(tile_expert, tile_valid, h, w["moe_down"], w["layer"], tm_moe)
    g1b = jnp.broadcast_to(gates[:, 0:1], (x.shape[0], LANES))
    g2b = jnp.broadcast_to(gates[:, 1:2], (x.shape[0], LANES))
    y = _moe_combine(local, wstart, ys, g1b, g2b, x, final_norm, MOE_TMC)
    return y, st.transpose(0, 1, 3, 2)


def kernel(x_prompt, x_sample, cache_mla, state_rwkv, state_rwkv_shift, state_gla, page_table, norm_mix_even, norm_ffn_even, w_in_even, mla_q_norm, mla_kv_norm, mla_w_uq, mla_w_uk, mla_w_uv, rwkv_mu, rwkv_w0, rwkv_w2, rwkv_a0, rwkv_a2, rwkv_g2, rwkv_k_k, rwkv_k_a, rwkv_r_k, rwkv_ln_g, rwkv_ln_b, w_out_even, ffn_w_gu_even, ffn_w_down_even, norm_mix_odd, norm_ffn_odd, w_in_odd, gla_a2, gla_ab, gla_norm, w_out_odd, moe_router, moe_w_gu, moe_w_down, final_norm):
    bp, tp, _ = x_prompt.shape
    bs, ts, _ = x_sample.shape
    past_len = page_table.shape[1] * PAGE_SIZE
    tm_p, tm_s = 512, bs * ts
    we = _prep_even(0, norm_mix_even, norm_ffn_even, w_in_even, mla_q_norm, mla_kv_norm, mla_w_uq,
                    mla_w_uk, mla_w_uv, rwkv_mu, rwkv_w0, rwkv_w2, rwkv_a0, rwkv_a2, rwkv_g2,
                    rwkv_k_k, rwkv_k_a, rwkv_r_k, rwkv_ln_g, rwkv_ln_b, w_out_even, ffn_w_gu_even,
                    ffn_w_down_even)
    wo = _prep_odd(0, norm_mix_odd, norm_ffn_odd, w_in_odd, gla_a2, gla_ab, gla_norm, w_out_odd,
                   moe_router, moe_w_gu, moe_w_down)
    fn = final_norm.reshape(1, -1)
    tabs_p = _rope_tables(jnp.arange(tp), 1)
    tabs_s = _rope_tables(past_len + jnp.arange(ts), bs)

    hp = x_prompt.reshape(bp * tp, D_MODEL)
    hs = x_sample.reshape(bs * ts, D_MODEL)
    zeros_state = jnp.zeros((bp, RWKV_HEADS, RWKV_HEAD, RWKV_HEAD), F32)
    zeros_shift = jnp.zeros((bp, RWKV_PROJ), F32)
    hp, lat_p, rs_p, sh_p = _even_layer(hp, bp, tp, tabs_p, zeros_state, zeros_shift, None, we,
                                        tm_p, 64)
    hs, lat_s, rs_s, sh_s = _even_layer(hs, bs, ts, tabs_s, state_rwkv[0], state_rwkv_shift[0],
                                        (cache_mla, 0, page_table), we, tm_s, ts)
    zeros_gla = jnp.zeros((bp, GLA_HEADS, GLA_DK, GLA_DV), F32)
    yp, gs_p = _odd_layer(hp, bp, tp, zeros_gla, wo, fn, tm_p, 512)
    ys, gs_s = _odd_layer(hs, bs, ts, state_gla[0], wo, fn, tm_s, 128)
    return (yp.reshape(bp, tp, D_MODEL), ys.reshape(bs, ts, D_MODEL), lat_p[None], lat_s[None],
            rs_p[None], rs_s[None], sh_p[None], sh_s[None], gs_p[None], gs_s[None])
```

```python
import functools

import jax
import jax.numpy as jnp
from jax import lax
from jax.experimental import pallas as pl
from jax.experimental.pallas import tpu as pltpu

F32 = jnp.float32
BF16 = jnp.bfloat16

D_MODEL = 1024
PAGE_SIZE = 128
NORM_EPS = 1e-6

MLA_HEADS = 8
MLA_NOPE = 64
MLA_ROPE = 32
MLA_V = 64
MLA_Q_RANK = 384
MLA_KV_RANK = 256
MLA_LAT = MLA_KV_RANK + MLA_ROPE
MLA_LATB = MLA_KV_RANK + 128
MLA_SCALE = (MLA_NOPE + MLA_ROPE) ** -0.5
ROPE_THETA = 10000.0

RWKV_HEADS = 8
RWKV_HEAD = 64
RWKV_DIM = RWKV_HEADS * RWKV_HEAD
RWKV_W_LORA = 64
RWKV_A_LORA = 64
RWKV_G_LORA = 128
RWKV_PROJ = 3 * RWKV_DIM + RWKV_W_LORA + RWKV_A_LORA + RWKV_G_LORA
RWKV_LN_EPS = 64e-5

GLA_HEADS = 4
GLA_DK = 128
GLA_DV = 256
GLA_KDIM = GLA_HEADS * GLA_DK
GLA_VDIM = GLA_HEADS * GLA_DV
GLA_GATE_RANK = 16
GLA_GATE_NORM = 16.0
GLA_CHUNK = 128

D_FF = 2816
N_EXPERTS = 8
D_FF_EXPERT = 3584

LANES = 128
VMEM_LIMIT = 56 * 1024 * 1024
NEG_BIG = -1e30
LOG2_E = 1.4426950408889634
Q_PRESCALE = MLA_SCALE * LOG2_E


def _cparams(sem):
    return pltpu.CompilerParams(dimension_semantics=sem, vmem_limit_bytes=VMEM_LIMIT)


def _const_spec(shape):
    nd = len(shape)
    return pl.BlockSpec(shape, lambda *_: (0,) * nd)


def _row_spec(tm, width):
    return pl.BlockSpec((tm, width), lambda i: (i, 0))


def _dot(a, b):
    return jnp.dot(a.astype(BF16), b.astype(BF16), preferred_element_type=F32)


def _dot_nt(a, b):
    return lax.dot_general(a.astype(BF16), b.astype(BF16), (((1,), (1,)), ((), ())),
                           preferred_element_type=F32)


def _split2(x):
    hi = x.astype(BF16)
    lo = (x - hi.astype(F32)).astype(BF16)
    return hi, lo


def _split3(x):
    hi = x.astype(BF16)
    r1 = x - hi.astype(F32)
    mid = r1.astype(BF16)
    lo = (r1 - mid.astype(F32)).astype(BF16)
    return hi, mid, lo


def _dot_exact_rhs(x, e):
    hi, mid, lo = _split3(x)
    return (jnp.dot(hi, e, preferred_element_type=F32) + jnp.dot(mid, e, preferred_element_type=F32)
            + jnp.dot(lo, e, preferred_element_type=F32))


def _dot_exact_lhs(e, x):
    hi, mid, lo = _split3(x)
    return (jnp.dot(e, hi, preferred_element_type=F32) + jnp.dot(e, mid, preferred_element_type=F32)
            + jnp.dot(e, lo, preferred_element_type=F32))


def _dot_f32ish(a, b):
    ah, al = _split2(a)
    bh, bl = _split2(b)
    return (jnp.dot(ah, bh, preferred_element_type=F32) + jnp.dot(ah, bl, preferred_element_type=F32)
            + jnp.dot(al, bh, preferred_element_type=F32))


def _lane_tile(x, width):
    return x if width == LANES else jnp.concatenate([x] * (width // LANES), axis=1)


def _rms(x, g, eps=NORM_EPS):
    return x * lax.rsqrt(jnp.mean(x * x, axis=-1, keepdims=True) + eps) * g


def _sigmoid(x):
    return 1.0 / (1.0 + jnp.exp(-x))


def _softplus(x):
    return jnp.maximum(x, 0.0) + jnp.log(1.0 + jnp.exp(-jnp.abs(x)))


def _fold_qlat_kernel(uq_ref, uk_ref, o_ref):
    a = uq_ref[...]
    b = uk_ref[...]
    ah, al = _split2(a)
    bh, bl = _split2(b)
    dn = (((1,), (1,)), ((), ()))
    o = (lax.dot_general(ah, bh, dn, preferred_element_type=F32)
         + lax.dot_general(ah, bl, dn, preferred_element_type=F32)
         + lax.dot_general(al, bh, dn, preferred_element_type=F32))
    o_ref[...] = o.astype(BF16)


def _fold_qlat(uq_nope, uk):
    return pl.pallas_call(
        _fold_qlat_kernel,
        grid=(MLA_HEADS,),
        in_specs=[pl.BlockSpec((None, MLA_Q_RANK, MLA_NOPE), lambda h: (h, 0, 0)),
                  pl.BlockSpec((None, MLA_KV_RANK, MLA_NOPE), lambda h: (h, 0, 0))],
        out_specs=pl.BlockSpec((MLA_Q_RANK, MLA_KV_RANK), lambda h: (0, h)),
        out_shape=jax.ShapeDtypeStruct((MLA_Q_RANK, MLA_HEADS * MLA_KV_RANK), BF16),
        compiler_params=_cparams(("arbitrary",)),
    )(uq_nope, uk)


def _even_in_kernel(x_ref, g_ref, wq_ref, wckv_ref, wpa_ref, wpb_ref, wrw_ref, qn_ref, kvn_ref,
                    cs_ref, sn_ref, wql_ref, wqa_ref, wqb_ref, cs8_ref, sn8_ref,
                    lat_ref, latb_ref, ql_ref, qpe_ref, rw_ref):
    xn = _rms(x_ref[...], g_ref[...]).astype(BF16)
    cq = _rms(_dot(xn, wq_ref[...]), qn_ref[...]).astype(BF16)
    ql_ref[...] = (_dot(cq, wql_ref[...]) * Q_PRESCALE).astype(BF16)
    qpe = _dot(cq, wqa_ref[...]) * cs8_ref[...] + _dot(cq, wqb_ref[...]) * sn8_ref[...]
    qpe_ref[...] = (qpe * Q_PRESCALE).astype(BF16)
    ckv = _rms(_dot(xn, wckv_ref[...]), kvn_ref[...])
    kpe = _dot(xn, wpa_ref[...]) * cs_ref[...] + _dot(xn, wpb_ref[...]) * sn_ref[...]
    lat_ref[:, :MLA_KV_RANK] = ckv
    lat_ref[:, MLA_KV_RANK:] = kpe[:, :MLA_ROPE]
    latb_ref[:, :MLA_KV_RANK] = ckv.astype(BF16)
    latb_ref[:, MLA_KV_RANK:] = kpe.astype(BF16)
    rw_ref[...] = _dot(xn, wrw_ref[...])


def _even_in(x, w, tabs, tm):
    n = x.shape[0]
    nt = tabs["cs"].shape[0] // tm
    tab = lambda width: pl.BlockSpec((tm, width), lambda i: (i % nt, 0))
    hq = MLA_HEADS * MLA_KV_RANK
    hr = MLA_HEADS * LANES
    return pl.pallas_call(
        _even_in_kernel,
        grid=(n // tm,),
        in_specs=[_row_spec(tm, D_MODEL), _const_spec((1, D_MODEL)),
                  _const_spec((D_MODEL, MLA_Q_RANK)), _const_spec((D_MODEL, MLA_KV_RANK)),
                  _const_spec((D_MODEL, LANES)), _const_spec((D_MODEL, LANES)),
                  _const_spec((D_MODEL, RWKV_PROJ)), _const_spec((1, MLA_Q_RANK)),
                  _const_spec((1, MLA_KV_RANK)), tab(LANES), tab(LANES),
                  _const_spec((MLA_Q_RANK, hq)), _const_spec((MLA_Q_RANK, hr)),
                  _const_spec((MLA_Q_RANK, hr)), tab(hr), tab(hr)],
        out_specs=[_row_spec(tm, MLA_LAT), _row_spec(tm, MLA_LATB), _row_spec(tm, hq),
                   _row_spec(tm, hr), _row_spec(tm, RWKV_PROJ)],
        out_shape=[jax.ShapeDtypeStruct((n, MLA_LAT), F32), jax.ShapeDtypeStruct((n, MLA_LATB), BF16),
                   jax.ShapeDtypeStruct((n, hq), BF16), jax.ShapeDtypeStruct((n, hr), BF16),
                   jax.ShapeDtypeStruct((n, RWKV_PROJ), F32)],
        compiler_params=_cparams(("parallel",)),
    )(x, w["norm_mix"], w["w_q"], w["w_ckv"], w["w_pe_a"], w["w_pe_b"], w["w_rw"], w["q_norm"],
      w["kv_norm"], tabs["cs"], tabs["sn"], w["w_qlat"], w["w_qpe_a"], w["w_qpe_b"],
      tabs["cs8"], tabs["sn8"])


ATT_TQ = 256


def _mla_prompt_kernel(qi_ref, kj_ref, ql_ref, qpe_ref, lat_ref, o_ref,
                       m_sc, l_sc, a_sc, acc_sc, s_sc, p_sc):
    step = pl.program_id(1)
    i = qi_ref[step]
    j = kj_ref[step]
    heads = range(MLA_HEADS)

    @pl.when(j == 0)
    def _():
        m_sc[...] = jnp.full(m_sc.shape, NEG_BIG, F32)
        l_sc[...] = jnp.zeros(l_sc.shape, F32)
        acc_sc[...] = jnp.zeros(acc_sc.shape, F32)

    def tile(masked):
        ckv = lat_ref[:, :MLA_KV_RANK]
        kpe = lat_ref[:, MLA_KV_RANK:]
        for h in heads:
            s_sc[h] = (_dot_nt(ql_ref[:, h * MLA_KV_RANK:(h + 1) * MLA_KV_RANK], ckv)
                       + _dot_nt(qpe_ref[:, h * LANES:(h + 1) * LANES], kpe))
        for h in heads:
            s = s_sc[h]
            if masked:
                tok = lax.broadcasted_iota(jnp.int32, s.shape, 0)
                key = lax.broadcasted_iota(jnp.int32, s.shape, 1)
                s = jnp.where(key <= tok, s, NEG_BIG)
            m_prev = m_sc[h]
            m_new = jnp.maximum(m_prev, jnp.max(s, axis=-1, keepdims=True))
            alpha = jnp.exp2(m_prev - m_new)
            p = jnp.exp2(s - _lane_tile(m_new, ATT_TQ))
            l_sc[h] = alpha * l_sc[h] + jnp.sum(p, axis=-1, keepdims=True)
            m_sc[h] = m_new
            a_sc[h] = alpha
            p_sc[h] = p.astype(BF16)
        for h in heads:
            acc_sc[h] = _lane_tile(a_sc[h], MLA_KV_RANK) * acc_sc[h] + _dot(p_sc[h], ckv)

    @pl.when(j < i)
    def _():
        tile(False)

    @pl.when(j == i)
    def _():
        tile(True)
        for h in heads:
            o_ref[:, h * MLA_KV_RANK:(h + 1) * MLA_KV_RANK] = (
                acc_sc[h] / _lane_tile(l_sc[h], MLA_KV_RANK)).astype(BF16)


def _mla_prompt(q_lat, q_pe, lat_b, batch, seq):
    nq = seq // ATT_TQ
    pairs = [(i, j) for i in range(nq) for j in range(i + 1)]
    qi = jnp.array([p[0] for p in pairs], jnp.int32)
    kj = jnp.array([p[1] for p in pairs], jnp.int32)
    hq = MLA_HEADS * MLA_KV_RANK
    grid_spec = pltpu.PrefetchScalarGridSpec(
        num_scalar_prefetch=2,
        grid=(batch, len(pairs)),
        in_specs=[pl.BlockSpec((ATT_TQ, hq), lambda b, s, qi, kj: (b * nq + qi[s], 0)),
                  pl.BlockSpec((ATT_TQ, MLA_HEADS * LANES), lambda b, s, qi, kj: (b * nq + qi[s], 0)),
                  pl.BlockSpec((ATT_TQ, MLA_LATB), lambda b, s, qi, kj: (b * nq + kj[s], 0))],
        out_specs=pl.BlockSpec((ATT_TQ, hq), lambda b, s, qi, kj: (b * nq + qi[s], 0)),
        scratch_shapes=[pltpu.VMEM((MLA_HEADS, ATT_TQ, LANES), F32),
                        pltpu.VMEM((MLA_HEADS, ATT_TQ, LANES), F32),
                        pltpu.VMEM((MLA_HEADS, ATT_TQ, LANES), F32),
                        pltpu.VMEM((MLA_HEADS, ATT_TQ, MLA_KV_RANK), F32),
                        pltpu.VMEM((MLA_HEADS, ATT_TQ, ATT_TQ), F32),
                        pltpu.VMEM((MLA_HEADS, ATT_TQ, ATT_TQ), BF16)],
    )
    return pl.pallas_call(
        _mla_prompt_kernel,
        grid_spec=grid_spec,
        out_shape=jax.ShapeDtypeStruct(q_lat.shape, BF16),
        compiler_params=_cparams(("parallel", "arbitrary")),
    )(qi, kj, q_lat, q_pe, lat_b)


PAGES_PER_STEP = 16
DECODE_GROUPS = 4


def _mla_decode_kernel(pt_ref, q_ref, new_ref, *rest):
    page_refs = rest[:PAGES_PER_STEP]
    o_ref, m_sc, l_sc, acc_sc = rest[PAGES_PER_STEP:]
    j = pl.program_id(1)
    q = q_ref[0]

    @pl.when(j == 0)
    def _():
        m_sc[...] = jnp.full(m_sc.shape, NEG_BIG, F32)
        l_sc[...] = jnp.zeros(l_sc.shape, F32)
        acc_sc[...] = jnp.zeros(acc_sc.shape, F32)

    def update(state, s, values_t):
        m_prev, l_prev, acc = state
        m_new = jnp.maximum(m_prev, jnp.max(s, axis=-1, keepdims=True))
        alpha = jnp.exp2(m_prev - m_new)
        p = jnp.exp2(s - _lane_tile(m_new, s.shape[1]))
        l_new = alpha * l_prev + jnp.sum(p, axis=-1, keepdims=True)
        return m_new, l_new, _lane_tile(alpha, MLA_KV_RANK) * acc + _dot_nt(p, values_t)

    group = PAGES_PER_STEP // DECODE_GROUPS
    keys = [jnp.concatenate([pr[...].astype(BF16) for pr in page_refs[g * group:(g + 1) * group]],
                            axis=1) for g in range(DECODE_GROUPS)]
    scores = [_dot(q, kt) for kt in keys]
    state = (m_sc[...], l_sc[...], acc_sc[...])
    for s, kt in zip(scores, keys):
        state = update(state, s, kt[:MLA_KV_RANK, :])
    m_sc[...], l_sc[...], acc_sc[...] = state

    @pl.when(j == pl.num_programs(1) - 1)
    def _():
        new_t = new_ref[0]
        sn = _dot(q, new_t)
        tok = lax.broadcasted_iota(jnp.int32, sn.shape, 0) >> 3
        key = lax.broadcasted_iota(jnp.int32, sn.shape, 1)
        sn = jnp.where(key <= tok, sn, NEG_BIG)
        _, l_fin, acc_fin = update(state, sn, new_t[:MLA_KV_RANK, :])
        o_ref[0] = (acc_fin / _lane_tile(l_fin, MLA_KV_RANK)).astype(BF16)


def _mla_decode(page_table, q_full, new_pad_t, cache_t, layer):
    db, n_pages = page_table.shape
    rows = q_full.shape[1]
    steps = n_pages // PAGES_PER_STEP

    def page_spec(p):
        return pl.BlockSpec((None, None, MLA_LAT, PAGE_SIZE),
                            lambda b, j, pt: (layer, pt[b, j * PAGES_PER_STEP + p], 0, 0))

    grid_spec = pltpu.PrefetchScalarGridSpec(
        num_scalar_prefetch=1,
        grid=(db, steps),
        in_specs=[pl.BlockSpec((1, rows, MLA_LAT), lambda b, j, pt: (b, 0, 0)),
                  pl.BlockSpec((1, MLA_LAT, PAGE_SIZE), lambda b, j, pt: (b, 0, 0))]
        + [page_spec(p) for p in range(PAGES_PER_STEP)],
        out_specs=pl.BlockSpec((1, rows, MLA_KV_RANK), lambda b, j, pt: (b, 0, 0)),
        scratch_shapes=[pltpu.VMEM((rows, LANES), F32), pltpu.VMEM((rows, LANES), F32),
                        pltpu.VMEM((rows, MLA_KV_RANK), F32)],
    )
    return pl.pallas_call(
        _mla_decode_kernel,
        grid_spec=grid_spec,
        out_shape=jax.ShapeDtypeStruct((db, rows, MLA_KV_RANK), BF16),
        compiler_params=_cparams(("parallel", "arbitrary")),
    )(page_table, q_full, new_pad_t, *([cache_t] * PAGES_PER_STEP))


def _rwkv_prep_kernel(rw_ref, before_ref, sh_ref, mu_ref, w0_ref, w2_ref, a0_ref, a2_ref, g2_ref,
                      kk_ref, ka_ref, rk_ref, ones_ref, xs_ref, v_ref, g_ref, rkv_ref, *, tm, seq):
    rw = rw_ref[...]
    rolled = pltpu.roll(rw, 1, axis=0)
    row = lax.broadcasted_iota(jnp.int32, rw.shape, 0)
    if seq >= tm:
        at_start = pl.program_id(0) % (seq // tm) == 0
        first = jnp.where(at_start, sh_ref[...], before_ref[7:8, :])
        prev = jnp.where(row == 0, first, rolled)
    else:
        prev = jnp.where((row & (seq - 1)) == 0, sh_ref[...], rolled)
    xs = rw + (prev - rw) * mu_ref[...]
    d = RWKV_DIM
    r = xs[:, :d]
    k = xs[:, d:2 * d]
    v = xs[:, 2 * d:3 * d]
    xwa = xs[:, 3 * d:3 * d + LANES]
    xg = xs[:, 3 * d + LANES:]
    ones = ones_ref[...]
    w_log = -_softplus(-(w0_ref[...] + _dot(jnp.tanh(xwa), w2_ref[...]))) - 0.5
    a = _sigmoid(a0_ref[...] + _dot(xwa, a2_ref[...]))
    g_ref[...] = _dot(_sigmoid(xg), g2_ref[...])
    kk = k * kk_ref[...]
    ss = _dot_exact_rhs(kk * kk, ones)
    kk = kk / jnp.maximum(jnp.sqrt(ss), 1e-12)
    k2 = k * (1.0 + (a - 1.0) * ka_ref[...])
    xs_ref[0] = -kk
    xs_ref[1] = jnp.exp(-jnp.exp(w_log))
    xs_ref[2] = kk * a
    xs_ref[3] = k2
    xs_ref[4] = r
    v_ref[...] = v
    rkv_ref[...] = _dot_exact_rhs(r * k2 * rk_ref[...], ones) * v


def _rwkv_prep(rw, shift0, w, tm, seq):
    n = rw.shape[0]
    d = RWKV_DIM
    vec = _const_spec((1, d))
    if seq >= tm:
        tiles = seq // tm
        sh = shift0.reshape(-1, 1, RWKV_PROJ)
        sh_spec = pl.BlockSpec((None, 1, RWKV_PROJ), lambda i: (i // tiles, 0, 0))
    else:
        sh = jnp.repeat(shift0, seq, axis=0)
        sh_spec = _row_spec(tm, RWKV_PROJ)
    before_spec = pl.BlockSpec((8, RWKV_PROJ), lambda i: (jnp.maximum(i * (tm // 8) - 1, 0), 0))
    return pl.pallas_call(
        functools.partial(_rwkv_prep_kernel, tm=tm, seq=seq),
        grid=(n // tm,),
        in_specs=[_row_spec(tm, RWKV_PROJ), before_spec, sh_spec, _const_spec((1, RWKV_PROJ)),
                  vec, _const_spec((LANES, d)), vec, _const_spec((LANES, d)),
                  _const_spec((RWKV_G_LORA, d)), vec, vec, vec, _const_spec((d, d))],
        out_specs=[pl.BlockSpec((5, tm, d), lambda i: (0, i, 0))] + [_row_spec(tm, d)] * 3,
        out_shape=[jax.ShapeDtypeStruct((5, n, d), F32)] + [jax.ShapeDtypeStruct((n, d), F32)] * 3,
        compiler_params=_cparams(("parallel",)),
    )(rw, rw, sh, w["mu"], w["w0"], w["w2p"], w["a0"], w["a2p"], w["g2"], w["k_k"], w["k_a"],
      w["r_k"], w["ones_bd"])


SCAN_KH = RWKV_HEAD // 2
SCAN_PAIRS = LANES // 2
SCAN_VR = RWKV_HEAD // 2


def _rwkv_scan_kernel(x_ref, v_ref, s0_ref, y_ref, s_ref, c_sc, d_sc, *, tc):
    @pl.when(pl.program_id(1) == 0)
    def _():
        s_ref[...] = s0_ref[...]

    half_a = slice(0, SCAN_VR)
    half_b = slice(SCAN_VR, RWKV_HEAD)

    def both_halves(p):
        return p + pltpu.roll(p, SCAN_PAIRS, axis=1)

    def key_dot(u, w):
        return both_halves(jnp.sum(u * w, axis=0, keepdims=True))

    def first_partial(rows):
        p = s_ref[0, 0, rows, :] * x_ref[0, 0, 0, 0:1, :]
        for k in range(1, SCAN_KH):
            p = p + s_ref[0, k, rows, :] * x_ref[0, 0, 0, k:k + 1, :]
        return p

    def half_step(t, rows, sa):
        v_half = v_ref[0, t, rows, :]
        v = jnp.concatenate([v_half, v_half], axis=1)
        q = None
        y = None
        for k in range(SCAN_KH):
            s_old = s_ref[0, k, rows, :]
            qk = s_old * c_sc[k:k + 1, :]
            sn = (s_old * x_ref[1, 0, t, k:k + 1, :] + sa * x_ref[2, 0, t, k:k + 1, :]
                  + v * x_ref[3, 0, t, k:k + 1, :])
            s_ref[0, k, rows, :] = sn
            yk = sn * x_ref[4, 0, t, k:k + 1, :]
            q = qk if q is None else q + qk
            y = yk if y is None else y + yk
        return q, sa * d_sc[0:1, :] + v * d_sc[1:2, :], y

    def store_y(t, y_a, y_b):
        y_ref[0, t, half_a, :] = both_halves(y_a)[:, :SCAN_PAIRS]
        y_ref[0, t, half_b, :] = both_halves(y_b)[:, :SCAN_PAIRS]

    def step(t, carry):
        sa_a, q_b, corr_b, y_a, y_b = carry
        store_y(jnp.maximum(t - 1, 0), y_a, y_b)
        a_next = x_ref[0, 0, jnp.minimum(t + 1, tc - 1)]
        c_sc[...] = x_ref[1, 0, t] * a_next
        d_sc[0:1, :] = key_dot(x_ref[2, 0, t], a_next)
        d_sc[1:2, :] = key_dot(x_ref[3, 0, t], a_next)
        sa_b = both_halves(q_b) + corr_b
        q_a, corr_a, y_a_new = half_step(t, half_a, sa_a)
        sa_a_next = both_halves(q_a) + corr_a
        q_b_next, corr_b_next, y_b_new = half_step(t, half_b, sa_b)
        return sa_a_next, q_b_next, corr_b_next, y_a_new, y_b_new

    zero = jnp.zeros((SCAN_VR, LANES), F32)
    init = (both_halves(first_partial(half_a)), first_partial(half_b), zero, zero, zero)
    final = lax.fori_loop(0, tc, step, init)
    store_y(tc - 1, final[3], final[4])


def _rwkv_scan(xs, v, s0, tc):
    _, nb, t, _, _ = xs.shape
    xspec = pl.BlockSpec((5, 1, tc, SCAN_KH, LANES), lambda n, c: (0, n, c, 0, 0))
    vspec = pl.BlockSpec((1, tc, RWKV_HEAD, SCAN_PAIRS), lambda n, c: (n, c, 0, 0))
    sspec = pl.BlockSpec((1, SCAN_KH, RWKV_HEAD, LANES), lambda n, c: (n, 0, 0, 0))
    return pl.pallas_call(
        functools.partial(_rwkv_scan_kernel, tc=tc),
        grid=(nb, t // tc),
        in_specs=[xspec, vspec, sspec],
        out_specs=[vspec, sspec],
        out_shape=[jax.ShapeDtypeStruct(v.shape, F32), jax.ShapeDtypeStruct(s0.shape, F32)],
        scratch_shapes=[pltpu.VMEM((SCAN_KH, LANES), F32), pltpu.VMEM((8, LANES), F32)],
        compiler_params=_cparams(("parallel", "arbitrary")),
    )(xs, v, s0)


def _even_out_kernel(y_ref, rkv_ref, g_ref, lng_ref, lnb_ref, ones_ref, ol_ref, wuv_ref, woa_ref,
                     wob_ref, x_ref, o_ref):
    ones = ones_ref[...]
    y = y_ref[...]
    inv = 1.0 / RWKV_HEAD
    mean = _dot_exact_rhs(y, ones) * inv
    dlt = y - mean
    var = _dot_exact_rhs(dlt * dlt, ones) * inv
    yn = dlt * lax.rsqrt(var + RWKV_LN_EPS) * lng_ref[...] + lnb_ref[...] + rkv_ref[...]
    ob = (yn * g_ref[...]).astype(BF16)
    pair = 2 * MLA_KV_RANK
    oa = jnp.concatenate(
        [_dot(ol_ref[:, p * pair:(p + 1) * pair], wuv_ref[p]) for p in range(MLA_HEADS // 2)], axis=1)
    o_ref[...] = x_ref[...] + _dot(oa, woa_ref[...]) + _dot(ob, wob_ref[...])


def _even_out(y, rkv, g, o_lat, x, w, tm):
    n = x.shape[0]
    d = RWKV_DIM
    hq = MLA_HEADS * MLA_KV_RANK
    return pl.pallas_call(
        _even_out_kernel,
        grid=(n // tm,),
        in_specs=[_row_spec(tm, d), _row_spec(tm, d), _row_spec(tm, d), _const_spec((1, d)),
                  _const_spec((1, d)), _const_spec((d, d)), _row_spec(tm, hq),
                  _const_spec((MLA_HEADS // 2, 2 * MLA_KV_RANK, 2 * MLA_V)),
                  _const_spec((MLA_HEADS * MLA_V, D_MODEL)), _const_spec((d, D_MODEL)),
                  _row_spec(tm, D_MODEL)],
        out_specs=_row_spec(tm, D_MODEL),
        out_shape=jax.ShapeDtypeStruct((n, D_MODEL), F32),
        compiler_params=_cparams(("parallel",)),
    )(y, rkv, g, w["ln_g"], w["ln_b"], w["ones_bd"], o_lat, w["w_uv_bd"], w["w_out_a"],
      w["w_out_b"], x)


FFN_TF = 1408


def _ffn_kernel(x_ref, g_ref, wg_ref, wu_ref, wd_ref, o_ref, xn_sc, acc_sc):
    f = pl.program_id(1)

    @pl.when(f == 0)
    def _():
        xn_sc[...] = _rms(x_ref[...], g_ref[...]).astype(BF16)
        acc_sc[...] = jnp.zeros(acc_sc.shape, F32)

    xn = xn_sc[...]
    gate = _dot(xn, wg_ref[...])
    up = _dot(xn, wu_ref[...])
    acc_sc[...] += _dot(gate * _sigmoid(gate) * up, wd_ref[...])

    @pl.when(f == pl.num_programs(1) - 1)
    def _():
        o_ref[...] = x_ref[...] + acc_sc[...]


def _ffn(x, g, w_gu, w_down, tm):
    n = x.shape[0]
    nf = D_FF // FFN_TF
    return pl.pallas_call(
        _ffn_kernel,
        grid=(n // tm, nf),
        in_specs=[pl.BlockSpec((tm, D_MODEL), lambda i, f: (i, 0)),
                  pl.BlockSpec((1, D_MODEL), lambda i, f: (0, 0)),
                  pl.BlockSpec((D_MODEL, FFN_TF), lambda i, f: (0, f)),
                  pl.BlockSpec((D_MODEL, FFN_TF), lambda i, f: (0, nf + f)),
                  pl.BlockSpec((FFN_TF, D_MODEL), lambda i, f: (f, 0))],
        out_specs=pl.BlockSpec((tm, D_MODEL), lambda i, f: (i, 0)),
        out_shape=jax.ShapeDtypeStruct((n, D_MODEL), F32),
        scratch_shapes=[pltpu.VMEM((tm, D_MODEL), BF16), pltpu.VMEM((tm, D_MODEL), F32)],
        compiler_params=_cparams(("parallel", "arbitrary")),
    )(x, g, w_gu, w_gu, w_down)


def _odd_in_kernel(x_ref, g_ref, wq_ref, wk_ref, wv_ref, wg_ref, wxa_ref, a2_ref, ab_ref,
                   q_ref, k_ref, v_ref, gate_ref, la_ref):
    xn = _rms(x_ref[...], g_ref[...]).astype(BF16)
    q_ref[...] = _dot(xn, wq_ref[...]) * (GLA_DK ** -0.5)
    k_ref[...] = _dot(xn, wk_ref[...])
    v_ref[...] = _dot(xn, wv_ref[...])
    gate_ref[...] = _dot(xn, wg_ref[...])
    z = _dot(_dot(xn, wxa_ref[...]), a2_ref[...]) + ab_ref[...]
    la_ref[...] = -_softplus(-z) * (1.0 / GLA_GATE_NORM)


def _odd_in(x, w, tm):
    n = x.shape[0]
    return pl.pallas_call(
        _odd_in_kernel,
        grid=(n // tm,),
        in_specs=[_row_spec(tm, D_MODEL), _const_spec((1, D_MODEL)),
                  _const_spec((D_MODEL, GLA_KDIM)), _const_spec((D_MODEL, GLA_KDIM)),
                  _const_spec((D_MODEL, GLA_VDIM)), _const_spec((D_MODEL, GLA_VDIM)),
                  _const_spec((D_MODEL, LANES)), _const_spec((LANES, GLA_KDIM)),
                  _const_spec((1, GLA_KDIM))],
        out_specs=[_row_spec(tm, GLA_KDIM), _row_spec(tm, GLA_KDIM), _row_spec(tm, GLA_VDIM),
                   _row_spec(tm, GLA_VDIM), _row_spec(tm, GLA_KDIM)],
        out_shape=[jax.ShapeDtypeStruct((n, GLA_KDIM), F32), jax.ShapeDtypeStruct((n, GLA_KDIM), F32),
                   jax.ShapeDtypeStruct((n, GLA_VDIM), F32), jax.ShapeDtypeStruct((n, GLA_VDIM), F32),
                   jax.ShapeDtypeStruct((n, GLA_KDIM), F32)],
        compiler_params=_cparams(("parallel",)),
    )(x, w["norm_mix"], w["w_q"], w["w_k"], w["w_v"], w["w_g"], w["w_xa"], w["a2p"], w["ab"])


def _gla_kernel(q_ref, k_ref, v_ref, la_ref, s0_ref, o_ref, st_ref):
    c = GLA_CHUNK

    @pl.when(pl.program_id(1) == 0)
    def _():
        st_ref[...] = s0_ref[...]

    row = lax.broadcasted_iota(jnp.int32, (c, c), 0)
    col = lax.broadcasted_iota(jnp.int32, (c, c), 1)
    tri = row >= col
    tri_b = jnp.where(tri, 1.0, 0.0).astype(BF16)
    for h in range(GLA_HEADS):
        ks = slice(h * GLA_DK, (h + 1) * GLA_DK)
        vs = slice(h * GLA_DV, (h + 1) * GLA_DV)
        b = _dot_exact_lhs(tri_b, la_ref[:, ks])
        q = q_ref[:, ks]
        k = k_ref[:, ks]
        v = v_ref[:, vs]
        b_end = b[c - 1:c, :]
        qe = (q * jnp.exp(b)).astype(BF16)
        ke = (k * jnp.exp(-b)).astype(BF16)
        a_mat = jnp.where(tri, _dot_nt(qe, ke), 0.0)
        st = st_ref[0, h]
        o_ref[:, vs] = _dot_nt(qe, st) + _dot(a_mat, v)
        k_end = k * jnp.exp(b_end - b)
        st_ref[0, h] = st * jnp.exp(b_end) + _dot(v.T, k_end)


def _gla(q, k, v, la, s0t, batch, seq):
    nc = seq // GLA_CHUNK
    rspec = lambda width: pl.BlockSpec((GLA_CHUNK, width), lambda b, c: (b * nc + c, 0))
    sspec = pl.BlockSpec((1, GLA_HEADS, GLA_DV, GLA_DK), lambda b, c: (b, 0, 0, 0))
    return pl.pallas_call(
        _gla_kernel,
        grid=(batch, nc),
        in_specs=[rspec(GLA_KDIM), rspec(GLA_KDIM), rspec(GLA_VDIM), rspec(GLA_KDIM), sspec],
        out_specs=[rspec(GLA_VDIM), sspec],
        out_shape=[jax.ShapeDtypeStruct(v.shape, F32), jax.ShapeDtypeStruct(s0t.shape, F32)],
        compiler_params=_cparams(("parallel", "arbitrary")),
    )(q, k, v, la, s0t)


def _odd_out_kernel(o_ref, gate_ref, gn_ref, wo_ref, x_ref, y_ref):
    parts = []
    for h in range(GLA_HEADS):
        vs = slice(h * GLA_DV, (h + 1) * GLA_DV)
        parts.append(_rms(o_ref[:, vs], gn_ref[:, vs]))
    gate = gate_ref[...]
    on = jnp.concatenate(parts, axis=1) * (gate * _sigmoid(gate))
    y_ref[...] = x_ref[...] + _dot(on, wo_ref[...])


def _odd_out(o, gate, x, w, tm):
    n = x.shape[0]
    return pl.pallas_call(
        _odd_out_kernel,
        grid=(n // tm,),
        in_specs=[_row_spec(tm, GLA_VDIM), _row_spec(tm, GLA_VDIM), _const_spec((1, GLA_VDIM)),
                  _const_spec((GLA_VDIM, D_MODEL)), _row_spec(tm, D_MODEL)],
        out_specs=_row_spec(tm, D_MODEL),
        out_shape=jax.ShapeDtypeStruct((n, D_MODEL), F32),
        compiler_params=_cparams(("parallel",)),
    )(o, gate, w["gla_norm"], w["w_out"], x)


def _router_kernel(x_ref, g_ref, wr_ref, xn_ref, idx_ref, gate_ref):
    xn = _rms(x_ref[...], g_ref[...])
    half = D_MODEL // 2
    xn_ref[0] = xn[:, :half]
    xn_ref[1] = xn[:, half:]
    logits = _dot_f32ish(xn, wr_ref[...])
    lane = lax.broadcasted_iota(jnp.int32, logits.shape, 1)
    logits = jnp.where(lane < N_EXPERTS, logits, NEG_BIG)
    m1 = jnp.max(logits, axis=-1, keepdims=True)
    i1 = jnp.min(jnp.where(logits == m1, lane, LANES), axis=-1, keepdims=True)
    rest = jnp.where(lane == i1, NEG_BIG, logits)
    m2 = jnp.max(rest, axis=-1, keepdims=True)
    i2 = jnp.min(jnp.where(rest == m2, lane, LANES), axis=-1, keepdims=True)
    e2 = jnp.exp(m2 - m1)
    g1 = 1.0 / (1.0 + e2)
    g2 = e2 / (1.0 + e2)
    idx_ref[...] = jnp.where(lane == 0, i1, jnp.where(lane == 1, i2, 0))
    gate_ref[...] = jnp.where(lane == 0, g1, jnp.where(lane == 1, g2, 0.0))


def _router(x, g, wr, tm):
    n = x.shape[0]
    half = D_MODEL // 2
    return pl.pallas_call(
        _router_kernel,
        grid=(n // tm,),
        in_specs=[_row_spec(tm, D_MODEL), _const_spec((1, D_MODEL)), _const_spec((D_MODEL, LANES))],
        out_specs=[pl.BlockSpec((2, tm, half), lambda i: (0, i, 0)), _row_spec(tm, LANES),
                   _row_spec(tm, LANES)],
        out_shape=[jax.ShapeDtypeStruct((2, n, half), F32),
                   jax.ShapeDtypeStruct((n, LANES), jnp.int32), jax.ShapeDtypeStruct((n, LANES), F32)],
        compiler_params=_cparams(("parallel",)),
    )(x, g, wr)


MOE_TF = 1792
MOE_TMC = 256


def _route(top_i, tm, tmc):
    n = top_i.shape[0]
    slots = 2 * n
    n_tiles = -(-(slots + N_EXPERTS * (tm - 1)) // tm)
    win = tmc + 8
    e_flat = top_i.reshape(-1)
    onehot = (e_flat[:, None] == jnp.arange(N_EXPERTS, dtype=jnp.int32)[None, :]).astype(jnp.int32)
    csum = jnp.cumsum(onehot, axis=0)
    rank = jnp.sum(onehot * csum, axis=1) - 1
    counts = csum[-1]
    padded = ((counts + tm - 1) // tm) * tm
    ends = jnp.cumsum(padded)
    starts = ends - padded
    dest = (jnp.sum(onehot * starts[None, :], axis=1) + rank).astype(jnp.int32)
    tile_start = jnp.arange(n_tiles, dtype=jnp.int32) * tm
    tile_expert = jnp.minimum(jnp.sum((tile_start[:, None] >= ends[None, :]).astype(jnp.int32), axis=1),
                              N_EXPERTS - 1).astype(jnp.int32)
    tile_valid = (tile_start < ends[-1]).astype(jnp.int32)
    src = jnp.zeros((n_tiles * tm,), jnp.int32).at[dest].set(jnp.arange(slots, dtype=jnp.int32) // 2)
    before = jnp.concatenate([jnp.zeros((1, N_EXPERTS), jnp.int32), csum[2 * tmc - 1:-1:2 * tmc]], axis=0)
    wstart = jnp.clip(((starts[None, :] + before) // 8) * 8, 0, n_tiles * tm - win).astype(jnp.int32)
    ws_slot = jnp.sum(onehot * jnp.repeat(wstart, 2 * tmc, axis=0), axis=1)
    local = (e_flat * win + dest - ws_slot).astype(jnp.int32)
    return local, wstart.reshape(-1), src, tile_expert, tile_valid


def _moe_gather_kernel(src_ref, x_ref, o_ref, *, tg):
    base = pl.program_id(1) * tg

    def body(r, carry):
        o_ref[pl.ds(r, 1), :] = x_ref[pl.ds(src_ref[base + r], 1), :]
        return carry

    lax.fori_loop(0, tg, body, 0, unroll=8)


def _moe_gather(src, xn2, tg):
    rows = src.shape[0]
    _, n, half = xn2.shape
    grid_spec = pltpu.PrefetchScalarGridSpec(
        num_scalar_prefetch=1,
        grid=(2, rows // tg),
        in_specs=[pl.BlockSpec((None, n, half), lambda h, i, s: (h, 0, 0),
                               pipeline_mode=pl.Buffered(1))],
        out_specs=pl.BlockSpec((tg, half), lambda h, i, s: (i, h)),
    )
    return pl.pallas_call(
        functools.partial(_moe_gather_kernel, tg=tg),
        grid_spec=grid_spec,
        out_shape=jax.ShapeDtypeStruct((rows, 2 * half), F32),
        compiler_params=_cparams(("arbitrary", "arbitrary")),
    )(src, xn2)


def _moe_up_kernel(te_ref, tv_ref, xs_ref, wg_ref, wu_ref, h_ref):
    @pl.when(tv_ref[pl.program_id(1)] != 0)
    def _():
        xs = xs_ref[...].astype(BF16)
        gate = _dot(xs, wg_ref[...])
        up = _dot(xs, wu_ref[...])
        h_ref[...] = (gate * _sigmoid(gate) * up).astype(BF16)


def _moe_up(te, tv, xs, w_gu, layer, tm):
    rows = xs.shape[0]
    nf = D_FF_EXPERT // MOE_TF
    wspec = lambda off: pl.BlockSpec((None, None, D_MODEL, MOE_TF),
                                     lambda f, t, te, tv: (layer, te[t], 0, off + f))
    grid_spec = pltpu.PrefetchScalarGridSpec(
        num_scalar_prefetch=2,
        grid=(nf, rows // tm),
        in_specs=[pl.BlockSpec((tm, D_MODEL), lambda f, t, te, tv: (t, 0)), wspec(0), wspec(nf)],
        out_specs=pl.BlockSpec((tm, MOE_TF), lambda f, t, te, tv: (t, f)),
    )
    return pl.pallas_call(
        _moe_up_kernel,
        grid_spec=grid_spec,
        out_shape=jax.ShapeDtypeStruct((rows, D_FF_EXPERT), BF16),
        compiler_params=_cparams(("arbitrary", "arbitrary")),
    )(te, tv, xs, w_gu, w_gu)


def _moe_down_kernel(te_ref, tv_ref, h_ref, wd_ref, y_ref):
    @pl.when(tv_ref[pl.program_id(0)] != 0)
    def _():
        y_ref[...] = _dot(h_ref[...], wd_ref[...])


def _moe_down(te, tv, h, w_down, layer, tm):
    rows = h.shape[0]
    grid_spec = pltpu.PrefetchScalarGridSpec(
        num_scalar_prefetch=2,
        grid=(rows // tm,),
        in_specs=[pl.BlockSpec((tm, D_FF_EXPERT), lambda t, te, tv: (t, 0)),
                  pl.BlockSpec((None, None, D_FF_EXPERT, D_MODEL),
                               lambda t, te, tv: (layer, te[t], 0, 0))],
        out_specs=pl.BlockSpec((tm, D_MODEL), lambda t, te, tv: (t, 0)),
    )
    return pl.pallas_call(
        _moe_down_kernel,
        grid_spec=grid_spec,
        out_shape=jax.ShapeDtypeStruct((rows, D_MODEL), F32),
        compiler_params=_cparams(("arbitrary",)),
    )(te, tv, h, w_down)


def _moe_combine_kernel(ws_ref, local_ref, *refs, tmc, win):
    win_refs = refs[:N_EXPERTS]
    g1_ref, g2_ref, x_ref, fn_ref, o_ref, buf = refs[N_EXPERTS:]
    for e in range(N_EXPERTS):
        buf[e * win:(e + 1) * win, :] = win_refs[e][...]
    base = 2 * pl.program_id(0) * tmc

    def body(r, carry):
        row = pl.ds(r, 1)
        y1 = buf[pl.ds(local_ref[base + 2 * r], 1), :]
        y2 = buf[pl.ds(local_ref[base + 2 * r + 1], 1), :]
        g1 = g1_ref[row, :]
        g2 = g2_ref[row, :]
        parts = []
        for c in range(D_MODEL // LANES):
            cs = slice(c * LANES, (c + 1) * LANES)
            parts.append(g1 * y1[:, cs] + g2 * y2[:, cs])
        o_ref[row, :] = x_ref[row, :] + jnp.concatenate(parts, axis=1)
        return carry

    lax.fori_loop(0, tmc, body, 0, unroll=4)
    o_ref[...] = _rms(o_ref[...], fn_ref[...])


def _moe_combine(local, wstart, ys, g1b, g2b, x, fn, tmc):
    n = x.shape[0]
    win = tmc + 8

    def win_spec(e):
        return pl.BlockSpec((pl.Element(win), pl.Element(D_MODEL)),
                            lambda i, ws, lo: (pl.multiple_of(ws[i * N_EXPERTS + e], 8), 0))

    grid_spec = pltpu.PrefetchScalarGridSpec(
        num_scalar_prefetch=2,
        grid=(n // tmc,),
        in_specs=[win_spec(e) for e in range(N_EXPERTS)]
        + [pl.BlockSpec((tmc, LANES), lambda i, ws, lo: (i, 0)),
           pl.BlockSpec((tmc, LANES), lambda i, ws, lo: (i, 0)),
           pl.BlockSpec((tmc, D_MODEL), lambda i, ws, lo: (i, 0)),
           pl.BlockSpec((1, D_MODEL), lambda i, ws, lo: (0, 0))],
        out_specs=pl.BlockSpec((tmc, D_MODEL), lambda i, ws, lo: (i, 0)),
        scratch_shapes=[pltpu.VMEM((N_EXPERTS * win, D_MODEL), F32)],
    )
    return pl.pallas_call(
        functools.partial(_moe_combine_kernel, tmc=tmc, win=win),
        grid_spec=grid_spec,
        out_shape=jax.ShapeDtypeStruct((n, D_MODEL), F32),
        compiler_params=_cparams(("arbitrary",)),
    )(wstart, local, *([ys] * N_EXPERTS), g1b, g2b, x, fn)


def _scan_vec_layout(xs, batch, seq):
    nb = batch * RWKV_HEADS // SCAN_PAIRS
    x = xs.reshape(5, batch, seq, RWKV_HEADS, 2, SCAN_KH).transpose(0, 2, 5, 4, 1, 3)
    x = x.reshape(5, seq, SCAN_KH, 2, nb, SCAN_PAIRS).transpose(0, 4, 1, 2, 3, 5)
    return x.reshape(5, nb, seq, SCAN_KH, LANES)


def _scan_val_layout(v, batch, seq):
    nb = batch * RWKV_HEADS // SCAN_PAIRS
    v4 = v.reshape(batch, seq, RWKV_HEADS, RWKV_HEAD).transpose(1, 3, 0, 2)
    return v4.reshape(seq, RWKV_HEAD, nb, SCAN_PAIRS).transpose(2, 0, 1, 3)


def _scan_val_unlayout(y, batch, seq):
    v4 = y.transpose(1, 2, 0, 3).reshape(seq, RWKV_HEAD, batch, RWKV_HEADS)
    return v4.transpose(2, 0, 3, 1).reshape(batch * seq, RWKV_DIM)


def _scan_state_layout(s, batch):
    nb = batch * RWKV_HEADS // SCAN_PAIRS
    s6 = s.reshape(batch, RWKV_HEADS, RWKV_HEAD, 2, SCAN_KH).transpose(4, 2, 3, 0, 1)
    s6 = s6.reshape(SCAN_KH, RWKV_HEAD, 2, nb, SCAN_PAIRS).transpose(3, 0, 1, 2, 4)
    return s6.reshape(nb, SCAN_KH, RWKV_HEAD, LANES)


def _scan_state_unlayout(arr, batch):
    nb = arr.shape[0]
    s = arr.reshape(nb, SCAN_KH, RWKV_HEAD, 2, SCAN_PAIRS).transpose(0, 4, 2, 3, 1)
    return s.reshape(batch, RWKV_HEADS, RWKV_HEAD, RWKV_HEAD)


def _swap_halves(w):
    half = w.shape[-1] // 2
    return jnp.concatenate([w[..., half:], w[..., :half]], axis=-1)


def _prep_even(i, norm_mix, norm_ffn, w_in, q_norm, kv_norm, w_uq, w_uk, w_uv, mu, w0, w2, a0, a2,
               g2, k_k, k_a, r_k, ln_g, ln_b, w_out, ffn_gu, ffn_down):
    w = {}
    row = lambda v: v[i].reshape(1, -1)
    w_in = w_in[i]
    w["norm_mix"] = row(norm_mix)
    w["norm_ffn"] = row(norm_ffn)
    w["w_q"] = w_in[:, :MLA_Q_RANK].astype(BF16)
    w_kv = w_in[:, MLA_Q_RANK:MLA_Q_RANK + MLA_LAT]
    w["w_ckv"] = w_kv[:, :MLA_KV_RANK].astype(BF16)
    lane_pad = lambda m: jnp.pad(m, [(0, 0)] * (m.ndim - 1) + [(0, LANES - m.shape[-1])])
    w["w_pe_a"] = lane_pad(w_kv[:, MLA_KV_RANK:]).astype(BF16)
    w["w_pe_b"] = lane_pad(_swap_halves(w_kv[:, MLA_KV_RANK:])).astype(BF16)
    w["w_rw"] = w_in[:, MLA_Q_RANK + MLA_LAT:].astype(BF16)
    w["q_norm"] = row(q_norm)
    w["kv_norm"] = row(kv_norm)
    uq = w_uq[i].reshape(MLA_Q_RANK, MLA_HEADS, MLA_NOPE + MLA_ROPE)
    uq_pe = uq[:, :, MLA_NOPE:]
    w["w_qpe_a"] = lane_pad(uq_pe).reshape(MLA_Q_RANK, -1).astype(BF16)
    w["w_qpe_b"] = lane_pad(_swap_halves(uq_pe)).reshape(MLA_Q_RANK, -1).astype(BF16)
    w["w_qlat"] = _fold_qlat(uq[:, :, :MLA_NOPE].transpose(1, 0, 2), w_uk[i].transpose(1, 0, 2))
    uv = w_uv[i].transpose(1, 0, 2).reshape(MLA_HEADS // 2, 2, MLA_KV_RANK, MLA_V)
    zero = jnp.zeros_like(uv[:, 0])
    w["w_uv_bd"] = jnp.concatenate(
        [jnp.concatenate([uv[:, 0], zero], axis=-1), jnp.concatenate([zero, uv[:, 1]], axis=-1)],
        axis=1).astype(BF16)
    w["mu"] = row(mu)
    w["w0"] = row(w0)
    pad = lambda m, before: jnp.pad(m, ((before, LANES - before - m.shape[0]), (0, 0))).astype(BF16)
    w["w2p"] = pad(w2[i], 0)
    w["a2p"] = pad(a2[i], RWKV_W_LORA)
    w["a0"] = row(a0)
    w["g2"] = g2[i].astype(BF16)
    w["k_k"] = row(k_k)
    w["k_a"] = row(k_a)
    w["r_k"] = row(r_k)
    w["ln_g"] = row(ln_g)
    w["ln_b"] = row(ln_b)
    head = jnp.arange(RWKV_DIM) // RWKV_HEAD
    w["ones_bd"] = (head[:, None] == head[None, :]).astype(BF16)
    w["w_out_a"] = w_out[i][:MLA_HEADS * MLA_V].astype(BF16)
    w["w_out_b"] = w_out[i][MLA_HEADS * MLA_V:].astype(BF16)
    w["ffn_gu"] = ffn_gu[i].astype(BF16)
    w["ffn_down"] = ffn_down[i].astype(BF16)
    return w


def _prep_odd(i, norm_mix, norm_ffn, w_in, a2, ab, gla_norm, w_out, router, moe_gu, moe_down):
    w = {}
    row = lambda v: v[i].reshape(1, -1)
    w_in = w_in[i]
    w["norm_mix"] = row(norm_mix)
    w["norm_ffn"] = row(norm_ffn)
    w["w_q"] = w_in[:, :GLA_KDIM].astype(BF16)
    w["w_k"] = w_in[:, GLA_KDIM:2 * GLA_KDIM].astype(BF16)
    w["w_v"] = w_in[:, 2 * GLA_KDIM:2 * GLA_KDIM + GLA_VDIM].astype(BF16)
    w["w_g"] = w_in[:, 2 * GLA_KDIM + GLA_VDIM:2 * GLA_KDIM + 2 * GLA_VDIM].astype(BF16)
    w["w_xa"] = jnp.pad(w_in[:, 2 * GLA_KDIM + 2 * GLA_VDIM:],
                        ((0, 0), (0, LANES - GLA_GATE_RANK))).astype(BF16)
    w["a2p"] = jnp.pad(a2[i], ((0, LANES - GLA_GATE_RANK), (0, 0))).astype(BF16)
    w["ab"] = row(ab)
    w["gla_norm"] = row(gla_norm)
    w["w_out"] = w_out[i].astype(BF16)
    w["router"] = jnp.pad(router[i], ((0, 0), (0, LANES - N_EXPERTS)))
    w["layer"] = i
    w["moe_gu"] = moe_gu
    w["moe_down"] = moe_down
    return w


def _rope_tables(pos, reps):
    inv = ROPE_THETA ** (-jnp.arange(0, MLA_ROPE, 2, dtype=F32) / MLA_ROPE)
    ang = pos.astype(F32)[:, None] * inv[None, :]
    cos, sin = jnp.cos(ang), jnp.sin(ang)
    pad = ((0, 0), (0, LANES - MLA_ROPE))
    cs = jnp.tile(jnp.pad(jnp.concatenate([cos, cos], axis=-1), pad), (reps, 1))
    sn = jnp.tile(jnp.pad(jnp.concatenate([-sin, sin], axis=-1), pad), (reps, 1))
    return {"cs": cs, "sn": sn, "cs8": jnp.tile(cs, (1, MLA_HEADS)), "sn8": jnp.tile(sn, (1, MLA_HEADS))}


def _even_layer(x, batch, seq, tabs, state, shift0, past, w, tm, tc):
    n = batch * seq
    lat, lat_b, q_lat, q_pe, rw = _even_in(x, w, tabs, tm)
    if past is None:
        o_lat = _mla_prompt(q_lat, q_pe, lat_b, batch, seq)
    else:
        cache, layer, page_table = past
        rows = seq * MLA_HEADS
        q_full = jnp.concatenate([q_lat.reshape(batch, rows, MLA_KV_RANK),
                                  q_pe.reshape(batch, rows, LANES)[:, :, :MLA_ROPE]], axis=-1)
        new_pad_t = jnp.pad(lat_b.reshape(batch, seq, MLA_LATB)[:, :, :MLA_LAT],
                            ((0, 0), (0, PAGE_SIZE - seq), (0, 0))).transpose(0, 2, 1)
        o_lat = _mla_decode(page_table, q_full, new_pad_t, cache.transpose(0, 1, 3, 2), layer)
        o_lat = o_lat.reshape(n, MLA_HEADS * MLA_KV_RANK)

    rw3 = rw.reshape(batch, seq, RWKV_PROJ)
    xs5, v, g, rkv = _rwkv_prep(rw, shift0, w, tm, seq)
    y_l, s_l = _rwkv_scan(_scan_vec_layout(xs5, batch, seq), _scan_val_layout(v, batch, seq),
                          _scan_state_layout(state, batch), tc)
    y = _scan_val_unlayout(y_l, batch, seq)
    new_state = _scan_state_unlayout(s_l, batch)

    x = _even_out(y, rkv, g, o_lat, x, w, tm)
    x = _ffn(x, w["norm_ffn"], w["ffn_gu"], w["ffn_down"], tm)
    return x, lat.reshape(batch, seq, MLA_LAT), new_state, rw3[:, -1]


def _odd_layer(x, batch, seq, state, w, final_norm, tm, tm_moe):
    q, k, v, gate, la = _odd_in(x, w, tm)
    seq_p = -(-seq // GLA_CHUNK) * GLA_CHUNK
    if seq_p != seq:
        padr = lambda t: jnp.pad(t.reshape(batch, seq, -1), ((0, 0), (0, seq_p - seq), (0, 0))
                                 ).reshape(batch * seq_p, -1)
        qp, kp, vp, lap = padr(q), padr(k), padr(v), padr(la)
    else:
        qp, kp, vp, lap = q, k, v, la
    o, st = _gla(qp, kp, vp, lap, state.transpose(0, 1, 3, 2), batch, seq_p)
    if seq_p != seq:
        o = o.reshape(batch, seq_p, GLA_VDIM)[:, :seq].reshape(batch * seq, GLA_VDIM)
    x = _odd_out(o, gate, x, w, tm)
    xn2, idx, gates = _router(x, w["norm_ffn"], w["router"], tm)
    local, wstart, src, tile_expert, tile_valid = _route(idx[:, :2], tm_moe, MOE_TMC)
    xs = _moe_gather(src, xn2, tm_moe)
    h = _moe_up(tile_expert, tile_valid, xs, w["moe_gu"], w["layer"], tm_moe)
    ys = _moe_down(tile_expert, tile_valid, h, w["moe_down"], w["layer"], tm_moe)
    g1b = jnp.broadcast_to(gates[:, 0:1], (x.shape[0], LANES))
    g2b = jnp.broadcast_to(gates[:, 1:2], (x.shape[0], LANES))
    y = _moe_combine(local, wstart, ys, g1b, g2b, x, final_norm, MOE_TMC)
    return y, st.transpose(0, 1, 3, 2)


def kernel(x_prompt, x_sample, cache_mla, state_rwkv, state_rwkv_shift, state_gla, page_table, norm_mix_even, norm_ffn_even, w_in_even, mla_q_norm, mla_kv_norm, mla_w_uq, mla_w_uk, mla_w_uv, rwkv_mu, rwkv_w0, rwkv_w2, rwkv_a0, rwkv_a2, rwkv_g2, rwkv_k_k, rwkv_k_a, rwkv_r_k, rwkv_ln_g, rwkv_ln_b, w_out_even, ffn_w_gu_even, ffn_w_down_even, norm_mix_odd, norm_ffn_odd, w_in_odd, gla_a2, gla_ab, gla_norm, w_out_odd, moe_router, moe_w_gu, moe_w_down, final_norm):
    bp, tp, _ = x_prompt.shape
    bs, ts, _ = x_sample.shape
    past_len = page_table.shape[1] * PAGE_SIZE
    tm_p, tm_s = 512, bs * ts
    we = _prep_even(0, norm_mix_even, norm_ffn_even, w_in_even, mla_q_norm, mla_kv_norm, mla_w_uq,
                    mla_w_uk, mla_w_uv, rwkv_mu, rwkv_w0, rwkv_w2, rwkv_a0, rwkv_a2, rwkv_g2,
                    rwkv_k_k, rwkv_k_a, rwkv_r_k, rwkv_ln_g, rwkv_ln_b, w_out_even, ffn_w_gu_even,
                    ffn_w_down_even)
    wo = _prep_odd(0, norm_mix_odd, norm_ffn_odd, w_in_odd, gla_a2, gla_ab, gla_norm, w_out_odd,
                   moe_router, moe_w_gu, moe_w_down)
    fn = final_norm.reshape(1, -1)
    tabs_p = _rope_tables(jnp.arange(tp), 1)
    tabs_s = _rope_tables(past_len + jnp.arange(ts), bs)

    hp = x_prompt.reshape(bp * tp, D_MODEL)
    hs = x_sample.reshape(bs * ts, D_MODEL)
    zeros_state = jnp.zeros((bp, RWKV_HEADS, RWKV_HEAD, RWKV_HEAD), F32)
    zeros_shift = jnp.zeros((bp, RWKV_PROJ), F32)
    hp, lat_p, rs_p, sh_p = _even_layer(hp, bp, tp, tabs_p, zeros_state, zeros_shift, None, we,
                                        tm_p, 64)
    hs, lat_s, rs_s, sh_s = _even_layer(hs, bs, ts, tabs_s, state_rwkv[0], state_rwkv_shift[0],
                                        (cache_mla, 0, page_table), we, tm_s, ts)
    zeros_gla = jnp.zeros((bp, GLA_HEADS, GLA_DK, GLA_DV), F32)
    yp, gs_p = _odd_layer(hp, bp, tp, zeros_gla, wo, fn, tm_p, 512)
    ys, gs_s = _odd_layer(hs, bs, ts, state_gla[0], wo, fn, tm_s, 128)
    return (yp.reshape(bp, tp, D_MODEL), ys.reshape(bs, ts, D_MODEL), lat_p[None], lat_s[None],
            rs_p[None], rs_s[None], sh_p[None], sh_s[None], gs_p[None], gs_s[None])
```

```python
import functools

import jax
import jax.numpy as jnp
from jax import lax
from jax.experimental import pallas as pl
from jax.experimental.pallas import tpu as pltpu

F32 = jnp.float32
BF16 = jnp.bfloat16

D_MODEL = 1024
PAGE_SIZE = 128
NORM_EPS = 1e-6

MLA_HEADS = 8
MLA_NOPE = 64
MLA_ROPE = 32
MLA_V = 64
MLA_Q_RANK = 384
MLA_KV_RANK = 256
MLA_LAT = MLA_KV_RANK + MLA_ROPE
MLA_LATB = MLA_KV_RANK + 128
MLA_SCALE = (MLA_NOPE + MLA_ROPE) ** -0.5
ROPE_THETA = 10000.0

RWKV_HEADS = 8
RWKV_HEAD = 64
RWKV_DIM = RWKV_HEADS * RWKV_HEAD
RWKV_W_LORA = 64
RWKV_A_LORA = 64
RWKV_G_LORA = 128
RWKV_PROJ = 3 * RWKV_DIM + RWKV_W_LORA + RWKV_A_LORA + RWKV_G_LORA
RWKV_LN_EPS = 64e-5

GLA_HEADS = 4
GLA_DK = 128
GLA_DV = 256
GLA_KDIM = GLA_HEADS * GLA_DK
GLA_VDIM = GLA_HEADS * GLA_DV
GLA_GATE_RANK = 16
GLA_GATE_NORM = 16.0
GLA_CHUNK = 128

D_FF = 2816
N_EXPERTS = 8
D_FF_EXPERT = 3584

LANES = 128
VMEM_LIMIT = 56 * 1024 * 1024
NEG_BIG = -1e30
LOG2_E = 1.4426950408889634
Q_PRESCALE = MLA_SCALE * LOG2_E


def _cparams(sem):
    return pltpu.CompilerParams(dimension_semantics=sem, vmem_limit_bytes=VMEM_LIMIT)


def _const_spec(shape):
    nd = len(shape)
    return pl.BlockSpec(shape, lambda *_: (0,) * nd)


def _row_spec(tm, width):
    return pl.BlockSpec((tm, width), lambda i: (i, 0))


def _dot(a, b):
    return jnp.dot(a.astype(BF16), b.astype(BF16), preferred_element_type=F32)


def _dot_nt(a, b):
    return lax.dot_general(a.astype(BF16), b.astype(BF16), (((1,), (1,)), ((), ())),
                           preferred_element_type=F32)


def _split2(x):
    hi = x.astype(BF16)
    lo = (x - hi.astype(F32)).astype(BF16)
    return hi, lo


def _split3(x):
    hi = x.astype(BF16)
    r1 = x - hi.astype(F32)
    mid = r1.astype(BF16)
    lo = (r1 - mid.astype(F32)).astype(BF16)
    return hi, mid, lo


def _dot_exact_rhs(x, e):
    hi, mid, lo = _split3(x)
    return (jnp.dot(hi, e, preferred_element_type=F32) + jnp.dot(mid, e, preferred_element_type=F32)
            + jnp.dot(lo, e, preferred_element_type=F32))


def _dot_exact_lhs(e, x):
    hi, mid, lo = _split3(x)
    return (jnp.dot(e, hi, preferred_element_type=F32) + jnp.dot(e, mid, preferred_element_type=F32)
            + jnp.dot(e, lo, preferred_element_type=F32))


def _dot_f32ish(a, b):
    ah, al = _split2(a)
    bh, bl = _split2(b)
    return (jnp.dot(ah, bh, preferred_element_type=F32) + jnp.dot(ah, bl, preferred_element_type=F32)
            + jnp.dot(al, bh, preferred_element_type=F32))


def _lane_tile(x, width):
    return x if width == LANES else jnp.concatenate([x] * (width // LANES), axis=1)


def _rms(x, g, eps=NORM_EPS):
    return x * lax.rsqrt(jnp.mean(x * x, axis=-1, keepdims=True) + eps) * g


def _sigmoid(x):
    return 1.0 / (1.0 + jnp.exp(-x))


def _softplus(x):
    return jnp.maximum(x, 0.0) + jnp.log(1.0 + jnp.exp(-jnp.abs(x)))


def _fold_qlat_kernel(uq_ref, uk_ref, o_ref):
    a = uq_ref[...]
    b = uk_ref[...]
    ah, al = _split2(a)
    bh, bl = _split2(b)
    dn = (((1,), (1,)), ((), ()))
    o = (lax.dot_general(ah, bh, dn, preferred_element_type=F32)
         + lax.dot_general(ah, bl, dn, preferred_element_type=F32)
         + lax.dot_general(al, bh, dn, preferred_element_type=F32))
    o_ref[...] = o.astype(BF16)


def _fold_qlat(uq_nope, uk):
    return pl.pallas_call(
        _fold_qlat_kernel,
        grid=(MLA_HEADS,),
        in_specs=[pl.BlockSpec((None, MLA_Q_RANK, MLA_NOPE), lambda h: (h, 0, 0)),
                  pl.BlockSpec((None, MLA_KV_RANK, MLA_NOPE), lambda h: (h, 0, 0))],
        out_specs=pl.BlockSpec((MLA_Q_RANK, MLA_KV_RANK), lambda h: (0, h)),
        out_shape=jax.ShapeDtypeStruct((MLA_Q_RANK, MLA_HEADS * MLA_KV_RANK), BF16),
        compiler_params=_cparams(("arbitrary",)),
    )(uq_nope, uk)


def _even_in_kernel(x_ref, g_ref, wq_ref, wckv_ref, wpa_ref, wpb_ref, wrw_ref, qn_ref, kvn_ref,
                    cs_ref, sn_ref, wql_ref, wqa_ref, wqb_ref, cs8_ref, sn8_ref,
                    lat_ref, latb_ref, ql_ref, qpe_ref, rw_ref):
    xn = _rms(x_ref[...], g_ref[...]).astype(BF16)
    cq = _rms(_dot(xn, wq_ref[...]), qn_ref[...]).astype(BF16)
    ql_ref[...] = (_dot(cq, wql_ref[...]) * Q_PRESCALE).astype(BF16)
    qpe = _dot(cq, wqa_ref[...]) * cs8_ref[...] + _dot(cq, wqb_ref[...]) * sn8_ref[...]
    qpe_ref[...] = (qpe * Q_PRESCALE).astype(BF16)
    ckv = _rms(_dot(xn, wckv_ref[...]), kvn_ref[...])
    kpe = _dot(xn, wpa_ref[...]) * cs_ref[...] + _dot(xn, wpb_ref[...]) * sn_ref[...]
    lat_ref[:, :MLA_KV_RANK] = ckv
    lat_ref[:, MLA_KV_RANK:] = kpe[:, :MLA_ROPE]
    latb_ref[:, :MLA_KV_RANK] = ckv.astype(BF16)
    latb_ref[:, MLA_KV_RANK:] = kpe.astype(BF16)
    rw_ref[...] = _dot(xn, wrw_ref[...])


def _even_in(x, w, tabs, tm):
    n = x.shape[0]
    nt = tabs["cs"].shape[0] // tm
    tab = lambda width: pl.BlockSpec((tm, width), lambda i: (i % nt, 0))
    hq = MLA_HEADS * MLA_KV_RANK
    hr = MLA_HEADS * LANES
    return pl.pallas_call(
        _even_in_kernel,
        grid=(n // tm,),
        in_specs=[_row_spec(tm, D_MODEL), _const_spec((1, D_MODEL)),
                  _const_spec((D_MODEL, MLA_Q_RANK)), _const_spec((D_MODEL, MLA_KV_RANK)),
                  _const_spec((D_MODEL, LANES)), _const_spec((D_MODEL, LANES)),
                  _const_spec((D_MODEL, RWKV_PROJ)), _const_spec((1, MLA_Q_RANK)),
                  _const_spec((1, MLA_KV_RANK)), tab(LANES), tab(LANES),
                  _const_spec((MLA_Q_RANK, hq)), _const_spec((MLA_Q_RANK, hr)),
                  _const_spec((MLA_Q_RANK, hr)), tab(hr), tab(hr)],
        out_specs=[_row_spec(tm, MLA_LAT), _row_spec(tm, MLA_LATB), _row_spec(tm, hq),
                   _row_spec(tm, hr), _row_spec(tm, RWKV_PROJ)],
        out_shape=[jax.ShapeDtypeStruct((n, MLA_LAT), F32), jax.ShapeDtypeStruct((n, MLA_LATB), BF16),
                   jax.ShapeDtypeStruct((n, hq), BF16), jax.ShapeDtypeStruct((n, hr), BF16),
                   jax.ShapeDtypeStruct((n, RWKV_PROJ), F32)],
        compiler_params=_cparams(("parallel",)),
    )(x, w["norm_mix"], w["w_q"], w["w_ckv"], w["w_pe_a"], w["w_pe_b"], w["w_rw"], w["q_norm"],
      w["kv_norm"], tabs["cs"], tabs["sn"], w["w_qlat"], w["w_qpe_a"], w["w_qpe_b"],
      tabs["cs8"], tabs["sn8"])


ATT_TQ = 256


def _mla_prompt_kernel(qi_ref, kj_ref, ql_ref, qpe_ref, lat_ref, o_ref,
                       m_sc, l_sc, a_sc, acc_sc, s_sc, p_sc):
    step = pl.program_id(1)
    i = qi_ref[step]
    j = kj_ref[step]
    heads = range(MLA_HEADS)

    @pl.when(j == 0)
    def _():
        m_sc[...] = jnp.full(m_sc.shape, NEG_BIG, F32)
        l_sc[...] = jnp.zeros(l_sc.shape, F32)
        acc_sc[...] = jnp.zeros(acc_sc.shape, F32)

    def tile(masked):
        ckv = lat_ref[:, :MLA_KV_RANK]
        kpe = lat_ref[:, MLA_KV_RANK:]
        for h in heads:
            s_sc[h] = (_dot_nt(ql_ref[:, h * MLA_KV_RANK:(h + 1) * MLA_KV_RANK], ckv)
                       + _dot_nt(qpe_ref[:, h * LANES:(h + 1) * LANES], kpe))
        for h in heads:
            s = s_sc[h]
            if masked:
                tok = lax.broadcasted_iota(jnp.int32, s.shape, 0)
                key = lax.broadcasted_iota(jnp.int32, s.shape, 1)
                s = jnp.where(key <= tok, s, NEG_BIG)
            m_prev = m_sc[h]
            m_new = jnp.maximum(m_prev, jnp.max(s, axis=-1, keepdims=True))
            alpha = jnp.exp2(m_prev - m_new)
            p = jnp.exp2(s - _lane_tile(m_new, ATT_TQ))
            l_sc[h] = alpha * l_sc[h] + jnp.sum(p, axis=-1, keepdims=True)
            m_sc[h] = m_new
            a_sc[h] = alpha
            p_sc[h] = p.astype(BF16)
        for h in heads:
            acc_sc[h] = _lane_tile(a_sc[h], MLA_KV_RANK) * acc_sc[h] + _dot(p_sc[h], ckv)

    @pl.when(j < i)
    def _():
        tile(False)

    @pl.when(j == i)
    def _():
        tile(True)
        for h in heads:
            o_ref[:, h * MLA_KV_RANK:(h + 1) * MLA_KV_RANK] = (
                acc_sc[h] / _lane_tile(l_sc[h], MLA_KV_RANK)).astype(BF16)


def _mla_prompt(q_lat, q_pe, lat_b, batch, seq):
    nq = seq // ATT_TQ
    pairs = [(i, j) for i in range(nq) for j in range(i + 1)]
    qi = jnp.array([p[0] for p in pairs], jnp.int32)
    kj = jnp.array([p[1] for p in pairs], jnp.int32)
    hq = MLA_HEADS * MLA_KV_RANK
    grid_spec = pltpu.PrefetchScalarGridSpec(
        num_scalar_prefetch=2,
        grid=(batch, len(pairs)),
        in_specs=[pl.BlockSpec((ATT_TQ, hq), lambda b, s, qi, kj: (b * nq + qi[s], 0)),
                  pl.BlockSpec((ATT_TQ, MLA_HEADS * LANES), lambda b, s, qi, kj: (b * nq + qi[s], 0)),
                  pl.BlockSpec((ATT_TQ, MLA_LATB), lambda b, s, qi, kj: (b * nq + kj[s], 0))],
        out_specs=pl.BlockSpec((ATT_TQ, hq), lambda b, s, qi, kj: (b * nq + qi[s], 0)),
        scratch_shapes=[pltpu.VMEM((MLA_HEADS, ATT_TQ, LANES), F32),
                        pltpu.VMEM((MLA_HEADS, ATT_TQ, LANES), F32),
                        pltpu.VMEM((MLA_HEADS, ATT_TQ, LANES), F32),
                        pltpu.VMEM((MLA_HEADS, ATT_TQ, MLA_KV_RANK), F32),
                        pltpu.VMEM((MLA_HEADS, ATT_TQ, ATT_TQ), F32),
                        pltpu.VMEM((MLA_HEADS, ATT_TQ, ATT_TQ), BF16)],
    )
    return pl.pallas_call(
        _mla_prompt_kernel,
        grid_spec=grid_spec,
        out_shape=jax.ShapeDtypeStruct(q_lat.shape, BF16),
        compiler_params=_cparams(("parallel", "arbitrary")),
    )(qi, kj, q_lat, q_pe, lat_b)


PAGES_PER_STEP = 16
DECODE_GROUPS = 4


def _mla_decode_kernel(pt_ref, q_ref, new_ref, *rest):
    page_refs = rest[:PAGES_PER_STEP]
    o_ref, m_sc, l_sc, acc_sc = rest[PAGES_PER_STEP:]
    j = pl.program_id(1)
    q = q_ref[0]

    @pl.when(j == 0)
    def _():
        m_sc[...] = jnp.full(m_sc.shape, NEG_BIG, F32)
        l_sc[...] = jnp.zeros(l_sc.shape, F32)
        acc_sc[...] = jnp.zeros(acc_sc.shape, F32)

    def update(state, s, values_t):
        m_prev, l_prev, acc = state
        m_new = jnp.maximum(m_prev, jnp.max(s, axis=-1, keepdims=True))
        alpha = jnp.exp2(m_prev - m_new)
        p = jnp.exp2(s - _lane_tile(m_new, s.shape[1]))
        l_new = alpha * l_prev + jnp.sum(p, axis=-1, keepdims=True)
        return m_new, l_new, _lane_tile(alpha, MLA_KV_RANK) * acc + _dot_nt(p, values_t)

    group = PAGES_PER_STEP // DECODE_GROUPS
    keys = [jnp.concatenate([pr[...].astype(BF16) for pr in page_refs[g * group:(g + 1) * group]],
                            axis=1) for g in range(DECODE_GROUPS)]
    scores = [_dot(q, kt) for kt in keys]
    state = (m_sc[...], l_sc[...], acc_sc[...])
    for s, kt in zip(scores, keys):
        state = update(state, s, kt[:MLA_KV_RANK, :])
    m_sc[...], l_sc[...], acc_sc[...] = state

    @pl.when(j == pl.num_programs(1) - 1)
    def _():
        new_t = new_ref[0]
        sn = _dot(q, new_t)
        tok = lax.broadcasted_iota(jnp.int32, sn.shape, 0) >> 3
        key = lax.broadcasted_iota(jnp.int32, sn.shape, 1)
        sn = jnp.where(key <= tok, sn, NEG_BIG)
        _, l_fin, acc_fin = update(state, sn, new_t[:MLA_KV_RANK, :])
        o_ref[0] = (acc_fin / _lane_tile(l_fin, MLA_KV_RANK)).astype(BF16)


def _mla_decode(page_table, q_full, new_pad_t, cache_t, layer):
    db, n_pages = page_table.shape
    rows = q_full.shape[1]
    steps = n_pages // PAGES_PER_STEP

    def page_spec(p):
        return pl.BlockSpec((None, None, MLA_LAT, PAGE_SIZE),
                            lambda b, j, pt: (layer, pt[b, j * PAGES_PER_STEP + p], 0, 0))

    grid_spec = pltpu.PrefetchScalarGridSpec(
        num_scalar_prefetch=1,
        grid=(db, steps),
        in_specs=[pl.BlockSpec((1, rows, MLA_LAT), lambda b, j, pt: (b, 0, 0)),
                  pl.BlockSpec((1, MLA_LAT, PAGE_SIZE), lambda b, j, pt: (b, 0, 0))]
        + [page_spec(p) for p in range(PAGES_PER_STEP)],
        out_specs=pl.BlockSpec((1, rows, MLA_KV_RANK), lambda b, j, pt: (b, 0, 0)),
        scratch_shapes=[pltpu.VMEM((rows, LANES), F32), pltpu.VMEM((rows, LANES), F32),
                        pltpu.VMEM((rows, MLA_KV_RANK), F32)],
    )
    return pl.pallas_call(
        _mla_decode_kernel,
        grid_spec=grid_spec,
        out_shape=jax.ShapeDtypeStruct((db, rows, MLA_KV_RANK), BF16),
        compiler_params=_cparams(("parallel", "arbitrary")),
    )(page_table, q_full, new_pad_t, *([cache_t] * PAGES_PER_STEP))


def _rwkv_prep_kernel(rw_ref, before_ref, sh_ref, mu_ref, w0_ref, w2_ref, a0_ref, a2_ref, g2_ref,
                      kk_ref, ka_ref, rk_ref, ones_ref, xs_ref, v_ref, g_ref, rkv_ref, *, tm, seq):
    rw = rw_ref[...]
    rolled = pltpu.roll(rw, 1, axis=0)
    row = lax.broadcasted_iota(jnp.int32, rw.shape, 0)
    if seq >= tm:
        at_start = pl.program_id(0) % (seq // tm) == 0
        first = jnp.where(at_start, sh_ref[...], before_ref[7:8, :])
        prev = jnp.where(row == 0, first, rolled)
    else:
        prev = jnp.where((row & (seq - 1)) == 0, sh_ref[...], rolled)
    xs = rw + (prev - rw) * mu_ref[...]
    d = RWKV_DIM
    r = xs[:, :d]
    k = xs[:, d:2 * d]
    v = xs[:, 2 * d:3 * d]
    xwa = xs[:, 3 * d:3 * d + LANES]
    xg = xs[:, 3 * d + LANES:]
    ones = ones_ref[...]
    w_log = -_softplus(-(w0_ref[...] + _dot(jnp.tanh(xwa), w2_ref[...]))) - 0.5
    a = _sigmoid(a0_ref[...] + _dot(xwa, a2_ref[...]))
    g_ref[...] = _dot(_sigmoid(xg), g2_ref[...])
    kk = k * kk_ref[...]
    ss = _dot_exact_rhs(kk * kk, ones)
    kk = kk / jnp.maximum(jnp.sqrt(ss), 1e-12)
    k2 = k * (1.0 + (a - 1.0) * ka_ref[...])
    xs_ref[0] = -kk
    xs_ref[1] = jnp.exp(-jnp.exp(w_log))
    xs_ref[2] = kk * a
    xs_ref[3] = k2
    xs_ref[4] = r
    v_ref[...] = v
    rkv_ref[...] = _dot_exact_rhs(r * k2 * rk_ref[...], ones) * v


def _rwkv_prep(rw, shift0, w, tm, seq):
    n = rw.shape[0]
    d = RWKV_DIM
    vec = _const_spec((1, d))
    if seq >= tm:
        tiles = seq // tm
        sh = shift0.reshape(-1, 1, RWKV_PROJ)
        sh_spec = pl.BlockSpec((None, 1, RWKV_PROJ), lambda i: (i // tiles, 0, 0))
    else:
        sh = jnp.repeat(shift0, seq, axis=0)
        sh_spec = _row_spec(tm, RWKV_PROJ)
    before_spec = pl.BlockSpec((8, RWKV_PROJ), lambda i: (jnp.maximum(i * (tm // 8) - 1, 0), 0))
    return pl.pallas_call(
        functools.partial(_rwkv_prep_kernel, tm=tm, seq=seq),
        grid=(n // tm,),
        in_specs=[_row_spec(tm, RWKV_PROJ), before_spec, sh_spec, _const_spec((1, RWKV_PROJ)),
                  vec, _const_spec((LANES, d)), vec, _const_spec((LANES, d)),
                  _const_spec((RWKV_G_LORA, d)), vec, vec, vec, _const_spec((d, d))],
        out_specs=[pl.BlockSpec((5, tm, d), lambda i: (0, i, 0))] + [_row_spec(tm, d)] * 3,
        out_shape=[jax.ShapeDtypeStruct((5, n, d), F32)] + [jax.ShapeDtypeStruct((n, d), F32)] * 3,
        compiler_params=_cparams(("parallel",)),
    )(rw, rw, sh, w["mu"], w["w0"], w["w2p"], w["a0"], w["a2p"], w["g2"], w["k_k"], w["k_a"],
      w["r_k"], w["ones_bd"])


SCAN_KH = RWKV_HEAD // 2
SCAN_PAIRS = LANES // 2
SCAN_VR = RWKV_HEAD // 2


def _rwkv_scan_kernel(x_ref, v_ref, s0_ref, after_ref, y_ref, s_ref, c_sc, d_sc, *, tc):
    del after_ref
    @pl.when(pl.program_id(1) == 0)
    def _():
        s_ref[...] = s0_ref[...]

    half_a = slice(0, SCAN_VR)
    half_b = slice(SCAN_VR, RWKV_HEAD)

    def both_halves(p):
        return p + pltpu.roll(p, SCAN_PAIRS, axis=1)

    def key_dot(u, w):
        return both_halves(jnp.sum(u * w, axis=0, keepdims=True))

    def first_partial(rows):
        p = s_ref[0, 0, rows, :] * x_ref[0, 0, 0, 0:1, :]
        for k in range(1, SCAN_KH):
            p = p + s_ref[0, k, rows, :] * x_ref[0, 0, 0, k:k + 1, :]
        return p

    def half_step(t, rows, sa):
        v_half = v_ref[0, t, rows, :]
        v = jnp.concatenate([v_half, v_half], axis=1)
        q = None
        y = None
        for k in range(SCAN_KH):
            s_old = s_ref[0, k, rows, :]
            qk = s_old * c_sc[k:k + 1, :]
            sn = (s_old * x_ref[1, 0, t, k:k + 1, :] + sa * x_ref[2, 0, t, k:k + 1, :]
                  + v * x_ref[3, 0, t, k:k + 1, :])
            s_ref[0, k, rows, :] = sn
            yk = sn * x_ref[4, 0, t, k:k + 1, :]
            q = qk if q is None else q + qk
            y = yk if y is None else y + yk
        return q, sa * d_sc[0:1, :] + v * d_sc[1:2, :], y

    def store_y(t, y_a, y_b):
        y_ref[0, t, half_a, :] = both_halves(y_a)[:, :SCAN_PAIRS]
        y_ref[0, t, half_b, :] = both_halves(y_b)[:, :SCAN_PAIRS]

    def step(t, carry):
        sa_a, q_b, corr_b, y_a, y_b = carry
        store_y(jnp.maximum(t - 1, 0), y_a, y_b)
        a_next = x_ref[0, 0, jnp.minimum(t + 1, tc - 1)]
        c_sc[...] = x_ref[1, 0, t] * a_next
        d_sc[0:1, :] = key_dot(x_ref[2, 0, t], a_next)
        d_sc[1:2, :] = key_dot(x_ref[3, 0, t], a_next)
        sa_b = both_halves(q_b) + corr_b
        q_a, corr_a, y_a_new = half_step(t, half_a, sa_a)
        sa_a_next = both_halves(q_a) + corr_a
        q_b_next, corr_b_next, y_b_new = half_step(t, half_b, sa_b)
        return sa_a_next, q_b_next, corr_b_next, y_a_new, y_b_new

    zero = jnp.zeros((SCAN_VR, LANES), F32)
    init = (both_halves(first_partial(half_a)), first_partial(half_b), zero, zero, zero)
    final = lax.fori_loop(0, tc, step, init)
    store_y(tc - 1, final[3], final[4])


def _rwkv_scan(xs, v, s0, after, tc):
    _, nb, t, _, _ = xs.shape
    xspec = pl.BlockSpec((5, 1, tc, SCAN_KH, LANES), lambda n, c: (0, n, c, 0, 0))
    vspec = pl.BlockSpec((1, tc, RWKV_HEAD, SCAN_PAIRS), lambda n, c: (n, c, 0, 0))
    sspec = pl.BlockSpec((1, SCAN_KH, RWKV_HEAD, LANES), lambda n, c: (n, 0, 0, 0))
    return pl.pallas_call(
        functools.partial(_rwkv_scan_kernel, tc=tc),
        grid=(nb, t // tc),
        in_specs=[xspec, vspec, sspec, pl.BlockSpec(memory_space=pl.ANY)],
        out_specs=[vspec, sspec],
        out_shape=[jax.ShapeDtypeStruct(v.shape, F32), jax.ShapeDtypeStruct(s0.shape, F32)],
        scratch_shapes=[pltpu.VMEM((SCAN_KH, LANES), F32), pltpu.VMEM((8, LANES), F32)],
        compiler_params=_cparams(("parallel", "arbitrary")),
    )(xs, v, s0, after)


def _even_out_kernel(y_ref, rkv_ref, g_ref, lng_ref, lnb_ref, ones_ref, ol_ref, wuv_ref, woa_ref,
                     wob_ref, x_ref, o_ref):
    ones = ones_ref[...]
    y = y_ref[...]
    inv = 1.0 / RWKV_HEAD
    mean = _dot_exact_rhs(y, ones) * inv
    dlt = y - mean
    var = _dot_exact_rhs(dlt * dlt, ones) * inv
    yn = dlt * lax.rsqrt(var + RWKV_LN_EPS) * lng_ref[...] + lnb_ref[...] + rkv_ref[...]
    ob = (yn * g_ref[...]).astype(BF16)
    pair = 2 * MLA_KV_RANK
    oa = jnp.concatenate(
        [_dot(ol_ref[:, p * pair:(p + 1) * pair], wuv_ref[p]) for p in range(MLA_HEADS // 2)], axis=1)
    o_ref[...] = x_ref[...] + _dot(oa, woa_ref[...]) + _dot(ob, wob_ref[...])


def _even_out(y, rkv, g, o_lat, x, w, tm):
    n = x.shape[0]
    d = RWKV_DIM
    hq = MLA_HEADS * MLA_KV_RANK
    return pl.pallas_call(
        _even_out_kernel,
        grid=(n // tm,),
        in_specs=[_row_spec(tm, d), _row_spec(tm, d), _row_spec(tm, d), _const_spec((1, d)),
                  _const_spec((1, d)), _const_spec((d, d)), _row_spec(tm, hq),
                  _const_spec((MLA_HEADS // 2, 2 * MLA_KV_RANK, 2 * MLA_V)),
                  _const_spec((MLA_HEADS * MLA_V, D_MODEL)), _const_spec((d, D_MODEL)),
                  _row_spec(tm, D_MODEL)],
        out_specs=_row_spec(tm, D_MODEL),
        out_shape=jax.ShapeDtypeStruct((n, D_MODEL), F32),
        compiler_params=_cparams(("parallel",)),
    )(y, rkv, g, w["ln_g"], w["ln_b"], w["ones_bd"], o_lat, w["w_uv_bd"], w["w_out_a"],
      w["w_out_b"], x)


FFN_TF = 1408


def _ffn_kernel(x_ref, g_ref, wg_ref, wu_ref, wd_ref, o_ref, xn_sc, acc_sc):
    f = pl.program_id(1)

    @pl.when(f == 0)
    def _():
        xn_sc[...] = _rms(x_ref[...], g_ref[...]).astype(BF16)
        acc_sc[...] = jnp.zeros(acc_sc.shape, F32)

    xn = xn_sc[...]
    gate = _dot(xn, wg_ref[...])
    up = _dot(xn, wu_ref[...])
    acc_sc[...] += _dot(gate * _sigmoid(gate) * up, wd_ref[...])

    @pl.when(f == pl.num_programs(1) - 1)
    def _():
        o_ref[...] = x_ref[...] + acc_sc[...]


def _ffn(x, g, w_gu, w_down, tm):
    n = x.shape[0]
    nf = D_FF // FFN_TF
    return pl.pallas_call(
        _ffn_kernel,
        grid=(n // tm, nf),
        in_specs=[pl.BlockSpec((tm, D_MODEL), lambda i, f: (i, 0)),
                  pl.BlockSpec((1, D_MODEL), lambda i, f: (0, 0)),
                  pl.BlockSpec((D_MODEL, FFN_TF), lambda i, f: (0, f)),
                  pl.BlockSpec((D_MODEL, FFN_TF), lambda i, f: (0, nf + f)),
                  pl.BlockSpec((FFN_TF, D_MODEL), lambda i, f: (f, 0))],
        out_specs=pl.BlockSpec((tm, D_MODEL), lambda i, f: (i, 0)),
        out_shape=jax.ShapeDtypeStruct((n, D_MODEL), F32),
        scratch_shapes=[pltpu.VMEM((tm, D_MODEL), BF16), pltpu.VMEM((tm, D_MODEL), F32)],
        compiler_params=_cparams(("parallel", "arbitrary")),
    )(x, g, w_gu, w_gu, w_down)


def _odd_in_kernel(x_ref, g_ref, wq_ref, wk_ref, wv_ref, wg_ref, wxa_ref, a2_ref, ab_ref,
                   q_ref, k_ref, v_ref, gate_ref, la_ref):
    xn = _rms(x_ref[...], g_ref[...]).astype(BF16)
    q_ref[...] = _dot(xn, wq_ref[...]) * (GLA_DK ** -0.5)
    k_ref[...] = _dot(xn, wk_ref[...])
    v_ref[...] = _dot(xn, wv_ref[...])
    gate_ref[...] = _dot(xn, wg_ref[...])
    z = _dot(_dot(xn, wxa_ref[...]), a2_ref[...]) + ab_ref[...]
    la_ref[...] = -_softplus(-z) * (1.0 / GLA_GATE_NORM)


def _odd_in(x, w, tm):
    n = x.shape[0]
    return pl.pallas_call(
        _odd_in_kernel,
        grid=(n // tm,),
        in_specs=[_row_spec(tm, D_MODEL), _const_spec((1, D_MODEL)),
                  _const_spec((D_MODEL, GLA_KDIM)), _const_spec((D_MODEL, GLA_KDIM)),
                  _const_spec((D_MODEL, GLA_VDIM)), _const_spec((D_MODEL, GLA_VDIM)),
                  _const_spec((D_MODEL, LANES)), _const_spec((LANES, GLA_KDIM)),
                  _const_spec((1, GLA_KDIM))],
        out_specs=[_row_spec(tm, GLA_KDIM), _row_spec(tm, GLA_KDIM), _row_spec(tm, GLA_VDIM),
                   _row_spec(tm, GLA_VDIM), _row_spec(tm, GLA_KDIM)],
        out_shape=[jax.ShapeDtypeStruct((n, GLA_KDIM), F32), jax.ShapeDtypeStruct((n, GLA_KDIM), F32),
                   jax.ShapeDtypeStruct((n, GLA_VDIM), F32), jax.ShapeDtypeStruct((n, GLA_VDIM), F32),
                   jax.ShapeDtypeStruct((n, GLA_KDIM), F32)],
        compiler_params=_cparams(("parallel",)),
    )(x, w["norm_mix"], w["w_q"], w["w_k"], w["w_v"], w["w_g"], w["w_xa"], w["a2p"], w["ab"])


def _gla_kernel(q_ref, k_ref, v_ref, la_ref, s0_ref, o_ref, st_ref):
    c = GLA_CHUNK

    @pl.when(pl.program_id(1) == 0)
    def _():
        st_ref[...] = s0_ref[...]

    row = lax.broadcasted_iota(jnp.int32, (c, c), 0)
    col = lax.broadcasted_iota(jnp.int32, (c, c), 1)
    tri = row >= col
    tri_b = jnp.where(tri, 1.0, 0.0).astype(BF16)
    for h in range(GLA_HEADS):
        ks = slice(h * GLA_DK, (h + 1) * GLA_DK)
        vs = slice(h * GLA_DV, (h + 1) * GLA_DV)
        b = _dot_exact_lhs(tri_b, la_ref[:, ks])
        q = q_ref[:, ks]
        k = k_ref[:, ks]
        v = v_ref[:, vs]
        b_end = b[c - 1:c, :]
        qe = (q * jnp.exp(b)).astype(BF16)
        ke = (k * jnp.exp(-b)).astype(BF16)
        a_mat = jnp.where(tri, _dot_nt(qe, ke), 0.0)
        st = st_ref[0, h]
        o_ref[:, vs] = _dot_nt(qe, st) + _dot(a_mat, v)
        k_end = k * jnp.exp(b_end - b)
        st_ref[0, h] = st * jnp.exp(b_end) + _dot(v.T, k_end)


def _gla(q, k, v, la, s0t, batch, seq):
    nc = seq // GLA_CHUNK
    rspec = lambda width: pl.BlockSpec((GLA_CHUNK, width), lambda b, c: (b * nc + c, 0))
    sspec = pl.BlockSpec((1, GLA_HEADS, GLA_DV, GLA_DK), lambda b, c: (b, 0, 0, 0))
    return pl.pallas_call(
        _gla_kernel,
        grid=(batch, nc),
        in_specs=[rspec(GLA_KDIM), rspec(GLA_KDIM), rspec(GLA_VDIM), rspec(GLA_KDIM), sspec],
        out_specs=[rspec(GLA_VDIM), sspec],
        out_shape=[jax.ShapeDtypeStruct(v.shape, F32), jax.ShapeDtypeStruct(s0t.shape, F32)],
        compiler_params=_cparams(("parallel", "arbitrary")),
    )(q, k, v, la, s0t)


def _odd_out_kernel(o_ref, gate_ref, gn_ref, wo_ref, x_ref, y_ref):
    parts = []
    for h in range(GLA_HEADS):
        vs = slice(h * GLA_DV, (h + 1) * GLA_DV)
        parts.append(_rms(o_ref[:, vs], gn_ref[:, vs]))
    gate = gate_ref[...]
    on = jnp.concatenate(parts, axis=1) * (gate * _sigmoid(gate))
    y_ref[...] = x_ref[...] + _dot(on, wo_ref[...])


def _odd_out(o, gate, x, w, tm):
    n = x.shape[0]
    return pl.pallas_call(
        _odd_out_kernel,
        grid=(n // tm,),
        in_specs=[_row_spec(tm, GLA_VDIM), _row_spec(tm, GLA_VDIM), _const_spec((1, GLA_VDIM)),
                  _const_spec((GLA_VDIM, D_MODEL)), _row_spec(tm, D_MODEL)],
        out_specs=_row_spec(tm, D_MODEL),
        out_shape=jax.ShapeDtypeStruct((n, D_MODEL), F32),
        compiler_params=_cparams(("parallel",)),
    )(o, gate, w["gla_norm"], w["w_out"], x)


def _router_kernel(x_ref, g_ref, wr_ref, xn_ref, idx_ref, gate_ref):
    xn = _rms(x_ref[...], g_ref[...])
    half = D_MODEL // 2
    xn_ref[0] = xn[:, :half]
    xn_ref[1] = xn[:, half:]
    logits = _dot_f32ish(xn, wr_ref[...])
    lane = lax.broadcasted_iota(jnp.int32, logits.shape, 1)
    logits = jnp.where(lane < N_EXPERTS, logits, NEG_BIG)
    m1 = jnp.max(logits, axis=-1, keepdims=True)
    i1 = jnp.min(jnp.where(logits == m1, lane, LANES), axis=-1, keepdims=True)
    rest = jnp.where(lane == i1, NEG_BIG, logits)
    m2 = jnp.max(rest, axis=-1, keepdims=True)
    i2 = jnp.min(jnp.where(rest == m2, lane, LANES), axis=-1, keepdims=True)
    e2 = jnp.exp(m2 - m1)
    g1 = 1.0 / (1.0 + e2)
    g2 = e2 / (1.0 + e2)
    idx_ref[...] = jnp.where(lane == 0, i1, jnp.where(lane == 1, i2, 0))
    gate_ref[...] = jnp.where(lane == 0, g1, jnp.where(lane == 1, g2, 0.0))


def _router(x, g, wr, tm):
    n = x.shape[0]
    half = D_MODEL // 2
    return pl.pallas_call(
        _router_kernel,
        grid=(n // tm,),
        in_specs=[_row_spec(tm, D_MODEL), _const_spec((1, D_MODEL)), _const_spec((D_MODEL, LANES))],
        out_specs=[pl.BlockSpec((2, tm, half), lambda i: (0, i, 0)), _row_spec(tm, LANES),
                   _row_spec(tm, LANES)],
        out_shape=[jax.ShapeDtypeStruct((2, n, half), F32),
                   jax.ShapeDtypeStruct((n, LANES), jnp.int32), jax.ShapeDtypeStruct((n, LANES), F32)],
        compiler_params=_cparams(("parallel",)),
    )(x, g, wr)


MOE_TF = 1792
MOE_TMC = 256


def _route(top_i, tm, tmc):
    n = top_i.shape[0]
    slots = 2 * n
    n_tiles = -(-(slots + N_EXPERTS * (tm - 1)) // tm)
    win = tmc + 8
    e_flat = top_i.reshape(-1)
    onehot = (e_flat[:, None] == jnp.arange(N_EXPERTS, dtype=jnp.int32)[None, :]).astype(jnp.int32)
    csum = jnp.cumsum(onehot, axis=0)
    rank = jnp.sum(onehot * csum, axis=1) - 1
    counts = csum[-1]
    padded = ((counts + tm - 1) // tm) * tm
    ends = jnp.cumsum(padded)
    starts = ends - padded
    dest = (jnp.sum(onehot * starts[None, :], axis=1) + rank).astype(jnp.int32)
    tile_start = jnp.arange(n_tiles, dtype=jnp.int32) * tm
    tile_expert = jnp.minimum(jnp.sum((tile_start[:, None] >= ends[None, :]).astype(jnp.int32), axis=1),
                              N_EXPERTS - 1).astype(jnp.int32)
    tile_valid = (tile_start < ends[-1]).astype(jnp.int32)
    src = jnp.zeros((n_tiles * tm,), jnp.int32).at[dest].set(jnp.arange(slots, dtype=jnp.int32) // 2)
    before = jnp.concatenate([jnp.zeros((1, N_EXPERTS), jnp.int32), csum[2 * tmc - 1:-1:2 * tmc]], axis=0)
    wstart = jnp.clip(((starts[None, :] + before) // 8) * 8, 0, n_tiles * tm - win).astype(jnp.int32)
    ws_slot = jnp.sum(onehot * jnp.repeat(wstart, 2 * tmc, axis=0), axis=1)
    local = (e_flat * win + dest - ws_slot).astype(jnp.int32)
    return local, wstart.reshape(-1), src, tile_expert, tile_valid


def _moe_gather_kernel(src_ref, x_ref, o_ref, *, tg):
    base = pl.program_id(1) * tg

    def body(r, carry):
        o_ref[pl.ds(r, 1), :] = x_ref[pl.ds(src_ref[base + r], 1), :]
        return carry

    lax.fori_loop(0, tg, body, 0, unroll=8)


def _moe_gather(src, xn2, tg):
    rows = src.shape[0]
    _, n, half = xn2.shape
    grid_spec = pltpu.PrefetchScalarGridSpec(
        num_scalar_prefetch=1,
        grid=(2, rows // tg),
        in_specs=[pl.BlockSpec((None, n, half), lambda h, i, s: (h, 0, 0),
                               pipeline_mode=pl.Buffered(1))],
        out_specs=pl.BlockSpec((tg, half), lambda h, i, s: (i, h)),
    )
    return pl.pallas_call(
        functools.partial(_moe_gather_kernel, tg=tg),
        grid_spec=grid_spec,
        out_shape=jax.ShapeDtypeStruct((rows, 2 * half), F32),
        compiler_params=_cparams(("arbitrary", "arbitrary")),
    )(src, xn2)


def _moe_up_kernel(te_ref, tv_ref, xs_ref, wg_ref, wu_ref, h_ref):
    @pl.when(tv_ref[pl.program_id(1)] != 0)
    def _():
        xs = xs_ref[...].astype(BF16)
        gate = _dot(xs, wg_ref[...])
        up = _dot(xs, wu_ref[...])
        h_ref[...] = (gate * _sigmoid(gate) * up).astype(BF16)


def _moe_up(te, tv, xs, w_gu, layer, tm):
    rows = xs.shape[0]
    nf = D_FF_EXPERT // MOE_TF
    wspec = lambda off: pl.BlockSpec((None, None, D_MODEL, MOE_TF),
                                     lambda f, t, te, tv: (layer, te[t], 0, off + f))
    grid_spec = pltpu.PrefetchScalarGridSpec(
        num_scalar_prefetch=2,
        grid=(nf, rows // tm),
        in_specs=[pl.BlockSpec((tm, D_MODEL), lambda f, t, te, tv: (t, 0)), wspec(0), wspec(nf)],
        out_specs=pl.BlockSpec((tm, MOE_TF), lambda f, t, te, tv: (t, f)),
    )
    return pl.pallas_call(
        _moe_up_kernel,
        grid_spec=grid_spec,
        out_shape=jax.ShapeDtypeStruct((rows, D_FF_EXPERT), BF16),
        compiler_params=_cparams(("arbitrary", "arbitrary")),
    )(te, tv, xs, w_gu, w_gu)


def _moe_down_kernel(te_ref, tv_ref, h_ref, wd_ref, y_ref):
    @pl.when(tv_ref[pl.program_id(0)] != 0)
    def _():
        y_ref[...] = _dot(h_ref[...], wd_ref[...])


def _moe_down(te, tv, h, w_down, layer, tm):
    rows = h.shape[0]
    grid_spec = pltpu.PrefetchScalarGridSpec(
        num_scalar_prefetch=2,
        grid=(rows // tm,),
        in_specs=[pl.BlockSpec((tm, D_FF_EXPERT), lambda t, te, tv: (t, 0)),
                  pl.BlockSpec((None, None, D_FF_EXPERT, D_MODEL),
                               lambda t, te, tv: (layer, te[t], 0, 0))],
        out_specs=pl.BlockSpec((tm, D_MODEL), lambda t, te, tv: (t, 0)),
    )
    return pl.pallas_call(
        _moe_down_kernel,
        grid_spec=grid_spec,
        out_shape=jax.ShapeDtypeStruct((rows, D_MODEL), F32),
        compiler_params=_cparams(("arbitrary",)),
    )(te, tv, h, w_down)


def _moe_combine_kernel(ws_ref, local_ref, *refs, tmc, win):
    win_refs = refs[:N_EXPERTS]
    g1_ref, g2_ref, x_ref, fn_ref, o_ref, buf = refs[N_EXPERTS:]
    for e in range(N_EXPERTS):
        buf[e * win:(e + 1) * win, :] = win_refs[e][...]
    base = 2 * pl.program_id(0) * tmc

    def body(r, carry):
        row = pl.ds(r, 1)
        y1 = buf[pl.ds(local_ref[base + 2 * r], 1), :]
        y2 = buf[pl.ds(local_ref[base + 2 * r + 1], 1), :]
        g1 = g1_ref[row, :]
        g2 = g2_ref[row, :]
        parts = []
        for c in range(D_MODEL // LANES):
            cs = slice(c * LANES, (c + 1) * LANES)
            parts.append(g1 * y1[:, cs] + g2 * y2[:, cs])
        o_ref[row, :] = x_ref[row, :] + jnp.concatenate(parts, axis=1)
        return carry

    lax.fori_loop(0, tmc, body, 0, unroll=4)
    o_ref[...] = _rms(o_ref[...], fn_ref[...])


def _moe_combine(local, wstart, ys, g1b, g2b, x, fn, tmc):
    n = x.shape[0]
    win = tmc + 8

    def win_spec(e):
        return pl.BlockSpec((pl.Element(win), pl.Element(D_MODEL)),
                            lambda i, ws, lo: (pl.multiple_of(ws[i * N_EXPERTS + e], 8), 0))

    grid_spec = pltpu.PrefetchScalarGridSpec(
        num_scalar_prefetch=2,
        grid=(n // tmc,),
        in_specs=[win_spec(e) for e in range(N_EXPERTS)]
        + [pl.BlockSpec((tmc, LANES), lambda i, ws, lo: (i, 0)),
           pl.BlockSpec((tmc, LANES), lambda i, ws, lo: (i, 0)),
           pl.BlockSpec((tmc, D_MODEL), lambda i, ws, lo: (i, 0)),
           pl.BlockSpec((1, D_MODEL), lambda i, ws, lo: (0, 0))],
        out_specs=pl.BlockSpec((tmc, D_MODEL), lambda i, ws, lo: (i, 0)),
        scratch_shapes=[pltpu.VMEM((N_EXPERTS * win, D_MODEL), F32)],
    )
    return pl.pallas_call(
        functools.partial(_moe_combine_kernel, tmc=tmc, win=win),
        grid_spec=grid_spec,
        out_shape=jax.ShapeDtypeStruct((n, D_MODEL), F32),
        compiler_params=_cparams(("arbitrary",)),
    )(wstart, local, *([ys] * N_EXPERTS), g1b, g2b, x, fn)


def _scan_vec_layout(xs, batch, seq):
    nb = batch * RWKV_HEADS // SCAN_PAIRS
    x = xs.reshape(5, batch, seq, RWKV_HEADS, 2, SCAN_KH).transpose(0, 2, 5, 4, 1, 3)
    x = x.reshape(5, seq, SCAN_KH, 2, nb, SCAN_PAIRS).transpose(0, 4, 1, 2, 3, 5)
    return x.reshape(5, nb, seq, SCAN_KH, LANES)


def _scan_val_layout(v, batch, seq):
    nb = batch * RWKV_HEADS // SCAN_PAIRS
    v4 = v.reshape(batch, seq, RWKV_HEADS, RWKV_HEAD).transpose(1, 3, 0, 2)
    return v4.reshape(seq, RWKV_HEAD, nb, SCAN_PAIRS).transpose(2, 0, 1, 3)


def _scan_val_unlayout(y, batch, seq):
    v4 = y.transpose(1, 2, 0, 3).reshape(seq, RWKV_HEAD, batch, RWKV_HEADS)
    return v4.transpose(2, 0, 3, 1).reshape(batch * seq, RWKV_DIM)


def _scan_state_layout(s, batch):
    nb = batch * RWKV_HEADS // SCAN_PAIRS
    s6 = s.reshape(batch, RWKV_HEADS, RWKV_HEAD, 2, SCAN_KH).transpose(4, 2, 3, 0, 1)
    s6 = s6.reshape(SCAN_KH, RWKV_HEAD, 2, nb, SCAN_PAIRS).transpose(3, 0, 1, 2, 4)
    return s6.reshape(nb, SCAN_KH, RWKV_HEAD, LANES)


def _scan_state_unlayout(arr, batch):
    nb = arr.shape[0]
    s = arr.reshape(nb, SCAN_KH, RWKV_HEAD, 2, SCAN_PAIRS).transpose(0, 4, 2, 3, 1)
    return s.reshape(batch, RWKV_HEADS, RWKV_HEAD, RWKV_HEAD)


def _swap_halves(w):
    half = w.shape[-1] // 2
    return jnp.concatenate([w[..., half:], w[..., :half]], axis=-1)


def _prep_even(i, norm_mix, norm_ffn, w_in, q_norm, kv_norm, w_uq, w_uk, w_uv, mu, w0, w2, a0, a2,
               g2, k_k, k_a, r_k, ln_g, ln_b, w_out, ffn_gu, ffn_down):
    w = {}
    row = lambda v: v[i].reshape(1, -1)
    w_in = w_in[i]
    w["norm_mix"] = row(norm_mix)
    w["norm_ffn"] = row(norm_ffn)
    w["w_q"] = w_in[:, :MLA_Q_RANK].astype(BF16)
    w_kv = w_in[:, MLA_Q_RANK:MLA_Q_RANK + MLA_LAT]
    w["w_ckv"] = w_kv[:, :MLA_KV_RANK].astype(BF16)
    lane_pad = lambda m: jnp.pad(m, [(0, 0)] * (m.ndim - 1) + [(0, LANES - m.shape[-1])])
    w["w_pe_a"] = lane_pad(w_kv[:, MLA_KV_RANK:]).astype(BF16)
    w["w_pe_b"] = lane_pad(_swap_halves(w_kv[:, MLA_KV_RANK:])).astype(BF16)
    w["w_rw"] = w_in[:, MLA_Q_RANK + MLA_LAT:].astype(BF16)
    w["q_norm"] = row(q_norm)
    w["kv_norm"] = row(kv_norm)
    uq = w_uq[i].reshape(MLA_Q_RANK, MLA_HEADS, MLA_NOPE + MLA_ROPE)
    uq_pe = uq[:, :, MLA_NOPE:]
    w["w_qpe_a"] = lane_pad(uq_pe).reshape(MLA_Q_RANK, -1).astype(BF16)
    w["w_qpe_b"] = lane_pad(_swap_halves(uq_pe)).reshape(MLA_Q_RANK, -1).astype(BF16)
    w["w_qlat"] = _fold_qlat(uq[:, :, :MLA_NOPE].transpose(1, 0, 2), w_uk[i].transpose(1, 0, 2))
    uv = w_uv[i].transpose(1, 0, 2).reshape(MLA_HEADS // 2, 2, MLA_KV_RANK, MLA_V)
    zero = jnp.zeros_like(uv[:, 0])
    w["w_uv_bd"] = jnp.concatenate(
        [jnp.concatenate([uv[:, 0], zero], axis=-1), jnp.concatenate([zero, uv[:, 1]], axis=-1)],
        axis=1).astype(BF16)
    w["mu"] = row(mu)
    w["w0"] = row(w0)
    pad = lambda m, before: jnp.pad(m, ((before, LANES - before - m.shape[0]), (0, 0))).astype(BF16)
    w["w2p"] = pad(w2[i], 0)
    w["a2p"] = pad(a2[i], RWKV_W_LORA)
    w["a0"] = row(a0)
    w["g2"] = g2[i].astype(BF16)
    w["k_k"] = row(k_k)
    w["k_a"] = row(k_a)
    w["r_k"] = row(r_k)
    w["ln_g"] = row(ln_g)
    w["ln_b"] = row(ln_b)
    head = jnp.arange(RWKV_DIM) // RWKV_HEAD
    w["ones_bd"] = (head[:, None] == head[None, :]).astype(BF16)
    w["w_out_a"] = w_out[i][:MLA_HEADS * MLA_V].astype(BF16)
    w["w_out_b"] = w_out[i][MLA_HEADS * MLA_V:].astype(BF16)
    w["ffn_gu"] = ffn_gu[i].astype(BF16)
    w["ffn_down"] = ffn_down[i].astype(BF16)
    return w


def _prep_odd(i, norm_mix, norm_ffn, w_in, a2, ab, gla_norm, w_out, router, moe_gu, moe_down):
    w = {}
    row = lambda v: v[i].reshape(1, -1)
    w_in = w_in[i]
    w["norm_mix"] = row(norm_mix)
    w["norm_ffn"] = row(norm_ffn)
    w["w_q"] = w_in[:, :GLA_KDIM].astype(BF16)
    w["w_k"] = w_in[:, GLA_KDIM:2 * GLA_KDIM].astype(BF16)
    w["w_v"] = w_in[:, 2 * GLA_KDIM:2 * GLA_KDIM + GLA_VDIM].astype(BF16)
    w["w_g"] = w_in[:, 2 * GLA_KDIM + GLA_VDIM:2 * GLA_KDIM + 2 * GLA_VDIM].astype(BF16)
    w["w_xa"] = jnp.pad(w_in[:, 2 * GLA_KDIM + 2 * GLA_VDIM:],
                        ((0, 0), (0, LANES - GLA_GATE_RANK))).astype(BF16)
    w["a2p"] = jnp.pad(a2[i], ((0, LANES - GLA_GATE_RANK), (0, 0))).astype(BF16)
    w["ab"] = row(ab)
    w["gla_norm"] = row(gla_norm)
    w["w_out"] = w_out[i].astype(BF16)
    w["router"] = jnp.pad(router[i], ((0, 0), (0, LANES - N_EXPERTS)))
    w["layer"] = i
    w["moe_gu"] = moe_gu
    w["moe_down"] = moe_down
    return w


def _rope_tables(pos, reps):
    inv = ROPE_THETA ** (-jnp.arange(0, MLA_ROPE, 2, dtype=F32) / MLA_ROPE)
    ang = pos.astype(F32)[:, None] * inv[None, :]
    cos, sin = jnp.cos(ang), jnp.sin(ang)
    pad = ((0, 0), (0, LANES - MLA_ROPE))
    cs = jnp.tile(jnp.pad(jnp.concatenate([cos, cos], axis=-1), pad), (reps, 1))
    sn = jnp.tile(jnp.pad(jnp.concatenate([-sin, sin], axis=-1), pad), (reps, 1))
    return {"cs": cs, "sn": sn, "cs8": jnp.tile(cs, (1, MLA_HEADS)), "sn8": jnp.tile(sn, (1, MLA_HEADS))}


def _even_layer(x, batch, seq, tabs, state, shift0, past, w, tm, tc):
    n = batch * seq
    lat, lat_b, q_lat, q_pe, rw = _even_in(x, w, tabs, tm)
    if past is None:
        o_lat = _mla_prompt(q_lat, q_pe, lat_b, batch, seq)
    else:
        cache, layer, page_table = past
        rows = seq * MLA_HEADS
        q_full = jnp.concatenate([q_lat.reshape(batch, rows, MLA_KV_RANK),
                                  q_pe.reshape(batch, rows, LANES)[:, :, :MLA_ROPE]], axis=-1)
        new_pad_t = jnp.pad(lat_b.reshape(batch, seq, MLA_LATB)[:, :, :MLA_LAT],
                            ((0, 0), (0, PAGE_SIZE - seq), (0, 0))).transpose(0, 2, 1)
        o_lat = _mla_decode(page_table, q_full, new_pad_t, cache.transpose(0, 1, 3, 2), layer)
        o_lat = o_lat.reshape(n, MLA_HEADS * MLA_KV_RANK)

    rw3 = rw.reshape(batch, seq, RWKV_PROJ)
    xs5, v, g, rkv = _rwkv_prep(rw, shift0, w, tm, seq)
    y_l, s_l = _rwkv_scan(_scan_vec_layout(xs5, batch, seq), _scan_val_layout(v, batch, seq),
                          _scan_state_layout(state, batch), o_lat, tc)
    y = _scan_val_unlayout(y_l, batch, seq)
    new_state = _scan_state_unlayout(s_l, batch)

    x = _even_out(y, rkv, g, o_lat, x, w, tm)
    x = _ffn(x, w["norm_ffn"], w["ffn_gu"], w["ffn_down"], tm)
    return x, lat.reshape(batch, seq, MLA_LAT), new_state, rw3[:, -1]


def _odd_layer(x, batch, seq, state, w, final_norm, tm, tm_moe):
    q, k, v, gate, la = _odd_in(x, w, tm)
    seq_p = -(-seq // GLA_CHUNK) * GLA_CHUNK
    if seq_p != seq:
        padr = lambda t: jnp.pad(t.reshape(batch, seq, -1), ((0, 0), (0, seq_p - seq), (0, 0))
                                 ).reshape(batch * seq_p, -1)
        qp, kp, vp, lap = padr(q), padr(k), padr(v), padr(la)
    else:
        qp, kp, vp, lap = q, k, v, la
    o, st = _gla(qp, kp, vp, lap, state.transpose(0, 1, 3, 2), batch, seq_p)
    if seq_p != seq:
        o = o.reshape(batch, seq_p, GLA_VDIM)[:, :seq].reshape(batch * seq, GLA_VDIM)
    x = _odd_out(o, gate, x, w, tm)
    xn2, idx, gates = _router(x, w["norm_ffn"], w["router"], tm)
    local, wstart, src, tile_expert, tile_valid = _route(idx[:, :2], tm_moe, MOE_TMC)
    xs = _moe_gather(src, xn2, tm_moe)
    h = _moe_up(tile_expert, tile_valid, xs, w["moe_gu"], w["layer"], tm_moe)
    ys = _moe_down(tile_expert, tile_valid, h, w["moe_down"], w["layer"], tm_moe)
    g1b = jnp.broadcast_to(gates[:, 0:1], (x.shape[0], LANES))
    g2b = jnp.broadcast_to(gates[:, 1:2], (x.shape[0], LANES))
    y = _moe_combine(local, wstart, ys, g1b, g2b, x, final_norm, MOE_TMC)
    return y, st.transpose(0, 1, 3, 2)


def kernel(x_prompt, x_sample, cache_mla, state_rwkv, state_rwkv_shift, state_gla, page_table, norm_mix_even, norm_ffn_even, w_in_even, mla_q_norm, mla_kv_norm, mla_w_uq, mla_w_uk, mla_w_uv, rwkv_mu, rwkv_w0, rwkv_w2, rwkv_a0, rwkv_a2, rwkv_g2, rwkv_k_k, rwkv_k_a, rwkv_r_k, rwkv_ln_g, rwkv_ln_b, w_out_even, ffn_w_gu_even, ffn_w_down_even, norm_mix_odd, norm_ffn_odd, w_in_odd, gla_a2, gla_ab, gla_norm, w_out_odd, moe_router, moe_w_gu, moe_w_down, final_norm):
    bp, tp, _ = x_prompt.shape
    bs, ts, _ = x_sample.shape
    past_len = page_table.shape[1] * PAGE_SIZE
    tm_p, tm_s = 512, bs * ts
    we = _prep_even(0, norm_mix_even, norm_ffn_even, w_in_even, mla_q_norm, mla_kv_norm, mla_w_uq,
                    mla_w_uk, mla_w_uv, rwkv_mu, rwkv_w0, rwkv_w2, rwkv_a0, rwkv_a2, rwkv_g2,
                    rwkv_k_k, rwkv_k_a, rwkv_r_k, rwkv_ln_g, rwkv_ln_b, w_out_even, ffn_w_gu_even,
                    ffn_w_down_even)
    wo = _prep_odd(0, norm_mix_odd, norm_ffn_odd, w_in_odd, gla_a2, gla_ab, gla_norm, w_out_odd,
                   moe_router, moe_w_gu, moe_w_down)
    fn = final_norm.reshape(1, -1)
    tabs_p = _rope_tables(jnp.arange(tp), 1)
    tabs_s = _rope_tables(past_len + jnp.arange(ts), bs)

    hp = x_prompt.reshape(bp * tp, D_MODEL)
    hs = x_sample.reshape(bs * ts, D_MODEL)
    zeros_state = jnp.zeros((bp, RWKV_HEADS, RWKV_HEAD, RWKV_HEAD), F32)
    zeros_shift = jnp.zeros((bp, RWKV_PROJ), F32)
    hp, lat_p, rs_p, sh_p = _even_layer(hp, bp, tp, tabs_p, zeros_state, zeros_shift, None, we,
                                        tm_p, 64)
    hs, lat_s, rs_s, sh_s = _even_layer(hs, bs, ts, tabs_s, state_rwkv[0], state_rwkv_shift[0],
                                        (cache_mla, 0, page_table), we, tm_s, ts)
    zeros_gla = jnp.zeros((bp, GLA_HEADS, GLA_DK, GLA_DV), F32)
    yp, gs_p = _odd_layer(hp, bp, tp, zeros_gla, wo, fn, tm_p, 512)
    ys, gs_s = _odd_layer(hs, bs, ts, state_gla[0], wo, fn, tm_s, 128)
    return (yp.reshape(bp, tp, D_MODEL), ys.reshape(bs, ts, D_MODEL), lat_p[None], lat_s[None],
            rs_p[None], rs_s[None], sh_p[None], sh_s[None], gs_p[None], gs_s[None])
```

```python
import functools

import jax
import jax.numpy as jnp
from jax import lax
from jax.experimental import pallas as pl
from jax.experimental.pallas import tpu as pltpu

F32 = jnp.float32
BF16 = jnp.bfloat16

D_MODEL = 1024
PAGE_SIZE = 128
NORM_EPS = 1e-6

MLA_HEADS = 8
MLA_NOPE = 64
MLA_ROPE = 32
MLA_V = 64
MLA_Q_RANK = 384
MLA_KV_RANK = 256
MLA_LAT = MLA_KV_RANK + MLA_ROPE
MLA_LATB = MLA_KV_RANK + 128
MLA_SCALE = (MLA_NOPE + MLA_ROPE) ** -0.5
ROPE_THETA = 10000.0

RWKV_HEADS = 8
RWKV_HEAD = 64
RWKV_DIM = RWKV_HEADS * RWKV_HEAD
RWKV_W_LORA = 64
RWKV_A_LORA = 64
RWKV_G_LORA = 128
RWKV_PROJ = 3 * RWKV_DIM + RWKV_W_LORA + RWKV_A_LORA + RWKV_G_LORA
RWKV_LN_EPS = 64e-5

GLA_HEADS = 4
GLA_DK = 128
GLA_DV = 256
GLA_KDIM = GLA_HEADS * GLA_DK
GLA_VDIM = GLA_HEADS * GLA_DV
GLA_GATE_RANK = 16
GLA_GATE_NORM = 16.0
GLA_CHUNK = 128

D_FF = 2816
N_EXPERTS = 8
D_FF_EXPERT = 3584

LANES = 128
VMEM_LIMIT = 56 * 1024 * 1024
NEG_BIG = -1e30
LOG2_E = 1.4426950408889634
Q_PRESCALE = MLA_SCALE * LOG2_E


def _cparams(sem):
    return pltpu.CompilerParams(dimension_semantics=sem, vmem_limit_bytes=VMEM_LIMIT)


def _const_spec(shape):
    nd = len(shape)
    return pl.BlockSpec(shape, lambda *_: (0,) * nd)


def _row_spec(tm, width):
    return pl.BlockSpec((tm, width), lambda i: (i, 0))


def _dot(a, b):
    return jnp.dot(a.astype(BF16), b.astype(BF16), preferred_element_type=F32)


def _dot_nt(a, b):
    return lax.dot_general(a.astype(BF16), b.astype(BF16), (((1,), (1,)), ((), ())),
                           preferred_element_type=F32)


def _split2(x):
    hi = x.astype(BF16)
    lo = (x - hi.astype(F32)).astype(BF16)
    return hi, lo


def _split3(x):
    hi = x.astype(BF16)
    r1 = x - hi.astype(F32)
    mid = r1.astype(BF16)
    lo = (r1 - mid.astype(F32)).astype(BF16)
    return hi, mid, lo


def _dot_exact_rhs(x, e):
    hi, mid, lo = _split3(x)
    return (jnp.dot(hi, e, preferred_element_type=F32) + jnp.dot(mid, e, preferred_element_type=F32)
            + jnp.dot(lo, e, preferred_element_type=F32))


def _dot_exact_lhs(e, x):
    hi, mid, lo = _split3(x)
    return (jnp.dot(e, hi, preferred_element_type=F32) + jnp.dot(e, mid, preferred_element_type=F32)
            + jnp.dot(e, lo, preferred_element_type=F32))


def _dot_f32ish(a, b):
    ah, al = _split2(a)
    bh, bl = _split2(b)
    return (jnp.dot(ah, bh, preferred_element_type=F32) + jnp.dot(ah, bl, preferred_element_type=F32)
            + jnp.dot(al, bh, preferred_element_type=F32))


def _lane_tile(x, width):
    return x if width == LANES else jnp.concatenate([x] * (width // LANES), axis=1)


def _rms(x, g, eps=NORM_EPS):
    return x * lax.rsqrt(jnp.mean(x * x, axis=-1, keepdims=True) + eps) * g


def _sigmoid(x):
    return 1.0 / (1.0 + jnp.exp(-x))


def _softplus(x):
    return jnp.maximum(x, 0.0) + jnp.log(1.0 + jnp.exp(-jnp.abs(x)))


def _fold_qlat_kernel(uq_ref, uk_ref, o_ref):
    a = uq_ref[...]
    b = uk_ref[...]
    ah, al = _split2(a)
    bh, bl = _split2(b)
    dn = (((1,), (1,)), ((), ()))
    o = (lax.dot_general(ah, bh, dn, preferred_element_type=F32)
         + lax.dot_general(ah, bl, dn, preferred_element_type=F32)
         + lax.dot_general(al, bh, dn, preferred_element_type=F32))
    o_ref[...] = o.astype(BF16)


def _fold_qlat(uq_nope, uk):
    return pl.pallas_call(
        _fold_qlat_kernel,
        grid=(MLA_HEADS,),
        in_specs=[pl.BlockSpec((None, MLA_Q_RANK, MLA_NOPE), lambda h: (h, 0, 0)),
                  pl.BlockSpec((None, MLA_KV_RANK, MLA_NOPE), lambda h: (h, 0, 0))],
        out_specs=pl.BlockSpec((MLA_Q_RANK, MLA_KV_RANK), lambda h: (0, h)),
        out_shape=jax.ShapeDtypeStruct((MLA_Q_RANK, MLA_HEADS * MLA_KV_RANK), BF16),
        compiler_params=_cparams(("arbitrary",)),
    )(uq_nope, uk)


def _even_in_kernel(x_ref, g_ref, wq_ref, wckv_ref, wpa_ref, wpb_ref, wrw_ref, qn_ref, kvn_ref,
                    cs_ref, sn_ref, wql_ref, wqa_ref, wqb_ref, cs8_ref, sn8_ref,
                    lat_ref, latb_ref, ql_ref, qpe_ref, rw_ref):
    xn = _rms(x_ref[...], g_ref[...]).astype(BF16)
    cq = _rms(_dot(xn, wq_ref[...]), qn_ref[...]).astype(BF16)
    ql_ref[...] = (_dot(cq, wql_ref[...]) * Q_PRESCALE).astype(BF16)
    qpe = _dot(cq, wqa_ref[...]) * cs8_ref[...] + _dot(cq, wqb_ref[...]) * sn8_ref[...]
    qpe_ref[...] = (qpe * Q_PRESCALE).astype(BF16)
    ckv = _rms(_dot(xn, wckv_ref[...]), kvn_ref[...])
    kpe = _dot(xn, wpa_ref[...]) * cs_ref[...] + _dot(xn, wpb_ref[...]) * sn_ref[...]
    lat_ref[:, :MLA_KV_RANK] = ckv
    lat_ref[:, MLA_KV_RANK:] = kpe[:, :MLA_ROPE]
    latb_ref[:, :MLA_KV_RANK] = ckv.astype(BF16)
    latb_ref[:, MLA_KV_RANK:] = kpe.astype(BF16)
    rw_ref[...] = _dot(xn, wrw_ref[...])


def _even_in(x, w, tabs, tm):
    n = x.shape[0]
    nt = tabs["cs"].shape[0] // tm
    tab = lambda width: pl.BlockSpec((tm, width), lambda i: (i % nt, 0))
    hq = MLA_HEADS * MLA_KV_RANK
    hr = MLA_HEADS * LANES
    return pl.pallas_call(
        _even_in_kernel,
        grid=(n // tm,),
        in_specs=[_row_spec(tm, D_MODEL), _const_spec((1, D_MODEL)),
                  _const_spec((D_MODEL, MLA_Q_RANK)), _const_spec((D_MODEL, MLA_KV_RANK)),
                  _const_spec((D_MODEL, LANES)), _const_spec((D_MODEL, LANES)),
                  _const_spec((D_MODEL, RWKV_PROJ)), _const_spec((1, MLA_Q_RANK)),
                  _const_spec((1, MLA_KV_RANK)), tab(LANES), tab(LANES),
                  _const_spec((MLA_Q_RANK, hq)), _const_spec((MLA_Q_RANK, hr)),
                  _const_spec((MLA_Q_RANK, hr)), tab(hr), tab(hr)],
        out_specs=[_row_spec(tm, MLA_LAT), _row_spec(tm, MLA_LATB), _row_spec(tm, hq),
                   _row_spec(tm, hr), _row_spec(tm, RWKV_PROJ)],
        out_shape=[jax.ShapeDtypeStruct((n, MLA_LAT), F32), jax.ShapeDtypeStruct((n, MLA_LATB), BF16),
                   jax.ShapeDtypeStruct((n, hq), BF16), jax.ShapeDtypeStruct((n, hr), BF16),
                   jax.ShapeDtypeStruct((n, RWKV_PROJ), F32)],
        compiler_params=_cparams(("parallel",)),
    )(x, w["norm_mix"], w["w_q"], w["w_ckv"], w["w_pe_a"], w["w_pe_b"], w["w_rw"], w["q_norm"],
      w["kv_norm"], tabs["cs"], tabs["sn"], w["w_qlat"], w["w_qpe_a"], w["w_qpe_b"],
      tabs["cs8"], tabs["sn8"])


ATT_TQ = 256


def _mla_prompt_kernel(qi_ref, kj_ref, ql_ref, qpe_ref, lat_ref, o_ref,
                       m_sc, l_sc, a_sc, acc_sc, s_sc, p_sc):
    step = pl.program_id(1)
    i = qi_ref[step]
    j = kj_ref[step]
    heads = range(MLA_HEADS)

    @pl.when(j == 0)
    def _():
        m_sc[...] = jnp.full(m_sc.shape, NEG_BIG, F32)
        l_sc[...] = jnp.zeros(l_sc.shape, F32)
        acc_sc[...] = jnp.zeros(acc_sc.shape, F32)

    def tile(masked):
        ckv = lat_ref[:, :MLA_KV_RANK]
        kpe = lat_ref[:, MLA_KV_RANK:]
        for h in heads:
            s_sc[h] = (_dot_nt(ql_ref[:, h * MLA_KV_RANK:(h + 1) * MLA_KV_RANK], ckv)
                       + _dot_nt(qpe_ref[:, h * LANES:(h + 1) * LANES], kpe))
        for h in heads:
            s = s_sc[h]
            if masked:
                tok = lax.broadcasted_iota(jnp.int32, s.shape, 0)
                key = lax.broadcasted_iota(jnp.int32, s.shape, 1)
                s = jnp.where(key <= tok, s, NEG_BIG)
            m_prev = m_sc[h]
            m_new = jnp.maximum(m_prev, jnp.max(s, axis=-1, keepdims=True))
            alpha = jnp.exp2(m_prev - m_new)
            p = jnp.exp2(s - _lane_tile(m_new, ATT_TQ))
            l_sc[h] = alpha * l_sc[h] + jnp.sum(p, axis=-1, keepdims=True)
            m_sc[h] = m_new
            a_sc[h] = alpha
            p_sc[h] = p.astype(BF16)
        for h in heads:
            acc_sc[h] = _lane_tile(a_sc[h], MLA_KV_RANK) * acc_sc[h] + _dot(p_sc[h], ckv)

    @pl.when(j < i)
    def _():
        tile(False)

    @pl.when(j == i)
    def _():
        tile(True)
        for h in heads:
            o_ref[:, h * MLA_KV_RANK:(h + 1) * MLA_KV_RANK] = (
                acc_sc[h] / _lane_tile(l_sc[h], MLA_KV_RANK)).astype(BF16)


def _mla_prompt(q_lat, q_pe, lat_b, batch, seq):
    nq = seq // ATT_TQ
    pairs = [(i, j) for i in range(nq) for j in range(i + 1)]
    qi = jnp.array([p[0] for p in pairs], jnp.int32)
    kj = jnp.array([p[1] for p in pairs], jnp.int32)
    hq = MLA_HEADS * MLA_KV_RANK
    grid_spec = pltpu.PrefetchScalarGridSpec(
        num_scalar_prefetch=2,
        grid=(batch, len(pairs)),
        in_specs=[pl.BlockSpec((ATT_TQ, hq), lambda b, s, qi, kj: (b * nq + qi[s], 0)),
                  pl.BlockSpec((ATT_TQ, MLA_HEADS * LANES), lambda b, s, qi, kj: (b * nq + qi[s], 0)),
                  pl.BlockSpec((ATT_TQ, MLA_LATB), lambda b, s, qi, kj: (b * nq + kj[s], 0))],
        out_specs=pl.BlockSpec((ATT_TQ, hq), lambda b, s, qi, kj: (b * nq + qi[s], 0)),
        scratch_shapes=[pltpu.VMEM((MLA_HEADS, ATT_TQ, LANES), F32),
                        pltpu.VMEM((MLA_HEADS, ATT_TQ, LANES), F32),
                        pltpu.VMEM((MLA_HEADS, ATT_TQ, LANES), F32),
                        pltpu.VMEM((MLA_HEADS, ATT_TQ, MLA_KV_RANK), F32),
                        pltpu.VMEM((MLA_HEADS, ATT_TQ, ATT_TQ), F32),
                        pltpu.VMEM((MLA_HEADS, ATT_TQ, ATT_TQ), BF16)],
    )
    return pl.pallas_call(
        _mla_prompt_kernel,
        grid_spec=grid_spec,
        out_shape=jax.ShapeDtypeStruct(q_lat.shape, BF16),
        compiler_params=_cparams(("parallel", "arbitrary")),
    )(qi, kj, q_lat, q_pe, lat_b)


PAGES_PER_STEP = 16
DECODE_GROUPS = 4


def _mla_decode_kernel(pt_ref, q_ref, new_ref, *rest):
    page_refs = rest[:PAGES_PER_STEP]
    o_ref, m_sc, l_sc, acc_sc = rest[PAGES_PER_STEP:]
    j = pl.program_id(1)
    q = q_ref[0]

    @pl.when(j == 0)
    def _():
        m_sc[...] = jnp.full(m_sc.shape, NEG_BIG, F32)
        l_sc[...] = jnp.zeros(l_sc.shape, F32)
        acc_sc[...] = jnp.zeros(acc_sc.shape, F32)

    def update(state, s, values_t):
        m_prev, l_prev, acc = state
        m_new = jnp.maximum(m_prev, jnp.max(s, axis=-1, keepdims=True))
        alpha = jnp.exp2(m_prev - m_new)
        p = jnp.exp2(s - _lane_tile(m_new, s.shape[1]))
        l_new = alpha * l_prev + jnp.sum(p, axis=-1, keepdims=True)
        return m_new, l_new, _lane_tile(alpha, MLA_KV_RANK) * acc + _dot_nt(p, values_t)

    group = PAGES_PER_STEP // DECODE_GROUPS
    keys = [jnp.concatenate([pr[...].astype(BF16) for pr in page_refs[g * group:(g + 1) * group]],
                            axis=1) for g in range(DECODE_GROUPS)]
    scores = [_dot(q, kt) for kt in keys]
    state = (m_sc[...], l_sc[...], acc_sc[...])
    for s, kt in zip(scores, keys):
        state = update(state, s, kt[:MLA_KV_RANK, :])
    m_sc[...], l_sc[...], acc_sc[...] = state

    @pl.when(j == pl.num_programs(1) - 1)
    def _():
        new_t = new_ref[0]
        sn = _dot(q, new_t)
        tok = lax.broadcasted_iota(jnp.int32, sn.shape, 0) >> 3
        key = lax.broadcasted_iota(jnp.int32, sn.shape, 1)
        sn = jnp.where(key <= tok, sn, NEG_BIG)
        _, l_fin, acc_fin = update(state, sn, new_t[:MLA_KV_RANK, :])
        o_ref[0] = (acc_fin / _lane_tile(l_fin, MLA_KV_RANK)).astype(BF16)


def _mla_decode(page_table, q_full, new_pad_t, cache_t, layer):
    db, n_pages = page_table.shape
    rows = q_full.shape[1]
    steps = n_pages // PAGES_PER_STEP

    def page_spec(p):
        return pl.BlockSpec((None, None, MLA_LAT, PAGE_SIZE),
                            lambda b, j, pt: (layer, pt[b, j * PAGES_PER_STEP + p], 0, 0))

    grid_spec = pltpu.PrefetchScalarGridSpec(
        num_scalar_prefetch=1,
        grid=(db, steps),
        in_specs=[pl.BlockSpec((1, rows, MLA_LAT), lambda b, j, pt: (b, 0, 0)),
                  pl.BlockSpec((1, MLA_LAT, PAGE_SIZE), lambda b, j, pt: (b, 0, 0))]
        + [page_spec(p) for p in range(PAGES_PER_STEP)],
        out_specs=pl.BlockSpec((1, rows, MLA_KV_RANK), lambda b, j, pt: (b, 0, 0)),
        scratch_shapes=[pltpu.VMEM((rows, LANES), F32), pltpu.VMEM((rows, LANES), F32),
                        pltpu.VMEM((rows, MLA_KV_RANK), F32)],
    )
    return pl.pallas_call(
        _mla_decode_kernel,
        grid_spec=grid_spec,
        out_shape=jax.ShapeDtypeStruct((db, rows, MLA_KV_RANK), BF16),
        compiler_params=_cparams(("parallel", "arbitrary")),
    )(page_table, q_full, new_pad_t, *([cache_t] * PAGES_PER_STEP))


def _rwkv_prep_kernel(rw_ref, before_ref, sh_ref, mu_ref, w0_ref, w2_ref, a0_ref, a2_ref, g2_ref,
                      kk_ref, ka_ref, rk_ref, ones_ref, xs_ref, v_ref, g_ref, rkv_ref, *, tm, seq):
    rw = rw_ref[...]
    rolled = pltpu.roll(rw, 1, axis=0)
    row = lax.broadcasted_iota(jnp.int32, rw.shape, 0)
    if seq >= tm:
        at_start = pl.program_id(0) % (seq // tm) == 0
        first = jnp.where(at_start, sh_ref[...], before_ref[7:8, :])
        prev = jnp.where(row == 0, first, rolled)
    else:
        prev = jnp.where((row & (seq - 1)) == 0, sh_ref[...], rolled)
    xs = rw + (prev - rw) * mu_ref[...]
    d = RWKV_DIM
    r = xs[:, :d]
    k = xs[:, d:2 * d]
    v = xs[:, 2 * d:3 * d]
    xwa = xs[:, 3 * d:3 * d + LANES]
    xg = xs[:, 3 * d + LANES:]
    ones = ones_ref[...]
    w_log = -_softplus(-(w0_ref[...] + _dot(jnp.tanh(xwa), w2_ref[...]))) - 0.5
    a = _sigmoid(a0_ref[...] + _dot(xwa, a2_ref[...]))
    g_ref[...] = _dot(_sigmoid(xg), g2_ref[...])
    kk = k * kk_ref[...]
    ss = _dot_exact_rhs(kk * kk, ones)
    kk = kk / jnp.maximum(jnp.sqrt(ss), 1e-12)
    k2 = k * (1.0 + (a - 1.0) * ka_ref[...])
    xs_ref[0] = -kk
    xs_ref[1] = jnp.exp(-jnp.exp(w_log))
    xs_ref[2] = kk * a
    xs_ref[3] = k2
    xs_ref[4] = r
    v_ref[...] = v
    rkv_ref[...] = _dot_exact_rhs(r * k2 * rk_ref[...], ones) * v


def _rwkv_prep(rw, shift0, w, tm, seq):
    n = rw.shape[0]
    d = RWKV_DIM
    vec = _const_spec((1, d))
    if seq >= tm:
        tiles = seq // tm
        sh = shift0.reshape(-1, 1, RWKV_PROJ)
        sh_spec = pl.BlockSpec((None, 1, RWKV_PROJ), lambda i: (i // tiles, 0, 0))
    else:
        sh = jnp.repeat(shift0, seq, axis=0)
        sh_spec = _row_spec(tm, RWKV_PROJ)
    before_spec = pl.BlockSpec((8, RWKV_PROJ), lambda i: (jnp.maximum(i * (tm // 8) - 1, 0), 0))
    return pl.pallas_call(
        functools.partial(_rwkv_prep_kernel, tm=tm, seq=seq),
        grid=(n // tm,),
        in_specs=[_row_spec(tm, RWKV_PROJ), before_spec, sh_spec, _const_spec((1, RWKV_PROJ)),
                  vec, _const_spec((LANES, d)), vec, _const_spec((LANES, d)),
                  _const_spec((RWKV_G_LORA, d)), vec, vec, vec, _const_spec((d, d))],
        out_specs=[pl.BlockSpec((5, tm, d), lambda i: (0, i, 0))] + [_row_spec(tm, d)] * 3,
        out_shape=[jax.ShapeDtypeStruct((5, n, d), F32)] + [jax.ShapeDtypeStruct((n, d), F32)] * 3,
        compiler_params=_cparams(("parallel",)),
    )(rw, rw, sh, w["mu"], w["w0"], w["w2p"], w["a0"], w["a2p"], w["g2"], w["k_k"], w["k_a"],
      w["r_k"], w["ones_bd"])


SCAN_KH = RWKV_HEAD // 2
SCAN_PAIRS = LANES // 2
SCAN_VR = RWKV_HEAD // 2


def _rwkv_scan_kernel(x_ref, v_ref, s0_ref, after_ref, y_ref, s_ref, c_sc, d_sc, *, tc):
    del after_ref
    @pl.when(pl.program_id(1) == 0)
    def _():
        s_ref[...] = s0_ref[...]

    half_a = slice(0, SCAN_VR)
    half_b = slice(SCAN_VR, RWKV_HEAD)

    def both_halves(p):
        return p + pltpu.roll(p, SCAN_PAIRS, axis=1)

    def key_dot(u, w):
        return both_halves(jnp.sum(u * w, axis=0, keepdims=True))

    def first_partial(rows):
        p = s_ref[0, 0, rows, :] * x_ref[0, 0, 0, 0:1, :]
        for k in range(1, SCAN_KH):
            p = p + s_ref[0, k, rows, :] * x_ref[0, 0, 0, k:k + 1, :]
        return p

    def half_step(t, rows, sa):
        v_half = v_ref[0, t, rows, :]
        v = jnp.concatenate([v_half, v_half], axis=1)
        q = None
        y = None
        for k in range(SCAN_KH):
            s_old = s_ref[0, k, rows, :]
            qk = s_old * c_sc[k:k + 1, :]
            sn = (s_old * x_ref[1, 0, t, k:k + 1, :] + sa * x_ref[2, 0, t, k:k + 1, :]
                  + v * x_ref[3, 0, t, k:k + 1, :])
            s_ref[0, k, rows, :] = sn
            yk = sn * x_ref[4, 0, t, k:k + 1, :]
            q = qk if q is None else q + qk
            y = yk if y is None else y + yk
        return q, sa * d_sc[0:1, :] + v * d_sc[1:2, :], y

    def store_y(t, y_a, y_b):
        y_ref[0, t, half_a, :] = both_halves(y_a)[:, :SCAN_PAIRS]
        y_ref[0, t, half_b, :] = both_halves(y_b)[:, :SCAN_PAIRS]

    def step(t, carry):
        sa_a, q_b, corr_b, y_a, y_b = carry
        store_y(jnp.maximum(t - 1, 0), y_a, y_b)
        a_next = x_ref[0, 0, jnp.minimum(t + 1, tc - 1)]
        c_sc[...] = x_ref[1, 0, t] * a_next
        d_sc[0:1, :] = key_dot(x_ref[2, 0, t], a_next)
        d_sc[1:2, :] = key_dot(x_ref[3, 0, t], a_next)
        sa_b = both_halves(q_b) + corr_b
        q_a, corr_a, y_a_new = half_step(t, half_a, sa_a)
        sa_a_next = both_halves(q_a) + corr_a
        q_b_next, corr_b_next, y_b_new = half_step(t, half_b, sa_b)
        return sa_a_next, q_b_next, corr_b_next, y_a_new, y_b_new

    zero = jnp.zeros((SCAN_VR, LANES), F32)
    init = (both_halves(first_partial(half_a)), first_partial(half_b), zero, zero, zero)
    final = lax.fori_loop(0, tc, step, init)
    store_y(tc - 1, final[3], final[4])


def _rwkv_scan(xs, v, s0, after, tc):
    _, nb, t, _, _ = xs.shape
    xspec = pl.BlockSpec((5, 1, tc, SCAN_KH, LANES), lambda n, c: (0, n, c, 0, 0))
    vspec = pl.BlockSpec((1, tc, RWKV_HEAD, SCAN_PAIRS), lambda n, c: (n, c, 0, 0))
    sspec = pl.BlockSpec((1, SCAN_KH, RWKV_HEAD, LANES), lambda n, c: (n, 0, 0, 0))
    return pl.pallas_call(
        functools.partial(_rwkv_scan_kernel, tc=tc),
        grid=(nb, t // tc),
        in_specs=[xspec, vspec, sspec, pl.BlockSpec(memory_space=pl.ANY)],
        out_specs=[vspec, sspec],
        out_shape=[jax.ShapeDtypeStruct(v.shape, F32), jax.ShapeDtypeStruct(s0.shape, F32)],
        scratch_shapes=[pltpu.VMEM((SCAN_KH, LANES), F32), pltpu.VMEM((8, LANES), F32)],
        compiler_params=_cparams(("parallel", "arbitrary")),
    )(xs, v, s0, after)


def _even_out_kernel(y_ref, rkv_ref, g_ref, lng_ref, lnb_ref, ones_ref, ol_ref, wuv_ref, woa_ref,
                     wob_ref, x_ref, o_ref):
    ones = ones_ref[...]
    y = y_ref[...]
    inv = 1.0 / RWKV_HEAD
    mean = _dot_exact_rhs(y, ones) * inv
    dlt = y - mean
    var = _dot_exact_rhs(dlt * dlt, ones) * inv
    yn = dlt * lax.rsqrt(var + RWKV_LN_EPS) * lng_ref[...] + lnb_ref[...] + rkv_ref[...]
    ob = (yn * g_ref[...]).astype(BF16)
    pair = 2 * MLA_KV_RANK
    oa = jnp.concatenate(
        [_dot(ol_ref[:, p * pair:(p + 1) * pair], wuv_ref[p]) for p in range(MLA_HEADS // 2)], axis=1)
    o_ref[...] = x_ref[...] + _dot(oa, woa_ref[...]) + _dot(ob, wob_ref[...])


def _even_out(y, rkv, g, o_lat, x, w, tm):
    n = x.shape[0]
    d = RWKV_DIM
    hq = MLA_HEADS * MLA_KV_RANK
    return pl.pallas_call(
        _even_out_kernel,
        grid=(n // tm,),
        in_specs=[_row_spec(tm, d), _row_spec(tm, d), _row_spec(tm, d), _const_spec((1, d)),
                  _const_spec((1, d)), _const_spec((d, d)), _row_spec(tm, hq),
                  _const_spec((MLA_HEADS // 2, 2 * MLA_KV_RANK, 2 * MLA_V)),
                  _const_spec((MLA_HEADS * MLA_V, D_MODEL)), _const_spec((d, D_MODEL)),
                  _row_spec(tm, D_MODEL)],
        out_specs=_row_spec(tm, D_MODEL),
        out_shape=jax.ShapeDtypeStruct((n, D_MODEL), F32),
        compiler_params=_cparams(("parallel",)),
    )(y, rkv, g, w["ln_g"], w["ln_b"], w["ones_bd"], o_lat, w["w_uv_bd"], w["w_out_a"],
      w["w_out_b"], x)


FFN_TF = 1408


def _ffn_kernel(x_ref, g_ref, wg_ref, wu_ref, wd_ref, o_ref, xn_sc, acc_sc):
    f = pl.program_id(1)

    @pl.when(f == 0)
    def _():
        xn_sc[...] = _rms(x_ref[...], g_ref[...]).astype(BF16)
        acc_sc[...] = jnp.zeros(acc_sc.shape, F32)

    xn = xn_sc[...]
    gate = _dot(xn, wg_ref[...])
    up = _dot(xn, wu_ref[...])
    acc_sc[...] += _dot(gate * _sigmoid(gate) * up, wd_ref[...])

    @pl.when(f == pl.num_programs(1) - 1)
    def _():
        o_ref[...] = x_ref[...] + acc_sc[...]


def _ffn(x, g, w_gu, w_down, tm):
    n = x.shape[0]
    nf = D_FF // FFN_TF
    return pl.pallas_call(
        _ffn_kernel,
        grid=(n // tm, nf),
        in_specs=[pl.BlockSpec((tm, D_MODEL), lambda i, f: (i, 0)),
                  pl.BlockSpec((1, D_MODEL), lambda i, f: (0, 0)),
                  pl.BlockSpec((D_MODEL, FFN_TF), lambda i, f: (0, f)),
                  pl.BlockSpec((D_MODEL, FFN_TF), lambda i, f: (0, nf + f)),
                  pl.BlockSpec((FFN_TF, D_MODEL), lambda i, f: (f, 0))],
        out_specs=pl.BlockSpec((tm, D_MODEL), lambda i, f: (i, 0)),
        out_shape=jax.ShapeDtypeStruct((n, D_MODEL), F32),
        scratch_shapes=[pltpu.VMEM((tm, D_MODEL), BF16), pltpu.VMEM((tm, D_MODEL), F32)],
        compiler_params=_cparams(("parallel", "arbitrary")),
    )(x, g, w_gu, w_gu, w_down)


def _odd_in_kernel(x_ref, g_ref, wq_ref, wk_ref, wv_ref, wg_ref, wxa_ref, a2_ref, ab_ref,
                   q_ref, k_ref, v_ref, gate_ref, la_ref):
    xn = _rms(x_ref[...], g_ref[...]).astype(BF16)
    q_ref[...] = _dot(xn, wq_ref[...]) * (GLA_DK ** -0.5)
    k_ref[...] = _dot(xn, wk_ref[...])
    v_ref[...] = _dot(xn, wv_ref[...])
    gate_ref[...] = _dot(xn, wg_ref[...])
    z = _dot(_dot(xn, wxa_ref[...]), a2_ref[...]) + ab_ref[...]
    la_ref[...] = -_softplus(-z) * (1.0 / GLA_GATE_NORM)


def _odd_in(x, w, tm):
    n = x.shape[0]
    return pl.pallas_call(
        _odd_in_kernel,
        grid=(n // tm,),
        in_specs=[_row_spec(tm, D_MODEL), _const_spec((1, D_MODEL)),
                  _const_spec((D_MODEL, GLA_KDIM)), _const_spec((D_MODEL, GLA_KDIM)),
                  _const_spec((D_MODEL, GLA_VDIM)), _const_spec((D_MODEL, GLA_VDIM)),
                  _const_spec((D_MODEL, LANES)), _const_spec((LANES, GLA_KDIM)),
                  _const_spec((1, GLA_KDIM))],
        out_specs=[_row_spec(tm, GLA_KDIM), _row_spec(tm, GLA_KDIM), _row_spec(tm, GLA_VDIM),
                   _row_spec(tm, GLA_VDIM), _row_spec(tm, GLA_KDIM)],
        out_shape=[jax.ShapeDtypeStruct((n, GLA_KDIM), F32), jax.ShapeDtypeStruct((n, GLA_KDIM), F32),
                   jax.ShapeDtypeStruct((n, GLA_VDIM), F32), jax.ShapeDtypeStruct((n, GLA_VDIM), F32),
                   jax.ShapeDtypeStruct((n, GLA_KDIM), F32)],
        compiler_params=_cparams(("parallel",)),
    )(x, w["norm_mix"], w["w_q"], w["w_k"], w["w_v"], w["w_g"], w["w_xa"], w["a2p"], w["ab"])


def _gla_kernel(q_ref, k_ref, v_ref, la_ref, s0_ref, o_ref, st_ref):
    c = GLA_CHUNK

    @pl.when(pl.program_id(1) == 0)
    def _():
        st_ref[...] = s0_ref[...]

    row = lax.broadcasted_iota(jnp.int32, (c, c), 0)
    col = lax.broadcasted_iota(jnp.int32, (c, c), 1)
    tri = row >= col
    tri_b = jnp.where(tri, 1.0, 0.0).astype(BF16)
    for h in range(GLA_HEADS):
        ks = slice(h * GLA_DK, (h + 1) * GLA_DK)
        vs = slice(h * GLA_DV, (h + 1) * GLA_DV)
        b = _dot_exact_lhs(tri_b, la_ref[:, ks])
        q = q_ref[:, ks]
        k = k_ref[:, ks]
        v = v_ref[:, vs]
        b_end = b[c - 1:c, :]
        qe = (q * jnp.exp(b)).astype(BF16)
        ke = (k * jnp.exp(-b)).astype(BF16)
        a_mat = jnp.where(tri, _dot_nt(qe, ke), 0.0)
        st = st_ref[0, h]
        o_ref[:, vs] = _dot_nt(qe, st) + _dot(a_mat, v)
        k_end = k * jnp.exp(b_end - b)
        st_ref[0, h] = st * jnp.exp(b_end) + _dot(v.T, k_end)


def _gla(q, k, v, la, s0t, batch, seq):
    nc = seq // GLA_CHUNK
    rspec = lambda width: pl.BlockSpec((GLA_CHUNK, width), lambda b, c: (b * nc + c, 0))
    sspec = pl.BlockSpec((1, GLA_HEADS, GLA_DV, GLA_DK), lambda b, c: (b, 0, 0, 0))
    return pl.pallas_call(
        _gla_kernel,
        grid=(batch, nc),
        in_specs=[rspec(GLA_KDIM), rspec(GLA_KDIM), rspec(GLA_VDIM), rspec(GLA_KDIM), sspec],
        out_specs=[rspec(GLA_VDIM), sspec],
        out_shape=[jax.ShapeDtypeStruct(v.shape, F32), jax.ShapeDtypeStruct(s0t.shape, F32)],
        compiler_params=_cparams(("parallel", "arbitrary")),
    )(q, k, v, la, s0t)


def _odd_out_kernel(o_ref, gate_ref, gn_ref, wo_ref, x_ref, y_ref):
    parts = []
    for h in range(GLA_HEADS):
        vs = slice(h * GLA_DV, (h + 1) * GLA_DV)
        parts.append(_rms(o_ref[:, vs], gn_ref[:, vs]))
    gate = gate_ref[...]
    on = jnp.concatenate(parts, axis=1) * (gate * _sigmoid(gate))
    y_ref[...] = x_ref[...] + _dot(on, wo_ref[...])


def _odd_out(o, gate, x, w, tm):
    n = x.shape[0]
    return pl.pallas_call(
        _odd_out_kernel,
        grid=(n // tm,),
        in_specs=[_row_spec(tm, GLA_VDIM), _row_spec(tm, GLA_VDIM), _const_spec((1, GLA_VDIM)),
                  _const_spec((GLA_VDIM, D_MODEL)), _row_spec(tm, D_MODEL)],
        out_specs=_row_spec(tm, D_MODEL),
        out_shape=jax.ShapeDtypeStruct((n, D_MODEL), F32),
        compiler_params=_cparams(("parallel",)),
    )(o, gate, w["gla_norm"], w["w_out"], x)


def _router_kernel(x_ref, g_ref, wr_ref, *rest):
    xn_ref, idx_ref, gate_ref = rest[-3:]
    xn = _rms(x_ref[...], g_ref[...])
    half = D_MODEL // 2
    xn_ref[0] = xn[:, :half]
    xn_ref[1] = xn[:, half:]
    logits = _dot_f32ish(xn, wr_ref[...])
    lane = lax.broadcasted_iota(jnp.int32, logits.shape, 1)
    logits = jnp.where(lane < N_EXPERTS, logits, NEG_BIG)
    m1 = jnp.max(logits, axis=-1, keepdims=True)
    i1 = jnp.min(jnp.where(logits == m1, lane, LANES), axis=-1, keepdims=True)
    rest = jnp.where(lane == i1, NEG_BIG, logits)
    m2 = jnp.max(rest, axis=-1, keepdims=True)
    i2 = jnp.min(jnp.where(rest == m2, lane, LANES), axis=-1, keepdims=True)
    e2 = jnp.exp(m2 - m1)
    g1 = 1.0 / (1.0 + e2)
    g2 = e2 / (1.0 + e2)
    idx_ref[...] = jnp.where(lane == 0, i1, jnp.where(lane == 1, i2, 0))
    gate_ref[...] = jnp.where(lane == 0, g1, jnp.where(lane == 1, g2, 0.0))


def _router(x, g, wr, tm, n_total, row0, prev=None):
    n = x.shape[0]
    half = D_MODEL // 2
    blk0 = row0 // tm
    prev = () if prev is None else tuple(prev)
    return pl.pallas_call(
        _router_kernel,
        grid=(n // tm,),
        in_specs=[_row_spec(tm, D_MODEL), _const_spec((1, D_MODEL)), _const_spec((D_MODEL, LANES))]
        + [pl.BlockSpec(memory_space=pl.ANY)] * len(prev),
        out_specs=[pl.BlockSpec((2, tm, half), lambda i: (0, blk0 + i, 0)),
                   pl.BlockSpec((tm, LANES), lambda i: (blk0 + i, 0)),
                   pl.BlockSpec((tm, LANES), lambda i: (blk0 + i, 0))],
        out_shape=[jax.ShapeDtypeStruct((2, n_total, half), F32),
                   jax.ShapeDtypeStruct((n_total, LANES), jnp.int32),
                   jax.ShapeDtypeStruct((n_total, LANES), F32)],
        input_output_aliases={3 + k: k for k in range(len(prev))},
        compiler_params=_cparams(("parallel",)),
    )(x, g, wr, *prev)


MOE_TF = 1792
MOE_TMC = 256


def _route(top_i, tm, tmc):
    n = top_i.shape[0]
    slots = 2 * n
    n_tiles = -(-(slots + N_EXPERTS * (tm - 1)) // tm)
    win = tmc + 8
    e_flat = top_i.reshape(-1)
    onehot = (e_flat[:, None] == jnp.arange(N_EXPERTS, dtype=jnp.int32)[None, :]).astype(jnp.int32)
    csum = jnp.cumsum(onehot, axis=0)
    rank = jnp.sum(onehot * csum, axis=1) - 1
    counts = csum[-1]
    padded = ((counts + tm - 1) // tm) * tm
    ends = jnp.cumsum(padded)
    starts = ends - padded
    dest = (jnp.sum(onehot * starts[None, :], axis=1) + rank).astype(jnp.int32)
    tile_start = jnp.arange(n_tiles, dtype=jnp.int32) * tm
    tile_expert = jnp.minimum(jnp.sum((tile_start[:, None] >= ends[None, :]).astype(jnp.int32), axis=1),
                              N_EXPERTS - 1).astype(jnp.int32)
    tile_valid = (tile_start < ends[-1]).astype(jnp.int32)
    src = jnp.zeros((n_tiles * tm,), jnp.int32).at[dest].set(jnp.arange(slots, dtype=jnp.int32) // 2)
    before = jnp.concatenate([jnp.zeros((1, N_EXPERTS), jnp.int32), csum[2 * tmc - 1:-1:2 * tmc]], axis=0)
    wstart = jnp.clip(((starts[None, :] + before) // 8) * 8, 0, n_tiles * tm - win).astype(jnp.int32)
    ws_slot = jnp.sum(onehot * jnp.repeat(wstart, 2 * tmc, axis=0), axis=1)
    local = (e_flat * win + dest - ws_slot).astype(jnp.int32)
    return local, wstart.reshape(-1), src, tile_expert, tile_valid


def _moe_gather_kernel(src_ref, x_ref, o_ref, *, tg):
    base = pl.program_id(1) * tg

    def body(r, carry):
        o_ref[pl.ds(r, 1), :] = x_ref[pl.ds(src_ref[base + r], 1), :]
        return carry

    lax.fori_loop(0, tg, body, 0, unroll=8)


def _moe_gather(src, xn2, tg):
    rows = src.shape[0]
    _, n, half = xn2.shape
    grid_spec = pltpu.PrefetchScalarGridSpec(
        num_scalar_prefetch=1,
        grid=(2, rows // tg),
        in_specs=[pl.BlockSpec((None, n, half), lambda h, i, s: (h, 0, 0),
                               pipeline_mode=pl.Buffered(1))],
        out_specs=pl.BlockSpec((tg, half), lambda h, i, s: (i, h)),
    )
    return pl.pallas_call(
        functools.partial(_moe_gather_kernel, tg=tg),
        grid_spec=grid_spec,
        out_shape=jax.ShapeDtypeStruct((rows, 2 * half), F32),
        compiler_params=_cparams(("arbitrary", "arbitrary")),
    )(src, xn2)


def _moe_up_kernel(te_ref, tv_ref, xs_ref, wg_ref, wu_ref, h_ref):
    @pl.when(tv_ref[pl.program_id(1)] != 0)
    def _():
        xs = xs_ref[...].astype(BF16)
        gate = _dot(xs, wg_ref[...])
        up = _dot(xs, wu_ref[...])
        h_ref[...] = (gate * _sigmoid(gate) * up).astype(BF16)


def _moe_up(te, tv, xs, w_gu, layer, tm):
    rows = xs.shape[0]
    nf = D_FF_EXPERT // MOE_TF
    wspec = lambda off: pl.BlockSpec((None, None, D_MODEL, MOE_TF),
                                     lambda f, t, te, tv: (layer, te[t], 0, off + f))
    grid_spec = pltpu.PrefetchScalarGridSpec(
        num_scalar_prefetch=2,
        grid=(nf, rows // tm),
        in_specs=[pl.BlockSpec((tm, D_MODEL), lambda f, t, te, tv: (t, 0)), wspec(0), wspec(nf)],
        out_specs=pl.BlockSpec((tm, MOE_TF), lambda f, t, te, tv: (t, f)),
    )
    return pl.pallas_call(
        _moe_up_kernel,
        grid_spec=grid_spec,
        out_shape=jax.ShapeDtypeStruct((rows, D_FF_EXPERT), BF16),
        compiler_params=_cparams(("arbitrary", "arbitrary")),
    )(te, tv, xs, w_gu, w_gu)


def _moe_down_kernel(te_ref, tv_ref, h_ref, wd_ref, y_ref):
    @pl.when(tv_ref[pl.program_id(0)] != 0)
    def _():
        y_ref[...] = _dot(h_ref[...], wd_ref[...])


def _moe_down(te, tv, h, w_down, layer, tm):
    rows = h.shape[0]
    grid_spec = pltpu.PrefetchScalarGridSpec(
        num_scalar_prefetch=2,
        grid=(rows // tm,),
        in_specs=[pl.BlockSpec((tm, D_FF_EXPERT), lambda t, te, tv: (t, 0)),
                  pl.BlockSpec((None, None, D_FF_EXPERT, D_MODEL),
                               lambda t, te, tv: (layer, te[t], 0, 0))],
        out_specs=pl.BlockSpec((tm, D_MODEL), lambda t, te, tv: (t, 0)),
    )
    return pl.pallas_call(
        _moe_down_kernel,
        grid_spec=grid_spec,
        out_shape=jax.ShapeDtypeStruct((rows, D_MODEL), F32),
        compiler_params=_cparams(("arbitrary",)),
    )(te, tv, h, w_down)


def _moe_combine_kernel(ws_ref, local_ref, *refs, tmc, win, tile0):
    win_refs = refs[:N_EXPERTS]
    g1_ref, g2_ref, x_ref, fn_ref, o_ref, buf = refs[N_EXPERTS:]
    for e in range(N_EXPERTS):
        buf[e * win:(e + 1) * win, :] = win_refs[e][...]
    base = 2 * (tile0 + pl.program_id(0)) * tmc

    def body(r, carry):
        row = pl.ds(r, 1)
        y1 = buf[pl.ds(local_ref[base + 2 * r], 1), :]
        y2 = buf[pl.ds(local_ref[base + 2 * r + 1], 1), :]
        g1 = g1_ref[row, :]
        g2 = g2_ref[row, :]
        parts = []
        for c in range(D_MODEL // LANES):
            cs = slice(c * LANES, (c + 1) * LANES)
            parts.append(g1 * y1[:, cs] + g2 * y2[:, cs])
        o_ref[row, :] = x_ref[row, :] + jnp.concatenate(parts, axis=1)
        return carry

    lax.fori_loop(0, tmc, body, 0, unroll=4)
    o_ref[...] = _rms(o_ref[...], fn_ref[...])


def _moe_combine(local, wstart, ys, g1b, g2b, x, fn, tmc, row0):
    n = x.shape[0]
    win = tmc + 8
    tile0 = row0 // tmc

    def win_spec(e):
        return pl.BlockSpec(
            (pl.Element(win), pl.Element(D_MODEL)),
            lambda i, ws, lo: (pl.multiple_of(ws[(tile0 + i) * N_EXPERTS + e], 8), 0))

    grid_spec = pltpu.PrefetchScalarGridSpec(
        num_scalar_prefetch=2,
        grid=(n // tmc,),
        in_specs=[win_spec(e) for e in range(N_EXPERTS)]
        + [pl.BlockSpec((tmc, LANES), lambda i, ws, lo: (tile0 + i, 0)),
           pl.BlockSpec((tmc, LANES), lambda i, ws, lo: (tile0 + i, 0)),
           pl.BlockSpec((tmc, D_MODEL), lambda i, ws, lo: (i, 0)),
           pl.BlockSpec((1, D_MODEL), lambda i, ws, lo: (0, 0))],
        out_specs=pl.BlockSpec((tmc, D_MODEL), lambda i, ws, lo: (i, 0)),
        scratch_shapes=[pltpu.VMEM((N_EXPERTS * win, D_MODEL), F32)],
    )
    return pl.pallas_call(
        functools.partial(_moe_combine_kernel, tmc=tmc, win=win, tile0=tile0),
        grid_spec=grid_spec,
        out_shape=jax.ShapeDtypeStruct((n, D_MODEL), F32),
        compiler_params=_cparams(("arbitrary",)),
    )(wstart, local, *([ys] * N_EXPERTS), g1b, g2b, x, fn)


def _scan_vec_layout(xs, batch, seq):
    nb = batch * RWKV_HEADS // SCAN_PAIRS
    x = xs.reshape(5, batch, seq, RWKV_HEADS, 2, SCAN_KH).transpose(0, 2, 5, 4, 1, 3)
    x = x.reshape(5, seq, SCAN_KH, 2, nb, SCAN_PAIRS).transpose(0, 4, 1, 2, 3, 5)
    return x.reshape(5, nb, seq, SCAN_KH, LANES)


def _scan_val_layout(v, batch, seq):
    nb = batch * RWKV_HEADS // SCAN_PAIRS
    v4 = v.reshape(batch, seq, RWKV_HEADS, RWKV_HEAD).transpose(1, 3, 0, 2)
    return v4.reshape(seq, RWKV_HEAD, nb, SCAN_PAIRS).transpose(2, 0, 1, 3)


def _scan_val_unlayout(y, batch, seq):
    v4 = y.transpose(1, 2, 0, 3).reshape(seq, RWKV_HEAD, batch, RWKV_HEADS)
    return v4.transpose(2, 0, 3, 1).reshape(batch * seq, RWKV_DIM)


def _scan_state_layout(s, batch):
    nb = batch * RWKV_HEADS // SCAN_PAIRS
    s6 = s.reshape(batch, RWKV_HEADS, RWKV_HEAD, 2, SCAN_KH).transpose(4, 2, 3, 0, 1)
    s6 = s6.reshape(SCAN_KH, RWKV_HEAD, 2, nb, SCAN_PAIRS).transpose(3, 0, 1, 2, 4)
    return s6.reshape(nb, SCAN_KH, RWKV_HEAD, LANES)


def _scan_state_unlayout(arr, batch):
    nb = arr.shape[0]
    s = arr.reshape(nb, SCAN_KH, RWKV_HEAD, 2, SCAN_PAIRS).transpose(0, 4, 2, 3, 1)
    return s.reshape(batch, RWKV_HEADS, RWKV_HEAD, RWKV_HEAD)


def _swap_halves(w):
    half = w.shape[-1] // 2
    return jnp.concatenate([w[..., half:], w[..., :half]], axis=-1)


def _prep_even(i, norm_mix, norm_ffn, w_in, q_norm, kv_norm, w_uq, w_uk, w_uv, mu, w0, w2, a0, a2,
               g2, k_k, k_a, r_k, ln_g, ln_b, w_out, ffn_gu, ffn_down):
    w = {}
    row = lambda v: v[i].reshape(1, -1)
    w_in = w_in[i]
    w["norm_mix"] = row(norm_mix)
    w["norm_ffn"] = row(norm_ffn)
    w["w_q"] = w_in[:, :MLA_Q_RANK].astype(BF16)
    w_kv = w_in[:, MLA_Q_RANK:MLA_Q_RANK + MLA_LAT]
    w["w_ckv"] = w_kv[:, :MLA_KV_RANK].astype(BF16)
    lane_pad = lambda m: jnp.pad(m, [(0, 0)] * (m.ndim - 1) + [(0, LANES - m.shape[-1])])
    w["w_pe_a"] = lane_pad(w_kv[:, MLA_KV_RANK:]).astype(BF16)
    w["w_pe_b"] = lane_pad(_swap_halves(w_kv[:, MLA_KV_RANK:])).astype(BF16)
    w["w_rw"] = w_in[:, MLA_Q_RANK + MLA_LAT:].astype(BF16)
    w["q_norm"] = row(q_norm)
    w["kv_norm"] = row(kv_norm)
    uq = w_uq[i].reshape(MLA_Q_RANK, MLA_HEADS, MLA_NOPE + MLA_ROPE)
    uq_pe = uq[:, :, MLA_NOPE:]
    w["w_qpe_a"] = lane_pad(uq_pe).reshape(MLA_Q_RANK, -1).astype(BF16)
    w["w_qpe_b"] = lane_pad(_swap_halves(uq_pe)).reshape(MLA_Q_RANK, -1).astype(BF16)
    w["w_qlat"] = _fold_qlat(uq[:, :, :MLA_NOPE].transpose(1, 0, 2), w_uk[i].transpose(1, 0, 2))
    uv = w_uv[i].transpose(1, 0, 2).reshape(MLA_HEADS // 2, 2, MLA_KV_RANK, MLA_V)
    zero = jnp.zeros_like(uv[:, 0])
    w["w_uv_bd"] = jnp.concatenate(
        [jnp.concatenate([uv[:, 0], zero], axis=-1), jnp.concatenate([zero, uv[:, 1]], axis=-1)],
        axis=1).astype(BF16)
    w["mu"] = row(mu)
    w["w0"] = row(w0)
    pad = lambda m, before: jnp.pad(m, ((before, LANES - before - m.shape[0]), (0, 0))).astype(BF16)
    w["w2p"] = pad(w2[i], 0)
    w["a2p"] = pad(a2[i], RWKV_W_LORA)
    w["a0"] = row(a0)
    w["g2"] = g2[i].astype(BF16)
    w["k_k"] = row(k_k)
    w["k_a"] = row(k_a)
    w["r_k"] = row(r_k)
    w["ln_g"] = row(ln_g)
    w["ln_b"] = row(ln_b)
    head = jnp.arange(RWKV_DIM) // RWKV_HEAD
    w["ones_bd"] = (head[:, None] == head[None, :]).astype(BF16)
    w["w_out_a"] = w_out[i][:MLA_HEADS * MLA_V].astype(BF16)
    w["w_out_b"] = w_out[i][MLA_HEADS * MLA_V:].astype(BF16)
    w["ffn_gu"] = ffn_gu[i].astype(BF16)
    w["ffn_down"] = ffn_down[i].astype(BF16)
    return w


def _prep_odd(i, norm_mix, norm_ffn, w_in, a2, ab, gla_norm, w_out, router, moe_gu, moe_down):
    w = {}
    row = lambda v: v[i].reshape(1, -1)
    w_in = w_in[i]
    w["norm_mix"] = row(norm_mix)
    w["norm_ffn"] = row(norm_ffn)
    w["w_q"] = w_in[:, :GLA_KDIM].astype(BF16)
    w["w_k"] = w_in[:, GLA_KDIM:2 * GLA_KDIM].astype(BF16)
    w["w_v"] = w_in[:, 2 * GLA_KDIM:2 * GLA_KDIM + GLA_VDIM].astype(BF16)
    w["w_g"] = w_in[:, 2 * GLA_KDIM + GLA_VDIM:2 * GLA_KDIM + 2 * GLA_VDIM].astype(BF16)
    w["w_xa"] = jnp.pad(w_in[:, 2 * GLA_KDIM + 2 * GLA_VDIM:],
                        ((0, 0), (0, LANES - GLA_GATE_RANK))).astype(BF16)
    w["a2p"] = jnp.pad(a2[i], ((0, LANES - GLA_GATE_RANK), (0, 0))).astype(BF16)
    w["ab"] = row(ab)
    w["gla_norm"] = row(gla_norm)
    w["w_out"] = w_out[i].astype(BF16)
    w["router"] = jnp.pad(router[i], ((0, 0), (0, LANES - N_EXPERTS)))
    w["layer"] = i
    w["moe_gu"] = moe_gu
    w["moe_down"] = moe_down
    return w


def _rope_tables(pos, reps):
    inv = ROPE_THETA ** (-jnp.arange(0, MLA_ROPE, 2, dtype=F32) / MLA_ROPE)
    ang = pos.astype(F32)[:, None] * inv[None, :]
    cos, sin = jnp.cos(ang), jnp.sin(ang)
    pad = ((0, 0), (0, LANES - MLA_ROPE))
    cs = jnp.tile(jnp.pad(jnp.concatenate([cos, cos], axis=-1), pad), (reps, 1))
    sn = jnp.tile(jnp.pad(jnp.concatenate([-sin, sin], axis=-1), pad), (reps, 1))
    return {"cs": cs, "sn": sn, "cs8": jnp.tile(cs, (1, MLA_HEADS)), "sn8": jnp.tile(sn, (1, MLA_HEADS))}


def _even_layer(x, batch, seq, tabs, state, shift0, past, w, tm, tc):
    n = batch * seq
    lat, lat_b, q_lat, q_pe, rw = _even_in(x, w, tabs, tm)
    if past is None:
        o_lat = _mla_prompt(q_lat, q_pe, lat_b, batch, seq)
    else:
        cache, layer, page_table = past
        rows = seq * MLA_HEADS
        q_full = jnp.concatenate([q_lat.reshape(batch, rows, MLA_KV_RANK),
                                  q_pe.reshape(batch, rows, LANES)[:, :, :MLA_ROPE]], axis=-1)
        new_pad_t = jnp.pad(lat_b.reshape(batch, seq, MLA_LATB)[:, :, :MLA_LAT],
                            ((0, 0), (0, PAGE_SIZE - seq), (0, 0))).transpose(0, 2, 1)
        o_lat = _mla_decode(page_table, q_full, new_pad_t, cache.transpose(0, 1, 3, 2), layer)
        o_lat = o_lat.reshape(n, MLA_HEADS * MLA_KV_RANK)

    rw3 = rw.reshape(batch, seq, RWKV_PROJ)
    xs5, v, g, rkv = _rwkv_prep(rw, shift0, w, tm, seq)
    y_l, s_l = _rwkv_scan(_scan_vec_layout(xs5, batch, seq), _scan_val_layout(v, batch, seq),
                          _scan_state_layout(state, batch), o_lat, tc)
    y = _scan_val_unlayout(y_l, batch, seq)
    new_state = _scan_state_unlayout(s_l, batch)

    x = _even_out(y, rkv, g, o_lat, x, w, tm)
    x = _ffn(x, w["norm_ffn"], w["ffn_gu"], w["ffn_down"], tm)
    return x, lat.reshape(batch, seq, MLA_LAT), new_state, rw3[:, -1]


def _odd_mixer_layer(x, batch, seq, state, w, tm):
    q, k, v, gate, la = _odd_in(x, w, tm)
    seq_p = -(-seq // GLA_CHUNK) * GLA_CHUNK
    if seq_p != seq:
        padr = lambda t: jnp.pad(t.reshape(batch, seq, -1), ((0, 0), (0, seq_p - seq), (0, 0))
                                 ).reshape(batch * seq_p, -1)
        qp, kp, vp, lap = padr(q), padr(k), padr(v), padr(la)
    else:
        qp, kp, vp, lap = q, k, v, la
    o, st = _gla(qp, kp, vp, lap, state.transpose(0, 1, 3, 2), batch, seq_p)
    if seq_p != seq:
        o = o.reshape(batch, seq_p, GLA_VDIM)[:, :seq].reshape(batch * seq, GLA_VDIM)
    return _odd_out(o, gate, x, w, tm), st.transpose(0, 1, 3, 2)


def _moe_all_groups(xs_groups, tms, w, final_norm, tm_moe):
    sizes = [x.shape[0] for x in xs_groups]
    n_total = sum(sizes)
    bufs, row0 = None, 0
    for x, tm in zip(xs_groups, tms):
        bufs = _router(x, w["norm_ffn"], w["router"], tm, n_total, row0, bufs)
        row0 += x.shape[0]
    xn2, idx, gates = bufs
    local, wstart, src, tile_expert, tile_valid = _route(idx[:, :2], tm_moe, MOE_TMC)
    rows = _moe_gather(src, xn2, tm_moe)
    h = _moe_up(tile_expert, tile_valid, rows, w["moe_gu"], w["layer"], tm_moe)
    ys = _moe_down(tile_expert, tile_valid, h, w["moe_down"], w["layer"], tm_moe)
    g1b = jnp.broadcast_to(gates[:, 0:1], (n_total, LANES))
    g2b = jnp.broadcast_to(gates[:, 1:2], (n_total, LANES))
    outs, row0 = [], 0
    for x in xs_groups:
        outs.append(_moe_combine(local, wstart, ys, g1b, g2b, x, final_norm, MOE_TMC, row0))
        row0 += x.shape[0]
    return outs


def kernel(x_prompt, x_sample, cache_mla, state_rwkv, state_rwkv_shift, state_gla, page_table, norm_mix_even, norm_ffn_even, w_in_even, mla_q_norm, mla_kv_norm, mla_w_uq, mla_w_uk, mla_w_uv, rwkv_mu, rwkv_w0, rwkv_w2, rwkv_a0, rwkv_a2, rwkv_g2, rwkv_k_k, rwkv_k_a, rwkv_r_k, rwkv_ln_g, rwkv_ln_b, w_out_even, ffn_w_gu_even, ffn_w_down_even, norm_mix_odd, norm_ffn_odd, w_in_odd, gla_a2, gla_ab, gla_norm, w_out_odd, moe_router, moe_w_gu, moe_w_down, final_norm):
    bp, tp, _ = x_prompt.shape
    bs, ts, _ = x_sample.shape
    past_len = page_table.shape[1] * PAGE_SIZE
    tm_p, tm_s = 512, bs * ts
    we = _prep_even(0, norm_mix_even, norm_ffn_even, w_in_even, mla_q_norm, mla_kv_norm, mla_w_uq,
                    mla_w_uk, mla_w_uv, rwkv_mu, rwkv_w0, rwkv_w2, rwkv_a0, rwkv_a2, rwkv_g2,
                    rwkv_k_k, rwkv_k_a, rwkv_r_k, rwkv_ln_g, rwkv_ln_b, w_out_even, ffn_w_gu_even,
                    ffn_w_down_even)
    wo = _prep_odd(0, norm_mix_odd, norm_ffn_odd, w_in_odd, gla_a2, gla_ab, gla_norm, w_out_odd,
                   moe_router, moe_w_gu, moe_w_down)
    fn = final_norm.reshape(1, -1)
    tabs_p = _rope_tables(jnp.arange(tp), 1)
    tabs_s = _rope_tables(past_len + jnp.arange(ts), bs)

    hp = x_prompt.reshape(bp * tp, D_MODEL)
    hs = x_sample.reshape(bs * ts, D_MODEL)
    zeros_state = jnp.zeros((bp, RWKV_HEADS, RWKV_HEAD, RWKV_HEAD), F32)
    zeros_shift = jnp.zeros((bp, RWKV_PROJ), F32)
    hp, lat_p, rs_p, sh_p = _even_layer(hp, bp, tp, tabs_p, zeros_state, zeros_shift, None, we,
                                        tm_p, 64)
    hs, lat_s, rs_s, sh_s = _even_layer(hs, bs, ts, tabs_s, state_rwkv[0], state_rwkv_shift[0],
                                        (cache_mla, 0, page_table), we, tm_s, ts)
    zeros_gla = jnp.zeros((bp, GLA_HEADS, GLA_DK, GLA_DV), F32)
    hp, gs_p = _odd_mixer_layer(hp, bp, tp, zeros_gla, wo, tm_p)
    hs, gs_s = _odd_mixer_layer(hs, bs, ts, state_gla[0], wo, tm_s)
    yp, ys = _moe_all_groups([hp, hs], [tm_p, tm_s], wo, fn, 512)
    return (yp.reshape(bp, tp, D_MODEL), ys.reshape(bs, ts, D_MODEL), lat_p[None], lat_s[None],
            rs_p[None], rs_s[None], sh_p[None], sh_s[None], gs_p[None], gs_s[None])
```

```python
import functools

import jax
import jax.numpy as jnp
from jax import lax
from jax.experimental import pallas as pl
from jax.experimental.pallas import tpu as pltpu

F32 = jnp.float32
BF16 = jnp.bfloat16

D_MODEL = 1024
PAGE_SIZE = 128
NORM_EPS = 1e-6

MLA_HEADS = 8
MLA_NOPE = 64
MLA_ROPE = 32
MLA_V = 64
MLA_Q_RANK = 384
MLA_KV_RANK = 256
MLA_LAT = MLA_KV_RANK + MLA_ROPE
MLA_LATB = MLA_KV_RANK + 128
MLA_SCALE = (MLA_NOPE + MLA_ROPE) ** -0.5
ROPE_THETA = 10000.0

RWKV_HEADS = 8
RWKV_HEAD = 64
RWKV_DIM = RWKV_HEADS * RWKV_HEAD
RWKV_W_LORA = 64
RWKV_A_LORA = 64
RWKV_G_LORA = 128
RWKV_PROJ = 3 * RWKV_DIM + RWKV_W_LORA + RWKV_A_LORA + RWKV_G_LORA
RWKV_LN_EPS = 64e-5

GLA_HEADS = 4
GLA_DK = 128
GLA_DV = 256
GLA_KDIM = GLA_HEADS * GLA_DK
GLA_VDIM = GLA_HEADS * GLA_DV
GLA_GATE_RANK = 16
GLA_GATE_NORM = 16.0
GLA_CHUNK = 128

D_FF = 2816
N_EXPERTS = 8
D_FF_EXPERT = 3584

LANES = 128
VMEM_LIMIT = 56 * 1024 * 1024
NEG_BIG = -1e30
LOG2_E = 1.4426950408889634
Q_PRESCALE = MLA_SCALE * LOG2_E


def _cparams(sem):
    return pltpu.CompilerParams(dimension_semantics=sem, vmem_limit_bytes=VMEM_LIMIT)


def _const_spec(shape):
    nd = len(shape)
    return pl.BlockSpec(shape, lambda *_: (0,) * nd)


def _row_spec(tm, width):
    return pl.BlockSpec((tm, width), lambda i: (i, 0))


def _dot(a, b):
    return jnp.dot(a.astype(BF16), b.astype(BF16), preferred_element_type=F32)


def _dot_nt(a, b):
    return lax.dot_general(a.astype(BF16), b.astype(BF16), (((1,), (1,)), ((), ())),
                           preferred_element_type=F32)


def _split2(x):
    hi = x.astype(BF16)
    lo = (x - hi.astype(F32)).astype(BF16)
    return hi, lo


def _split3(x):
    hi = x.astype(BF16)
    r1 = x - hi.astype(F32)
    mid = r1.astype(BF16)
    lo = (r1 - mid.astype(F32)).astype(BF16)
    return hi, mid, lo


def _dot_exact_rhs(x, e):
    hi, mid, lo = _split3(x)
    return (jnp.dot(hi, e, preferred_element_type=F32) + jnp.dot(mid, e, preferred_element_type=F32)
            + jnp.dot(lo, e, preferred_element_type=F32))


def _dot_exact_lhs(e, x):
    hi, mid, lo = _split3(x)
    return (jnp.dot(e, hi, preferred_element_type=F32) + jnp.dot(e, mid, preferred_element_type=F32)
            + jnp.dot(e, lo, preferred_element_type=F32))


def _dot_f32ish(a, b):
    ah, al = _split2(a)
    bh, bl = _split2(b)
    return (jnp.dot(ah, bh, preferred_element_type=F32) + jnp.dot(ah, bl, preferred_element_type=F32)
            + jnp.dot(al, bh, preferred_element_type=F32))


def _lane_tile(x, width):
    return x if width == LANES else jnp.concatenate([x] * (width // LANES), axis=1)


def _rms(x, g, eps=NORM_EPS):
    return x * lax.rsqrt(jnp.mean(x * x, axis=-1, keepdims=True) + eps) * g


def _sigmoid(x):
    return 1.0 / (1.0 + jnp.exp(-x))


def _softplus(x):
    return jnp.maximum(x, 0.0) + jnp.log(1.0 + jnp.exp(-jnp.abs(x)))


def _fold_qlat_kernel(uq_ref, uk_ref, o_ref):
    a = uq_ref[...]
    b = uk_ref[...]
    ah, al = _split2(a)
    bh, bl = _split2(b)
    dn = (((1,), (1,)), ((), ()))
    o = (lax.dot_general(ah, bh, dn, preferred_element_type=F32)
         + lax.dot_general(ah, bl, dn, preferred_element_type=F32)
         + lax.dot_general(al, bh, dn, preferred_element_type=F32))
    o_ref[...] = o.astype(BF16)


def _fold_qlat(uq_nope, uk):
    return pl.pallas_call(
        _fold_qlat_kernel,
        grid=(MLA_HEADS,),
        in_specs=[pl.BlockSpec((None, MLA_Q_RANK, MLA_NOPE), lambda h: (h, 0, 0)),
                  pl.BlockSpec((None, MLA_KV_RANK, MLA_NOPE), lambda h: (h, 0, 0))],
        out_specs=pl.BlockSpec((MLA_Q_RANK, MLA_KV_RANK), lambda h: (0, h)),
        out_shape=jax.ShapeDtypeStruct((MLA_Q_RANK, MLA_HEADS * MLA_KV_RANK), BF16),
        compiler_params=_cparams(("arbitrary",)),
    )(uq_nope, uk)


def _even_in_kernel(x_ref, g_ref, wq_ref, wckv_ref, wpa_ref, wpb_ref, wrw_ref, qn_ref, kvn_ref,
                    cs_ref, sn_ref, wql_ref, wqa_ref, wqb_ref, cs8_ref, sn8_ref,
                    lat_ref, latb_ref, ql_ref, qpe_ref, rw_ref):
    xn = _rms(x_ref[...], g_ref[...]).astype(BF16)
    cq = _rms(_dot(xn, wq_ref[...]), qn_ref[...]).astype(BF16)
    ql_ref[...] = (_dot(cq, wql_ref[...]) * Q_PRESCALE).astype(BF16)
    qpe = _dot(cq, wqa_ref[...]) * cs8_ref[...] + _dot(cq, wqb_ref[...]) * sn8_ref[...]
    qpe_ref[...] = (qpe * Q_PRESCALE).astype(BF16)
    ckv = _rms(_dot(xn, wckv_ref[...]), kvn_ref[...])
    kpe = _dot(xn, wpa_ref[...]) * cs_ref[...] + _dot(xn, wpb_ref[...]) * sn_ref[...]
    lat_ref[:, :MLA_KV_RANK] = ckv
    lat_ref[:, MLA_KV_RANK:] = kpe[:, :MLA_ROPE]
    latb_ref[:, :MLA_KV_RANK] = ckv.astype(BF16)
    latb_ref[:, MLA_KV_RANK:] = kpe.astype(BF16)
    rw_ref[...] = _dot(xn, wrw_ref[...])


def _even_in(x, w, tabs, tm):
    n = x.shape[0]
    nt = tabs["cs"].shape[0] // tm
    tab = lambda width: pl.BlockSpec((tm, width), lambda i: (i % nt, 0))
    hq = MLA_HEADS * MLA_KV_RANK
    hr = MLA_HEADS * LANES
    return pl.pallas_call(
        _even_in_kernel,
        grid=(n // tm,),
        in_specs=[_row_spec(tm, D_MODEL), _const_spec((1, D_MODEL)),
                  _const_spec((D_MODEL, MLA_Q_RANK)), _const_spec((D_MODEL, MLA_KV_RANK)),
                  _const_spec((D_MODEL, LANES)), _const_spec((D_MODEL, LANES)),
                  _const_spec((D_MODEL, RWKV_PROJ)), _const_spec((1, MLA_Q_RANK)),
                  _const_spec((1, MLA_KV_RANK)), tab(LANES), tab(LANES),
                  _const_spec((MLA_Q_RANK, hq)), _const_spec((MLA_Q_RANK, hr)),
                  _const_spec((MLA_Q_RANK, hr)), tab(hr), tab(hr)],
        out_specs=[_row_spec(tm, MLA_LAT), _row_spec(tm, MLA_LATB), _row_spec(tm, hq),
                   _row_spec(tm, hr), _row_spec(tm, RWKV_PROJ)],
        out_shape=[jax.ShapeDtypeStruct((n, MLA_LAT), F32), jax.ShapeDtypeStruct((n, MLA_LATB), BF16),
                   jax.ShapeDtypeStruct((n, hq), BF16), jax.ShapeDtypeStruct((n, hr), BF16),
                   jax.ShapeDtypeStruct((n, RWKV_PROJ), F32)],
        compiler_params=_cparams(("parallel",)),
    )(x, w["norm_mix"], w["w_q"], w["w_ckv"], w["w_pe_a"], w["w_pe_b"], w["w_rw"], w["q_norm"],
      w["kv_norm"], tabs["cs"], tabs["sn"], w["w_qlat"], w["w_qpe_a"], w["w_qpe_b"],
      tabs["cs8"], tabs["sn8"])


ATT_TQ = 256


def _mla_prompt_kernel(qi_ref, kj_ref, ql_ref, qpe_ref, lat_ref, o_ref,
                       m_sc, l_sc, a_sc, acc_sc, s_sc, p_sc):
    step = pl.program_id(1)
    i = qi_ref[step]
    j = kj_ref[step]
    heads = range(MLA_HEADS)

    @pl.when(j == 0)
    def _():
        m_sc[...] = jnp.full(m_sc.shape, NEG_BIG, F32)
        l_sc[...] = jnp.zeros(l_sc.shape, F32)
        acc_sc[...] = jnp.zeros(acc_sc.shape, F32)

    def tile(masked):
        ckv = lat_ref[:, :MLA_KV_RANK]
        kpe = lat_ref[:, MLA_KV_RANK:]
        for h in heads:
            s_sc[h] = (_dot_nt(ql_ref[:, h * MLA_KV_RANK:(h + 1) * MLA_KV_RANK], ckv)
                       + _dot_nt(qpe_ref[:, h * LANES:(h + 1) * LANES], kpe))
        for h in heads:
            s = s_sc[h]
            if masked:
                tok = lax.broadcasted_iota(jnp.int32, s.shape, 0)
                key = lax.broadcasted_iota(jnp.int32, s.shape, 1)
                s = jnp.where(key <= tok, s, NEG_BIG)
            m_prev = m_sc[h]
            m_new = jnp.maximum(m_prev, jnp.max(s, axis=-1, keepdims=True))
            alpha = jnp.exp2(m_prev - m_new)
            p = jnp.exp2(s - _lane_tile(m_new, ATT_TQ))
            l_sc[h] = alpha * l_sc[h] + jnp.sum(p, axis=-1, keepdims=True)
            m_sc[h] = m_new
            a_sc[h] = alpha
            p_sc[h] = p.astype(BF16)
        for h in heads:
            acc_sc[h] = _lane_tile(a_sc[h], MLA_KV_RANK) * acc_sc[h] + _dot(p_sc[h], ckv)

    @pl.when(j < i)
    def _():
        tile(False)

    @pl.when(j == i)
    def _():
        tile(True)
        for h in heads:
            o_ref[:, h * MLA_KV_RANK:(h + 1) * MLA_KV_RANK] = (
                acc_sc[h] / _lane_tile(l_sc[h], MLA_KV_RANK)).astype(BF16)


def _mla_prompt(q_lat, q_pe, lat_b, batch, seq):
    nq = seq // ATT_TQ
    pairs = [(i, j) for i in range(nq) for j in range(i + 1)]
    qi = jnp.array([p[0] for p in pairs], jnp.int32)
    kj = jnp.array([p[1] for p in pairs], jnp.int32)
    hq = MLA_HEADS * MLA_KV_RANK
    grid_spec = pltpu.PrefetchScalarGridSpec(
        num_scalar_prefetch=2,
        grid=(batch, len(pairs)),
        in_specs=[pl.BlockSpec((ATT_TQ, hq), lambda b, s, qi, kj: (b * nq + qi[s], 0)),
                  pl.BlockSpec((ATT_TQ, MLA_HEADS * LANES), lambda b, s, qi, kj: (b * nq + qi[s], 0)),
                  pl.BlockSpec((ATT_TQ, MLA_LATB), lambda b, s, qi, kj: (b * nq + kj[s], 0))],
        out_specs=pl.BlockSpec((ATT_TQ, hq), lambda b, s, qi, kj: (b * nq + qi[s], 0)),
        scratch_shapes=[pltpu.VMEM((MLA_HEADS, ATT_TQ, LANES), F32),
                        pltpu.VMEM((MLA_HEADS, ATT_TQ, LANES), F32),
                        pltpu.VMEM((MLA_HEADS, ATT_TQ, LANES), F32),
                        pltpu.VMEM((MLA_HEADS, ATT_TQ, MLA_KV_RANK), F32),
                        pltpu.VMEM((MLA_HEADS, ATT_TQ, ATT_TQ), F32),
                        pltpu.VMEM((MLA_HEADS, ATT_TQ, ATT_TQ), BF16)],
    )
    return pl.pallas_call(
        _mla_prompt_kernel,
        grid_spec=grid_spec,
        out_shape=jax.ShapeDtypeStruct(q_lat.shape, BF16),
        compiler_params=_cparams(("parallel", "arbitrary")),
    )(qi, kj, q_lat, q_pe, lat_b)


PAGES_PER_STEP = 32
DECODE_GROUPS = 8


def _mla_decode_kernel(pt_ref, q_ref, new_ref, *rest):
    page_refs = rest[:PAGES_PER_STEP]
    o_ref, m_sc, l_sc, acc_sc = rest[PAGES_PER_STEP:]
    j = pl.program_id(1)
    q = q_ref[0]

    @pl.when(j == 0)
    def _():
        m_sc[...] = jnp.full(m_sc.shape, NEG_BIG, F32)
        l_sc[...] = jnp.zeros(l_sc.shape, F32)
        acc_sc[...] = jnp.zeros(acc_sc.shape, F32)

    def update(state, s, values_t):
        m_prev, l_prev, acc = state
        m_new = jnp.maximum(m_prev, jnp.max(s, axis=-1, keepdims=True))
        alpha = jnp.exp2(m_prev - m_new)
        p = jnp.exp2(s - _lane_tile(m_new, s.shape[1]))
        l_new = alpha * l_prev + jnp.sum(p, axis=-1, keepdims=True)
        return m_new, l_new, _lane_tile(alpha, MLA_KV_RANK) * acc + _dot_nt(p, values_t)

    group = PAGES_PER_STEP // DECODE_GROUPS
    keys = [jnp.concatenate([pr[...].astype(BF16) for pr in page_refs[g * group:(g + 1) * group]],
                            axis=1) for g in range(DECODE_GROUPS)]
    scores = [_dot(q, kt) for kt in keys]
    state = (m_sc[...], l_sc[...], acc_sc[...])
    for s, kt in zip(scores, keys):
        state = update(state, s, kt[:MLA_KV_RANK, :])
    m_sc[...], l_sc[...], acc_sc[...] = state

    @pl.when(j == pl.num_programs(1) - 1)
    def _():
        new_t = new_ref[0]
        sn = _dot(q, new_t)
        tok = lax.broadcasted_iota(jnp.int32, sn.shape, 0) >> 3
        key = lax.broadcasted_iota(jnp.int32, sn.shape, 1)
        sn = jnp.where(key <= tok, sn, NEG_BIG)
        _, l_fin, acc_fin = update(state, sn, new_t[:MLA_KV_RANK, :])
        o_ref[0] = (acc_fin / _lane_tile(l_fin, MLA_KV_RANK)).astype(BF16)


def _mla_decode(page_table, q_full, new_pad_t, cache_t, layer):
    db, n_pages = page_table.shape
    rows = q_full.shape[1]
    steps = n_pages // PAGES_PER_STEP

    def page_spec(p):
        return pl.BlockSpec((None, None, MLA_LAT, PAGE_SIZE),
                            lambda b, j, pt: (layer, pt[b, j * PAGES_PER_STEP + p], 0, 0))

    grid_spec = pltpu.PrefetchScalarGridSpec(
        num_scalar_prefetch=1,
        grid=(db, steps),
        in_specs=[pl.BlockSpec((1, rows, MLA_LAT), lambda b, j, pt: (b, 0, 0)),
                  pl.BlockSpec((1, MLA_LAT, PAGE_SIZE), lambda b, j, pt: (b, 0, 0))]
        + [page_spec(p) for p in range(PAGES_PER_STEP)],
        out_specs=pl.BlockSpec((1, rows, MLA_KV_RANK), lambda b, j, pt: (b, 0, 0)),
        scratch_shapes=[pltpu.VMEM((rows, LANES), F32), pltpu.VMEM((rows, LANES), F32),
                        pltpu.VMEM((rows, MLA_KV_RANK), F32)],
    )
    return pl.pallas_call(
        _mla_decode_kernel,
        grid_spec=grid_spec,
        out_shape=jax.ShapeDtypeStruct((db, rows, MLA_KV_RANK), BF16),
        compiler_params=_cparams(("parallel", "arbitrary")),
    )(page_table, q_full, new_pad_t, *([cache_t] * PAGES_PER_STEP))


def _rwkv_prep_kernel(rw_ref, before_ref, sh_ref, mu_ref, w0_ref, w2_ref, a0_ref, a2_ref, g2_ref,
                      kk_ref, ka_ref, rk_ref, ones_ref, xs_ref, v_ref, g_ref, rkv_ref, *, tm, seq):
    rw = rw_ref[...]
    rolled = pltpu.roll(rw, 1, axis=0)
    row = lax.broadcasted_iota(jnp.int32, rw.shape, 0)
    if seq >= tm:
        at_start = pl.program_id(0) % (seq // tm) == 0
        first = jnp.where(at_start, sh_ref[...], before_ref[7:8, :])
        prev = jnp.where(row == 0, first, rolled)
    else:
        prev = jnp.where((row & (seq - 1)) == 0, sh_ref[...], rolled)
    xs = rw + (prev - rw) * mu_ref[...]
    d = RWKV_DIM
    r = xs[:, :d]
    k = xs[:, d:2 * d]
    v = xs[:, 2 * d:3 * d]
    xwa = xs[:, 3 * d:3 * d + LANES]
    xg = xs[:, 3 * d + LANES:]
    ones = ones_ref[...]
    w_log = -_softplus(-(w0_ref[...] + _dot(jnp.tanh(xwa), w2_ref[...]))) - 0.5
    a = _sigmoid(a0_ref[...] + _dot(xwa, a2_ref[...]))
    g_ref[...] = _dot(_sigmoid(xg), g2_ref[...])
    kk = k * kk_ref[...]
    ss = _dot_exact_rhs(kk * kk, ones)
    kk = kk / jnp.maximum(jnp.sqrt(ss), 1e-12)
    k2 = k * (1.0 + (a - 1.0) * ka_ref[...])
    xs_ref[0] = -kk
    xs_ref[1] = jnp.exp(-jnp.exp(w_log))
    xs_ref[2] = kk * a
    xs_ref[3] = k2
    xs_ref[4] = r
    v_ref[...] = v
    rkv_ref[...] = _dot_exact_rhs(r * k2 * rk_ref[...], ones) * v


def _rwkv_prep(rw, shift0, w, tm, seq):
    n = rw.shape[0]
    d = RWKV_DIM
    vec = _const_spec((1, d))
    if seq >= tm:
        tiles = seq // tm
        sh = shift0.reshape(-1, 1, RWKV_PROJ)
        sh_spec = pl.BlockSpec((None, 1, RWKV_PROJ), lambda i: (i // tiles, 0, 0))
    else:
        sh = jnp.repeat(shift0, seq, axis=0)
        sh_spec = _row_spec(tm, RWKV_PROJ)
    before_spec = pl.BlockSpec((8, RWKV_PROJ), lambda i: (jnp.maximum(i * (tm // 8) - 1, 0), 0))
    return pl.pallas_call(
        functools.partial(_rwkv_prep_kernel, tm=tm, seq=seq),
        grid=(n // tm,),
        in_specs=[_row_spec(tm, RWKV_PROJ), before_spec, sh_spec, _const_spec((1, RWKV_PROJ)),
                  vec, _const_spec((LANES, d)), vec, _const_spec((LANES, d)),
                  _const_spec((RWKV_G_LORA, d)), vec, vec, vec, _const_spec((d, d))],
        out_specs=[pl.BlockSpec((5, tm, d), lambda i: (0, i, 0))] + [_row_spec(tm, d)] * 3,
        out_shape=[jax.ShapeDtypeStruct((5, n, d), F32)] + [jax.ShapeDtypeStruct((n, d), F32)] * 3,
        compiler_params=_cparams(("parallel",)),
    )(rw, rw, sh, w["mu"], w["w0"], w["w2p"], w["a0"], w["a2p"], w["g2"], w["k_k"], w["k_a"],
      w["r_k"], w["ones_bd"])


SCAN_KH = RWKV_HEAD // 2
SCAN_PAIRS = LANES // 2
SCAN_VR = RWKV_HEAD // 2


def _rwkv_scan_kernel(x_ref, v_ref, s0_ref, after_ref, y_ref, s_ref, c_sc, d_sc, *, tc):
    del after_ref
    @pl.when(pl.program_id(1) == 0)
    def _():
        s_ref[...] = s0_ref[...]

    half_a = slice(0, SCAN_VR)
    half_b = slice(SCAN_VR, RWKV_HEAD)

    def both_halves(p):
        return p + pltpu.roll(p, SCAN_PAIRS, axis=1)

    def key_dot(u, w):
        return both_halves(jnp.sum(u * w, axis=0, keepdims=True))

    def first_partial(rows):
        p = s_ref[0, 0, rows, :] * x_ref[0, 0, 0, 0:1, :]
        for k in range(1, SCAN_KH):
            p = p + s_ref[0, k, rows, :] * x_ref[0, 0, 0, k:k + 1, :]
        return p

    def half_step(t, rows, sa):
        v_half = v_ref[0, t, rows, :]
        v = jnp.concatenate([v_half, v_half], axis=1)
        q = None
        y = None
        for k in range(SCAN_KH):
            s_old = s_ref[0, k, rows, :]
            qk = s_old * c_sc[k:k + 1, :]
            sn = (s_old * x_ref[1, 0, t, k:k + 1, :] + sa * x_ref[2, 0, t, k:k + 1, :]
                  + v * x_ref[3, 0, t, k:k + 1, :])
            s_ref[0, k, rows, :] = sn
            yk = sn * x_ref[4, 0, t, k:k + 1, :]
            q = qk if q is None else q + qk
            y = yk if y is None else y + yk
        return q, sa * d_sc[0:1, :] + v * d_sc[1:2, :], y

    def store_y(t, y_a, y_b):
        y_ref[0, t, half_a, :] = both_halves(y_a)[:, :SCAN_PAIRS]
        y_ref[0, t, half_b, :] = both_halves(y_b)[:, :SCAN_PAIRS]

    def step(t, carry):
        sa_a, q_b, corr_b, y_a, y_b = carry
        store_y(jnp.maximum(t - 1, 0), y_a, y_b)
        a_next = x_ref[0, 0, jnp.minimum(t + 1, tc - 1)]
        c_sc[...] = x_ref[1, 0, t] * a_next
        d_sc[0:1, :] = key_dot(x_ref[2, 0, t], a_next)
        d_sc[1:2, :] = key_dot(x_ref[3, 0, t], a_next)
        sa_b = both_halves(q_b) + corr_b
        q_a, corr_a, y_a_new = half_step(t, half_a, sa_a)
        sa_a_next = both_halves(q_a) + corr_a
        q_b_next, corr_b_next, y_b_new = half_step(t, half_b, sa_b)
        return sa_a_next, q_b_next, corr_b_next, y_a_new, y_b_new

    zero = jnp.zeros((SCAN_VR, LANES), F32)
    init = (both_halves(first_partial(half_a)), first_partial(half_b), zero, zero, zero)
    final = lax.fori_loop(0, tc, step, init)
    store_y(tc - 1, final[3], final[4])


def _rwkv_scan(xs, v, s0, after, tc):
    _, nb, t, _, _ = xs.shape
    xspec = pl.BlockSpec((5, 1, tc, SCAN_KH, LANES), lambda n, c: (0, n, c, 0, 0))
    vspec = pl.BlockSpec((1, tc, RWKV_HEAD, SCAN_PAIRS), lambda n, c: (n, c, 0, 0))
    sspec = pl.BlockSpec((1, SCAN_KH, RWKV_HEAD, LANES), lambda n, c: (n, 0, 0, 0))
    return pl.pallas_call(
        functools.partial(_rwkv_scan_kernel, tc=tc),
        grid=(nb, t // tc),
        in_specs=[xspec, vspec, sspec, pl.BlockSpec(memory_space=pl.ANY)],
        out_specs=[vspec, sspec],
        out_shape=[jax.ShapeDtypeStruct(v.shape, F32), jax.ShapeDtypeStruct(s0.shape, F32)],
        scratch_shapes=[pltpu.VMEM((SCAN_KH, LANES), F32), pltpu.VMEM((8, LANES), F32)],
        compiler_params=_cparams(("parallel", "arbitrary")),
    )(xs, v, s0, after)


def _even_out_kernel(y_ref, rkv_ref, g_ref, lng_ref, lnb_ref, ones_ref, ol_ref, wuv_ref, woa_ref,
                     wob_ref, x_ref, o_ref):
    ones = ones_ref[...]
    y = y_ref[...]
    inv = 1.0 / RWKV_HEAD
    mean = _dot_exact_rhs(y, ones) * inv
    dlt = y - mean
    var = _dot_exact_rhs(dlt * dlt, ones) * inv
    yn = dlt * lax.rsqrt(var + RWKV_LN_EPS) * lng_ref[...] + lnb_ref[...] + rkv_ref[...]
    ob = (yn * g_ref[...]).astype(BF16)
    pair = 2 * MLA_KV_RANK
    oa = jnp.concatenate(
        [_dot(ol_ref[:, p * pair:(p + 1) * pair], wuv_ref[p]) for p in range(MLA_HEADS // 2)], axis=1)
    o_ref[...] = x_ref[...] + _dot(oa, woa_ref[...]) + _dot(ob, wob_ref[...])


def _even_out(y, rkv, g, o_lat, x, w, tm):
    n = x.shape[0]
    d = RWKV_DIM
    hq = MLA_HEADS * MLA_KV_RANK
    return pl.pallas_call(
        _even_out_kernel,
        grid=(n // tm,),
        in_specs=[_row_spec(tm, d), _row_spec(tm, d), _row_spec(tm, d), _const_spec((1, d)),
                  _const_spec((1, d)), _const_spec((d, d)), _row_spec(tm, hq),
                  _const_spec((MLA_HEADS // 2, 2 * MLA_KV_RANK, 2 * MLA_V)),
                  _const_spec((MLA_HEADS * MLA_V, D_MODEL)), _const_spec((d, D_MODEL)),
                  _row_spec(tm, D_MODEL)],
        out_specs=_row_spec(tm, D_MODEL),
        out_shape=jax.ShapeDtypeStruct((n, D_MODEL), F32),
        compiler_params=_cparams(("parallel",)),
    )(y, rkv, g, w["ln_g"], w["ln_b"], w["ones_bd"], o_lat, w["w_uv_bd"], w["w_out_a"],
      w["w_out_b"], x)


FFN_TF = 1408


def _ffn_kernel(x_ref, g_ref, wg_ref, wu_ref, wd_ref, o_ref, xn_sc, acc_sc):
    f = pl.program_id(1)

    @pl.when(f == 0)
    def _():
        xn_sc[...] = _rms(x_ref[...], g_ref[...]).astype(BF16)
        acc_sc[...] = jnp.zeros(acc_sc.shape, F32)

    xn = xn_sc[...]
    gate = _dot(xn, wg_ref[...])
    up = _dot(xn, wu_ref[...])
    acc_sc[...] += _dot(gate * _sigmoid(gate) * up, wd_ref[...])

    @pl.when(f == pl.num_programs(1) - 1)
    def _():
        o_ref[...] = x_ref[...] + acc_sc[...]


def _ffn(x, g, w_gu, w_down, tm):
    n = x.shape[0]
    nf = D_FF // FFN_TF
    return pl.pallas_call(
        _ffn_kernel,
        grid=(n // tm, nf),
        in_specs=[pl.BlockSpec((tm, D_MODEL), lambda i, f: (i, 0)),
                  pl.BlockSpec((1, D_MODEL), lambda i, f: (0, 0)),
                  pl.BlockSpec((D_MODEL, FFN_TF), lambda i, f: (0, f)),
                  pl.BlockSpec((D_MODEL, FFN_TF), lambda i, f: (0, nf + f)),
                  pl.BlockSpec((FFN_TF, D_MODEL), lambda i, f: (f, 0))],
        out_specs=pl.BlockSpec((tm, D_MODEL), lambda i, f: (i, 0)),
        out_shape=jax.ShapeDtypeStruct((n, D_MODEL), F32),
        scratch_shapes=[pltpu.VMEM((tm, D_MODEL), BF16), pltpu.VMEM((tm, D_MODEL), F32)],
        compiler_params=_cparams(("parallel", "arbitrary")),
    )(x, g, w_gu, w_gu, w_down)


def _odd_in_kernel(x_ref, g_ref, wq_ref, wk_ref, wv_ref, wg_ref, wxa_ref, a2_ref, ab_ref,
                   q_ref, k_ref, v_ref, gate_ref, la_ref):
    xn = _rms(x_ref[...], g_ref[...]).astype(BF16)
    q_ref[...] = _dot(xn, wq_ref[...]) * (GLA_DK ** -0.5)
    k_ref[...] = _dot(xn, wk_ref[...])
    v_ref[...] = _dot(xn, wv_ref[...])
    gate_ref[...] = _dot(xn, wg_ref[...])
    z = _dot(_dot(xn, wxa_ref[...]), a2_ref[...]) + ab_ref[...]
    la_ref[...] = -_softplus(-z) * (1.0 / GLA_GATE_NORM)


def _odd_in(x, w, tm):
    n = x.shape[0]
    return pl.pallas_call(
        _odd_in_kernel,
        grid=(n // tm,),
        in_specs=[_row_spec(tm, D_MODEL), _const_spec((1, D_MODEL)),
                  _const_spec((D_MODEL, GLA_KDIM)), _const_spec((D_MODEL, GLA_KDIM)),
                  _const_spec((D_MODEL, GLA_VDIM)), _const_spec((D_MODEL, GLA_VDIM)),
                  _const_spec((D_MODEL, LANES)), _const_spec((LANES, GLA_KDIM)),
                  _const_spec((1, GLA_KDIM))],
        out_specs=[_row_spec(tm, GLA_KDIM), _row_spec(tm, GLA_KDIM), _row_spec(tm, GLA_VDIM),
                   _row_spec(tm, GLA_VDIM), _row_spec(tm, GLA_KDIM)],
        out_shape=[jax.ShapeDtypeStruct((n, GLA_KDIM), F32), jax.ShapeDtypeStruct((n, GLA_KDIM), F32),
                   jax.ShapeDtypeStruct((n, GLA_VDIM), F32), jax.ShapeDtypeStruct((n, GLA_VDIM), F32),
                   jax.ShapeDtypeStruct((n, GLA_KDIM), F32)],
        compiler_params=_cparams(("parallel",)),
    )(x, w["norm_mix"], w["w_q"], w["w_k"], w["w_v"], w["w_g"], w["w_xa"], w["a2p"], w["ab"])


def _gla_kernel(q_ref, k_ref, v_ref, la_ref, s0_ref, o_ref, st_ref):
    c = GLA_CHUNK

    @pl.when(pl.program_id(1) == 0)
    def _():
        st_ref[...] = s0_ref[...]

    row = lax.broadcasted_iota(jnp.int32, (c, c), 0)
    col = lax.broadcasted_iota(jnp.int32, (c, c), 1)
    tri = row >= col
    tri_b = jnp.where(tri, 1.0, 0.0).astype(BF16)
    for h in range(GLA_HEADS):
        ks = slice(h * GLA_DK, (h + 1) * GLA_DK)
        vs = slice(h * GLA_DV, (h + 1) * GLA_DV)
        b = _dot_exact_lhs(tri_b, la_ref[:, ks])
        q = q_ref[:, ks]
        k = k_ref[:, ks]
        v = v_ref[:, vs]
        b_end = b[c - 1:c, :]
        qe = (q * jnp.exp(b)).astype(BF16)
        ke = (k * jnp.exp(-b)).astype(BF16)
        a_mat = jnp.where(tri, _dot_nt(qe, ke), 0.0)
        st = st_ref[0, h]
        o_ref[:, vs] = _dot_nt(qe, st) + _dot(a_mat, v)
        k_end = k * jnp.exp(b_end - b)
        st_ref[0, h] = st * jnp.exp(b_end) + _dot(v.T, k_end)


def _gla(q, k, v, la, s0t, batch, seq):
    nc = seq // GLA_CHUNK
    rspec = lambda width: pl.BlockSpec((GLA_CHUNK, width), lambda b, c: (b * nc + c, 0))
    sspec = pl.BlockSpec((1, GLA_HEADS, GLA_DV, GLA_DK), lambda b, c: (b, 0, 0, 0))
    return pl.pallas_call(
        _gla_kernel,
        grid=(batch, nc),
        in_specs=[rspec(GLA_KDIM), rspec(GLA_KDIM), rspec(GLA_VDIM), rspec(GLA_KDIM), sspec],
        out_specs=[rspec(GLA_VDIM), sspec],
        out_shape=[jax.ShapeDtypeStruct(v.shape, F32), jax.ShapeDtypeStruct(s0t.shape, F32)],
        compiler_params=_cparams(("parallel", "arbitrary")),
    )(q, k, v, la, s0t)


def _odd_out_kernel(o_ref, gate_ref, gn_ref, wo_ref, x_ref, y_ref):
    parts = []
    for h in range(GLA_HEADS):
        vs = slice(h * GLA_DV, (h + 1) * GLA_DV)
        parts.append(_rms(o_ref[:, vs], gn_ref[:, vs]))
    gate = gate_ref[...]
    on = jnp.concatenate(parts, axis=1) * (gate * _sigmoid(gate))
    y_ref[...] = x_ref[...] + _dot(on, wo_ref[...])


def _odd_out(o, gate, x, w, tm):
    n = x.shape[0]
    return pl.pallas_call(
        _odd_out_kernel,
        grid=(n // tm,),
        in_specs=[_row_spec(tm, GLA_VDIM), _row_spec(tm, GLA_VDIM), _const_spec((1, GLA_VDIM)),
                  _const_spec((GLA_VDIM, D_MODEL)), _row_spec(tm, D_MODEL)],
        out_specs=_row_spec(tm, D_MODEL),
        out_shape=jax.ShapeDtypeStruct((n, D_MODEL), F32),
        compiler_params=_cparams(("parallel",)),
    )(o, gate, w["gla_norm"], w["w_out"], x)


def _router_kernel(x_ref, g_ref, wr_ref, *rest):
    xn_ref, idx_ref, gate_ref = rest[-3:]
    xn = _rms(x_ref[...], g_ref[...])
    half = D_MODEL // 2
    xn_ref[0] = xn[:, :half]
    xn_ref[1] = xn[:, half:]
    logits = _dot_f32ish(xn, wr_ref[...])
    lane = lax.broadcasted_iota(jnp.int32, logits.shape, 1)
    logits = jnp.where(lane < N_EXPERTS, logits, NEG_BIG)
    m1 = jnp.max(logits, axis=-1, keepdims=True)
    i1 = jnp.min(jnp.where(logits == m1, lane, LANES), axis=-1, keepdims=True)
    rest = jnp.where(lane == i1, NEG_BIG, logits)
    m2 = jnp.max(rest, axis=-1, keepdims=True)
    i2 = jnp.min(jnp.where(rest == m2, lane, LANES), axis=-1, keepdims=True)
    e2 = jnp.exp(m2 - m1)
    g1 = 1.0 / (1.0 + e2)
    g2 = e2 / (1.0 + e2)
    idx_ref[...] = jnp.where(lane == 0, i1, jnp.where(lane == 1, i2, 0))
    gate_ref[...] = jnp.where(lane == 0, g1, jnp.where(lane == 1, g2, 0.0))


def _router(x, g, wr, tm, n_total, row0, prev=None):
    n = x.shape[0]
    half = D_MODEL // 2
    blk0 = row0 // tm
    prev = () if prev is None else tuple(prev)
    return pl.pallas_call(
        _router_kernel,
        grid=(n // tm,),
        in_specs=[_row_spec(tm, D_MODEL), _const_spec((1, D_MODEL)), _const_spec((D_MODEL, LANES))]
        + [pl.BlockSpec(memory_space=pl.ANY)] * len(prev),
        out_specs=[pl.BlockSpec((2, tm, half), lambda i: (0, blk0 + i, 0)),
                   pl.BlockSpec((tm, LANES), lambda i: (blk0 + i, 0)),
                   pl.BlockSpec((tm, LANES), lambda i: (blk0 + i, 0))],
        out_shape=[jax.ShapeDtypeStruct((2, n_total, half), F32),
                   jax.ShapeDtypeStruct((n_total, LANES), jnp.int32),
                   jax.ShapeDtypeStruct((n_total, LANES), F32)],
        input_output_aliases={3 + k: k for k in range(len(prev))},
        compiler_params=_cparams(("parallel",)),
    )(x, g, wr, *prev)


MOE_TF = 1792
MOE_TMC = 256


def _route(top_i, tm, tmc):
    n = top_i.shape[0]
    slots = 2 * n
    n_tiles = -(-(slots + N_EXPERTS * (tm - 1)) // tm)
    win = tmc + 8
    e_flat = top_i.reshape(-1)
    onehot = (e_flat[:, None] == jnp.arange(N_EXPERTS, dtype=jnp.int32)[None, :]).astype(jnp.int32)
    csum = jnp.cumsum(onehot, axis=0)
    rank = jnp.sum(onehot * csum, axis=1) - 1
    counts = csum[-1]
    padded = ((counts + tm - 1) // tm) * tm
    ends = jnp.cumsum(padded)
    starts = ends - padded
    dest = (jnp.sum(onehot * starts[None, :], axis=1) + rank).astype(jnp.int32)
    tile_start = jnp.arange(n_tiles, dtype=jnp.int32) * tm
    tile_expert = jnp.minimum(jnp.sum((tile_start[:, None] >= ends[None, :]).astype(jnp.int32), axis=1),
                              N_EXPERTS - 1).astype(jnp.int32)
    tile_valid = (tile_start < ends[-1]).astype(jnp.int32)
    src = jnp.zeros((n_tiles * tm,), jnp.int32).at[dest].set(jnp.arange(slots, dtype=jnp.int32) // 2)
    before = jnp.concatenate([jnp.zeros((1, N_EXPERTS), jnp.int32), csum[2 * tmc - 1:-1:2 * tmc]], axis=0)
    wstart = jnp.clip(((starts[None, :] + before) // 8) * 8, 0, n_tiles * tm - win).astype(jnp.int32)
    ws_slot = jnp.sum(onehot * jnp.repeat(wstart, 2 * tmc, axis=0), axis=1)
    local = (e_flat * win + dest - ws_slot).astype(jnp.int32)
    return local, wstart.reshape(-1), src, tile_expert, tile_valid


def _moe_gather_kernel(src_ref, x_ref, o_ref, *, tg):
    base = pl.program_id(1) * tg

    def body(r, carry):
        o_ref[pl.ds(r, 1), :] = x_ref[pl.ds(src_ref[base + r], 1), :]
        return carry

    lax.fori_loop(0, tg, body, 0, unroll=8)


def _moe_gather(src, xn2, tg):
    rows = src.shape[0]
    _, n, half = xn2.shape
    grid_spec = pltpu.PrefetchScalarGridSpec(
        num_scalar_prefetch=1,
        grid=(2, rows // tg),
        in_specs=[pl.BlockSpec((None, n, half), lambda h, i, s: (h, 0, 0),
                               pipeline_mode=pl.Buffered(1))],
        out_specs=pl.BlockSpec((tg, half), lambda h, i, s: (i, h)),
    )
    return pl.pallas_call(
        functools.partial(_moe_gather_kernel, tg=tg),
        grid_spec=grid_spec,
        out_shape=jax.ShapeDtypeStruct((rows, 2 * half), F32),
        compiler_params=_cparams(("arbitrary", "arbitrary")),
    )(src, xn2)


def _moe_up_kernel(te_ref, tv_ref, xs_ref, wg_ref, wu_ref, h_ref):
    @pl.when(tv_ref[pl.program_id(1)] != 0)
    def _():
        xs = xs_ref[...].astype(BF16)
        gate = _dot(xs, wg_ref[...])
        up = _dot(xs, wu_ref[...])
        h_ref[...] = (gate * _sigmoid(gate) * up).astype(BF16)


def _moe_up(te, tv, xs, w_gu, layer, tm):
    rows = xs.shape[0]
    nf = D_FF_EXPERT // MOE_TF
    wspec = lambda off: pl.BlockSpec((None, None, D_MODEL, MOE_TF),
                                     lambda f, t, te, tv: (layer, te[t], 0, off + f))
    grid_spec = pltpu.PrefetchScalarGridSpec(
        num_scalar_prefetch=2,
        grid=(nf, rows // tm),
        in_specs=[pl.BlockSpec((tm, D_MODEL), lambda f, t, te, tv: (t, 0)), wspec(0), wspec(nf)],
        out_specs=pl.BlockSpec((tm, MOE_TF), lambda f, t, te, tv: (t, f)),
    )
    return pl.pallas_call(
        _moe_up_kernel,
        grid_spec=grid_spec,
        out_shape=jax.ShapeDtypeStruct((rows, D_FF_EXPERT), BF16),
        compiler_params=_cparams(("arbitrary", "arbitrary")),
    )(te, tv, xs, w_gu, w_gu)


def _moe_down_kernel(te_ref, tv_ref, h_ref, wd_ref, y_ref):
    @pl.when(tv_ref[pl.program_id(0)] != 0)
    def _():
        y_ref[...] = _dot(h_ref[...], wd_ref[...])


def _moe_down(te, tv, h, w_down, layer, tm):
    rows = h.shape[0]
    grid_spec = pltpu.PrefetchScalarGridSpec(
        num_scalar_prefetch=2,
        grid=(rows // tm,),
        in_specs=[pl.BlockSpec((tm, D_FF_EXPERT), lambda t, te, tv: (t, 0)),
                  pl.BlockSpec((None, None, D_FF_EXPERT, D_MODEL),
                               lambda t, te, tv: (layer, te[t], 0, 0))],
        out_specs=pl.BlockSpec((tm, D_MODEL), lambda t, te, tv: (t, 0)),
    )
    return pl.pallas_call(
        _moe_down_kernel,
        grid_spec=grid_spec,
        out_shape=jax.ShapeDtypeStruct((rows, D_MODEL), F32),
        compiler_params=_cparams(("arbitrary",)),
    )(te, tv, h, w_down)


def _moe_combine_kernel(ws_ref, local_ref, *refs, tmc, win, tile0):
    win_refs = refs[:N_EXPERTS]
    g1_ref, g2_ref, x_ref, fn_ref, o_ref, buf = refs[N_EXPERTS:]
    for e in range(N_EXPERTS):
        buf[e * win:(e + 1) * win, :] = win_refs[e][...]
    base = 2 * (tile0 + pl.program_id(0)) * tmc

    def body(r, carry):
        row = pl.ds(r, 1)
        y1 = buf[pl.ds(local_ref[base + 2 * r], 1), :]
        y2 = buf[pl.ds(local_ref[base + 2 * r + 1], 1), :]
        g1 = g1_ref[row, :]
        g2 = g2_ref[row, :]
        parts = []
        for c in range(D_MODEL // LANES):
            cs = slice(c * LANES, (c + 1) * LANES)
            parts.append(g1 * y1[:, cs] + g2 * y2[:, cs])
        o_ref[row, :] = x_ref[row, :] + jnp.concatenate(parts, axis=1)
        return carry

    lax.fori_loop(0, tmc, body, 0, unroll=4)
    o_ref[...] = _rms(o_ref[...], fn_ref[...])


def _moe_combine(local, wstart, ys, g1b, g2b, x, fn, tmc, row0):
    n = x.shape[0]
    win = tmc + 8
    tile0 = row0 // tmc

    def win_spec(e):
        return pl.BlockSpec(
            (pl.Element(win), pl.Element(D_MODEL)),
            lambda i, ws, lo: (pl.multiple_of(ws[(tile0 + i) * N_EXPERTS + e], 8), 0))

    grid_spec = pltpu.PrefetchScalarGridSpec(
        num_scalar_prefetch=2,
        grid=(n // tmc,),
        in_specs=[win_spec(e) for e in range(N_EXPERTS)]
        + [pl.BlockSpec((tmc, LANES), lambda i, ws, lo: (tile0 + i, 0)),
           pl.BlockSpec((tmc, LANES), lambda i, ws, lo: (tile0 + i, 0)),
           pl.BlockSpec((tmc, D_MODEL), lambda i, ws, lo: (i, 0)),
           pl.BlockSpec((1, D_MODEL), lambda i, ws, lo: (0, 0))],
        out_specs=pl.BlockSpec((tmc, D_MODEL), lambda i, ws, lo: (i, 0)),
        scratch_shapes=[pltpu.VMEM((N_EXPERTS * win, D_MODEL), F32)],
    )
    return pl.pallas_call(
        functools.partial(_moe_combine_kernel, tmc=tmc, win=win, tile0=tile0),
        grid_spec=grid_spec,
        out_shape=jax.ShapeDtypeStruct((n, D_MODEL), F32),
        compiler_params=_cparams(("arbitrary",)),
    )(wstart, local, *([ys] * N_EXPERTS), g1b, g2b, x, fn)


def _scan_vec_layout(xs, batch, seq):
    nb = batch * RWKV_HEADS // SCAN_PAIRS
    x = xs.reshape(5, batch, seq, RWKV_HEADS, 2, SCAN_KH).transpose(0, 2, 5, 4, 1, 3)
    x = x.reshape(5, seq, SCAN_KH, 2, nb, SCAN_PAIRS).transpose(0, 4, 1, 2, 3, 5)
    return x.reshape(5, nb, seq, SCAN_KH, LANES)


def _scan_val_layout(v, batch, seq):
    nb = batch * RWKV_HEADS // SCAN_PAIRS
    v4 = v.reshape(batch, seq, RWKV_HEADS, RWKV_HEAD).transpose(1, 3, 0, 2)
    return v4.reshape(seq, RWKV_HEAD, nb, SCAN_PAIRS).transpose(2, 0, 1, 3)


def _scan_val_unlayout(y, batch, seq):
    v4 = y.transpose(1, 2, 0, 3).reshape(seq, RWKV_HEAD, batch, RWKV_HEADS)
    return v4.transpose(2, 0, 3, 1).reshape(batch * seq, RWKV_DIM)


def _scan_state_layout(s, batch):
    nb = batch * RWKV_HEADS // SCAN_PAIRS
    s6 = s.reshape(batch, RWKV_HEADS, RWKV_HEAD, 2, SCAN_KH).transpose(4, 2, 3, 0, 1)
    s6 = s6.reshape(SCAN_KH, RWKV_HEAD, 2, nb, SCAN_PAIRS).transpose(3, 0, 1, 2, 4)
    return s6.reshape(nb, SCAN_KH, RWKV_HEAD, LANES)


def _scan_state_unlayout(arr, batch):
    nb = arr.shape[0]
    s = arr.reshape(nb, SCAN_KH, RWKV_HEAD, 2, SCAN_PAIRS).transpose(0, 4, 2, 3, 1)
    return s.reshape(batch, RWKV_HEADS, RWKV_HEAD, RWKV_HEAD)


def _swap_halves(w):
    half = w.shape[-1] // 2
    return jnp.concatenate([w[..., half:], w[..., :half]], axis=-1)


def _prep_even(i, norm_mix, norm_ffn, w_in, q_norm, kv_norm, w_uq, w_uk, w_uv, mu, w0, w2, a0, a2,
               g2, k_k, k_a, r_k, ln_g, ln_b, w_out, ffn_gu, ffn_down):
    w = {}
    row = lambda v: v[i].reshape(1, -1)
    w_in = w_in[i]
    w["norm_mix"] = row(norm_mix)
    w["norm_ffn"] = row(norm_ffn)
    w["w_q"] = w_in[:, :MLA_Q_RANK].astype(BF16)
    w_kv = w_in[:, MLA_Q_RANK:MLA_Q_RANK + MLA_LAT]
    w["w_ckv"] = w_kv[:, :MLA_KV_RANK].astype(BF16)
    lane_pad = lambda m: jnp.pad(m, [(0, 0)] * (m.ndim - 1) + [(0, LANES - m.shape[-1])])
    w["w_pe_a"] = lane_pad(w_kv[:, MLA_KV_RANK:]).astype(BF16)
    w["w_pe_b"] = lane_pad(_swap_halves(w_kv[:, MLA_KV_RANK:])).astype(BF16)
    w["w_rw"] = w_in[:, MLA_Q_RANK + MLA_LAT:].astype(BF16)
    w["q_norm"] = row(q_norm)
    w["kv_norm"] = row(kv_norm)
    uq = w_uq[i].reshape(MLA_Q_RANK, MLA_HEADS, MLA_NOPE + MLA_ROPE)
    uq_pe = uq[:, :, MLA_NOPE:]
    w["w_qpe_a"] = lane_pad(uq_pe).reshape(MLA_Q_RANK, -1).astype(BF16)
    w["w_qpe_b"] = lane_pad(_swap_halves(uq_pe)).reshape(MLA_Q_RANK, -1).astype(BF16)
    w["w_qlat"] = _fold_qlat(uq[:, :, :MLA_NOPE].transpose(1, 0, 2), w_uk[i].transpose(1, 0, 2))
    uv = w_uv[i].transpose(1, 0, 2).reshape(MLA_HEADS // 2, 2, MLA_KV_RANK, MLA_V)
    zero = jnp.zeros_like(uv[:, 0])
    w["w_uv_bd"] = jnp.concatenate(
        [jnp.concatenate([uv[:, 0], zero], axis=-1), jnp.concatenate([zero, uv[:, 1]], axis=-1)],
        axis=1).astype(BF16)
    w["mu"] = row(mu)
    w["w0"] = row(w0)
    pad = lambda m, before: jnp.pad(m, ((before, LANES - before - m.shape[0]), (0, 0))).astype(BF16)
    w["w2p"] = pad(w2[i], 0)
    w["a2p"] = pad(a2[i], RWKV_W_LORA)
    w["a0"] = row(a0)
    w["g2"] = g2[i].astype(BF16)
    w["k_k"] = row(k_k)
    w["k_a"] = row(k_a)
    w["r_k"] = row(r_k)
    w["ln_g"] = row(ln_g)
    w["ln_b"] = row(ln_b)
    head = jnp.arange(RWKV_DIM) // RWKV_HEAD
    w["ones_bd"] = (head[:, None] == head[None, :]).astype(BF16)
    w["w_out_a"] = w_out[i][:MLA_HEADS * MLA_V].astype(BF16)
    w["w_out_b"] = w_out[i][MLA_HEADS * MLA_V:].astype(BF16)
    w["ffn_gu"] = ffn_gu[i].astype(BF16)
    w["ffn_down"] = ffn_down[i].astype(BF16)
    return w


def _prep_odd(i, norm_mix, norm_ffn, w_in, a2, ab, gla_norm, w_out, router, moe_gu, moe_down):
    w = {}
    row = lambda v: v[i].reshape(1, -1)
    w_in = w_in[i]
    w["norm_mix"] = row(norm_mix)
    w["norm_ffn"] = row(norm_ffn)
    w["w_q"] = w_in[:, :GLA_KDIM].astype(BF16)
    w["w_k"] = w_in[:, GLA_KDIM:2 * GLA_KDIM].astype(BF16)
    w["w_v"] = w_in[:, 2 * GLA_KDIM:2 * GLA_KDIM + GLA_VDIM].astype(BF16)
    w["w_g"] = w_in[:, 2 * GLA_KDIM + GLA_VDIM:2 * GLA_KDIM + 2 * GLA_VDIM].astype(BF16)
    w["w_xa"] = jnp.pad(w_in[:, 2 * GLA_KDIM + 2 * GLA_VDIM:],
                        ((0, 0), (0, LANES - GLA_GATE_RANK))).astype(BF16)
    w["a2p"] = jnp.pad(a2[i], ((0, LANES - GLA_GATE_RANK), (0, 0))).astype(BF16)
    w["ab"] = row(ab)
    w["gla_norm"] = row(gla_norm)
    w["w_out"] = w_out[i].astype(BF16)
    w["router"] = jnp.pad(router[i], ((0, 0), (0, LANES - N_EXPERTS)))
    w["layer"] = i
    w["moe_gu"] = moe_gu
    w["moe_down"] = moe_down
    return w


def _rope_tables(pos, reps):
    inv = ROPE_THETA ** (-jnp.arange(0, MLA_ROPE, 2, dtype=F32) / MLA_ROPE)
    ang = pos.astype(F32)[:, None] * inv[None, :]
    cos, sin = jnp.cos(ang), jnp.sin(ang)
    pad = ((0, 0), (0, LANES - MLA_ROPE))
    cs = jnp.tile(jnp.pad(jnp.concatenate([cos, cos], axis=-1), pad), (reps, 1))
    sn = jnp.tile(jnp.pad(jnp.concatenate([-sin, sin], axis=-1), pad), (reps, 1))
    return {"cs": cs, "sn": sn, "cs8": jnp.tile(cs, (1, MLA_HEADS)), "sn8": jnp.tile(sn, (1, MLA_HEADS))}


def _even_layer(x, batch, seq, tabs, state, shift0, past, w, tm, tc):
    n = batch * seq
    lat, lat_b, q_lat, q_pe, rw = _even_in(x, w, tabs, tm)
    if past is None:
        o_lat = _mla_prompt(q_lat, q_pe, lat_b, batch, seq)
    else:
        cache, layer, page_table = past
        rows = seq * MLA_HEADS
        q_full = jnp.concatenate([q_lat.reshape(batch, rows, MLA_KV_RANK),
                                  q_pe.reshape(batch, rows, LANES)[:, :, :MLA_ROPE]], axis=-1)
        new_pad_t = jnp.pad(lat_b.reshape(batch, seq, MLA_LATB)[:, :, :MLA_LAT],
                            ((0, 0), (0, PAGE_SIZE - seq), (0, 0))).transpose(0, 2, 1)
        o_lat = _mla_decode(page_table, q_full, new_pad_t, cache.transpose(0, 1, 3, 2), layer)
        o_lat = o_lat.reshape(n, MLA_HEADS * MLA_KV_RANK)

    rw3 = rw.reshape(batch, seq, RWKV_PROJ)
    xs5, v, g, rkv = _rwkv_prep(rw, shift0, w, tm, seq)
    y_l, s_l = _rwkv_scan(_scan_vec_layout(xs5, batch, seq), _scan_val_layout(v, batch, seq),
                          _scan_state_layout(state, batch), o_lat, tc)
    y = _scan_val_unlayout(y_l, batch, seq)
    new_state = _scan_state_unlayout(s_l, batch)

    x = _even_out(y, rkv, g, o_lat, x, w, tm)
    x = _ffn(x, w["norm_ffn"], w["ffn_gu"], w["ffn_down"], tm)
    return x, lat.reshape(batch, seq, MLA_LAT), new_state, rw3[:, -1]


def _odd_mixer_layer(x, batch, seq, state, w, tm):
    q, k, v, gate, la = _odd_in(x, w, tm)
    seq_p = -(-seq // GLA_CHUNK) * GLA_CHUNK
    if seq_p != seq:
        padr = lambda t: jnp.pad(t.reshape(batch, seq, -1), ((0, 0), (0, seq_p - seq), (0, 0))
                                 ).reshape(batch * seq_p, -1)
        qp, kp, vp, lap = padr(q), padr(k), padr(v), padr(la)
    else:
        qp, kp, vp, lap = q, k, v, la
    o, st = _gla(qp, kp, vp, lap, state.transpose(0, 1, 3, 2), batch, seq_p)
    if seq_p != seq:
        o = o.reshape(batch, seq_p, GLA_VDIM)[:, :seq].reshape(batch * seq, GLA_VDIM)
    return _odd_out(o, gate, x, w, tm), st.transpose(0, 1, 3, 2)


def _moe_all_groups(xs_groups, tms, w, final_norm, tm_moe):
    sizes = [x.shape[0] for x in xs_groups]
    n_total = sum(sizes)
    bufs, row0 = None, 0
    for x, tm in zip(xs_groups, tms):
        bufs = _router(x, w["norm_ffn"], w["router"], tm, n_total, row0, bufs)
        row0 += x.shape[0]
    xn2, idx, gates = bufs
    local, wstart, src, tile_expert, tile_valid = _route(idx[:, :2], tm_moe, MOE_TMC)
    rows = _moe_gather(src, xn2, tm_moe)
    h = _moe_up(tile_expert, tile_valid, rows, w["moe_gu"], w["layer"], tm_moe)
    ys = _moe_down(tile_expert, tile_valid, h, w["moe_down"], w["layer"], tm_moe)
    g1b = jnp.broadcast_to(gates[:, 0:1], (n_total, LANES))
    g2b = jnp.broadcast_to(gates[:, 1:2], (n_total, LANES))
    outs, row0 = [], 0
    for x in xs_groups:
        outs.append(_moe_combine(local, wstart, ys, g1b, g2b, x, final_norm, MOE_TMC, row0))
        row0 += x.shape[0]
    return outs


def kernel(x_prompt, x_sample, cache_mla, state_rwkv, state_rwkv_shift, state_gla, page_table, norm_mix_even, norm_ffn_even, w_in_even, mla_q_norm, mla_kv_norm, mla_w_uq, mla_w_uk, mla_w_uv, rwkv_mu, rwkv_w0, rwkv_w2, rwkv_a0, rwkv_a2, rwkv_g2, rwkv_k_k, rwkv_k_a, rwkv_r_k, rwkv_ln_g, rwkv_ln_b, w_out_even, ffn_w_gu_even, ffn_w_down_even, norm_mix_odd, norm_ffn_odd, w_in_odd, gla_a2, gla_ab, gla_norm, w_out_odd, moe_router, moe_w_gu, moe_w_down, final_norm):
    bp, tp, _ = x_prompt.shape
    bs, ts, _ = x_sample.shape
    past_len = page_table.shape[1] * PAGE_SIZE
    tm_p, tm_s = 512, bs * ts
    we = _prep_even(0, norm_mix_even, norm_ffn_even, w_in_even, mla_q_norm, mla_kv_norm, mla_w_uq,
                    mla_w_uk, mla_w_uv, rwkv_mu, rwkv_w0, rwkv_w2, rwkv_a0, rwkv_a2, rwkv_g2,
                    rwkv_k_k, rwkv_k_a, rwkv_r_k, rwkv_ln_g, rwkv_ln_b, w_out_even, ffn_w_gu_even,
                    ffn_w_down_even)
    wo = _prep_odd(0, norm_mix_odd, norm_ffn_odd, w_in_odd, gla_a2, gla_ab, gla_norm, w_out_odd,
                   moe_router, moe_w_gu, moe_w_down)
    fn = final_norm.reshape(1, -1)
    tabs_p = _rope_tables(jnp.arange(tp), 1)
    tabs_s = _rope_tables(past_len + jnp.arange(ts), bs)

    hp = x_prompt.reshape(bp * tp, D_MODEL)
    hs = x_sample.reshape(bs * ts, D_MODEL)
    zeros_state = jnp.zeros((bp, RWKV_HEADS, RWKV_HEAD, RWKV_HEAD), F32)
    zeros_shift = jnp.zeros((bp, RWKV_PROJ), F32)
    hp, lat_p, rs_p, sh_p = _even_layer(hp, bp, tp, tabs_p, zeros_state, zeros_shift, None, we,
                                        tm_p, 64)
    hs, lat_s, rs_s, sh_s = _even_layer(hs, bs, ts, tabs_s, state_rwkv[0], state_rwkv_shift[0],
                                        (cache_mla, 0, page_table), we, tm_s, ts)
    zeros_gla = jnp.zeros((bp, GLA_HEADS, GLA_DK, GLA_DV), F32)
    hp, gs_p = _odd_mixer_layer(hp, bp, tp, zeros_gla, wo, tm_p)
    hs, gs_s = _odd_mixer_layer(hs, bs, ts, state_gla[0], wo, tm_s)
    yp, ys = _moe_all_groups([hp, hs], [tm_p, tm_s], wo, fn, 512)
    return (yp.reshape(bp, tp, D_MODEL), ys.reshape(bs, ts, D_MODEL), lat_p[None], lat_s[None],
            rs_p[None], rs_s[None], sh_p[None], sh_s[None], gs_p[None], gs_s[None])
```

```python
import functools

import jax
import jax.numpy as jnp
from jax import lax
from jax.experimental import pallas as pl
from jax.experimental.pallas import tpu as pltpu

F32 = jnp.float32
BF16 = jnp.bfloat16

D_MODEL = 1024
PAGE_SIZE = 128
NORM_EPS = 1e-6

MLA_HEADS = 8
MLA_NOPE = 64
MLA_ROPE = 32
MLA_V = 64
MLA_Q_RANK = 384
MLA_KV_RANK = 256
MLA_LAT = MLA_KV_RANK + MLA_ROPE
MLA_LATB = MLA_KV_RANK + 128
MLA_SCALE = (MLA_NOPE + MLA_ROPE) ** -0.5
ROPE_THETA = 10000.0

RWKV_HEADS = 8
RWKV_HEAD = 64
RWKV_DIM = RWKV_HEADS * RWKV_HEAD
RWKV_W_LORA = 64
RWKV_A_LORA = 64
RWKV_G_LORA = 128
RWKV_PROJ = 3 * RWKV_DIM + RWKV_W_LORA + RWKV_A_LORA + RWKV_G_LORA
RWKV_LN_EPS = 64e-5

GLA_HEADS = 4
GLA_DK = 128
GLA_DV = 256
GLA_KDIM = GLA_HEADS * GLA_DK
GLA_VDIM = GLA_HEADS * GLA_DV
GLA_GATE_RANK = 16
GLA_GATE_NORM = 16.0
GLA_CHUNK = 128

D_FF = 2816
N_EXPERTS = 8
D_FF_EXPERT = 3584

LANES = 128
VMEM_LIMIT = 56 * 1024 * 1024
NEG_BIG = -1e30
LOG2_E = 1.4426950408889634
Q_PRESCALE = MLA_SCALE * LOG2_E


def _cparams(sem):
    return pltpu.CompilerParams(dimension_semantics=sem, vmem_limit_bytes=VMEM_LIMIT)


def _const_spec(shape):
    nd = len(shape)
    return pl.BlockSpec(shape, lambda *_: (0,) * nd)


def _row_spec(tm, width):
    return pl.BlockSpec((tm, width), lambda i: (i, 0))


def _dot(a, b):
    return jnp.dot(a.astype(BF16), b.astype(BF16), preferred_element_type=F32)


def _dot_nt(a, b):
    return lax.dot_general(a.astype(BF16), b.astype(BF16), (((1,), (1,)), ((), ())),
                           preferred_element_type=F32)


def _split2(x):
    hi = x.astype(BF16)
    lo = (x - hi.astype(F32)).astype(BF16)
    return hi, lo


def _split3(x):
    hi = x.astype(BF16)
    r1 = x - hi.astype(F32)
    mid = r1.astype(BF16)
    lo = (r1 - mid.astype(F32)).astype(BF16)
    return hi, mid, lo


def _dot_exact_rhs(x, e):
    hi, mid, lo = _split3(x)
    return (jnp.dot(hi, e, preferred_element_type=F32) + jnp.dot(mid, e, preferred_element_type=F32)
            + jnp.dot(lo, e, preferred_element_type=F32))


def _dot_exact_lhs(e, x):
    hi, mid, lo = _split3(x)
    return (jnp.dot(e, hi, preferred_element_type=F32) + jnp.dot(e, mid, preferred_element_type=F32)
            + jnp.dot(e, lo, preferred_element_type=F32))


def _dot_f32ish(a, b):
    ah, al = _split2(a)
    bh, bl = _split2(b)
    return (jnp.dot(ah, bh, preferred_element_type=F32) + jnp.dot(ah, bl, preferred_element_type=F32)
            + jnp.dot(al, bh, preferred_element_type=F32))


def _lane_tile(x, width):
    return x if width == LANES else jnp.concatenate([x] * (width // LANES), axis=1)


def _rms(x, g, eps=NORM_EPS):
    return x * lax.rsqrt(jnp.mean(x * x, axis=-1, keepdims=True) + eps) * g


def _sigmoid(x):
    return 1.0 / (1.0 + jnp.exp(-x))


def _softplus(x):
    return jnp.maximum(x, 0.0) + jnp.log(1.0 + jnp.exp(-jnp.abs(x)))


def _fold_qlat_kernel(uq_ref, uk_ref, o_ref):
    a = uq_ref[...]
    b = uk_ref[...]
    ah, al = _split2(a)
    bh, bl = _split2(b)
    dn = (((1,), (1,)), ((), ()))
    o = (lax.dot_general(ah, bh, dn, preferred_element_type=F32)
         + lax.dot_general(ah, bl, dn, preferred_element_type=F32)
         + lax.dot_general(al, bh, dn, preferred_element_type=F32))
    o_ref[...] = o.astype(BF16)


def _fold_qlat(uq_nope, uk):
    return pl.pallas_call(
        _fold_qlat_kernel,
        grid=(MLA_HEADS,),
        in_specs=[pl.BlockSpec((None, MLA_Q_RANK, MLA_NOPE), lambda h: (h, 0, 0)),
                  pl.BlockSpec((None, MLA_KV_RANK, MLA_NOPE), lambda h: (h, 0, 0))],
        out_specs=pl.BlockSpec((MLA_Q_RANK, MLA_KV_RANK), lambda h: (0, h)),
        out_shape=jax.ShapeDtypeStruct((MLA_Q_RANK, MLA_HEADS * MLA_KV_RANK), BF16),
        compiler_params=_cparams(("arbitrary",)),
    )(uq_nope, uk)


def _even_in_kernel(x_ref, g_ref, wq_ref, wckv_ref, wpa_ref, wpb_ref, wrw_ref, qn_ref, kvn_ref,
                    cs_ref, sn_ref, wql_ref, wqa_ref, wqb_ref, cs8_ref, sn8_ref,
                    lat_ref, latb_ref, ql_ref, qpe_ref, rw_ref):
    xn = _rms(x_ref[...], g_ref[...]).astype(BF16)
    cq = _rms(_dot(xn, wq_ref[...]), qn_ref[...]).astype(BF16)
    ql_ref[...] = (_dot(cq, wql_ref[...]) * Q_PRESCALE).astype(BF16)
    qpe = _dot(cq, wqa_ref[...]) * cs8_ref[...] + _dot(cq, wqb_ref[...]) * sn8_ref[...]
    qpe_ref[...] = (qpe * Q_PRESCALE).astype(BF16)
    ckv = _rms(_dot(xn, wckv_ref[...]), kvn_ref[...])
    kpe = _dot(xn, wpa_ref[...]) * cs_ref[...] + _dot(xn, wpb_ref[...]) * sn_ref[...]
    lat_ref[:, :MLA_KV_RANK] = ckv
    lat_ref[:, MLA_KV_RANK:] = kpe[:, :MLA_ROPE]
    latb_ref[:, :MLA_KV_RANK] = ckv.astype(BF16)
    latb_ref[:, MLA_KV_RANK:] = kpe.astype(BF16)
    rw_ref[...] = _dot(xn, wrw_ref[...])


def _even_in(x, w, tabs, tm):
    n = x.shape[0]
    nt = tabs["cs"].shape[0] // tm
    tab = lambda width: pl.BlockSpec((tm, width), lambda i: (i % nt, 0))
    hq = MLA_HEADS * MLA_KV_RANK
    hr = MLA_HEADS * LANES
    return pl.pallas_call(
        _even_in_kernel,
        grid=(n // tm,),
        in_specs=[_row_spec(tm, D_MODEL), _const_spec((1, D_MODEL)),
                  _const_spec((D_MODEL, MLA_Q_RANK)), _const_spec((D_MODEL, MLA_KV_RANK)),
                  _const_spec((D_MODEL, LANES)), _const_spec((D_MODEL, LANES)),
                  _const_spec((D_MODEL, RWKV_PROJ)), _const_spec((1, MLA_Q_RANK)),
                  _const_spec((1, MLA_KV_RANK)), tab(LANES), tab(LANES),
                  _const_spec((MLA_Q_RANK, hq)), _const_spec((MLA_Q_RANK, hr)),
                  _const_spec((MLA_Q_RANK, hr)), tab(hr), tab(hr)],
        out_specs=[_row_spec(tm, MLA_LAT), _row_spec(tm, MLA_LATB), _row_spec(tm, hq),
                   _row_spec(tm, hr), _row_spec(tm, RWKV_PROJ)],
        out_shape=[jax.ShapeDtypeStruct((n, MLA_LAT), F32), jax.ShapeDtypeStruct((n, MLA_LATB), BF16),
                   jax.ShapeDtypeStruct((n, hq), BF16), jax.ShapeDtypeStruct((n, hr), BF16),
                   jax.ShapeDtypeStruct((n, RWKV_PROJ), F32)],
        compiler_params=_cparams(("parallel",)),
    )(x, w["norm_mix"], w["w_q"], w["w_ckv"], w["w_pe_a"], w["w_pe_b"], w["w_rw"], w["q_norm"],
      w["kv_norm"], tabs["cs"], tabs["sn"], w["w_qlat"], w["w_qpe_a"], w["w_qpe_b"],
      tabs["cs8"], tabs["sn8"])


ATT_TQ = 256


def _mla_prompt_kernel(qi_ref, kj_ref, ql_ref, qpe_ref, lat_ref, o_ref,
                       m_sc, l_sc, a_sc, acc_sc, s_sc, p_sc):
    step = pl.program_id(1)
    i = qi_ref[step]
    j = kj_ref[step]
    heads = range(MLA_HEADS)

    @pl.when(j == 0)
    def _():
        m_sc[...] = jnp.full(m_sc.shape, NEG_BIG, F32)
        l_sc[...] = jnp.zeros(l_sc.shape, F32)
        acc_sc[...] = jnp.zeros(acc_sc.shape, F32)

    def tile(masked):
        ckv = lat_ref[:, :MLA_KV_RANK]
        kpe = lat_ref[:, MLA_KV_RANK:]
        for h in heads:
            s_sc[h] = (_dot_nt(ql_ref[:, h * MLA_KV_RANK:(h + 1) * MLA_KV_RANK], ckv)
                       + _dot_nt(qpe_ref[:, h * LANES:(h + 1) * LANES], kpe))
        for h in heads:
            s = s_sc[h]
            if masked:
                tok = lax.broadcasted_iota(jnp.int32, s.shape, 0)
                key = lax.broadcasted_iota(jnp.int32, s.shape, 1)
                s = jnp.where(key <= tok, s, NEG_BIG)
            m_prev = m_sc[h]
            m_new = jnp.maximum(m_prev, jnp.max(s, axis=-1, keepdims=True))
            alpha = jnp.exp2(m_prev - m_new)
            p = jnp.exp2(s - _lane_tile(m_new, ATT_TQ))
            l_sc[h] = alpha * l_sc[h] + jnp.sum(p, axis=-1, keepdims=True)
            m_sc[h] = m_new
            a_sc[h] = alpha
            p_sc[h] = p.astype(BF16)
        for h in heads:
            acc_sc[h] = _lane_tile(a_sc[h], MLA_KV_RANK) * acc_sc[h] + _dot(p_sc[h], ckv)

    @pl.when(j < i)
    def _():
        tile(False)

    @pl.when(j == i)
    def _():
        tile(True)
        for h in heads:
            o_ref[:, h * MLA_KV_RANK:(h + 1) * MLA_KV_RANK] = (
                acc_sc[h] / _lane_tile(l_sc[h], MLA_KV_RANK)).astype(BF16)


def _mla_prompt(q_lat, q_pe, lat_b, batch, seq):
    nq = seq // ATT_TQ
    pairs = [(i, j) for i in range(nq) for j in range(i + 1)]
    qi = jnp.array([p[0] for p in pairs], jnp.int32)
    kj = jnp.array([p[1] for p in pairs], jnp.int32)
    hq = MLA_HEADS * MLA_KV_RANK
    grid_spec = pltpu.PrefetchScalarGridSpec(
        num_scalar_prefetch=2,
        grid=(batch, len(pairs)),
        in_specs=[pl.BlockSpec((ATT_TQ, hq), lambda b, s, qi, kj: (b * nq + qi[s], 0)),
                  pl.BlockSpec((ATT_TQ, MLA_HEADS * LANES), lambda b, s, qi, kj: (b * nq + qi[s], 0)),
                  pl.BlockSpec((ATT_TQ, MLA_LATB), lambda b, s, qi, kj: (b * nq + kj[s], 0))],
        out_specs=pl.BlockSpec((ATT_TQ, hq), lambda b, s, qi, kj: (b * nq + qi[s], 0)),
        scratch_shapes=[pltpu.VMEM((MLA_HEADS, ATT_TQ, LANES), F32),
                        pltpu.VMEM((MLA_HEADS, ATT_TQ, LANES), F32),
                        pltpu.VMEM((MLA_HEADS, ATT_TQ, LANES), F32),
                        pltpu.VMEM((MLA_HEADS, ATT_TQ, MLA_KV_RANK), F32),
                        pltpu.VMEM((MLA_HEADS, ATT_TQ, ATT_TQ), F32),
                        pltpu.VMEM((MLA_HEADS, ATT_TQ, ATT_TQ), BF16)],
    )
    return pl.pallas_call(
        _mla_prompt_kernel,
        grid_spec=grid_spec,
        out_shape=jax.ShapeDtypeStruct(q_lat.shape, BF16),
        compiler_params=_cparams(("parallel", "arbitrary")),
    )(qi, kj, q_lat, q_pe, lat_b)


PAGES_PER_STEP = 32
DECODE_GROUPS = 8


def _mla_decode_kernel(pt_ref, q_ref, new_ref, *rest):
    page_refs = rest[:PAGES_PER_STEP]
    o_ref, m_sc, l_sc, acc_sc = rest[PAGES_PER_STEP:]
    j = pl.program_id(1)
    q = q_ref[0]

    @pl.when(j == 0)
    def _():
        m_sc[...] = jnp.full(m_sc.shape, NEG_BIG, F32)
        l_sc[...] = jnp.zeros(l_sc.shape, F32)
        acc_sc[...] = jnp.zeros(acc_sc.shape, F32)

    def update(state, s, values_t):
        m_prev, l_prev, acc = state
        m_new = jnp.maximum(m_prev, jnp.max(s, axis=-1, keepdims=True))
        alpha = jnp.exp2(m_prev - m_new)
        p = jnp.exp2(s - _lane_tile(m_new, s.shape[1]))
        l_new = alpha * l_prev + jnp.sum(p, axis=-1, keepdims=True)
        return m_new, l_new, _lane_tile(alpha, MLA_KV_RANK) * acc + _dot_nt(p, values_t)

    group = PAGES_PER_STEP // DECODE_GROUPS
    keys = [jnp.concatenate([pr[...].astype(BF16) for pr in page_refs[g * group:(g + 1) * group]],
                            axis=1) for g in range(DECODE_GROUPS)]
    scores = [_dot(q, kt) for kt in keys]
    state = (m_sc[...], l_sc[...], acc_sc[...])
    for s, kt in zip(scores, keys):
        state = update(state, s, kt[:MLA_KV_RANK, :])
    m_sc[...], l_sc[...], acc_sc[...] = state

    @pl.when(j == pl.num_programs(1) - 1)
    def _():
        new_t = new_ref[0]
        sn = _dot(q, new_t)
        tok = lax.broadcasted_iota(jnp.int32, sn.shape, 0) >> 3
        key = lax.broadcasted_iota(jnp.int32, sn.shape, 1)
        sn = jnp.where(key <= tok, sn, NEG_BIG)
        _, l_fin, acc_fin = update(state, sn, new_t[:MLA_KV_RANK, :])
        o_ref[0] = (acc_fin / _lane_tile(l_fin, MLA_KV_RANK)).astype(BF16)


def _mla_decode(page_table, q_full, new_pad_t, cache_t, layer):
    db, n_pages = page_table.shape
    rows = q_full.shape[1]
    steps = n_pages // PAGES_PER_STEP

    def page_spec(p):
        return pl.BlockSpec((None, None, MLA_LAT, PAGE_SIZE),
                            lambda b, j, pt: (layer, pt[b, j * PAGES_PER_STEP + p], 0, 0))

    grid_spec = pltpu.PrefetchScalarGridSpec(
        num_scalar_prefetch=1,
        grid=(db, steps),
        in_specs=[pl.BlockSpec((1, rows, MLA_LAT), lambda b, j, pt: (b, 0, 0)),
                  pl.BlockSpec((1, MLA_LAT, PAGE_SIZE), lambda b, j, pt: (b, 0, 0))]
        + [page_spec(p) for p in range(PAGES_PER_STEP)],
        out_specs=pl.BlockSpec((1, rows, MLA_KV_RANK), lambda b, j, pt: (b, 0, 0)),
        scratch_shapes=[pltpu.VMEM((rows, LANES), F32), pltpu.VMEM((rows, LANES), F32),
                        pltpu.VMEM((rows, MLA_KV_RANK), F32)],
    )
    return pl.pallas_call(
        _mla_decode_kernel,
        grid_spec=grid_spec,
        out_shape=jax.ShapeDtypeStruct((db, rows, MLA_KV_RANK), BF16),
        compiler_params=_cparams(("parallel", "arbitrary")),
    )(page_table, q_full, new_pad_t, *([cache_t] * PAGES_PER_STEP))


def _rwkv_prep_kernel(rw_ref, before_ref, sh_ref, mu_ref, w0_ref, w2_ref, a0_ref, a2_ref, g2_ref,
                      kk_ref, ka_ref, rk_ref, ones_ref, xs_ref, v_ref, g_ref, rkv_ref, *, tm, seq):
    rw = rw_ref[...]
    rolled = pltpu.roll(rw, 1, axis=0)
    row = lax.broadcasted_iota(jnp.int32, rw.shape, 0)
    if seq >= tm:
        at_start = pl.program_id(0) % (seq // tm) == 0
        first = jnp.where(at_start, sh_ref[...], before_ref[7:8, :])
        prev = jnp.where(row == 0, first, rolled)
    else:
        prev = jnp.where((row & (seq - 1)) == 0, sh_ref[...], rolled)
    xs = rw + (prev - rw) * mu_ref[...]
    d = RWKV_DIM
    r = xs[:, :d]
    k = xs[:, d:2 * d]
    v = xs[:, 2 * d:3 * d]
    xwa = xs[:, 3 * d:3 * d + LANES]
    xg = xs[:, 3 * d + LANES:]
    ones = ones_ref[...]
    w_log = -_softplus(-(w0_ref[...] + _dot(jnp.tanh(xwa), w2_ref[...]))) - 0.5
    a = _sigmoid(a0_ref[...] + _dot(xwa, a2_ref[...]))
    g_ref[...] = _dot(_sigmoid(xg), g2_ref[...])
    kk = k * kk_ref[...]
    ss = _dot_exact_rhs(kk * kk, ones)
    kk = kk / jnp.maximum(jnp.sqrt(ss), 1e-12)
    k2 = k * (1.0 + (a - 1.0) * ka_ref[...])
    xs_ref[0] = -kk
    xs_ref[1] = jnp.exp(-jnp.exp(w_log))
    xs_ref[2] = kk * a
    xs_ref[3] = k2
    xs_ref[4] = r
    v_ref[...] = v
    rkv_ref[...] = _dot_exact_rhs(r * k2 * rk_ref[...], ones) * v


def _rwkv_prep(rw, shift0, w, tm, seq):
    n = rw.shape[0]
    d = RWKV_DIM
    vec = _const_spec((1, d))
    if seq >= tm:
        tiles = seq // tm
        sh = shift0.reshape(-1, 1, RWKV_PROJ)
        sh_spec = pl.BlockSpec((None, 1, RWKV_PROJ), lambda i: (i // tiles, 0, 0))
    else:
        sh = jnp.repeat(shift0, seq, axis=0)
        sh_spec = _row_spec(tm, RWKV_PROJ)
    before_spec = pl.BlockSpec((8, RWKV_PROJ), lambda i: (jnp.maximum(i * (tm // 8) - 1, 0), 0))
    return pl.pallas_call(
        functools.partial(_rwkv_prep_kernel, tm=tm, seq=seq),
        grid=(n // tm,),
        in_specs=[_row_spec(tm, RWKV_PROJ), before_spec, sh_spec, _const_spec((1, RWKV_PROJ)),
                  vec, _const_spec((LANES, d)), vec, _const_spec((LANES, d)),
                  _const_spec((RWKV_G_LORA, d)), vec, vec, vec, _const_spec((d, d))],
        out_specs=[pl.BlockSpec((5, tm, d), lambda i: (0, i, 0))] + [_row_spec(tm, d)] * 3,
        out_shape=[jax.ShapeDtypeStruct((5, n, d), F32)] + [jax.ShapeDtypeStruct((n, d), F32)] * 3,
        compiler_params=_cparams(("parallel",)),
    )(rw, rw, sh, w["mu"], w["w0"], w["w2p"], w["a0"], w["a2p"], w["g2"], w["k_k"], w["k_a"],
      w["r_k"], w["ones_bd"])


SCAN_KH = RWKV_HEAD // 2
SCAN_PAIRS = LANES // 2
SCAN_VR = RWKV_HEAD // 2


def _rwkv_scan_kernel(x_ref, v_ref, s0_ref, *rest, tc):
    y_ref, s_ref, c_sc, d_sc = rest[-4:]
    @pl.when(pl.program_id(1) == 0)
    def _():
        s_ref[...] = s0_ref[...]

    half_a = slice(0, SCAN_VR)
    half_b = slice(SCAN_VR, RWKV_HEAD)

    def both_halves(p):
        return p + pltpu.roll(p, SCAN_PAIRS, axis=1)

    def key_dot(u, w):
        return both_halves(jnp.sum(u * w, axis=0, keepdims=True))

    def first_partial(rows):
        p = s_ref[0, 0, rows, :] * x_ref[0, 0, 0, 0:1, :]
        for k in range(1, SCAN_KH):
            p = p + s_ref[0, k, rows, :] * x_ref[0, 0, 0, k:k + 1, :]
        return p

    def half_step(t, rows, sa):
        v_half = v_ref[0, t, rows, :]
        v = jnp.concatenate([v_half, v_half], axis=1)
        q = None
        y = None
        for k in range(SCAN_KH):
            s_old = s_ref[0, k, rows, :]
            qk = s_old * c_sc[k:k + 1, :]
            sn = (s_old * x_ref[1, 0, t, k:k + 1, :] + sa * x_ref[2, 0, t, k:k + 1, :]
                  + v * x_ref[3, 0, t, k:k + 1, :])
            s_ref[0, k, rows, :] = sn
            yk = sn * x_ref[4, 0, t, k:k + 1, :]
            q = qk if q is None else q + qk
            y = yk if y is None else y + yk
        return q, sa * d_sc[0:1, :] + v * d_sc[1:2, :], y

    def store_y(t, y_a, y_b):
        y_ref[0, t, half_a, :] = both_halves(y_a)[:, :SCAN_PAIRS]
        y_ref[0, t, half_b, :] = both_halves(y_b)[:, :SCAN_PAIRS]

    def step(t, carry):
        sa_a, q_b, corr_b, y_a, y_b = carry
        store_y(jnp.maximum(t - 1, 0), y_a, y_b)
        a_next = x_ref[0, 0, jnp.minimum(t + 1, tc - 1)]
        c_sc[...] = x_ref[1, 0, t] * a_next
        d_sc[0:1, :] = key_dot(x_ref[2, 0, t], a_next)
        d_sc[1:2, :] = key_dot(x_ref[3, 0, t], a_next)
        sa_b = both_halves(q_b) + corr_b
        q_a, corr_a, y_a_new = half_step(t, half_a, sa_a)
        sa_a_next = both_halves(q_a) + corr_a
        q_b_next, corr_b_next, y_b_new = half_step(t, half_b, sa_b)
        return sa_a_next, q_b_next, corr_b_next, y_a_new, y_b_new

    zero = jnp.zeros((SCAN_VR, LANES), F32)
    init = (both_halves(first_partial(half_a)), first_partial(half_b), zero, zero, zero)
    final = lax.fori_loop(0, tc, step, init)
    store_y(tc - 1, final[3], final[4])


def _rwkv_scan(xs, v, s0, after, tc):
    _, nb, t, _, _ = xs.shape
    xspec = pl.BlockSpec((5, 1, tc, SCAN_KH, LANES), lambda n, c: (0, n, c, 0, 0))
    vspec = pl.BlockSpec((1, tc, RWKV_HEAD, SCAN_PAIRS), lambda n, c: (n, c, 0, 0))
    sspec = pl.BlockSpec((1, SCAN_KH, RWKV_HEAD, LANES), lambda n, c: (n, 0, 0, 0))
    return pl.pallas_call(
        functools.partial(_rwkv_scan_kernel, tc=tc),
        grid=(nb, t // tc),
        in_specs=[xspec, vspec, sspec] + [pl.BlockSpec(memory_space=pl.ANY)] * len(after),
        out_specs=[vspec, sspec],
        out_shape=[jax.ShapeDtypeStruct(v.shape, F32), jax.ShapeDtypeStruct(s0.shape, F32)],
        scratch_shapes=[pltpu.VMEM((SCAN_KH, LANES), F32), pltpu.VMEM((8, LANES), F32)],
        compiler_params=_cparams(("parallel", "arbitrary")),
    )(xs, v, s0, *after)


def _even_out_kernel(y_ref, rkv_ref, g_ref, lng_ref, lnb_ref, ones_ref, ol_ref, wuv_ref, woa_ref,
                     wob_ref, x_ref, o_ref):
    ones = ones_ref[...]
    y = y_ref[...]
    inv = 1.0 / RWKV_HEAD
    mean = _dot_exact_rhs(y, ones) * inv
    dlt = y - mean
    var = _dot_exact_rhs(dlt * dlt, ones) * inv
    yn = dlt * lax.rsqrt(var + RWKV_LN_EPS) * lng_ref[...] + lnb_ref[...] + rkv_ref[...]
    ob = (yn * g_ref[...]).astype(BF16)
    pair = 2 * MLA_KV_RANK
    oa = jnp.concatenate(
        [_dot(ol_ref[:, p * pair:(p + 1) * pair], wuv_ref[p]) for p in range(MLA_HEADS // 2)], axis=1)
    o_ref[...] = x_ref[...] + _dot(oa, woa_ref[...]) + _dot(ob, wob_ref[...])


def _even_out(y, rkv, g, o_lat, x, w, tm):
    n = x.shape[0]
    d = RWKV_DIM
    hq = MLA_HEADS * MLA_KV_RANK
    return pl.pallas_call(
        _even_out_kernel,
        grid=(n // tm,),
        in_specs=[_row_spec(tm, d), _row_spec(tm, d), _row_spec(tm, d), _const_spec((1, d)),
                  _const_spec((1, d)), _const_spec((d, d)), _row_spec(tm, hq),
                  _const_spec((MLA_HEADS // 2, 2 * MLA_KV_RANK, 2 * MLA_V)),
                  _const_spec((MLA_HEADS * MLA_V, D_MODEL)), _const_spec((d, D_MODEL)),
                  _row_spec(tm, D_MODEL)],
        out_specs=_row_spec(tm, D_MODEL),
        out_shape=jax.ShapeDtypeStruct((n, D_MODEL), F32),
        compiler_params=_cparams(("parallel",)),
    )(y, rkv, g, w["ln_g"], w["ln_b"], w["ones_bd"], o_lat, w["w_uv_bd"], w["w_out_a"],
      w["w_out_b"], x)


FFN_TF = 1408


def _ffn_kernel(x_ref, g_ref, wg_ref, wu_ref, wd_ref, o_ref, xn_sc, acc_sc):
    f = pl.program_id(1)

    @pl.when(f == 0)
    def _():
        xn_sc[...] = _rms(x_ref[...], g_ref[...]).astype(BF16)
        acc_sc[...] = jnp.zeros(acc_sc.shape, F32)

    xn = xn_sc[...]
    gate = _dot(xn, wg_ref[...])
    up = _dot(xn, wu_ref[...])
    acc_sc[...] += _dot(gate * _sigmoid(gate) * up, wd_ref[...])

    @pl.when(f == pl.num_programs(1) - 1)
    def _():
        o_ref[...] = x_ref[...] + acc_sc[...]


def _ffn(x, g, w_gu, w_down, tm):
    n = x.shape[0]
    nf = D_FF // FFN_TF
    return pl.pallas_call(
        _ffn_kernel,
        grid=(n // tm, nf),
        in_specs=[pl.BlockSpec((tm, D_MODEL), lambda i, f: (i, 0)),
                  pl.BlockSpec((1, D_MODEL), lambda i, f: (0, 0)),
                  pl.BlockSpec((D_MODEL, FFN_TF), lambda i, f: (0, f)),
                  pl.BlockSpec((D_MODEL, FFN_TF), lambda i, f: (0, nf + f)),
                  pl.BlockSpec((FFN_TF, D_MODEL), lambda i, f: (f, 0))],
        out_specs=pl.BlockSpec((tm, D_MODEL), lambda i, f: (i, 0)),
        out_shape=jax.ShapeDtypeStruct((n, D_MODEL), F32),
        scratch_shapes=[pltpu.VMEM((tm, D_MODEL), BF16), pltpu.VMEM((tm, D_MODEL), F32)],
        compiler_params=_cparams(("parallel", "arbitrary")),
    )(x, g, w_gu, w_gu, w_down)


def _odd_in_kernel(x_ref, g_ref, wq_ref, wk_ref, wv_ref, wg_ref, wxa_ref, a2_ref, ab_ref,
                   q_ref, k_ref, v_ref, gate_ref, la_ref):
    xn = _rms(x_ref[...], g_ref[...]).astype(BF16)
    q_ref[...] = _dot(xn, wq_ref[...]) * (GLA_DK ** -0.5)
    k_ref[...] = _dot(xn, wk_ref[...])
    v_ref[...] = _dot(xn, wv_ref[...])
    gate_ref[...] = _dot(xn, wg_ref[...])
    z = _dot(_dot(xn, wxa_ref[...]), a2_ref[...]) + ab_ref[...]
    la_ref[...] = -_softplus(-z) * (1.0 / GLA_GATE_NORM)


def _odd_in(x, w, tm):
    n = x.shape[0]
    return pl.pallas_call(
        _odd_in_kernel,
        grid=(n // tm,),
        in_specs=[_row_spec(tm, D_MODEL), _const_spec((1, D_MODEL)),
                  _const_spec((D_MODEL, GLA_KDIM)), _const_spec((D_MODEL, GLA_KDIM)),
                  _const_spec((D_MODEL, GLA_VDIM)), _const_spec((D_MODEL, GLA_VDIM)),
                  _const_spec((D_MODEL, LANES)), _const_spec((LANES, GLA_KDIM)),
                  _const_spec((1, GLA_KDIM))],
        out_specs=[_row_spec(tm, GLA_KDIM), _row_spec(tm, GLA_KDIM), _row_spec(tm, GLA_VDIM),
                   _row_spec(tm, GLA_VDIM), _row_spec(tm, GLA_KDIM)],
        out_shape=[jax.ShapeDtypeStruct((n, GLA_KDIM), F32), jax.ShapeDtypeStruct((n, GLA_KDIM), F32),
                   jax.ShapeDtypeStruct((n, GLA_VDIM), F32), jax.ShapeDtypeStruct((n, GLA_VDIM), F32),
                   jax.ShapeDtypeStruct((n, GLA_KDIM), F32)],
        compiler_params=_cparams(("parallel",)),
    )(x, w["norm_mix"], w["w_q"], w["w_k"], w["w_v"], w["w_g"], w["w_xa"], w["a2p"], w["ab"])


def _gla_kernel(q_ref, k_ref, v_ref, la_ref, s0_ref, o_ref, st_ref):
    c = GLA_CHUNK

    @pl.when(pl.program_id(1) == 0)
    def _():
        st_ref[...] = s0_ref[...]

    row = lax.broadcasted_iota(jnp.int32, (c, c), 0)
    col = lax.broadcasted_iota(jnp.int32, (c, c), 1)
    tri = row >= col
    tri_b = jnp.where(tri, 1.0, 0.0).astype(BF16)
    for h in range(GLA_HEADS):
        ks = slice(h * GLA_DK, (h + 1) * GLA_DK)
        vs = slice(h * GLA_DV, (h + 1) * GLA_DV)
        b = _dot_exact_lhs(tri_b, la_ref[:, ks])
        q = q_ref[:, ks]
        k = k_ref[:, ks]
        v = v_ref[:, vs]
        b_end = b[c - 1:c, :]
        qe = (q * jnp.exp(b)).astype(BF16)
        ke = (k * jnp.exp(-b)).astype(BF16)
        a_mat = jnp.where(tri, _dot_nt(qe, ke), 0.0)
        st = st_ref[0, h]
        o_ref[:, vs] = _dot_nt(qe, st) + _dot(a_mat, v)
        k_end = k * jnp.exp(b_end - b)
        st_ref[0, h] = st * jnp.exp(b_end) + _dot(v.T, k_end)


def _gla(q, k, v, la, s0t, batch, seq):
    nc = seq // GLA_CHUNK
    rspec = lambda width: pl.BlockSpec((GLA_CHUNK, width), lambda b, c: (b * nc + c, 0))
    sspec = pl.BlockSpec((1, GLA_HEADS, GLA_DV, GLA_DK), lambda b, c: (b, 0, 0, 0))
    return pl.pallas_call(
        _gla_kernel,
        grid=(batch, nc),
        in_specs=[rspec(GLA_KDIM), rspec(GLA_KDIM), rspec(GLA_VDIM), rspec(GLA_KDIM), sspec],
        out_specs=[rspec(GLA_VDIM), sspec],
        out_shape=[jax.ShapeDtypeStruct(v.shape, F32), jax.ShapeDtypeStruct(s0t.shape, F32)],
        compiler_params=_cparams(("parallel", "arbitrary")),
    )(q, k, v, la, s0t)


def _odd_out_kernel(o_ref, gate_ref, gn_ref, wo_ref, x_ref, y_ref):
    parts = []
    for h in range(GLA_HEADS):
        vs = slice(h * GLA_DV, (h + 1) * GLA_DV)
        parts.append(_rms(o_ref[:, vs], gn_ref[:, vs]))
    gate = gate_ref[...]
    on = jnp.concatenate(parts, axis=1) * (gate * _sigmoid(gate))
    y_ref[...] = x_ref[...] + _dot(on, wo_ref[...])


def _odd_out(o, gate, x, w, tm):
    n = x.shape[0]
    return pl.pallas_call(
        _odd_out_kernel,
        grid=(n // tm,),
        in_specs=[_row_spec(tm, GLA_VDIM), _row_spec(tm, GLA_VDIM), _const_spec((1, GLA_VDIM)),
                  _const_spec((GLA_VDIM, D_MODEL)), _row_spec(tm, D_MODEL)],
        out_specs=_row_spec(tm, D_MODEL),
        out_shape=jax.ShapeDtypeStruct((n, D_MODEL), F32),
        compiler_params=_cparams(("parallel",)),
    )(o, gate, w["gla_norm"], w["w_out"], x)


def _router_kernel(x_ref, g_ref, wr_ref, *rest):
    xn_ref, idx_ref, gate_ref = rest[-3:]
    xn = _rms(x_ref[...], g_ref[...])
    half = D_MODEL // 2
    xn_ref[0] = xn[:, :half]
    xn_ref[1] = xn[:, half:]
    logits = _dot_f32ish(xn, wr_ref[...])
    lane = lax.broadcasted_iota(jnp.int32, logits.shape, 1)
    logits = jnp.where(lane < N_EXPERTS, logits, NEG_BIG)
    m1 = jnp.max(logits, axis=-1, keepdims=True)
    i1 = jnp.min(jnp.where(logits == m1, lane, LANES), axis=-1, keepdims=True)
    rest = jnp.where(lane == i1, NEG_BIG, logits)
    m2 = jnp.max(rest, axis=-1, keepdims=True)
    i2 = jnp.min(jnp.where(rest == m2, lane, LANES), axis=-1, keepdims=True)
    e2 = jnp.exp(m2 - m1)
    g1 = 1.0 / (1.0 + e2)
    g2 = e2 / (1.0 + e2)
    idx_ref[...] = jnp.where(lane == 0, i1, jnp.where(lane == 1, i2, 0))
    gate_ref[...] = jnp.where(lane == 0, g1, jnp.where(lane == 1, g2, 0.0))


def _router(x, g, wr, tm, n_total, row0, prev=None):
    n = x.shape[0]
    half = D_MODEL // 2
    blk0 = row0 // tm
    prev = () if prev is None else tuple(prev)
    return pl.pallas_call(
        _router_kernel,
        grid=(n // tm,),
        in_specs=[_row_spec(tm, D_MODEL), _const_spec((1, D_MODEL)), _const_spec((D_MODEL, LANES))]
        + [pl.BlockSpec(memory_space=pl.ANY)] * len(prev),
        out_specs=[pl.BlockSpec((2, tm, half), lambda i: (0, blk0 + i, 0)),
                   pl.BlockSpec((tm, LANES), lambda i: (blk0 + i, 0)),
                   pl.BlockSpec((tm, LANES), lambda i: (blk0 + i, 0))],
        out_shape=[jax.ShapeDtypeStruct((2, n_total, half), F32),
                   jax.ShapeDtypeStruct((n_total, LANES), jnp.int32),
                   jax.ShapeDtypeStruct((n_total, LANES), F32)],
        input_output_aliases={3 + k: k for k in range(len(prev))},
        compiler_params=_cparams(("parallel",)),
    )(x, g, wr, *prev)


MOE_TF = 1792
MOE_TMC = 256


def _route(top_i, tm, tmc):
    n = top_i.shape[0]
    slots = 2 * n
    n_tiles = -(-(slots + N_EXPERTS * (tm - 1)) // tm)
    win = tmc + 8
    e_flat = top_i.reshape(-1)
    onehot = (e_flat[:, None] == jnp.arange(N_EXPERTS, dtype=jnp.int32)[None, :]).astype(jnp.int32)
    csum = jnp.cumsum(onehot, axis=0)
    rank = jnp.sum(onehot * csum, axis=1) - 1
    counts = csum[-1]
    padded = ((counts + tm - 1) // tm) * tm
    ends = jnp.cumsum(padded)
    starts = ends - padded
    dest = (jnp.sum(onehot * starts[None, :], axis=1) + rank).astype(jnp.int32)
    tile_start = jnp.arange(n_tiles, dtype=jnp.int32) * tm
    tile_expert = jnp.minimum(jnp.sum((tile_start[:, None] >= ends[None, :]).astype(jnp.int32), axis=1),
                              N_EXPERTS - 1).astype(jnp.int32)
    tile_valid = (tile_start < ends[-1]).astype(jnp.int32)
    src = jnp.zeros((n_tiles * tm,), jnp.int32).at[dest].set(jnp.arange(slots, dtype=jnp.int32) // 2)
    before = jnp.concatenate([jnp.zeros((1, N_EXPERTS), jnp.int32), csum[2 * tmc - 1:-1:2 * tmc]], axis=0)
    wstart = jnp.clip(((starts[None, :] + before) // 8) * 8, 0, n_tiles * tm - win).astype(jnp.int32)
    ws_slot = jnp.sum(onehot * jnp.repeat(wstart, 2 * tmc, axis=0), axis=1)
    local = (e_flat * win + dest - ws_slot).astype(jnp.int32)
    return local, wstart.reshape(-1), src, tile_expert, tile_valid


def _moe_gather_kernel(src_ref, x_ref, o_ref, *, tg):
    base = pl.program_id(1) * tg

    def body(r, carry):
        o_ref[pl.ds(r, 1), :] = x_ref[pl.ds(src_ref[base + r], 1), :]
        return carry

    lax.fori_loop(0, tg, body, 0, unroll=8)


def _moe_gather(src, xn2, tg):
    rows = src.shape[0]
    _, n, half = xn2.shape
    grid_spec = pltpu.PrefetchScalarGridSpec(
        num_scalar_prefetch=1,
        grid=(2, rows // tg),
        in_specs=[pl.BlockSpec((None, n, half), lambda h, i, s: (h, 0, 0),
                               pipeline_mode=pl.Buffered(1))],
        out_specs=pl.BlockSpec((tg, half), lambda h, i, s: (i, h)),
    )
    return pl.pallas_call(
        functools.partial(_moe_gather_kernel, tg=tg),
        grid_spec=grid_spec,
        out_shape=jax.ShapeDtypeStruct((rows, 2 * half), F32),
        compiler_params=_cparams(("arbitrary", "arbitrary")),
    )(src, xn2)


def _moe_up_kernel(te_ref, tv_ref, xs_ref, wg_ref, wu_ref, h_ref):
    @pl.when(tv_ref[pl.program_id(1)] != 0)
    def _():
        xs = xs_ref[...].astype(BF16)
        gate = _dot(xs, wg_ref[...])
        up = _dot(xs, wu_ref[...])
        h_ref[...] = (gate * _sigmoid(gate) * up).astype(BF16)


def _moe_up(te, tv, xs, w_gu, layer, tm):
    rows = xs.shape[0]
    nf = D_FF_EXPERT // MOE_TF
    wspec = lambda off: pl.BlockSpec((None, None, D_MODEL, MOE_TF),
                                     lambda f, t, te, tv: (layer, te[t], 0, off + f))
    grid_spec = pltpu.PrefetchScalarGridSpec(
        num_scalar_prefetch=2,
        grid=(nf, rows // tm),
        in_specs=[pl.BlockSpec((tm, D_MODEL), lambda f, t, te, tv: (t, 0)), wspec(0), wspec(nf)],
        out_specs=pl.BlockSpec((tm, MOE_TF), lambda f, t, te, tv: (t, f)),
    )
    return pl.pallas_call(
        _moe_up_kernel,
        grid_spec=grid_spec,
        out_shape=jax.ShapeDtypeStruct((rows, D_FF_EXPERT), BF16),
        compiler_params=_cparams(("arbitrary", "arbitrary")),
    )(te, tv, xs, w_gu, w_gu)


def _moe_down_kernel(te_ref, tv_ref, h_ref, wd_ref, y_ref):
    @pl.when(tv_ref[pl.program_id(0)] != 0)
    def _():
        y_ref[...] = _dot(h_ref[...], wd_ref[...])


def _moe_down(te, tv, h, w_down, layer, tm):
    rows = h.shape[0]
    grid_spec = pltpu.PrefetchScalarGridSpec(
        num_scalar_prefetch=2,
        grid=(rows // tm,),
        in_specs=[pl.BlockSpec((tm, D_FF_EXPERT), lambda t, te, tv: (t, 0)),
                  pl.BlockSpec((None, None, D_FF_EXPERT, D_MODEL),
                               lambda t, te, tv: (layer, te[t], 0, 0))],
        out_specs=pl.BlockSpec((tm, D_MODEL), lambda t, te, tv: (t, 0)),
    )
    return pl.pallas_call(
        _moe_down_kernel,
        grid_spec=grid_spec,
        out_shape=jax.ShapeDtypeStruct((rows, D_MODEL), F32),
        compiler_params=_cparams(("arbitrary",)),
    )(te, tv, h, w_down)


def _moe_combine_kernel(ws_ref, local_ref, *refs, tmc, win, tile0):
    win_refs = refs[:N_EXPERTS]
    g1_ref, g2_ref, x_ref, fn_ref, o_ref, buf = refs[N_EXPERTS:]
    for e in range(N_EXPERTS):
        buf[e * win:(e + 1) * win, :] = win_refs[e][...]
    base = 2 * (tile0 + pl.program_id(0)) * tmc

    def body(r, carry):
        row = pl.ds(r, 1)
        y1 = buf[pl.ds(local_ref[base + 2 * r], 1), :]
        y2 = buf[pl.ds(local_ref[base + 2 * r + 1], 1), :]
        g1 = g1_ref[row, :]
        g2 = g2_ref[row, :]
        parts = []
        for c in range(D_MODEL // LANES):
            cs = slice(c * LANES, (c + 1) * LANES)
            parts.append(g1 * y1[:, cs] + g2 * y2[:, cs])
        o_ref[row, :] = x_ref[row, :] + jnp.concatenate(parts, axis=1)
        return carry

    lax.fori_loop(0, tmc, body, 0, unroll=4)
    o_ref[...] = _rms(o_ref[...], fn_ref[...])


def _moe_combine(local, wstart, ys, g1b, g2b, x, fn, tmc, row0):
    n = x.shape[0]
    win = tmc + 8
    tile0 = row0 // tmc

    def win_spec(e):
        return pl.BlockSpec(
            (pl.Element(win), pl.Element(D_MODEL)),
            lambda i, ws, lo: (pl.multiple_of(ws[(tile0 + i) * N_EXPERTS + e], 8), 0))

    grid_spec = pltpu.PrefetchScalarGridSpec(
        num_scalar_prefetch=2,
        grid=(n // tmc,),
        in_specs=[win_spec(e) for e in range(N_EXPERTS)]
        + [pl.BlockSpec((tmc, LANES), lambda i, ws, lo: (tile0 + i, 0)),
           pl.BlockSpec((tmc, LANES), lambda i, ws, lo: (tile0 + i, 0)),
           pl.BlockSpec((tmc, D_MODEL), lambda i, ws, lo: (i, 0)),
           pl.BlockSpec((1, D_MODEL), lambda i, ws, lo: (0, 0))],
        out_specs=pl.BlockSpec((tmc, D_MODEL), lambda i, ws, lo: (i, 0)),
        scratch_shapes=[pltpu.VMEM((N_EXPERTS * win, D_MODEL), F32)],
    )
    return pl.pallas_call(
        functools.partial(_moe_combine_kernel, tmc=tmc, win=win, tile0=tile0),
        grid_spec=grid_spec,
        out_shape=jax.ShapeDtypeStruct((n, D_MODEL), F32),
        compiler_params=_cparams(("arbitrary",)),
    )(wstart, local, *([ys] * N_EXPERTS), g1b, g2b, x, fn)


def _scan_vec_layout(xs, batch, seq):
    nb = batch * RWKV_HEADS // SCAN_PAIRS
    x = xs.reshape(5, batch, seq, RWKV_HEADS, 2, SCAN_KH).transpose(0, 2, 5, 4, 1, 3)
    x = x.reshape(5, seq, SCAN_KH, 2, nb, SCAN_PAIRS).transpose(0, 4, 1, 2, 3, 5)
    return x.reshape(5, nb, seq, SCAN_KH, LANES)


def _scan_val_layout(v, batch, seq):
    nb = batch * RWKV_HEADS // SCAN_PAIRS
    v4 = v.reshape(batch, seq, RWKV_HEADS, RWKV_HEAD).transpose(1, 3, 0, 2)
    return v4.reshape(seq, RWKV_HEAD, nb, SCAN_PAIRS).transpose(2, 0, 1, 3)


def _scan_val_unlayout(y, batch, seq):
    v4 = y.transpose(1, 2, 0, 3).reshape(seq, RWKV_HEAD, batch, RWKV_HEADS)
    return v4.transpose(2, 0, 3, 1).reshape(batch * seq, RWKV_DIM)


def _scan_state_layout(s, batch):
    nb = batch * RWKV_HEADS // SCAN_PAIRS
    s6 = s.reshape(batch, RWKV_HEADS, RWKV_HEAD, 2, SCAN_KH).transpose(4, 2, 3, 0, 1)
    s6 = s6.reshape(SCAN_KH, RWKV_HEAD, 2, nb, SCAN_PAIRS).transpose(3, 0, 1, 2, 4)
    return s6.reshape(nb, SCAN_KH, RWKV_HEAD, LANES)


def _scan_state_unlayout(arr, batch):
    nb = arr.shape[0]
    s = arr.reshape(nb, SCAN_KH, RWKV_HEAD, 2, SCAN_PAIRS).transpose(0, 4, 2, 3, 1)
    return s.reshape(batch, RWKV_HEADS, RWKV_HEAD, RWKV_HEAD)


def _swap_halves(w):
    half = w.shape[-1] // 2
    return jnp.concatenate([w[..., half:], w[..., :half]], axis=-1)


def _prep_even(i, norm_mix, norm_ffn, w_in, q_norm, kv_norm, w_uq, w_uk, w_uv, mu, w0, w2, a0, a2,
               g2, k_k, k_a, r_k, ln_g, ln_b, w_out, ffn_gu, ffn_down):
    w = {}
    row = lambda v: v[i].reshape(1, -1)
    w_in = w_in[i]
    w["norm_mix"] = row(norm_mix)
    w["norm_ffn"] = row(norm_ffn)
    w["w_q"] = w_in[:, :MLA_Q_RANK].astype(BF16)
    w_kv = w_in[:, MLA_Q_RANK:MLA_Q_RANK + MLA_LAT]
    w["w_ckv"] = w_kv[:, :MLA_KV_RANK].astype(BF16)
    lane_pad = lambda m: jnp.pad(m, [(0, 0)] * (m.ndim - 1) + [(0, LANES - m.shape[-1])])
    w["w_pe_a"] = lane_pad(w_kv[:, MLA_KV_RANK:]).astype(BF16)
    w["w_pe_b"] = lane_pad(_swap_halves(w_kv[:, MLA_KV_RANK:])).astype(BF16)
    w["w_rw"] = w_in[:, MLA_Q_RANK + MLA_LAT:].astype(BF16)
    w["q_norm"] = row(q_norm)
    w["kv_norm"] = row(kv_norm)
    uq = w_uq[i].reshape(MLA_Q_RANK, MLA_HEADS, MLA_NOPE + MLA_ROPE)
    uq_pe = uq[:, :, MLA_NOPE:]
    w["w_qpe_a"] = lane_pad(uq_pe).reshape(MLA_Q_RANK, -1).astype(BF16)
    w["w_qpe_b"] = lane_pad(_swap_halves(uq_pe)).reshape(MLA_Q_RANK, -1).astype(BF16)
    w["w_qlat"] = _fold_qlat(uq[:, :, :MLA_NOPE].transpose(1, 0, 2), w_uk[i].transpose(1, 0, 2))
    uv = w_uv[i].transpose(1, 0, 2).reshape(MLA_HEADS // 2, 2, MLA_KV_RANK, MLA_V)
    zero = jnp.zeros_like(uv[:, 0])
    w["w_uv_bd"] = jnp.concatenate(
        [jnp.concatenate([uv[:, 0], zero], axis=-1), jnp.concatenate([zero, uv[:, 1]], axis=-1)],
        axis=1).astype(BF16)
    w["mu"] = row(mu)
    w["w0"] = row(w0)
    pad = lambda m, before: jnp.pad(m, ((before, LANES - before - m.shape[0]), (0, 0))).astype(BF16)
    w["w2p"] = pad(w2[i], 0)
    w["a2p"] = pad(a2[i], RWKV_W_LORA)
    w["a0"] = row(a0)
    w["g2"] = g2[i].astype(BF16)
    w["k_k"] = row(k_k)
    w["k_a"] = row(k_a)
    w["r_k"] = row(r_k)
    w["ln_g"] = row(ln_g)
    w["ln_b"] = row(ln_b)
    head = jnp.arange(RWKV_DIM) // RWKV_HEAD
    w["ones_bd"] = (head[:, None] == head[None, :]).astype(BF16)
    w["w_out_a"] = w_out[i][:MLA_HEADS * MLA_V].astype(BF16)
    w["w_out_b"] = w_out[i][MLA_HEADS * MLA_V:].astype(BF16)
    w["ffn_gu"] = ffn_gu[i].astype(BF16)
    w["ffn_down"] = ffn_down[i].astype(BF16)
    return w


def _prep_odd(i, norm_mix, norm_ffn, w_in, a2, ab, gla_norm, w_out, router, moe_gu, moe_down):
    w = {}
    row = lambda v: v[i].reshape(1, -1)
    w_in = w_in[i]
    w["norm_mix"] = row(norm_mix)
    w["norm_ffn"] = row(norm_ffn)
    w["w_q"] = w_in[:, :GLA_KDIM].astype(BF16)
    w["w_k"] = w_in[:, GLA_KDIM:2 * GLA_KDIM].astype(BF16)
    w["w_v"] = w_in[:, 2 * GLA_KDIM:2 * GLA_KDIM + GLA_VDIM].astype(BF16)
    w["w_g"] = w_in[:, 2 * GLA_KDIM + GLA_VDIM:2 * GLA_KDIM + 2 * GLA_VDIM].astype(BF16)
    w["w_xa"] = jnp.pad(w_in[:, 2 * GLA_KDIM + 2 * GLA_VDIM:],
                        ((0, 0), (0, LANES - GLA_GATE_RANK))).astype(BF16)
    w["a2p"] = jnp.pad(a2[i], ((0, LANES - GLA_GATE_RANK), (0, 0))).astype(BF16)
    w["ab"] = row(ab)
    w["gla_norm"] = row(gla_norm)
    w["w_out"] = w_out[i].astype(BF16)
    w["router"] = jnp.pad(router[i], ((0, 0), (0, LANES - N_EXPERTS)))
    w["layer"] = i
    w["moe_gu"] = moe_gu
    w["moe_down"] = moe_down
    return w


def _rope_tables(pos, reps):
    inv = ROPE_THETA ** (-jnp.arange(0, MLA_ROPE, 2, dtype=F32) / MLA_ROPE)
    ang = pos.astype(F32)[:, None] * inv[None, :]
    cos, sin = jnp.cos(ang), jnp.sin(ang)
    pad = ((0, 0), (0, LANES - MLA_ROPE))
    cs = jnp.tile(jnp.pad(jnp.concatenate([cos, cos], axis=-1), pad), (reps, 1))
    sn = jnp.tile(jnp.pad(jnp.concatenate([-sin, sin], axis=-1), pad), (reps, 1))
    return {"cs": cs, "sn": sn, "cs8": jnp.tile(cs, (1, MLA_HEADS)), "sn8": jnp.tile(sn, (1, MLA_HEADS))}


def _even_layer(x, batch, seq, tabs, state, shift0, past, w, tm, tc, after=()):
    n = batch * seq
    lat, lat_b, q_lat, q_pe, rw = _even_in(x, w, tabs, tm)
    if past is None:
        o_lat = _mla_prompt(q_lat, q_pe, lat_b, batch, seq)
    else:
        cache, layer, page_table = past
        rows = seq * MLA_HEADS
        q_full = jnp.concatenate([q_lat.reshape(batch, rows, MLA_KV_RANK),
                                  q_pe.reshape(batch, rows, LANES)[:, :, :MLA_ROPE]], axis=-1)
        new_pad_t = jnp.pad(lat_b.reshape(batch, seq, MLA_LATB)[:, :, :MLA_LAT],
                            ((0, 0), (0, PAGE_SIZE - seq), (0, 0))).transpose(0, 2, 1)
        o_lat = _mla_decode(page_table, q_full, new_pad_t, cache.transpose(0, 1, 3, 2), layer)
        o_lat = o_lat.reshape(n, MLA_HEADS * MLA_KV_RANK)

    rw3 = rw.reshape(batch, seq, RWKV_PROJ)
    xs5, v, g, rkv = _rwkv_prep(rw, shift0, w, tm, seq)
    y_l, s_l = _rwkv_scan(_scan_vec_layout(xs5, batch, seq), _scan_val_layout(v, batch, seq),
                          _scan_state_layout(state, batch), (o_lat,) + tuple(after), tc)
    y = _scan_val_unlayout(y_l, batch, seq)
    new_state = _scan_state_unlayout(s_l, batch)

    x = _even_out(y, rkv, g, o_lat, x, w, tm)
    x = _ffn(x, w["norm_ffn"], w["ffn_gu"], w["ffn_down"], tm)
    return x, lat.reshape(batch, seq, MLA_LAT), new_state, rw3[:, -1], o_lat


def _odd_mixer_layer(x, batch, seq, state, w, tm):
    q, k, v, gate, la = _odd_in(x, w, tm)
    seq_p = -(-seq // GLA_CHUNK) * GLA_CHUNK
    if seq_p != seq:
        padr = lambda t: jnp.pad(t.reshape(batch, seq, -1), ((0, 0), (0, seq_p - seq), (0, 0))
                                 ).reshape(batch * seq_p, -1)
        qp, kp, vp, lap = padr(q), padr(k), padr(v), padr(la)
    else:
        qp, kp, vp, lap = q, k, v, la
    o, st = _gla(qp, kp, vp, lap, state.transpose(0, 1, 3, 2), batch, seq_p)
    if seq_p != seq:
        o = o.reshape(batch, seq_p, GLA_VDIM)[:, :seq].reshape(batch * seq, GLA_VDIM)
    return _odd_out(o, gate, x, w, tm), st.transpose(0, 1, 3, 2)


def _moe_all_groups(xs_groups, tms, w, final_norm, tm_moe):
    sizes = [x.shape[0] for x in xs_groups]
    n_total = sum(sizes)
    bufs, row0 = None, 0
    for x, tm in zip(xs_groups, tms):
        bufs = _router(x, w["norm_ffn"], w["router"], tm, n_total, row0, bufs)
        row0 += x.shape[0]
    xn2, idx, gates = bufs
    local, wstart, src, tile_expert, tile_valid = _route(idx[:, :2], tm_moe, MOE_TMC)
    rows = _moe_gather(src, xn2, tm_moe)
    h = _moe_up(tile_expert, tile_valid, rows, w["moe_gu"], w["layer"], tm_moe)
    ys = _moe_down(tile_expert, tile_valid, h, w["moe_down"], w["layer"], tm_moe)
    g1b = jnp.broadcast_to(gates[:, 0:1], (n_total, LANES))
    g2b = jnp.broadcast_to(gates[:, 1:2], (n_total, LANES))
    outs, row0 = [], 0
    for x in xs_groups:
        outs.append(_moe_combine(local, wstart, ys, g1b, g2b, x, final_norm, MOE_TMC, row0))
        row0 += x.shape[0]
    return outs


def kernel(x_prompt, x_sample, cache_mla, state_rwkv, state_rwkv_shift, state_gla, page_table, norm_mix_even, norm_ffn_even, w_in_even, mla_q_norm, mla_kv_norm, mla_w_uq, mla_w_uk, mla_w_uv, rwkv_mu, rwkv_w0, rwkv_w2, rwkv_a0, rwkv_a2, rwkv_g2, rwkv_k_k, rwkv_k_a, rwkv_r_k, rwkv_ln_g, rwkv_ln_b, w_out_even, ffn_w_gu_even, ffn_w_down_even, norm_mix_odd, norm_ffn_odd, w_in_odd, gla_a2, gla_ab, gla_norm, w_out_odd, moe_router, moe_w_gu, moe_w_down, final_norm):
    bp, tp, _ = x_prompt.shape
    bs, ts, _ = x_sample.shape
    past_len = page_table.shape[1] * PAGE_SIZE
    tm_p, tm_s = 512, bs * ts
    we = _prep_even(0, norm_mix_even, norm_ffn_even, w_in_even, mla_q_norm, mla_kv_norm, mla_w_uq,
                    mla_w_uk, mla_w_uv, rwkv_mu, rwkv_w0, rwkv_w2, rwkv_a0, rwkv_a2, rwkv_g2,
                    rwkv_k_k, rwkv_k_a, rwkv_r_k, rwkv_ln_g, rwkv_ln_b, w_out_even, ffn_w_gu_even,
                    ffn_w_down_even)
    wo = _prep_odd(0, norm_mix_odd, norm_ffn_odd, w_in_odd, gla_a2, gla_ab, gla_norm, w_out_odd,
                   moe_router, moe_w_gu, moe_w_down)
    fn = final_norm.reshape(1, -1)
    tabs_p = _rope_tables(jnp.arange(tp), 1)
    tabs_s = _rope_tables(past_len + jnp.arange(ts), bs)

    hp = x_prompt.reshape(bp * tp, D_MODEL)
    hs = x_sample.reshape(bs * ts, D_MODEL)
    zeros_state = jnp.zeros((bp, RWKV_HEADS, RWKV_HEAD, RWKV_HEAD), F32)
    zeros_shift = jnp.zeros((bp, RWKV_PROJ), F32)
    hs, lat_s, rs_s, sh_s, att_s = _even_layer(hs, bs, ts, tabs_s, state_rwkv[0], state_rwkv_shift[0],
                                               (cache_mla, 0, page_table), we, tm_s, ts)
    hp, lat_p, rs_p, sh_p, _ = _even_layer(hp, bp, tp, tabs_p, zeros_state, zeros_shift, None, we,
                                           tm_p, 64, after=(att_s,))
    zeros_gla = jnp.zeros((bp, GLA_HEADS, GLA_DK, GLA_DV), F32)
    hp, gs_p = _odd_mixer_layer(hp, bp, tp, zeros_gla, wo, tm_p)
    hs, gs_s = _odd_mixer_layer(hs, bs, ts, state_gla[0], wo, tm_s)
    yp, ys = _moe_all_groups([hp, hs], [tm_p, tm_s], wo, fn, 512)
    return (yp.reshape(bp, tp, D_MODEL), ys.reshape(bs, ts, D_MODEL), lat_p[None], lat_s[None],
            rs_p[None], rs_s[None], sh_p[None], sh_s[None], gs_p[None], gs_s[None])
```

```python
import functools

import jax
import jax.numpy as jnp
from jax import lax
from jax.experimental import pallas as pl
from jax.experimental.pallas import tpu as pltpu

F32 = jnp.float32
BF16 = jnp.bfloat16

D_MODEL = 1024
PAGE_SIZE = 128
NORM_EPS = 1e-6

MLA_HEADS = 8
MLA_NOPE = 64
MLA_ROPE = 32
MLA_V = 64
MLA_Q_RANK = 384
MLA_KV_RANK = 256
MLA_LAT = MLA_KV_RANK + MLA_ROPE
MLA_LATB = MLA_KV_RANK + 128
MLA_SCALE = (MLA_NOPE + MLA_ROPE) ** -0.5
ROPE_THETA = 10000.0

RWKV_HEADS = 8
RWKV_HEAD = 64
RWKV_DIM = RWKV_HEADS * RWKV_HEAD
RWKV_W_LORA = 64
RWKV_A_LORA = 64
RWKV_G_LORA = 128
RWKV_PROJ = 3 * RWKV_DIM + RWKV_W_LORA + RWKV_A_LORA + RWKV_G_LORA
RWKV_LN_EPS = 64e-5

GLA_HEADS = 4
GLA_DK = 128
GLA_DV = 256
GLA_KDIM = GLA_HEADS * GLA_DK
GLA_VDIM = GLA_HEADS * GLA_DV
GLA_GATE_RANK = 16
GLA_GATE_NORM = 16.0
GLA_CHUNK = 128

D_FF = 2816
N_EXPERTS = 8
D_FF_EXPERT = 3584

LANES = 128
VMEM_LIMIT = 56 * 1024 * 1024
NEG_BIG = -1e30
LOG2_E = 1.4426950408889634
Q_PRESCALE = MLA_SCALE * LOG2_E


def _cparams(sem):
    return pltpu.CompilerParams(dimension_semantics=sem, vmem_limit_bytes=VMEM_LIMIT)


def _const_spec(shape):
    nd = len(shape)
    return pl.BlockSpec(shape, lambda *_: (0,) * nd)


def _row_spec(tm, width):
    return pl.BlockSpec((tm, width), lambda i: (i, 0))


def _dot(a, b):
    return jnp.dot(a.astype(BF16), b.astype(BF16), preferred_element_type=F32)


def _dot_nt(a, b):
    return lax.dot_general(a.astype(BF16), b.astype(BF16), (((1,), (1,)), ((), ())),
                           preferred_element_type=F32)


def _split2(x):
    hi = x.astype(BF16)
    lo = (x - hi.astype(F32)).astype(BF16)
    return hi, lo


def _split3(x):
    hi = x.astype(BF16)
    r1 = x - hi.astype(F32)
    mid = r1.astype(BF16)
    lo = (r1 - mid.astype(F32)).astype(BF16)
    return hi, mid, lo


def _dot_exact_rhs(x, e):
    hi, mid, lo = _split3(x)
    return (jnp.dot(hi, e, preferred_element_type=F32) + jnp.dot(mid, e, preferred_element_type=F32)
            + jnp.dot(lo, e, preferred_element_type=F32))


def _dot_exact_lhs(e, x):
    hi, mid, lo = _split3(x)
    return (jnp.dot(e, hi, preferred_element_type=F32) + jnp.dot(e, mid, preferred_element_type=F32)
            + jnp.dot(e, lo, preferred_element_type=F32))


def _dot_f32ish(a, b):
    ah, al = _split2(a)
    bh, bl = _split2(b)
    return (jnp.dot(ah, bh, preferred_element_type=F32) + jnp.dot(ah, bl, preferred_element_type=F32)
            + jnp.dot(al, bh, preferred_element_type=F32))


def _lane_tile(x, width):
    return x if width == LANES else jnp.concatenate([x] * (width // LANES), axis=1)


def _rms(x, g, eps=NORM_EPS):
    return x * lax.rsqrt(jnp.mean(x * x, axis=-1, keepdims=True) + eps) * g


def _sigmoid(x):
    return 1.0 / (1.0 + jnp.exp(-x))


def _softplus(x):
    return jnp.maximum(x, 0.0) + jnp.log(1.0 + jnp.exp(-jnp.abs(x)))


def _fold_qlat_kernel(uq_ref, uk_ref, o_ref):
    a = uq_ref[...]
    b = uk_ref[...]
    ah, al = _split2(a)
    bh, bl = _split2(b)
    dn = (((1,), (1,)), ((), ()))
    o = (lax.dot_general(ah, bh, dn, preferred_element_type=F32)
         + lax.dot_general(ah, bl, dn, preferred_element_type=F32)
         + lax.dot_general(al, bh, dn, preferred_element_type=F32))
    o_ref[...] = o.astype(BF16)


def _fold_qlat(uq_nope, uk):
    return pl.pallas_call(
        _fold_qlat_kernel,
        grid=(MLA_HEADS,),
        in_specs=[pl.BlockSpec((None, MLA_Q_RANK, MLA_NOPE), lambda h: (h, 0, 0)),
                  pl.BlockSpec((None, MLA_KV_RANK, MLA_NOPE), lambda h: (h, 0, 0))],
        out_specs=pl.BlockSpec((MLA_Q_RANK, MLA_KV_RANK), lambda h: (0, h)),
        out_shape=jax.ShapeDtypeStruct((MLA_Q_RANK, MLA_HEADS * MLA_KV_RANK), BF16),
        compiler_params=_cparams(("arbitrary",)),
    )(uq_nope, uk)


def _even_in_kernel(x_ref, g_ref, wq_ref, wckv_ref, wpa_ref, wpb_ref, wrw_ref, qn_ref, kvn_ref,
                    cs_ref, sn_ref, wql_ref, wqa_ref, wqb_ref, cs8_ref, sn8_ref,
                    lat_ref, latb_ref, ql_ref, qpe_ref, rw_ref):
    xn = _rms(x_ref[...], g_ref[...]).astype(BF16)
    cq = _rms(_dot(xn, wq_ref[...]), qn_ref[...]).astype(BF16)
    ql_ref[...] = (_dot(cq, wql_ref[...]) * Q_PRESCALE).astype(BF16)
    qpe = _dot(cq, wqa_ref[...]) * cs8_ref[...] + _dot(cq, wqb_ref[...]) * sn8_ref[...]
    qpe_ref[...] = (qpe * Q_PRESCALE).astype(BF16)
    ckv = _rms(_dot(xn, wckv_ref[...]), kvn_ref[...])
    kpe = _dot(xn, wpa_ref[...]) * cs_ref[...] + _dot(xn, wpb_ref[...]) * sn_ref[...]
    lat_ref[:, :MLA_KV_RANK] = ckv
    lat_ref[:, MLA_KV_RANK:] = kpe[:, :MLA_ROPE]
    latb_ref[:, :MLA_KV_RANK] = ckv.astype(BF16)
    latb_ref[:, MLA_KV_RANK:] = kpe.astype(BF16)
    rw_ref[...] = _dot(xn, wrw_ref[...])


def _even_in(x, w, tabs, tm):
    n = x.shape[0]
    nt = tabs["cs"].shape[0] // tm
    tab = lambda width: pl.BlockSpec((tm, width), lambda i: (i % nt, 0))
    hq = MLA_HEADS * MLA_KV_RANK
    hr = MLA_HEADS * LANES
    return pl.pallas_call(
        _even_in_kernel,
        grid=(n // tm,),
        in_specs=[_row_spec(tm, D_MODEL), _const_spec((1, D_MODEL)),
                  _const_spec((D_MODEL, MLA_Q_RANK)), _const_spec((D_MODEL, MLA_KV_RANK)),
                  _const_spec((D_MODEL, LANES)), _const_spec((D_MODEL, LANES)),
                  _const_spec((D_MODEL, RWKV_PROJ)), _const_spec((1, MLA_Q_RANK)),
                  _const_spec((1, MLA_KV_RANK)), tab(LANES), tab(LANES),
                  _const_spec((MLA_Q_RANK, hq)), _const_spec((MLA_Q_RANK, hr)),
                  _const_spec((MLA_Q_RANK, hr)), tab(hr), tab(hr)],
        out_specs=[_row_spec(tm, MLA_LAT), _row_spec(tm, MLA_LATB), _row_spec(tm, hq),
                   _row_spec(tm, hr), _row_spec(tm, RWKV_PROJ)],
        out_shape=[jax.ShapeDtypeStruct((n, MLA_LAT), F32), jax.ShapeDtypeStruct((n, MLA_LATB), BF16),
                   jax.ShapeDtypeStruct((n, hq), BF16), jax.ShapeDtypeStruct((n, hr), BF16),
                   jax.ShapeDtypeStruct((n, RWKV_PROJ), F32)],
        compiler_params=_cparams(("parallel",)),
    )(x, w["norm_mix"], w["w_q"], w["w_ckv"], w["w_pe_a"], w["w_pe_b"], w["w_rw"], w["q_norm"],
      w["kv_norm"], tabs["cs"], tabs["sn"], w["w_qlat"], w["w_qpe_a"], w["w_qpe_b"],
      tabs["cs8"], tabs["sn8"])


ATT_TQ = 256


def _mla_prompt_kernel(qi_ref, kj_ref, ql_ref, qpe_ref, lat_ref, o_ref,
                       m_sc, l_sc, a_sc, acc_sc, s_sc, p_sc):
    step = pl.program_id(1)
    i = qi_ref[step]
    j = kj_ref[step]
    heads = range(MLA_HEADS)

    @pl.when(j == 0)
    def _():
        m_sc[...] = jnp.full(m_sc.shape, NEG_BIG, F32)
        l_sc[...] = jnp.zeros(l_sc.shape, F32)
        acc_sc[...] = jnp.zeros(acc_sc.shape, F32)

    def tile(masked):
        ckv = lat_ref[:, :MLA_KV_RANK]
        kpe = lat_ref[:, MLA_KV_RANK:]
        for h in heads:
            s_sc[h] = (_dot_nt(ql_ref[:, h * MLA_KV_RANK:(h + 1) * MLA_KV_RANK], ckv)
                       + _dot_nt(qpe_ref[:, h * LANES:(h + 1) * LANES], kpe))
        for h in heads:
            s = s_sc[h]
            if masked:
                tok = lax.broadcasted_iota(jnp.int32, s.shape, 0)
                key = lax.broadcasted_iota(jnp.int32, s.shape, 1)
                s = jnp.where(key <= tok, s, NEG_BIG)
            m_prev = m_sc[h]
            m_new = jnp.maximum(m_prev, jnp.max(s, axis=-1, keepdims=True))
            alpha = jnp.exp2(m_prev - m_new)
            p = jnp.exp2(s - _lane_tile(m_new, ATT_TQ))
            l_sc[h] = alpha * l_sc[h] + jnp.sum(p, axis=-1, keepdims=True)
            m_sc[h] = m_new
            a_sc[h] = alpha
            p_sc[h] = p.astype(BF16)
        for h in heads:
            acc_sc[h] = _lane_tile(a_sc[h], MLA_KV_RANK) * acc_sc[h] + _dot(p_sc[h], ckv)

    @pl.when(j < i)
    def _():
        tile(False)

    @pl.when(j == i)
    def _():
        tile(True)
        for h in heads:
            o_ref[:, h * MLA_KV_RANK:(h + 1) * MLA_KV_RANK] = (
                acc_sc[h] / _lane_tile(l_sc[h], MLA_KV_RANK)).astype(BF16)


def _mla_prompt(q_lat, q_pe, lat_b, batch, seq):
    nq = seq // ATT_TQ
    pairs = [(i, j) for i in range(nq) for j in range(i + 1)]
    qi = jnp.array([p[0] for p in pairs], jnp.int32)
    kj = jnp.array([p[1] for p in pairs], jnp.int32)
    hq = MLA_HEADS * MLA_KV_RANK
    grid_spec = pltpu.PrefetchScalarGridSpec(
        num_scalar_prefetch=2,
        grid=(batch, len(pairs)),
        in_specs=[pl.BlockSpec((ATT_TQ, hq), lambda b, s, qi, kj: (b * nq + qi[s], 0)),
                  pl.BlockSpec((ATT_TQ, MLA_HEADS * LANES), lambda b, s, qi, kj: (b * nq + qi[s], 0)),
                  pl.BlockSpec((ATT_TQ, MLA_LATB), lambda b, s, qi, kj: (b * nq + kj[s], 0))],
        out_specs=pl.BlockSpec((ATT_TQ, hq), lambda b, s, qi, kj: (b * nq + qi[s], 0)),
        scratch_shapes=[pltpu.VMEM((MLA_HEADS, ATT_TQ, LANES), F32),
                        pltpu.VMEM((MLA_HEADS, ATT_TQ, LANES), F32),
                        pltpu.VMEM((MLA_HEADS, ATT_TQ, LANES), F32),
                        pltpu.VMEM((MLA_HEADS, ATT_TQ, MLA_KV_RANK), F32),
                        pltpu.VMEM((MLA_HEADS, ATT_TQ, ATT_TQ), F32),
                        pltpu.VMEM((MLA_HEADS, ATT_TQ, ATT_TQ), BF16)],
    )
    return pl.pallas_call(
        _mla_prompt_kernel,
        grid_spec=grid_spec,
        out_shape=jax.ShapeDtypeStruct(q_lat.shape, BF16),
        compiler_params=_cparams(("parallel", "arbitrary")),
    )(qi, kj, q_lat, q_pe, lat_b)


PAGES_PER_STEP = 32
DECODE_GROUPS = 8


def _mla_decode_kernel(pt_ref, q_ref, new_ref, *rest):
    page_refs = rest[:PAGES_PER_STEP]
    o_ref, m_sc, l_sc, acc_sc = rest[PAGES_PER_STEP:]
    j = pl.program_id(1)
    q = q_ref[0]

    @pl.when(j == 0)
    def _():
        m_sc[...] = jnp.full(m_sc.shape, NEG_BIG, F32)
        l_sc[...] = jnp.zeros(l_sc.shape, F32)
        acc_sc[...] = jnp.zeros(acc_sc.shape, F32)

    def update(state, s, values_t):
        m_prev, l_prev, acc = state
        m_new = jnp.maximum(m_prev, jnp.max(s, axis=-1, keepdims=True))
        alpha = jnp.exp2(m_prev - m_new)
        p = jnp.exp2(s - _lane_tile(m_new, s.shape[1]))
        l_new = alpha * l_prev + jnp.sum(p, axis=-1, keepdims=True)
        return m_new, l_new, _lane_tile(alpha, MLA_KV_RANK) * acc + _dot_nt(p, values_t)

    group = PAGES_PER_STEP // DECODE_GROUPS
    keys = [jnp.concatenate([pr[...].astype(BF16) for pr in page_refs[g * group:(g + 1) * group]],
                            axis=1) for g in range(DECODE_GROUPS)]
    scores = [_dot(q, kt) for kt in keys]
    state = (m_sc[...], l_sc[...], acc_sc[...])
    for s, kt in zip(scores, keys):
        state = update(state, s, kt[:MLA_KV_RANK, :])
    m_sc[...], l_sc[...], acc_sc[...] = state

    @pl.when(j == pl.num_programs(1) - 1)
    def _():
        new_t = new_ref[0]
        sn = _dot(q, new_t)
        tok = lax.broadcasted_iota(jnp.int32, sn.shape, 0) >> 3
        key = lax.broadcasted_iota(jnp.int32, sn.shape, 1)
        sn = jnp.where(key <= tok, sn, NEG_BIG)
        _, l_fin, acc_fin = update(state, sn, new_t[:MLA_KV_RANK, :])
        o_ref[0] = (acc_fin / _lane_tile(l_fin, MLA_KV_RANK)).astype(BF16)


def _mla_decode(page_table, q_full, new_pad_t, cache_t, layer):
    db, n_pages = page_table.shape
    rows = q_full.shape[1]
    steps = n_pages // PAGES_PER_STEP

    def page_spec(p):
        return pl.BlockSpec((None, None, MLA_LAT, PAGE_SIZE),
                            lambda b, j, pt: (layer, pt[b, j * PAGES_PER_STEP + p], 0, 0))

    grid_spec = pltpu.PrefetchScalarGridSpec(
        num_scalar_prefetch=1,
        grid=(db, steps),
        in_specs=[pl.BlockSpec((1, rows, MLA_LAT), lambda b, j, pt: (b, 0, 0)),
                  pl.BlockSpec((1, MLA_LAT, PAGE_SIZE), lambda b, j, pt: (b, 0, 0))]
        + [page_spec(p) for p in range(PAGES_PER_STEP)],
        out_specs=pl.BlockSpec((1, rows, MLA_KV_RANK), lambda b, j, pt: (b, 0, 0)),
        scratch_shapes=[pltpu.VMEM((rows, LANES), F32), pltpu.VMEM((rows, LANES), F32),
                        pltpu.VMEM((rows, MLA_KV_RANK), F32)],
    )
    return pl.pallas_call(
        _mla_decode_kernel,
        grid_spec=grid_spec,
        out_shape=jax.ShapeDtypeStruct((db, rows, MLA_KV_RANK), BF16),
        compiler_params=_cparams(("parallel", "arbitrary")),
    )(page_table, q_full, new_pad_t, *([cache_t] * PAGES_PER_STEP))


def _rwkv_prep_kernel(rw_ref, before_ref, sh_ref, mu_ref, w0_ref, w2_ref, a0_ref, a2_ref, g2_ref,
                      kk_ref, ka_ref, rk_ref, ones_ref, xs_ref, v_ref, g_ref, rkv_ref, *, tm, seq):
    rw = rw_ref[...]
    rolled = pltpu.roll(rw, 1, axis=0)
    row = lax.broadcasted_iota(jnp.int32, rw.shape, 0)
    if seq >= tm:
        at_start = pl.program_id(0) % (seq // tm) == 0
        first = jnp.where(at_start, sh_ref[...], before_ref[7:8, :])
        prev = jnp.where(row == 0, first, rolled)
    else:
        prev = jnp.where((row & (seq - 1)) == 0, sh_ref[...], rolled)
    xs = rw + (prev - rw) * mu_ref[...]
    d = RWKV_DIM
    r = xs[:, :d]
    k = xs[:, d:2 * d]
    v = xs[:, 2 * d:3 * d]
    xwa = xs[:, 3 * d:3 * d + LANES]
    xg = xs[:, 3 * d + LANES:]
    ones = ones_ref[...]
    w_log = -_softplus(-(w0_ref[...] + _dot(jnp.tanh(xwa), w2_ref[...]))) - 0.5
    a = _sigmoid(a0_ref[...] + _dot(xwa, a2_ref[...]))
    g_ref[...] = _dot(_sigmoid(xg), g2_ref[...])
    kk = k * kk_ref[...]
    ss = _dot_exact_rhs(kk * kk, ones)
    kk = kk / jnp.maximum(jnp.sqrt(ss), 1e-12)
    k2 = k * (1.0 + (a - 1.0) * ka_ref[...])
    xs_ref[0] = -kk
    xs_ref[1] = jnp.exp(-jnp.exp(w_log))
    xs_ref[2] = kk * a
    xs_ref[3] = k2
    xs_ref[4] = r
    v_ref[...] = v
    rkv_ref[...] = _dot_exact_rhs(r * k2 * rk_ref[...], ones) * v


def _rwkv_prep(rw, shift0, w, tm, seq):
    n = rw.shape[0]
    d = RWKV_DIM
    vec = _const_spec((1, d))
    if seq >= tm:
        tiles = seq // tm
        sh = shift0.reshape(-1, 1, RWKV_PROJ)
        sh_spec = pl.BlockSpec((None, 1, RWKV_PROJ), lambda i: (i // tiles, 0, 0))
    else:
        sh = jnp.repeat(shift0, seq, axis=0)
        sh_spec = _row_spec(tm, RWKV_PROJ)
    before_spec = pl.BlockSpec((8, RWKV_PROJ), lambda i: (jnp.maximum(i * (tm // 8) - 1, 0), 0))
    return pl.pallas_call(
        functools.partial(_rwkv_prep_kernel, tm=tm, seq=seq),
        grid=(n // tm,),
        in_specs=[_row_spec(tm, RWKV_PROJ), before_spec, sh_spec, _const_spec((1, RWKV_PROJ)),
                  vec, _const_spec((LANES, d)), vec, _const_spec((LANES, d)),
                  _const_spec((RWKV_G_LORA, d)), vec, vec, vec, _const_spec((d, d))],
        out_specs=[pl.BlockSpec((5, tm, d), lambda i: (0, i, 0))] + [_row_spec(tm, d)] * 3,
        out_shape=[jax.ShapeDtypeStruct((5, n, d), F32)] + [jax.ShapeDtypeStruct((n, d), F32)] * 3,
        compiler_params=_cparams(("parallel",)),
    )(rw, rw, sh, w["mu"], w["w0"], w["w2p"], w["a0"], w["a2p"], w["g2"], w["k_k"], w["k_a"],
      w["r_k"], w["ones_bd"])


SCAN_KH = RWKV_HEAD // 2
SCAN_PAIRS = LANES // 2
SCAN_VR = RWKV_HEAD // 2


def _rwkv_scan_kernel(x_ref, v_ref, s0_ref, *rest, tc):
    y_ref, s_ref, c_sc, d_sc = rest[-4:]
    @pl.when(pl.program_id(1) == 0)
    def _():
        s_ref[...] = s0_ref[...]

    half_a = slice(0, SCAN_VR)
    half_b = slice(SCAN_VR, RWKV_HEAD)

    def both_halves(p):
        return p + pltpu.roll(p, SCAN_PAIRS, axis=1)

    def key_dot(u, w):
        return both_halves(jnp.sum(u * w, axis=0, keepdims=True))

    def first_partial(rows):
        p = s_ref[0, 0, rows, :] * x_ref[0, 0, 0, 0:1, :]
        for k in range(1, SCAN_KH):
            p = p + s_ref[0, k, rows, :] * x_ref[0, 0, 0, k:k + 1, :]
        return p

    def half_step(t, rows, sa):
        v_half = v_ref[0, t, rows, :]
        v = jnp.concatenate([v_half, v_half], axis=1)
        q = None
        y = None
        for k in range(SCAN_KH):
            s_old = s_ref[0, k, rows, :]
            qk = s_old * c_sc[k:k + 1, :]
            sn = (s_old * x_ref[1, 0, t, k:k + 1, :] + sa * x_ref[2, 0, t, k:k + 1, :]
                  + v * x_ref[3, 0, t, k:k + 1, :])
            s_ref[0, k, rows, :] = sn
            yk = sn * x_ref[4, 0, t, k:k + 1, :]
            q = qk if q is None else q + qk
            y = yk if y is None else y + yk
        return q, sa * d_sc[0:1, :] + v * d_sc[1:2, :], y

    def store_y(t, y_a, y_b):
        y_ref[0, t, half_a, :] = both_halves(y_a)[:, :SCAN_PAIRS]
        y_ref[0, t, half_b, :] = both_halves(y_b)[:, :SCAN_PAIRS]

    def step(t, carry):
        sa_a, q_b, corr_b, y_a, y_b = carry
        store_y(jnp.maximum(t - 1, 0), y_a, y_b)
        a_next = x_ref[0, 0, jnp.minimum(t + 1, tc - 1)]
        c_sc[...] = x_ref[1, 0, t] * a_next
        d_sc[0:1, :] = key_dot(x_ref[2, 0, t], a_next)
        d_sc[1:2, :] = key_dot(x_ref[3, 0, t], a_next)
        sa_b = both_halves(q_b) + corr_b
        q_a, corr_a, y_a_new = half_step(t, half_a, sa_a)
        sa_a_next = both_halves(q_a) + corr_a
        q_b_next, corr_b_next, y_b_new = half_step(t, half_b, sa_b)
        return sa_a_next, q_b_next, corr_b_next, y_a_new, y_b_new

    zero = jnp.zeros((SCAN_VR, LANES), F32)
    init = (both_halves(first_partial(half_a)), first_partial(half_b), zero, zero, zero)
    final = lax.fori_loop(0, tc, step, init)
    store_y(tc - 1, final[3], final[4])


def _rwkv_scan(xs, v, s0, after, tc):
    _, nb, t, _, _ = xs.shape
    xspec = pl.BlockSpec((5, 1, tc, SCAN_KH, LANES), lambda n, c: (0, n, c, 0, 0))
    vspec = pl.BlockSpec((1, tc, RWKV_HEAD, SCAN_PAIRS), lambda n, c: (n, c, 0, 0))
    sspec = pl.BlockSpec((1, SCAN_KH, RWKV_HEAD, LANES), lambda n, c: (n, 0, 0, 0))
    return pl.pallas_call(
        functools.partial(_rwkv_scan_kernel, tc=tc),
        grid=(nb, t // tc),
        in_specs=[xspec, vspec, sspec] + [pl.BlockSpec(memory_space=pl.ANY)] * len(after),
        out_specs=[vspec, sspec],
        out_shape=[jax.ShapeDtypeStruct(v.shape, F32), jax.ShapeDtypeStruct(s0.shape, F32)],
        scratch_shapes=[pltpu.VMEM((SCAN_KH, LANES), F32), pltpu.VMEM((8, LANES), F32)],
        compiler_params=_cparams(("parallel", "arbitrary")),
    )(xs, v, s0, *after)


def _even_out_kernel(y_ref, rkv_ref, g_ref, lng_ref, lnb_ref, ones_ref, ol_ref, wuv_ref, woa_ref,
                     wob_ref, x_ref, o_ref):
    ones = ones_ref[...]
    y = y_ref[...]
    inv = 1.0 / RWKV_HEAD
    mean = _dot_exact_rhs(y, ones) * inv
    dlt = y - mean
    var = _dot_exact_rhs(dlt * dlt, ones) * inv
    yn = dlt * lax.rsqrt(var + RWKV_LN_EPS) * lng_ref[...] + lnb_ref[...] + rkv_ref[...]
    ob = (yn * g_ref[...]).astype(BF16)
    pair = 2 * MLA_KV_RANK
    oa = jnp.concatenate(
        [_dot(ol_ref[:, p * pair:(p + 1) * pair], wuv_ref[p]) for p in range(MLA_HEADS // 2)], axis=1)
    o_ref[...] = x_ref[...] + _dot(oa, woa_ref[...]) + _dot(ob, wob_ref[...])


def _even_out(y, rkv, g, o_lat, x, w, tm):
    n = x.shape[0]
    d = RWKV_DIM
    hq = MLA_HEADS * MLA_KV_RANK
    return pl.pallas_call(
        _even_out_kernel,
        grid=(n // tm,),
        in_specs=[_row_spec(tm, d), _row_spec(tm, d), _row_spec(tm, d), _const_spec((1, d)),
                  _const_spec((1, d)), _const_spec((d, d)), _row_spec(tm, hq),
                  _const_spec((MLA_HEADS // 2, 2 * MLA_KV_RANK, 2 * MLA_V)),
                  _const_spec((MLA_HEADS * MLA_V, D_MODEL)), _const_spec((d, D_MODEL)),
                  _row_spec(tm, D_MODEL)],
        out_specs=_row_spec(tm, D_MODEL),
        out_shape=jax.ShapeDtypeStruct((n, D_MODEL), F32),
        compiler_params=_cparams(("parallel",)),
    )(y, rkv, g, w["ln_g"], w["ln_b"], w["ones_bd"], o_lat, w["w_uv_bd"], w["w_out_a"],
      w["w_out_b"], x)


FFN_TF = 1408


def _ffn_kernel(x_ref, g_ref, wg_ref, wu_ref, wd_ref, o_ref, xn_sc, acc_sc):
    f = pl.program_id(1)

    @pl.when(f == 0)
    def _():
        xn_sc[...] = _rms(x_ref[...], g_ref[...]).astype(BF16)
        acc_sc[...] = jnp.zeros(acc_sc.shape, F32)

    xn = xn_sc[...]
    gate = _dot(xn, wg_ref[...])
    up = _dot(xn, wu_ref[...])
    acc_sc[...] += _dot(gate * _sigmoid(gate) * up, wd_ref[...])

    @pl.when(f == pl.num_programs(1) - 1)
    def _():
        o_ref[...] = x_ref[...] + acc_sc[...]


def _ffn(x, g, w_gu, w_down, tm):
    n = x.shape[0]
    nf = D_FF // FFN_TF
    return pl.pallas_call(
        _ffn_kernel,
        grid=(n // tm, nf),
        in_specs=[pl.BlockSpec((tm, D_MODEL), lambda i, f: (i, 0)),
                  pl.BlockSpec((1, D_MODEL), lambda i, f: (0, 0)),
                  pl.BlockSpec((D_MODEL, FFN_TF), lambda i, f: (0, f)),
                  pl.BlockSpec((D_MODEL, FFN_TF), lambda i, f: (0, nf + f)),
                  pl.BlockSpec((FFN_TF, D_MODEL), lambda i, f: (f, 0))],
        out_specs=pl.BlockSpec((tm, D_MODEL), lambda i, f: (i, 0)),
        out_shape=jax.ShapeDtypeStruct((n, D_MODEL), F32),
        scratch_shapes=[pltpu.VMEM((tm, D_MODEL), BF16), pltpu.VMEM((tm, D_MODEL), F32)],
        compiler_params=_cparams(("parallel", "arbitrary")),
    )(x, g, w_gu, w_gu, w_down)


def _odd_in_kernel(x_ref, g_ref, wq_ref, wk_ref, wv_ref, wg_ref, wxa_ref, a2_ref, ab_ref,
                   q_ref, k_ref, v_ref, gate_ref, la_ref):
    xn = _rms(x_ref[...], g_ref[...]).astype(BF16)
    q_ref[...] = _dot(xn, wq_ref[...]) * (GLA_DK ** -0.5)
    k_ref[...] = _dot(xn, wk_ref[...])
    v_ref[...] = _dot(xn, wv_ref[...])
    gate_ref[...] = _dot(xn, wg_ref[...])
    z = _dot(_dot(xn, wxa_ref[...]), a2_ref[...]) + ab_ref[...]
    la_ref[...] = -_softplus(-z) * (1.0 / GLA_GATE_NORM)


def _odd_in(x, w, tm):
    n = x.shape[0]
    return pl.pallas_call(
        _odd_in_kernel,
        grid=(n // tm,),
        in_specs=[_row_spec(tm, D_MODEL), _const_spec((1, D_MODEL)),
                  _const_spec((D_MODEL, GLA_KDIM)), _const_spec((D_MODEL, GLA_KDIM)),
                  _const_spec((D_MODEL, GLA_VDIM)), _const_spec((D_MODEL, GLA_VDIM)),
                  _const_spec((D_MODEL, LANES)), _const_spec((LANES, GLA_KDIM)),
                  _const_spec((1, GLA_KDIM))],
        out_specs=[_row_spec(tm, GLA_KDIM), _row_spec(tm, GLA_KDIM), _row_spec(tm, GLA_VDIM),
                   _row_spec(tm, GLA_VDIM), _row_spec(tm, GLA_KDIM)],
        out_shape=[jax.ShapeDtypeStruct((n, GLA_KDIM), F32), jax.ShapeDtypeStruct((n, GLA_KDIM), F32),
                   jax.ShapeDtypeStruct((n, GLA_VDIM), F32), jax.ShapeDtypeStruct((n, GLA_VDIM), F32),
                   jax.ShapeDtypeStruct((n, GLA_KDIM), F32)],
        compiler_params=_cparams(("parallel",)),
    )(x, w["norm_mix"], w["w_q"], w["w_k"], w["w_v"], w["w_g"], w["w_xa"], w["a2p"], w["ab"])


def _gla_kernel(q_ref, k_ref, v_ref, la_ref, s0_ref, o_ref, st_ref):
    c = GLA_CHUNK

    @pl.when(pl.program_id(1) == 0)
    def _():
        st_ref[...] = s0_ref[...]

    row = lax.broadcasted_iota(jnp.int32, (c, c), 0)
    col = lax.broadcasted_iota(jnp.int32, (c, c), 1)
    tri = row >= col
    tri_b = jnp.where(tri, 1.0, 0.0).astype(BF16)
    for h in range(GLA_HEADS):
        ks = slice(h * GLA_DK, (h + 1) * GLA_DK)
        vs = slice(h * GLA_DV, (h + 1) * GLA_DV)
        b = _dot_exact_lhs(tri_b, la_ref[:, ks])
        q = q_ref[:, ks]
        k = k_ref[:, ks]
        v = v_ref[:, vs]
        b_end = b[c - 1:c, :]
        qe = (q * jnp.exp(b)).astype(BF16)
        ke = (k * jnp.exp(-b)).astype(BF16)
        a_mat = jnp.where(tri, _dot_nt(qe, ke), 0.0)
        st = st_ref[0, h]
        o_ref[:, vs] = _dot_nt(qe, st) + _dot(a_mat, v)
        k_end = k * jnp.exp(b_end - b)
        st_ref[0, h] = st * jnp.exp(b_end) + _dot(v.T, k_end)


def _gla(q, k, v, la, s0t, batch, seq):
    nc = seq // GLA_CHUNK
    rspec = lambda width: pl.BlockSpec((GLA_CHUNK, width), lambda b, c: (b * nc + c, 0))
    sspec = pl.BlockSpec((1, GLA_HEADS, GLA_DV, GLA_DK), lambda b, c: (b, 0, 0, 0))
    return pl.pallas_call(
        _gla_kernel,
        grid=(batch, nc),
        in_specs=[rspec(GLA_KDIM), rspec(GLA_KDIM), rspec(GLA_VDIM), rspec(GLA_KDIM), sspec],
        out_specs=[rspec(GLA_VDIM), sspec],
        out_shape=[jax.ShapeDtypeStruct(v.shape, F32), jax.ShapeDtypeStruct(s0t.shape, F32)],
        compiler_params=_cparams(("parallel", "arbitrary")),
    )(q, k, v, la, s0t)


def _odd_out_kernel(o_ref, gate_ref, gn_ref, wo_ref, x_ref, y_ref):
    parts = []
    for h in range(GLA_HEADS):
        vs = slice(h * GLA_DV, (h + 1) * GLA_DV)
        parts.append(_rms(o_ref[:, vs], gn_ref[:, vs]))
    gate = gate_ref[...]
    on = jnp.concatenate(parts, axis=1) * (gate * _sigmoid(gate))
    y_ref[...] = x_ref[...] + _dot(on, wo_ref[...])


def _odd_out(o, gate, x, w, tm):
    n = x.shape[0]
    return pl.pallas_call(
        _odd_out_kernel,
        grid=(n // tm,),
        in_specs=[_row_spec(tm, GLA_VDIM), _row_spec(tm, GLA_VDIM), _const_spec((1, GLA_VDIM)),
                  _const_spec((GLA_VDIM, D_MODEL)), _row_spec(tm, D_MODEL)],
        out_specs=_row_spec(tm, D_MODEL),
        out_shape=jax.ShapeDtypeStruct((n, D_MODEL), F32),
        compiler_params=_cparams(("parallel",)),
    )(o, gate, w["gla_norm"], w["w_out"], x)


def _router_kernel(x_ref, g_ref, wr_ref, *rest):
    xn_ref, idx_ref, gate_ref = rest[-3:]
    xn = _rms(x_ref[...], g_ref[...])
    half = D_MODEL // 2
    xn_ref[0] = xn[:, :half]
    xn_ref[1] = xn[:, half:]
    logits = _dot_f32ish(xn, wr_ref[...])
    lane = lax.broadcasted_iota(jnp.int32, logits.shape, 1)
    logits = jnp.where(lane < N_EXPERTS, logits, NEG_BIG)
    m1 = jnp.max(logits, axis=-1, keepdims=True)
    i1 = jnp.min(jnp.where(logits == m1, lane, LANES), axis=-1, keepdims=True)
    rest = jnp.where(lane == i1, NEG_BIG, logits)
    m2 = jnp.max(rest, axis=-1, keepdims=True)
    i2 = jnp.min(jnp.where(rest == m2, lane, LANES), axis=-1, keepdims=True)
    e2 = jnp.exp(m2 - m1)
    g1 = 1.0 / (1.0 + e2)
    g2 = e2 / (1.0 + e2)
    idx_ref[...] = jnp.where(lane == 0, i1, jnp.where(lane == 1, i2, 0))
    gate_ref[...] = jnp.where(lane == 0, g1, jnp.where(lane == 1, g2, 0.0))


def _router(x, g, wr, tm, n_total, row0, prev=None):
    n = x.shape[0]
    half = D_MODEL // 2
    blk0 = row0 // tm
    prev = () if prev is None else tuple(prev)
    return pl.pallas_call(
        _router_kernel,
        grid=(n // tm,),
        in_specs=[_row_spec(tm, D_MODEL), _const_spec((1, D_MODEL)), _const_spec((D_MODEL, LANES))]
        + [pl.BlockSpec(memory_space=pl.ANY)] * len(prev),
        out_specs=[pl.BlockSpec((2, tm, half), lambda i: (0, blk0 + i, 0)),
                   pl.BlockSpec((tm, LANES), lambda i: (blk0 + i, 0)),
                   pl.BlockSpec((tm, LANES), lambda i: (blk0 + i, 0))],
        out_shape=[jax.ShapeDtypeStruct((2, n_total, half), F32),
                   jax.ShapeDtypeStruct((n_total, LANES), jnp.int32),
                   jax.ShapeDtypeStruct((n_total, LANES), F32)],
        input_output_aliases={3 + k: k for k in range(len(prev))},
        compiler_params=_cparams(("parallel",)),
    )(x, g, wr, *prev)


MOE_TF = 1792
MOE_TMC = 256


def _route(top_i, tm, tmc):
    n = top_i.shape[0]
    slots = 2 * n
    n_tiles = -(-(slots + N_EXPERTS * (tm - 1)) // tm)
    win = tmc + 8
    e_flat = top_i.reshape(-1)
    onehot = (e_flat[:, None] == jnp.arange(N_EXPERTS, dtype=jnp.int32)[None, :]).astype(jnp.int32)
    csum = jnp.cumsum(onehot, axis=0)
    rank = jnp.sum(onehot * csum, axis=1) - 1
    counts = csum[-1]
    padded = ((counts + tm - 1) // tm) * tm
    ends = jnp.cumsum(padded)
    starts = ends - padded
    dest = (jnp.sum(onehot * starts[None, :], axis=1) + rank).astype(jnp.int32)
    tile_start = jnp.arange(n_tiles, dtype=jnp.int32) * tm
    tile_expert = jnp.minimum(jnp.sum((tile_start[:, None] >= ends[None, :]).astype(jnp.int32), axis=1),
                              N_EXPERTS - 1).astype(jnp.int32)
    tile_valid = (tile_start < ends[-1]).astype(jnp.int32)
    src = jnp.zeros((n_tiles * tm,), jnp.int32).at[dest].set(jnp.arange(slots, dtype=jnp.int32) // 2)
    before = jnp.concatenate([jnp.zeros((1, N_EXPERTS), jnp.int32), csum[2 * tmc - 1:-1:2 * tmc]], axis=0)
    wstart = jnp.clip(((starts[None, :] + before) // 8) * 8, 0, n_tiles * tm - win).astype(jnp.int32)
    ws_slot = jnp.sum(onehot * jnp.repeat(wstart, 2 * tmc, axis=0), axis=1)
    local = (e_flat * win + dest - ws_slot).astype(jnp.int32)
    return local, wstart.reshape(-1), src, tile_expert, tile_valid


def _moe_gather_kernel(src_ref, x_ref, o_ref, *, tg):
    base = pl.program_id(1) * tg

    def body(r, carry):
        o_ref[pl.ds(r, 1), :] = x_ref[pl.ds(src_ref[base + r], 1), :]
        return carry

    lax.fori_loop(0, tg, body, 0, unroll=8)


def _moe_gather(src, xn2, tg):
    rows = src.shape[0]
    _, n, half = xn2.shape
    grid_spec = pltpu.PrefetchScalarGridSpec(
        num_scalar_prefetch=1,
        grid=(2, rows // tg),
        in_specs=[pl.BlockSpec((None, n, half), lambda h, i, s: (h, 0, 0),
                               pipeline_mode=pl.Buffered(1))],
        out_specs=pl.BlockSpec((tg, half), lambda h, i, s: (i, h)),
    )
    return pl.pallas_call(
        functools.partial(_moe_gather_kernel, tg=tg),
        grid_spec=grid_spec,
        out_shape=jax.ShapeDtypeStruct((rows, 2 * half), F32),
        compiler_params=_cparams(("arbitrary", "arbitrary")),
    )(src, xn2)


def _moe_up_kernel(te_ref, tv_ref, xs_ref, wg_ref, wu_ref, h_ref):
    @pl.when(tv_ref[pl.program_id(1)] != 0)
    def _():
        xs = xs_ref[...].astype(BF16)
        gate = _dot(xs, wg_ref[...])
        up = _dot(xs, wu_ref[...])
        h_ref[...] = (gate * _sigmoid(gate) * up).astype(BF16)


def _moe_up(te, tv, xs, w_gu, layer, tm):
    rows = xs.shape[0]
    nf = D_FF_EXPERT // MOE_TF
    wspec = lambda off: pl.BlockSpec((None, None, D_MODEL, MOE_TF),
                                     lambda f, t, te, tv: (layer, te[t], 0, off + f))
    grid_spec = pltpu.PrefetchScalarGridSpec(
        num_scalar_prefetch=2,
        grid=(nf, rows // tm),
        in_specs=[pl.BlockSpec((tm, D_MODEL), lambda f, t, te, tv: (t, 0)), wspec(0), wspec(nf)],
        out_specs=pl.BlockSpec((tm, MOE_TF), lambda f, t, te, tv: (t, f)),
    )
    return pl.pallas_call(
        _moe_up_kernel,
        grid_spec=grid_spec,
        out_shape=jax.ShapeDtypeStruct((rows, D_FF_EXPERT), BF16),
        compiler_params=_cparams(("arbitrary", "arbitrary")),
    )(te, tv, xs, w_gu, w_gu)


def _moe_down_kernel(te_ref, tv_ref, h_ref, wd_ref, y_ref):
    @pl.when(tv_ref[pl.program_id(0)] != 0)
    def _():
        y_ref[...] = _dot(h_ref[...], wd_ref[...])


def _moe_down(te, tv, h, w_down, layer, tm):
    rows = h.shape[0]
    grid_spec = pltpu.PrefetchScalarGridSpec(
        num_scalar_prefetch=2,
        grid=(rows // tm,),
        in_specs=[pl.BlockSpec((tm, D_FF_EXPERT), lambda t, te, tv: (t, 0)),
                  pl.BlockSpec((None, None, D_FF_EXPERT, D_MODEL),
                               lambda t, te, tv: (layer, te[t], 0, 0))],
        out_specs=pl.BlockSpec((tm, D_MODEL), lambda t, te, tv: (t, 0)),
    )
    return pl.pallas_call(
        _moe_down_kernel,
        grid_spec=grid_spec,
        out_shape=jax.ShapeDtypeStruct((rows, D_MODEL), F32),
        compiler_params=_cparams(("arbitrary",)),
    )(te, tv, h, w_down)


def _moe_combine_kernel(ws_ref, local_ref, *refs, tmc, win, tile0):
    win_refs = refs[:N_EXPERTS]
    g1_ref, g2_ref, x_ref, fn_ref, o_ref, buf = refs[N_EXPERTS:]
    for e in range(N_EXPERTS):
        buf[e * win:(e + 1) * win, :] = win_refs[e][...]
    base = 2 * (tile0 + pl.program_id(0)) * tmc

    def body(r, carry):
        row = pl.ds(r, 1)
        y1 = buf[pl.ds(local_ref[base + 2 * r], 1), :]
        y2 = buf[pl.ds(local_ref[base + 2 * r + 1], 1), :]
        g1 = g1_ref[row, :]
        g2 = g2_ref[row, :]
        parts = []
        for c in range(D_MODEL // LANES):
            cs = slice(c * LANES, (c + 1) * LANES)
            parts.append(g1 * y1[:, cs] + g2 * y2[:, cs])
        o_ref[row, :] = x_ref[row, :] + jnp.concatenate(parts, axis=1)
        return carry

    lax.fori_loop(0, tmc, body, 0, unroll=4)
    o_ref[...] = _rms(o_ref[...], fn_ref[...])


def _moe_combine(local, wstart, ys, g1b, g2b, x, fn, tmc, row0):
    n = x.shape[0]
    win = tmc + 8
    tile0 = row0 // tmc

    def win_spec(e):
        return pl.BlockSpec(
            (pl.Element(win), pl.Element(D_MODEL)),
            lambda i, ws, lo: (pl.multiple_of(ws[(tile0 + i) * N_EXPERTS + e], 8), 0))

    grid_spec = pltpu.PrefetchScalarGridSpec(
        num_scalar_prefetch=2,
        grid=(n // tmc,),
        in_specs=[win_spec(e) for e in range(N_EXPERTS)]
        + [pl.BlockSpec((tmc, LANES), lambda i, ws, lo: (tile0 + i, 0)),
           pl.BlockSpec((tmc, LANES), lambda i, ws, lo: (tile0 + i, 0)),
           pl.BlockSpec((tmc, D_MODEL), lambda i, ws, lo: (i, 0)),
           pl.BlockSpec((1, D_MODEL), lambda i, ws, lo: (0, 0))],
        out_specs=pl.BlockSpec((tmc, D_MODEL), lambda i, ws, lo: (i, 0)),
        scratch_shapes=[pltpu.VMEM((N_EXPERTS * win, D_MODEL), F32)],
    )
    return pl.pallas_call(
        functools.partial(_moe_combine_kernel, tmc=tmc, win=win, tile0=tile0),
        grid_spec=grid_spec,
        out_shape=jax.ShapeDtypeStruct((n, D_MODEL), F32),
        compiler_params=_cparams(("arbitrary",)),
    )(wstart, local, *([ys] * N_EXPERTS), g1b, g2b, x, fn)


def _scan_vec_layout(xs, batch, seq):
    nb = batch * RWKV_HEADS // SCAN_PAIRS
    x = xs.reshape(5, batch, seq, RWKV_HEADS, 2, SCAN_KH).transpose(0, 2, 5, 4, 1, 3)
    x = x.reshape(5, seq, SCAN_KH, 2, nb, SCAN_PAIRS).transpose(0, 4, 1, 2, 3, 5)
    return x.reshape(5, nb, seq, SCAN_KH, LANES)


def _scan_val_layout(v, batch, seq):
    nb = batch * RWKV_HEADS // SCAN_PAIRS
    v4 = v.reshape(batch, seq, RWKV_HEADS, RWKV_HEAD).transpose(1, 3, 0, 2)
    return v4.reshape(seq, RWKV_HEAD, nb, SCAN_PAIRS).transpose(2, 0, 1, 3)


def _scan_val_unlayout(y, batch, seq):
    v4 = y.transpose(1, 2, 0, 3).reshape(seq, RWKV_HEAD, batch, RWKV_HEADS)
    return v4.transpose(2, 0, 3, 1).reshape(batch * seq, RWKV_DIM)


def _scan_state_layout(s, batch):
    nb = batch * RWKV_HEADS // SCAN_PAIRS
    s6 = s.reshape(batch, RWKV_HEADS, RWKV_HEAD, 2, SCAN_KH).transpose(4, 2, 3, 0, 1)
    s6 = s6.reshape(SCAN_KH, RWKV_HEAD, 2, nb, SCAN_PAIRS).transpose(3, 0, 1, 2, 4)
    return s6.reshape(nb, SCAN_KH, RWKV_HEAD, LANES)


def _scan_state_unlayout(arr, batch):
    nb = arr.shape[0]
    s = arr.reshape(nb, SCAN_KH, RWKV_HEAD, 2, SCAN_PAIRS).transpose(0, 4, 2, 3, 1)
    return s.reshape(batch, RWKV_HEADS, RWKV_HEAD, RWKV_HEAD)


def _swap_halves(w):
    half = w.shape[-1] // 2
    return jnp.concatenate([w[..., half:], w[..., :half]], axis=-1)


def _prep_even(i, norm_mix, norm_ffn, w_in, q_norm, kv_norm, w_uq, w_uk, w_uv, mu, w0, w2, a0, a2,
               g2, k_k, k_a, r_k, ln_g, ln_b, w_out, ffn_gu, ffn_down):
    w = {}
    row = lambda v: v[i].reshape(1, -1)
    w_in = w_in[i]
    w["norm_mix"] = row(norm_mix)
    w["norm_ffn"] = row(norm_ffn)
    w["w_q"] = w_in[:, :MLA_Q_RANK].astype(BF16)
    w_kv = w_in[:, MLA_Q_RANK:MLA_Q_RANK + MLA_LAT]
    w["w_ckv"] = w_kv[:, :MLA_KV_RANK].astype(BF16)
    lane_pad = lambda m: jnp.pad(m, [(0, 0)] * (m.ndim - 1) + [(0, LANES - m.shape[-1])])
    w["w_pe_a"] = lane_pad(w_kv[:, MLA_KV_RANK:]).astype(BF16)
    w["w_pe_b"] = lane_pad(_swap_halves(w_kv[:, MLA_KV_RANK:])).astype(BF16)
    w["w_rw"] = w_in[:, MLA_Q_RANK + MLA_LAT:].astype(BF16)
    w["q_norm"] = row(q_norm)
    w["kv_norm"] = row(kv_norm)
    uq = w_uq[i].reshape(MLA_Q_RANK, MLA_HEADS, MLA_NOPE + MLA_ROPE)
    uq_pe = uq[:, :, MLA_NOPE:]
    w["w_qpe_a"] = lane_pad(uq_pe).reshape(MLA_Q_RANK, -1).astype(BF16)
    w["w_qpe_b"] = lane_pad(_swap_halves(uq_pe)).reshape(MLA_Q_RANK, -1).astype(BF16)
    w["w_qlat"] = _fold_qlat(uq[:, :, :MLA_NOPE].transpose(1, 0, 2), w_uk[i].transpose(1, 0, 2))
    uv = w_uv[i].transpose(1, 0, 2).reshape(MLA_HEADS // 2, 2, MLA_KV_RANK, MLA_V)
    zero = jnp.zeros_like(uv[:, 0])
    w["w_uv_bd"] = jnp.concatenate(
        [jnp.concatenate([uv[:, 0], zero], axis=-1), jnp.concatenate([zero, uv[:, 1]], axis=-1)],
        axis=1).astype(BF16)
    w["mu"] = row(mu)
    w["w0"] = row(w0)
    pad = lambda m, before: jnp.pad(m, ((before, LANES - before - m.shape[0]), (0, 0))).astype(BF16)
    w["w2p"] = pad(w2[i], 0)
    w["a2p"] = pad(a2[i], RWKV_W_LORA)
    w["a0"] = row(a0)
    w["g2"] = g2[i].astype(BF16)
    w["k_k"] = row(k_k)
    w["k_a"] = row(k_a)
    w["r_k"] = row(r_k)
    w["ln_g"] = row(ln_g)
    w["ln_b"] = row(ln_b)
    head = jnp.arange(RWKV_DIM) // RWKV_HEAD
    w["ones_bd"] = (head[:, None] == head[None, :]).astype(BF16)
    w["w_out_a"] = w_out[i][:MLA_HEADS * MLA_V].astype(BF16)
    w["w_out_b"] = w_out[i][MLA_HEADS * MLA_V:].astype(BF16)
    w["ffn_gu"] = ffn_gu[i].astype(BF16)
    w["ffn_down"] = ffn_down[i].astype(BF16)
    return w


def _prep_odd(i, norm_mix, norm_ffn, w_in, a2, ab, gla_norm, w_out, router, moe_gu, moe_down):
    w = {}
    row = lambda v: v[i].reshape(1, -1)
    w_in = w_in[i]
    w["norm_mix"] = row(norm_mix)
    w["norm_ffn"] = row(norm_ffn)
    w["w_q"] = w_in[:, :GLA_KDIM].astype(BF16)
    w["w_k"] = w_in[:, GLA_KDIM:2 * GLA_KDIM].astype(BF16)
    w["w_v"] = w_in[:, 2 * GLA_KDIM:2 * GLA_KDIM + GLA_VDIM].astype(BF16)
    w["w_g"] = w_in[:, 2 * GLA_KDIM + GLA_VDIM:2 * GLA_KDIM + 2 * GLA_VDIM].astype(BF16)
    w["w_xa"] = jnp.pad(w_in[:, 2 * GLA_KDIM + 2 * GLA_VDIM:],
                        ((0, 0), (0, LANES - GLA_GATE_RANK))).astype(BF16)
    w["a2p"] = jnp.pad(a2[i], ((0, LANES - GLA_GATE_RANK), (0, 0))).astype(BF16)
    w["ab"] = row(ab)
    w["gla_norm"] = row(gla_norm)
    w["w_out"] = w_out[i].astype(BF16)
    w["router"] = jnp.pad(router[i], ((0, 0), (0, LANES - N_EXPERTS)))
    w["layer"] = i
    w["moe_gu"] = moe_gu
    w["moe_down"] = moe_down
    return w


def _rope_tables(pos, reps):
    inv = ROPE_THETA ** (-jnp.arange(0, MLA_ROPE, 2, dtype=F32) / MLA_ROPE)
    ang = pos.astype(F32)[:, None] * inv[None, :]
    cos, sin = jnp.cos(ang), jnp.sin(ang)
    pad = ((0, 0), (0, LANES - MLA_ROPE))
    cs = jnp.tile(jnp.pad(jnp.concatenate([cos, cos], axis=-1), pad), (reps, 1))
    sn = jnp.tile(jnp.pad(jnp.concatenate([-sin, sin], axis=-1), pad), (reps, 1))
    return {"cs": cs, "sn": sn, "cs8": jnp.tile(cs, (1, MLA_HEADS)), "sn8": jnp.tile(sn, (1, MLA_HEADS))}


def _even_layer(x, batch, seq, tabs, state, shift0, past, w, tm, tc, after=()):
    n = batch * seq
    lat, lat_b, q_lat, q_pe, rw = _even_in(x, w, tabs, tm)
    if past is None:
        o_lat = _mla_prompt(q_lat, q_pe, lat_b, batch, seq)
    else:
        cache, layer, page_table = past
        rows = seq * MLA_HEADS
        q_full = jnp.concatenate([q_lat.reshape(batch, rows, MLA_KV_RANK),
                                  q_pe.reshape(batch, rows, LANES)[:, :, :MLA_ROPE]], axis=-1)
        new_pad_t = jnp.pad(lat_b.reshape(batch, seq, MLA_LATB)[:, :, :MLA_LAT],
                            ((0, 0), (0, PAGE_SIZE - seq), (0, 0))).transpose(0, 2, 1)
        o_lat = _mla_decode(page_table, q_full, new_pad_t, cache.transpose(0, 1, 3, 2), layer)
        o_lat = o_lat.reshape(n, MLA_HEADS * MLA_KV_RANK)

    rw3 = rw.reshape(batch, seq, RWKV_PROJ)
    xs5, v, g, rkv = _rwkv_prep(rw, shift0, w, tm, seq)
    y_l, s_l = _rwkv_scan(_scan_vec_layout(xs5, batch, seq), _scan_val_layout(v, batch, seq),
                          _scan_state_layout(state, batch), (o_lat,) + tuple(after), tc)
    y = _scan_val_unlayout(y_l, batch, seq)
    new_state = _scan_state_unlayout(s_l, batch)

    x = _even_out(y, rkv, g, o_lat, x, w, tm)
    x = _ffn(x, w["norm_ffn"], w["ffn_gu"], w["ffn_down"], tm)
    return x, lat.reshape(batch, seq, MLA_LAT), new_state, rw3[:, -1], o_lat


def _odd_mixer_layer(x, batch, seq, state, w, tm):
    q, k, v, gate, la = _odd_in(x, w, tm)
    seq_p = -(-seq // GLA_CHUNK) * GLA_CHUNK
    if seq_p != seq:
        padr = lambda t: jnp.pad(t.reshape(batch, seq, -1), ((0, 0), (0, seq_p - seq), (0, 0))
                                 ).reshape(batch * seq_p, -1)
        qp, kp, vp, lap = padr(q), padr(k), padr(v), padr(la)
    else:
        qp, kp, vp, lap = q, k, v, la
    o, st = _gla(qp, kp, vp, lap, state.transpose(0, 1, 3, 2), batch, seq_p)
    if seq_p != seq:
        o = o.reshape(batch, seq_p, GLA_VDIM)[:, :seq].reshape(batch * seq, GLA_VDIM)
    return _odd_out(o, gate, x, w, tm), st.transpose(0, 1, 3, 2)


def _moe_all_groups(xs_groups, tms, w, final_norm, tm_moe):
    sizes = [x.shape[0] for x in xs_groups]
    n_total = sum(sizes)
    bufs, row0 = None, 0
    for x, tm in zip(xs_groups, tms):
        bufs = _router(x, w["norm_ffn"], w["router"], tm, n_total, row0, bufs)
        row0 += x.shape[0]
    xn2, idx, gates = bufs
    local, wstart, src, tile_expert, tile_valid = _route(idx[:, :2], tm_moe, MOE_TMC)
    rows = _moe_gather(src, xn2, tm_moe)
    h = _moe_up(tile_expert, tile_valid, rows, w["moe_gu"], w["layer"], tm_moe)
    ys = _moe_down(tile_expert, tile_valid, h, w["moe_down"], w["layer"], tm_moe)
    g1b = jnp.broadcast_to(gates[:, 0:1], (n_total, LANES))
    g2b = jnp.broadcast_to(gates[:, 1:2], (n_total, LANES))
    outs, row0 = [], 0
    for x in xs_groups:
        outs.append(_moe_combine(local, wstart, ys, g1b, g2b, x, final_norm, MOE_TMC, row0))
        row0 += x.shape[0]
    return outs


def kernel(x_prompt, x_sample, cache_mla, state_rwkv, state_rwkv_shift, state_gla, page_table, norm_mix_even, norm_ffn_even, w_in_even, mla_q_norm, mla_kv_norm, mla_w_uq, mla_w_uk, mla_w_uv, rwkv_mu, rwkv_w0, rwkv_w2, rwkv_a0, rwkv_a2, rwkv_g2, rwkv_k_k, rwkv_k_a, rwkv_r_k, rwkv_ln_g, rwkv_ln_b, w_out_even, ffn_w_gu_even, ffn_w_down_even, norm_mix_odd, norm_ffn_odd, w_in_odd, gla_a2, gla_ab, gla_norm, w_out_odd, moe_router, moe_w_gu, moe_w_down, final_norm):
    bp, tp, _ = x_prompt.shape
    bs, ts, _ = x_sample.shape
    past_len = page_table.shape[1] * PAGE_SIZE
    tm_p, tm_s = 512, bs * ts
    we = _prep_even(0, norm_mix_even, norm_ffn_even, w_in_even, mla_q_norm, mla_kv_norm, mla_w_uq,
                    mla_w_uk, mla_w_uv, rwkv_mu, rwkv_w0, rwkv_w2, rwkv_a0, rwkv_a2, rwkv_g2,
                    rwkv_k_k, rwkv_k_a, rwkv_r_k, rwkv_ln_g, rwkv_ln_b, w_out_even, ffn_w_gu_even,
                    ffn_w_down_even)
    wo = _prep_odd(0, norm_mix_odd, norm_ffn_odd, w_in_odd, gla_a2, gla_ab, gla_norm, w_out_odd,
                   moe_router, moe_w_gu, moe_w_down)
    fn = final_norm.reshape(1, -1)
    tabs_p = _rope_tables(jnp.arange(tp), 1)
    tabs_s = _rope_tables(past_len + jnp.arange(ts), bs)

    hp = x_prompt.reshape(bp * tp, D_MODEL)
    hs = x_sample.reshape(bs * ts, D_MODEL)
    zeros_state = jnp.zeros((bp, RWKV_HEADS, RWKV_HEAD, RWKV_HEAD), F32)
    zeros_shift = jnp.zeros((bp, RWKV_PROJ), F32)
    hp, lat_p, rs_p, sh_p, _ = _even_layer(hp, bp, tp, tabs_p, zeros_state, zeros_shift, None, we,
                                           tm_p, 64)
    hs, lat_s, rs_s, sh_s, _ = _even_layer(hs, bs, ts, tabs_s, state_rwkv[0], state_rwkv_shift[0],
                                           (cache_mla, 0, page_table), we, tm_s, ts)
    zeros_gla = jnp.zeros((bp, GLA_HEADS, GLA_DK, GLA_DV), F32)
    hp, gs_p = _odd_mixer_layer(hp, bp, tp, zeros_gla, wo, tm_p)
    hs, gs_s = _odd_mixer_layer(hs, bs, ts, state_gla[0], wo, tm_s)
    yp, ys = _moe_all_groups([hp, hs], [tm_p, tm_s], wo, fn, 512)
    return (yp.reshape(bp, tp, D_MODEL), ys.reshape(bs, ts, D_MODEL), lat_p[None], lat_s[None],
            rs_p[None], rs_s[None], sh_p[None], sh_s[None], gs_p[None], gs_s[None])
```

```python
import functools

import jax
import jax.numpy as jnp
from jax import lax
from jax.experimental import pallas as pl
from jax.experimental.pallas import tpu as pltpu

F32 = jnp.float32
BF16 = jnp.bfloat16

D_MODEL = 1024
PAGE_SIZE = 128
NORM_EPS = 1e-6

MLA_HEADS = 8
MLA_NOPE = 64
MLA_ROPE = 32
MLA_V = 64
MLA_Q_RANK = 384
MLA_KV_RANK = 256
MLA_LAT = MLA_KV_RANK + MLA_ROPE
MLA_LATB = MLA_KV_RANK + 128
MLA_SCALE = (MLA_NOPE + MLA_ROPE) ** -0.5
ROPE_THETA = 10000.0

RWKV_HEADS = 8
RWKV_HEAD = 64
RWKV_DIM = RWKV_HEADS * RWKV_HEAD
RWKV_W_LORA = 64
RWKV_A_LORA = 64
RWKV_G_LORA = 128
RWKV_PROJ = 3 * RWKV_DIM + RWKV_W_LORA + RWKV_A_LORA + RWKV_G_LORA
RWKV_LN_EPS = 64e-5

GLA_HEADS = 4
GLA_DK = 128
GLA_DV = 256
GLA_KDIM = GLA_HEADS * GLA_DK
GLA_VDIM = GLA_HEADS * GLA_DV
GLA_GATE_RANK = 16
GLA_GATE_NORM = 16.0
GLA_CHUNK = 128

D_FF = 2816
N_EXPERTS = 8
D_FF_EXPERT = 3584

LANES = 128
VMEM_LIMIT = 56 * 1024 * 1024
NEG_BIG = -1e30
LOG2_E = 1.4426950408889634
Q_PRESCALE = MLA_SCALE * LOG2_E


def _cparams(sem):
    return pltpu.CompilerParams(dimension_semantics=sem, vmem_limit_bytes=VMEM_LIMIT)


def _const_spec(shape):
    nd = len(shape)
    return pl.BlockSpec(shape, lambda *_: (0,) * nd)


def _row_spec(tm, width):
    return pl.BlockSpec((tm, width), lambda i: (i, 0))


def _dot(a, b):
    return jnp.dot(a.astype(BF16), b.astype(BF16), preferred_element_type=F32)


def _dot_nt(a, b):
    return lax.dot_general(a.astype(BF16), b.astype(BF16), (((1,), (1,)), ((), ())),
                           preferred_element_type=F32)


def _split2(x):
    hi = x.astype(BF16)
    lo = (x - hi.astype(F32)).astype(BF16)
    return hi, lo


def _split3(x):
    hi = x.astype(BF16)
    r1 = x - hi.astype(F32)
    mid = r1.astype(BF16)
    lo = (r1 - mid.astype(F32)).astype(BF16)
    return hi, mid, lo


def _dot_exact_rhs(x, e):
    hi, mid, lo = _split3(x)
    return (jnp.dot(hi, e, preferred_element_type=F32) + jnp.dot(mid, e, preferred_element_type=F32)
            + jnp.dot(lo, e, preferred_element_type=F32))


def _dot_exact_lhs(e, x):
    hi, mid, lo = _split3(x)
    return (jnp.dot(e, hi, preferred_element_type=F32) + jnp.dot(e, mid, preferred_element_type=F32)
            + jnp.dot(e, lo, preferred_element_type=F32))


def _dot_f32ish(a, b):
    ah, al = _split2(a)
    bh, bl = _split2(b)
    return (jnp.dot(ah, bh, preferred_element_type=F32) + jnp.dot(ah, bl, preferred_element_type=F32)
            + jnp.dot(al, bh, preferred_element_type=F32))


def _lane_tile(x, width):
    return x if width == LANES else jnp.concatenate([x] * (width // LANES), axis=1)


def _rms(x, g, eps=NORM_EPS):
    return x * lax.rsqrt(jnp.mean(x * x, axis=-1, keepdims=True) + eps) * g


def _sigmoid(x):
    return 1.0 / (1.0 + jnp.exp(-x))


def _softplus(x):
    return jnp.maximum(x, 0.0) + jnp.log(1.0 + jnp.exp(-jnp.abs(x)))


def _fold_qlat_kernel(uq_ref, uk_ref, o_ref):
    a = uq_ref[...]
    b = uk_ref[...]
    ah, al = _split2(a)
    bh, bl = _split2(b)
    dn = (((1,), (1,)), ((), ()))
    o = (lax.dot_general(ah, bh, dn, preferred_element_type=F32)
         + lax.dot_general(ah, bl, dn, preferred_element_type=F32)
         + lax.dot_general(al, bh, dn, preferred_element_type=F32))
    o_ref[...] = o.astype(BF16)


def _fold_qlat(uq_nope, uk):
    return pl.pallas_call(
        _fold_qlat_kernel,
        grid=(MLA_HEADS,),
        in_specs=[pl.BlockSpec((None, MLA_Q_RANK, MLA_NOPE), lambda h: (h, 0, 0)),
                  pl.BlockSpec((None, MLA_KV_RANK, MLA_NOPE), lambda h: (h, 0, 0))],
        out_specs=pl.BlockSpec((MLA_Q_RANK, MLA_KV_RANK), lambda h: (0, h)),
        out_shape=jax.ShapeDtypeStruct((MLA_Q_RANK, MLA_HEADS * MLA_KV_RANK), BF16),
        compiler_params=_cparams(("arbitrary",)),
    )(uq_nope, uk)


def _even_in_kernel(x_ref, g_ref, wq_ref, wckv_ref, wpa_ref, wpb_ref, wrw_ref, qn_ref, kvn_ref,
                    cs_ref, sn_ref, wql_ref, wqa_ref, wqb_ref, cs8_ref, sn8_ref,
                    lat_ref, latb_ref, ql_ref, qpe_ref, rw_ref):
    xn = _rms(x_ref[...], g_ref[...]).astype(BF16)
    cq = _rms(_dot(xn, wq_ref[...]), qn_ref[...]).astype(BF16)
    ql_ref[...] = (_dot(cq, wql_ref[...]) * Q_PRESCALE).astype(BF16)
    qpe = _dot(cq, wqa_ref[...]) * cs8_ref[...] + _dot(cq, wqb_ref[...]) * sn8_ref[...]
    qpe_ref[...] = (qpe * Q_PRESCALE).astype(BF16)
    ckv = _rms(_dot(xn, wckv_ref[...]), kvn_ref[...])
    kpe = _dot(xn, wpa_ref[...]) * cs_ref[...] + _dot(xn, wpb_ref[...]) * sn_ref[...]
    lat_ref[:, :MLA_KV_RANK] = ckv
    lat_ref[:, MLA_KV_RANK:] = kpe[:, :MLA_ROPE]
    latb_ref[:, :MLA_KV_RANK] = ckv.astype(BF16)
    latb_ref[:, MLA_KV_RANK:] = kpe.astype(BF16)
    rw_ref[...] = _dot(xn, wrw_ref[...])


def _even_in(x, w, tabs, tm):
    n = x.shape[0]
    nt = tabs["cs"].shape[0] // tm
    tab = lambda width: pl.BlockSpec((tm, width), lambda i: (i % nt, 0))
    hq = MLA_HEADS * MLA_KV_RANK
    hr = MLA_HEADS * LANES
    return pl.pallas_call(
        _even_in_kernel,
        grid=(n // tm,),
        in_specs=[_row_spec(tm, D_MODEL), _const_spec((1, D_MODEL)),
                  _const_spec((D_MODEL, MLA_Q_RANK)), _const_spec((D_MODEL, MLA_KV_RANK)),
                  _const_spec((D_MODEL, LANES)), _const_spec((D_MODEL, LANES)),
                  _const_spec((D_MODEL, RWKV_PROJ)), _const_spec((1, MLA_Q_RANK)),
                  _const_spec((1, MLA_KV_RANK)), tab(LANES), tab(LANES),
                  _const_spec((MLA_Q_RANK, hq)), _const_spec((MLA_Q_RANK, hr)),
                  _const_spec((MLA_Q_RANK, hr)), tab(hr), tab(hr)],
        out_specs=[_row_spec(tm, MLA_LAT), _row_spec(tm, MLA_LATB), _row_spec(tm, hq),
                   _row_spec(tm, hr), _row_spec(tm, RWKV_PROJ)],
        out_shape=[jax.ShapeDtypeStruct((n, MLA_LAT), F32), jax.ShapeDtypeStruct((n, MLA_LATB), BF16),
                   jax.ShapeDtypeStruct((n, hq), BF16), jax.ShapeDtypeStruct((n, hr), BF16),
                   jax.ShapeDtypeStruct((n, RWKV_PROJ), F32)],
        compiler_params=_cparams(("parallel",)),
    )(x, w["norm_mix"], w["w_q"], w["w_ckv"], w["w_pe_a"], w["w_pe_b"], w["w_rw"], w["q_norm"],
      w["kv_norm"], tabs["cs"], tabs["sn"], w["w_qlat"], w["w_qpe_a"], w["w_qpe_b"],
      tabs["cs8"], tabs["sn8"])


ATT_TQ = 256
ATT_TK = 512


def _mla_prompt_kernel(qi_ref, kj_ref, ql_ref, qpe_ref, lat_ref, o_ref,
                       m_sc, l_sc, a_sc, acc_sc, s_sc, p_sc):
    step = pl.program_id(1)
    i = qi_ref[step]
    j = kj_ref[step]
    heads = range(MLA_HEADS)

    @pl.when(j == 0)
    def _():
        m_sc[...] = jnp.full(m_sc.shape, NEG_BIG, F32)
        l_sc[...] = jnp.zeros(l_sc.shape, F32)
        acc_sc[...] = jnp.zeros(acc_sc.shape, F32)

    def tile(masked):
        ckv = lat_ref[:, :MLA_KV_RANK]
        kpe = lat_ref[:, MLA_KV_RANK:]
        for h in heads:
            s_sc[h] = (_dot_nt(ql_ref[:, h * MLA_KV_RANK:(h + 1) * MLA_KV_RANK], ckv)
                       + _dot_nt(qpe_ref[:, h * LANES:(h + 1) * LANES], kpe))
        for h in heads:
            s = s_sc[h]
            if masked:
                tok = lax.broadcasted_iota(jnp.int32, s.shape, 0) + offset
                key = lax.broadcasted_iota(jnp.int32, s.shape, 1)
                s = jnp.where(key <= tok, s, NEG_BIG)
            m_prev = m_sc[h]
            m_new = jnp.maximum(m_prev, jnp.max(s, axis=-1, keepdims=True))
            alpha = jnp.exp2(m_prev - m_new)
            p = jnp.exp2(s - _lane_tile(m_new, ATT_TK))
            l_sc[h] = alpha * l_sc[h] + jnp.sum(p, axis=-1, keepdims=True)
            m_sc[h] = m_new
            a_sc[h] = alpha
            p_sc[h] = p.astype(BF16)
        for h in heads:
            acc_sc[h] = _lane_tile(a_sc[h], MLA_KV_RANK) * acc_sc[h] + _dot(p_sc[h], ckv)

    offset = i * ATT_TQ - j * ATT_TK
    on_diagonal = offset < ATT_TK - 1

    @pl.when(jnp.logical_not(on_diagonal))
    def _():
        tile(False)

    @pl.when(on_diagonal)
    def _():
        tile(True)

    @pl.when(j == (i * ATT_TQ + ATT_TQ - 1) // ATT_TK)
    def _():
        for h in heads:
            o_ref[:, h * MLA_KV_RANK:(h + 1) * MLA_KV_RANK] = (
                acc_sc[h] / _lane_tile(l_sc[h], MLA_KV_RANK)).astype(BF16)


def _mla_prompt(q_lat, q_pe, lat_b, batch, seq):
    nq = seq // ATT_TQ
    nk = seq // ATT_TK
    pairs = [(i, j) for i in range(nq)
             for j in range((i * ATT_TQ + ATT_TQ - 1) // ATT_TK + 1)]
    qi = jnp.array([p[0] for p in pairs], jnp.int32)
    kj = jnp.array([p[1] for p in pairs], jnp.int32)
    hq = MLA_HEADS * MLA_KV_RANK
    grid_spec = pltpu.PrefetchScalarGridSpec(
        num_scalar_prefetch=2,
        grid=(batch, len(pairs)),
        in_specs=[pl.BlockSpec((ATT_TQ, hq), lambda b, s, qi, kj: (b * nq + qi[s], 0)),
                  pl.BlockSpec((ATT_TQ, MLA_HEADS * LANES), lambda b, s, qi, kj: (b * nq + qi[s], 0)),
                  pl.BlockSpec((ATT_TK, MLA_LATB), lambda b, s, qi, kj: (b * nk + kj[s], 0))],
        out_specs=pl.BlockSpec((ATT_TQ, hq), lambda b, s, qi, kj: (b * nq + qi[s], 0)),
        scratch_shapes=[pltpu.VMEM((MLA_HEADS, ATT_TQ, LANES), F32),
                        pltpu.VMEM((MLA_HEADS, ATT_TQ, LANES), F32),
                        pltpu.VMEM((MLA_HEADS, ATT_TQ, LANES), F32),
                        pltpu.VMEM((MLA_HEADS, ATT_TQ, MLA_KV_RANK), F32),
                        pltpu.VMEM((MLA_HEADS, ATT_TQ, ATT_TK), F32),
                        pltpu.VMEM((MLA_HEADS, ATT_TQ, ATT_TK), BF16)],
    )
    return pl.pallas_call(
        _mla_prompt_kernel,
        grid_spec=grid_spec,
        out_shape=jax.ShapeDtypeStruct(q_lat.shape, BF16),
        compiler_params=_cparams(("parallel", "arbitrary")),
    )(qi, kj, q_lat, q_pe, lat_b)


PAGES_PER_STEP = 32
DECODE_GROUPS = 8


def _mla_decode_kernel(pt_ref, q_ref, new_ref, *rest):
    page_refs = rest[:PAGES_PER_STEP]
    o_ref, m_sc, l_sc, acc_sc = rest[PAGES_PER_STEP:]
    j = pl.program_id(1)
    q = q_ref[0]

    @pl.when(j == 0)
    def _():
        m_sc[...] = jnp.full(m_sc.shape, NEG_BIG, F32)
        l_sc[...] = jnp.zeros(l_sc.shape, F32)
        acc_sc[...] = jnp.zeros(acc_sc.shape, F32)

    def update(state, s, values_t):
        m_prev, l_prev, acc = state
        m_new = jnp.maximum(m_prev, jnp.max(s, axis=-1, keepdims=True))
        alpha = jnp.exp2(m_prev - m_new)
        p = jnp.exp2(s - _lane_tile(m_new, s.shape[1]))
        l_new = alpha * l_prev + jnp.sum(p, axis=-1, keepdims=True)
        return m_new, l_new, _lane_tile(alpha, MLA_KV_RANK) * acc + _dot_nt(p, values_t)

    group = PAGES_PER_STEP // DECODE_GROUPS
    keys = [jnp.concatenate([pr[...].astype(BF16) for pr in page_refs[g * group:(g + 1) * group]],
                            axis=1) for g in range(DECODE_GROUPS)]
    scores = [_dot(q, kt) for kt in keys]
    state = (m_sc[...], l_sc[...], acc_sc[...])
    for s, kt in zip(scores, keys):
        state = update(state, s, kt[:MLA_KV_RANK, :])
    m_sc[...], l_sc[...], acc_sc[...] = state

    @pl.when(j == pl.num_programs(1) - 1)
    def _():
        new_t = new_ref[0]
        sn = _dot(q, new_t)
        tok = lax.broadcasted_iota(jnp.int32, sn.shape, 0) >> 3
        key = lax.broadcasted_iota(jnp.int32, sn.shape, 1)
        sn = jnp.where(key <= tok, sn, NEG_BIG)
        _, l_fin, acc_fin = update(state, sn, new_t[:MLA_KV_RANK, :])
        o_ref[0] = (acc_fin / _lane_tile(l_fin, MLA_KV_RANK)).astype(BF16)


def _mla_decode(page_table, q_full, new_pad_t, cache_t, layer):
    db, n_pages = page_table.shape
    rows = q_full.shape[1]
    steps = n_pages // PAGES_PER_STEP

    def page_spec(p):
        return pl.BlockSpec((None, None, MLA_LAT, PAGE_SIZE),
                            lambda b, j, pt: (layer, pt[b, j * PAGES_PER_STEP + p], 0, 0))

    grid_spec = pltpu.PrefetchScalarGridSpec(
        num_scalar_prefetch=1,
        grid=(db, steps),
        in_specs=[pl.BlockSpec((1, rows, MLA_LAT), lambda b, j, pt: (b, 0, 0)),
                  pl.BlockSpec((1, MLA_LAT, PAGE_SIZE), lambda b, j, pt: (b, 0, 0))]
        + [page_spec(p) for p in range(PAGES_PER_STEP)],
        out_specs=pl.BlockSpec((1, rows, MLA_KV_RANK), lambda b, j, pt: (b, 0, 0)),
        scratch_shapes=[pltpu.VMEM((rows, LANES), F32), pltpu.VMEM((rows, LANES), F32),
                        pltpu.VMEM((rows, MLA_KV_RANK), F32)],
    )
    return pl.pallas_call(
        _mla_decode_kernel,
        grid_spec=grid_spec,
        out_shape=jax.ShapeDtypeStruct((db, rows, MLA_KV_RANK), BF16),
        compiler_params=_cparams(("parallel", "arbitrary")),
    )(page_table, q_full, new_pad_t, *([cache_t] * PAGES_PER_STEP))


def _rwkv_prep_kernel(rw_ref, before_ref, sh_ref, mu_ref, w0_ref, w2_ref, a0_ref, a2_ref, g2_ref,
                      kk_ref, ka_ref, rk_ref, ones_ref, xs_ref, v_ref, g_ref, rkv_ref, *, tm, seq):
    rw = rw_ref[...]
    rolled = pltpu.roll(rw, 1, axis=0)
    row = lax.broadcasted_iota(jnp.int32, rw.shape, 0)
    if seq >= tm:
        at_start = pl.program_id(0) % (seq // tm) == 0
        first = jnp.where(at_start, sh_ref[...], before_ref[7:8, :])
        prev = jnp.where(row == 0, first, rolled)
    else:
        prev = jnp.where((row & (seq - 1)) == 0, sh_ref[...], rolled)
    xs = rw + (prev - rw) * mu_ref[...]
    d = RWKV_DIM
    r = xs[:, :d]
    k = xs[:, d:2 * d]
    v = xs[:, 2 * d:3 * d]
    xwa = xs[:, 3 * d:3 * d + LANES]
    xg = xs[:, 3 * d + LANES:]
    ones = ones_ref[...]
    w_log = -_softplus(-(w0_ref[...] + _dot(jnp.tanh(xwa), w2_ref[...]))) - 0.5
    a = _sigmoid(a0_ref[...] + _dot(xwa, a2_ref[...]))
    g_ref[...] = _dot(_sigmoid(xg), g2_ref[...])
    kk = k * kk_ref[...]
    ss = _dot_exact_rhs(kk * kk, ones)
    kk = kk / jnp.maximum(jnp.sqrt(ss), 1e-12)
    k2 = k * (1.0 + (a - 1.0) * ka_ref[...])
    xs_ref[0] = -kk
    xs_ref[1] = jnp.exp(-jnp.exp(w_log))
    xs_ref[2] = kk * a
    xs_ref[3] = k2
    xs_ref[4] = r
    v_ref[...] = v
    rkv_ref[...] = _dot_exact_rhs(r * k2 * rk_ref[...], ones) * v


def _rwkv_prep(rw, shift0, w, tm, seq):
    n = rw.shape[0]
    d = RWKV_DIM
    vec = _const_spec((1, d))
    if seq >= tm:
        tiles = seq // tm
        sh = shift0.reshape(-1, 1, RWKV_PROJ)
        sh_spec = pl.BlockSpec((None, 1, RWKV_PROJ), lambda i: (i // tiles, 0, 0))
    else:
        sh = jnp.repeat(shift0, seq, axis=0)
        sh_spec = _row_spec(tm, RWKV_PROJ)
    before_spec = pl.BlockSpec((8, RWKV_PROJ), lambda i: (jnp.maximum(i * (tm // 8) - 1, 0), 0))
    return pl.pallas_call(
        functools.partial(_rwkv_prep_kernel, tm=tm, seq=seq),
        grid=(n // tm,),
        in_specs=[_row_spec(tm, RWKV_PROJ), before_spec, sh_spec, _const_spec((1, RWKV_PROJ)),
                  vec, _const_spec((LANES, d)), vec, _const_spec((LANES, d)),
                  _const_spec((RWKV_G_LORA, d)), vec, vec, vec, _const_spec((d, d))],
        out_specs=[pl.BlockSpec((5, tm, d), lambda i: (0, i, 0))] + [_row_spec(tm, d)] * 3,
        out_shape=[jax.ShapeDtypeStruct((5, n, d), F32)] + [jax.ShapeDtypeStruct((n, d), F32)] * 3,
        compiler_params=_cparams(("parallel",)),
    )(rw, rw, sh, w["mu"], w["w0"], w["w2p"], w["a0"], w["a2p"], w["g2"], w["k_k"], w["k_a"],
      w["r_k"], w["ones_bd"])


SCAN_KH = RWKV_HEAD // 2
SCAN_PAIRS = LANES // 2
SCAN_VR = RWKV_HEAD // 2


def _rwkv_scan_kernel(x_ref, v_ref, s0_ref, *rest, tc):
    y_ref, s_ref, c_sc, d_sc = rest[-4:]
    @pl.when(pl.program_id(1) == 0)
    def _():
        s_ref[...] = s0_ref[...]

    half_a = slice(0, SCAN_VR)
    half_b = slice(SCAN_VR, RWKV_HEAD)

    def both_halves(p):
        return p + pltpu.roll(p, SCAN_PAIRS, axis=1)

    def key_dot(u, w):
        return both_halves(jnp.sum(u * w, axis=0, keepdims=True))

    def first_partial(rows):
        p = s_ref[0, 0, rows, :] * x_ref[0, 0, 0, 0:1, :]
        for k in range(1, SCAN_KH):
            p = p + s_ref[0, k, rows, :] * x_ref[0, 0, 0, k:k + 1, :]
        return p

    def half_step(t, rows, sa):
        v_half = v_ref[0, t, rows, :]
        v = jnp.concatenate([v_half, v_half], axis=1)
        q = None
        y = None
        for k in range(SCAN_KH):
            s_old = s_ref[0, k, rows, :]
            qk = s_old * c_sc[k:k + 1, :]
            sn = (s_old * x_ref[1, 0, t, k:k + 1, :] + sa * x_ref[2, 0, t, k:k + 1, :]
                  + v * x_ref[3, 0, t, k:k + 1, :])
            s_ref[0, k, rows, :] = sn
            yk = sn * x_ref[4, 0, t, k:k + 1, :]
            q = qk if q is None else q + qk
            y = yk if y is None else y + yk
        return q, sa * d_sc[0:1, :] + v * d_sc[1:2, :], y

    def store_y(t, y_a, y_b):
        y_ref[0, t, half_a, :] = both_halves(y_a)[:, :SCAN_PAIRS]
        y_ref[0, t, half_b, :] = both_halves(y_b)[:, :SCAN_PAIRS]

    def step(t, carry):
        sa_a, q_b, corr_b, y_a, y_b = carry
        store_y(jnp.maximum(t - 1, 0), y_a, y_b)
        a_next = x_ref[0, 0, jnp.minimum(t + 1, tc - 1)]
        c_sc[...] = x_ref[1, 0, t] * a_next
        d_sc[0:1, :] = key_dot(x_ref[2, 0, t], a_next)
        d_sc[1:2, :] = key_dot(x_ref[3, 0, t], a_next)
        sa_b = both_halves(q_b) + corr_b
        q_a, corr_a, y_a_new = half_step(t, half_a, sa_a)
        sa_a_next = both_halves(q_a) + corr_a
        q_b_next, corr_b_next, y_b_new = half_step(t, half_b, sa_b)
        return sa_a_next, q_b_next, corr_b_next, y_a_new, y_b_new

    zero = jnp.zeros((SCAN_VR, LANES), F32)
    init = (both_halves(first_partial(half_a)), first_partial(half_b), zero, zero, zero)
    final = lax.fori_loop(0, tc, step, init)
    store_y(tc - 1, final[3], final[4])


def _rwkv_scan(xs, v, s0, after, tc):
    _, nb, t, _, _ = xs.shape
    xspec = pl.BlockSpec((5, 1, tc, SCAN_KH, LANES), lambda n, c: (0, n, c, 0, 0))
    vspec = pl.BlockSpec((1, tc, RWKV_HEAD, SCAN_PAIRS), lambda n, c: (n, c, 0, 0))
    sspec = pl.BlockSpec((1, SCAN_KH, RWKV_HEAD, LANES), lambda n, c: (n, 0, 0, 0))
    return pl.pallas_call(
        functools.partial(_rwkv_scan_kernel, tc=tc),
        grid=(nb, t // tc),
        in_specs=[xspec, vspec, sspec] + [pl.BlockSpec(memory_space=pl.ANY)] * len(after),
        out_specs=[vspec, sspec],
        out_shape=[jax.ShapeDtypeStruct(v.shape, F32), jax.ShapeDtypeStruct(s0.shape, F32)],
        scratch_shapes=[pltpu.VMEM((SCAN_KH, LANES), F32), pltpu.VMEM((8, LANES), F32)],
        compiler_params=_cparams(("parallel", "arbitrary")),
    )(xs, v, s0, *after)


def _even_out_kernel(y_ref, rkv_ref, g_ref, lng_ref, lnb_ref, ones_ref, ol_ref, wuv_ref, woa_ref,
                     wob_ref, x_ref, o_ref):
    ones = ones_ref[...]
    y = y_ref[...]
    inv = 1.0 / RWKV_HEAD
    mean = _dot_exact_rhs(y, ones) * inv
    dlt = y - mean
    var = _dot_exact_rhs(dlt * dlt, ones) * inv
    yn = dlt * lax.rsqrt(var + RWKV_LN_EPS) * lng_ref[...] + lnb_ref[...] + rkv_ref[...]
    ob = (yn * g_ref[...]).astype(BF16)
    pair = 2 * MLA_KV_RANK
    oa = jnp.concatenate(
        [_dot(ol_ref[:, p * pair:(p + 1) * pair], wuv_ref[p]) for p in range(MLA_HEADS // 2)], axis=1)
    o_ref[...] = x_ref[...] + _dot(oa, woa_ref[...]) + _dot(ob, wob_ref[...])


def _even_out(y, rkv, g, o_lat, x, w, tm):
    n = x.shape[0]
    d = RWKV_DIM
    hq = MLA_HEADS * MLA_KV_RANK
    return pl.pallas_call(
        _even_out_kernel,
        grid=(n // tm,),
        in_specs=[_row_spec(tm, d), _row_spec(tm, d), _row_spec(tm, d), _const_spec((1, d)),
                  _const_spec((1, d)), _const_spec((d, d)), _row_spec(tm, hq),
                  _const_spec((MLA_HEADS // 2, 2 * MLA_KV_RANK, 2 * MLA_V)),
                  _const_spec((MLA_HEADS * MLA_V, D_MODEL)), _const_spec((d, D_MODEL)),
                  _row_spec(tm, D_MODEL)],
        out_specs=_row_spec(tm, D_MODEL),
        out_shape=jax.ShapeDtypeStruct((n, D_MODEL), F32),
        compiler_params=_cparams(("parallel",)),
    )(y, rkv, g, w["ln_g"], w["ln_b"], w["ones_bd"], o_lat, w["w_uv_bd"], w["w_out_a"],
      w["w_out_b"], x)


FFN_TF = 1408


def _ffn_kernel(x_ref, g_ref, wg_ref, wu_ref, wd_ref, o_ref, xn_sc, acc_sc):
    f = pl.program_id(1)

    @pl.when(f == 0)
    def _():
        xn_sc[...] = _rms(x_ref[...], g_ref[...]).astype(BF16)
        acc_sc[...] = jnp.zeros(acc_sc.shape, F32)

    xn = xn_sc[...]
    gate = _dot(xn, wg_ref[...])
    up = _dot(xn, wu_ref[...])
    acc_sc[...] += _dot(gate * _sigmoid(gate) * up, wd_ref[...])

    @pl.when(f == pl.num_programs(1) - 1)
    def _():
        o_ref[...] = x_ref[...] + acc_sc[...]


def _ffn(x, g, w_gu, w_down, tm):
    n = x.shape[0]
    nf = D_FF // FFN_TF
    return pl.pallas_call(
        _ffn_kernel,
        grid=(n // tm, nf),
        in_specs=[pl.BlockSpec((tm, D_MODEL), lambda i, f: (i, 0)),
                  pl.BlockSpec((1, D_MODEL), lambda i, f: (0, 0)),
                  pl.BlockSpec((D_MODEL, FFN_TF), lambda i, f: (0, f)),
                  pl.BlockSpec((D_MODEL, FFN_TF), lambda i, f: (0, nf + f)),
                  pl.BlockSpec((FFN_TF, D_MODEL), lambda i, f: (f, 0))],
        out_specs=pl.BlockSpec((tm, D_MODEL), lambda i, f: (i, 0)),
        out_shape=jax.ShapeDtypeStruct((n, D_MODEL), F32),
        scratch_shapes=[pltpu.VMEM((tm, D_MODEL), BF16), pltpu.VMEM((tm, D_MODEL), F32)],
        compiler_params=_cparams(("parallel", "arbitrary")),
    )(x, g, w_gu, w_gu, w_down)


def _odd_in_kernel(x_ref, g_ref, wq_ref, wk_ref, wv_ref, wg_ref, wxa_ref, a2_ref, ab_ref,
                   q_ref, k_ref, v_ref, gate_ref, la_ref):
    xn = _rms(x_ref[...], g_ref[...]).astype(BF16)
    q_ref[...] = _dot(xn, wq_ref[...]) * (GLA_DK ** -0.5)
    k_ref[...] = _dot(xn, wk_ref[...])
    v_ref[...] = _dot(xn, wv_ref[...])
    gate_ref[...] = _dot(xn, wg_ref[...])
    z = _dot(_dot(xn, wxa_ref[...]), a2_ref[...]) + ab_ref[...]
    la_ref[...] = -_softplus(-z) * (1.0 / GLA_GATE_NORM)


def _odd_in(x, w, tm):
    n = x.shape[0]
    return pl.pallas_call(
        _odd_in_kernel,
        grid=(n // tm,),
        in_specs=[_row_spec(tm, D_MODEL), _const_spec((1, D_MODEL)),
                  _const_spec((D_MODEL, GLA_KDIM)), _const_spec((D_MODEL, GLA_KDIM)),
                  _const_spec((D_MODEL, GLA_VDIM)), _const_spec((D_MODEL, GLA_VDIM)),
                  _const_spec((D_MODEL, LANES)), _const_spec((LANES, GLA_KDIM)),
                  _const_spec((1, GLA_KDIM))],
        out_specs=[_row_spec(tm, GLA_KDIM), _row_spec(tm, GLA_KDIM), _row_spec(tm, GLA_VDIM),
                   _row_spec(tm, GLA_VDIM), _row_spec(tm, GLA_KDIM)],
        out_shape=[jax.ShapeDtypeStruct((n, GLA_KDIM), F32), jax.ShapeDtypeStruct((n, GLA_KDIM), F32),
                   jax.ShapeDtypeStruct((n, GLA_VDIM), F32), jax.ShapeDtypeStruct((n, GLA_VDIM), F32),
                   jax.ShapeDtypeStruct((n, GLA_KDIM), F32)],
        compiler_params=_cparams(("parallel",)),
    )(x, w["norm_mix"], w["w_q"], w["w_k"], w["w_v"], w["w_g"], w["w_xa"], w["a2p"], w["ab"])


def _gla_kernel(q_ref, k_ref, v_ref, la_ref, s0_ref, o_ref, st_ref):
    c = GLA_CHUNK

    @pl.when(pl.program_id(1) == 0)
    def _():
        st_ref[...] = s0_ref[...]

    row = lax.broadcasted_iota(jnp.int32, (c, c), 0)
    col = lax.broadcasted_iota(jnp.int32, (c, c), 1)
    tri = row >= col
    tri_b = jnp.where(tri, 1.0, 0.0).astype(BF16)
    for h in range(GLA_HEADS):
        ks = slice(h * GLA_DK, (h + 1) * GLA_DK)
        vs = slice(h * GLA_DV, (h + 1) * GLA_DV)
        b = _dot_exact_lhs(tri_b, la_ref[:, ks])
        q = q_ref[:, ks]
        k = k_ref[:, ks]
        v = v_ref[:, vs]
        b_end = b[c - 1:c, :]
        qe = (q * jnp.exp(b)).astype(BF16)
        ke = (k * jnp.exp(-b)).astype(BF16)
        a_mat = jnp.where(tri, _dot_nt(qe, ke), 0.0)
        st = st_ref[0, h]
        o_ref[:, vs] = _dot_nt(qe, st) + _dot(a_mat, v)
        k_end = k * jnp.exp(b_end - b)
        st_ref[0, h] = st * jnp.exp(b_end) + _dot(v.T, k_end)


def _gla(q, k, v, la, s0t, batch, seq):
    nc = seq // GLA_CHUNK
    rspec = lambda width: pl.BlockSpec((GLA_CHUNK, width), lambda b, c: (b * nc + c, 0))
    sspec = pl.BlockSpec((1, GLA_HEADS, GLA_DV, GLA_DK), lambda b, c: (b, 0, 0, 0))
    return pl.pallas_call(
        _gla_kernel,
        grid=(batch, nc),
        in_specs=[rspec(GLA_KDIM), rspec(GLA_KDIM), rspec(GLA_VDIM), rspec(GLA_KDIM), sspec],
        out_specs=[rspec(GLA_VDIM), sspec],
        out_shape=[jax.ShapeDtypeStruct(v.shape, F32), jax.ShapeDtypeStruct(s0t.shape, F32)],
        compiler_params=_cparams(("parallel", "arbitrary")),
    )(q, k, v, la, s0t)


def _odd_out_kernel(o_ref, gate_ref, gn_ref, wo_ref, x_ref, y_ref):
    parts = []
    for h in range(GLA_HEADS):
        vs = slice(h * GLA_DV, (h + 1) * GLA_DV)
        parts.append(_rms(o_ref[:, vs], gn_ref[:, vs]))
    gate = gate_ref[...]
    on = jnp.concatenate(parts, axis=1) * (gate * _sigmoid(gate))
    y_ref[...] = x_ref[...] + _dot(on, wo_ref[...])


def _odd_out(o, gate, x, w, tm):
    n = x.shape[0]
    return pl.pallas_call(
        _odd_out_kernel,
        grid=(n // tm,),
        in_specs=[_row_spec(tm, GLA_VDIM), _row_spec(tm, GLA_VDIM), _const_spec((1, GLA_VDIM)),
                  _const_spec((GLA_VDIM, D_MODEL)), _row_spec(tm, D_MODEL)],
        out_specs=_row_spec(tm, D_MODEL),
        out_shape=jax.ShapeDtypeStruct((n, D_MODEL), F32),
        compiler_params=_cparams(("parallel",)),
    )(o, gate, w["gla_norm"], w["w_out"], x)


def _router_kernel(x_ref, g_ref, wr_ref, *rest):
    xn_ref, idx_ref, gate_ref = rest[-3:]
    xn = _rms(x_ref[...], g_ref[...])
    half = D_MODEL // 2
    xn_ref[0] = xn[:, :half]
    xn_ref[1] = xn[:, half:]
    logits = _dot_f32ish(xn, wr_ref[...])
    lane = lax.broadcasted_iota(jnp.int32, logits.shape, 1)
    logits = jnp.where(lane < N_EXPERTS, logits, NEG_BIG)
    m1 = jnp.max(logits, axis=-1, keepdims=True)
    i1 = jnp.min(jnp.where(logits == m1, lane, LANES), axis=-1, keepdims=True)
    rest = jnp.where(lane == i1, NEG_BIG, logits)
    m2 = jnp.max(rest, axis=-1, keepdims=True)
    i2 = jnp.min(jnp.where(rest == m2, lane, LANES), axis=-1, keepdims=True)
    e2 = jnp.exp(m2 - m1)
    g1 = 1.0 / (1.0 + e2)
    g2 = e2 / (1.0 + e2)
    idx_ref[...] = jnp.where(lane == 0, i1, jnp.where(lane == 1, i2, 0))
    gate_ref[...] = jnp.where(lane == 0, g1, jnp.where(lane == 1, g2, 0.0))


def _router(x, g, wr, tm, n_total, row0, prev=None):
    n = x.shape[0]
    half = D_MODEL // 2
    blk0 = row0 // tm
    prev = () if prev is None else tuple(prev)
    return pl.pallas_call(
        _router_kernel,
        grid=(n // tm,),
        in_specs=[_row_spec(tm, D_MODEL), _const_spec((1, D_MODEL)), _const_spec((D_MODEL, LANES))]
        + [pl.BlockSpec(memory_space=pl.ANY)] * len(prev),
        out_specs=[pl.BlockSpec((2, tm, half), lambda i: (0, blk0 + i, 0)),
                   pl.BlockSpec((tm, LANES), lambda i: (blk0 + i, 0)),
                   pl.BlockSpec((tm, LANES), lambda i: (blk0 + i, 0))],
        out_shape=[jax.ShapeDtypeStruct((2, n_total, half), F32),
                   jax.ShapeDtypeStruct((n_total, LANES), jnp.int32),
                   jax.ShapeDtypeStruct((n_total, LANES), F32)],
        input_output_aliases={3 + k: k for k in range(len(prev))},
        compiler_params=_cparams(("parallel",)),
    )(x, g, wr, *prev)


MOE_TF = 1792
MOE_TMC = 256


def _route(top_i, tm, tmc):
    n = top_i.shape[0]
    slots = 2 * n
    n_tiles = -(-(slots + N_EXPERTS * (tm - 1)) // tm)
    win = tmc + 8
    e_flat = top_i.reshape(-1)
    onehot = (e_flat[:, None] == jnp.arange(N_EXPERTS, dtype=jnp.int32)[None, :]).astype(jnp.int32)
    csum = jnp.cumsum(onehot, axis=0)
    rank = jnp.sum(onehot * csum, axis=1) - 1
    counts = csum[-1]
    padded = ((counts + tm - 1) // tm) * tm
    ends = jnp.cumsum(padded)
    starts = ends - padded
    dest = (jnp.sum(onehot * starts[None, :], axis=1) + rank).astype(jnp.int32)
    tile_start = jnp.arange(n_tiles, dtype=jnp.int32) * tm
    tile_expert = jnp.minimum(jnp.sum((tile_start[:, None] >= ends[None, :]).astype(jnp.int32), axis=1),
                              N_EXPERTS - 1).astype(jnp.int32)
    tile_valid = (tile_start < ends[-1]).astype(jnp.int32)
    src = jnp.zeros((n_tiles * tm,), jnp.int32).at[dest].set(jnp.arange(slots, dtype=jnp.int32) // 2)
    before = jnp.concatenate([jnp.zeros((1, N_EXPERTS), jnp.int32), csum[2 * tmc - 1:-1:2 * tmc]], axis=0)
    wstart = jnp.clip(((starts[None, :] + before) // 8) * 8, 0, n_tiles * tm - win).astype(jnp.int32)
    ws_slot = jnp.sum(onehot * jnp.repeat(wstart, 2 * tmc, axis=0), axis=1)
    local = (e_flat * win + dest - ws_slot).astype(jnp.int32)
    return local, wstart.reshape(-1), src, tile_expert, tile_valid


def _moe_gather_kernel(src_ref, x_ref, o_ref, *, tg):
    base = pl.program_id(1) * tg

    def body(r, carry):
        o_ref[pl.ds(r, 1), :] = x_ref[pl.ds(src_ref[base + r], 1), :]
        return carry

    lax.fori_loop(0, tg, body, 0, unroll=8)


def _moe_gather(src, xn2, tg):
    rows = src.shape[0]
    _, n, half = xn2.shape
    grid_spec = pltpu.PrefetchScalarGridSpec(
        num_scalar_prefetch=1,
        grid=(2, rows // tg),
        in_specs=[pl.BlockSpec((None, n, half), lambda h, i, s: (h, 0, 0),
                               pipeline_mode=pl.Buffered(1))],
        out_specs=pl.BlockSpec((tg, half), lambda h, i, s: (i, h)),
    )
    return pl.pallas_call(
        functools.partial(_moe_gather_kernel, tg=tg),
        grid_spec=grid_spec,
        out_shape=jax.ShapeDtypeStruct((rows, 2 * half), F32),
        compiler_params=_cparams(("arbitrary", "arbitrary")),
    )(src, xn2)


def _moe_up_kernel(te_ref, tv_ref, xs_ref, wg_ref, wu_ref, h_ref):
    @pl.when(tv_ref[pl.program_id(1)] != 0)
    def _():
        xs = xs_ref[...].astype(BF16)
        gate = _dot(xs, wg_ref[...])
        up = _dot(xs, wu_ref[...])
        h_ref[...] = (gate * _sigmoid(gate) * up).astype(BF16)


def _moe_up(te, tv, xs, w_gu, layer, tm):
    rows = xs.shape[0]
    nf = D_FF_EXPERT // MOE_TF
    wspec = lambda off: pl.BlockSpec((None, None, D_MODEL, MOE_TF),
                                     lambda f, t, te, tv: (layer, te[t], 0, off + f))
    grid_spec = pltpu.PrefetchScalarGridSpec(
        num_scalar_prefetch=2,
        grid=(nf, rows // tm),
        in_specs=[pl.BlockSpec((tm, D_MODEL), lambda f, t, te, tv: (t, 0)), wspec(0), wspec(nf)],
        out_specs=pl.BlockSpec((tm, MOE_TF), lambda f, t, te, tv: (t, f)),
    )
    return pl.pallas_call(
        _moe_up_kernel,
        grid_spec=grid_spec,
        out_shape=jax.ShapeDtypeStruct((rows, D_FF_EXPERT), BF16),
        compiler_params=_cparams(("arbitrary", "arbitrary")),
    )(te, tv, xs, w_gu, w_gu)


def _moe_down_kernel(te_ref, tv_ref, h_ref, wd_ref, y_ref):
    @pl.when(tv_ref[pl.program_id(0)] != 0)
    def _():
        y_ref[...] = _dot(h_ref[...], wd_ref[...])


def _moe_down(te, tv, h, w_down, layer, tm):
    rows = h.shape[0]
    grid_spec = pltpu.PrefetchScalarGridSpec(
        num_scalar_prefetch=2,
        grid=(rows // tm,),
        in_specs=[pl.BlockSpec((tm, D_FF_EXPERT), lambda t, te, tv: (t, 0)),
                  pl.BlockSpec((None, None, D_FF_EXPERT, D_MODEL),
                               lambda t, te, tv: (layer, te[t], 0, 0))],
        out_specs=pl.BlockSpec((tm, D_MODEL), lambda t, te, tv: (t, 0)),
    )
    return pl.pallas_call(
        _moe_down_kernel,
        grid_spec=grid_spec,
        out_shape=jax.ShapeDtypeStruct((rows, D_MODEL), F32),
        compiler_params=_cparams(("arbitrary",)),
    )(te, tv, h, w_down)


def _moe_combine_kernel(ws_ref, local_ref, *refs, tmc, win, tile0):
    win_refs = refs[:N_EXPERTS]
    g1_ref, g2_ref, x_ref, fn_ref, o_ref, buf = refs[N_EXPERTS:]
    for e in range(N_EXPERTS):
        buf[e * win:(e + 1) * win, :] = win_refs[e][...]
    base = 2 * (tile0 + pl.program_id(0)) * tmc

    def body(r, carry):
        row = pl.ds(r, 1)
        y1 = buf[pl.ds(local_ref[base + 2 * r], 1), :]
        y2 = buf[pl.ds(local_ref[base + 2 * r + 1], 1), :]
        g1 = g1_ref[row, :]
        g2 = g2_ref[row, :]
        parts = []
        for c in range(D_MODEL // LANES):
            cs = slice(c * LANES, (c + 1) * LANES)
            parts.append(g1 * y1[:, cs] + g2 * y2[:, cs])
        o_ref[row, :] = x_ref[row, :] + jnp.concatenate(parts, axis=1)
        return carry

    lax.fori_loop(0, tmc, body, 0, unroll=4)
    o_ref[...] = _rms(o_ref[...], fn_ref[...])


def _moe_combine(local, wstart, ys, g1b, g2b, x, fn, tmc, row0):
    n = x.shape[0]
    win = tmc + 8
    tile0 = row0 // tmc

    def win_spec(e):
        return pl.BlockSpec(
            (pl.Element(win), pl.Element(D_MODEL)),
            lambda i, ws, lo: (pl.multiple_of(ws[(tile0 + i) * N_EXPERTS + e], 8), 0))

    grid_spec = pltpu.PrefetchScalarGridSpec(
        num_scalar_prefetch=2,
        grid=(n // tmc,),
        in_specs=[win_spec(e) for e in range(N_EXPERTS)]
        + [pl.BlockSpec((tmc, LANES), lambda i, ws, lo: (tile0 + i, 0)),
           pl.BlockSpec((tmc, LANES), lambda i, ws, lo: (tile0 + i, 0)),
           pl.BlockSpec((tmc, D_MODEL), lambda i, ws, lo: (i, 0)),
           pl.BlockSpec((1, D_MODEL), lambda i, ws, lo: (0, 0))],
        out_specs=pl.BlockSpec((tmc, D_MODEL), lambda i, ws, lo: (i, 0)),
        scratch_shapes=[pltpu.VMEM((N_EXPERTS * win, D_MODEL), F32)],
    )
    return pl.pallas_call(
        functools.partial(_moe_combine_kernel, tmc=tmc, win=win, tile0=tile0),
        grid_spec=grid_spec,
        out_shape=jax.ShapeDtypeStruct((n, D_MODEL), F32),
        compiler_params=_cparams(("arbitrary",)),
    )(wstart, local, *([ys] * N_EXPERTS), g1b, g2b, x, fn)


def _scan_vec_layout(xs, batch, seq):
    nb = batch * RWKV_HEADS // SCAN_PAIRS
    x = xs.reshape(5, batch, seq, RWKV_HEADS, 2, SCAN_KH).transpose(0, 2, 5, 4, 1, 3)
    x = x.reshape(5, seq, SCAN_KH, 2, nb, SCAN_PAIRS).transpose(0, 4, 1, 2, 3, 5)
    return x.reshape(5, nb, seq, SCAN_KH, LANES)


def _scan_val_layout(v, batch, seq):
    nb = batch * RWKV_HEADS // SCAN_PAIRS
    v4 = v.reshape(batch, seq, RWKV_HEADS, RWKV_HEAD).transpose(1, 3, 0, 2)
    return v4.reshape(seq, RWKV_HEAD, nb, SCAN_PAIRS).transpose(2, 0, 1, 3)


def _scan_val_unlayout(y, batch, seq):
    v4 = y.transpose(1, 2, 0, 3).reshape(seq, RWKV_HEAD, batch, RWKV_HEADS)
    return v4.transpose(2, 0, 3, 1).reshape(batch * seq, RWKV_DIM)


def _scan_state_layout(s, batch):
    nb = batch * RWKV_HEADS // SCAN_PAIRS
    s6 = s.reshape(batch, RWKV_HEADS, RWKV_HEAD, 2, SCAN_KH).transpose(4, 2, 3, 0, 1)
    s6 = s6.reshape(SCAN_KH, RWKV_HEAD, 2, nb, SCAN_PAIRS).transpose(3, 0, 1, 2, 4)
    return s6.reshape(nb, SCAN_KH, RWKV_HEAD, LANES)


def _scan_state_unlayout(arr, batch):
    nb = arr.shape[0]
    s = arr.reshape(nb, SCAN_KH, RWKV_HEAD, 2, SCAN_PAIRS).transpose(0, 4, 2, 3, 1)
    return s.reshape(batch, RWKV_HEADS, RWKV_HEAD, RWKV_HEAD)


def _swap_halves(w):
    half = w.shape[-1] // 2
    return jnp.concatenate([w[..., half:], w[..., :half]], axis=-1)


def _prep_even(i, norm_mix, norm_ffn, w_in, q_norm, kv_norm, w_uq, w_uk, w_uv, mu, w0, w2, a0, a2,
               g2, k_k, k_a, r_k, ln_g, ln_b, w_out, ffn_gu, ffn_down):
    w = {}
    row = lambda v: v[i].reshape(1, -1)
    w_in = w_in[i]
    w["norm_mix"] = row(norm_mix)
    w["norm_ffn"] = row(norm_ffn)
    w["w_q"] = w_in[:, :MLA_Q_RANK].astype(BF16)
    w_kv = w_in[:, MLA_Q_RANK:MLA_Q_RANK + MLA_LAT]
    w["w_ckv"] = w_kv[:, :MLA_KV_RANK].astype(BF16)
    lane_pad = lambda m: jnp.pad(m, [(0, 0)] * (m.ndim - 1) + [(0, LANES - m.shape[-1])])
    w["w_pe_a"] = lane_pad(w_kv[:, MLA_KV_RANK:]).astype(BF16)
    w["w_pe_b"] = lane_pad(_swap_halves(w_kv[:, MLA_KV_RANK:])).astype(BF16)
    w["w_rw"] = w_in[:, MLA_Q_RANK + MLA_LAT:].astype(BF16)
    w["q_norm"] = row(q_norm)
    w["kv_norm"] = row(kv_norm)
    uq = w_uq[i].reshape(MLA_Q_RANK, MLA_HEADS, MLA_NOPE + MLA_ROPE)
    uq_pe = uq[:, :, MLA_NOPE:]
    w["w_qpe_a"] = lane_pad(uq_pe).reshape(MLA_Q_RANK, -1).astype(BF16)
    w["w_qpe_b"] = lane_pad(_swap_halves(uq_pe)).reshape(MLA_Q_RANK, -1).astype(BF16)
    w["w_qlat"] = _fold_qlat(uq[:, :, :MLA_NOPE].transpose(1, 0, 2), w_uk[i].transpose(1, 0, 2))
    uv = w_uv[i].transpose(1, 0, 2).reshape(MLA_HEADS // 2, 2, MLA_KV_RANK, MLA_V)
    zero = jnp.zeros_like(uv[:, 0])
    w["w_uv_bd"] = jnp.concatenate(
        [jnp.concatenate([uv[:, 0], zero], axis=-1), jnp.concatenate([zero, uv[:, 1]], axis=-1)],
        axis=1).astype(BF16)
    w["mu"] = row(mu)
    w["w0"] = row(w0)
    pad = lambda m, before: jnp.pad(m, ((before, LANES - before - m.shape[0]), (0, 0))).astype(BF16)
    w["w2p"] = pad(w2[i], 0)
    w["a2p"] = pad(a2[i], RWKV_W_LORA)
    w["a0"] = row(a0)
    w["g2"] = g2[i].astype(BF16)
    w["k_k"] = row(k_k)
    w["k_a"] = row(k_a)
    w["r_k"] = row(r_k)
    w["ln_g"] = row(ln_g)
    w["ln_b"] = row(ln_b)
    head = jnp.arange(RWKV_DIM) // RWKV_HEAD
    w["ones_bd"] = (head[:, None] == head[None, :]).astype(BF16)
    w["w_out_a"] = w_out[i][:MLA_HEADS * MLA_V].astype(BF16)
    w["w_out_b"] = w_out[i][MLA_HEADS * MLA_V:].astype(BF16)
    w["ffn_gu"] = ffn_gu[i].astype(BF16)
    w["ffn_down"] = ffn_down[i].astype(BF16)
    return w


def _prep_odd(i, norm_mix, norm_ffn, w_in, a2, ab, gla_norm, w_out, router, moe_gu, moe_down):
    w = {}
    row = lambda v: v[i].reshape(1, -1)
    w_in = w_in[i]
    w["norm_mix"] = row(norm_mix)
    w["norm_ffn"] = row(norm_ffn)
    w["w_q"] = w_in[:, :GLA_KDIM].astype(BF16)
    w["w_k"] = w_in[:, GLA_KDIM:2 * GLA_KDIM].astype(BF16)
    w["w_v"] = w_in[:, 2 * GLA_KDIM:2 * GLA_KDIM + GLA_VDIM].astype(BF16)
    w["w_g"] = w_in[:, 2 * GLA_KDIM + GLA_VDIM:2 * GLA_KDIM + 2 * GLA_VDIM].astype(BF16)
    w["w_xa"] = jnp.pad(w_in[:, 2 * GLA_KDIM + 2 * GLA_VDIM:],
                        ((0, 0), (0, LANES - GLA_GATE_RANK))).astype(BF16)
    w["a2p"] = jnp.pad(a2[i], ((0, LANES - GLA_GATE_RANK), (0, 0))).astype(BF16)
    w["ab"] = row(ab)
    w["gla_norm"] = row(gla_norm)
    w["w_out"] = w_out[i].astype(BF16)
    w["router"] = jnp.pad(router[i], ((0, 0), (0, LANES - N_EXPERTS)))
    w["layer"] = i
    w["moe_gu"] = moe_gu
    w["moe_down"] = moe_down
    return w


def _rope_tables(pos, reps):
    inv = ROPE_THETA ** (-jnp.arange(0, MLA_ROPE, 2, dtype=F32) / MLA_ROPE)
    ang = pos.astype(F32)[:, None] * inv[None, :]
    cos, sin = jnp.cos(ang), jnp.sin(ang)
    pad = ((0, 0), (0, LANES - MLA_ROPE))
    cs = jnp.tile(jnp.pad(jnp.concatenate([cos, cos], axis=-1), pad), (reps, 1))
    sn = jnp.tile(jnp.pad(jnp.concatenate([-sin, sin], axis=-1), pad), (reps, 1))
    return {"cs": cs, "sn": sn, "cs8": jnp.tile(cs, (1, MLA_HEADS)), "sn8": jnp.tile(sn, (1, MLA_HEADS))}


def _even_layer(x, batch, seq, tabs, state, shift0, past, w, tm, tc, after=()):
    n = batch * seq
    lat, lat_b, q_lat, q_pe, rw = _even_in(x, w, tabs, tm)
    if past is None:
        o_lat = _mla_prompt(q_lat, q_pe, lat_b, batch, seq)
    else:
        cache, layer, page_table = past
        rows = seq * MLA_HEADS
        q_full = jnp.concatenate([q_lat.reshape(batch, rows, MLA_KV_RANK),
                                  q_pe.reshape(batch, rows, LANES)[:, :, :MLA_ROPE]], axis=-1)
        new_pad_t = jnp.pad(lat_b.reshape(batch, seq, MLA_LATB)[:, :, :MLA_LAT],
                            ((0, 0), (0, PAGE_SIZE - seq), (0, 0))).transpose(0, 2, 1)
        o_lat = _mla_decode(page_table, q_full, new_pad_t, cache.transpose(0, 1, 3, 2), layer)
        o_lat = o_lat.reshape(n, MLA_HEADS * MLA_KV_RANK)

    rw3 = rw.reshape(batch, seq, RWKV_PROJ)
    xs5, v, g, rkv = _rwkv_prep(rw, shift0, w, tm, seq)
    y_l, s_l = _rwkv_scan(_scan_vec_layout(xs5, batch, seq), _scan_val_layout(v, batch, seq),
                          _scan_state_layout(state, batch), (o_lat,) + tuple(after), tc)
    y = _scan_val_unlayout(y_l, batch, seq)
    new_state = _scan_state_unlayout(s_l, batch)

    x = _even_out(y, rkv, g, o_lat, x, w, tm)
    x = _ffn(x, w["norm_ffn"], w["ffn_gu"], w["ffn_down"], tm)
    return x, lat.reshape(batch, seq, MLA_LAT), new_state, rw3[:, -1], o_lat


def _odd_mixer_layer(x, batch, seq, state, w, tm):
    q, k, v, gate, la = _odd_in(x, w, tm)
    seq_p = -(-seq // GLA_CHUNK) * GLA_CHUNK
    if seq_p != seq:
        padr = lambda t: jnp.pad(t.reshape(batch, seq, -1), ((0, 0), (0, seq_p - seq), (0, 0))
                                 ).reshape(batch * seq_p, -1)
        qp, kp, vp, lap = padr(q), padr(k), padr(v), padr(la)
    else:
        qp, kp, vp, lap = q, k, v, la
    o, st = _gla(qp, kp, vp, lap, state.transpose(0, 1, 3, 2), batch, seq_p)
    if seq_p != seq:
        o = o.reshape(batch, seq_p, GLA_VDIM)[:, :seq].reshape(batch * seq, GLA_VDIM)
    return _odd_out(o, gate, x, w, tm), st.transpose(0, 1, 3, 2)


def _moe_all_groups(xs_groups, tms, w, final_norm, tm_moe):
    sizes = [x.shape[0] for x in xs_groups]
    n_total = sum(sizes)
    bufs, row0 = None, 0
    for x, tm in zip(xs_groups, tms):
        bufs = _router(x, w["norm_ffn"], w["router"], tm, n_total, row0, bufs)
        row0 += x.shape[0]
    xn2, idx, gates = bufs
    local, wstart, src, tile_expert, tile_valid = _route(idx[:, :2], tm_moe, MOE_TMC)
    rows = _moe_gather(src, xn2, tm_moe)
    h = _moe_up(tile_expert, tile_valid, rows, w["moe_gu"], w["layer"], tm_moe)
    ys = _moe_down(tile_expert, tile_valid, h, w["moe_down"], w["layer"], tm_moe)
    g1b = jnp.broadcast_to(gates[:, 0:1], (n_total, LANES))
    g2b = jnp.broadcast_to(gates[:, 1:2], (n_total, LANES))
    outs, row0 = [], 0
    for x in xs_groups:
        outs.append(_moe_combine(local, wstart, ys, g1b, g2b, x, final_norm, MOE_TMC, row0))
        row0 += x.shape[0]
    return outs


def kernel(x_prompt, x_sample, cache_mla, state_rwkv, state_rwkv_shift, state_gla, page_table, norm_mix_even, norm_ffn_even, w_in_even, mla_q_norm, mla_kv_norm, mla_w_uq, mla_w_uk, mla_w_uv, rwkv_mu, rwkv_w0, rwkv_w2, rwkv_a0, rwkv_a2, rwkv_g2, rwkv_k_k, rwkv_k_a, rwkv_r_k, rwkv_ln_g, rwkv_ln_b, w_out_even, ffn_w_gu_even, ffn_w_down_even, norm_mix_odd, norm_ffn_odd, w_in_odd, gla_a2, gla_ab, gla_norm, w_out_odd, moe_router, moe_w_gu, moe_w_down, final_norm):
    bp, tp, _ = x_prompt.shape
    bs, ts, _ = x_sample.shape
    past_len = page_table.shape[1] * PAGE_SIZE
    tm_p, tm_s = 512, bs * ts
    we = _prep_even(0, norm_mix_even, norm_ffn_even, w_in_even, mla_q_norm, mla_kv_norm, mla_w_uq,
                    mla_w_uk, mla_w_uv, rwkv_mu, rwkv_w0, rwkv_w2, rwkv_a0, rwkv_a2, rwkv_g2,
                    rwkv_k_k, rwkv_k_a, rwkv_r_k, rwkv_ln_g, rwkv_ln_b, w_out_even, ffn_w_gu_even,
                    ffn_w_down_even)
    wo = _prep_odd(0, norm_mix_odd, norm_ffn_odd, w_in_odd, gla_a2, gla_ab, gla_norm, w_out_odd,
                   moe_router, moe_w_gu, moe_w_down)
    fn = final_norm.reshape(1, -1)
    tabs_p = _rope_tables(jnp.arange(tp), 1)
    tabs_s = _rope_tables(past_len + jnp.arange(ts), bs)

    hp = x_prompt.reshape(bp * tp, D_MODEL)
    hs = x_sample.reshape(bs * ts, D_MODEL)
    zeros_state = jnp.zeros((bp, RWKV_HEADS, RWKV_HEAD, RWKV_HEAD), F32)
    zeros_shift = jnp.zeros((bp, RWKV_PROJ), F32)
    hp, lat_p, rs_p, sh_p, _ = _even_layer(hp, bp, tp, tabs_p, zeros_state, zeros_shift, None, we,
                                           tm_p, 64)
    hs, lat_s, rs_s, sh_s, _ = _even_layer(hs, bs, ts, tabs_s, state_rwkv[0], state_rwkv_shift[0],
                                           (cache_mla, 0, page_table), we, tm_s, ts)
    zeros_gla = jnp.zeros((bp, GLA_HEADS, GLA_DK, GLA_DV), F32)
    hp, gs_p = _odd_mixer_layer(hp, bp, tp, zeros_gla, wo, tm_p)
    hs, gs_s = _odd_mixer_layer(hs, bs, ts, state_gla[0], wo, tm_s)
    yp, ys = _moe_all_groups([hp, hs], [tm_p, tm_s], wo, fn, 512)
    return (yp.reshape(bp, tp, D_MODEL), ys.reshape(bs, ts, D_MODEL), lat_p[None], lat_s[None],
            rs_p[None], rs_s[None], sh_p[None], sh_s[None], gs_p[None], gs_s[None])
```

```python
import functools

import jax
import jax.numpy as jnp
from jax import lax
from jax.experimental import pallas as pl
from jax.experimental.pallas import tpu as pltpu

F32 = jnp.float32
BF16 = jnp.bfloat16

D_MODEL = 1024
PAGE_SIZE = 128
NORM_EPS = 1e-6

MLA_HEADS = 8
MLA_NOPE = 64
MLA_ROPE = 32
MLA_V = 64
MLA_Q_RANK = 384
MLA_KV_RANK = 256
MLA_LAT = MLA_KV_RANK + MLA_ROPE
MLA_LATB = MLA_KV_RANK + 128
MLA_SCALE = (MLA_NOPE + MLA_ROPE) ** -0.5
ROPE_THETA = 10000.0

RWKV_HEADS = 8
RWKV_HEAD = 64
RWKV_DIM = RWKV_HEADS * RWKV_HEAD
RWKV_W_LORA = 64
RWKV_A_LORA = 64
RWKV_G_LORA = 128
RWKV_PROJ = 3 * RWKV_DIM + RWKV_W_LORA + RWKV_A_LORA + RWKV_G_LORA
RWKV_LN_EPS = 64e-5

GLA_HEADS = 4
GLA_DK = 128
GLA_DV = 256
GLA_KDIM = GLA_HEADS * GLA_DK
GLA_VDIM = GLA_HEADS * GLA_DV
GLA_GATE_RANK = 16
GLA_GATE_NORM = 16.0
GLA_CHUNK = 128

D_FF = 2816
N_EXPERTS = 8
D_FF_EXPERT = 3584

LANES = 128
VMEM_LIMIT = 56 * 1024 * 1024
NEG_BIG = -1e30
LOG2_E = 1.4426950408889634
Q_PRESCALE = MLA_SCALE * LOG2_E


def _cparams(sem):
    return pltpu.CompilerParams(dimension_semantics=sem, vmem_limit_bytes=VMEM_LIMIT)


def _const_spec(shape):
    nd = len(shape)
    return pl.BlockSpec(shape, lambda *_: (0,) * nd)


def _row_spec(tm, width):
    return pl.BlockSpec((tm, width), lambda i: (i, 0))


def _dot(a, b):
    return jnp.dot(a.astype(BF16), b.astype(BF16), preferred_element_type=F32)


def _dot_nt(a, b):
    return lax.dot_general(a.astype(BF16), b.astype(BF16), (((1,), (1,)), ((), ())),
                           preferred_element_type=F32)


def _split2(x):
    hi = x.astype(BF16)
    lo = (x - hi.astype(F32)).astype(BF16)
    return hi, lo


def _split3(x):
    hi = x.astype(BF16)
    r1 = x - hi.astype(F32)
    mid = r1.astype(BF16)
    lo = (r1 - mid.astype(F32)).astype(BF16)
    return hi, mid, lo


def _dot_exact_rhs(x, e):
    hi, mid, lo = _split3(x)
    return (jnp.dot(hi, e, preferred_element_type=F32) + jnp.dot(mid, e, preferred_element_type=F32)
            + jnp.dot(lo, e, preferred_element_type=F32))


def _dot_exact_lhs(e, x):
    hi, mid, lo = _split3(x)
    return (jnp.dot(e, hi, preferred_element_type=F32) + jnp.dot(e, mid, preferred_element_type=F32)
            + jnp.dot(e, lo, preferred_element_type=F32))


def _dot_f32ish(a, b):
    ah, al = _split2(a)
    bh, bl = _split2(b)
    return (jnp.dot(ah, bh, preferred_element_type=F32) + jnp.dot(ah, bl, preferred_element_type=F32)
            + jnp.dot(al, bh, preferred_element_type=F32))


def _lane_tile(x, width):
    return x if width == LANES else jnp.concatenate([x] * (width // LANES), axis=1)


def _rms(x, g, eps=NORM_EPS):
    return x * lax.rsqrt(jnp.mean(x * x, axis=-1, keepdims=True) + eps) * g


def _sigmoid(x):
    return 1.0 / (1.0 + jnp.exp(-x))


def _softplus(x):
    return jnp.maximum(x, 0.0) + jnp.log(1.0 + jnp.exp(-jnp.abs(x)))


def _fold_qlat_kernel(uq_ref, uk_ref, o_ref):
    a = uq_ref[...]
    b = uk_ref[...]
    ah, al = _split2(a)
    bh, bl = _split2(b)
    dn = (((1,), (1,)), ((), ()))
    o = (lax.dot_general(ah, bh, dn, preferred_element_type=F32)
         + lax.dot_general(ah, bl, dn, preferred_element_type=F32)
         + lax.dot_general(al, bh, dn, preferred_element_type=F32))
    o_ref[...] = o.astype(BF16)


def _fold_qlat(uq_nope, uk):
    return pl.pallas_call(
        _fold_qlat_kernel,
        grid=(MLA_HEADS,),
        in_specs=[pl.BlockSpec((None, MLA_Q_RANK, MLA_NOPE), lambda h: (h, 0, 0)),
                  pl.BlockSpec((None, MLA_KV_RANK, MLA_NOPE), lambda h: (h, 0, 0))],
        out_specs=pl.BlockSpec((MLA_Q_RANK, MLA_KV_RANK), lambda h: (0, h)),
        out_shape=jax.ShapeDtypeStruct((MLA_Q_RANK, MLA_HEADS * MLA_KV_RANK), BF16),
        compiler_params=_cparams(("arbitrary",)),
    )(uq_nope, uk)


def _even_in_kernel(x_ref, g_ref, wq_ref, wckv_ref, wpa_ref, wpb_ref, wrw_ref, qn_ref, kvn_ref,
                    cs_ref, sn_ref, wql_ref, wqa_ref, wqb_ref, cs8_ref, sn8_ref,
                    lat_ref, latb_ref, ql_ref, qpe_ref, rw_ref):
    xn = _rms(x_ref[...], g_ref[...]).astype(BF16)
    cq = _rms(_dot(xn, wq_ref[...]), qn_ref[...]).astype(BF16)
    ql_ref[...] = (_dot(cq, wql_ref[...]) * Q_PRESCALE).astype(BF16)
    qpe = _dot(cq, wqa_ref[...]) * cs8_ref[...] + _dot(cq, wqb_ref[...]) * sn8_ref[...]
    qpe_ref[...] = (qpe * Q_PRESCALE).astype(BF16)
    ckv = _rms(_dot(xn, wckv_ref[...]), kvn_ref[...])
    kpe = _dot(xn, wpa_ref[...]) * cs_ref[...] + _dot(xn, wpb_ref[...]) * sn_ref[...]
    lat_ref[:, :MLA_KV_RANK] = ckv
    lat_ref[:, MLA_KV_RANK:] = kpe[:, :MLA_ROPE]
    latb_ref[:, :MLA_KV_RANK] = ckv.astype(BF16)
    latb_ref[:, MLA_KV_RANK:] = kpe.astype(BF16)
    rw_ref[...] = _dot(xn, wrw_ref[...])


def _even_in(x, w, tabs, tm):
    n = x.shape[0]
    nt = tabs["cs"].shape[0] // tm
    tab = lambda width: pl.BlockSpec((tm, width), lambda i: (i % nt, 0))
    hq = MLA_HEADS * MLA_KV_RANK
    hr = MLA_HEADS * LANES
    return pl.pallas_call(
        _even_in_kernel,
        grid=(n // tm,),
        in_specs=[_row_spec(tm, D_MODEL), _const_spec((1, D_MODEL)),
                  _const_spec((D_MODEL, MLA_Q_RANK)), _const_spec((D_MODEL, MLA_KV_RANK)),
                  _const_spec((D_MODEL, LANES)), _const_spec((D_MODEL, LANES)),
                  _const_spec((D_MODEL, RWKV_PROJ)), _const_spec((1, MLA_Q_RANK)),
                  _const_spec((1, MLA_KV_RANK)), tab(LANES), tab(LANES),
                  _const_spec((MLA_Q_RANK, hq)), _const_spec((MLA_Q_RANK, hr)),
                  _const_spec((MLA_Q_RANK, hr)), tab(hr), tab(hr)],
        out_specs=[_row_spec(tm, MLA_LAT), _row_spec(tm, MLA_LATB), _row_spec(tm, hq),
                   _row_spec(tm, hr), _row_spec(tm, RWKV_PROJ)],
        out_shape=[jax.ShapeDtypeStruct((n, MLA_LAT), F32), jax.ShapeDtypeStruct((n, MLA_LATB), BF16),
                   jax.ShapeDtypeStruct((n, hq), BF16), jax.ShapeDtypeStruct((n, hr), BF16),
                   jax.ShapeDtypeStruct((n, RWKV_PROJ), F32)],
        compiler_params=_cparams(("parallel",)),
    )(x, w["norm_mix"], w["w_q"], w["w_ckv"], w["w_pe_a"], w["w_pe_b"], w["w_rw"], w["q_norm"],
      w["kv_norm"], tabs["cs"], tabs["sn"], w["w_qlat"], w["w_qpe_a"], w["w_qpe_b"],
      tabs["cs8"], tabs["sn8"])


ATT_TQ = 256
ATT_TK = 512


def _mla_prompt_kernel(qi_ref, kj_ref, ql_ref, qpe_ref, lat_ref, o_ref,
                       m_sc, l_sc, a_sc, acc_sc, s_sc, p_sc):
    step = pl.program_id(1)
    i = qi_ref[step]
    j = kj_ref[step]
    heads = range(MLA_HEADS)

    @pl.when(j == 0)
    def _():
        m_sc[...] = jnp.full(m_sc.shape, NEG_BIG, F32)
        l_sc[...] = jnp.zeros(l_sc.shape, F32)
        acc_sc[...] = jnp.zeros(acc_sc.shape, F32)

    def tile(masked):
        ckv = lat_ref[:, :MLA_KV_RANK]
        kpe = lat_ref[:, MLA_KV_RANK:]
        for h in heads:
            s_sc[h] = (_dot_nt(ql_ref[:, h * MLA_KV_RANK:(h + 1) * MLA_KV_RANK], ckv)
                       + _dot_nt(qpe_ref[:, h * LANES:(h + 1) * LANES], kpe))
        for h in heads:
            s = s_sc[h]
            if masked:
                tok = lax.broadcasted_iota(jnp.int32, s.shape, 0) + offset
                key = lax.broadcasted_iota(jnp.int32, s.shape, 1)
                s = jnp.where(key <= tok, s, NEG_BIG)
            m_prev = m_sc[h]
            m_new = jnp.maximum(m_prev, jnp.max(s, axis=-1, keepdims=True))
            alpha = jnp.exp2(m_prev - m_new)
            p = jnp.exp2(s - _lane_tile(m_new, ATT_TK))
            l_sc[h] = alpha * l_sc[h] + jnp.sum(p, axis=-1, keepdims=True)
            m_sc[h] = m_new
            a_sc[h] = alpha
            p_sc[h] = p.astype(BF16)
        for h in heads:
            acc_sc[h] = _lane_tile(a_sc[h], MLA_KV_RANK) * acc_sc[h] + _dot(p_sc[h], ckv)

    offset = i * ATT_TQ - j * ATT_TK
    on_diagonal = offset < ATT_TK - 1

    @pl.when(jnp.logical_not(on_diagonal))
    def _():
        tile(False)

    @pl.when(on_diagonal)
    def _():
        tile(True)

    @pl.when(j == (i * ATT_TQ + ATT_TQ - 1) // ATT_TK)
    def _():
        for h in heads:
            o_ref[:, h * MLA_KV_RANK:(h + 1) * MLA_KV_RANK] = (
                acc_sc[h] / _lane_tile(l_sc[h], MLA_KV_RANK)).astype(BF16)


def _mla_prompt(q_lat, q_pe, lat_b, batch, seq):
    nq = seq // ATT_TQ
    nk = seq // ATT_TK
    pairs = [(i, j) for i in range(nq)
             for j in range((i * ATT_TQ + ATT_TQ - 1) // ATT_TK + 1)]
    qi = jnp.array([p[0] for p in pairs], jnp.int32)
    kj = jnp.array([p[1] for p in pairs], jnp.int32)
    hq = MLA_HEADS * MLA_KV_RANK
    grid_spec = pltpu.PrefetchScalarGridSpec(
        num_scalar_prefetch=2,
        grid=(batch, len(pairs)),
        in_specs=[pl.BlockSpec((ATT_TQ, hq), lambda b, s, qi, kj: (b * nq + qi[s], 0)),
                  pl.BlockSpec((ATT_TQ, MLA_HEADS * LANES), lambda b, s, qi, kj: (b * nq + qi[s], 0)),
                  pl.BlockSpec((ATT_TK, MLA_LATB), lambda b, s, qi, kj: (b * nk + kj[s], 0))],
        out_specs=pl.BlockSpec((ATT_TQ, hq), lambda b, s, qi, kj: (b * nq + qi[s], 0)),
        scratch_shapes=[pltpu.VMEM((MLA_HEADS, ATT_TQ, LANES), F32),
                        pltpu.VMEM((MLA_HEADS, ATT_TQ, LANES), F32),
                        pltpu.VMEM((MLA_HEADS, ATT_TQ, LANES), F32),
                        pltpu.VMEM((MLA_HEADS, ATT_TQ, MLA_KV_RANK), F32),
                        pltpu.VMEM((MLA_HEADS, ATT_TQ, ATT_TK), F32),
                        pltpu.VMEM((MLA_HEADS, ATT_TQ, ATT_TK), BF16)],
    )
    return pl.pallas_call(
        _mla_prompt_kernel,
        grid_spec=grid_spec,
        out_shape=jax.ShapeDtypeStruct(q_lat.shape, BF16),
        compiler_params=_cparams(("parallel", "arbitrary")),
    )(qi, kj, q_lat, q_pe, lat_b)


PAGES_PER_STEP = 32
DECODE_GROUPS = 8


def _mla_decode_kernel(pt_ref, q_ref, new_ref, *rest):
    page_refs = rest[:PAGES_PER_STEP]
    o_ref, m_sc, l_sc, acc_sc = rest[PAGES_PER_STEP:]
    j = pl.program_id(1)
    q = q_ref[0]

    @pl.when(j == 0)
    def _():
        m_sc[...] = jnp.full(m_sc.shape, NEG_BIG, F32)
        l_sc[...] = jnp.zeros(l_sc.shape, F32)
        acc_sc[...] = jnp.zeros(acc_sc.shape, F32)

    def update(state, s, values_t):
        m_prev, l_prev, acc = state
        m_new = jnp.maximum(m_prev, jnp.max(s, axis=-1, keepdims=True))
        alpha = jnp.exp2(m_prev - m_new)
        p = jnp.exp2(s - _lane_tile(m_new, s.shape[1]))
        l_new = alpha * l_prev + jnp.sum(p, axis=-1, keepdims=True)
        return m_new, l_new, _lane_tile(alpha, MLA_KV_RANK) * acc + _dot_nt(p, values_t)

    group = PAGES_PER_STEP // DECODE_GROUPS
    keys = [jnp.concatenate([pr[...].astype(BF16) for pr in page_refs[g * group:(g + 1) * group]],
                            axis=1) for g in range(DECODE_GROUPS)]
    scores = [_dot(q, kt) for kt in keys]
    state = (m_sc[...], l_sc[...], acc_sc[...])
    for s, kt in zip(scores, keys):
        state = update(state, s, kt[:MLA_KV_RANK, :])
    m_sc[...], l_sc[...], acc_sc[...] = state

    @pl.when(j == pl.num_programs(1) - 1)
    def _():
        new_t = new_ref[0]
        sn = _dot(q, new_t)
        tok = lax.broadcasted_iota(jnp.int32, sn.shape, 0) >> 3
        key = lax.broadcasted_iota(jnp.int32, sn.shape, 1)
        sn = jnp.where(key <= tok, sn, NEG_BIG)
        _, l_fin, acc_fin = update(state, sn, new_t[:MLA_KV_RANK, :])
        o_ref[0] = (acc_fin / _lane_tile(l_fin, MLA_KV_RANK)).astype(BF16)


def _mla_decode(page_table, q_full, new_pad_t, cache_t, layer):
    db, n_pages = page_table.shape
    rows = q_full.shape[1]
    steps = n_pages // PAGES_PER_STEP

    def page_spec(p):
        return pl.BlockSpec((None, None, MLA_LAT, PAGE_SIZE),
                            lambda b, j, pt: (layer, pt[b, j * PAGES_PER_STEP + p], 0, 0))

    grid_spec = pltpu.PrefetchScalarGridSpec(
        num_scalar_prefetch=1,
        grid=(db, steps),
        in_specs=[pl.BlockSpec((1, rows, MLA_LAT), lambda b, j, pt: (b, 0, 0)),
                  pl.BlockSpec((1, MLA_LAT, PAGE_SIZE), lambda b, j, pt: (b, 0, 0))]
        + [page_spec(p) for p in range(PAGES_PER_STEP)],
        out_specs=pl.BlockSpec((1, rows, MLA_KV_RANK), lambda b, j, pt: (b, 0, 0)),
        scratch_shapes=[pltpu.VMEM((rows, LANES), F32), pltpu.VMEM((rows, LANES), F32),
                        pltpu.VMEM((rows, MLA_KV_RANK), F32)],
    )
    return pl.pallas_call(
        _mla_decode_kernel,
        grid_spec=grid_spec,
        out_shape=jax.ShapeDtypeStruct((db, rows, MLA_KV_RANK), BF16),
        compiler_params=_cparams(("parallel", "arbitrary")),
    )(page_table, q_full, new_pad_t, *([cache_t] * PAGES_PER_STEP))


def _rwkv_prep_kernel(rw_ref, before_ref, sh_ref, mu_ref, w0_ref, w2_ref, a0_ref, a2_ref, g2_ref,
                      kk_ref, ka_ref, rk_ref, ones_ref, xs_ref, v_ref, g_ref, rkv_ref, *, tm, seq):
    rw = rw_ref[...]
    rolled = pltpu.roll(rw, 1, axis=0)
    row = lax.broadcasted_iota(jnp.int32, rw.shape, 0)
    if seq >= tm:
        at_start = pl.program_id(0) % (seq // tm) == 0
        first = jnp.where(at_start, sh_ref[...], before_ref[7:8, :])
        prev = jnp.where(row == 0, first, rolled)
    else:
        prev = jnp.where((row & (seq - 1)) == 0, sh_ref[...], rolled)
    xs = rw + (prev - rw) * mu_ref[...]
    d = RWKV_DIM
    r = xs[:, :d]
    k = xs[:, d:2 * d]
    v = xs[:, 2 * d:3 * d]
    xwa = xs[:, 3 * d:3 * d + LANES]
    xg = xs[:, 3 * d + LANES:]
    ones = ones_ref[...]
    w_log = -_softplus(-(w0_ref[...] + _dot(jnp.tanh(xwa), w2_ref[...]))) - 0.5
    a = _sigmoid(a0_ref[...] + _dot(xwa, a2_ref[...]))
    g_ref[...] = _dot(_sigmoid(xg), g2_ref[...])
    kk = k * kk_ref[...]
    ss = _dot_exact_rhs(kk * kk, ones)
    kk = kk / jnp.maximum(jnp.sqrt(ss), 1e-12)
    k2 = k * (1.0 + (a - 1.0) * ka_ref[...])
    xs_ref[0] = -kk
    xs_ref[1] = jnp.exp(-jnp.exp(w_log))
    xs_ref[2] = kk * a
    xs_ref[3] = k2
    xs_ref[4] = r
    v_ref[...] = v
    rkv_ref[...] = _dot_exact_rhs(r * k2 * rk_ref[...], ones) * v


def _rwkv_prep(rw, shift0, w, tm, seq):
    n = rw.shape[0]
    d = RWKV_DIM
    vec = _const_spec((1, d))
    if seq >= tm:
        tiles = seq // tm
        sh = shift0.reshape(-1, 1, RWKV_PROJ)
        sh_spec = pl.BlockSpec((None, 1, RWKV_PROJ), lambda i: (i // tiles, 0, 0))
    else:
        sh = jnp.repeat(shift0, seq, axis=0)
        sh_spec = _row_spec(tm, RWKV_PROJ)
    before_spec = pl.BlockSpec((8, RWKV_PROJ), lambda i: (jnp.maximum(i * (tm // 8) - 1, 0), 0))
    return pl.pallas_call(
        functools.partial(_rwkv_prep_kernel, tm=tm, seq=seq),
        grid=(n // tm,),
        in_specs=[_row_spec(tm, RWKV_PROJ), before_spec, sh_spec, _const_spec((1, RWKV_PROJ)),
                  vec, _const_spec((LANES, d)), vec, _const_spec((LANES, d)),
                  _const_spec((RWKV_G_LORA, d)), vec, vec, vec, _const_spec((d, d))],
        out_specs=[pl.BlockSpec((5, tm, d), lambda i: (0, i, 0))] + [_row_spec(tm, d)] * 3,
        out_shape=[jax.ShapeDtypeStruct((5, n, d), F32)] + [jax.ShapeDtypeStruct((n, d), F32)] * 3,
        compiler_params=_cparams(("parallel",)),
    )(rw, rw, sh, w["mu"], w["w0"], w["w2p"], w["a0"], w["a2p"], w["g2"], w["k_k"], w["k_a"],
      w["r_k"], w["ones_bd"])


SCAN_KH = RWKV_HEAD // 2
SCAN_PAIRS = LANES // 2
SCAN_VR = RWKV_HEAD // 2


def _rwkv_scan_kernel(x_ref, v_ref, s0_ref, *rest, tc):
    y_ref, s_ref, c_sc, d_sc = rest[-4:]
    @pl.when(pl.program_id(1) == 0)
    def _():
        s_ref[...] = s0_ref[...]

    half_a = slice(0, SCAN_VR)
    half_b = slice(SCAN_VR, RWKV_HEAD)

    def both_halves(p):
        return p + pltpu.roll(p, SCAN_PAIRS, axis=1)

    def key_dot(u, w):
        return both_halves(jnp.sum(u * w, axis=0, keepdims=True))

    def first_partial(rows):
        p = s_ref[0, 0, rows, :] * x_ref[0, 0, 0, 0:1, :]
        for k in range(1, SCAN_KH):
            p = p + s_ref[0, k, rows, :] * x_ref[0, 0, 0, k:k + 1, :]
        return p

    def half_step(t, rows, sa):
        v_half = v_ref[0, t, rows, :]
        v = jnp.concatenate([v_half, v_half], axis=1)
        q = None
        y = None
        for k in range(SCAN_KH):
            s_old = s_ref[0, k, rows, :]
            qk = s_old * c_sc[k:k + 1, :]
            sn = (s_old * x_ref[1, 0, t, k:k + 1, :] + sa * x_ref[2, 0, t, k:k + 1, :]
                  + v * x_ref[3, 0, t, k:k + 1, :])
            s_ref[0, k, rows, :] = sn
            yk = sn * x_ref[4, 0, t, k:k + 1, :]
            q = qk if q is None else q + qk
            y = yk if y is None else y + yk
        return q, sa * d_sc[0:1, :] + v * d_sc[1:2, :], y

    def store_y(t, y_a, y_b):
        y_ref[0, t, half_a, :] = both_halves(y_a)[:, :SCAN_PAIRS]
        y_ref[0, t, half_b, :] = both_halves(y_b)[:, :SCAN_PAIRS]

    def step(t, carry):
        sa_a, q_b, corr_b, y_a, y_b = carry
        store_y(jnp.maximum(t - 1, 0), y_a, y_b)
        a_next = x_ref[0, 0, jnp.minimum(t + 1, tc - 1)]
        c_sc[...] = x_ref[1, 0, t] * a_next
        d_sc[0:1, :] = key_dot(x_ref[2, 0, t], a_next)
        d_sc[1:2, :] = key_dot(x_ref[3, 0, t], a_next)
        sa_b = both_halves(q_b) + corr_b
        q_a, corr_a, y_a_new = half_step(t, half_a, sa_a)
        sa_a_next = both_halves(q_a) + corr_a
        q_b_next, corr_b_next, y_b_new = half_step(t, half_b, sa_b)
        return sa_a_next, q_b_next, corr_b_next, y_a_new, y_b_new

    zero = jnp.zeros((SCAN_VR, LANES), F32)
    init = (both_halves(first_partial(half_a)), first_partial(half_b), zero, zero, zero)
    final = lax.fori_loop(0, tc, step, init)
    store_y(tc - 1, final[3], final[4])


def _rwkv_scan(xs, v, s0, after, tc):
    _, nb, t, _, _ = xs.shape
    xspec = pl.BlockSpec((5, 1, tc, SCAN_KH, LANES), lambda n, c: (0, n, c, 0, 0))
    vspec = pl.BlockSpec((1, tc, RWKV_HEAD, SCAN_PAIRS), lambda n, c: (n, c, 0, 0))
    sspec = pl.BlockSpec((1, SCAN_KH, RWKV_HEAD, LANES), lambda n, c: (n, 0, 0, 0))
    return pl.pallas_call(
        functools.partial(_rwkv_scan_kernel, tc=tc),
        grid=(nb, t // tc),
        in_specs=[xspec, vspec, sspec] + [pl.BlockSpec(memory_space=pl.ANY)] * len(after),
        out_specs=[vspec, sspec],
        out_shape=[jax.ShapeDtypeStruct(v.shape, F32), jax.ShapeDtypeStruct(s0.shape, F32)],
        scratch_shapes=[pltpu.VMEM((SCAN_KH, LANES), F32), pltpu.VMEM((8, LANES), F32)],
        compiler_params=_cparams(("parallel", "arbitrary")),
    )(xs, v, s0, *after)


def _even_out_kernel(y_ref, rkv_ref, g_ref, lng_ref, lnb_ref, ones_ref, ol_ref, wuv_ref, woa_ref,
                     wob_ref, x_ref, o_ref):
    ones = ones_ref[...]
    y = y_ref[...]
    inv = 1.0 / RWKV_HEAD
    mean = _dot_exact_rhs(y, ones) * inv
    dlt = y - mean
    var = _dot_exact_rhs(dlt * dlt, ones) * inv
    yn = dlt * lax.rsqrt(var + RWKV_LN_EPS) * lng_ref[...] + lnb_ref[...] + rkv_ref[...]
    ob = (yn * g_ref[...]).astype(BF16)
    pair = 2 * MLA_KV_RANK
    oa = jnp.concatenate(
        [_dot(ol_ref[:, p * pair:(p + 1) * pair], wuv_ref[p]) for p in range(MLA_HEADS // 2)], axis=1)
    o_ref[...] = x_ref[...] + _dot(oa, woa_ref[...]) + _dot(ob, wob_ref[...])


def _even_out(y, rkv, g, o_lat, x, w, tm):
    n = x.shape[0]
    d = RWKV_DIM
    hq = MLA_HEADS * MLA_KV_RANK
    return pl.pallas_call(
        _even_out_kernel,
        grid=(n // tm,),
        in_specs=[_row_spec(tm, d), _row_spec(tm, d), _row_spec(tm, d), _const_spec((1, d)),
                  _const_spec((1, d)), _const_spec((d, d)), _row_spec(tm, hq),
                  _const_spec((MLA_HEADS // 2, 2 * MLA_KV_RANK, 2 * MLA_V)),
                  _const_spec((MLA_HEADS * MLA_V, D_MODEL)), _const_spec((d, D_MODEL)),
                  _row_spec(tm, D_MODEL)],
        out_specs=_row_spec(tm, D_MODEL),
        out_shape=jax.ShapeDtypeStruct((n, D_MODEL), F32),
        compiler_params=_cparams(("parallel",)),
    )(y, rkv, g, w["ln_g"], w["ln_b"], w["ones_bd"], o_lat, w["w_uv_bd"], w["w_out_a"],
      w["w_out_b"], x)


FFN_TF = 1408


def _ffn_kernel(x_ref, g_ref, wg_ref, wu_ref, wd_ref, o_ref, xn_sc, acc_sc):
    f = pl.program_id(1)

    @pl.when(f == 0)
    def _():
        xn_sc[...] = _rms(x_ref[...], g_ref[...]).astype(BF16)
        acc_sc[...] = jnp.zeros(acc_sc.shape, F32)

    xn = xn_sc[...]
    gate = _dot(xn, wg_ref[...])
    up = _dot(xn, wu_ref[...])
    acc_sc[...] += _dot(gate * _sigmoid(gate) * up, wd_ref[...])

    @pl.when(f == pl.num_programs(1) - 1)
    def _():
        o_ref[...] = x_ref[...] + acc_sc[...]


def _ffn(x, g, w_gu, w_down, tm):
    n = x.shape[0]
    nf = D_FF // FFN_TF
    return pl.pallas_call(
        _ffn_kernel,
        grid=(n // tm, nf),
        in_specs=[pl.BlockSpec((tm, D_MODEL), lambda i, f: (i, 0)),
                  pl.BlockSpec((1, D_MODEL), lambda i, f: (0, 0)),
                  pl.BlockSpec((D_MODEL, FFN_TF), lambda i, f: (0, f)),
                  pl.BlockSpec((D_MODEL, FFN_TF), lambda i, f: (0, nf + f)),
                  pl.BlockSpec((FFN_TF, D_MODEL), lambda i, f: (f, 0))],
        out_specs=pl.BlockSpec((tm, D_MODEL), lambda i, f: (i, 0)),
        out_shape=jax.ShapeDtypeStruct((n, D_MODEL), F32),
        scratch_shapes=[pltpu.VMEM((tm, D_MODEL), BF16), pltpu.VMEM((tm, D_MODEL), F32)],
        compiler_params=_cparams(("parallel", "arbitrary")),
    )(x, g, w_gu, w_gu, w_down)


def _odd_in_kernel(x_ref, g_ref, wq_ref, wk_ref, wv_ref, wg_ref, wxa_ref, a2_ref, ab_ref,
                   q_ref, k_ref, v_ref, gate_ref, la_ref):
    xn = _rms(x_ref[...], g_ref[...]).astype(BF16)
    q_ref[...] = _dot(xn, wq_ref[...]) * (GLA_DK ** -0.5)
    k_ref[...] = _dot(xn, wk_ref[...])
    v_ref[...] = _dot(xn, wv_ref[...])
    gate_ref[...] = _dot(xn, wg_ref[...])
    z = _dot(_dot(xn, wxa_ref[...]), a2_ref[...]) + ab_ref[...]
    la_ref[...] = -_softplus(-z) * (1.0 / GLA_GATE_NORM)


def _odd_in(x, w, tm):
    n = x.shape[0]
    return pl.pallas_call(
        _odd_in_kernel,
        grid=(n // tm,),
        in_specs=[_row_spec(tm, D_MODEL), _const_spec((1, D_MODEL)),
                  _const_spec((D_MODEL, GLA_KDIM)), _const_spec((D_MODEL, GLA_KDIM)),
                  _const_spec((D_MODEL, GLA_VDIM)), _const_spec((D_MODEL, GLA_VDIM)),
                  _const_spec((D_MODEL, LANES)), _const_spec((LANES, GLA_KDIM)),
                  _const_spec((1, GLA_KDIM))],
        out_specs=[_row_spec(tm, GLA_KDIM), _row_spec(tm, GLA_KDIM), _row_spec(tm, GLA_VDIM),
                   _row_spec(tm, GLA_VDIM), _row_spec(tm, GLA_KDIM)],
        out_shape=[jax.ShapeDtypeStruct((n, GLA_KDIM), F32), jax.ShapeDtypeStruct((n, GLA_KDIM), F32),
                   jax.ShapeDtypeStruct((n, GLA_VDIM), F32), jax.ShapeDtypeStruct((n, GLA_VDIM), F32),
                   jax.ShapeDtypeStruct((n, GLA_KDIM), F32)],
        compiler_params=_cparams(("parallel",)),
    )(x, w["norm_mix"], w["w_q"], w["w_k"], w["w_v"], w["w_g"], w["w_xa"], w["a2p"], w["ab"])


def _gla_kernel(q_ref, k_ref, v_ref, la_ref, s0_ref, o_ref, st_ref):
    c = GLA_CHUNK

    @pl.when(pl.program_id(1) == 0)
    def _():
        st_ref[...] = s0_ref[...]

    row = lax.broadcasted_iota(jnp.int32, (c, c), 0)
    col = lax.broadcasted_iota(jnp.int32, (c, c), 1)
    tri = row >= col
    tri_b = jnp.where(tri, 1.0, 0.0).astype(BF16)
    heads = range(GLA_HEADS)
    ks = [slice(h * GLA_DK, (h + 1) * GLA_DK) for h in heads]
    vs = [slice(h * GLA_DV, (h + 1) * GLA_DV) for h in heads]
    b = _dot_exact_lhs(tri_b, la_ref[...])
    k = k_ref[...]
    b_end = b[c - 1:c, :]
    qe = (q_ref[...] * jnp.exp(b)).astype(BF16)
    ke = (k * jnp.exp(-b)).astype(BF16)
    k_end = (k * jnp.exp(b_end - b)).astype(BF16)
    e_end = jnp.exp(b_end)
    a_mats = [jnp.where(tri, _dot_nt(qe[:, ks[h]], ke[:, ks[h]]), 0.0).astype(BF16) for h in heads]
    states = [st_ref[0, h] for h in heads]
    for h in heads:
        o_ref[:, vs[h]] = _dot_nt(qe[:, ks[h]], states[h]) + _dot(a_mats[h], v_ref[:, vs[h]])
    for h in heads:
        st_ref[0, h] = states[h] * e_end[:, ks[h]] + _dot(v_ref[:, vs[h]].T, k_end[:, ks[h]])


def _gla(q, k, v, la, s0t, batch, seq):
    nc = seq // GLA_CHUNK
    rspec = lambda width: pl.BlockSpec((GLA_CHUNK, width), lambda b, c: (b * nc + c, 0))
    sspec = pl.BlockSpec((1, GLA_HEADS, GLA_DV, GLA_DK), lambda b, c: (b, 0, 0, 0))
    return pl.pallas_call(
        _gla_kernel,
        grid=(batch, nc),
        in_specs=[rspec(GLA_KDIM), rspec(GLA_KDIM), rspec(GLA_VDIM), rspec(GLA_KDIM), sspec],
        out_specs=[rspec(GLA_VDIM), sspec],
        out_shape=[jax.ShapeDtypeStruct(v.shape, F32), jax.ShapeDtypeStruct(s0t.shape, F32)],
        compiler_params=_cparams(("parallel", "arbitrary")),
    )(q, k, v, la, s0t)


def _odd_out_kernel(o_ref, gate_ref, gn_ref, wo_ref, x_ref, y_ref):
    parts = []
    for h in range(GLA_HEADS):
        vs = slice(h * GLA_DV, (h + 1) * GLA_DV)
        parts.append(_rms(o_ref[:, vs], gn_ref[:, vs]))
    gate = gate_ref[...]
    on = jnp.concatenate(parts, axis=1) * (gate * _sigmoid(gate))
    y_ref[...] = x_ref[...] + _dot(on, wo_ref[...])


def _odd_out(o, gate, x, w, tm):
    n = x.shape[0]
    return pl.pallas_call(
        _odd_out_kernel,
        grid=(n // tm,),
        in_specs=[_row_spec(tm, GLA_VDIM), _row_spec(tm, GLA_VDIM), _const_spec((1, GLA_VDIM)),
                  _const_spec((GLA_VDIM, D_MODEL)), _row_spec(tm, D_MODEL)],
        out_specs=_row_spec(tm, D_MODEL),
        out_shape=jax.ShapeDtypeStruct((n, D_MODEL), F32),
        compiler_params=_cparams(("parallel",)),
    )(o, gate, w["gla_norm"], w["w_out"], x)


def _router_kernel(x_ref, g_ref, wr_ref, *rest):
    xn_ref, idx_ref, gate_ref = rest[-3:]
    xn = _rms(x_ref[...], g_ref[...])
    half = D_MODEL // 2
    xn_ref[0] = xn[:, :half]
    xn_ref[1] = xn[:, half:]
    logits = _dot_f32ish(xn, wr_ref[...])
    lane = lax.broadcasted_iota(jnp.int32, logits.shape, 1)
    logits = jnp.where(lane < N_EXPERTS, logits, NEG_BIG)
    m1 = jnp.max(logits, axis=-1, keepdims=True)
    i1 = jnp.min(jnp.where(logits == m1, lane, LANES), axis=-1, keepdims=True)
    rest = jnp.where(lane == i1, NEG_BIG, logits)
    m2 = jnp.max(rest, axis=-1, keepdims=True)
    i2 = jnp.min(jnp.where(rest == m2, lane, LANES), axis=-1, keepdims=True)
    e2 = jnp.exp(m2 - m1)
    g1 = 1.0 / (1.0 + e2)
    g2 = e2 / (1.0 + e2)
    idx_ref[...] = jnp.where(lane == 0, i1, jnp.where(lane == 1, i2, 0))
    gate_ref[...] = jnp.where(lane == 0, g1, jnp.where(lane == 1, g2, 0.0))


def _router(x, g, wr, tm, n_total, row0, prev=None):
    n = x.shape[0]
    half = D_MODEL // 2
    blk0 = row0 // tm
    prev = () if prev is None else tuple(prev)
    return pl.pallas_call(
        _router_kernel,
        grid=(n // tm,),
        in_specs=[_row_spec(tm, D_MODEL), _const_spec((1, D_MODEL)), _const_spec((D_MODEL, LANES))]
        + [pl.BlockSpec(memory_space=pl.ANY)] * len(prev),
        out_specs=[pl.BlockSpec((2, tm, half), lambda i: (0, blk0 + i, 0)),
                   pl.BlockSpec((tm, LANES), lambda i: (blk0 + i, 0)),
                   pl.BlockSpec((tm, LANES), lambda i: (blk0 + i, 0))],
        out_shape=[jax.ShapeDtypeStruct((2, n_total, half), F32),
                   jax.ShapeDtypeStruct((n_total, LANES), jnp.int32),
                   jax.ShapeDtypeStruct((n_total, LANES), F32)],
        input_output_aliases={3 + k: k for k in range(len(prev))},
        compiler_params=_cparams(("parallel",)),
    )(x, g, wr, *prev)


MOE_TF = 1792
MOE_TMC = 256


def _route(top_i, tm, tmc):
    n = top_i.shape[0]
    slots = 2 * n
    n_tiles = -(-(slots + N_EXPERTS * (tm - 1)) // tm)
    win = tmc + 8
    e_flat = top_i.reshape(-1)
    onehot = (e_flat[:, None] == jnp.arange(N_EXPERTS, dtype=jnp.int32)[None, :]).astype(jnp.int32)
    csum = jnp.cumsum(onehot, axis=0)
    rank = jnp.sum(onehot * csum, axis=1) - 1
    counts = csum[-1]
    padded = ((counts + tm - 1) // tm) * tm
    ends = jnp.cumsum(padded)
    starts = ends - padded
    dest = (jnp.sum(onehot * starts[None, :], axis=1) + rank).astype(jnp.int32)
    tile_start = jnp.arange(n_tiles, dtype=jnp.int32) * tm
    tile_expert = jnp.minimum(jnp.sum((tile_start[:, None] >= ends[None, :]).astype(jnp.int32), axis=1),
                              N_EXPERTS - 1).astype(jnp.int32)
    tile_valid = (tile_start < ends[-1]).astype(jnp.int32)
    src = jnp.zeros((n_tiles * tm,), jnp.int32).at[dest].set(jnp.arange(slots, dtype=jnp.int32) // 2)
    before = jnp.concatenate([jnp.zeros((1, N_EXPERTS), jnp.int32), csum[2 * tmc - 1:-1:2 * tmc]], axis=0)
    wstart = jnp.clip(((starts[None, :] + before) // 8) * 8, 0, n_tiles * tm - win).astype(jnp.int32)
    ws_slot = jnp.sum(onehot * jnp.repeat(wstart, 2 * tmc, axis=0), axis=1)
    local = (e_flat * win + dest - ws_slot).astype(jnp.int32)
    return local, wstart.reshape(-1), src, tile_expert, tile_valid


def _moe_gather_kernel(src_ref, x_ref, o_ref, *, tg):
    base = pl.program_id(1) * tg

    def body(r, carry):
        o_ref[pl.ds(r, 1), :] = x_ref[pl.ds(src_ref[base + r], 1), :]
        return carry

    lax.fori_loop(0, tg, body, 0, unroll=8)


def _moe_gather(src, xn2, tg):
    rows = src.shape[0]
    _, n, half = xn2.shape
    grid_spec = pltpu.PrefetchScalarGridSpec(
        num_scalar_prefetch=1,
        grid=(2, rows // tg),
        in_specs=[pl.BlockSpec((None, n, half), lambda h, i, s: (h, 0, 0),
                               pipeline_mode=pl.Buffered(1))],
        out_specs=pl.BlockSpec((tg, half), lambda h, i, s: (i, h)),
    )
    return pl.pallas_call(
        functools.partial(_moe_gather_kernel, tg=tg),
        grid_spec=grid_spec,
        out_shape=jax.ShapeDtypeStruct((rows, 2 * half), F32),
        compiler_params=_cparams(("arbitrary", "arbitrary")),
    )(src, xn2)


def _moe_up_kernel(te_ref, tv_ref, xs_ref, wg_ref, wu_ref, h_ref):
    @pl.when(tv_ref[pl.program_id(1)] != 0)
    def _():
        xs = xs_ref[...].astype(BF16)
        gate = _dot(xs, wg_ref[...])
        up = _dot(xs, wu_ref[...])
        h_ref[...] = (gate * _sigmoid(gate) * up).astype(BF16)


def _moe_up(te, tv, xs, w_gu, layer, tm):
    rows = xs.shape[0]
    nf = D_FF_EXPERT // MOE_TF
    wspec = lambda off: pl.BlockSpec((None, None, D_MODEL, MOE_TF),
                                     lambda f, t, te, tv: (layer, te[t], 0, off + f))
    grid_spec = pltpu.PrefetchScalarGridSpec(
        num_scalar_prefetch=2,
        grid=(nf, rows // tm),
        in_specs=[pl.BlockSpec((tm, D_MODEL), lambda f, t, te, tv: (t, 0)), wspec(0), wspec(nf)],
        out_specs=pl.BlockSpec((tm, MOE_TF), lambda f, t, te, tv: (t, f)),
    )
    return pl.pallas_call(
        _moe_up_kernel,
        grid_spec=grid_spec,
        out_shape=jax.ShapeDtypeStruct((rows, D_FF_EXPERT), BF16),
        compiler_params=_cparams(("arbitrary", "arbitrary")),
    )(te, tv, xs, w_gu, w_gu)


def _moe_down_kernel(te_ref, tv_ref, h_ref, wd_ref, y_ref):
    @pl.when(tv_ref[pl.program_id(0)] != 0)
    def _():
        y_ref[...] = _dot(h_ref[...], wd_ref[...])


def _moe_down(te, tv, h, w_down, layer, tm):
    rows = h.shape[0]
    grid_spec = pltpu.PrefetchScalarGridSpec(
        num_scalar_prefetch=2,
        grid=(rows // tm,),
        in_specs=[pl.BlockSpec((tm, D_FF_EXPERT), lambda t, te, tv: (t, 0)),
                  pl.BlockSpec((None, None, D_FF_EXPERT, D_MODEL),
                               lambda t, te, tv: (layer, te[t], 0, 0))],
        out_specs=pl.BlockSpec((tm, D_MODEL), lambda t, te, tv: (t, 0)),
    )
    return pl.pallas_call(
        _moe_down_kernel,
        grid_spec=grid_spec,
        out_shape=jax.ShapeDtypeStruct((rows, D_MODEL), F32),
        compiler_params=_cparams(("arbitrary",)),
    )(te, tv, h, w_down)


def _moe_combine_kernel(ws_ref, local_ref, *refs, tmc, win, tile0):
    win_refs = refs[:N_EXPERTS]
    g1_ref, g2_ref, x_ref, fn_ref, o_ref, buf = refs[N_EXPERTS:]
    for e in range(N_EXPERTS):
        buf[e * win:(e + 1) * win, :] = win_refs[e][...]
    base = 2 * (tile0 + pl.program_id(0)) * tmc

    def body(r, carry):
        row = pl.ds(r, 1)
        y1 = buf[pl.ds(local_ref[base + 2 * r], 1), :]
        y2 = buf[pl.ds(local_ref[base + 2 * r + 1], 1), :]
        g1 = g1_ref[row, :]
        g2 = g2_ref[row, :]
        parts = []
        for c in range(D_MODEL // LANES):
            cs = slice(c * LANES, (c + 1) * LANES)
            parts.append(g1 * y1[:, cs] + g2 * y2[:, cs])
        o_ref[row, :] = x_ref[row, :] + jnp.concatenate(parts, axis=1)
        return carry

    lax.fori_loop(0, tmc, body, 0, unroll=4)
    o_ref[...] = _rms(o_ref[...], fn_ref[...])


def _moe_combine(local, wstart, ys, g1b, g2b, x, fn, tmc, row0):
    n = x.shape[0]
    win = tmc + 8
    tile0 = row0 // tmc

    def win_spec(e):
        return pl.BlockSpec(
            (pl.Element(win), pl.Element(D_MODEL)),
            lambda i, ws, lo: (pl.multiple_of(ws[(tile0 + i) * N_EXPERTS + e], 8), 0))

    grid_spec = pltpu.PrefetchScalarGridSpec(
        num_scalar_prefetch=2,
        grid=(n // tmc,),
        in_specs=[win_spec(e) for e in range(N_EXPERTS)]
        + [pl.BlockSpec((tmc, LANES), lambda i, ws, lo: (tile0 + i, 0)),
           pl.BlockSpec((tmc, LANES), lambda i, ws, lo: (tile0 + i, 0)),
           pl.BlockSpec((tmc, D_MODEL), lambda i, ws, lo: (i, 0)),
           pl.BlockSpec((1, D_MODEL), lambda i, ws, lo: (0, 0))],
        out_specs=pl.BlockSpec((tmc, D_MODEL), lambda i, ws, lo: (i, 0)),
        scratch_shapes=[pltpu.VMEM((N_EXPERTS * win, D_MODEL), F32)],
    )
    return pl.pallas_call(
        functools.partial(_moe_combine_kernel, tmc=tmc, win=win, tile0=tile0),
        grid_spec=grid_spec,
        out_shape=jax.ShapeDtypeStruct((n, D_MODEL), F32),
        compiler_params=_cparams(("arbitrary",)),
    )(wstart, local, *([ys] * N_EXPERTS), g1b, g2b, x, fn)


def _scan_vec_layout(xs, batch, seq):
    nb = batch * RWKV_HEADS // SCAN_PAIRS
    x = xs.reshape(5, batch, seq, RWKV_HEADS, 2, SCAN_KH).transpose(0, 2, 5, 4, 1, 3)
    x = x.reshape(5, seq, SCAN_KH, 2, nb, SCAN_PAIRS).transpose(0, 4, 1, 2, 3, 5)
    return x.reshape(5, nb, seq, SCAN_KH, LANES)


def _scan_val_layout(v, batch, seq):
    nb = batch * RWKV_HEADS // SCAN_PAIRS
    v4 = v.reshape(batch, seq, RWKV_HEADS, RWKV_HEAD).transpose(1, 3, 0, 2)
    return v4.reshape(seq, RWKV_HEAD, nb, SCAN_PAIRS).transpose(2, 0, 1, 3)


def _scan_val_unlayout(y, batch, seq):
    v4 = y.transpose(1, 2, 0, 3).reshape(seq, RWKV_HEAD, batch, RWKV_HEADS)
    return v4.transpose(2, 0, 3, 1).reshape(batch * seq, RWKV_DIM)


def _scan_state_layout(s, batch):
    nb = batch * RWKV_HEADS // SCAN_PAIRS
    s6 = s.reshape(batch, RWKV_HEADS, RWKV_HEAD, 2, SCAN_KH).transpose(4, 2, 3, 0, 1)
    s6 = s6.reshape(SCAN_KH, RWKV_HEAD, 2, nb, SCAN_PAIRS).transpose(3, 0, 1, 2, 4)
    return s6.reshape(nb, SCAN_KH, RWKV_HEAD, LANES)


def _scan_state_unlayout(arr, batch):
    nb = arr.shape[0]
    s = arr.reshape(nb, SCAN_KH, RWKV_HEAD, 2, SCAN_PAIRS).transpose(0, 4, 2, 3, 1)
    return s.reshape(batch, RWKV_HEADS, RWKV_HEAD, RWKV_HEAD)


def _swap_halves(w):
    half = w.shape[-1] // 2
    return jnp.concatenate([w[..., half:], w[..., :half]], axis=-1)


def _prep_even(i, norm_mix, norm_ffn, w_in, q_norm, kv_norm, w_uq, w_uk, w_uv, mu, w0, w2, a0, a2,
               g2, k_k, k_a, r_k, ln_g, ln_b, w_out, ffn_gu, ffn_down):
    w = {}
    row = lambda v: v[i].reshape(1, -1)
    w_in = w_in[i]
    w["norm_mix"] = row(norm_mix)
    w["norm_ffn"] = row(norm_ffn)
    w["w_q"] = w_in[:, :MLA_Q_RANK].astype(BF16)
    w_kv = w_in[:, MLA_Q_RANK:MLA_Q_RANK + MLA_LAT]
    w["w_ckv"] = w_kv[:, :MLA_KV_RANK].astype(BF16)
    lane_pad = lambda m: jnp.pad(m, [(0, 0)] * (m.ndim - 1) + [(0, LANES - m.shape[-1])])
    w["w_pe_a"] = lane_pad(w_kv[:, MLA_KV_RANK:]).astype(BF16)
    w["w_pe_b"] = lane_pad(_swap_halves(w_kv[:, MLA_KV_RANK:])).astype(BF16)
    w["w_rw"] = w_in[:, MLA_Q_RANK + MLA_LAT:].astype(BF16)
    w["q_norm"] = row(q_norm)
    w["kv_norm"] = row(kv_norm)
    uq = w_uq[i].reshape(MLA_Q_RANK, MLA_HEADS, MLA_NOPE + MLA_ROPE)
    uq_pe = uq[:, :, MLA_NOPE:]
    w["w_qpe_a"] = lane_pad(uq_pe).reshape(MLA_Q_RANK, -1).astype(BF16)
    w["w_qpe_b"] = lane_pad(_swap_halves(uq_pe)).reshape(MLA_Q_RANK, -1).astype(BF16)
    w["w_qlat"] = _fold_qlat(uq[:, :, :MLA_NOPE].transpose(1, 0, 2), w_uk[i].transpose(1, 0, 2))
    uv = w_uv[i].transpose(1, 0, 2).reshape(MLA_HEADS // 2, 2, MLA_KV_RANK, MLA_V)
    zero = jnp.zeros_like(uv[:, 0])
    w["w_uv_bd"] = jnp.concatenate(
        [jnp.concatenate([uv[:, 0], zero], axis=-1), jnp.concatenate([zero, uv[:, 1]], axis=-1)],
        axis=1).astype(BF16)
    w["mu"] = row(mu)
    w["w0"] = row(w0)
    pad = lambda m, before: jnp.pad(m, ((before, LANES - before - m.shape[0]), (0, 0))).astype(BF16)
    w["w2p"] = pad(w2[i], 0)
    w["a2p"] = pad(a2[i], RWKV_W_LORA)
    w["a0"] = row(a0)
    w["g2"] = g2[i].astype(BF16)
    w["k_k"] = row(k_k)
    w["k_a"] = row(k_a)
    w["r_k"] = row(r_k)
    w["ln_g"] = row(ln_g)
    w["ln_b"] = row(ln_b)
    head = jnp.arange(RWKV_DIM) // RWKV_HEAD
    w["ones_bd"] = (head[:, None] == head[None, :]).astype(BF16)
    w["w_out_a"] = w_out[i][:MLA_HEADS * MLA_V].astype(BF16)
    w["w_out_b"] = w_out[i][MLA_HEADS * MLA_V:].astype(BF16)
    w["ffn_gu"] = ffn_gu[i].astype(BF16)
    w["ffn_down"] = ffn_down[i].astype(BF16)
    return w


def _prep_odd(i, norm_mix, norm_ffn, w_in, a2, ab, gla_norm, w_out, router, moe_gu, moe_down):
    w = {}
    row = lambda v: v[i].reshape(1, -1)
    w_in = w_in[i]
    w["norm_mix"] = row(norm_mix)
    w["norm_ffn"] = row(norm_ffn)
    w["w_q"] = w_in[:, :GLA_KDIM].astype(BF16)
    w["w_k"] = w_in[:, GLA_KDIM:2 * GLA_KDIM].astype(BF16)
    w["w_v"] = w_in[:, 2 * GLA_KDIM:2 * GLA_KDIM + GLA_VDIM].astype(BF16)
    w["w_g"] = w_in[:, 2 * GLA_KDIM + GLA_VDIM:2 * GLA_KDIM + 2 * GLA_VDIM].astype(BF16)
    w["w_xa"] = jnp.pad(w_in[:, 2 * GLA_KDIM + 2 * GLA_VDIM:],
                        ((0, 0), (0, LANES - GLA_GATE_RANK))).astype(BF16)
    w["a2p"] = jnp.pad(a2[i], ((0, LANES - GLA_GATE_RANK), (0, 0))).astype(BF16)
    w["ab"] = row(ab)
    w["gla_norm"] = row(gla_norm)
    w["w_out"] = w_out[i].astype(BF16)
    w["router"] = jnp.pad(router[i], ((0, 0), (0, LANES - N_EXPERTS)))
    w["layer"] = i
    w["moe_gu"] = moe_gu
    w["moe_down"] = moe_down
    return w


def _rope_tables(pos, reps):
    inv = ROPE_THETA ** (-jnp.arange(0, MLA_ROPE, 2, dtype=F32) / MLA_ROPE)
    ang = pos.astype(F32)[:, None] * inv[None, :]
    cos, sin = jnp.cos(ang), jnp.sin(ang)
    pad = ((0, 0), (0, LANES - MLA_ROPE))
    cs = jnp.tile(jnp.pad(jnp.concatenate([cos, cos], axis=-1), pad), (reps, 1))
    sn = jnp.tile(jnp.pad(jnp.concatenate([-sin, sin], axis=-1), pad), (reps, 1))
    return {"cs": cs, "sn": sn, "cs8": jnp.tile(cs, (1, MLA_HEADS)), "sn8": jnp.tile(sn, (1, MLA_HEADS))}


def _even_layer(x, batch, seq, tabs, state, shift0, past, w, tm, tc, after=()):
    n = batch * seq
    lat, lat_b, q_lat, q_pe, rw = _even_in(x, w, tabs, tm)
    if past is None:
        o_lat = _mla_prompt(q_lat, q_pe, lat_b, batch, seq)
    else:
        cache, layer, page_table = past
        rows = seq * MLA_HEADS
        q_full = jnp.concatenate([q_lat.reshape(batch, rows, MLA_KV_RANK),
                                  q_pe.reshape(batch, rows, LANES)[:, :, :MLA_ROPE]], axis=-1)
        new_pad_t = jnp.pad(lat_b.reshape(batch, seq, MLA_LATB)[:, :, :MLA_LAT],
                            ((0, 0), (0, PAGE_SIZE - seq), (0, 0))).transpose(0, 2, 1)
        o_lat = _mla_decode(page_table, q_full, new_pad_t, cache.transpose(0, 1, 3, 2), layer)
        o_lat = o_lat.reshape(n, MLA_HEADS * MLA_KV_RANK)

    rw3 = rw.reshape(batch, seq, RWKV_PROJ)
    xs5, v, g, rkv = _rwkv_prep(rw, shift0, w, tm, seq)
    y_l, s_l = _rwkv_scan(_scan_vec_layout(xs5, batch, seq), _scan_val_layout(v, batch, seq),
                          _scan_state_layout(state, batch), (o_lat,) + tuple(after), tc)
    y = _scan_val_unlayout(y_l, batch, seq)
    new_state = _scan_state_unlayout(s_l, batch)

    x = _even_out(y, rkv, g, o_lat, x, w, tm)
    x = _ffn(x, w["norm_ffn"], w["ffn_gu"], w["ffn_down"], tm)
    return x, lat.reshape(batch, seq, MLA_LAT), new_state, rw3[:, -1], o_lat


def _odd_mixer_layer(x, batch, seq, state, w, tm):
    q, k, v, gate, la = _odd_in(x, w, tm)
    seq_p = -(-seq // GLA_CHUNK) * GLA_CHUNK
    if seq_p != seq:
        padr = lambda t: jnp.pad(t.reshape(batch, seq, -1), ((0, 0), (0, seq_p - seq), (0, 0))
                                 ).reshape(batch * seq_p, -1)
        qp, kp, vp, lap = padr(q), padr(k), padr(v), padr(la)
    else:
        qp, kp, vp, lap = q, k, v, la
    o, st = _gla(qp, kp, vp, lap, state.transpose(0, 1, 3, 2), batch, seq_p)
    if seq_p != seq:
        o = o.reshape(batch, seq_p, GLA_VDIM)[:, :seq].reshape(batch * seq, GLA_VDIM)
    return _odd_out(o, gate, x, w, tm), st.transpose(0, 1, 3, 2)


def _moe_all_groups(xs_groups, tms, w, final_norm, tm_moe):
    sizes = [x.shape[0] for x in xs_groups]
    n_total = sum(sizes)
    bufs, row0 = None, 0
    for x, tm in zip(xs_groups, tms):
        bufs = _router(x, w["norm_ffn"], w["router"], tm, n_total, row0, bufs)
        row0 += x.shape[0]
    xn2, idx, gates = bufs
    local, wstart, src, tile_expert, tile_valid = _route(idx[:, :2], tm_moe, MOE_TMC)
    rows = _moe_gather(src, xn2, tm_moe)
    h = _moe_up(tile_expert, tile_valid, rows, w["moe_gu"], w["layer"], tm_moe)
    ys = _moe_down(tile_expert, tile_valid, h, w["moe_down"], w["layer"], tm_moe)
    g1b = jnp.broadcast_to(gates[:, 0:1], (n_total, LANES))
    g2b = jnp.broadcast_to(gates[:, 1:2], (n_total, LANES))
    outs, row0 = [], 0
    for x in xs_groups:
        outs.append(_moe_combine(local, wstart, ys, g1b, g2b, x, final_norm, MOE_TMC, row0))
        row0 += x.shape[0]
    return outs


def kernel(x_prompt, x_sample, cache_mla, state_rwkv, state_rwkv_shift, state_gla, page_table, norm_mix_even, norm_ffn_even, w_in_even, mla_q_norm, mla_kv_norm, mla_w_uq, mla_w_uk, mla_w_uv, rwkv_mu, rwkv_w0, rwkv_w2, rwkv_a0, rwkv_a2, rwkv_g2, rwkv_k_k, rwkv_k_a, rwkv_r_k, rwkv_ln_g, rwkv_ln_b, w_out_even, ffn_w_gu_even, ffn_w_down_even, norm_mix_odd, norm_ffn_odd, w_in_odd, gla_a2, gla_ab, gla_norm, w_out_odd, moe_router, moe_w_gu, moe_w_down, final_norm):
    bp, tp, _ = x_prompt.shape
    bs, ts, _ = x_sample.shape
    past_len = page_table.shape[1] * PAGE_SIZE
    tm_p, tm_s = 512, bs * ts
    we = _prep_even(0, norm_mix_even, norm_ffn_even, w_in_even, mla_q_norm, mla_kv_norm, mla_w_uq,
                    mla_w_uk, mla_w_uv, rwkv_mu, rwkv_w0, rwkv_w2, rwkv_a0, rwkv_a2, rwkv_g2,
                    rwkv_k_k, rwkv_k_a, rwkv_r_k, rwkv_ln_g, rwkv_ln_b, w_out_even, ffn_w_gu_even,
                    ffn_w_down_even)
    wo = _prep_odd(0, norm_mix_odd, norm_ffn_odd, w_in_odd, gla_a2, gla_ab, gla_norm, w_out_odd,
                   moe_router, moe_w_gu, moe_w_down)
    fn = final_norm.reshape(1, -1)
    tabs_p = _rope_tables(jnp.arange(tp), 1)
    tabs_s = _rope_tables(past_len + jnp.arange(ts), bs)

    hp = x_prompt.reshape(bp * tp, D_MODEL)
    hs = x_sample.reshape(bs * ts, D_MODEL)
    zeros_state = jnp.zeros((bp, RWKV_HEADS, RWKV_HEAD, RWKV_HEAD), F32)
    zeros_shift = jnp.zeros((bp, RWKV_PROJ), F32)
    hp, lat_p, rs_p, sh_p, _ = _even_layer(hp, bp, tp, tabs_p, zeros_state, zeros_shift, None, we,
                                           tm_p, 64)
    hs, lat_s, rs_s, sh_s, _ = _even_layer(hs, bs, ts, tabs_s, state_rwkv[0], state_rwkv_shift[0],
                                           (cache_mla, 0, page_table), we, tm_s, ts)
    zeros_gla = jnp.zeros((bp, GLA_HEADS, GLA_DK, GLA_DV), F32)
    hp, gs_p = _odd_mixer_layer(hp, bp, tp, zeros_gla, wo, tm_p)
    hs, gs_s = _odd_mixer_layer(hs, bs, ts, state_gla[0], wo, tm_s)
    yp, ys = _moe_all_groups([hp, hs], [tm_p, tm_s], wo, fn, 512)
    return (yp.reshape(bp, tp, D_MODEL), ys.reshape(bs, ts, D_MODEL), lat_p[None], lat_s[None],
            rs_p[None], rs_s[None], sh_p[None], sh_s[None], gs_p[None], gs_s[None])
```

```python
import functools

import jax
import jax.numpy as jnp
from jax import lax
from jax.experimental import pallas as pl
from jax.experimental.pallas import tpu as pltpu

F32 = jnp.float32
BF16 = jnp.bfloat16

D_MODEL = 1024
PAGE_SIZE = 128
NORM_EPS = 1e-6

MLA_HEADS = 8
MLA_NOPE = 64
MLA_ROPE = 32
MLA_V = 64
MLA_Q_RANK = 384
MLA_KV_RANK = 256
MLA_LAT = MLA_KV_RANK + MLA_ROPE
MLA_LATB = MLA_KV_RANK + 128
MLA_SCALE = (MLA_NOPE + MLA_ROPE) ** -0.5
ROPE_THETA = 10000.0

RWKV_HEADS = 8
RWKV_HEAD = 64
RWKV_DIM = RWKV_HEADS * RWKV_HEAD
RWKV_W_LORA = 64
RWKV_A_LORA = 64
RWKV_G_LORA = 128
RWKV_PROJ = 3 * RWKV_DIM + RWKV_W_LORA + RWKV_A_LORA + RWKV_G_LORA
RWKV_LN_EPS = 64e-5

GLA_HEADS = 4
GLA_DK = 128
GLA_DV = 256
GLA_KDIM = GLA_HEADS * GLA_DK
GLA_VDIM = GLA_HEADS * GLA_DV
GLA_GATE_RANK = 16
GLA_GATE_NORM = 16.0
GLA_CHUNK = 128

D_FF = 2816
N_EXPERTS = 8
D_FF_EXPERT = 3584

LANES = 128
VMEM_LIMIT = 56 * 1024 * 1024
NEG_BIG = -1e30
LOG2_E = 1.4426950408889634
Q_PRESCALE = MLA_SCALE * LOG2_E


def _cparams(sem):
    return pltpu.CompilerParams(dimension_semantics=sem, vmem_limit_bytes=VMEM_LIMIT)


def _const_spec(shape):
    nd = len(shape)
    return pl.BlockSpec(shape, lambda *_: (0,) * nd)


def _row_spec(tm, width):
    return pl.BlockSpec((tm, width), lambda i: (i, 0))


def _dot(a, b):
    return jnp.dot(a.astype(BF16), b.astype(BF16), preferred_element_type=F32)


def _dot_nt(a, b):
    return lax.dot_general(a.astype(BF16), b.astype(BF16), (((1,), (1,)), ((), ())),
                           preferred_element_type=F32)


def _split2(x):
    hi = x.astype(BF16)
    lo = (x - hi.astype(F32)).astype(BF16)
    return hi, lo


def _split3(x):
    hi = x.astype(BF16)
    r1 = x - hi.astype(F32)
    mid = r1.astype(BF16)
    lo = (r1 - mid.astype(F32)).astype(BF16)
    return hi, mid, lo


def _dot_exact_rhs(x, e):
    hi, mid, lo = _split3(x)
    return (jnp.dot(hi, e, preferred_element_type=F32) + jnp.dot(mid, e, preferred_element_type=F32)
            + jnp.dot(lo, e, preferred_element_type=F32))


def _dot_exact_lhs(e, x):
    hi, mid, lo = _split3(x)
    return (jnp.dot(e, hi, preferred_element_type=F32) + jnp.dot(e, mid, preferred_element_type=F32)
            + jnp.dot(e, lo, preferred_element_type=F32))


def _dot_f32ish(a, b):
    ah, al = _split2(a)
    bh, bl = _split2(b)
    return (jnp.dot(ah, bh, preferred_element_type=F32) + jnp.dot(ah, bl, preferred_element_type=F32)
            + jnp.dot(al, bh, preferred_element_type=F32))


def _lane_tile(x, width):
    return x if width == LANES else jnp.concatenate([x] * (width // LANES), axis=1)


def _rms(x, g, eps=NORM_EPS):
    return x * lax.rsqrt(jnp.mean(x * x, axis=-1, keepdims=True) + eps) * g


def _sigmoid(x):
    return 1.0 / (1.0 + jnp.exp(-x))


def _softplus(x):
    return jnp.maximum(x, 0.0) + jnp.log(1.0 + jnp.exp(-jnp.abs(x)))


def _fold_qlat_kernel(uq_ref, uk_ref, o_ref):
    a = uq_ref[...]
    b = uk_ref[...]
    ah, al = _split2(a)
    bh, bl = _split2(b)
    dn = (((1,), (1,)), ((), ()))
    o = (lax.dot_general(ah, bh, dn, preferred_element_type=F32)
         + lax.dot_general(ah, bl, dn, preferred_element_type=F32)
         + lax.dot_general(al, bh, dn, preferred_element_type=F32))
    o_ref[...] = o.astype(BF16)


def _fold_qlat(uq_nope, uk):
    return pl.pallas_call(
        _fold_qlat_kernel,
        grid=(MLA_HEADS,),
        in_specs=[pl.BlockSpec((None, MLA_Q_RANK, MLA_NOPE), lambda h: (h, 0, 0)),
                  pl.BlockSpec((None, MLA_KV_RANK, MLA_NOPE), lambda h: (h, 0, 0))],
        out_specs=pl.BlockSpec((MLA_Q_RANK, MLA_KV_RANK), lambda h: (0, h)),
        out_shape=jax.ShapeDtypeStruct((MLA_Q_RANK, MLA_HEADS * MLA_KV_RANK), BF16),
        compiler_params=_cparams(("arbitrary",)),
    )(uq_nope, uk)


def _even_in_kernel(x_ref, g_ref, wq_ref, wckv_ref, wpa_ref, wpb_ref, wrw_ref, qn_ref, kvn_ref,
                    cs_ref, sn_ref, wql_ref, wqa_ref, wqb_ref, cs8_ref, sn8_ref,
                    lat_ref, latb_ref, ql_ref, qpe_ref, rw_ref):
    xn = _rms(x_ref[...], g_ref[...]).astype(BF16)
    cq = _rms(_dot(xn, wq_ref[...]), qn_ref[...]).astype(BF16)
    ql_ref[...] = (_dot(cq, wql_ref[...]) * Q_PRESCALE).astype(BF16)
    qpe = _dot(cq, wqa_ref[...]) * cs8_ref[...] + _dot(cq, wqb_ref[...]) * sn8_ref[...]
    qpe_ref[...] = (qpe * Q_PRESCALE).astype(BF16)
    ckv = _rms(_dot(xn, wckv_ref[...]), kvn_ref[...])
    kpe = _dot(xn, wpa_ref[...]) * cs_ref[...] + _dot(xn, wpb_ref[...]) * sn_ref[...]
    lat_ref[:, :MLA_KV_RANK] = ckv
    lat_ref[:, MLA_KV_RANK:] = kpe[:, :MLA_ROPE]
    latb_ref[:, :MLA_KV_RANK] = ckv.astype(BF16)
    latb_ref[:, MLA_KV_RANK:] = kpe.astype(BF16)
    rw_ref[...] = _dot(xn, wrw_ref[...])


def _even_in(x, w, tabs, tm):
    n = x.shape[0]
    nt = tabs["cs"].shape[0] // tm
    tab = lambda width: pl.BlockSpec((tm, width), lambda i: (i % nt, 0))
    hq = MLA_HEADS * MLA_KV_RANK
    hr = MLA_HEADS * LANES
    return pl.pallas_call(
        _even_in_kernel,
        grid=(n // tm,),
        in_specs=[_row_spec(tm, D_MODEL), _const_spec((1, D_MODEL)),
                  _const_spec((D_MODEL, MLA_Q_RANK)), _const_spec((D_MODEL, MLA_KV_RANK)),
                  _const_spec((D_MODEL, LANES)), _const_spec((D_MODEL, LANES)),
                  _const_spec((D_MODEL, RWKV_PROJ)), _const_spec((1, MLA_Q_RANK)),
                  _const_spec((1, MLA_KV_RANK)), tab(LANES), tab(LANES),
                  _const_spec((MLA_Q_RANK, hq)), _const_spec((MLA_Q_RANK, hr)),
                  _const_spec((MLA_Q_RANK, hr)), tab(hr), tab(hr)],
        out_specs=[_row_spec(tm, MLA_LAT), _row_spec(tm, MLA_LATB), _row_spec(tm, hq),
                   _row_spec(tm, hr), _row_spec(tm, RWKV_PROJ)],
        out_shape=[jax.ShapeDtypeStruct((n, MLA_LAT), F32), jax.ShapeDtypeStruct((n, MLA_LATB), BF16),
                   jax.ShapeDtypeStruct((n, hq), BF16), jax.ShapeDtypeStruct((n, hr), BF16),
                   jax.ShapeDtypeStruct((n, RWKV_PROJ), F32)],
        compiler_params=_cparams(("parallel",)),
    )(x, w["norm_mix"], w["w_q"], w["w_ckv"], w["w_pe_a"], w["w_pe_b"], w["w_rw"], w["q_norm"],
      w["kv_norm"], tabs["cs"], tabs["sn"], w["w_qlat"], w["w_qpe_a"], w["w_qpe_b"],
      tabs["cs8"], tabs["sn8"])


ATT_TQ = 256
ATT_TK = 512


def _mla_prompt_kernel(qi_ref, kj_ref, ql_ref, qpe_ref, lat_ref, o_ref,
                       m_sc, l_sc, a_sc, acc_sc, s_sc, p_sc):
    step = pl.program_id(1)
    i = qi_ref[step]
    j = kj_ref[step]
    heads = range(MLA_HEADS)

    @pl.when(j == 0)
    def _():
        m_sc[...] = jnp.full(m_sc.shape, NEG_BIG, F32)
        l_sc[...] = jnp.zeros(l_sc.shape, F32)
        acc_sc[...] = jnp.zeros(acc_sc.shape, F32)

    def tile(masked):
        ckv = lat_ref[:, :MLA_KV_RANK]
        kpe = lat_ref[:, MLA_KV_RANK:]
        for h in heads:
            s_sc[h] = (_dot_nt(ql_ref[:, h * MLA_KV_RANK:(h + 1) * MLA_KV_RANK], ckv)
                       + _dot_nt(qpe_ref[:, h * LANES:(h + 1) * LANES], kpe))
        for h in heads:
            s = s_sc[h]
            if masked:
                tok = lax.broadcasted_iota(jnp.int32, s.shape, 0) + offset
                key = lax.broadcasted_iota(jnp.int32, s.shape, 1)
                s = jnp.where(key <= tok, s, NEG_BIG)
            m_prev = m_sc[h]
            m_new = jnp.maximum(m_prev, jnp.max(s, axis=-1, keepdims=True))
            alpha = jnp.exp2(m_prev - m_new)
            p = jnp.exp2(s - _lane_tile(m_new, ATT_TK))
            l_sc[h] = alpha * l_sc[h] + jnp.sum(p, axis=-1, keepdims=True)
            m_sc[h] = m_new
            a_sc[h] = alpha
            p_sc[h] = p.astype(BF16)
        for h in heads:
            acc_sc[h] = _lane_tile(a_sc[h], MLA_KV_RANK) * acc_sc[h] + _dot(p_sc[h], ckv)

    offset = i * ATT_TQ - j * ATT_TK
    on_diagonal = offset < ATT_TK - 1

    @pl.when(jnp.logical_not(on_diagonal))
    def _():
        tile(False)

    @pl.when(on_diagonal)
    def _():
        tile(True)

    @pl.when(j == (i * ATT_TQ + ATT_TQ - 1) // ATT_TK)
    def _():
        for h in heads:
            o_ref[:, h * MLA_KV_RANK:(h + 1) * MLA_KV_RANK] = (
                acc_sc[h] / _lane_tile(l_sc[h], MLA_KV_RANK)).astype(BF16)


def _mla_prompt(q_lat, q_pe, lat_b, batch, seq):
    nq = seq // ATT_TQ
    nk = seq // ATT_TK
    pairs = [(i, j) for i in range(nq)
             for j in range((i * ATT_TQ + ATT_TQ - 1) // ATT_TK + 1)]
    qi = jnp.array([p[0] for p in pairs], jnp.int32)
    kj = jnp.array([p[1] for p in pairs], jnp.int32)
    hq = MLA_HEADS * MLA_KV_RANK
    grid_spec = pltpu.PrefetchScalarGridSpec(
        num_scalar_prefetch=2,
        grid=(batch, len(pairs)),
        in_specs=[pl.BlockSpec((ATT_TQ, hq), lambda b, s, qi, kj: (b * nq + qi[s], 0)),
                  pl.BlockSpec((ATT_TQ, MLA_HEADS * LANES), lambda b, s, qi, kj: (b * nq + qi[s], 0)),
                  pl.BlockSpec((ATT_TK, MLA_LATB), lambda b, s, qi, kj: (b * nk + kj[s], 0))],
        out_specs=pl.BlockSpec((ATT_TQ, hq), lambda b, s, qi, kj: (b * nq + qi[s], 0)),
        scratch_shapes=[pltpu.VMEM((MLA_HEADS, ATT_TQ, LANES), F32),
                        pltpu.VMEM((MLA_HEADS, ATT_TQ, LANES), F32),
                        pltpu.VMEM((MLA_HEADS, ATT_TQ, LANES), F32),
                        pltpu.VMEM((MLA_HEADS, ATT_TQ, MLA_KV_RANK), F32),
                        pltpu.VMEM((MLA_HEADS, ATT_TQ, ATT_TK), F32),
                        pltpu.VMEM((MLA_HEADS, ATT_TQ, ATT_TK), BF16)],
    )
    return pl.pallas_call(
        _mla_prompt_kernel,
        grid_spec=grid_spec,
        out_shape=jax.ShapeDtypeStruct(q_lat.shape, BF16),
        compiler_params=_cparams(("parallel", "arbitrary")),
    )(qi, kj, q_lat, q_pe, lat_b)


PAGES_PER_STEP = 64
DECODE_GROUPS = 16


def _mla_decode_kernel(pt_ref, q_ref, new_ref, *rest):
    page_refs = rest[:PAGES_PER_STEP]
    o_ref, m_sc, l_sc, acc_sc = rest[PAGES_PER_STEP:]
    j = pl.program_id(1)
    q = q_ref[0]

    @pl.when(j == 0)
    def _():
        m_sc[...] = jnp.full(m_sc.shape, NEG_BIG, F32)
        l_sc[...] = jnp.zeros(l_sc.shape, F32)
        acc_sc[...] = jnp.zeros(acc_sc.shape, F32)

    def update(state, s, values_t):
        m_prev, l_prev, acc = state
        m_new = jnp.maximum(m_prev, jnp.max(s, axis=-1, keepdims=True))
        alpha = jnp.exp2(m_prev - m_new)
        p = jnp.exp2(s - _lane_tile(m_new, s.shape[1]))
        l_new = alpha * l_prev + jnp.sum(p, axis=-1, keepdims=True)
        return m_new, l_new, _lane_tile(alpha, MLA_KV_RANK) * acc + _dot_nt(p, values_t)

    group = PAGES_PER_STEP // DECODE_GROUPS
    keys = [jnp.concatenate([pr[...].astype(BF16) for pr in page_refs[g * group:(g + 1) * group]],
                            axis=1) for g in range(DECODE_GROUPS)]
    scores = [_dot(q, kt) for kt in keys]
    state = (m_sc[...], l_sc[...], acc_sc[...])
    for s, kt in zip(scores, keys):
        state = update(state, s, kt[:MLA_KV_RANK, :])
    m_sc[...], l_sc[...], acc_sc[...] = state

    @pl.when(j == pl.num_programs(1) - 1)
    def _():
        new_t = new_ref[0]
        sn = _dot(q, new_t)
        tok = lax.broadcasted_iota(jnp.int32, sn.shape, 0) >> 3
        key = lax.broadcasted_iota(jnp.int32, sn.shape, 1)
        sn = jnp.where(key <= tok, sn, NEG_BIG)
        _, l_fin, acc_fin = update(state, sn, new_t[:MLA_KV_RANK, :])
        o_ref[0] = (acc_fin / _lane_tile(l_fin, MLA_KV_RANK)).astype(BF16)


def _mla_decode(page_table, q_full, new_pad_t, cache_t, layer):
    db, n_pages = page_table.shape
    rows = q_full.shape[1]
    steps = n_pages // PAGES_PER_STEP

    def page_spec(p):
        return pl.BlockSpec((None, None, MLA_LAT, PAGE_SIZE),
                            lambda b, j, pt: (layer, pt[b, j * PAGES_PER_STEP + p], 0, 0))

    grid_spec = pltpu.PrefetchScalarGridSpec(
        num_scalar_prefetch=1,
        grid=(db, steps),
        in_specs=[pl.BlockSpec((1, rows, MLA_LAT), lambda b, j, pt: (b, 0, 0)),
                  pl.BlockSpec((1, MLA_LAT, PAGE_SIZE), lambda b, j, pt: (b, 0, 0))]
        + [page_spec(p) for p in range(PAGES_PER_STEP)],
        out_specs=pl.BlockSpec((1, rows, MLA_KV_RANK), lambda b, j, pt: (b, 0, 0)),
        scratch_shapes=[pltpu.VMEM((rows, LANES), F32), pltpu.VMEM((rows, LANES), F32),
                        pltpu.VMEM((rows, MLA_KV_RANK), F32)],
    )
    return pl.pallas_call(
        _mla_decode_kernel,
        grid_spec=grid_spec,
        out_shape=jax.ShapeDtypeStruct((db, rows, MLA_KV_RANK), BF16),
        compiler_params=_cparams(("parallel", "arbitrary")),
    )(page_table, q_full, new_pad_t, *([cache_t] * PAGES_PER_STEP))


def _rwkv_prep_kernel(rw_ref, before_ref, sh_ref, mu_ref, w0_ref, w2_ref, a0_ref, a2_ref, g2_ref,
                      kk_ref, ka_ref, rk_ref, ones_ref, xs_ref, v_ref, g_ref, rkv_ref, *, tm, seq):
    rw = rw_ref[...]
    rolled = pltpu.roll(rw, 1, axis=0)
    row = lax.broadcasted_iota(jnp.int32, rw.shape, 0)
    if seq >= tm:
        at_start = pl.program_id(0) % (seq // tm) == 0
        first = jnp.where(at_start, sh_ref[...], before_ref[7:8, :])
        prev = jnp.where(row == 0, first, rolled)
    else:
        prev = jnp.where((row & (seq - 1)) == 0, sh_ref[...], rolled)
    xs = rw + (prev - rw) * mu_ref[...]
    d = RWKV_DIM
    r = xs[:, :d]
    k = xs[:, d:2 * d]
    v = xs[:, 2 * d:3 * d]
    xwa = xs[:, 3 * d:3 * d + LANES]
    xg = xs[:, 3 * d + LANES:]
    ones = ones_ref[...]
    w_log = -_softplus(-(w0_ref[...] + _dot(jnp.tanh(xwa), w2_ref[...]))) - 0.5
    a = _sigmoid(a0_ref[...] + _dot(xwa, a2_ref[...]))
    g_ref[...] = _dot(_sigmoid(xg), g2_ref[...])
    kk = k * kk_ref[...]
    ss = _dot_exact_rhs(kk * kk, ones)
    kk = kk / jnp.maximum(jnp.sqrt(ss), 1e-12)
    k2 = k * (1.0 + (a - 1.0) * ka_ref[...])
    xs_ref[0] = -kk
    xs_ref[1] = jnp.exp(-jnp.exp(w_log))
    xs_ref[2] = kk * a
    xs_ref[3] = k2
    xs_ref[4] = r
    v_ref[...] = v
    rkv_ref[...] = _dot_exact_rhs(r * k2 * rk_ref[...], ones) * v


def _rwkv_prep(rw, shift0, w, tm, seq):
    n = rw.shape[0]
    d = RWKV_DIM
    vec = _const_spec((1, d))
    if seq >= tm:
        tiles = seq // tm
        sh = shift0.reshape(-1, 1, RWKV_PROJ)
        sh_spec = pl.BlockSpec((None, 1, RWKV_PROJ), lambda i: (i // tiles, 0, 0))
    else:
        sh = jnp.repeat(shift0, seq, axis=0)
        sh_spec = _row_spec(tm, RWKV_PROJ)
    before_spec = pl.BlockSpec((8, RWKV_PROJ), lambda i: (jnp.maximum(i * (tm // 8) - 1, 0), 0))
    return pl.pallas_call(
        functools.partial(_rwkv_prep_kernel, tm=tm, seq=seq),
        grid=(n // tm,),
        in_specs=[_row_spec(tm, RWKV_PROJ), before_spec, sh_spec, _const_spec((1, RWKV_PROJ)),
                  vec, _const_spec((LANES, d)), vec, _const_spec((LANES, d)),
                  _const_spec((RWKV_G_LORA, d)), vec, vec, vec, _const_spec((d, d))],
        out_specs=[pl.BlockSpec((5, tm, d), lambda i: (0, i, 0))] + [_row_spec(tm, d)] * 3,
        out_shape=[jax.ShapeDtypeStruct((5, n, d), F32)] + [jax.ShapeDtypeStruct((n, d), F32)] * 3,
        compiler_params=_cparams(("parallel",)),
    )(rw, rw, sh, w["mu"], w["w0"], w["w2p"], w["a0"], w["a2p"], w["g2"], w["k_k"], w["k_a"],
      w["r_k"], w["ones_bd"])


SCAN_KH = RWKV_HEAD // 2
SCAN_PAIRS = LANES // 2
SCAN_VR = RWKV_HEAD // 2


def _rwkv_scan_kernel(x_ref, v_ref, s0_ref, *rest, tc):
    y_ref, s_ref, c_sc, d_sc = rest[-4:]
    @pl.when(pl.program_id(1) == 0)
    def _():
        s_ref[...] = s0_ref[...]

    half_a = slice(0, SCAN_VR)
    half_b = slice(SCAN_VR, RWKV_HEAD)

    def both_halves(p):
        return p + pltpu.roll(p, SCAN_PAIRS, axis=1)

    def key_dot(u, w):
        return both_halves(jnp.sum(u * w, axis=0, keepdims=True))

    def first_partial(rows):
        p = s_ref[0, 0, rows, :] * x_ref[0, 0, 0, 0:1, :]
        for k in range(1, SCAN_KH):
            p = p + s_ref[0, k, rows, :] * x_ref[0, 0, 0, k:k + 1, :]
        return p

    def half_step(t, rows, sa):
        v_half = v_ref[0, t, rows, :]
        v = jnp.concatenate([v_half, v_half], axis=1)
        q = None
        y = None
        for k in range(SCAN_KH):
            s_old = s_ref[0, k, rows, :]
            qk = s_old * c_sc[k:k + 1, :]
            sn = (s_old * x_ref[1, 0, t, k:k + 1, :] + sa * x_ref[2, 0, t, k:k + 1, :]
                  + v * x_ref[3, 0, t, k:k + 1, :])
            s_ref[0, k, rows, :] = sn
            yk = sn * x_ref[4, 0, t, k:k + 1, :]
            q = qk if q is None else q + qk
            y = yk if y is None else y + yk
        return q, sa * d_sc[0:1, :] + v * d_sc[1:2, :], y

    def store_y(t, y_a, y_b):
        y_ref[0, t, half_a, :] = both_halves(y_a)[:, :SCAN_PAIRS]
        y_ref[0, t, half_b, :] = both_halves(y_b)[:, :SCAN_PAIRS]

    def step(t, carry):
        sa_a, q_b, corr_b, y_a, y_b = carry
        store_y(jnp.maximum(t - 1, 0), y_a, y_b)
        a_next = x_ref[0, 0, jnp.minimum(t + 1, tc - 1)]
        c_sc[...] = x_ref[1, 0, t] * a_next
        d_sc[0:1, :] = key_dot(x_ref[2, 0, t], a_next)
        d_sc[1:2, :] = key_dot(x_ref[3, 0, t], a_next)
        sa_b = both_halves(q_b) + corr_b
        q_a, corr_a, y_a_new = half_step(t, half_a, sa_a)
        sa_a_next = both_halves(q_a) + corr_a
        q_b_next, corr_b_next, y_b_new = half_step(t, half_b, sa_b)
        return sa_a_next, q_b_next, corr_b_next, y_a_new, y_b_new

    zero = jnp.zeros((SCAN_VR, LANES), F32)
    init = (both_halves(first_partial(half_a)), first_partial(half_b), zero, zero, zero)
    final = lax.fori_loop(0, tc, step, init)
    store_y(tc - 1, final[3], final[4])


def _rwkv_scan(xs, v, s0, after, tc):
    _, nb, t, _, _ = xs.shape
    xspec = pl.BlockSpec((5, 1, tc, SCAN_KH, LANES), lambda n, c: (0, n, c, 0, 0))
    vspec = pl.BlockSpec((1, tc, RWKV_HEAD, SCAN_PAIRS), lambda n, c: (n, c, 0, 0))
    sspec = pl.BlockSpec((1, SCAN_KH, RWKV_HEAD, LANES), lambda n, c: (n, 0, 0, 0))
    return pl.pallas_call(
        functools.partial(_rwkv_scan_kernel, tc=tc),
        grid=(nb, t // tc),
        in_specs=[xspec, vspec, sspec] + [pl.BlockSpec(memory_space=pl.ANY)] * len(after),
        out_specs=[vspec, sspec],
        out_shape=[jax.ShapeDtypeStruct(v.shape, F32), jax.ShapeDtypeStruct(s0.shape, F32)],
        scratch_shapes=[pltpu.VMEM((SCAN_KH, LANES), F32), pltpu.VMEM((8, LANES), F32)],
        compiler_params=_cparams(("parallel", "arbitrary")),
    )(xs, v, s0, *after)


def _even_out_kernel(y_ref, rkv_ref, g_ref, lng_ref, lnb_ref, ones_ref, ol_ref, wuv_ref, woa_ref,
                     wob_ref, x_ref, o_ref):
    ones = ones_ref[...]
    y = y_ref[...]
    inv = 1.0 / RWKV_HEAD
    mean = _dot_exact_rhs(y, ones) * inv
    dlt = y - mean
    var = _dot_exact_rhs(dlt * dlt, ones) * inv
    yn = dlt * lax.rsqrt(var + RWKV_LN_EPS) * lng_ref[...] + lnb_ref[...] + rkv_ref[...]
    ob = (yn * g_ref[...]).astype(BF16)
    pair = 2 * MLA_KV_RANK
    oa = jnp.concatenate(
        [_dot(ol_ref[:, p * pair:(p + 1) * pair], wuv_ref[p]) for p in range(MLA_HEADS // 2)], axis=1)
    o_ref[...] = x_ref[...] + _dot(oa, woa_ref[...]) + _dot(ob, wob_ref[...])


def _even_out(y, rkv, g, o_lat, x, w, tm):
    n = x.shape[0]
    d = RWKV_DIM
    hq = MLA_HEADS * MLA_KV_RANK
    return pl.pallas_call(
        _even_out_kernel,
        grid=(n // tm,),
        in_specs=[_row_spec(tm, d), _row_spec(tm, d), _row_spec(tm, d), _const_spec((1, d)),
                  _const_spec((1, d)), _const_spec((d, d)), _row_spec(tm, hq),
                  _const_spec((MLA_HEADS // 2, 2 * MLA_KV_RANK, 2 * MLA_V)),
                  _const_spec((MLA_HEADS * MLA_V, D_MODEL)), _const_spec((d, D_MODEL)),
                  _row_spec(tm, D_MODEL)],
        out_specs=_row_spec(tm, D_MODEL),
        out_shape=jax.ShapeDtypeStruct((n, D_MODEL), F32),
        compiler_params=_cparams(("parallel",)),
    )(y, rkv, g, w["ln_g"], w["ln_b"], w["ones_bd"], o_lat, w["w_uv_bd"], w["w_out_a"],
      w["w_out_b"], x)


FFN_TF = 1408


def _ffn_kernel(x_ref, g_ref, wg_ref, wu_ref, wd_ref, o_ref, xn_sc, acc_sc):
    f = pl.program_id(1)

    @pl.when(f == 0)
    def _():
        xn_sc[...] = _rms(x_ref[...], g_ref[...]).astype(BF16)
        acc_sc[...] = jnp.zeros(acc_sc.shape, F32)

    xn = xn_sc[...]
    gate = _dot(xn, wg_ref[...])
    up = _dot(xn, wu_ref[...])
    acc_sc[...] += _dot(gate * _sigmoid(gate) * up, wd_ref[...])

    @pl.when(f == pl.num_programs(1) - 1)
    def _():
        o_ref[...] = x_ref[...] + acc_sc[...]


def _ffn(x, g, w_gu, w_down, tm):
    n = x.shape[0]
    nf = D_FF // FFN_TF
    return pl.pallas_call(
        _ffn_kernel,
        grid=(n // tm, nf),
        in_specs=[pl.BlockSpec((tm, D_MODEL), lambda i, f: (i, 0)),
                  pl.BlockSpec((1, D_MODEL), lambda i, f: (0, 0)),
                  pl.BlockSpec((D_MODEL, FFN_TF), lambda i, f: (0, f)),
                  pl.BlockSpec((D_MODEL, FFN_TF), lambda i, f: (0, nf + f)),
                  pl.BlockSpec((FFN_TF, D_MODEL), lambda i, f: (f, 0))],
        out_specs=pl.BlockSpec((tm, D_MODEL), lambda i, f: (i, 0)),
        out_shape=jax.ShapeDtypeStruct((n, D_MODEL), F32),
        scratch_shapes=[pltpu.VMEM((tm, D_MODEL), BF16), pltpu.VMEM((tm, D_MODEL), F32)],
        compiler_params=_cparams(("parallel", "arbitrary")),
    )(x, g, w_gu, w_gu, w_down)


def _odd_in_kernel(x_ref, g_ref, wq_ref, wk_ref, wv_ref, wg_ref, wxa_ref, a2_ref, ab_ref,
                   q_ref, k_ref, v_ref, gate_ref, la_ref):
    xn = _rms(x_ref[...], g_ref[...]).astype(BF16)
    q_ref[...] = _dot(xn, wq_ref[...]) * (GLA_DK ** -0.5)
    k_ref[...] = _dot(xn, wk_ref[...])
    v_ref[...] = _dot(xn, wv_ref[...])
    gate_ref[...] = _dot(xn, wg_ref[...])
    z = _dot(_dot(xn, wxa_ref[...]), a2_ref[...]) + ab_ref[...]
    la_ref[...] = -_softplus(-z) * (1.0 / GLA_GATE_NORM)


def _odd_in(x, w, tm):
    n = x.shape[0]
    return pl.pallas_call(
        _odd_in_kernel,
        grid=(n // tm,),
        in_specs=[_row_spec(tm, D_MODEL), _const_spec((1, D_MODEL)),
                  _const_spec((D_MODEL, GLA_KDIM)), _const_spec((D_MODEL, GLA_KDIM)),
                  _const_spec((D_MODEL, GLA_VDIM)), _const_spec((D_MODEL, GLA_VDIM)),
                  _const_spec((D_MODEL, LANES)), _const_spec((LANES, GLA_KDIM)),
                  _const_spec((1, GLA_KDIM))],
        out_specs=[_row_spec(tm, GLA_KDIM), _row_spec(tm, GLA_KDIM), _row_spec(tm, GLA_VDIM),
                   _row_spec(tm, GLA_VDIM), _row_spec(tm, GLA_KDIM)],
        out_shape=[jax.ShapeDtypeStruct((n, GLA_KDIM), F32), jax.ShapeDtypeStruct((n, GLA_KDIM), F32),
                   jax.ShapeDtypeStruct((n, GLA_VDIM), F32), jax.ShapeDtypeStruct((n, GLA_VDIM), F32),
                   jax.ShapeDtypeStruct((n, GLA_KDIM), F32)],
        compiler_params=_cparams(("parallel",)),
    )(x, w["norm_mix"], w["w_q"], w["w_k"], w["w_v"], w["w_g"], w["w_xa"], w["a2p"], w["ab"])


def _gla_kernel(q_ref, k_ref, v_ref, la_ref, s0_ref, o_ref, st_ref):
    c = GLA_CHUNK

    @pl.when(pl.program_id(1) == 0)
    def _():
        st_ref[...] = s0_ref[...]

    row = lax.broadcasted_iota(jnp.int32, (c, c), 0)
    col = lax.broadcasted_iota(jnp.int32, (c, c), 1)
    tri = row >= col
    tri_b = jnp.where(tri, 1.0, 0.0).astype(BF16)
    heads = range(GLA_HEADS)
    ks = [slice(h * GLA_DK, (h + 1) * GLA_DK) for h in heads]
    vs = [slice(h * GLA_DV, (h + 1) * GLA_DV) for h in heads]
    b = _dot_exact_lhs(tri_b, la_ref[...])
    k = k_ref[...]
    b_end = b[c - 1:c, :]
    qe = (q_ref[...] * jnp.exp(b)).astype(BF16)
    ke = (k * jnp.exp(-b)).astype(BF16)
    k_end = (k * jnp.exp(b_end - b)).astype(BF16)
    e_end = jnp.exp(b_end)
    a_mats = [jnp.where(tri, _dot_nt(qe[:, ks[h]], ke[:, ks[h]]), 0.0).astype(BF16) for h in heads]
    states = [st_ref[0, h] for h in heads]
    for h in heads:
        o_ref[:, vs[h]] = _dot_nt(qe[:, ks[h]], states[h]) + _dot(a_mats[h], v_ref[:, vs[h]])
    for h in heads:
        st_ref[0, h] = states[h] * e_end[:, ks[h]] + _dot(v_ref[:, vs[h]].T, k_end[:, ks[h]])


def _gla(q, k, v, la, s0t, batch, seq):
    nc = seq // GLA_CHUNK
    rspec = lambda width: pl.BlockSpec((GLA_CHUNK, width), lambda b, c: (b * nc + c, 0))
    sspec = pl.BlockSpec((1, GLA_HEADS, GLA_DV, GLA_DK), lambda b, c: (b, 0, 0, 0))
    return pl.pallas_call(
        _gla_kernel,
        grid=(batch, nc),
        in_specs=[rspec(GLA_KDIM), rspec(GLA_KDIM), rspec(GLA_VDIM), rspec(GLA_KDIM), sspec],
        out_specs=[rspec(GLA_VDIM), sspec],
        out_shape=[jax.ShapeDtypeStruct(v.shape, F32), jax.ShapeDtypeStruct(s0t.shape, F32)],
        compiler_params=_cparams(("parallel", "arbitrary")),
    )(q, k, v, la, s0t)


def _odd_out_kernel(o_ref, gate_ref, gn_ref, wo_ref, x_ref, y_ref):
    parts = []
    for h in range(GLA_HEADS):
        vs = slice(h * GLA_DV, (h + 1) * GLA_DV)
        parts.append(_rms(o_ref[:, vs], gn_ref[:, vs]))
    gate = gate_ref[...]
    on = jnp.concatenate(parts, axis=1) * (gate * _sigmoid(gate))
    y_ref[...] = x_ref[...] + _dot(on, wo_ref[...])


def _odd_out(o, gate, x, w, tm):
    n = x.shape[0]
    return pl.pallas_call(
        _odd_out_kernel,
        grid=(n // tm,),
        in_specs=[_row_spec(tm, GLA_VDIM), _row_spec(tm, GLA_VDIM), _const_spec((1, GLA_VDIM)),
                  _const_spec((GLA_VDIM, D_MODEL)), _row_spec(tm, D_MODEL)],
        out_specs=_row_spec(tm, D_MODEL),
        out_shape=jax.ShapeDtypeStruct((n, D_MODEL), F32),
        compiler_params=_cparams(("parallel",)),
    )(o, gate, w["gla_norm"], w["w_out"], x)


def _router_kernel(x_ref, g_ref, wr_ref, *rest):
    xn_ref, idx_ref, gate_ref = rest[-3:]
    xn = _rms(x_ref[...], g_ref[...])
    half = D_MODEL // 2
    xn_ref[0] = xn[:, :half]
    xn_ref[1] = xn[:, half:]
    logits = _dot_f32ish(xn, wr_ref[...])
    lane = lax.broadcasted_iota(jnp.int32, logits.shape, 1)
    logits = jnp.where(lane < N_EXPERTS, logits, NEG_BIG)
    m1 = jnp.max(logits, axis=-1, keepdims=True)
    i1 = jnp.min(jnp.where(logits == m1, lane, LANES), axis=-1, keepdims=True)
    rest = jnp.where(lane == i1, NEG_BIG, logits)
    m2 = jnp.max(rest, axis=-1, keepdims=True)
    i2 = jnp.min(jnp.where(rest == m2, lane, LANES), axis=-1, keepdims=True)
    e2 = jnp.exp(m2 - m1)
    g1 = 1.0 / (1.0 + e2)
    g2 = e2 / (1.0 + e2)
    idx_ref[...] = jnp.where(lane == 0, i1, jnp.where(lane == 1, i2, 0))
    gate_ref[...] = jnp.where(lane == 0, g1, jnp.where(lane == 1, g2, 0.0))


def _router(x, g, wr, tm, n_total, row0, prev=None):
    n = x.shape[0]
    half = D_MODEL // 2
    blk0 = row0 // tm
    prev = () if prev is None else tuple(prev)
    return pl.pallas_call(
        _router_kernel,
        grid=(n // tm,),
        in_specs=[_row_spec(tm, D_MODEL), _const_spec((1, D_MODEL)), _const_spec((D_MODEL, LANES))]
        + [pl.BlockSpec(memory_space=pl.ANY)] * len(prev),
        out_specs=[pl.BlockSpec((2, tm, half), lambda i: (0, blk0 + i, 0)),
                   pl.BlockSpec((tm, LANES), lambda i: (blk0 + i, 0)),
                   pl.BlockSpec((tm, LANES), lambda i: (blk0 + i, 0))],
        out_shape=[jax.ShapeDtypeStruct((2, n_total, half), F32),
                   jax.ShapeDtypeStruct((n_total, LANES), jnp.int32),
                   jax.ShapeDtypeStruct((n_total, LANES), F32)],
        input_output_aliases={3 + k: k for k in range(len(prev))},
        compiler_params=_cparams(("parallel",)),
    )(x, g, wr, *prev)


MOE_TF = 1792
MOE_TMC = 256


def _route(top_i, tm, tmc):
    n = top_i.shape[0]
    slots = 2 * n
    n_tiles = -(-(slots + N_EXPERTS * (tm - 1)) // tm)
    win = tmc + 8
    e_flat = top_i.reshape(-1)
    onehot = (e_flat[:, None] == jnp.arange(N_EXPERTS, dtype=jnp.int32)[None, :]).astype(jnp.int32)
    csum = jnp.cumsum(onehot, axis=0)
    rank = jnp.sum(onehot * csum, axis=1) - 1
    counts = csum[-1]
    padded = ((counts + tm - 1) // tm) * tm
    ends = jnp.cumsum(padded)
    starts = ends - padded
    dest = (jnp.sum(onehot * starts[None, :], axis=1) + rank).astype(jnp.int32)
    tile_start = jnp.arange(n_tiles, dtype=jnp.int32) * tm
    tile_expert = jnp.minimum(jnp.sum((tile_start[:, None] >= ends[None, :]).astype(jnp.int32), axis=1),
                              N_EXPERTS - 1).astype(jnp.int32)
    tile_valid = (tile_start < ends[-1]).astype(jnp.int32)
    src = jnp.zeros((n_tiles * tm,), jnp.int32).at[dest].set(jnp.arange(slots, dtype=jnp.int32) // 2)
    before = jnp.concatenate([jnp.zeros((1, N_EXPERTS), jnp.int32), csum[2 * tmc - 1:-1:2 * tmc]], axis=0)
    wstart = jnp.clip(((starts[None, :] + before) // 8) * 8, 0, n_tiles * tm - win).astype(jnp.int32)
    ws_slot = jnp.sum(onehot * jnp.repeat(wstart, 2 * tmc, axis=0), axis=1)
    local = (e_flat * win + dest - ws_slot).astype(jnp.int32)
    return local, wstart.reshape(-1), src, tile_expert, tile_valid


def _moe_gather_kernel(src_ref, x_ref, o_ref, *, tg):
    base = pl.program_id(1) * tg

    def body(r, carry):
        o_ref[pl.ds(r, 1), :] = x_ref[pl.ds(src_ref[base + r], 1), :]
        return carry

    lax.fori_loop(0, tg, body, 0, unroll=8)


def _moe_gather(src, xn2, tg):
    rows = src.shape[0]
    _, n, half = xn2.shape
    grid_spec = pltpu.PrefetchScalarGridSpec(
        num_scalar_prefetch=1,
        grid=(2, rows // tg),
        in_specs=[pl.BlockSpec((None, n, half), lambda h, i, s: (h, 0, 0),
                               pipeline_mode=pl.Buffered(1))],
        out_specs=pl.BlockSpec((tg, half), lambda h, i, s: (i, h)),
    )
    return pl.pallas_call(
        functools.partial(_moe_gather_kernel, tg=tg),
        grid_spec=grid_spec,
        out_shape=jax.ShapeDtypeStruct((rows, 2 * half), F32),
        compiler_params=_cparams(("arbitrary", "arbitrary")),
    )(src, xn2)


def _moe_up_kernel(te_ref, tv_ref, xs_ref, wg_ref, wu_ref, h_ref):
    @pl.when(tv_ref[pl.program_id(1)] != 0)
    def _():
        xs = xs_ref[...].astype(BF16)
        gate = _dot(xs, wg_ref[...])
        up = _dot(xs, wu_ref[...])
        h_ref[...] = (gate * _sigmoid(gate) * up).astype(BF16)


def _moe_up(te, tv, xs, w_gu, layer, tm):
    rows = xs.shape[0]
    nf = D_FF_EXPERT // MOE_TF
    wspec = lambda off: pl.BlockSpec((None, None, D_MODEL, MOE_TF),
                                     lambda f, t, te, tv: (layer, te[t], 0, off + f))
    grid_spec = pltpu.PrefetchScalarGridSpec(
        num_scalar_prefetch=2,
        grid=(nf, rows // tm),
        in_specs=[pl.BlockSpec((tm, D_MODEL), lambda f, t, te, tv: (t, 0)), wspec(0), wspec(nf)],
        out_specs=pl.BlockSpec((tm, MOE_TF), lambda f, t, te, tv: (t, f)),
    )
    return pl.pallas_call(
        _moe_up_kernel,
        grid_spec=grid_spec,
        out_shape=jax.ShapeDtypeStruct((rows, D_FF_EXPERT), BF16),
        compiler_params=_cparams(("arbitrary", "arbitrary")),
    )(te, tv, xs, w_gu, w_gu)


def _moe_down_kernel(te_ref, tv_ref, h_ref, wd_ref, y_ref):
    @pl.when(tv_ref[pl.program_id(0)] != 0)
    def _():
        y_ref[...] = _dot(h_ref[...], wd_ref[...])


def _moe_down(te, tv, h, w_down, layer, tm):
    rows = h.shape[0]
    grid_spec = pltpu.PrefetchScalarGridSpec(
        num_scalar_prefetch=2,
        grid=(rows // tm,),
        in_specs=[pl.BlockSpec((tm, D_FF_EXPERT), lambda t, te, tv: (t, 0)),
                  pl.BlockSpec((None, None, D_FF_EXPERT, D_MODEL),
                               lambda t, te, tv: (layer, te[t], 0, 0))],
        out_specs=pl.BlockSpec((tm, D_MODEL), lambda t, te, tv: (t, 0)),
    )
    return pl.pallas_call(
        _moe_down_kernel,
        grid_spec=grid_spec,
        out_shape=jax.ShapeDtypeStruct((rows, D_MODEL), F32),
        compiler_params=_cparams(("arbitrary",)),
    )(te, tv, h, w_down)


def _moe_combine_kernel(ws_ref, local_ref, *refs, tmc, win, tile0):
    win_refs = refs[:N_EXPERTS]
    g1_ref, g2_ref, x_ref, fn_ref, o_ref, buf = refs[N_EXPERTS:]
    for e in range(N_EXPERTS):
        buf[e * win:(e + 1) * win, :] = win_refs[e][...]
    base = 2 * (tile0 + pl.program_id(0)) * tmc

    def body(r, carry):
        row = pl.ds(r, 1)
        y1 = buf[pl.ds(local_ref[base + 2 * r], 1), :]
        y2 = buf[pl.ds(local_ref[base + 2 * r + 1], 1), :]
        g1 = g1_ref[row, :]
        g2 = g2_ref[row, :]
        parts = []
        for c in range(D_MODEL // LANES):
            cs = slice(c * LANES, (c + 1) * LANES)
            parts.append(g1 * y1[:, cs] + g2 * y2[:, cs])
        o_ref[row, :] = x_ref[row, :] + jnp.concatenate(parts, axis=1)
        return carry

    lax.fori_loop(0, tmc, body, 0, unroll=16)
    o_ref[...] = _rms(o_ref[...], fn_ref[...])


def _moe_combine(local, wstart, ys, g1b, g2b, x, fn, tmc, row0):
    n = x.shape[0]
    win = tmc + 8
    tile0 = row0 // tmc

    def win_spec(e):
        return pl.BlockSpec(
            (pl.Element(win), pl.Element(D_MODEL)),
            lambda i, ws, lo: (pl.multiple_of(ws[(tile0 + i) * N_EXPERTS + e], 8), 0))

    grid_spec = pltpu.PrefetchScalarGridSpec(
        num_scalar_prefetch=2,
        grid=(n // tmc,),
        in_specs=[win_spec(e) for e in range(N_EXPERTS)]
        + [pl.BlockSpec((tmc, LANES), lambda i, ws, lo: (tile0 + i, 0)),
           pl.BlockSpec((tmc, LANES), lambda i, ws, lo: (tile0 + i, 0)),
           pl.BlockSpec((tmc, D_MODEL), lambda i, ws, lo: (i, 0)),
           pl.BlockSpec((1, D_MODEL), lambda i, ws, lo: (0, 0))],
        out_specs=pl.BlockSpec((tmc, D_MODEL), lambda i, ws, lo: (i, 0)),
        scratch_shapes=[pltpu.VMEM((N_EXPERTS * win, D_MODEL), F32)],
    )
    return pl.pallas_call(
        functools.partial(_moe_combine_kernel, tmc=tmc, win=win, tile0=tile0),
        grid_spec=grid_spec,
        out_shape=jax.ShapeDtypeStruct((n, D_MODEL), F32),
        compiler_params=_cparams(("arbitrary",)),
    )(wstart, local, *([ys] * N_EXPERTS), g1b, g2b, x, fn)


def _scan_vec_layout(xs, batch, seq):
    nb = batch * RWKV_HEADS // SCAN_PAIRS
    x = xs.reshape(5, batch, seq, RWKV_HEADS, 2, SCAN_KH).transpose(0, 2, 5, 4, 1, 3)
    x = x.reshape(5, seq, SCAN_KH, 2, nb, SCAN_PAIRS).transpose(0, 4, 1, 2, 3, 5)
    return x.reshape(5, nb, seq, SCAN_KH, LANES)


def _scan_val_layout(v, batch, seq):
    nb = batch * RWKV_HEADS // SCAN_PAIRS
    v4 = v.reshape(batch, seq, RWKV_HEADS, RWKV_HEAD).transpose(1, 3, 0, 2)
    return v4.reshape(seq, RWKV_HEAD, nb, SCAN_PAIRS).transpose(2, 0, 1, 3)


def _scan_val_unlayout(y, batch, seq):
    v4 = y.transpose(1, 2, 0, 3).reshape(seq, RWKV_HEAD, batch, RWKV_HEADS)
    return v4.transpose(2, 0, 3, 1).reshape(batch * seq, RWKV_DIM)


def _scan_state_layout(s, batch):
    nb = batch * RWKV_HEADS // SCAN_PAIRS
    s6 = s.reshape(batch, RWKV_HEADS, RWKV_HEAD, 2, SCAN_KH).transpose(4, 2, 3, 0, 1)
    s6 = s6.reshape(SCAN_KH, RWKV_HEAD, 2, nb, SCAN_PAIRS).transpose(3, 0, 1, 2, 4)
    return s6.reshape(nb, SCAN_KH, RWKV_HEAD, LANES)


def _scan_state_unlayout(arr, batch):
    nb = arr.shape[0]
    s = arr.reshape(nb, SCAN_KH, RWKV_HEAD, 2, SCAN_PAIRS).transpose(0, 4, 2, 3, 1)
    return s.reshape(batch, RWKV_HEADS, RWKV_HEAD, RWKV_HEAD)


def _swap_halves(w):
    half = w.shape[-1] // 2
    return jnp.concatenate([w[..., half:], w[..., :half]], axis=-1)


def _prep_even(i, norm_mix, norm_ffn, w_in, q_norm, kv_norm, w_uq, w_uk, w_uv, mu, w0, w2, a0, a2,
               g2, k_k, k_a, r_k, ln_g, ln_b, w_out, ffn_gu, ffn_down):
    w = {}
    row = lambda v: v[i].reshape(1, -1)
    w_in = w_in[i]
    w["norm_mix"] = row(norm_mix)
    w["norm_ffn"] = row(norm_ffn)
    w["w_q"] = w_in[:, :MLA_Q_RANK].astype(BF16)
    w_kv = w_in[:, MLA_Q_RANK:MLA_Q_RANK + MLA_LAT]
    w["w_ckv"] = w_kv[:, :MLA_KV_RANK].astype(BF16)
    lane_pad = lambda m: jnp.pad(m, [(0, 0)] * (m.ndim - 1) + [(0, LANES - m.shape[-1])])
    w["w_pe_a"] = lane_pad(w_kv[:, MLA_KV_RANK:]).astype(BF16)
    w["w_pe_b"] = lane_pad(_swap_halves(w_kv[:, MLA_KV_RANK:])).astype(BF16)
    w["w_rw"] = w_in[:, MLA_Q_RANK + MLA_LAT:].astype(BF16)
    w["q_norm"] = row(q_norm)
    w["kv_norm"] = row(kv_norm)
    uq = w_uq[i].reshape(MLA_Q_RANK, MLA_HEADS, MLA_NOPE + MLA_ROPE)
    uq_pe = uq[:, :, MLA_NOPE:]
    w["w_qpe_a"] = lane_pad(uq_pe).reshape(MLA_Q_RANK, -1).astype(BF16)
    w["w_qpe_b"] = lane_pad(_swap_halves(uq_pe)).reshape(MLA_Q_RANK, -1).astype(BF16)
    w["w_qlat"] = _fold_qlat(uq[:, :, :MLA_NOPE].transpose(1, 0, 2), w_uk[i].transpose(1, 0, 2))
    uv = w_uv[i].transpose(1, 0, 2).reshape(MLA_HEADS // 2, 2, MLA_KV_RANK, MLA_V)
    zero = jnp.zeros_like(uv[:, 0])
    w["w_uv_bd"] = jnp.concatenate(
        [jnp.concatenate([uv[:, 0], zero], axis=-1), jnp.concatenate([zero, uv[:, 1]], axis=-1)],
        axis=1).astype(BF16)
    w["mu"] = row(mu)
    w["w0"] = row(w0)
    pad = lambda m, before: jnp.pad(m, ((before, LANES - before - m.shape[0]), (0, 0))).astype(BF16)
    w["w2p"] = pad(w2[i], 0)
    w["a2p"] = pad(a2[i], RWKV_W_LORA)
    w["a0"] = row(a0)
    w["g2"] = g2[i].astype(BF16)
    w["k_k"] = row(k_k)
    w["k_a"] = row(k_a)
    w["r_k"] = row(r_k)
    w["ln_g"] = row(ln_g)
    w["ln_b"] = row(ln_b)
    head = jnp.arange(RWKV_DIM) // RWKV_HEAD
    w["ones_bd"] = (head[:, None] == head[None, :]).astype(BF16)
    w["w_out_a"] = w_out[i][:MLA_HEADS * MLA_V].astype(BF16)
    w["w_out_b"] = w_out[i][MLA_HEADS * MLA_V:].astype(BF16)
    w["ffn_gu"] = ffn_gu[i].astype(BF16)
    w["ffn_down"] = ffn_down[i].astype(BF16)
    return w


def _prep_odd(i, norm_mix, norm_ffn, w_in, a2, ab, gla_norm, w_out, router, moe_gu, moe_down):
    w = {}
    row = lambda v: v[i].reshape(1, -1)
    w_in = w_in[i]
    w["norm_mix"] = row(norm_mix)
    w["norm_ffn"] = row(norm_ffn)
    w["w_q"] = w_in[:, :GLA_KDIM].astype(BF16)
    w["w_k"] = w_in[:, GLA_KDIM:2 * GLA_KDIM].astype(BF16)
    w["w_v"] = w_in[:, 2 * GLA_KDIM:2 * GLA_KDIM + GLA_VDIM].astype(BF16)
    w["w_g"] = w_in[:, 2 * GLA_KDIM + GLA_VDIM:2 * GLA_KDIM + 2 * GLA_VDIM].astype(BF16)
    w["w_xa"] = jnp.pad(w_in[:, 2 * GLA_KDIM + 2 * GLA_VDIM:],
                        ((0, 0), (0, LANES - GLA_GATE_RANK))).astype(BF16)
    w["a2p"] = jnp.pad(a2[i], ((0, LANES - GLA_GATE_RANK), (0, 0))).astype(BF16)
    w["ab"] = row(ab)
    w["gla_norm"] = row(gla_norm)
    w["w_out"] = w_out[i].astype(BF16)
    w["router"] = jnp.pad(router[i], ((0, 0), (0, LANES - N_EXPERTS)))
    w["layer"] = i
    w["moe_gu"] = moe_gu
    w["moe_down"] = moe_down
    return w


def _rope_tables(pos, reps):
    inv = ROPE_THETA ** (-jnp.arange(0, MLA_ROPE, 2, dtype=F32) / MLA_ROPE)
    ang = pos.astype(F32)[:, None] * inv[None, :]
    cos, sin = jnp.cos(ang), jnp.sin(ang)
    pad = ((0, 0), (0, LANES - MLA_ROPE))
    cs = jnp.tile(jnp.pad(jnp.concatenate([cos, cos], axis=-1), pad), (reps, 1))
    sn = jnp.tile(jnp.pad(jnp.concatenate([-sin, sin], axis=-1), pad), (reps, 1))
    return {"cs": cs, "sn": sn, "cs8": jnp.tile(cs, (1, MLA_HEADS)), "sn8": jnp.tile(sn, (1, MLA_HEADS))}


def _even_layer(x, batch, seq, tabs, state, shift0, past, w, tm, tc, after=()):
    n = batch * seq
    lat, lat_b, q_lat, q_pe, rw = _even_in(x, w, tabs, tm)
    if past is None:
        o_lat = _mla_prompt(q_lat, q_pe, lat_b, batch, seq)
    else:
        cache, layer, page_table = past
        rows = seq * MLA_HEADS
        q_full = jnp.concatenate([q_lat.reshape(batch, rows, MLA_KV_RANK),
                                  q_pe.reshape(batch, rows, LANES)[:, :, :MLA_ROPE]], axis=-1)
        new_pad_t = jnp.pad(lat_b.reshape(batch, seq, MLA_LATB)[:, :, :MLA_LAT],
                            ((0, 0), (0, PAGE_SIZE - seq), (0, 0))).transpose(0, 2, 1)
        o_lat = _mla_decode(page_table, q_full, new_pad_t, cache.transpose(0, 1, 3, 2), layer)
        o_lat = o_lat.reshape(n, MLA_HEADS * MLA_KV_RANK)

    rw3 = rw.reshape(batch, seq, RWKV_PROJ)
    xs5, v, g, rkv = _rwkv_prep(rw, shift0, w, tm, seq)
    y_l, s_l = _rwkv_scan(_scan_vec_layout(xs5, batch, seq), _scan_val_layout(v, batch, seq),
                          _scan_state_layout(state, batch), (o_lat,) + tuple(after), tc)
    y = _scan_val_unlayout(y_l, batch, seq)
    new_state = _scan_state_unlayout(s_l, batch)

    x = _even_out(y, rkv, g, o_lat, x, w, tm)
    x = _ffn(x, w["norm_ffn"], w["ffn_gu"], w["ffn_down"], tm)
    return x, lat.reshape(batch, seq, MLA_LAT), new_state, rw3[:, -1], o_lat


def _odd_mixer_layer(x, batch, seq, state, w, tm):
    q, k, v, gate, la = _odd_in(x, w, tm)
    seq_p = -(-seq // GLA_CHUNK) * GLA_CHUNK
    if seq_p != seq:
        padr = lambda t: jnp.pad(t.reshape(batch, seq, -1), ((0, 0), (0, seq_p - seq), (0, 0))
                                 ).reshape(batch * seq_p, -1)
        qp, kp, vp, lap = padr(q), padr(k), padr(v), padr(la)
    else:
        qp, kp, vp, lap = q, k, v, la
    o, st = _gla(qp, kp, vp, lap, state.transpose(0, 1, 3, 2), batch, seq_p)
    if seq_p != seq:
        o = o.reshape(batch, seq_p, GLA_VDIM)[:, :seq].reshape(batch * seq, GLA_VDIM)
    return _odd_out(o, gate, x, w, tm), st.transpose(0, 1, 3, 2)


def _moe_all_groups(xs_groups, tms, w, final_norm, tm_moe):
    sizes = [x.shape[0] for x in xs_groups]
    n_total = sum(sizes)
    bufs, row0 = None, 0
    for x, tm in zip(xs_groups, tms):
        bufs = _router(x, w["norm_ffn"], w["router"], tm, n_total, row0, bufs)
        row0 += x.shape[0]
    xn2, idx, gates = bufs
    local, wstart, src, tile_expert, tile_valid = _route(idx[:, :2], tm_moe, MOE_TMC)
    rows = _moe_gather(src, xn2, tm_moe)
    h = _moe_up(tile_expert, tile_valid, rows, w["moe_gu"], w["layer"], tm_moe)
    ys = _moe_down(tile_expert, tile_valid, h, w["moe_down"], w["layer"], tm_moe)
    g1b = jnp.broadcast_to(gates[:, 0:1], (n_total, LANES))
    g2b = jnp.broadcast_to(gates[:, 1:2], (n_total, LANES))
    outs, row0 = [], 0
    for x in xs_groups:
        outs.append(_moe_combine(local, wstart, ys, g1b, g2b, x, final_norm, MOE_TMC, row0))
        row0 += x.shape[0]
    return outs


def kernel(x_prompt, x_sample, cache_mla, state_rwkv, state_rwkv_shift, state_gla, page_table, norm_mix_even, norm_ffn_even, w_in_even, mla_q_norm, mla_kv_norm, mla_w_uq, mla_w_uk, mla_w_uv, rwkv_mu, rwkv_w0, rwkv_w2, rwkv_a0, rwkv_a2, rwkv_g2, rwkv_k_k, rwkv_k_a, rwkv_r_k, rwkv_ln_g, rwkv_ln_b, w_out_even, ffn_w_gu_even, ffn_w_down_even, norm_mix_odd, norm_ffn_odd, w_in_odd, gla_a2, gla_ab, gla_norm, w_out_odd, moe_router, moe_w_gu, moe_w_down, final_norm):
    bp, tp, _ = x_prompt.shape
    bs, ts, _ = x_sample.shape
    past_len = page_table.shape[1] * PAGE_SIZE
    tm_p, tm_s = 512, bs * ts
    we = _prep_even(0, norm_mix_even, norm_ffn_even, w_in_even, mla_q_norm, mla_kv_norm, mla_w_uq,
                    mla_w_uk, mla_w_uv, rwkv_mu, rwkv_w0, rwkv_w2, rwkv_a0, rwkv_a2, rwkv_g2,
                    rwkv_k_k, rwkv_k_a, rwkv_r_k, rwkv_ln_g, rwkv_ln_b, w_out_even, ffn_w_gu_even,
                    ffn_w_down_even)
    wo = _prep_odd(0, norm_mix_odd, norm_ffn_odd, w_in_odd, gla_a2, gla_ab, gla_norm, w_out_odd,
                   moe_router, moe_w_gu, moe_w_down)
    fn = final_norm.reshape(1, -1)
    tabs_p = _rope_tables(jnp.arange(tp), 1)
    tabs_s = _rope_tables(past_len + jnp.arange(ts), bs)

    hp = x_prompt.reshape(bp * tp, D_MODEL)
    hs = x_sample.reshape(bs * ts, D_MODEL)
    zeros_state = jnp.zeros((bp, RWKV_HEADS, RWKV_HEAD, RWKV_HEAD), F32)
    zeros_shift = jnp.zeros((bp, RWKV_PROJ), F32)
    hp, lat_p, rs_p, sh_p, _ = _even_layer(hp, bp, tp, tabs_p, zeros_state, zeros_shift, None, we,
                                           tm_p, 64)
    hs, lat_s, rs_s, sh_s, _ = _even_layer(hs, bs, ts, tabs_s, state_rwkv[0], state_rwkv_shift[0],
                                           (cache_mla, 0, page_table), we, tm_s, ts)
    zeros_gla = jnp.zeros((bp, GLA_HEADS, GLA_DK, GLA_DV), F32)
    hp, gs_p = _odd_mixer_layer(hp, bp, tp, zeros_gla, wo, tm_p)
    hs, gs_s = _odd_mixer_layer(hs, bs, ts, state_gla[0], wo, tm_s)
    yp, ys = _moe_all_groups([hp, hs], [tm_p, tm_s], wo, fn, 512)
    return (yp.reshape(bp, tp, D_MODEL), ys.reshape(bs, ts, D_MODEL), lat_p[None], lat_s[None],
            rs_p[None], rs_s[None], sh_p[None], sh_s[None], gs_p[None], gs_s[None])
```

```python
import functools

import jax
import jax.numpy as jnp
from jax import lax
from jax.experimental import pallas as pl
from jax.experimental.pallas import tpu as pltpu

F32 = jnp.float32
BF16 = jnp.bfloat16

D_MODEL = 1024
PAGE_SIZE = 128
NORM_EPS = 1e-6

MLA_HEADS = 8
MLA_NOPE = 64
MLA_ROPE = 32
MLA_V = 64
MLA_Q_RANK = 384
MLA_KV_RANK = 256
MLA_LAT = MLA_KV_RANK + MLA_ROPE
MLA_LATB = MLA_KV_RANK + 128
MLA_SCALE = (MLA_NOPE + MLA_ROPE) ** -0.5
ROPE_THETA = 10000.0

RWKV_HEADS = 8
RWKV_HEAD = 64
RWKV_DIM = RWKV_HEADS * RWKV_HEAD
RWKV_W_LORA = 64
RWKV_A_LORA = 64
RWKV_G_LORA = 128
RWKV_PROJ = 3 * RWKV_DIM + RWKV_W_LORA + RWKV_A_LORA + RWKV_G_LORA
RWKV_LN_EPS = 64e-5

GLA_HEADS = 4
GLA_DK = 128
GLA_DV = 256
GLA_KDIM = GLA_HEADS * GLA_DK
GLA_VDIM = GLA_HEADS * GLA_DV
GLA_GATE_RANK = 16
GLA_GATE_NORM = 16.0
GLA_CHUNK = 128

D_FF = 2816
N_EXPERTS = 8
D_FF_EXPERT = 3584

LANES = 128
VMEM_LIMIT = 56 * 1024 * 1024
NEG_BIG = -1e30
LOG2_E = 1.4426950408889634
Q_PRESCALE = MLA_SCALE * LOG2_E


def _cparams(sem):
    return pltpu.CompilerParams(dimension_semantics=sem, vmem_limit_bytes=VMEM_LIMIT)


def _const_spec(shape):
    nd = len(shape)
    return pl.BlockSpec(shape, lambda *_: (0,) * nd)


def _row_spec(tm, width):
    return pl.BlockSpec((tm, width), lambda i: (i, 0))


def _dot(a, b):
    return jnp.dot(a.astype(BF16), b.astype(BF16), preferred_element_type=F32)


def _dot_nt(a, b):
    return lax.dot_general(a.astype(BF16), b.astype(BF16), (((1,), (1,)), ((), ())),
                           preferred_element_type=F32)


def _split2(x):
    hi = x.astype(BF16)
    lo = (x - hi.astype(F32)).astype(BF16)
    return hi, lo


def _split3(x):
    hi = x.astype(BF16)
    r1 = x - hi.astype(F32)
    mid = r1.astype(BF16)
    lo = (r1 - mid.astype(F32)).astype(BF16)
    return hi, mid, lo


def _dot_exact_rhs(x, e):
    hi, mid, lo = _split3(x)
    return (jnp.dot(hi, e, preferred_element_type=F32) + jnp.dot(mid, e, preferred_element_type=F32)
            + jnp.dot(lo, e, preferred_element_type=F32))


def _dot_exact_lhs(e, x):
    hi, mid, lo = _split3(x)
    return (jnp.dot(e, hi, preferred_element_type=F32) + jnp.dot(e, mid, preferred_element_type=F32)
            + jnp.dot(e, lo, preferred_element_type=F32))


def _dot_f32ish(a, b):
    ah, al = _split2(a)
    bh, bl = _split2(b)
    return (jnp.dot(ah, bh, preferred_element_type=F32) + jnp.dot(ah, bl, preferred_element_type=F32)
            + jnp.dot(al, bh, preferred_element_type=F32))


def _lane_tile(x, width):
    return x if width == LANES else jnp.concatenate([x] * (width // LANES), axis=1)


def _rms(x, g, eps=NORM_EPS):
    return x * lax.rsqrt(jnp.mean(x * x, axis=-1, keepdims=True) + eps) * g


def _sigmoid(x):
    return 1.0 / (1.0 + jnp.exp(-x))


def _softplus(x):
    return jnp.maximum(x, 0.0) + jnp.log(1.0 + jnp.exp(-jnp.abs(x)))


def _fold_qlat_kernel(uq_ref, uk_ref, o_ref):
    a = uq_ref[...]
    b = uk_ref[...]
    ah, al = _split2(a)
    bh, bl = _split2(b)
    dn = (((1,), (1,)), ((), ()))
    o = (lax.dot_general(ah, bh, dn, preferred_element_type=F32)
         + lax.dot_general(ah, bl, dn, preferred_element_type=F32)
         + lax.dot_general(al, bh, dn, preferred_element_type=F32))
    o_ref[...] = o.astype(BF16)


def _fold_qlat(uq_nope, uk):
    return pl.pallas_call(
        _fold_qlat_kernel,
        grid=(MLA_HEADS,),
        in_specs=[pl.BlockSpec((None, MLA_Q_RANK, MLA_NOPE), lambda h: (h, 0, 0)),
                  pl.BlockSpec((None, MLA_KV_RANK, MLA_NOPE), lambda h: (h, 0, 0))],
        out_specs=pl.BlockSpec((MLA_Q_RANK, MLA_KV_RANK), lambda h: (0, h)),
        out_shape=jax.ShapeDtypeStruct((MLA_Q_RANK, MLA_HEADS * MLA_KV_RANK), BF16),
        compiler_params=_cparams(("arbitrary",)),
    )(uq_nope, uk)


def _even_in_kernel(x_ref, g_ref, wq_ref, wckv_ref, wpa_ref, wpb_ref, wrw_ref, qn_ref, kvn_ref,
                    cs_ref, sn_ref, wql_ref, wqa_ref, wqb_ref, cs8_ref, sn8_ref,
                    lat_ref, latb_ref, ql_ref, qpe_ref, rw_ref):
    xn = _rms(x_ref[...], g_ref[...]).astype(BF16)
    cq = _rms(_dot(xn, wq_ref[...]), qn_ref[...]).astype(BF16)
    ql_ref[...] = (_dot(cq, wql_ref[...]) * Q_PRESCALE).astype(BF16)
    qpe = _dot(cq, wqa_ref[...]) * cs8_ref[...] + _dot(cq, wqb_ref[...]) * sn8_ref[...]
    qpe_ref[...] = (qpe * Q_PRESCALE).astype(BF16)
    ckv = _rms(_dot(xn, wckv_ref[...]), kvn_ref[...])
    kpe = _dot(xn, wpa_ref[...]) * cs_ref[...] + _dot(xn, wpb_ref[...]) * sn_ref[...]
    lat_ref[:, :MLA_KV_RANK] = ckv
    lat_ref[:, MLA_KV_RANK:] = kpe[:, :MLA_ROPE]
    latb_ref[:, :MLA_KV_RANK] = ckv.astype(BF16)
    latb_ref[:, MLA_KV_RANK:] = kpe.astype(BF16)
    rw_ref[...] = _dot(xn, wrw_ref[...])


def _even_in(x, w, tabs, tm):
    n = x.shape[0]
    nt = tabs["cs"].shape[0] // tm
    tab = lambda width: pl.BlockSpec((tm, width), lambda i: (i % nt, 0))
    hq = MLA_HEADS * MLA_KV_RANK
    hr = MLA_HEADS * LANES
    return pl.pallas_call(
        _even_in_kernel,
        grid=(n // tm,),
        in_specs=[_row_spec(tm, D_MODEL), _const_spec((1, D_MODEL)),
                  _const_spec((D_MODEL, MLA_Q_RANK)), _const_spec((D_MODEL, MLA_KV_RANK)),
                  _const_spec((D_MODEL, LANES)), _const_spec((D_MODEL, LANES)),
                  _const_spec((D_MODEL, RWKV_PROJ)), _const_spec((1, MLA_Q_RANK)),
                  _const_spec((1, MLA_KV_RANK)), tab(LANES), tab(LANES),
                  _const_spec((MLA_Q_RANK, hq)), _const_spec((MLA_Q_RANK, hr)),
                  _const_spec((MLA_Q_RANK, hr)), tab(hr), tab(hr)],
        out_specs=[_row_spec(tm, MLA_LAT), _row_spec(tm, MLA_LATB), _row_spec(tm, hq),
                   _row_spec(tm, hr), _row_spec(tm, RWKV_PROJ)],
        out_shape=[jax.ShapeDtypeStruct((n, MLA_LAT), F32), jax.ShapeDtypeStruct((n, MLA_LATB), BF16),
                   jax.ShapeDtypeStruct((n, hq), BF16), jax.ShapeDtypeStruct((n, hr), BF16),
                   jax.ShapeDtypeStruct((n, RWKV_PROJ), F32)],
        compiler_params=_cparams(("parallel",)),
    )(x, w["norm_mix"], w["w_q"], w["w_ckv"], w["w_pe_a"], w["w_pe_b"], w["w_rw"], w["q_norm"],
      w["kv_norm"], tabs["cs"], tabs["sn"], w["w_qlat"], w["w_qpe_a"], w["w_qpe_b"],
      tabs["cs8"], tabs["sn8"])


ATT_TQ = 512
ATT_TK = 512


def _mla_prompt_kernel(qi_ref, kj_ref, ql_ref, qpe_ref, lat_ref, o_ref,
                       m_sc, l_sc, a_sc, acc_sc, s_sc, p_sc):
    step = pl.program_id(1)
    i = qi_ref[step]
    j = kj_ref[step]
    heads = range(MLA_HEADS)

    @pl.when(j == 0)
    def _():
        m_sc[...] = jnp.full(m_sc.shape, NEG_BIG, F32)
        l_sc[...] = jnp.zeros(l_sc.shape, F32)
        acc_sc[...] = jnp.zeros(acc_sc.shape, F32)

    def tile(masked):
        ckv = lat_ref[:, :MLA_KV_RANK]
        kpe = lat_ref[:, MLA_KV_RANK:]
        for h in heads:
            s_sc[h] = (_dot_nt(ql_ref[:, h * MLA_KV_RANK:(h + 1) * MLA_KV_RANK], ckv)
                       + _dot_nt(qpe_ref[:, h * LANES:(h + 1) * LANES], kpe))
        for h in heads:
            s = s_sc[h]
            if masked:
                tok = lax.broadcasted_iota(jnp.int32, s.shape, 0) + offset
                key = lax.broadcasted_iota(jnp.int32, s.shape, 1)
                s = jnp.where(key <= tok, s, NEG_BIG)
            m_prev = m_sc[h]
            m_new = jnp.maximum(m_prev, jnp.max(s, axis=-1, keepdims=True))
            alpha = jnp.exp2(m_prev - m_new)
            p = jnp.exp2(s - _lane_tile(m_new, ATT_TK))
            l_sc[h] = alpha * l_sc[h] + jnp.sum(p, axis=-1, keepdims=True)
            m_sc[h] = m_new
            a_sc[h] = alpha
            p_sc[h] = p.astype(BF16)
        for h in heads:
            acc_sc[h] = _lane_tile(a_sc[h], MLA_KV_RANK) * acc_sc[h] + _dot(p_sc[h], ckv)

    offset = i * ATT_TQ - j * ATT_TK
    on_diagonal = offset < ATT_TK - 1

    @pl.when(jnp.logical_not(on_diagonal))
    def _():
        tile(False)

    @pl.when(on_diagonal)
    def _():
        tile(True)

    @pl.when(j == (i * ATT_TQ + ATT_TQ - 1) // ATT_TK)
    def _():
        for h in heads:
            o_ref[:, h * MLA_KV_RANK:(h + 1) * MLA_KV_RANK] = (
                acc_sc[h] / _lane_tile(l_sc[h], MLA_KV_RANK)).astype(BF16)


def _mla_prompt(q_lat, q_pe, lat_b, batch, seq):
    nq = seq // ATT_TQ
    nk = seq // ATT_TK
    pairs = [(i, j) for i in range(nq)
             for j in range((i * ATT_TQ + ATT_TQ - 1) // ATT_TK + 1)]
    qi = jnp.array([p[0] for p in pairs], jnp.int32)
    kj = jnp.array([p[1] for p in pairs], jnp.int32)
    hq = MLA_HEADS * MLA_KV_RANK
    grid_spec = pltpu.PrefetchScalarGridSpec(
        num_scalar_prefetch=2,
        grid=(batch, len(pairs)),
        in_specs=[pl.BlockSpec((ATT_TQ, hq), lambda b, s, qi, kj: (b * nq + qi[s], 0)),
                  pl.BlockSpec((ATT_TQ, MLA_HEADS * LANES), lambda b, s, qi, kj: (b * nq + qi[s], 0)),
                  pl.BlockSpec((ATT_TK, MLA_LATB), lambda b, s, qi, kj: (b * nk + kj[s], 0))],
        out_specs=pl.BlockSpec((ATT_TQ, hq), lambda b, s, qi, kj: (b * nq + qi[s], 0)),
        scratch_shapes=[pltpu.VMEM((MLA_HEADS, ATT_TQ, LANES), F32),
                        pltpu.VMEM((MLA_HEADS, ATT_TQ, LANES), F32),
                        pltpu.VMEM((MLA_HEADS, ATT_TQ, LANES), F32),
                        pltpu.VMEM((MLA_HEADS, ATT_TQ, MLA_KV_RANK), F32),
                        pltpu.VMEM((MLA_HEADS, ATT_TQ, ATT_TK), F32),
                        pltpu.VMEM((MLA_HEADS, ATT_TQ, ATT_TK), BF16)],
    )
    return pl.pallas_call(
        _mla_prompt_kernel,
        grid_spec=grid_spec,
        out_shape=jax.ShapeDtypeStruct(q_lat.shape, BF16),
        compiler_params=_cparams(("parallel", "arbitrary")),
    )(qi, kj, q_lat, q_pe, lat_b)


PAGES_PER_STEP = 64
DECODE_GROUPS = 16


def _mla_decode_kernel(pt_ref, q_ref, new_ref, *rest):
    page_refs = rest[:PAGES_PER_STEP]
    o_ref, m_sc, l_sc, acc_sc = rest[PAGES_PER_STEP:]
    j = pl.program_id(1)
    q = q_ref[0]

    @pl.when(j == 0)
    def _():
        m_sc[...] = jnp.full(m_sc.shape, NEG_BIG, F32)
        l_sc[...] = jnp.zeros(l_sc.shape, F32)
        acc_sc[...] = jnp.zeros(acc_sc.shape, F32)

    def update(state, s, values_t):
        m_prev, l_prev, acc = state
        m_new = jnp.maximum(m_prev, jnp.max(s, axis=-1, keepdims=True))
        alpha = jnp.exp2(m_prev - m_new)
        p = jnp.exp2(s - _lane_tile(m_new, s.shape[1]))
        l_new = alpha * l_prev + jnp.sum(p, axis=-1, keepdims=True)
        return m_new, l_new, _lane_tile(alpha, MLA_KV_RANK) * acc + _dot_nt(p, values_t)

    group = PAGES_PER_STEP // DECODE_GROUPS
    keys = [jnp.concatenate([pr[...].astype(BF16) for pr in page_refs[g * group:(g + 1) * group]],
                            axis=1) for g in range(DECODE_GROUPS)]
    scores = [_dot(q, kt) for kt in keys]
    state = (m_sc[...], l_sc[...], acc_sc[...])
    for s, kt in zip(scores, keys):
        state = update(state, s, kt[:MLA_KV_RANK, :])
    m_sc[...], l_sc[...], acc_sc[...] = state

    @pl.when(j == pl.num_programs(1) - 1)
    def _():
        new_t = new_ref[0]
        sn = _dot(q, new_t)
        tok = lax.broadcasted_iota(jnp.int32, sn.shape, 0) >> 3
        key = lax.broadcasted_iota(jnp.int32, sn.shape, 1)
        sn = jnp.where(key <= tok, sn, NEG_BIG)
        _, l_fin, acc_fin = update(state, sn, new_t[:MLA_KV_RANK, :])
        o_ref[0] = (acc_fin / _lane_tile(l_fin, MLA_KV_RANK)).astype(BF16)


def _mla_decode(page_table, q_full, new_pad_t, cache_t, layer):
    db, n_pages = page_table.shape
    rows = q_full.shape[1]
    steps = n_pages // PAGES_PER_STEP

    def page_spec(p):
        return pl.BlockSpec((None, None, MLA_LAT, PAGE_SIZE),
                            lambda b, j, pt: (layer, pt[b, j * PAGES_PER_STEP + p], 0, 0))

    grid_spec = pltpu.PrefetchScalarGridSpec(
        num_scalar_prefetch=1,
        grid=(db, steps),
        in_specs=[pl.BlockSpec((1, rows, MLA_LAT), lambda b, j, pt: (b, 0, 0)),
                  pl.BlockSpec((1, MLA_LAT, PAGE_SIZE), lambda b, j, pt: (b, 0, 0))]
        + [page_spec(p) for p in range(PAGES_PER_STEP)],
        out_specs=pl.BlockSpec((1, rows, MLA_KV_RANK), lambda b, j, pt: (b, 0, 0)),
        scratch_shapes=[pltpu.VMEM((rows, LANES), F32), pltpu.VMEM((rows, LANES), F32),
                        pltpu.VMEM((rows, MLA_KV_RANK), F32)],
    )
    return pl.pallas_call(
        _mla_decode_kernel,
        grid_spec=grid_spec,
        out_shape=jax.ShapeDtypeStruct((db, rows, MLA_KV_RANK), BF16),
        compiler_params=_cparams(("parallel", "arbitrary")),
    )(page_table, q_full, new_pad_t, *([cache_t] * PAGES_PER_STEP))


def _rwkv_prep_kernel(rw_ref, before_ref, sh_ref, mu_ref, w0_ref, w2_ref, a0_ref, a2_ref, g2_ref,
                      kk_ref, ka_ref, rk_ref, ones_ref, xs_ref, v_ref, g_ref, rkv_ref, *, tm, seq):
    rw = rw_ref[...]
    rolled = pltpu.roll(rw, 1, axis=0)
    row = lax.broadcasted_iota(jnp.int32, rw.shape, 0)
    if seq >= tm:
        at_start = pl.program_id(0) % (seq // tm) == 0
        first = jnp.where(at_start, sh_ref[...], before_ref[7:8, :])
        prev = jnp.where(row == 0, first, rolled)
    else:
        prev = jnp.where((row & (seq - 1)) == 0, sh_ref[...], rolled)
    xs = rw + (prev - rw) * mu_ref[...]
    d = RWKV_DIM
    r = xs[:, :d]
    k = xs[:, d:2 * d]
    v = xs[:, 2 * d:3 * d]
    xwa = xs[:, 3 * d:3 * d + LANES]
    xg = xs[:, 3 * d + LANES:]
    ones = ones_ref[...]
    w_log = -_softplus(-(w0_ref[...] + _dot(jnp.tanh(xwa), w2_ref[...]))) - 0.5
    a = _sigmoid(a0_ref[...] + _dot(xwa, a2_ref[...]))
    g_ref[...] = _dot(_sigmoid(xg), g2_ref[...])
    kk = k * kk_ref[...]
    ss = _dot_exact_rhs(kk * kk, ones)
    kk = kk / jnp.maximum(jnp.sqrt(ss), 1e-12)
    k2 = k * (1.0 + (a - 1.0) * ka_ref[...])
    xs_ref[0] = -kk
    xs_ref[1] = jnp.exp(-jnp.exp(w_log))
    xs_ref[2] = kk * a
    xs_ref[3] = k2
    xs_ref[4] = r
    v_ref[...] = v
    rkv_ref[...] = _dot_exact_rhs(r * k2 * rk_ref[...], ones) * v


def _rwkv_prep(rw, shift0, w, tm, seq):
    n = rw.shape[0]
    d = RWKV_DIM
    vec = _const_spec((1, d))
    if seq >= tm:
        tiles = seq // tm
        sh = shift0.reshape(-1, 1, RWKV_PROJ)
        sh_spec = pl.BlockSpec((None, 1, RWKV_PROJ), lambda i: (i // tiles, 0, 0))
    else:
        sh = jnp.repeat(shift0, seq, axis=0)
        sh_spec = _row_spec(tm, RWKV_PROJ)
    before_spec = pl.BlockSpec((8, RWKV_PROJ), lambda i: (jnp.maximum(i * (tm // 8) - 1, 0), 0))
    return pl.pallas_call(
        functools.partial(_rwkv_prep_kernel, tm=tm, seq=seq),
        grid=(n // tm,),
        in_specs=[_row_spec(tm, RWKV_PROJ), before_spec, sh_spec, _const_spec((1, RWKV_PROJ)),
                  vec, _const_spec((LANES, d)), vec, _const_spec((LANES, d)),
                  _const_spec((RWKV_G_LORA, d)), vec, vec, vec, _const_spec((d, d))],
        out_specs=[pl.BlockSpec((5, tm, d), lambda i: (0, i, 0))] + [_row_spec(tm, d)] * 3,
        out_shape=[jax.ShapeDtypeStruct((5, n, d), F32)] + [jax.ShapeDtypeStruct((n, d), F32)] * 3,
        compiler_params=_cparams(("parallel",)),
    )(rw, rw, sh, w["mu"], w["w0"], w["w2p"], w["a0"], w["a2p"], w["g2"], w["k_k"], w["k_a"],
      w["r_k"], w["ones_bd"])


SCAN_KH = RWKV_HEAD // 2
SCAN_PAIRS = LANES // 2
SCAN_VR = RWKV_HEAD // 2


def _rwkv_scan_kernel(x_ref, v_ref, s0_ref, *rest, tc):
    y_ref, s_ref, c_sc, d_sc = rest[-4:]
    @pl.when(pl.program_id(1) == 0)
    def _():
        s_ref[...] = s0_ref[...]

    half_a = slice(0, SCAN_VR)
    half_b = slice(SCAN_VR, RWKV_HEAD)

    def both_halves(p):
        return p + pltpu.roll(p, SCAN_PAIRS, axis=1)

    def key_dot(u, w):
        return both_halves(jnp.sum(u * w, axis=0, keepdims=True))

    def first_partial(rows):
        p = s_ref[0, 0, rows, :] * x_ref[0, 0, 0, 0:1, :]
        for k in range(1, SCAN_KH):
            p = p + s_ref[0, k, rows, :] * x_ref[0, 0, 0, k:k + 1, :]
        return p

    def half_step(t, rows, sa):
        v_half = v_ref[0, t, rows, :]
        v = jnp.concatenate([v_half, v_half], axis=1)
        q = None
        y = None
        for k in range(SCAN_KH):
            s_old = s_ref[0, k, rows, :]
            qk = s_old * c_sc[k:k + 1, :]
            sn = (s_old * x_ref[1, 0, t, k:k + 1, :] + sa * x_ref[2, 0, t, k:k + 1, :]
                  + v * x_ref[3, 0, t, k:k + 1, :])
            s_ref[0, k, rows, :] = sn
            yk = sn * x_ref[4, 0, t, k:k + 1, :]
            q = qk if q is None else q + qk
            y = yk if y is None else y + yk
        return q, sa * d_sc[0:1, :] + v * d_sc[1:2, :], y

    def store_y(t, y_a, y_b):
        y_ref[0, t, half_a, :] = both_halves(y_a)[:, :SCAN_PAIRS]
        y_ref[0, t, half_b, :] = both_halves(y_b)[:, :SCAN_PAIRS]

    def step(t, carry):
        sa_a, q_b, corr_b, y_a, y_b = carry
        store_y(jnp.maximum(t - 1, 0), y_a, y_b)
        a_next = x_ref[0, 0, jnp.minimum(t + 1, tc - 1)]
        c_sc[...] = x_ref[1, 0, t] * a_next
        d_sc[0:1, :] = key_dot(x_ref[2, 0, t], a_next)
        d_sc[1:2, :] = key_dot(x_ref[3, 0, t], a_next)
        sa_b = both_halves(q_b) + corr_b
        q_a, corr_a, y_a_new = half_step(t, half_a, sa_a)
        sa_a_next = both_halves(q_a) + corr_a
        q_b_next, corr_b_next, y_b_new = half_step(t, half_b, sa_b)
        return sa_a_next, q_b_next, corr_b_next, y_a_new, y_b_new

    zero = jnp.zeros((SCAN_VR, LANES), F32)
    init = (both_halves(first_partial(half_a)), first_partial(half_b), zero, zero, zero)
    final = lax.fori_loop(0, tc, step, init)
    store_y(tc - 1, final[3], final[4])


def _rwkv_scan(xs, v, s0, after, tc):
    _, nb, t, _, _ = xs.shape
    xspec = pl.BlockSpec((5, 1, tc, SCAN_KH, LANES), lambda n, c: (0, n, c, 0, 0))
    vspec = pl.BlockSpec((1, tc, RWKV_HEAD, SCAN_PAIRS), lambda n, c: (n, c, 0, 0))
    sspec = pl.BlockSpec((1, SCAN_KH, RWKV_HEAD, LANES), lambda n, c: (n, 0, 0, 0))
    return pl.pallas_call(
        functools.partial(_rwkv_scan_kernel, tc=tc),
        grid=(nb, t // tc),
        in_specs=[xspec, vspec, sspec] + [pl.BlockSpec(memory_space=pl.ANY)] * len(after),
        out_specs=[vspec, sspec],
        out_shape=[jax.ShapeDtypeStruct(v.shape, F32), jax.ShapeDtypeStruct(s0.shape, F32)],
        scratch_shapes=[pltpu.VMEM((SCAN_KH, LANES), F32), pltpu.VMEM((8, LANES), F32)],
        compiler_params=_cparams(("parallel", "arbitrary")),
    )(xs, v, s0, *after)


def _even_out_kernel(y_ref, rkv_ref, g_ref, lng_ref, lnb_ref, ones_ref, ol_ref, wuv_ref, woa_ref,
                     wob_ref, x_ref, o_ref):
    ones = ones_ref[...]
    y = y_ref[...]
    inv = 1.0 / RWKV_HEAD
    mean = _dot_exact_rhs(y, ones) * inv
    dlt = y - mean
    var = _dot_exact_rhs(dlt * dlt, ones) * inv
    yn = dlt * lax.rsqrt(var + RWKV_LN_EPS) * lng_ref[...] + lnb_ref[...] + rkv_ref[...]
    ob = (yn * g_ref[...]).astype(BF16)
    pair = 2 * MLA_KV_RANK
    oa = jnp.concatenate(
        [_dot(ol_ref[:, p * pair:(p + 1) * pair], wuv_ref[p]) for p in range(MLA_HEADS // 2)], axis=1)
    o_ref[...] = x_ref[...] + _dot(oa, woa_ref[...]) + _dot(ob, wob_ref[...])


def _even_out(y, rkv, g, o_lat, x, w, tm):
    n = x.shape[0]
    d = RWKV_DIM
    hq = MLA_HEADS * MLA_KV_RANK
    return pl.pallas_call(
        _even_out_kernel,
        grid=(n // tm,),
        in_specs=[_row_spec(tm, d), _row_spec(tm, d), _row_spec(tm, d), _const_spec((1, d)),
                  _const_spec((1, d)), _const_spec((d, d)), _row_spec(tm, hq),
                  _const_spec((MLA_HEADS // 2, 2 * MLA_KV_RANK, 2 * MLA_V)),
                  _const_spec((MLA_HEADS * MLA_V, D_MODEL)), _const_spec((d, D_MODEL)),
                  _row_spec(tm, D_MODEL)],
        out_specs=_row_spec(tm, D_MODEL),
        out_shape=jax.ShapeDtypeStruct((n, D_MODEL), F32),
        compiler_params=_cparams(("parallel",)),
    )(y, rkv, g, w["ln_g"], w["ln_b"], w["ones_bd"], o_lat, w["w_uv_bd"], w["w_out_a"],
      w["w_out_b"], x)


FFN_TF = 1408


def _ffn_kernel(x_ref, g_ref, wg_ref, wu_ref, wd_ref, o_ref, xn_sc, acc_sc):
    f = pl.program_id(1)

    @pl.when(f == 0)
    def _():
        xn_sc[...] = _rms(x_ref[...], g_ref[...]).astype(BF16)
        acc_sc[...] = jnp.zeros(acc_sc.shape, F32)

    xn = xn_sc[...]
    gate = _dot(xn, wg_ref[...])
    up = _dot(xn, wu_ref[...])
    acc_sc[...] += _dot(gate * _sigmoid(gate) * up, wd_ref[...])

    @pl.when(f == pl.num_programs(1) - 1)
    def _():
        o_ref[...] = x_ref[...] + acc_sc[...]


def _ffn(x, g, w_gu, w_down, tm):
    n = x.shape[0]
    nf = D_FF // FFN_TF
    return pl.pallas_call(
        _ffn_kernel,
        grid=(n // tm, nf),
        in_specs=[pl.BlockSpec((tm, D_MODEL), lambda i, f: (i, 0)),
                  pl.BlockSpec((1, D_MODEL), lambda i, f: (0, 0)),
                  pl.BlockSpec((D_MODEL, FFN_TF), lambda i, f: (0, f)),
                  pl.BlockSpec((D_MODEL, FFN_TF), lambda i, f: (0, nf + f)),
                  pl.BlockSpec((FFN_TF, D_MODEL), lambda i, f: (f, 0))],
        out_specs=pl.BlockSpec((tm, D_MODEL), lambda i, f: (i, 0)),
        out_shape=jax.ShapeDtypeStruct((n, D_MODEL), F32),
        scratch_shapes=[pltpu.VMEM((tm, D_MODEL), BF16), pltpu.VMEM((tm, D_MODEL), F32)],
        compiler_params=_cparams(("parallel", "arbitrary")),
    )(x, g, w_gu, w_gu, w_down)


def _odd_in_kernel(x_ref, g_ref, wq_ref, wk_ref, wv_ref, wg_ref, wxa_ref, a2_ref, ab_ref,
                   q_ref, k_ref, v_ref, gate_ref, la_ref):
    xn = _rms(x_ref[...], g_ref[...]).astype(BF16)
    q_ref[...] = _dot(xn, wq_ref[...]) * (GLA_DK ** -0.5)
    k_ref[...] = _dot(xn, wk_ref[...])
    v_ref[...] = _dot(xn, wv_ref[...])
    gate_ref[...] = _dot(xn, wg_ref[...])
    z = _dot(_dot(xn, wxa_ref[...]), a2_ref[...]) + ab_ref[...]
    la_ref[...] = -_softplus(-z) * (1.0 / GLA_GATE_NORM)


def _odd_in(x, w, tm):
    n = x.shape[0]
    return pl.pallas_call(
        _odd_in_kernel,
        grid=(n // tm,),
        in_specs=[_row_spec(tm, D_MODEL), _const_spec((1, D_MODEL)),
                  _const_spec((D_MODEL, GLA_KDIM)), _const_spec((D_MODEL, GLA_KDIM)),
                  _const_spec((D_MODEL, GLA_VDIM)), _const_spec((D_MODEL, GLA_VDIM)),
                  _const_spec((D_MODEL, LANES)), _const_spec((LANES, GLA_KDIM)),
                  _const_spec((1, GLA_KDIM))],
        out_specs=[_row_spec(tm, GLA_KDIM), _row_spec(tm, GLA_KDIM), _row_spec(tm, GLA_VDIM),
                   _row_spec(tm, GLA_VDIM), _row_spec(tm, GLA_KDIM)],
        out_shape=[jax.ShapeDtypeStruct((n, GLA_KDIM), F32), jax.ShapeDtypeStruct((n, GLA_KDIM), F32),
                   jax.ShapeDtypeStruct((n, GLA_VDIM), F32), jax.ShapeDtypeStruct((n, GLA_VDIM), F32),
                   jax.ShapeDtypeStruct((n, GLA_KDIM), F32)],
        compiler_params=_cparams(("parallel",)),
    )(x, w["norm_mix"], w["w_q"], w["w_k"], w["w_v"], w["w_g"], w["w_xa"], w["a2p"], w["ab"])


def _gla_kernel(q_ref, k_ref, v_ref, la_ref, s0_ref, o_ref, st_ref):
    c = GLA_CHUNK

    @pl.when(pl.program_id(1) == 0)
    def _():
        st_ref[...] = s0_ref[...]

    row = lax.broadcasted_iota(jnp.int32, (c, c), 0)
    col = lax.broadcasted_iota(jnp.int32, (c, c), 1)
    tri = row >= col
    tri_b = jnp.where(tri, 1.0, 0.0).astype(BF16)
    heads = range(GLA_HEADS)
    ks = [slice(h * GLA_DK, (h + 1) * GLA_DK) for h in heads]
    vs = [slice(h * GLA_DV, (h + 1) * GLA_DV) for h in heads]
    b = _dot_exact_lhs(tri_b, la_ref[...])
    k = k_ref[...]
    b_end = b[c - 1:c, :]
    qe = (q_ref[...] * jnp.exp(b)).astype(BF16)
    ke = (k * jnp.exp(-b)).astype(BF16)
    k_end = (k * jnp.exp(b_end - b)).astype(BF16)
    e_end = jnp.exp(b_end)
    a_mats = [jnp.where(tri, _dot_nt(qe[:, ks[h]], ke[:, ks[h]]), 0.0).astype(BF16) for h in heads]
    states = [st_ref[0, h] for h in heads]
    for h in heads:
        o_ref[:, vs[h]] = _dot_nt(qe[:, ks[h]], states[h]) + _dot(a_mats[h], v_ref[:, vs[h]])
    for h in heads:
        st_ref[0, h] = states[h] * e_end[:, ks[h]] + _dot(v_ref[:, vs[h]].T, k_end[:, ks[h]])


def _gla(q, k, v, la, s0t, batch, seq):
    nc = seq // GLA_CHUNK
    rspec = lambda width: pl.BlockSpec((GLA_CHUNK, width), lambda b, c: (b * nc + c, 0))
    sspec = pl.BlockSpec((1, GLA_HEADS, GLA_DV, GLA_DK), lambda b, c: (b, 0, 0, 0))
    return pl.pallas_call(
        _gla_kernel,
        grid=(batch, nc),
        in_specs=[rspec(GLA_KDIM), rspec(GLA_KDIM), rspec(GLA_VDIM), rspec(GLA_KDIM), sspec],
        out_specs=[rspec(GLA_VDIM), sspec],
        out_shape=[jax.ShapeDtypeStruct(v.shape, F32), jax.ShapeDtypeStruct(s0t.shape, F32)],
        compiler_params=_cparams(("parallel", "arbitrary")),
    )(q, k, v, la, s0t)


def _odd_out_kernel(o_ref, gate_ref, gn_ref, wo_ref, x_ref, y_ref):
    parts = []
    for h in range(GLA_HEADS):
        vs = slice(h * GLA_DV, (h + 1) * GLA_DV)
        parts.append(_rms(o_ref[:, vs], gn_ref[:, vs]))
    gate = gate_ref[...]
    on = jnp.concatenate(parts, axis=1) * (gate * _sigmoid(gate))
    y_ref[...] = x_ref[...] + _dot(on, wo_ref[...])


def _odd_out(o, gate, x, w, tm):
    n = x.shape[0]
    return pl.pallas_call(
        _odd_out_kernel,
        grid=(n // tm,),
        in_specs=[_row_spec(tm, GLA_VDIM), _row_spec(tm, GLA_VDIM), _const_spec((1, GLA_VDIM)),
                  _const_spec((GLA_VDIM, D_MODEL)), _row_spec(tm, D_MODEL)],
        out_specs=_row_spec(tm, D_MODEL),
        out_shape=jax.ShapeDtypeStruct((n, D_MODEL), F32),
        compiler_params=_cparams(("parallel",)),
    )(o, gate, w["gla_norm"], w["w_out"], x)


def _router_kernel(x_ref, g_ref, wr_ref, *rest):
    xn_ref, idx_ref, gate_ref = rest[-3:]
    xn = _rms(x_ref[...], g_ref[...])
    half = D_MODEL // 2
    xn_ref[0] = xn[:, :half]
    xn_ref[1] = xn[:, half:]
    logits = _dot_f32ish(xn, wr_ref[...])
    lane = lax.broadcasted_iota(jnp.int32, logits.shape, 1)
    logits = jnp.where(lane < N_EXPERTS, logits, NEG_BIG)
    m1 = jnp.max(logits, axis=-1, keepdims=True)
    i1 = jnp.min(jnp.where(logits == m1, lane, LANES), axis=-1, keepdims=True)
    rest = jnp.where(lane == i1, NEG_BIG, logits)
    m2 = jnp.max(rest, axis=-1, keepdims=True)
    i2 = jnp.min(jnp.where(rest == m2, lane, LANES), axis=-1, keepdims=True)
    e2 = jnp.exp(m2 - m1)
    g1 = 1.0 / (1.0 + e2)
    g2 = e2 / (1.0 + e2)
    idx_ref[...] = jnp.where(lane == 0, i1, jnp.where(lane == 1, i2, 0))
    gate_ref[...] = jnp.where(lane == 0, g1, jnp.where(lane == 1, g2, 0.0))


def _router(x, g, wr, tm, n_total, row0, prev=None):
    n = x.shape[0]
    half = D_MODEL // 2
    blk0 = row0 // tm
    prev = () if prev is None else tuple(prev)
    return pl.pallas_call(
        _router_kernel,
        grid=(n // tm,),
        in_specs=[_row_spec(tm, D_MODEL), _const_spec((1, D_MODEL)), _const_spec((D_MODEL, LANES))]
        + [pl.BlockSpec(memory_space=pl.ANY)] * len(prev),
        out_specs=[pl.BlockSpec((2, tm, half), lambda i: (0, blk0 + i, 0)),
                   pl.BlockSpec((tm, LANES), lambda i: (blk0 + i, 0)),
                   pl.BlockSpec((tm, LANES), lambda i: (blk0 + i, 0))],
        out_shape=[jax.ShapeDtypeStruct((2, n_total, half), F32),
                   jax.ShapeDtypeStruct((n_total, LANES), jnp.int32),
                   jax.ShapeDtypeStruct((n_total, LANES), F32)],
        input_output_aliases={3 + k: k for k in range(len(prev))},
        compiler_params=_cparams(("parallel",)),
    )(x, g, wr, *prev)


MOE_TF = 1792
MOE_TMC = 256


def _route(top_i, tm, tmc):
    n = top_i.shape[0]
    slots = 2 * n
    n_tiles = -(-(slots + N_EXPERTS * (tm - 1)) // tm)
    win = tmc + 8
    e_flat = top_i.reshape(-1)
    onehot = (e_flat[:, None] == jnp.arange(N_EXPERTS, dtype=jnp.int32)[None, :]).astype(jnp.int32)
    csum = jnp.cumsum(onehot, axis=0)
    rank = jnp.sum(onehot * csum, axis=1) - 1
    counts = csum[-1]
    padded = ((counts + tm - 1) // tm) * tm
    ends = jnp.cumsum(padded)
    starts = ends - padded
    dest = (jnp.sum(onehot * starts[None, :], axis=1) + rank).astype(jnp.int32)
    tile_start = jnp.arange(n_tiles, dtype=jnp.int32) * tm
    tile_expert = jnp.minimum(jnp.sum((tile_start[:, None] >= ends[None, :]).astype(jnp.int32), axis=1),
                              N_EXPERTS - 1).astype(jnp.int32)
    tile_valid = (tile_start < ends[-1]).astype(jnp.int32)
    src = jnp.zeros((n_tiles * tm,), jnp.int32).at[dest].set(jnp.arange(slots, dtype=jnp.int32) // 2)
    before = jnp.concatenate([jnp.zeros((1, N_EXPERTS), jnp.int32), csum[2 * tmc - 1:-1:2 * tmc]], axis=0)
    wstart = jnp.clip(((starts[None, :] + before) // 8) * 8, 0, n_tiles * tm - win).astype(jnp.int32)
    ws_slot = jnp.sum(onehot * jnp.repeat(wstart, 2 * tmc, axis=0), axis=1)
    local = (e_flat * win + dest - ws_slot).astype(jnp.int32)
    return local, wstart.reshape(-1), src, tile_expert, tile_valid


def _moe_gather_kernel(src_ref, x_ref, o_ref, *, tg):
    base = pl.program_id(1) * tg

    def body(r, carry):
        o_ref[pl.ds(r, 1), :] = x_ref[pl.ds(src_ref[base + r], 1), :]
        return carry

    lax.fori_loop(0, tg, body, 0, unroll=8)


def _moe_gather(src, xn2, tg):
    rows = src.shape[0]
    _, n, half = xn2.shape
    grid_spec = pltpu.PrefetchScalarGridSpec(
        num_scalar_prefetch=1,
        grid=(2, rows // tg),
        in_specs=[pl.BlockSpec((None, n, half), lambda h, i, s: (h, 0, 0),
                               pipeline_mode=pl.Buffered(1))],
        out_specs=pl.BlockSpec((tg, half), lambda h, i, s: (i, h)),
    )
    return pl.pallas_call(
        functools.partial(_moe_gather_kernel, tg=tg),
        grid_spec=grid_spec,
        out_shape=jax.ShapeDtypeStruct((rows, 2 * half), F32),
        compiler_params=_cparams(("arbitrary", "arbitrary")),
    )(src, xn2)


def _moe_up_kernel(te_ref, tv_ref, xs_ref, wg_ref, wu_ref, h_ref):
    @pl.when(tv_ref[pl.program_id(1)] != 0)
    def _():
        xs = xs_ref[...].astype(BF16)
        gate = _dot(xs, wg_ref[...])
        up = _dot(xs, wu_ref[...])
        h_ref[...] = (gate * _sigmoid(gate) * up).astype(BF16)


def _moe_up(te, tv, xs, w_gu, layer, tm):
    rows = xs.shape[0]
    nf = D_FF_EXPERT // MOE_TF
    wspec = lambda off: pl.BlockSpec((None, None, D_MODEL, MOE_TF),
                                     lambda f, t, te, tv: (layer, te[t], 0, off + f))
    grid_spec = pltpu.PrefetchScalarGridSpec(
        num_scalar_prefetch=2,
        grid=(nf, rows // tm),
        in_specs=[pl.BlockSpec((tm, D_MODEL), lambda f, t, te, tv: (t, 0)), wspec(0), wspec(nf)],
        out_specs=pl.BlockSpec((tm, MOE_TF), lambda f, t, te, tv: (t, f)),
    )
    return pl.pallas_call(
        _moe_up_kernel,
        grid_spec=grid_spec,
        out_shape=jax.ShapeDtypeStruct((rows, D_FF_EXPERT), BF16),
        compiler_params=_cparams(("arbitrary", "arbitrary")),
    )(te, tv, xs, w_gu, w_gu)


def _moe_down_kernel(te_ref, tv_ref, h_ref, wd_ref, y_ref):
    @pl.when(tv_ref[pl.program_id(0)] != 0)
    def _():
        y_ref[...] = _dot(h_ref[...], wd_ref[...])


def _moe_down(te, tv, h, w_down, layer, tm):
    rows = h.shape[0]
    grid_spec = pltpu.PrefetchScalarGridSpec(
        num_scalar_prefetch=2,
        grid=(rows // tm,),
        in_specs=[pl.BlockSpec((tm, D_FF_EXPERT), lambda t, te, tv: (t, 0)),
                  pl.BlockSpec((None, None, D_FF_EXPERT, D_MODEL),
                               lambda t, te, tv: (layer, te[t], 0, 0))],
        out_specs=pl.BlockSpec((tm, D_MODEL), lambda t, te, tv: (t, 0)),
    )
    return pl.pallas_call(
        _moe_down_kernel,
        grid_spec=grid_spec,
        out_shape=jax.ShapeDtypeStruct((rows, D_MODEL), F32),
        compiler_params=_cparams(("arbitrary",)),
    )(te, tv, h, w_down)


def _moe_combine_kernel(ws_ref, local_ref, *refs, tmc, win, tile0):
    win_refs = refs[:N_EXPERTS]
    g1_ref, g2_ref, x_ref, fn_ref, o_ref, buf = refs[N_EXPERTS:]
    for e in range(N_EXPERTS):
        buf[e * win:(e + 1) * win, :] = win_refs[e][...]
    base = 2 * (tile0 + pl.program_id(0)) * tmc

    def body(r, carry):
        row = pl.ds(r, 1)
        y1 = buf[pl.ds(local_ref[base + 2 * r], 1), :]
        y2 = buf[pl.ds(local_ref[base + 2 * r + 1], 1), :]
        g1 = g1_ref[row, :]
        g2 = g2_ref[row, :]
        parts = []
        for c in range(D_MODEL // LANES):
            cs = slice(c * LANES, (c + 1) * LANES)
            parts.append(g1 * y1[:, cs] + g2 * y2[:, cs])
        o_ref[row, :] = x_ref[row, :] + jnp.concatenate(parts, axis=1)
        return carry

    lax.fori_loop(0, tmc, body, 0, unroll=16)
    o_ref[...] = _rms(o_ref[...], fn_ref[...])


def _moe_combine(local, wstart, ys, g1b, g2b, x, fn, tmc, row0):
    n = x.shape[0]
    win = tmc + 8
    tile0 = row0 // tmc

    def win_spec(e):
        return pl.BlockSpec(
            (pl.Element(win), pl.Element(D_MODEL)),
            lambda i, ws, lo: (pl.multiple_of(ws[(tile0 + i) * N_EXPERTS + e], 8), 0))

    grid_spec = pltpu.PrefetchScalarGridSpec(
        num_scalar_prefetch=2,
        grid=(n // tmc,),
        in_specs=[win_spec(e) for e in range(N_EXPERTS)]
        + [pl.BlockSpec((tmc, LANES), lambda i, ws, lo: (tile0 + i, 0)),
           pl.BlockSpec((tmc, LANES), lambda i, ws, lo: (tile0 + i, 0)),
           pl.BlockSpec((tmc, D_MODEL), lambda i, ws, lo: (i, 0)),
           pl.BlockSpec((1, D_MODEL), lambda i, ws, lo: (0, 0))],
        out_specs=pl.BlockSpec((tmc, D_MODEL), lambda i, ws, lo: (i, 0)),
        scratch_shapes=[pltpu.VMEM((N_EXPERTS * win, D_MODEL), F32)],
    )
    return pl.pallas_call(
        functools.partial(_moe_combine_kernel, tmc=tmc, win=win, tile0=tile0),
        grid_spec=grid_spec,
        out_shape=jax.ShapeDtypeStruct((n, D_MODEL), F32),
        compiler_params=_cparams(("arbitrary",)),
    )(wstart, local, *([ys] * N_EXPERTS), g1b, g2b, x, fn)


def _scan_vec_layout(xs, batch, seq):
    nb = batch * RWKV_HEADS // SCAN_PAIRS
    x = xs.reshape(5, batch, seq, RWKV_HEADS, 2, SCAN_KH).transpose(0, 2, 5, 4, 1, 3)
    x = x.reshape(5, seq, SCAN_KH, 2, nb, SCAN_PAIRS).transpose(0, 4, 1, 2, 3, 5)
    return x.reshape(5, nb, seq, SCAN_KH, LANES)


def _scan_val_layout(v, batch, seq):
    nb = batch * RWKV_HEADS // SCAN_PAIRS
    v4 = v.reshape(batch, seq, RWKV_HEADS, RWKV_HEAD).transpose(1, 3, 0, 2)
    return v4.reshape(seq, RWKV_HEAD, nb, SCAN_PAIRS).transpose(2, 0, 1, 3)


def _scan_val_unlayout(y, batch, seq):
    v4 = y.transpose(1, 2, 0, 3).reshape(seq, RWKV_HEAD, batch, RWKV_HEADS)
    return v4.transpose(2, 0, 3, 1).reshape(batch * seq, RWKV_DIM)


def _scan_state_layout(s, batch):
    nb = batch * RWKV_HEADS // SCAN_PAIRS
    s6 = s.reshape(batch, RWKV_HEADS, RWKV_HEAD, 2, SCAN_KH).transpose(4, 2, 3, 0, 1)
    s6 = s6.reshape(SCAN_KH, RWKV_HEAD, 2, nb, SCAN_PAIRS).transpose(3, 0, 1, 2, 4)
    return s6.reshape(nb, SCAN_KH, RWKV_HEAD, LANES)


def _scan_state_unlayout(arr, batch):
    nb = arr.shape[0]
    s = arr.reshape(nb, SCAN_KH, RWKV_HEAD, 2, SCAN_PAIRS).transpose(0, 4, 2, 3, 1)
    return s.reshape(batch, RWKV_HEADS, RWKV_HEAD, RWKV_HEAD)


def _swap_halves(w):
    half = w.shape[-1] // 2
    return jnp.concatenate([w[..., half:], w[..., :half]], axis=-1)


def _prep_even(i, norm_mix, norm_ffn, w_in, q_norm, kv_norm, w_uq, w_uk, w_uv, mu, w0, w2, a0, a2,
               g2, k_k, k_a, r_k, ln_g, ln_b, w_out, ffn_gu, ffn_down):
    w = {}
    row = lambda v: v[i].reshape(1, -1)
    w_in = w_in[i]
    w["norm_mix"] = row(norm_mix)
    w["norm_ffn"] = row(norm_ffn)
    w["w_q"] = w_in[:, :MLA_Q_RANK].astype(BF16)
    w_kv = w_in[:, MLA_Q_RANK:MLA_Q_RANK + MLA_LAT]
    w["w_ckv"] = w_kv[:, :MLA_KV_RANK].astype(BF16)
    lane_pad = lambda m: jnp.pad(m, [(0, 0)] * (m.ndim - 1) + [(0, LANES - m.shape[-1])])
    w["w_pe_a"] = lane_pad(w_kv[:, MLA_KV_RANK:]).astype(BF16)
    w["w_pe_b"] = lane_pad(_swap_halves(w_kv[:, MLA_KV_RANK:])).astype(BF16)
    w["w_rw"] = w_in[:, MLA_Q_RANK + MLA_LAT:].astype(BF16)
    w["q_norm"] = row(q_norm)
    w["kv_norm"] = row(kv_norm)
    uq = w_uq[i].reshape(MLA_Q_RANK, MLA_HEADS, MLA_NOPE + MLA_ROPE)
    uq_pe = uq[:, :, MLA_NOPE:]
    w["w_qpe_a"] = lane_pad(uq_pe).reshape(MLA_Q_RANK, -1).astype(BF16)
    w["w_qpe_b"] = lane_pad(_swap_halves(uq_pe)).reshape(MLA_Q_RANK, -1).astype(BF16)
    w["w_qlat"] = _fold_qlat(uq[:, :, :MLA_NOPE].transpose(1, 0, 2), w_uk[i].transpose(1, 0, 2))
    uv = w_uv[i].transpose(1, 0, 2).reshape(MLA_HEADS // 2, 2, MLA_KV_RANK, MLA_V)
    zero = jnp.zeros_like(uv[:, 0])
    w["w_uv_bd"] = jnp.concatenate(
        [jnp.concatenate([uv[:, 0], zero], axis=-1), jnp.concatenate([zero, uv[:, 1]], axis=-1)],
        axis=1).astype(BF16)
    w["mu"] = row(mu)
    w["w0"] = row(w0)
    pad = lambda m, before: jnp.pad(m, ((before, LANES - before - m.shape[0]), (0, 0))).astype(BF16)
    w["w2p"] = pad(w2[i], 0)
    w["a2p"] = pad(a2[i], RWKV_W_LORA)
    w["a0"] = row(a0)
    w["g2"] = g2[i].astype(BF16)
    w["k_k"] = row(k_k)
    w["k_a"] = row(k_a)
    w["r_k"] = row(r_k)
    w["ln_g"] = row(ln_g)
    w["ln_b"] = row(ln_b)
    head = jnp.arange(RWKV_DIM) // RWKV_HEAD
    w["ones_bd"] = (head[:, None] == head[None, :]).astype(BF16)
    w["w_out_a"] = w_out[i][:MLA_HEADS * MLA_V].astype(BF16)
    w["w_out_b"] = w_out[i][MLA_HEADS * MLA_V:].astype(BF16)
    w["ffn_gu"] = ffn_gu[i].astype(BF16)
    w["ffn_down"] = ffn_down[i].astype(BF16)
    return w


def _prep_odd(i, norm_mix, norm_ffn, w_in, a2, ab, gla_norm, w_out, router, moe_gu, moe_down):
    w = {}
    row = lambda v: v[i].reshape(1, -1)
    w_in = w_in[i]
    w["norm_mix"] = row(norm_mix)
    w["norm_ffn"] = row(norm_ffn)
    w["w_q"] = w_in[:, :GLA_KDIM].astype(BF16)
    w["w_k"] = w_in[:, GLA_KDIM:2 * GLA_KDIM].astype(BF16)
    w["w_v"] = w_in[:, 2 * GLA_KDIM:2 * GLA_KDIM + GLA_VDIM].astype(BF16)
    w["w_g"] = w_in[:, 2 * GLA_KDIM + GLA_VDIM:2 * GLA_KDIM + 2 * GLA_VDIM].astype(BF16)
    w["w_xa"] = jnp.pad(w_in[:, 2 * GLA_KDIM + 2 * GLA_VDIM:],
                        ((0, 0), (0, LANES - GLA_GATE_RANK))).astype(BF16)
    w["a2p"] = jnp.pad(a2[i], ((0, LANES - GLA_GATE_RANK), (0, 0))).astype(BF16)
    w["ab"] = row(ab)
    w["gla_norm"] = row(gla_norm)
    w["w_out"] = w_out[i].astype(BF16)
    w["router"] = jnp.pad(router[i], ((0, 0), (0, LANES - N_EXPERTS)))
    w["layer"] = i
    w["moe_gu"] = moe_gu
    w["moe_down"] = moe_down
    return w


def _rope_tables(pos, reps):
    inv = ROPE_THETA ** (-jnp.arange(0, MLA_ROPE, 2, dtype=F32) / MLA_ROPE)
    ang = pos.astype(F32)[:, None] * inv[None, :]
    cos, sin = jnp.cos(ang), jnp.sin(ang)
    pad = ((0, 0), (0, LANES - MLA_ROPE))
    cs = jnp.tile(jnp.pad(jnp.concatenate([cos, cos], axis=-1), pad), (reps, 1))
    sn = jnp.tile(jnp.pad(jnp.concatenate([-sin, sin], axis=-1), pad), (reps, 1))
    return {"cs": cs, "sn": sn, "cs8": jnp.tile(cs, (1, MLA_HEADS)), "sn8": jnp.tile(sn, (1, MLA_HEADS))}


def _even_layer(x, batch, seq, tabs, state, shift0, past, w, tm, tc, after=()):
    n = batch * seq
    lat, lat_b, q_lat, q_pe, rw = _even_in(x, w, tabs, tm)
    if past is None:
        o_lat = _mla_prompt(q_lat, q_pe, lat_b, batch, seq)
    else:
        cache, layer, page_table = past
        rows = seq * MLA_HEADS
        q_full = jnp.concatenate([q_lat.reshape(batch, rows, MLA_KV_RANK),
                                  q_pe.reshape(batch, rows, LANES)[:, :, :MLA_ROPE]], axis=-1)
        new_pad_t = jnp.pad(lat_b.reshape(batch, seq, MLA_LATB)[:, :, :MLA_LAT],
                            ((0, 0), (0, PAGE_SIZE - seq), (0, 0))).transpose(0, 2, 1)
        o_lat = _mla_decode(page_table, q_full, new_pad_t, cache.transpose(0, 1, 3, 2), layer)
        o_lat = o_lat.reshape(n, MLA_HEADS * MLA_KV_RANK)

    rw3 = rw.reshape(batch, seq, RWKV_PROJ)
    xs5, v, g, rkv = _rwkv_prep(rw, shift0, w, tm, seq)
    y_l, s_l = _rwkv_scan(_scan_vec_layout(xs5, batch, seq), _scan_val_layout(v, batch, seq),
                          _scan_state_layout(state, batch), (o_lat,) + tuple(after), tc)
    y = _scan_val_unlayout(y_l, batch, seq)
    new_state = _scan_state_unlayout(s_l, batch)

    x = _even_out(y, rkv, g, o_lat, x, w, tm)
    x = _ffn(x, w["norm_ffn"], w["ffn_gu"], w["ffn_down"], tm)
    return x, lat.reshape(batch, seq, MLA_LAT), new_state, rw3[:, -1], o_lat


def _odd_mixer_layer(x, batch, seq, state, w, tm):
    q, k, v, gate, la = _odd_in(x, w, tm)
    seq_p = -(-seq // GLA_CHUNK) * GLA_CHUNK
    if seq_p != seq:
        padr = lambda t: jnp.pad(t.reshape(batch, seq, -1), ((0, 0), (0, seq_p - seq), (0, 0))
                                 ).reshape(batch * seq_p, -1)
        qp, kp, vp, lap = padr(q), padr(k), padr(v), padr(la)
    else:
        qp, kp, vp, lap = q, k, v, la
    o, st = _gla(qp, kp, vp, lap, state.transpose(0, 1, 3, 2), batch, seq_p)
    if seq_p != seq:
        o = o.reshape(batch, seq_p, GLA_VDIM)[:, :seq].reshape(batch * seq, GLA_VDIM)
    return _odd_out(o, gate, x, w, tm), st.transpose(0, 1, 3, 2)


def _moe_all_groups(xs_groups, tms, w, final_norm, tm_moe):
    sizes = [x.shape[0] for x in xs_groups]
    n_total = sum(sizes)
    bufs, row0 = None, 0
    for x, tm in zip(xs_groups, tms):
        bufs = _router(x, w["norm_ffn"], w["router"], tm, n_total, row0, bufs)
        row0 += x.shape[0]
    xn2, idx, gates = bufs
    local, wstart, src, tile_expert, tile_valid = _route(idx[:, :2], tm_moe, MOE_TMC)
    rows = _moe_gather(src, xn2, tm_moe)
    h = _moe_up(tile_expert, tile_valid, rows, w["moe_gu"], w["layer"], tm_moe)
    ys = _moe_down(tile_expert, tile_valid, h, w["moe_down"], w["layer"], tm_moe)
    g1b = jnp.broadcast_to(gates[:, 0:1], (n_total, LANES))
    g2b = jnp.broadcast_to(gates[:, 1:2], (n_total, LANES))
    outs, row0 = [], 0
    for x in xs_groups:
        outs.append(_moe_combine(local, wstart, ys, g1b, g2b, x, final_norm, MOE_TMC, row0))
        row0 += x.shape[0]
    return outs


def kernel(x_prompt, x_sample, cache_mla, state_rwkv, state_rwkv_shift, state_gla, page_table, norm_mix_even, norm_ffn_even, w_in_even, mla_q_norm, mla_kv_norm, mla_w_uq, mla_w_uk, mla_w_uv, rwkv_mu, rwkv_w0, rwkv_w2, rwkv_a0, rwkv_a2, rwkv_g2, rwkv_k_k, rwkv_k_a, rwkv_r_k, rwkv_ln_g, rwkv_ln_b, w_out_even, ffn_w_gu_even, ffn_w_down_even, norm_mix_odd, norm_ffn_odd, w_in_odd, gla_a2, gla_ab, gla_norm, w_out_odd, moe_router, moe_w_gu, moe_w_down, final_norm):
    bp, tp, _ = x_prompt.shape
    bs, ts, _ = x_sample.shape
    past_len = page_table.shape[1] * PAGE_SIZE
    tm_p, tm_s = 512, bs * ts
    we = _prep_even(0, norm_mix_even, norm_ffn_even, w_in_even, mla_q_norm, mla_kv_norm, mla_w_uq,
                    mla_w_uk, mla_w_uv, rwkv_mu, rwkv_w0, rwkv_w2, rwkv_a0, rwkv_a2, rwkv_g2,
                    rwkv_k_k, rwkv_k_a, rwkv_r_k, rwkv_ln_g, rwkv_ln_b, w_out_even, ffn_w_gu_even,
                    ffn_w_down_even)
    wo = _prep_odd(0, norm_mix_odd, norm_ffn_odd, w_in_odd, gla_a2, gla_ab, gla_norm, w_out_odd,
                   moe_router, moe_w_gu, moe_w_down)
    fn = final_norm.reshape(1, -1)
    tabs_p = _rope_tables(jnp.arange(tp), 1)
    tabs_s = _rope_tables(past_len + jnp.arange(ts), bs)

    hp = x_prompt.reshape(bp * tp, D_MODEL)
    hs = x_sample.reshape(bs * ts, D_MODEL)
    zeros_state = jnp.zeros((bp, RWKV_HEADS, RWKV_HEAD, RWKV_HEAD), F32)
    zeros_shift = jnp.zeros((bp, RWKV_PROJ), F32)
    hp, lat_p, rs_p, sh_p, _ = _even_layer(hp, bp, tp, tabs_p, zeros_state, zeros_shift, None, we,
                                           tm_p, 64)
    hs, lat_s, rs_s, sh_s, _ = _even_layer(hs, bs, ts, tabs_s, state_rwkv[0], state_rwkv_shift[0],
                                           (cache_mla, 0, page_table), we, tm_s, ts)
    zeros_gla = jnp.zeros((bp, GLA_HEADS, GLA_DK, GLA_DV), F32)
    hp, gs_p = _odd_mixer_layer(hp, bp, tp, zeros_gla, wo, tm_p)
    hs, gs_s = _odd_mixer_layer(hs, bs, ts, state_gla[0], wo, tm_s)
    yp, ys = _moe_all_groups([hp, hs], [tm_p, tm_s], wo, fn, 512)
    return (yp.reshape(bp, tp, D_MODEL), ys.reshape(bs, ts, D_MODEL), lat_p[None], lat_s[None],
            rs_p[None], rs_s[None], sh_p[None], sh_s[None], gs_p[None], gs_s[None])
```

```python
import functools

import jax
import jax.numpy as jnp
from jax import lax
from jax.experimental import pallas as pl
from jax.experimental.pallas import tpu as pltpu

F32 = jnp.float32
BF16 = jnp.bfloat16

D_MODEL = 1024
PAGE_SIZE = 128
NORM_EPS = 1e-6

MLA_HEADS = 8
MLA_NOPE = 64
MLA_ROPE = 32
MLA_V = 64
MLA_Q_RANK = 384
MLA_KV_RANK = 256
MLA_LAT = MLA_KV_RANK + MLA_ROPE
MLA_LATB = MLA_KV_RANK + 128
MLA_SCALE = (MLA_NOPE + MLA_ROPE) ** -0.5
ROPE_THETA = 10000.0

RWKV_HEADS = 8
RWKV_HEAD = 64
RWKV_DIM = RWKV_HEADS * RWKV_HEAD
RWKV_W_LORA = 64
RWKV_A_LORA = 64
RWKV_G_LORA = 128
RWKV_PROJ = 3 * RWKV_DIM + RWKV_W_LORA + RWKV_A_LORA + RWKV_G_LORA
RWKV_LN_EPS = 64e-5

GLA_HEADS = 4
GLA_DK = 128
GLA_DV = 256
GLA_KDIM = GLA_HEADS * GLA_DK
GLA_VDIM = GLA_HEADS * GLA_DV
GLA_GATE_RANK = 16
GLA_GATE_NORM = 16.0
GLA_CHUNK = 128

D_FF = 2816
N_EXPERTS = 8
D_FF_EXPERT = 3584

LANES = 128
VMEM_LIMIT = 56 * 1024 * 1024
NEG_BIG = -1e30
LOG2_E = 1.4426950408889634
Q_PRESCALE = MLA_SCALE * LOG2_E


def _cparams(sem):
    return pltpu.CompilerParams(dimension_semantics=sem, vmem_limit_bytes=VMEM_LIMIT)


def _const_spec(shape):
    nd = len(shape)
    return pl.BlockSpec(shape, lambda *_: (0,) * nd)


def _row_spec(tm, width):
    return pl.BlockSpec((tm, width), lambda i: (i, 0))


def _dot(a, b):
    return jnp.dot(a.astype(BF16), b.astype(BF16), preferred_element_type=F32)


def _dot_nt(a, b):
    return lax.dot_general(a.astype(BF16), b.astype(BF16), (((1,), (1,)), ((), ())),
                           preferred_element_type=F32)


def _split2(x):
    hi = x.astype(BF16)
    lo = (x - hi.astype(F32)).astype(BF16)
    return hi, lo


def _split3(x):
    hi = x.astype(BF16)
    r1 = x - hi.astype(F32)
    mid = r1.astype(BF16)
    lo = (r1 - mid.astype(F32)).astype(BF16)
    return hi, mid, lo


def _dot_exact_rhs(x, e):
    hi, mid, lo = _split3(x)
    return (jnp.dot(hi, e, preferred_element_type=F32) + jnp.dot(mid, e, preferred_element_type=F32)
            + jnp.dot(lo, e, preferred_element_type=F32))


def _dot_exact_lhs(e, x):
    hi, mid, lo = _split3(x)
    return (jnp.dot(e, hi, preferred_element_type=F32) + jnp.dot(e, mid, preferred_element_type=F32)
            + jnp.dot(e, lo, preferred_element_type=F32))


def _dot_f32ish(a, b):
    ah, al = _split2(a)
    bh, bl = _split2(b)
    return (jnp.dot(ah, bh, preferred_element_type=F32) + jnp.dot(ah, bl, preferred_element_type=F32)
            + jnp.dot(al, bh, preferred_element_type=F32))


def _lane_tile(x, width):
    return x if width == LANES else jnp.concatenate([x] * (width // LANES), axis=1)


def _rms(x, g, eps=NORM_EPS):
    return x * lax.rsqrt(jnp.mean(x * x, axis=-1, keepdims=True) + eps) * g


def _sigmoid(x):
    return 1.0 / (1.0 + jnp.exp(-x))


def _softplus(x):
    return jnp.maximum(x, 0.0) + jnp.log(1.0 + jnp.exp(-jnp.abs(x)))


def _fold_qlat_kernel(uq_ref, uk_ref, o_ref):
    a = uq_ref[...]
    b = uk_ref[...]
    ah, al = _split2(a)
    bh, bl = _split2(b)
    dn = (((1,), (1,)), ((), ()))
    o = (lax.dot_general(ah, bh, dn, preferred_element_type=F32)
         + lax.dot_general(ah, bl, dn, preferred_element_type=F32)
         + lax.dot_general(al, bh, dn, preferred_element_type=F32))
    o_ref[...] = o.astype(BF16)


def _fold_qlat(uq_nope, uk):
    return pl.pallas_call(
        _fold_qlat_kernel,
        grid=(MLA_HEADS,),
        in_specs=[pl.BlockSpec((None, MLA_Q_RANK, MLA_NOPE), lambda h: (h, 0, 0)),
                  pl.BlockSpec((None, MLA_KV_RANK, MLA_NOPE), lambda h: (h, 0, 0))],
        out_specs=pl.BlockSpec((MLA_Q_RANK, MLA_KV_RANK), lambda h: (0, h)),
        out_shape=jax.ShapeDtypeStruct((MLA_Q_RANK, MLA_HEADS * MLA_KV_RANK), BF16),
        compiler_params=_cparams(("arbitrary",)),
    )(uq_nope, uk)


def _even_in_kernel(x_ref, g_ref, wq_ref, wckv_ref, wpa_ref, wpb_ref, wrw_ref, qn_ref, kvn_ref,
                    cs_ref, sn_ref, wql_ref, wqa_ref, wqb_ref, cs8_ref, sn8_ref,
                    lat_ref, latb_ref, ql_ref, qpe_ref, rw_ref):
    xn = _rms(x_ref[...], g_ref[...]).astype(BF16)
    cq = _rms(_dot(xn, wq_ref[...]), qn_ref[...]).astype(BF16)
    ql_ref[...] = (_dot(cq, wql_ref[...]) * Q_PRESCALE).astype(BF16)
    qpe = _dot(cq, wqa_ref[...]) * cs8_ref[...] + _dot(cq, wqb_ref[...]) * sn8_ref[...]
    qpe_ref[...] = (qpe * Q_PRESCALE).astype(BF16)
    ckv = _rms(_dot(xn, wckv_ref[...]), kvn_ref[...])
    kpe = _dot(xn, wpa_ref[...]) * cs_ref[...] + _dot(xn, wpb_ref[...]) * sn_ref[...]
    lat_ref[:, :MLA_KV_RANK] = ckv
    lat_ref[:, MLA_KV_RANK:] = kpe[:, :MLA_ROPE]
    latb_ref[:, :MLA_KV_RANK] = ckv.astype(BF16)
    latb_ref[:, MLA_KV_RANK:] = kpe.astype(BF16)
    rw_ref[...] = _dot(xn, wrw_ref[...])


def _even_in(x, w, tabs, tm):
    n = x.shape[0]
    nt = tabs["cs"].shape[0] // tm
    tab = lambda width: pl.BlockSpec((tm, width), lambda i: (i % nt, 0))
    hq = MLA_HEADS * MLA_KV_RANK
    hr = MLA_HEADS * LANES
    return pl.pallas_call(
        _even_in_kernel,
        grid=(n // tm,),
        in_specs=[_row_spec(tm, D_MODEL), _const_spec((1, D_MODEL)),
                  _const_spec((D_MODEL, MLA_Q_RANK)), _const_spec((D_MODEL, MLA_KV_RANK)),
                  _const_spec((D_MODEL, LANES)), _const_spec((D_MODEL, LANES)),
                  _const_spec((D_MODEL, RWKV_PROJ)), _const_spec((1, MLA_Q_RANK)),
                  _const_spec((1, MLA_KV_RANK)), tab(LANES), tab(LANES),
                  _const_spec((MLA_Q_RANK, hq)), _const_spec((MLA_Q_RANK, hr)),
                  _const_spec((MLA_Q_RANK, hr)), tab(hr), tab(hr)],
        out_specs=[_row_spec(tm, MLA_LAT), _row_spec(tm, MLA_LATB), _row_spec(tm, hq),
                   _row_spec(tm, hr), _row_spec(tm, RWKV_PROJ)],
        out_shape=[jax.ShapeDtypeStruct((n, MLA_LAT), F32), jax.ShapeDtypeStruct((n, MLA_LATB), BF16),
                   jax.ShapeDtypeStruct((n, hq), BF16), jax.ShapeDtypeStruct((n, hr), BF16),
                   jax.ShapeDtypeStruct((n, RWKV_PROJ), F32)],
        compiler_params=_cparams(("parallel",)),
    )(x, w["norm_mix"], w["w_q"], w["w_ckv"], w["w_pe_a"], w["w_pe_b"], w["w_rw"], w["q_norm"],
      w["kv_norm"], tabs["cs"], tabs["sn"], w["w_qlat"], w["w_qpe_a"], w["w_qpe_b"],
      tabs["cs8"], tabs["sn8"])


ATT_TQ = 512
ATT_TK = 512


def _mla_prompt_kernel(qi_ref, kj_ref, ql_ref, qpe_ref, lat_ref, o_ref,
                       m_sc, l_sc, a_sc, acc_sc, s_sc, p_sc):
    step = pl.program_id(1)
    i = qi_ref[step]
    j = kj_ref[step]
    heads = range(MLA_HEADS)

    @pl.when(j == 0)
    def _():
        m_sc[...] = jnp.full(m_sc.shape, NEG_BIG, F32)
        l_sc[...] = jnp.zeros(l_sc.shape, F32)
        acc_sc[...] = jnp.zeros(acc_sc.shape, F32)

    def tile(masked):
        ckv = lat_ref[:, :MLA_KV_RANK]
        kpe = lat_ref[:, MLA_KV_RANK:]
        for h in heads:
            s_sc[h] = (_dot_nt(ql_ref[:, h * MLA_KV_RANK:(h + 1) * MLA_KV_RANK], ckv)
                       + _dot_nt(qpe_ref[:, h * LANES:(h + 1) * LANES], kpe))
        for h in heads:
            s = s_sc[h]
            if masked:
                tok = lax.broadcasted_iota(jnp.int32, s.shape, 0) + offset
                key = lax.broadcasted_iota(jnp.int32, s.shape, 1)
                s = jnp.where(key <= tok, s, NEG_BIG)
            m_prev = m_sc[h]
            m_new = jnp.maximum(m_prev, jnp.max(s, axis=-1, keepdims=True))
            alpha = jnp.exp2(m_prev - m_new)
            p = jnp.exp2(s - _lane_tile(m_new, ATT_TK))
            l_sc[h] = alpha * l_sc[h] + jnp.sum(p, axis=-1, keepdims=True)
            m_sc[h] = m_new
            a_sc[h] = alpha
            p_sc[h] = p.astype(BF16)
        for h in heads:
            acc_sc[h] = _lane_tile(a_sc[h], MLA_KV_RANK) * acc_sc[h] + _dot(p_sc[h], ckv)

    offset = i * ATT_TQ - j * ATT_TK
    on_diagonal = offset < ATT_TK - 1

    @pl.when(jnp.logical_not(on_diagonal))
    def _():
        tile(False)

    @pl.when(on_diagonal)
    def _():
        tile(True)

    @pl.when(j == (i * ATT_TQ + ATT_TQ - 1) // ATT_TK)
    def _():
        for h in heads:
            o_ref[:, h * MLA_KV_RANK:(h + 1) * MLA_KV_RANK] = (
                acc_sc[h] / _lane_tile(l_sc[h], MLA_KV_RANK)).astype(BF16)


def _mla_prompt(q_lat, q_pe, lat_b, batch, seq):
    nq = seq // ATT_TQ
    nk = seq // ATT_TK
    pairs = [(i, j) for i in range(nq)
             for j in range((i * ATT_TQ + ATT_TQ - 1) // ATT_TK + 1)]
    qi = jnp.array([p[0] for p in pairs], jnp.int32)
    kj = jnp.array([p[1] for p in pairs], jnp.int32)
    hq = MLA_HEADS * MLA_KV_RANK
    grid_spec = pltpu.PrefetchScalarGridSpec(
        num_scalar_prefetch=2,
        grid=(batch, len(pairs)),
        in_specs=[pl.BlockSpec((ATT_TQ, hq), lambda b, s, qi, kj: (b * nq + qi[s], 0)),
                  pl.BlockSpec((ATT_TQ, MLA_HEADS * LANES), lambda b, s, qi, kj: (b * nq + qi[s], 0)),
                  pl.BlockSpec((ATT_TK, MLA_LATB), lambda b, s, qi, kj: (b * nk + kj[s], 0))],
        out_specs=pl.BlockSpec((ATT_TQ, hq), lambda b, s, qi, kj: (b * nq + qi[s], 0)),
        scratch_shapes=[pltpu.VMEM((MLA_HEADS, ATT_TQ, LANES), F32),
                        pltpu.VMEM((MLA_HEADS, ATT_TQ, LANES), F32),
                        pltpu.VMEM((MLA_HEADS, ATT_TQ, LANES), F32),
                        pltpu.VMEM((MLA_HEADS, ATT_TQ, MLA_KV_RANK), F32),
                        pltpu.VMEM((MLA_HEADS, ATT_TQ, ATT_TK), F32),
                        pltpu.VMEM((MLA_HEADS, ATT_TQ, ATT_TK), BF16)],
    )
    return pl.pallas_call(
        _mla_prompt_kernel,
        grid_spec=grid_spec,
        out_shape=jax.ShapeDtypeStruct(q_lat.shape, BF16),
        compiler_params=_cparams(("parallel", "arbitrary")),
    )(qi, kj, q_lat, q_pe, lat_b)


PAGES_PER_STEP = 64
DECODE_GROUPS = 16


def _mla_decode_kernel(pt_ref, q_ref, new_ref, *rest):
    page_refs = rest[:PAGES_PER_STEP]
    o_ref, m_sc, l_sc, acc_sc = rest[PAGES_PER_STEP:]
    j = pl.program_id(1)
    q = q_ref[0]

    @pl.when(j == 0)
    def _():
        m_sc[...] = jnp.full(m_sc.shape, NEG_BIG, F32)
        l_sc[...] = jnp.zeros(l_sc.shape, F32)
        acc_sc[...] = jnp.zeros(acc_sc.shape, F32)

    def update(state, s, values_t):
        m_prev, l_prev, acc = state
        m_new = jnp.maximum(m_prev, jnp.max(s, axis=-1, keepdims=True))
        alpha = jnp.exp2(m_prev - m_new)
        p = jnp.exp2(s - _lane_tile(m_new, s.shape[1]))
        l_new = alpha * l_prev + jnp.sum(p, axis=-1, keepdims=True)
        return m_new, l_new, _lane_tile(alpha, MLA_KV_RANK) * acc + _dot_nt(p, values_t)

    group = PAGES_PER_STEP // DECODE_GROUPS
    keys = [jnp.concatenate([pr[...].astype(BF16) for pr in page_refs[g * group:(g + 1) * group]],
                            axis=1) for g in range(DECODE_GROUPS)]
    scores = [_dot(q, kt) for kt in keys]
    state = (m_sc[...], l_sc[...], acc_sc[...])
    for s, kt in zip(scores, keys):
        state = update(state, s, kt[:MLA_KV_RANK, :])
    m_sc[...], l_sc[...], acc_sc[...] = state

    @pl.when(j == pl.num_programs(1) - 1)
    def _():
        new_t = new_ref[0]
        sn = _dot(q, new_t)
        tok = lax.broadcasted_iota(jnp.int32, sn.shape, 0) >> 3
        key = lax.broadcasted_iota(jnp.int32, sn.shape, 1)
        sn = jnp.where(key <= tok, sn, NEG_BIG)
        _, l_fin, acc_fin = update(state, sn, new_t[:MLA_KV_RANK, :])
        o_ref[0] = (acc_fin / _lane_tile(l_fin, MLA_KV_RANK)).astype(BF16)


def _mla_decode(page_table, q_full, new_pad_t, cache_t, layer):
    db, n_pages = page_table.shape
    rows = q_full.shape[1]
    steps = n_pages // PAGES_PER_STEP

    def page_spec(p):
        return pl.BlockSpec((None, None, MLA_LAT, PAGE_SIZE),
                            lambda b, j, pt: (layer, pt[b, j * PAGES_PER_STEP + p], 0, 0))

    grid_spec = pltpu.PrefetchScalarGridSpec(
        num_scalar_prefetch=1,
        grid=(db, steps),
        in_specs=[pl.BlockSpec((1, rows, MLA_LAT), lambda b, j, pt: (b, 0, 0)),
                  pl.BlockSpec((1, MLA_LAT, PAGE_SIZE), lambda b, j, pt: (b, 0, 0))]
        + [page_spec(p) for p in range(PAGES_PER_STEP)],
        out_specs=pl.BlockSpec((1, rows, MLA_KV_RANK), lambda b, j, pt: (b, 0, 0)),
        scratch_shapes=[pltpu.VMEM((rows, LANES), F32), pltpu.VMEM((rows, LANES), F32),
                        pltpu.VMEM((rows, MLA_KV_RANK), F32)],
    )
    return pl.pallas_call(
        _mla_decode_kernel,
        grid_spec=grid_spec,
        out_shape=jax.ShapeDtypeStruct((db, rows, MLA_KV_RANK), BF16),
        compiler_params=_cparams(("parallel", "arbitrary")),
    )(page_table, q_full, new_pad_t, *([cache_t] * PAGES_PER_STEP))


def _rwkv_prep_kernel(rw_ref, before_ref, sh_ref, mu_ref, w0_ref, w2_ref, a0_ref, a2_ref, g2_ref,
                      kk_ref, ka_ref, rk_ref, ones_ref, xs_ref, v_ref, g_ref, rkv_ref, *, tm, seq):
    rw = rw_ref[...]
    rolled = pltpu.roll(rw, 1, axis=0)
    row = lax.broadcasted_iota(jnp.int32, rw.shape, 0)
    if seq >= tm:
        at_start = pl.program_id(0) % (seq // tm) == 0
        first = jnp.where(at_start, sh_ref[...], before_ref[7:8, :])
        prev = jnp.where(row == 0, first, rolled)
    else:
        prev = jnp.where((row & (seq - 1)) == 0, sh_ref[...], rolled)
    xs = rw + (prev - rw) * mu_ref[...]
    d = RWKV_DIM
    r = xs[:, :d]
    k = xs[:, d:2 * d]
    v = xs[:, 2 * d:3 * d]
    xwa = xs[:, 3 * d:3 * d + LANES]
    xg = xs[:, 3 * d + LANES:]
    ones = ones_ref[...]
    w_log = -_softplus(-(w0_ref[...] + _dot(jnp.tanh(xwa), w2_ref[...]))) - 0.5
    a = _sigmoid(a0_ref[...] + _dot(xwa, a2_ref[...]))
    g_ref[...] = _dot(_sigmoid(xg), g2_ref[...])
    kk = k * kk_ref[...]
    ss = _dot_exact_rhs(kk * kk, ones)
    kk = kk / jnp.maximum(jnp.sqrt(ss), 1e-12)
    k2 = k * (1.0 + (a - 1.0) * ka_ref[...])
    xs_ref[0] = -kk
    xs_ref[1] = jnp.exp(-jnp.exp(w_log))
    xs_ref[2] = kk * a
    xs_ref[3] = k2
    xs_ref[4] = r
    v_ref[...] = v
    rkv_ref[...] = _dot_exact_rhs(r * k2 * rk_ref[...], ones) * v


def _rwkv_prep(rw, shift0, w, tm, seq):
    n = rw.shape[0]
    d = RWKV_DIM
    vec = _const_spec((1, d))
    if seq >= tm:
        tiles = seq // tm
        sh = shift0.reshape(-1, 1, RWKV_PROJ)
        sh_spec = pl.BlockSpec((None, 1, RWKV_PROJ), lambda i: (i // tiles, 0, 0))
    else:
        sh = jnp.repeat(shift0, seq, axis=0)
        sh_spec = _row_spec(tm, RWKV_PROJ)
    before_spec = pl.BlockSpec((8, RWKV_PROJ), lambda i: (jnp.maximum(i * (tm // 8) - 1, 0), 0))
    return pl.pallas_call(
        functools.partial(_rwkv_prep_kernel, tm=tm, seq=seq),
        grid=(n // tm,),
        in_specs=[_row_spec(tm, RWKV_PROJ), before_spec, sh_spec, _const_spec((1, RWKV_PROJ)),
                  vec, _const_spec((LANES, d)), vec, _const_spec((LANES, d)),
                  _const_spec((RWKV_G_LORA, d)), vec, vec, vec, _const_spec((d, d))],
        out_specs=[pl.BlockSpec((5, tm, d), lambda i: (0, i, 0))] + [_row_spec(tm, d)] * 3,
        out_shape=[jax.ShapeDtypeStruct((5, n, d), F32)] + [jax.ShapeDtypeStruct((n, d), F32)] * 3,
        compiler_params=_cparams(("parallel",)),
    )(rw, rw, sh, w["mu"], w["w0"], w["w2p"], w["a0"], w["a2p"], w["g2"], w["k_k"], w["k_a"],
      w["r_k"], w["ones_bd"])


SCAN_KH = RWKV_HEAD // 2
SCAN_PAIRS = LANES // 2
SCAN_VR = RWKV_HEAD // 2


def _rwkv_scan_kernel(a_ref, w_ref, b_ref, k_ref, r_ref, v_ref, s0_ref, *rest, tc):
    y_ref, s_ref, c_sc, d_sc = rest[-4:]
    x = (a_ref, w_ref, b_ref, k_ref, r_ref)
    @pl.when(pl.program_id(1) == 0)
    def _():
        s_ref[...] = s0_ref[...]

    half_a = slice(0, SCAN_VR)
    half_b = slice(SCAN_VR, RWKV_HEAD)

    def both_halves(p):
        return p + pltpu.roll(p, SCAN_PAIRS, axis=1)

    def key_dot(u, w):
        return both_halves(jnp.sum(u * w, axis=0, keepdims=True))

    def first_partial(rows):
        p = s_ref[0, 0, rows, :] * x[0][0, 0,0, 0:1, :]
        for k in range(1, SCAN_KH):
            p = p + s_ref[0, k, rows, :] * x[0][0, 0,0, k:k + 1, :]
        return p

    def half_step(t, rows, sa):
        v_half = v_ref[0, t, rows, :]
        v = jnp.concatenate([v_half, v_half], axis=1)
        q = None
        y = None
        for k in range(SCAN_KH):
            s_old = s_ref[0, k, rows, :]
            qk = s_old * c_sc[k:k + 1, :]
            sn = (s_old * x[1][0, 0,t, k:k + 1, :] + sa * x[2][0, 0,t, k:k + 1, :]
                  + v * x[3][0, 0,t, k:k + 1, :])
            s_ref[0, k, rows, :] = sn
            yk = sn * x[4][0, 0,t, k:k + 1, :]
            q = qk if q is None else q + qk
            y = yk if y is None else y + yk
        return q, sa * d_sc[0:1, :] + v * d_sc[1:2, :], y

    def store_y(t, y_a, y_b):
        y_ref[0, t, half_a, :] = both_halves(y_a)[:, :SCAN_PAIRS]
        y_ref[0, t, half_b, :] = both_halves(y_b)[:, :SCAN_PAIRS]

    def step(t, carry):
        sa_a, q_b, corr_b, y_a, y_b = carry
        store_y(jnp.maximum(t - 1, 0), y_a, y_b)
        a_next = x[0][0, 0,jnp.minimum(t + 1, tc - 1)]
        c_sc[...] = x[1][0, 0,t] * a_next
        d_sc[0:1, :] = key_dot(x[2][0, 0,t], a_next)
        d_sc[1:2, :] = key_dot(x[3][0, 0,t], a_next)
        sa_b = both_halves(q_b) + corr_b
        q_a, corr_a, y_a_new = half_step(t, half_a, sa_a)
        sa_a_next = both_halves(q_a) + corr_a
        q_b_next, corr_b_next, y_b_new = half_step(t, half_b, sa_b)
        return sa_a_next, q_b_next, corr_b_next, y_a_new, y_b_new

    zero = jnp.zeros((SCAN_VR, LANES), F32)
    init = (both_halves(first_partial(half_a)), first_partial(half_b), zero, zero, zero)
    final = lax.fori_loop(0, tc, step, init)
    store_y(tc - 1, final[3], final[4])


def _rwkv_scan(xs, v, s0, after, tc):
    _, nb, t, _, _ = xs[0].shape
    xspec = pl.BlockSpec((1, 1, tc, SCAN_KH, LANES), lambda n, c: (0, n, c, 0, 0))
    vspec = pl.BlockSpec((1, tc, RWKV_HEAD, SCAN_PAIRS), lambda n, c: (n, c, 0, 0))
    sspec = pl.BlockSpec((1, SCAN_KH, RWKV_HEAD, LANES), lambda n, c: (n, 0, 0, 0))
    return pl.pallas_call(
        functools.partial(_rwkv_scan_kernel, tc=tc),
        grid=(nb, t // tc),
        in_specs=[xspec] * 5 + [vspec, sspec] + [pl.BlockSpec(memory_space=pl.ANY)] * len(after),
        out_specs=[vspec, sspec],
        out_shape=[jax.ShapeDtypeStruct(v.shape, F32), jax.ShapeDtypeStruct(s0.shape, F32)],
        scratch_shapes=[pltpu.VMEM((SCAN_KH, LANES), F32), pltpu.VMEM((8, LANES), F32)],
        compiler_params=_cparams(("parallel", "arbitrary")),
    )(*xs, v, s0, *after)


def _even_out_kernel(y_ref, rkv_ref, g_ref, lng_ref, lnb_ref, ones_ref, ol_ref, wuv_ref, woa_ref,
                     wob_ref, x_ref, o_ref):
    ones = ones_ref[...]
    y = y_ref[...]
    inv = 1.0 / RWKV_HEAD
    mean = _dot_exact_rhs(y, ones) * inv
    dlt = y - mean
    var = _dot_exact_rhs(dlt * dlt, ones) * inv
    yn = dlt * lax.rsqrt(var + RWKV_LN_EPS) * lng_ref[...] + lnb_ref[...] + rkv_ref[...]
    ob = (yn * g_ref[...]).astype(BF16)
    pair = 2 * MLA_KV_RANK
    oa = jnp.concatenate(
        [_dot(ol_ref[:, p * pair:(p + 1) * pair], wuv_ref[p]) for p in range(MLA_HEADS // 2)], axis=1)
    o_ref[...] = x_ref[...] + _dot(oa, woa_ref[...]) + _dot(ob, wob_ref[...])


def _even_out(y, rkv, g, o_lat, x, w, tm):
    n = x.shape[0]
    d = RWKV_DIM
    hq = MLA_HEADS * MLA_KV_RANK
    return pl.pallas_call(
        _even_out_kernel,
        grid=(n // tm,),
        in_specs=[_row_spec(tm, d), _row_spec(tm, d), _row_spec(tm, d), _const_spec((1, d)),
                  _const_spec((1, d)), _const_spec((d, d)), _row_spec(tm, hq),
                  _const_spec((MLA_HEADS // 2, 2 * MLA_KV_RANK, 2 * MLA_V)),
                  _const_spec((MLA_HEADS * MLA_V, D_MODEL)), _const_spec((d, D_MODEL)),
                  _row_spec(tm, D_MODEL)],
        out_specs=_row_spec(tm, D_MODEL),
        out_shape=jax.ShapeDtypeStruct((n, D_MODEL), F32),
        compiler_params=_cparams(("parallel",)),
    )(y, rkv, g, w["ln_g"], w["ln_b"], w["ones_bd"], o_lat, w["w_uv_bd"], w["w_out_a"],
      w["w_out_b"], x)


FFN_TF = 1408


def _ffn_kernel(x_ref, g_ref, wg_ref, wu_ref, wd_ref, o_ref, xn_sc, acc_sc):
    f = pl.program_id(1)

    @pl.when(f == 0)
    def _():
        xn_sc[...] = _rms(x_ref[...], g_ref[...]).astype(BF16)
        acc_sc[...] = jnp.zeros(acc_sc.shape, F32)

    xn = xn_sc[...]
    gate = _dot(xn, wg_ref[...])
    up = _dot(xn, wu_ref[...])
    acc_sc[...] += _dot(gate * _sigmoid(gate) * up, wd_ref[...])

    @pl.when(f == pl.num_programs(1) - 1)
    def _():
        o_ref[...] = x_ref[...] + acc_sc[...]


def _ffn(x, g, w_gu, w_down, tm):
    n = x.shape[0]
    nf = D_FF // FFN_TF
    return pl.pallas_call(
        _ffn_kernel,
        grid=(n // tm, nf),
        in_specs=[pl.BlockSpec((tm, D_MODEL), lambda i, f: (i, 0)),
                  pl.BlockSpec((1, D_MODEL), lambda i, f: (0, 0)),
                  pl.BlockSpec((D_MODEL, FFN_TF), lambda i, f: (0, f)),
                  pl.BlockSpec((D_MODEL, FFN_TF), lambda i, f: (0, nf + f)),
                  pl.BlockSpec((FFN_TF, D_MODEL), lambda i, f: (f, 0))],
        out_specs=pl.BlockSpec((tm, D_MODEL), lambda i, f: (i, 0)),
        out_shape=jax.ShapeDtypeStruct((n, D_MODEL), F32),
        scratch_shapes=[pltpu.VMEM((tm, D_MODEL), BF16), pltpu.VMEM((tm, D_MODEL), F32)],
        compiler_params=_cparams(("parallel", "arbitrary")),
    )(x, g, w_gu, w_gu, w_down)


def _odd_in_kernel(x_ref, g_ref, wq_ref, wk_ref, wv_ref, wg_ref, wxa_ref, a2_ref, ab_ref,
                   q_ref, k_ref, v_ref, gate_ref, la_ref):
    xn = _rms(x_ref[...], g_ref[...]).astype(BF16)
    q_ref[...] = _dot(xn, wq_ref[...]) * (GLA_DK ** -0.5)
    k_ref[...] = _dot(xn, wk_ref[...])
    v_ref[...] = _dot(xn, wv_ref[...])
    gate_ref[...] = _dot(xn, wg_ref[...])
    z = _dot(_dot(xn, wxa_ref[...]), a2_ref[...]) + ab_ref[...]
    la_ref[...] = -_softplus(-z) * (1.0 / GLA_GATE_NORM)


def _odd_in(x, w, tm):
    n = x.shape[0]
    return pl.pallas_call(
        _odd_in_kernel,
        grid=(n // tm,),
        in_specs=[_row_spec(tm, D_MODEL), _const_spec((1, D_MODEL)),
                  _const_spec((D_MODEL, GLA_KDIM)), _const_spec((D_MODEL, GLA_KDIM)),
                  _const_spec((D_MODEL, GLA_VDIM)), _const_spec((D_MODEL, GLA_VDIM)),
                  _const_spec((D_MODEL, LANES)), _const_spec((LANES, GLA_KDIM)),
                  _const_spec((1, GLA_KDIM))],
        out_specs=[_row_spec(tm, GLA_KDIM), _row_spec(tm, GLA_KDIM), _row_spec(tm, GLA_VDIM),
                   _row_spec(tm, GLA_VDIM), _row_spec(tm, GLA_KDIM)],
        out_shape=[jax.ShapeDtypeStruct((n, GLA_KDIM), F32), jax.ShapeDtypeStruct((n, GLA_KDIM), F32),
                   jax.ShapeDtypeStruct((n, GLA_VDIM), F32), jax.ShapeDtypeStruct((n, GLA_VDIM), F32),
                   jax.ShapeDtypeStruct((n, GLA_KDIM), F32)],
        compiler_params=_cparams(("parallel",)),
    )(x, w["norm_mix"], w["w_q"], w["w_k"], w["w_v"], w["w_g"], w["w_xa"], w["a2p"], w["ab"])


def _gla_kernel(q_ref, k_ref, v_ref, la_ref, s0_ref, o_ref, st_ref):
    c = GLA_CHUNK

    @pl.when(pl.program_id(1) == 0)
    def _():
        st_ref[...] = s0_ref[...]

    row = lax.broadcasted_iota(jnp.int32, (c, c), 0)
    col = lax.broadcasted_iota(jnp.int32, (c, c), 1)
    tri = row >= col
    tri_b = jnp.where(tri, 1.0, 0.0).astype(BF16)
    heads = range(GLA_HEADS)
    ks = [slice(h * GLA_DK, (h + 1) * GLA_DK) for h in heads]
    vs = [slice(h * GLA_DV, (h + 1) * GLA_DV) for h in heads]
    b = _dot_exact_lhs(tri_b, la_ref[...])
    k = k_ref[...]
    b_end = b[c - 1:c, :]
    qe = (q_ref[...] * jnp.exp(b)).astype(BF16)
    ke = (k * jnp.exp(-b)).astype(BF16)
    k_end = (k * jnp.exp(b_end - b)).astype(BF16)
    e_end = jnp.exp(b_end)
    a_mats = [jnp.where(tri, _dot_nt(qe[:, ks[h]], ke[:, ks[h]]), 0.0).astype(BF16) for h in heads]
    states = [st_ref[0, h] for h in heads]
    for h in heads:
        o_ref[:, vs[h]] = _dot_nt(qe[:, ks[h]], states[h]) + _dot(a_mats[h], v_ref[:, vs[h]])
    for h in heads:
        st_ref[0, h] = states[h] * e_end[:, ks[h]] + _dot(v_ref[:, vs[h]].T, k_end[:, ks[h]])


def _gla(q, k, v, la, s0t, batch, seq):
    nc = seq // GLA_CHUNK
    rspec = lambda width: pl.BlockSpec((GLA_CHUNK, width), lambda b, c: (b * nc + c, 0))
    sspec = pl.BlockSpec((1, GLA_HEADS, GLA_DV, GLA_DK), lambda b, c: (b, 0, 0, 0))
    return pl.pallas_call(
        _gla_kernel,
        grid=(batch, nc),
        in_specs=[rspec(GLA_KDIM), rspec(GLA_KDIM), rspec(GLA_VDIM), rspec(GLA_KDIM), sspec],
        out_specs=[rspec(GLA_VDIM), sspec],
        out_shape=[jax.ShapeDtypeStruct(v.shape, F32), jax.ShapeDtypeStruct(s0t.shape, F32)],
        compiler_params=_cparams(("parallel", "arbitrary")),
    )(q, k, v, la, s0t)


def _odd_out_kernel(o_ref, gate_ref, gn_ref, wo_ref, x_ref, y_ref):
    parts = []
    for h in range(GLA_HEADS):
        vs = slice(h * GLA_DV, (h + 1) * GLA_DV)
        parts.append(_rms(o_ref[:, vs], gn_ref[:, vs]))
    gate = gate_ref[...]
    on = jnp.concatenate(parts, axis=1) * (gate * _sigmoid(gate))
    y_ref[...] = x_ref[...] + _dot(on, wo_ref[...])


def _odd_out(o, gate, x, w, tm):
    n = x.shape[0]
    return pl.pallas_call(
        _odd_out_kernel,
        grid=(n // tm,),
        in_specs=[_row_spec(tm, GLA_VDIM), _row_spec(tm, GLA_VDIM), _const_spec((1, GLA_VDIM)),
                  _const_spec((GLA_VDIM, D_MODEL)), _row_spec(tm, D_MODEL)],
        out_specs=_row_spec(tm, D_MODEL),
        out_shape=jax.ShapeDtypeStruct((n, D_MODEL), F32),
        compiler_params=_cparams(("parallel",)),
    )(o, gate, w["gla_norm"], w["w_out"], x)


def _router_kernel(x_ref, g_ref, wr_ref, *rest):
    xn_ref, idx_ref, gate_ref = rest[-3:]
    xn = _rms(x_ref[...], g_ref[...])
    half = D_MODEL // 2
    xn_ref[0] = xn[:, :half]
    xn_ref[1] = xn[:, half:]
    logits = _dot_f32ish(xn, wr_ref[...])
    lane = lax.broadcasted_iota(jnp.int32, logits.shape, 1)
    logits = jnp.where(lane < N_EXPERTS, logits, NEG_BIG)
    m1 = jnp.max(logits, axis=-1, keepdims=True)
    i1 = jnp.min(jnp.where(logits == m1, lane, LANES), axis=-1, keepdims=True)
    rest = jnp.where(lane == i1, NEG_BIG, logits)
    m2 = jnp.max(rest, axis=-1, keepdims=True)
    i2 = jnp.min(jnp.where(rest == m2, lane, LANES), axis=-1, keepdims=True)
    e2 = jnp.exp(m2 - m1)
    g1 = 1.0 / (1.0 + e2)
    g2 = e2 / (1.0 + e2)
    idx_ref[...] = jnp.where(lane == 0, i1, jnp.where(lane == 1, i2, 0))
    gate_ref[...] = jnp.where(lane == 0, g1, jnp.where(lane == 1, g2, 0.0))


def _router(x, g, wr, tm, n_total, row0, prev=None):
    n = x.shape[0]
    half = D_MODEL // 2
    blk0 = row0 // tm
    prev = () if prev is None else tuple(prev)
    return pl.pallas_call(
        _router_kernel,
        grid=(n // tm,),
        in_specs=[_row_spec(tm, D_MODEL), _const_spec((1, D_MODEL)), _const_spec((D_MODEL, LANES))]
        + [pl.BlockSpec(memory_space=pl.ANY)] * len(prev),
        out_specs=[pl.BlockSpec((2, tm, half), lambda i: (0, blk0 + i, 0)),
                   pl.BlockSpec((tm, LANES), lambda i: (blk0 + i, 0)),
                   pl.BlockSpec((tm, LANES), lambda i: (blk0 + i, 0))],
        out_shape=[jax.ShapeDtypeStruct((2, n_total, half), F32),
                   jax.ShapeDtypeStruct((n_total, LANES), jnp.int32),
                   jax.ShapeDtypeStruct((n_total, LANES), F32)],
        input_output_aliases={3 + k: k for k in range(len(prev))},
        compiler_params=_cparams(("parallel",)),
    )(x, g, wr, *prev)


MOE_TF = 1792
MOE_TMC = 256


def _route(top_i, tm, tmc):
    n = top_i.shape[0]
    slots = 2 * n
    n_tiles = -(-(slots + N_EXPERTS * (tm - 1)) // tm)
    win = tmc + 8
    e_flat = top_i.reshape(-1)
    onehot = (e_flat[:, None] == jnp.arange(N_EXPERTS, dtype=jnp.int32)[None, :]).astype(jnp.int32)
    csum = jnp.cumsum(onehot, axis=0)
    rank = jnp.sum(onehot * csum, axis=1) - 1
    counts = csum[-1]
    padded = ((counts + tm - 1) // tm) * tm
    ends = jnp.cumsum(padded)
    starts = ends - padded
    dest = (jnp.sum(onehot * starts[None, :], axis=1) + rank).astype(jnp.int32)
    tile_start = jnp.arange(n_tiles, dtype=jnp.int32) * tm
    tile_expert = jnp.minimum(jnp.sum((tile_start[:, None] >= ends[None, :]).astype(jnp.int32), axis=1),
                              N_EXPERTS - 1).astype(jnp.int32)
    tile_valid = (tile_start < ends[-1]).astype(jnp.int32)
    src = jnp.zeros((n_tiles * tm,), jnp.int32).at[dest].set(jnp.arange(slots, dtype=jnp.int32) // 2)
    before = jnp.concatenate([jnp.zeros((1, N_EXPERTS), jnp.int32), csum[2 * tmc - 1:-1:2 * tmc]], axis=0)
    wstart = jnp.clip(((starts[None, :] + before) // 8) * 8, 0, n_tiles * tm - win).astype(jnp.int32)
    ws_slot = jnp.sum(onehot * jnp.repeat(wstart, 2 * tmc, axis=0), axis=1)
    local = (e_flat * win + dest - ws_slot).astype(jnp.int32)
    return local, wstart.reshape(-1), src, tile_expert, tile_valid


def _moe_gather_kernel(src_ref, x_ref, o_ref, *, tg):
    base = pl.program_id(1) * tg

    def body(r, carry):
        o_ref[pl.ds(r, 1), :] = x_ref[pl.ds(src_ref[base + r], 1), :]
        return carry

    lax.fori_loop(0, tg, body, 0, unroll=8)


def _moe_gather(src, xn2, tg):
    rows = src.shape[0]
    _, n, half = xn2.shape
    grid_spec = pltpu.PrefetchScalarGridSpec(
        num_scalar_prefetch=1,
        grid=(2, rows // tg),
        in_specs=[pl.BlockSpec((None, n, half), lambda h, i, s: (h, 0, 0),
                               pipeline_mode=pl.Buffered(1))],
        out_specs=pl.BlockSpec((tg, half), lambda h, i, s: (i, h)),
    )
    return pl.pallas_call(
        functools.partial(_moe_gather_kernel, tg=tg),
        grid_spec=grid_spec,
        out_shape=jax.ShapeDtypeStruct((rows, 2 * half), F32),
        compiler_params=_cparams(("arbitrary", "arbitrary")),
    )(src, xn2)


def _moe_up_kernel(te_ref, tv_ref, xs_ref, wg_ref, wu_ref, h_ref):
    @pl.when(tv_ref[pl.program_id(1)] != 0)
    def _():
        xs = xs_ref[...].astype(BF16)
        gate = _dot(xs, wg_ref[...])
        up = _dot(xs, wu_ref[...])
        h_ref[...] = (gate * _sigmoid(gate) * up).astype(BF16)


def _moe_up(te, tv, xs, w_gu, layer, tm):
    rows = xs.shape[0]
    nf = D_FF_EXPERT // MOE_TF
    wspec = lambda off: pl.BlockSpec((None, None, D_MODEL, MOE_TF),
                                     lambda f, t, te, tv: (layer, te[t], 0, off + f))
    grid_spec = pltpu.PrefetchScalarGridSpec(
        num_scalar_prefetch=2,
        grid=(nf, rows // tm),
        in_specs=[pl.BlockSpec((tm, D_MODEL), lambda f, t, te, tv: (t, 0)), wspec(0), wspec(nf)],
        out_specs=pl.BlockSpec((tm, MOE_TF), lambda f, t, te, tv: (t, f)),
    )
    return pl.pallas_call(
        _moe_up_kernel,
        grid_spec=grid_spec,
        out_shape=jax.ShapeDtypeStruct((rows, D_FF_EXPERT), BF16),
        compiler_params=_cparams(("arbitrary", "arbitrary")),
    )(te, tv, xs, w_gu, w_gu)


def _moe_down_kernel(te_ref, tv_ref, h_ref, wd_ref, y_ref):
    @pl.when(tv_ref[pl.program_id(0)] != 0)
    def _():
        y_ref[...] = _dot(h_ref[...], wd_ref[...])


def _moe_down(te, tv, h, w_down, layer, tm):
    rows = h.shape[0]
    grid_spec = pltpu.PrefetchScalarGridSpec(
        num_scalar_prefetch=2,
        grid=(rows // tm,),
        in_specs=[pl.BlockSpec((tm, D_FF_EXPERT), lambda t, te, tv: (t, 0)),
                  pl.BlockSpec((None, None, D_FF_EXPERT, D_MODEL),
                               lambda t, te, tv: (layer, te[t], 0, 0))],
        out_specs=pl.BlockSpec((tm, D_MODEL), lambda t, te, tv: (t, 0)),
    )
    return pl.pallas_call(
        _moe_down_kernel,
        grid_spec=grid_spec,
        out_shape=jax.ShapeDtypeStruct((rows, D_MODEL), F32),
        compiler_params=_cparams(("arbitrary",)),
    )(te, tv, h, w_down)


def _moe_combine_kernel(ws_ref, local_ref, *refs, tmc, win, tile0):
    win_refs = refs[:N_EXPERTS]
    g1_ref, g2_ref, x_ref, fn_ref, o_ref, buf = refs[N_EXPERTS:]
    for e in range(N_EXPERTS):
        buf[e * win:(e + 1) * win, :] = win_refs[e][...]
    base = 2 * (tile0 + pl.program_id(0)) * tmc

    def body(r, carry):
        row = pl.ds(r, 1)
        y1 = buf[pl.ds(local_ref[base + 2 * r], 1), :]
        y2 = buf[pl.ds(local_ref[base + 2 * r + 1], 1), :]
        g1 = g1_ref[row, :]
        g2 = g2_ref[row, :]
        parts = []
        for c in range(D_MODEL // LANES):
            cs = slice(c * LANES, (c + 1) * LANES)
            parts.append(g1 * y1[:, cs] + g2 * y2[:, cs])
        o_ref[row, :] = x_ref[row, :] + jnp.concatenate(parts, axis=1)
        return carry

    lax.fori_loop(0, tmc, body, 0, unroll=16)
    o_ref[...] = _rms(o_ref[...], fn_ref[...])


def _moe_combine(local, wstart, ys, g1b, g2b, x, fn, tmc, row0):
    n = x.shape[0]
    win = tmc + 8
    tile0 = row0 // tmc

    def win_spec(e):
        return pl.BlockSpec(
            (pl.Element(win), pl.Element(D_MODEL)),
            lambda i, ws, lo: (pl.multiple_of(ws[(tile0 + i) * N_EXPERTS + e], 8), 0))

    grid_spec = pltpu.PrefetchScalarGridSpec(
        num_scalar_prefetch=2,
        grid=(n // tmc,),
        in_specs=[win_spec(e) for e in range(N_EXPERTS)]
        + [pl.BlockSpec((tmc, LANES), lambda i, ws, lo: (tile0 + i, 0)),
           pl.BlockSpec((tmc, LANES), lambda i, ws, lo: (tile0 + i, 0)),
           pl.BlockSpec((tmc, D_MODEL), lambda i, ws, lo: (i, 0)),
           pl.BlockSpec((1, D_MODEL), lambda i, ws, lo: (0, 0))],
        out_specs=pl.BlockSpec((tmc, D_MODEL), lambda i, ws, lo: (i, 0)),
        scratch_shapes=[pltpu.VMEM((N_EXPERTS * win, D_MODEL), F32)],
    )
    return pl.pallas_call(
        functools.partial(_moe_combine_kernel, tmc=tmc, win=win, tile0=tile0),
        grid_spec=grid_spec,
        out_shape=jax.ShapeDtypeStruct((n, D_MODEL), F32),
        compiler_params=_cparams(("arbitrary",)),
    )(wstart, local, *([ys] * N_EXPERTS), g1b, g2b, x, fn)


def _scan_vec_layout(xs, batch, seq):
    nb = batch * RWKV_HEADS // SCAN_PAIRS
    na = xs.shape[0]
    x = xs.reshape(na, batch, seq, RWKV_HEADS, 2, SCAN_KH).transpose(0, 2, 5, 4, 1, 3)
    x = x.reshape(na, seq, SCAN_KH, 2, nb, SCAN_PAIRS).transpose(0, 4, 1, 2, 3, 5)
    return x.reshape(na, nb, seq, SCAN_KH, LANES)


def _scan_val_layout(v, batch, seq):
    nb = batch * RWKV_HEADS // SCAN_PAIRS
    v4 = v.reshape(batch, seq, RWKV_HEADS, RWKV_HEAD).transpose(1, 3, 0, 2)
    return v4.reshape(seq, RWKV_HEAD, nb, SCAN_PAIRS).transpose(2, 0, 1, 3)


def _scan_val_unlayout(y, batch, seq):
    v4 = y.transpose(1, 2, 0, 3).reshape(seq, RWKV_HEAD, batch, RWKV_HEADS)
    return v4.transpose(2, 0, 3, 1).reshape(batch * seq, RWKV_DIM)


def _scan_state_layout(s, batch):
    nb = batch * RWKV_HEADS // SCAN_PAIRS
    s6 = s.reshape(batch, RWKV_HEADS, RWKV_HEAD, 2, SCAN_KH).transpose(4, 2, 3, 0, 1)
    s6 = s6.reshape(SCAN_KH, RWKV_HEAD, 2, nb, SCAN_PAIRS).transpose(3, 0, 1, 2, 4)
    return s6.reshape(nb, SCAN_KH, RWKV_HEAD, LANES)


def _scan_state_unlayout(arr, batch):
    nb = arr.shape[0]
    s = arr.reshape(nb, SCAN_KH, RWKV_HEAD, 2, SCAN_PAIRS).transpose(0, 4, 2, 3, 1)
    return s.reshape(batch, RWKV_HEADS, RWKV_HEAD, RWKV_HEAD)


def _swap_halves(w):
    half = w.shape[-1] // 2
    return jnp.concatenate([w[..., half:], w[..., :half]], axis=-1)


def _prep_even(i, norm_mix, norm_ffn, w_in, q_norm, kv_norm, w_uq, w_uk, w_uv, mu, w0, w2, a0, a2,
               g2, k_k, k_a, r_k, ln_g, ln_b, w_out, ffn_gu, ffn_down):
    w = {}
    row = lambda v: v[i].reshape(1, -1)
    w_in = w_in[i]
    w["norm_mix"] = row(norm_mix)
    w["norm_ffn"] = row(norm_ffn)
    w["w_q"] = w_in[:, :MLA_Q_RANK].astype(BF16)
    w_kv = w_in[:, MLA_Q_RANK:MLA_Q_RANK + MLA_LAT]
    w["w_ckv"] = w_kv[:, :MLA_KV_RANK].astype(BF16)
    lane_pad = lambda m: jnp.pad(m, [(0, 0)] * (m.ndim - 1) + [(0, LANES - m.shape[-1])])
    w["w_pe_a"] = lane_pad(w_kv[:, MLA_KV_RANK:]).astype(BF16)
    w["w_pe_b"] = lane_pad(_swap_halves(w_kv[:, MLA_KV_RANK:])).astype(BF16)
    w["w_rw"] = w_in[:, MLA_Q_RANK + MLA_LAT:].astype(BF16)
    w["q_norm"] = row(q_norm)
    w["kv_norm"] = row(kv_norm)
    uq = w_uq[i].reshape(MLA_Q_RANK, MLA_HEADS, MLA_NOPE + MLA_ROPE)
    uq_pe = uq[:, :, MLA_NOPE:]
    w["w_qpe_a"] = lane_pad(uq_pe).reshape(MLA_Q_RANK, -1).astype(BF16)
    w["w_qpe_b"] = lane_pad(_swap_halves(uq_pe)).reshape(MLA_Q_RANK, -1).astype(BF16)
    w["w_qlat"] = _fold_qlat(uq[:, :, :MLA_NOPE].transpose(1, 0, 2), w_uk[i].transpose(1, 0, 2))
    uv = w_uv[i].transpose(1, 0, 2).reshape(MLA_HEADS // 2, 2, MLA_KV_RANK, MLA_V)
    zero = jnp.zeros_like(uv[:, 0])
    w["w_uv_bd"] = jnp.concatenate(
        [jnp.concatenate([uv[:, 0], zero], axis=-1), jnp.concatenate([zero, uv[:, 1]], axis=-1)],
        axis=1).astype(BF16)
    w["mu"] = row(mu)
    w["w0"] = row(w0)
    pad = lambda m, before: jnp.pad(m, ((before, LANES - before - m.shape[0]), (0, 0))).astype(BF16)
    w["w2p"] = pad(w2[i], 0)
    w["a2p"] = pad(a2[i], RWKV_W_LORA)
    w["a0"] = row(a0)
    w["g2"] = g2[i].astype(BF16)
    w["k_k"] = row(k_k)
    w["k_a"] = row(k_a)
    w["r_k"] = row(r_k)
    w["ln_g"] = row(ln_g)
    w["ln_b"] = row(ln_b)
    head = jnp.arange(RWKV_DIM) // RWKV_HEAD
    w["ones_bd"] = (head[:, None] == head[None, :]).astype(BF16)
    w["w_out_a"] = w_out[i][:MLA_HEADS * MLA_V].astype(BF16)
    w["w_out_b"] = w_out[i][MLA_HEADS * MLA_V:].astype(BF16)
    w["ffn_gu"] = ffn_gu[i].astype(BF16)
    w["ffn_down"] = ffn_down[i].astype(BF16)
    return w


def _prep_odd(i, norm_mix, norm_ffn, w_in, a2, ab, gla_norm, w_out, router, moe_gu, moe_down):
    w = {}
    row = lambda v: v[i].reshape(1, -1)
    w_in = w_in[i]
    w["norm_mix"] = row(norm_mix)
    w["norm_ffn"] = row(norm_ffn)
    w["w_q"] = w_in[:, :GLA_KDIM].astype(BF16)
    w["w_k"] = w_in[:, GLA_KDIM:2 * GLA_KDIM].astype(BF16)
    w["w_v"] = w_in[:, 2 * GLA_KDIM:2 * GLA_KDIM + GLA_VDIM].astype(BF16)
    w["w_g"] = w_in[:, 2 * GLA_KDIM + GLA_VDIM:2 * GLA_KDIM + 2 * GLA_VDIM].astype(BF16)
    w["w_xa"] = jnp.pad(w_in[:, 2 * GLA_KDIM + 2 * GLA_VDIM:],
                        ((0, 0), (0, LANES - GLA_GATE_RANK))).astype(BF16)
    w["a2p"] = jnp.pad(a2[i], ((0, LANES - GLA_GATE_RANK), (0, 0))).astype(BF16)
    w["ab"] = row(ab)
    w["gla_norm"] = row(gla_norm)
    w["w_out"] = w_out[i].astype(BF16)
    w["router"] = jnp.pad(router[i], ((0, 0), (0, LANES - N_EXPERTS)))
    w["layer"] = i
    w["moe_gu"] = moe_gu
    w["moe_down"] = moe_down
    return w


def _rope_tables(pos, reps):
    inv = ROPE_THETA ** (-jnp.arange(0, MLA_ROPE, 2, dtype=F32) / MLA_ROPE)
    ang = pos.astype(F32)[:, None] * inv[None, :]
    cos, sin = jnp.cos(ang), jnp.sin(ang)
    pad = ((0, 0), (0, LANES - MLA_ROPE))
    cs = jnp.tile(jnp.pad(jnp.concatenate([cos, cos], axis=-1), pad), (reps, 1))
    sn = jnp.tile(jnp.pad(jnp.concatenate([-sin, sin], axis=-1), pad), (reps, 1))
    return {"cs": cs, "sn": sn, "cs8": jnp.tile(cs, (1, MLA_HEADS)), "sn8": jnp.tile(sn, (1, MLA_HEADS))}


def _even_layer(x, batch, seq, tabs, state, shift0, past, w, tm, tc, after=()):
    n = batch * seq
    lat, lat_b, q_lat, q_pe, rw = _even_in(x, w, tabs, tm)
    if past is None:
        o_lat = _mla_prompt(q_lat, q_pe, lat_b, batch, seq)
    else:
        cache, layer, page_table = past
        rows = seq * MLA_HEADS
        q_full = jnp.concatenate([q_lat.reshape(batch, rows, MLA_KV_RANK),
                                  q_pe.reshape(batch, rows, LANES)[:, :, :MLA_ROPE]], axis=-1)
        new_pad_t = jnp.pad(lat_b.reshape(batch, seq, MLA_LATB)[:, :, :MLA_LAT],
                            ((0, 0), (0, PAGE_SIZE - seq), (0, 0))).transpose(0, 2, 1)
        o_lat = _mla_decode(page_table, q_full, new_pad_t, cache.transpose(0, 1, 3, 2), layer)
        o_lat = o_lat.reshape(n, MLA_HEADS * MLA_KV_RANK)

    rw3 = rw.reshape(batch, seq, RWKV_PROJ)
    xs5, v, g, rkv = _rwkv_prep(rw, shift0, w, tm, seq)
    xs = [_scan_vec_layout(xs5[i:i + 1], batch, seq) for i in range(5)]
    y_l, s_l = _rwkv_scan(xs, _scan_val_layout(v, batch, seq),
                          _scan_state_layout(state, batch), (o_lat,) + tuple(after), tc)
    y = _scan_val_unlayout(y_l, batch, seq)
    new_state = _scan_state_unlayout(s_l, batch)

    x = _even_out(y, rkv, g, o_lat, x, w, tm)
    x = _ffn(x, w["norm_ffn"], w["ffn_gu"], w["ffn_down"], tm)
    return x, lat.reshape(batch, seq, MLA_LAT), new_state, rw3[:, -1], o_lat


def _odd_mixer_layer(x, batch, seq, state, w, tm):
    q, k, v, gate, la = _odd_in(x, w, tm)
    seq_p = -(-seq // GLA_CHUNK) * GLA_CHUNK
    if seq_p != seq:
        padr = lambda t: jnp.pad(t.reshape(batch, seq, -1), ((0, 0), (0, seq_p - seq), (0, 0))
                                 ).reshape(batch * seq_p, -1)
        qp, kp, vp, lap = padr(q), padr(k), padr(v), padr(la)
    else:
        qp, kp, vp, lap = q, k, v, la
    o, st = _gla(qp, kp, vp, lap, state.transpose(0, 1, 3, 2), batch, seq_p)
    if seq_p != seq:
        o = o.reshape(batch, seq_p, GLA_VDIM)[:, :seq].reshape(batch * seq, GLA_VDIM)
    return _odd_out(o, gate, x, w, tm), st.transpose(0, 1, 3, 2)


def _moe_all_groups(xs_groups, tms, w, final_norm, tm_moe):
    sizes = [x.shape[0] for x in xs_groups]
    n_total = sum(sizes)
    bufs, row0 = None, 0
    for x, tm in zip(xs_groups, tms):
        bufs = _router(x, w["norm_ffn"], w["router"], tm, n_total, row0, bufs)
        row0 += x.shape[0]
    xn2, idx, gates = bufs
    local, wstart, src, tile_expert, tile_valid = _route(idx[:, :2], tm_moe, MOE_TMC)
    rows = _moe_gather(src, xn2, tm_moe)
    h = _moe_up(tile_expert, tile_valid, rows, w["moe_gu"], w["layer"], tm_moe)
    ys = _moe_down(tile_expert, tile_valid, h, w["moe_down"], w["layer"], tm_moe)
    g1b = jnp.broadcast_to(gates[:, 0:1], (n_total, LANES))
    g2b = jnp.broadcast_to(gates[:, 1:2], (n_total, LANES))
    outs, row0 = [], 0
    for x in xs_groups:
        outs.append(_moe_combine(local, wstart, ys, g1b, g2b, x, final_norm, MOE_TMC, row0))
        row0 += x.shape[0]
    return outs


def kernel(x_prompt, x_sample, cache_mla, state_rwkv, state_rwkv_shift, state_gla, page_table, norm_mix_even, norm_ffn_even, w_in_even, mla_q_norm, mla_kv_norm, mla_w_uq, mla_w_uk, mla_w_uv, rwkv_mu, rwkv_w0, rwkv_w2, rwkv_a0, rwkv_a2, rwkv_g2, rwkv_k_k, rwkv_k_a, rwkv_r_k, rwkv_ln_g, rwkv_ln_b, w_out_even, ffn_w_gu_even, ffn_w_down_even, norm_mix_odd, norm_ffn_odd, w_in_odd, gla_a2, gla_ab, gla_norm, w_out_odd, moe_router, moe_w_gu, moe_w_down, final_norm):
    bp, tp, _ = x_prompt.shape
    bs, ts, _ = x_sample.shape
    past_len = page_table.shape[1] * PAGE_SIZE
    tm_p, tm_s = 512, bs * ts
    we = _prep_even(0, norm_mix_even, norm_ffn_even, w_in_even, mla_q_norm, mla_kv_norm, mla_w_uq,
                    mla_w_uk, mla_w_uv, rwkv_mu, rwkv_w0, rwkv_w2, rwkv_a0, rwkv_a2, rwkv_g2,
                    rwkv_k_k, rwkv_k_a, rwkv_r_k, rwkv_ln_g, rwkv_ln_b, w_out_even, ffn_w_gu_even,
                    ffn_w_down_even)
    wo = _prep_odd(0, norm_mix_odd, norm_ffn_odd, w_in_odd, gla_a2, gla_ab, gla_norm, w_out_odd,
                   moe_router, moe_w_gu, moe_w_down)
    fn = final_norm.reshape(1, -1)
    tabs_p = _rope_tables(jnp.arange(tp), 1)
    tabs_s = _rope_tables(past_len + jnp.arange(ts), bs)

    hp = x_prompt.reshape(bp * tp, D_MODEL)
    hs = x_sample.reshape(bs * ts, D_MODEL)
    zeros_state = jnp.zeros((bp, RWKV_HEADS, RWKV_HEAD, RWKV_HEAD), F32)
    zeros_shift = jnp.zeros((bp, RWKV_PROJ), F32)
    hp, lat_p, rs_p, sh_p, _ = _even_layer(hp, bp, tp, tabs_p, zeros_state, zeros_shift, None, we,
                                           tm_p, 64)
    hs, lat_s, rs_s, sh_s, _ = _even_layer(hs, bs, ts, tabs_s, state_rwkv[0], state_rwkv_shift[0],
                                           (cache_mla, 0, page_table), we, tm_s, ts)
    zeros_gla = jnp.zeros((bp, GLA_HEADS, GLA_DK, GLA_DV), F32)
    hp, gs_p = _odd_mixer_layer(hp, bp, tp, zeros_gla, wo, tm_p)
    hs, gs_s = _odd_mixer_layer(hs, bs, ts, state_gla[0], wo, tm_s)
    yp, ys = _moe_all_groups([hp, hs], [tm_p, tm_s], wo, fn, 512)
    return (yp.reshape(bp, tp, D_MODEL), ys.reshape(bs, ts, D_MODEL), lat_p[None], lat_s[None],
            rs_p[None], rs_s[None], sh_p[None], sh_s[None], gs_p[None], gs_s[None])
```

```python
import functools

import jax
import jax.numpy as jnp
from jax import lax
from jax.experimental import pallas as pl
from jax.experimental.pallas import tpu as pltpu

F32 = jnp.float32
BF16 = jnp.bfloat16

D_MODEL = 1024
PAGE_SIZE = 128
NORM_EPS = 1e-6

MLA_HEADS = 8
MLA_NOPE = 64
MLA_ROPE = 32
MLA_V = 64
MLA_Q_RANK = 384
MLA_KV_RANK = 256
MLA_LAT = MLA_KV_RANK + MLA_ROPE
MLA_LATB = MLA_KV_RANK + 128
MLA_SCALE = (MLA_NOPE + MLA_ROPE) ** -0.5
ROPE_THETA = 10000.0

RWKV_HEADS = 8
RWKV_HEAD = 64
RWKV_DIM = RWKV_HEADS * RWKV_HEAD
RWKV_W_LORA = 64
RWKV_A_LORA = 64
RWKV_G_LORA = 128
RWKV_PROJ = 3 * RWKV_DIM + RWKV_W_LORA + RWKV_A_LORA + RWKV_G_LORA
RWKV_LN_EPS = 64e-5

GLA_HEADS = 4
GLA_DK = 128
GLA_DV = 256
GLA_KDIM = GLA_HEADS * GLA_DK
GLA_VDIM = GLA_HEADS * GLA_DV
GLA_GATE_RANK = 16
GLA_GATE_NORM = 16.0
GLA_CHUNK = 128

D_FF = 2816
N_EXPERTS = 8
D_FF_EXPERT = 3584

LANES = 128
VMEM_LIMIT = 56 * 1024 * 1024
NEG_BIG = -1e30
LOG2_E = 1.4426950408889634
Q_PRESCALE = MLA_SCALE * LOG2_E


def _cparams(sem):
    return pltpu.CompilerParams(dimension_semantics=sem, vmem_limit_bytes=VMEM_LIMIT)


def _const_spec(shape):
    nd = len(shape)
    return pl.BlockSpec(shape, lambda *_: (0,) * nd)


def _row_spec(tm, width):
    return pl.BlockSpec((tm, width), lambda i: (i, 0))


def _dot(a, b):
    return jnp.dot(a.astype(BF16), b.astype(BF16), preferred_element_type=F32)


def _dot_nt(a, b):
    return lax.dot_general(a.astype(BF16), b.astype(BF16), (((1,), (1,)), ((), ())),
                           preferred_element_type=F32)


def _split2(x):
    hi = x.astype(BF16)
    lo = (x - hi.astype(F32)).astype(BF16)
    return hi, lo


def _split3(x):
    hi = x.astype(BF16)
    r1 = x - hi.astype(F32)
    mid = r1.astype(BF16)
    lo = (r1 - mid.astype(F32)).astype(BF16)
    return hi, mid, lo


def _dot_exact_rhs(x, e):
    hi, mid, lo = _split3(x)
    return (jnp.dot(hi, e, preferred_element_type=F32) + jnp.dot(mid, e, preferred_element_type=F32)
            + jnp.dot(lo, e, preferred_element_type=F32))


def _dot_exact_lhs(e, x):
    hi, mid, lo = _split3(x)
    return (jnp.dot(e, hi, preferred_element_type=F32) + jnp.dot(e, mid, preferred_element_type=F32)
            + jnp.dot(e, lo, preferred_element_type=F32))


def _dot_f32ish(a, b):
    ah, al = _split2(a)
    bh, bl = _split2(b)
    return (jnp.dot(ah, bh, preferred_element_type=F32) + jnp.dot(ah, bl, preferred_element_type=F32)
            + jnp.dot(al, bh, preferred_element_type=F32))


def _lane_tile(x, width):
    return x if width == LANES else jnp.concatenate([x] * (width // LANES), axis=1)


def _rms(x, g, eps=NORM_EPS):
    return x * lax.rsqrt(jnp.mean(x * x, axis=-1, keepdims=True) + eps) * g


def _sigmoid(x):
    return 1.0 / (1.0 + jnp.exp(-x))


def _softplus(x):
    return jnp.maximum(x, 0.0) + jnp.log(1.0 + jnp.exp(-jnp.abs(x)))


def _fold_qlat_kernel(uq_ref, uk_ref, o_ref):
    a = uq_ref[...]
    b = uk_ref[...]
    ah, al = _split2(a)
    bh, bl = _split2(b)
    dn = (((1,), (1,)), ((), ()))
    o = (lax.dot_general(ah, bh, dn, preferred_element_type=F32)
         + lax.dot_general(ah, bl, dn, preferred_element_type=F32)
         + lax.dot_general(al, bh, dn, preferred_element_type=F32))
    o_ref[...] = o.astype(BF16)


def _fold_qlat(uq_nope, uk):
    return pl.pallas_call(
        _fold_qlat_kernel,
        grid=(MLA_HEADS,),
        in_specs=[pl.BlockSpec((None, MLA_Q_RANK, MLA_NOPE), lambda h: (h, 0, 0)),
                  pl.BlockSpec((None, MLA_KV_RANK, MLA_NOPE), lambda h: (h, 0, 0))],
        out_specs=pl.BlockSpec((MLA_Q_RANK, MLA_KV_RANK), lambda h: (0, h)),
        out_shape=jax.ShapeDtypeStruct((MLA_Q_RANK, MLA_HEADS * MLA_KV_RANK), BF16),
        compiler_params=_cparams(("arbitrary",)),
    )(uq_nope, uk)


def _rw_proj_kernel(x_ref, g_ref, wrw_ref, rw_ref):
    rw_ref[...] = _dot(_rms(x_ref[...], g_ref[...]), wrw_ref[...])


def _rw_proj(x, w, tm):
    n = x.shape[0]
    return pl.pallas_call(
        _rw_proj_kernel,
        grid=(n // tm,),
        in_specs=[_row_spec(tm, D_MODEL), _const_spec((1, D_MODEL)), _const_spec((D_MODEL, RWKV_PROJ))],
        out_specs=_row_spec(tm, RWKV_PROJ),
        out_shape=jax.ShapeDtypeStruct((n, RWKV_PROJ), F32),
        compiler_params=_cparams(("parallel",)),
    )(x, w["norm_mix"], w["w_rw"])


def _even_in_kernel(x_ref, g_ref, wq_ref, wckv_ref, wpa_ref, wpb_ref, qn_ref, kvn_ref,
                    cs_ref, sn_ref, wql_ref, wqa_ref, wqb_ref, cs8_ref, sn8_ref,
                    lat_ref, latb_ref, ql_ref, qpe_ref):
    xn = _rms(x_ref[...], g_ref[...]).astype(BF16)
    cq = _rms(_dot(xn, wq_ref[...]), qn_ref[...]).astype(BF16)
    ql_ref[...] = (_dot(cq, wql_ref[...]) * Q_PRESCALE).astype(BF16)
    qpe = _dot(cq, wqa_ref[...]) * cs8_ref[...] + _dot(cq, wqb_ref[...]) * sn8_ref[...]
    qpe_ref[...] = (qpe * Q_PRESCALE).astype(BF16)
    ckv = _rms(_dot(xn, wckv_ref[...]), kvn_ref[...])
    kpe = _dot(xn, wpa_ref[...]) * cs_ref[...] + _dot(xn, wpb_ref[...]) * sn_ref[...]
    lat_ref[:, :MLA_KV_RANK] = ckv
    lat_ref[:, MLA_KV_RANK:] = kpe[:, :MLA_ROPE]
    latb_ref[:, :MLA_KV_RANK] = ckv.astype(BF16)
    latb_ref[:, MLA_KV_RANK:] = kpe.astype(BF16)


def _even_in(x, w, tabs, tm):
    n = x.shape[0]
    nt = tabs["cs"].shape[0] // tm
    tab = lambda width: pl.BlockSpec((tm, width), lambda i: (i % nt, 0))
    hq = MLA_HEADS * MLA_KV_RANK
    hr = MLA_HEADS * LANES
    return pl.pallas_call(
        _even_in_kernel,
        grid=(n // tm,),
        in_specs=[_row_spec(tm, D_MODEL), _const_spec((1, D_MODEL)),
                  _const_spec((D_MODEL, MLA_Q_RANK)), _const_spec((D_MODEL, MLA_KV_RANK)),
                  _const_spec((D_MODEL, LANES)), _const_spec((D_MODEL, LANES)),
                  _const_spec((1, MLA_Q_RANK)),
                  _const_spec((1, MLA_KV_RANK)), tab(LANES), tab(LANES),
                  _const_spec((MLA_Q_RANK, hq)), _const_spec((MLA_Q_RANK, hr)),
                  _const_spec((MLA_Q_RANK, hr)), tab(hr), tab(hr)],
        out_specs=[_row_spec(tm, MLA_LAT), _row_spec(tm, MLA_LATB), _row_spec(tm, hq),
                   _row_spec(tm, hr)],
        out_shape=[jax.ShapeDtypeStruct((n, MLA_LAT), F32), jax.ShapeDtypeStruct((n, MLA_LATB), BF16),
                   jax.ShapeDtypeStruct((n, hq), BF16), jax.ShapeDtypeStruct((n, hr), BF16)],
        compiler_params=_cparams(("parallel",)),
    )(x, w["norm_mix"], w["w_q"], w["w_ckv"], w["w_pe_a"], w["w_pe_b"], w["q_norm"],
      w["kv_norm"], tabs["cs"], tabs["sn"], w["w_qlat"], w["w_qpe_a"], w["w_qpe_b"],
      tabs["cs8"], tabs["sn8"])


ATT_TQ = 512
ATT_TK = 512


def _mla_prompt_kernel(qi_ref, kj_ref, ql_ref, qpe_ref, lat_ref, o_ref,
                       m_sc, l_sc, a_sc, acc_sc, s_sc, p_sc):
    step = pl.program_id(1)
    i = qi_ref[step]
    j = kj_ref[step]
    heads = range(MLA_HEADS)

    @pl.when(j == 0)
    def _():
        m_sc[...] = jnp.full(m_sc.shape, NEG_BIG, F32)
        l_sc[...] = jnp.zeros(l_sc.shape, F32)
        acc_sc[...] = jnp.zeros(acc_sc.shape, F32)

    def tile(masked):
        ckv = lat_ref[:, :MLA_KV_RANK]
        kpe = lat_ref[:, MLA_KV_RANK:]
        for h in heads:
            s_sc[h] = (_dot_nt(ql_ref[:, h * MLA_KV_RANK:(h + 1) * MLA_KV_RANK], ckv)
                       + _dot_nt(qpe_ref[:, h * LANES:(h + 1) * LANES], kpe))
        for h in heads:
            s = s_sc[h]
            if masked:
                tok = lax.broadcasted_iota(jnp.int32, s.shape, 0) + offset
                key = lax.broadcasted_iota(jnp.int32, s.shape, 1)
                s = jnp.where(key <= tok, s, NEG_BIG)
            m_prev = m_sc[h]
            m_new = jnp.maximum(m_prev, jnp.max(s, axis=-1, keepdims=True))
            alpha = jnp.exp2(m_prev - m_new)
            p = jnp.exp2(s - _lane_tile(m_new, ATT_TK))
            l_sc[h] = alpha * l_sc[h] + jnp.sum(p, axis=-1, keepdims=True)
            m_sc[h] = m_new
            a_sc[h] = alpha
            p_sc[h] = p.astype(BF16)
        for h in heads:
            acc_sc[h] = _lane_tile(a_sc[h], MLA_KV_RANK) * acc_sc[h] + _dot(p_sc[h], ckv)

    offset = i * ATT_TQ - j * ATT_TK
    on_diagonal = offset < ATT_TK - 1

    @pl.when(jnp.logical_not(on_diagonal))
    def _():
        tile(False)

    @pl.when(on_diagonal)
    def _():
        tile(True)

    @pl.when(j == (i * ATT_TQ + ATT_TQ - 1) // ATT_TK)
    def _():
        for h in heads:
            o_ref[:, h * MLA_KV_RANK:(h + 1) * MLA_KV_RANK] = (
                acc_sc[h] / _lane_tile(l_sc[h], MLA_KV_RANK)).astype(BF16)


def _mla_prompt(q_lat, q_pe, lat_b, batch, seq):
    nq = seq // ATT_TQ
    nk = seq // ATT_TK
    pairs = [(i, j) for i in range(nq)
             for j in range((i * ATT_TQ + ATT_TQ - 1) // ATT_TK + 1)]
    qi = jnp.array([p[0] for p in pairs], jnp.int32)
    kj = jnp.array([p[1] for p in pairs], jnp.int32)
    hq = MLA_HEADS * MLA_KV_RANK
    grid_spec = pltpu.PrefetchScalarGridSpec(
        num_scalar_prefetch=2,
        grid=(batch, len(pairs)),
        in_specs=[pl.BlockSpec((ATT_TQ, hq), lambda b, s, qi, kj: (b * nq + qi[s], 0)),
                  pl.BlockSpec((ATT_TQ, MLA_HEADS * LANES), lambda b, s, qi, kj: (b * nq + qi[s], 0)),
                  pl.BlockSpec((ATT_TK, MLA_LATB), lambda b, s, qi, kj: (b * nk + kj[s], 0))],
        out_specs=pl.BlockSpec((ATT_TQ, hq), lambda b, s, qi, kj: (b * nq + qi[s], 0)),
        scratch_shapes=[pltpu.VMEM((MLA_HEADS, ATT_TQ, LANES), F32),
                        pltpu.VMEM((MLA_HEADS, ATT_TQ, LANES), F32),
                        pltpu.VMEM((MLA_HEADS, ATT_TQ, LANES), F32),
                        pltpu.VMEM((MLA_HEADS, ATT_TQ, MLA_KV_RANK), F32),
                        pltpu.VMEM((MLA_HEADS, ATT_TQ, ATT_TK), F32),
                        pltpu.VMEM((MLA_HEADS, ATT_TQ, ATT_TK), BF16)],
    )
    return pl.pallas_call(
        _mla_prompt_kernel,
        grid_spec=grid_spec,
        out_shape=jax.ShapeDtypeStruct(q_lat.shape, BF16),
        compiler_params=_cparams(("parallel", "arbitrary")),
    )(qi, kj, q_lat, q_pe, lat_b)


PAGES_PER_STEP = 64
DECODE_GROUPS = 16


def _mla_decode_kernel(pt_ref, q_ref, new_ref, *rest):
    page_refs = rest[:PAGES_PER_STEP]
    o_ref, m_sc, l_sc, acc_sc = rest[PAGES_PER_STEP:]
    j = pl.program_id(1)
    q = q_ref[0]

    @pl.when(j == 0)
    def _():
        m_sc[...] = jnp.full(m_sc.shape, NEG_BIG, F32)
        l_sc[...] = jnp.zeros(l_sc.shape, F32)
        acc_sc[...] = jnp.zeros(acc_sc.shape, F32)

    def update(state, s, values_t):
        m_prev, l_prev, acc = state
        m_new = jnp.maximum(m_prev, jnp.max(s, axis=-1, keepdims=True))
        alpha = jnp.exp2(m_prev - m_new)
        p = jnp.exp2(s - _lane_tile(m_new, s.shape[1]))
        l_new = alpha * l_prev + jnp.sum(p, axis=-1, keepdims=True)
        return m_new, l_new, _lane_tile(alpha, MLA_KV_RANK) * acc + _dot_nt(p, values_t)

    group = PAGES_PER_STEP // DECODE_GROUPS
    keys = [jnp.concatenate([pr[...].astype(BF16) for pr in page_refs[g * group:(g + 1) * group]],
                            axis=1) for g in range(DECODE_GROUPS)]
    scores = [_dot(q, kt) for kt in keys]
    state = (m_sc[...], l_sc[...], acc_sc[...])
    for s, kt in zip(scores, keys):
        state = update(state, s, kt[:MLA_KV_RANK, :])
    m_sc[...], l_sc[...], acc_sc[...] = state

    @pl.when(j == pl.num_programs(1) - 1)
    def _():
        new_t = new_ref[0]
        sn = _dot(q, new_t)
        tok = lax.broadcasted_iota(jnp.int32, sn.shape, 0) >> 3
        key = lax.broadcasted_iota(jnp.int32, sn.shape, 1)
        sn = jnp.where(key <= tok, sn, NEG_BIG)
        _, l_fin, acc_fin = update(state, sn, new_t[:MLA_KV_RANK, :])
        o_ref[0] = (acc_fin / _lane_tile(l_fin, MLA_KV_RANK)).astype(BF16)


def _mla_decode(page_table, q_full, new_pad_t, cache_t, layer):
    db, n_pages = page_table.shape
    rows = q_full.shape[1]
    steps = n_pages // PAGES_PER_STEP

    def page_spec(p):
        return pl.BlockSpec((None, None, MLA_LAT, PAGE_SIZE),
                            lambda b, j, pt: (layer, pt[b, j * PAGES_PER_STEP + p], 0, 0))

    grid_spec = pltpu.PrefetchScalarGridSpec(
        num_scalar_prefetch=1,
        grid=(db, steps),
        in_specs=[pl.BlockSpec((1, rows, MLA_LAT), lambda b, j, pt: (b, 0, 0)),
                  pl.BlockSpec((1, MLA_LAT, PAGE_SIZE), lambda b, j, pt: (b, 0, 0))]
        + [page_spec(p) for p in range(PAGES_PER_STEP)],
        out_specs=pl.BlockSpec((1, rows, MLA_KV_RANK), lambda b, j, pt: (b, 0, 0)),
        scratch_shapes=[pltpu.VMEM((rows, LANES), F32), pltpu.VMEM((rows, LANES), F32),
                        pltpu.VMEM((rows, MLA_KV_RANK), F32)],
    )
    return pl.pallas_call(
        _mla_decode_kernel,
        grid_spec=grid_spec,
        out_shape=jax.ShapeDtypeStruct((db, rows, MLA_KV_RANK), BF16),
        compiler_params=_cparams(("parallel", "arbitrary")),
    )(page_table, q_full, new_pad_t, *([cache_t] * PAGES_PER_STEP))


def _rwkv_prep_kernel(rw_ref, before_ref, sh_ref, mu_ref, w0_ref, w2_ref, a0_ref, a2_ref, g2_ref,
                      kk_ref, ka_ref, rk_ref, ones_ref, xs_ref, v_ref, g_ref, rkv_ref, *, tm, seq):
    rw = rw_ref[...]
    rolled = pltpu.roll(rw, 1, axis=0)
    row = lax.broadcasted_iota(jnp.int32, rw.shape, 0)
    if seq >= tm:
        at_start = pl.program_id(0) % (seq // tm) == 0
        first = jnp.where(at_start, sh_ref[...], before_ref[7:8, :])
        prev = jnp.where(row == 0, first, rolled)
    else:
        prev = jnp.where((row & (seq - 1)) == 0, sh_ref[...], rolled)
    xs = rw + (prev - rw) * mu_ref[...]
    d = RWKV_DIM
    r = xs[:, :d]
    k = xs[:, d:2 * d]
    v = xs[:, 2 * d:3 * d]
    xwa = xs[:, 3 * d:3 * d + LANES]
    xg = xs[:, 3 * d + LANES:]
    ones = ones_ref[...]
    w_log = -_softplus(-(w0_ref[...] + _dot(jnp.tanh(xwa), w2_ref[...]))) - 0.5
    a = _sigmoid(a0_ref[...] + _dot(xwa, a2_ref[...]))
    g_ref[...] = _dot(_sigmoid(xg), g2_ref[...])
    kk = k * kk_ref[...]
    ss = _dot_exact_rhs(kk * kk, ones)
    kk = kk / jnp.maximum(jnp.sqrt(ss), 1e-12)
    k2 = k * (1.0 + (a - 1.0) * ka_ref[...])
    xs_ref[0] = -kk
    xs_ref[1] = jnp.exp(-jnp.exp(w_log))
    xs_ref[2] = kk * a
    xs_ref[3] = k2
    xs_ref[4] = r
    v_ref[...] = v
    rkv_ref[...] = _dot_exact_rhs(r * k2 * rk_ref[...], ones) * v


def _rwkv_prep(rw, shift0, w, tm, seq):
    n = rw.shape[0]
    d = RWKV_DIM
    vec = _const_spec((1, d))
    if seq >= tm:
        tiles = seq // tm
        sh = shift0.reshape(-1, 1, RWKV_PROJ)
        sh_spec = pl.BlockSpec((None, 1, RWKV_PROJ), lambda i: (i // tiles, 0, 0))
    else:
        sh = jnp.repeat(shift0, seq, axis=0)
        sh_spec = _row_spec(tm, RWKV_PROJ)
    before_spec = pl.BlockSpec((8, RWKV_PROJ), lambda i: (jnp.maximum(i * (tm // 8) - 1, 0), 0))
    return pl.pallas_call(
        functools.partial(_rwkv_prep_kernel, tm=tm, seq=seq),
        grid=(n // tm,),
        in_specs=[_row_spec(tm, RWKV_PROJ), before_spec, sh_spec, _const_spec((1, RWKV_PROJ)),
                  vec, _const_spec((LANES, d)), vec, _const_spec((LANES, d)),
                  _const_spec((RWKV_G_LORA, d)), vec, vec, vec, _const_spec((d, d))],
        out_specs=[pl.BlockSpec((5, tm, d), lambda i: (0, i, 0))] + [_row_spec(tm, d)] * 3,
        out_shape=[jax.ShapeDtypeStruct((5, n, d), F32)] + [jax.ShapeDtypeStruct((n, d), F32)] * 3,
        compiler_params=_cparams(("parallel",)),
    )(rw, rw, sh, w["mu"], w["w0"], w["w2p"], w["a0"], w["a2p"], w["g2"], w["k_k"], w["k_a"],
      w["r_k"], w["ones_bd"])


SCAN_KH = RWKV_HEAD // 2
SCAN_PAIRS = LANES // 2
SCAN_VR = RWKV_HEAD // 2


def _rwkv_scan_kernel(x_ref, v_ref, s0_ref, *rest, tc):
    y_ref, s_ref, c_sc, d_sc = rest[-4:]
    @pl.when(pl.program_id(1) == 0)
    def _():
        s_ref[...] = s0_ref[...]

    half_a = slice(0, SCAN_VR)
    half_b = slice(SCAN_VR, RWKV_HEAD)

    def both_halves(p):
        return p + pltpu.roll(p, SCAN_PAIRS, axis=1)

    def key_dot(u, w):
        return both_halves(jnp.sum(u * w, axis=0, keepdims=True))

    def first_partial(rows):
        p = s_ref[0, 0, rows, :] * x_ref[0, 0, 0, 0:1, :]
        for k in range(1, SCAN_KH):
            p = p + s_ref[0, k, rows, :] * x_ref[0, 0, 0, k:k + 1, :]
        return p

    def half_step(t, rows, sa):
        v_half = v_ref[0, t, rows, :]
        v = jnp.concatenate([v_half, v_half], axis=1)
        q = None
        y = None
        for k in range(SCAN_KH):
            s_old = s_ref[0, k, rows, :]
            qk = s_old * c_sc[k:k + 1, :]
            sn = (s_old * x_ref[1, 0, t, k:k + 1, :] + sa * x_ref[2, 0, t, k:k + 1, :]
                  + v * x_ref[3, 0, t, k:k + 1, :])
            s_ref[0, k, rows, :] = sn
            yk = sn * x_ref[4, 0, t, k:k + 1, :]
            q = qk if q is None else q + qk
            y = yk if y is None else y + yk
        return q, sa * d_sc[0:1, :] + v * d_sc[1:2, :], y

    def store_y(t, y_a, y_b):
        y_ref[0, t, half_a, :] = both_halves(y_a)[:, :SCAN_PAIRS]
        y_ref[0, t, half_b, :] = both_halves(y_b)[:, :SCAN_PAIRS]

    def step(t, carry):
        sa_a, q_b, corr_b, y_a, y_b = carry
        store_y(jnp.maximum(t - 1, 0), y_a, y_b)
        a_next = x_ref[0, 0, jnp.minimum(t + 1, tc - 1)]
        c_sc[...] = x_ref[1, 0, t] * a_next
        d_sc[0:1, :] = key_dot(x_ref[2, 0, t], a_next)
        d_sc[1:2, :] = key_dot(x_ref[3, 0, t], a_next)
        sa_b = both_halves(q_b) + corr_b
        q_a, corr_a, y_a_new = half_step(t, half_a, sa_a)
        sa_a_next = both_halves(q_a) + corr_a
        q_b_next, corr_b_next, y_b_new = half_step(t, half_b, sa_b)
        return sa_a_next, q_b_next, corr_b_next, y_a_new, y_b_new

    zero = jnp.zeros((SCAN_VR, LANES), F32)
    init = (both_halves(first_partial(half_a)), first_partial(half_b), zero, zero, zero)
    final = lax.fori_loop(0, tc, step, init)
    store_y(tc - 1, final[3], final[4])


def _rwkv_scan(xs, v, s0, after, tc):
    _, nb, t, _, _ = xs.shape
    xspec = pl.BlockSpec((5, 1, tc, SCAN_KH, LANES), lambda n, c: (0, n, c, 0, 0))
    vspec = pl.BlockSpec((1, tc, RWKV_HEAD, SCAN_PAIRS), lambda n, c: (n, c, 0, 0))
    sspec = pl.BlockSpec((1, SCAN_KH, RWKV_HEAD, LANES), lambda n, c: (n, 0, 0, 0))
    return pl.pallas_call(
        functools.partial(_rwkv_scan_kernel, tc=tc),
        grid=(nb, t // tc),
        in_specs=[xspec, vspec, sspec] + [pl.BlockSpec(memory_space=pl.ANY)] * len(after),
        out_specs=[vspec, sspec],
        out_shape=[jax.ShapeDtypeStruct(v.shape, F32), jax.ShapeDtypeStruct(s0.shape, F32)],
        scratch_shapes=[pltpu.VMEM((SCAN_KH, LANES), F32), pltpu.VMEM((8, LANES), F32)],
        compiler_params=_cparams(("parallel", "arbitrary")),
    )(xs, v, s0, *after)


def _even_out_kernel(y_ref, rkv_ref, g_ref, lng_ref, lnb_ref, ones_ref, ol_ref, wuv_ref, woa_ref,
                     wob_ref, x_ref, o_ref):
    ones = ones_ref[...]
    y = y_ref[...]
    inv = 1.0 / RWKV_HEAD
    mean = _dot_exact_rhs(y, ones) * inv
    dlt = y - mean
    var = _dot_exact_rhs(dlt * dlt, ones) * inv
    yn = dlt * lax.rsqrt(var + RWKV_LN_EPS) * lng_ref[...] + lnb_ref[...] + rkv_ref[...]
    ob = (yn * g_ref[...]).astype(BF16)
    pair = 2 * MLA_KV_RANK
    oa = jnp.concatenate(
        [_dot(ol_ref[:, p * pair:(p + 1) * pair], wuv_ref[p]) for p in range(MLA_HEADS // 2)], axis=1)
    o_ref[...] = x_ref[...] + _dot(oa, woa_ref[...]) + _dot(ob, wob_ref[...])


def _even_out(y, rkv, g, o_lat, x, w, tm):
    n = x.shape[0]
    d = RWKV_DIM
    hq = MLA_HEADS * MLA_KV_RANK
    return pl.pallas_call(
        _even_out_kernel,
        grid=(n // tm,),
        in_specs=[_row_spec(tm, d), _row_spec(tm, d), _row_spec(tm, d), _const_spec((1, d)),
                  _const_spec((1, d)), _const_spec((d, d)), _row_spec(tm, hq),
                  _const_spec((MLA_HEADS // 2, 2 * MLA_KV_RANK, 2 * MLA_V)),
                  _const_spec((MLA_HEADS * MLA_V, D_MODEL)), _const_spec((d, D_MODEL)),
                  _row_spec(tm, D_MODEL)],
        out_specs=_row_spec(tm, D_MODEL),
        out_shape=jax.ShapeDtypeStruct((n, D_MODEL), F32),
        compiler_params=_cparams(("parallel",)),
    )(y, rkv, g, w["ln_g"], w["ln_b"], w["ones_bd"], o_lat, w["w_uv_bd"], w["w_out_a"],
      w["w_out_b"], x)


FFN_TF = 1408


def _ffn_kernel(x_ref, g_ref, wg_ref, wu_ref, wd_ref, o_ref, xn_sc, acc_sc):
    f = pl.program_id(1)

    @pl.when(f == 0)
    def _():
        xn_sc[...] = _rms(x_ref[...], g_ref[...]).astype(BF16)
        acc_sc[...] = jnp.zeros(acc_sc.shape, F32)

    xn = xn_sc[...]
    gate = _dot(xn, wg_ref[...])
    up = _dot(xn, wu_ref[...])
    acc_sc[...] += _dot(gate * _sigmoid(gate) * up, wd_ref[...])

    @pl.when(f == pl.num_programs(1) - 1)
    def _():
        o_ref[...] = x_ref[...] + acc_sc[...]


def _ffn(x, g, w_gu, w_down, tm):
    n = x.shape[0]
    nf = D_FF // FFN_TF
    return pl.pallas_call(
        _ffn_kernel,
        grid=(n // tm, nf),
        in_specs=[pl.BlockSpec((tm, D_MODEL), lambda i, f: (i, 0)),
                  pl.BlockSpec((1, D_MODEL), lambda i, f: (0, 0)),
                  pl.BlockSpec((D_MODEL, FFN_TF), lambda i, f: (0, f)),
                  pl.BlockSpec((D_MODEL, FFN_TF), lambda i, f: (0, nf + f)),
                  pl.BlockSpec((FFN_TF, D_MODEL), lambda i, f: (f, 0))],
        out_specs=pl.BlockSpec((tm, D_MODEL), lambda i, f: (i, 0)),
        out_shape=jax.ShapeDtypeStruct((n, D_MODEL), F32),
        scratch_shapes=[pltpu.VMEM((tm, D_MODEL), BF16), pltpu.VMEM((tm, D_MODEL), F32)],
        compiler_params=_cparams(("parallel", "arbitrary")),
    )(x, g, w_gu, w_gu, w_down)


def _odd_in_kernel(x_ref, g_ref, wq_ref, wk_ref, wv_ref, wg_ref, wxa_ref, a2_ref, ab_ref,
                   q_ref, k_ref, v_ref, gate_ref, la_ref):
    xn = _rms(x_ref[...], g_ref[...]).astype(BF16)
    q_ref[...] = _dot(xn, wq_ref[...]) * (GLA_DK ** -0.5)
    k_ref[...] = _dot(xn, wk_ref[...])
    v_ref[...] = _dot(xn, wv_ref[...])
    gate_ref[...] = _dot(xn, wg_ref[...])
    z = _dot(_dot(xn, wxa_ref[...]), a2_ref[...]) + ab_ref[...]
    la_ref[...] = -_softplus(-z) * (1.0 / GLA_GATE_NORM)


def _odd_in(x, w, tm):
    n = x.shape[0]
    return pl.pallas_call(
        _odd_in_kernel,
        grid=(n // tm,),
        in_specs=[_row_spec(tm, D_MODEL), _const_spec((1, D_MODEL)),
                  _const_spec((D_MODEL, GLA_KDIM)), _const_spec((D_MODEL, GLA_KDIM)),
                  _const_spec((D_MODEL, GLA_VDIM)), _const_spec((D_MODEL, GLA_VDIM)),
                  _const_spec((D_MODEL, LANES)), _const_spec((LANES, GLA_KDIM)),
                  _const_spec((1, GLA_KDIM))],
        out_specs=[_row_spec(tm, GLA_KDIM), _row_spec(tm, GLA_KDIM), _row_spec(tm, GLA_VDIM),
                   _row_spec(tm, GLA_VDIM), _row_spec(tm, GLA_KDIM)],
        out_shape=[jax.ShapeDtypeStruct((n, GLA_KDIM), F32), jax.ShapeDtypeStruct((n, GLA_KDIM), F32),
                   jax.ShapeDtypeStruct((n, GLA_VDIM), F32), jax.ShapeDtypeStruct((n, GLA_VDIM), F32),
                   jax.ShapeDtypeStruct((n, GLA_KDIM), F32)],
        compiler_params=_cparams(("parallel",)),
    )(x, w["norm_mix"], w["w_q"], w["w_k"], w["w_v"], w["w_g"], w["w_xa"], w["a2p"], w["ab"])


def _gla_kernel(q_ref, k_ref, v_ref, la_ref, s0_ref, o_ref, st_ref):
    c = GLA_CHUNK

    @pl.when(pl.program_id(1) == 0)
    def _():
        st_ref[...] = s0_ref[...]

    row = lax.broadcasted_iota(jnp.int32, (c, c), 0)
    col = lax.broadcasted_iota(jnp.int32, (c, c), 1)
    tri = row >= col
    tri_b = jnp.where(tri, 1.0, 0.0).astype(BF16)
    heads = range(GLA_HEADS)
    ks = [slice(h * GLA_DK, (h + 1) * GLA_DK) for h in heads]
    vs = [slice(h * GLA_DV, (h + 1) * GLA_DV) for h in heads]
    b = _dot_exact_lhs(tri_b, la_ref[...])
    k = k_ref[...]
    b_end = b[c - 1:c, :]
    qe = (q_ref[...] * jnp.exp(b)).astype(BF16)
    ke = (k * jnp.exp(-b)).astype(BF16)
    k_end = (k * jnp.exp(b_end - b)).astype(BF16)
    e_end = jnp.exp(b_end)
    a_mats = [jnp.where(tri, _dot_nt(qe[:, ks[h]], ke[:, ks[h]]), 0.0).astype(BF16) for h in heads]
    states = [st_ref[0, h] for h in heads]
    for h in heads:
        o_ref[:, vs[h]] = _dot_nt(qe[:, ks[h]], states[h]) + _dot(a_mats[h], v_ref[:, vs[h]])
    for h in heads:
        st_ref[0, h] = states[h] * e_end[:, ks[h]] + _dot(v_ref[:, vs[h]].T, k_end[:, ks[h]])


def _gla(q, k, v, la, s0t, batch, seq):
    nc = seq // GLA_CHUNK
    rspec = lambda width: pl.BlockSpec((GLA_CHUNK, width), lambda b, c: (b * nc + c, 0))
    sspec = pl.BlockSpec((1, GLA_HEADS, GLA_DV, GLA_DK), lambda b, c: (b, 0, 0, 0))
    return pl.pallas_call(
        _gla_kernel,
        grid=(batch, nc),
        in_specs=[rspec(GLA_KDIM), rspec(GLA_KDIM), rspec(GLA_VDIM), rspec(GLA_KDIM), sspec],
        out_specs=[rspec(GLA_VDIM), sspec],
        out_shape=[jax.ShapeDtypeStruct(v.shape, F32), jax.ShapeDtypeStruct(s0t.shape, F32)],
        compiler_params=_cparams(("parallel", "arbitrary")),
    )(q, k, v, la, s0t)


def _odd_out_kernel(o_ref, gate_ref, gn_ref, wo_ref, x_ref, y_ref):
    parts = []
    for h in range(GLA_HEADS):
        vs = slice(h * GLA_DV, (h + 1) * GLA_DV)
        parts.append(_rms(o_ref[:, vs], gn_ref[:, vs]))
    gate = gate_ref[...]
    on = jnp.concatenate(parts, axis=1) * (gate * _sigmoid(gate))
    y_ref[...] = x_ref[...] + _dot(on, wo_ref[...])


def _odd_out(o, gate, x, w, tm):
    n = x.shape[0]
    return pl.pallas_call(
        _odd_out_kernel,
        grid=(n // tm,),
        in_specs=[_row_spec(tm, GLA_VDIM), _row_spec(tm, GLA_VDIM), _const_spec((1, GLA_VDIM)),
                  _const_spec((GLA_VDIM, D_MODEL)), _row_spec(tm, D_MODEL)],
        out_specs=_row_spec(tm, D_MODEL),
        out_shape=jax.ShapeDtypeStruct((n, D_MODEL), F32),
        compiler_params=_cparams(("parallel",)),
    )(o, gate, w["gla_norm"], w["w_out"], x)


def _router_kernel(x_ref, g_ref, wr_ref, *rest):
    xn_ref, idx_ref, gate_ref = rest[-3:]
    xn = _rms(x_ref[...], g_ref[...])
    half = D_MODEL // 2
    xn_ref[0] = xn[:, :half]
    xn_ref[1] = xn[:, half:]
    logits = _dot_f32ish(xn, wr_ref[...])
    lane = lax.broadcasted_iota(jnp.int32, logits.shape, 1)
    logits = jnp.where(lane < N_EXPERTS, logits, NEG_BIG)
    m1 = jnp.max(logits, axis=-1, keepdims=True)
    i1 = jnp.min(jnp.where(logits == m1, lane, LANES), axis=-1, keepdims=True)
    rest = jnp.where(lane == i1, NEG_BIG, logits)
    m2 = jnp.max(rest, axis=-1, keepdims=True)
    i2 = jnp.min(jnp.where(rest == m2, lane, LANES), axis=-1, keepdims=True)
    e2 = jnp.exp(m2 - m1)
    g1 = 1.0 / (1.0 + e2)
    g2 = e2 / (1.0 + e2)
    idx_ref[...] = jnp.where(lane == 0, i1, jnp.where(lane == 1, i2, 0))
    gate_ref[...] = jnp.where(lane == 0, g1, jnp.where(lane == 1, g2, 0.0))


def _router(x, g, wr, tm, n_total, row0, prev=None):
    n = x.shape[0]
    half = D_MODEL // 2
    blk0 = row0 // tm
    prev = () if prev is None else tuple(prev)
    return pl.pallas_call(
        _router_kernel,
        grid=(n // tm,),
        in_specs=[_row_spec(tm, D_MODEL), _const_spec((1, D_MODEL)), _const_spec((D_MODEL, LANES))]
        + [pl.BlockSpec(memory_space=pl.ANY)] * len(prev),
        out_specs=[pl.BlockSpec((2, tm, half), lambda i: (0, blk0 + i, 0)),
                   pl.BlockSpec((tm, LANES), lambda i: (blk0 + i, 0)),
                   pl.BlockSpec((tm, LANES), lambda i: (blk0 + i, 0))],
        out_shape=[jax.ShapeDtypeStruct((2, n_total, half), F32),
                   jax.ShapeDtypeStruct((n_total, LANES), jnp.int32),
                   jax.ShapeDtypeStruct((n_total, LANES), F32)],
        input_output_aliases={3 + k: k for k in range(len(prev))},
        compiler_params=_cparams(("parallel",)),
    )(x, g, wr, *prev)


MOE_TF = 1792
MOE_TMC = 256


def _route(top_i, tm, tmc):
    n = top_i.shape[0]
    slots = 2 * n
    n_tiles = -(-(slots + N_EXPERTS * (tm - 1)) // tm)
    win = tmc + 8
    e_flat = top_i.reshape(-1)
    onehot = (e_flat[:, None] == jnp.arange(N_EXPERTS, dtype=jnp.int32)[None, :]).astype(jnp.int32)
    csum = jnp.cumsum(onehot, axis=0)
    rank = jnp.sum(onehot * csum, axis=1) - 1
    counts = csum[-1]
    padded = ((counts + tm - 1) // tm) * tm
    ends = jnp.cumsum(padded)
    starts = ends - padded
    dest = (jnp.sum(onehot * starts[None, :], axis=1) + rank).astype(jnp.int32)
    tile_start = jnp.arange(n_tiles, dtype=jnp.int32) * tm
    tile_expert = jnp.minimum(jnp.sum((tile_start[:, None] >= ends[None, :]).astype(jnp.int32), axis=1),
                              N_EXPERTS - 1).astype(jnp.int32)
    tile_valid = (tile_start < ends[-1]).astype(jnp.int32)
    src = jnp.zeros((n_tiles * tm,), jnp.int32).at[dest].set(jnp.arange(slots, dtype=jnp.int32) // 2)
    before = jnp.concatenate([jnp.zeros((1, N_EXPERTS), jnp.int32), csum[2 * tmc - 1:-1:2 * tmc]], axis=0)
    wstart = jnp.clip(((starts[None, :] + before) // 8) * 8, 0, n_tiles * tm - win).astype(jnp.int32)
    ws_slot = jnp.sum(onehot * jnp.repeat(wstart, 2 * tmc, axis=0), axis=1)
    local = (e_flat * win + dest - ws_slot).astype(jnp.int32)
    return local, wstart.reshape(-1), src, tile_expert, tile_valid


def _moe_gather_kernel(src_ref, x_ref, o_ref, *, tg):
    base = pl.program_id(1) * tg

    def body(r, carry):
        o_ref[pl.ds(r, 1), :] = x_ref[pl.ds(src_ref[base + r], 1), :]
        return carry

    lax.fori_loop(0, tg, body, 0, unroll=8)


def _moe_gather(src, xn2, tg):
    rows = src.shape[0]
    _, n, half = xn2.shape
    grid_spec = pltpu.PrefetchScalarGridSpec(
        num_scalar_prefetch=1,
        grid=(2, rows // tg),
        in_specs=[pl.BlockSpec((None, n, half), lambda h, i, s: (h, 0, 0),
                               pipeline_mode=pl.Buffered(1))],
        out_specs=pl.BlockSpec((tg, half), lambda h, i, s: (i, h)),
    )
    return pl.pallas_call(
        functools.partial(_moe_gather_kernel, tg=tg),
        grid_spec=grid_spec,
        out_shape=jax.ShapeDtypeStruct((rows, 2 * half), F32),
        compiler_params=_cparams(("arbitrary", "arbitrary")),
    )(src, xn2)


def _moe_up_kernel(te_ref, tv_ref, xs_ref, wg_ref, wu_ref, h_ref):
    @pl.when(tv_ref[pl.program_id(1)] != 0)
    def _():
        xs = xs_ref[...].astype(BF16)
        gate = _dot(xs, wg_ref[...])
        up = _dot(xs, wu_ref[...])
        h_ref[...] = (gate * _sigmoid(gate) * up).astype(BF16)


def _moe_up(te, tv, xs, w_gu, layer, tm):
    rows = xs.shape[0]
    nf = D_FF_EXPERT // MOE_TF
    wspec = lambda off: pl.BlockSpec((None, None, D_MODEL, MOE_TF),
                                     lambda f, t, te, tv: (layer, te[t], 0, off + f))
    grid_spec = pltpu.PrefetchScalarGridSpec(
        num_scalar_prefetch=2,
        grid=(nf, rows // tm),
        in_specs=[pl.BlockSpec((tm, D_MODEL), lambda f, t, te, tv: (t, 0)), wspec(0), wspec(nf)],
        out_specs=pl.BlockSpec((tm, MOE_TF), lambda f, t, te, tv: (t, f)),
    )
    return pl.pallas_call(
        _moe_up_kernel,
        grid_spec=grid_spec,
        out_shape=jax.ShapeDtypeStruct((rows, D_FF_EXPERT), BF16),
        compiler_params=_cparams(("arbitrary", "arbitrary")),
    )(te, tv, xs, w_gu, w_gu)


def _moe_down_kernel(te_ref, tv_ref, h_ref, wd_ref, y_ref):
    @pl.when(tv_ref[pl.program_id(0)] != 0)
    def _():
        y_ref[...] = _dot(h_ref[...], wd_ref[...])


def _moe_down(te, tv, h, w_down, layer, tm):
    rows = h.shape[0]
    grid_spec = pltpu.PrefetchScalarGridSpec(
        num_scalar_prefetch=2,
        grid=(rows // tm,),
        in_specs=[pl.BlockSpec((tm, D_FF_EXPERT), lambda t, te, tv: (t, 0)),
                  pl.BlockSpec((None, None, D_FF_EXPERT, D_MODEL),
                               lambda t, te, tv: (layer, te[t], 0, 0))],
        out_specs=pl.BlockSpec((tm, D_MODEL), lambda t, te, tv: (t, 0)),
    )
    return pl.pallas_call(
        _moe_down_kernel,
        grid_spec=grid_spec,
        out_shape=jax.ShapeDtypeStruct((rows, D_MODEL), F32),
        compiler_params=_cparams(("arbitrary",)),
    )(te, tv, h, w_down)


def _moe_combine_kernel(ws_ref, local_ref, *refs, tmc, win, tile0):
    win_refs = refs[:N_EXPERTS]
    g1_ref, g2_ref, x_ref, fn_ref, o_ref, buf = refs[N_EXPERTS:]
    for e in range(N_EXPERTS):
        buf[e * win:(e + 1) * win, :] = win_refs[e][...]
    base = 2 * (tile0 + pl.program_id(0)) * tmc

    def body(r, carry):
        row = pl.ds(r, 1)
        y1 = buf[pl.ds(local_ref[base + 2 * r], 1), :]
        y2 = buf[pl.ds(local_ref[base + 2 * r + 1], 1), :]
        g1 = g1_ref[row, :]
        g2 = g2_ref[row, :]
        parts = []
        for c in range(D_MODEL // LANES):
            cs = slice(c * LANES, (c + 1) * LANES)
            parts.append(g1 * y1[:, cs] + g2 * y2[:, cs])
        o_ref[row, :] = x_ref[row, :] + jnp.concatenate(parts, axis=1)
        return carry

    lax.fori_loop(0, tmc, body, 0, unroll=16)
    o_ref[...] = _rms(o_ref[...], fn_ref[...])


def _moe_combine(local, wstart, ys, g1b, g2b, x, fn, tmc, row0):
    n = x.shape[0]
    win = tmc + 8
    tile0 = row0 // tmc

    def win_spec(e):
        return pl.BlockSpec(
            (pl.Element(win), pl.Element(D_MODEL)),
            lambda i, ws, lo: (pl.multiple_of(ws[(tile0 + i) * N_EXPERTS + e], 8), 0))

    grid_spec = pltpu.PrefetchScalarGridSpec(
        num_scalar_prefetch=2,
        grid=(n // tmc,),
        in_specs=[win_spec(e) for e in range(N_EXPERTS)]
        + [pl.BlockSpec((tmc, LANES), lambda i, ws, lo: (tile0 + i, 0)),
           pl.BlockSpec((tmc, LANES), lambda i, ws, lo: (tile0 + i, 0)),
           pl.BlockSpec((tmc, D_MODEL), lambda i, ws, lo: (i, 0)),
           pl.BlockSpec((1, D_MODEL), lambda i, ws, lo: (0, 0))],
        out_specs=pl.BlockSpec((tmc, D_MODEL), lambda i, ws, lo: (i, 0)),
        scratch_shapes=[pltpu.VMEM((N_EXPERTS * win, D_MODEL), F32)],
    )
    return pl.pallas_call(
        functools.partial(_moe_combine_kernel, tmc=tmc, win=win, tile0=tile0),
        grid_spec=grid_spec,
        out_shape=jax.ShapeDtypeStruct((n, D_MODEL), F32),
        compiler_params=_cparams(("arbitrary",)),
    )(wstart, local, *([ys] * N_EXPERTS), g1b, g2b, x, fn)


def _scan_vec_layout(xs, batch, seq):
    nb = batch * RWKV_HEADS // SCAN_PAIRS
    x = xs.reshape(5, batch, seq, RWKV_HEADS, 2, SCAN_KH).transpose(0, 2, 5, 4, 1, 3)
    x = x.reshape(5, seq, SCAN_KH, 2, nb, SCAN_PAIRS).transpose(0, 4, 1, 2, 3, 5)
    return x.reshape(5, nb, seq, SCAN_KH, LANES)


def _scan_val_layout(v, batch, seq):
    nb = batch * RWKV_HEADS // SCAN_PAIRS
    v4 = v.reshape(batch, seq, RWKV_HEADS, RWKV_HEAD).transpose(1, 3, 0, 2)
    return v4.reshape(seq, RWKV_HEAD, nb, SCAN_PAIRS).transpose(2, 0, 1, 3)


def _scan_val_unlayout(y, batch, seq):
    v4 = y.transpose(1, 2, 0, 3).reshape(seq, RWKV_HEAD, batch, RWKV_HEADS)
    return v4.transpose(2, 0, 3, 1).reshape(batch * seq, RWKV_DIM)


def _scan_state_layout(s, batch):
    nb = batch * RWKV_HEADS // SCAN_PAIRS
    s6 = s.reshape(batch, RWKV_HEADS, RWKV_HEAD, 2, SCAN_KH).transpose(4, 2, 3, 0, 1)
    s6 = s6.reshape(SCAN_KH, RWKV_HEAD, 2, nb, SCAN_PAIRS).transpose(3, 0, 1, 2, 4)
    return s6.reshape(nb, SCAN_KH, RWKV_HEAD, LANES)


def _scan_state_unlayout(arr, batch):
    nb = arr.shape[0]
    s = arr.reshape(nb, SCAN_KH, RWKV_HEAD, 2, SCAN_PAIRS).transpose(0, 4, 2, 3, 1)
    return s.reshape(batch, RWKV_HEADS, RWKV_HEAD, RWKV_HEAD)


def _swap_halves(w):
    half = w.shape[-1] // 2
    return jnp.concatenate([w[..., half:], w[..., :half]], axis=-1)


def _prep_even(i, norm_mix, norm_ffn, w_in, q_norm, kv_norm, w_uq, w_uk, w_uv, mu, w0, w2, a0, a2,
               g2, k_k, k_a, r_k, ln_g, ln_b, w_out, ffn_gu, ffn_down):
    w = {}
    row = lambda v: v[i].reshape(1, -1)
    w_in = w_in[i]
    w["norm_mix"] = row(norm_mix)
    w["norm_ffn"] = row(norm_ffn)
    w["w_q"] = w_in[:, :MLA_Q_RANK].astype(BF16)
    w_kv = w_in[:, MLA_Q_RANK:MLA_Q_RANK + MLA_LAT]
    w["w_ckv"] = w_kv[:, :MLA_KV_RANK].astype(BF16)
    lane_pad = lambda m: jnp.pad(m, [(0, 0)] * (m.ndim - 1) + [(0, LANES - m.shape[-1])])
    w["w_pe_a"] = lane_pad(w_kv[:, MLA_KV_RANK:]).astype(BF16)
    w["w_pe_b"] = lane_pad(_swap_halves(w_kv[:, MLA_KV_RANK:])).astype(BF16)
    w["w_rw"] = w_in[:, MLA_Q_RANK + MLA_LAT:].astype(BF16)
    w["q_norm"] = row(q_norm)
    w["kv_norm"] = row(kv_norm)
    uq = w_uq[i].reshape(MLA_Q_RANK, MLA_HEADS, MLA_NOPE + MLA_ROPE)
    uq_pe = uq[:, :, MLA_NOPE:]
    w["w_qpe_a"] = lane_pad(uq_pe).reshape(MLA_Q_RANK, -1).astype(BF16)
    w["w_qpe_b"] = lane_pad(_swap_halves(uq_pe)).reshape(MLA_Q_RANK, -1).astype(BF16)
    w["w_qlat"] = _fold_qlat(uq[:, :, :MLA_NOPE].transpose(1, 0, 2), w_uk[i].transpose(1, 0, 2))
    uv = w_uv[i].transpose(1, 0, 2).reshape(MLA_HEADS // 2, 2, MLA_KV_RANK, MLA_V)
    zero = jnp.zeros_like(uv[:, 0])
    w["w_uv_bd"] = jnp.concatenate(
        [jnp.concatenate([uv[:, 0], zero], axis=-1), jnp.concatenate([zero, uv[:, 1]], axis=-1)],
        axis=1).astype(BF16)
    w["mu"] = row(mu)
    w["w0"] = row(w0)
    pad = lambda m, before: jnp.pad(m, ((before, LANES - before - m.shape[0]), (0, 0))).astype(BF16)
    w["w2p"] = pad(w2[i], 0)
    w["a2p"] = pad(a2[i], RWKV_W_LORA)
    w["a0"] = row(a0)
    w["g2"] = g2[i].astype(BF16)
    w["k_k"] = row(k_k)
    w["k_a"] = row(k_a)
    w["r_k"] = row(r_k)
    w["ln_g"] = row(ln_g)
    w["ln_b"] = row(ln_b)
    head = jnp.arange(RWKV_DIM) // RWKV_HEAD
    w["ones_bd"] = (head[:, None] == head[None, :]).astype(BF16)
    w["w_out_a"] = w_out[i][:MLA_HEADS * MLA_V].astype(BF16)
    w["w_out_b"] = w_out[i][MLA_HEADS * MLA_V:].astype(BF16)
    w["ffn_gu"] = ffn_gu[i].astype(BF16)
    w["ffn_down"] = ffn_down[i].astype(BF16)
    return w


def _prep_odd(i, norm_mix, norm_ffn, w_in, a2, ab, gla_norm, w_out, router, moe_gu, moe_down):
    w = {}
    row = lambda v: v[i].reshape(1, -1)
    w_in = w_in[i]
    w["norm_mix"] = row(norm_mix)
    w["norm_ffn"] = row(norm_ffn)
    w["w_q"] = w_in[:, :GLA_KDIM].astype(BF16)
    w["w_k"] = w_in[:, GLA_KDIM:2 * GLA_KDIM].astype(BF16)
    w["w_v"] = w_in[:, 2 * GLA_KDIM:2 * GLA_KDIM + GLA_VDIM].astype(BF16)
    w["w_g"] = w_in[:, 2 * GLA_KDIM + GLA_VDIM:2 * GLA_KDIM + 2 * GLA_VDIM].astype(BF16)
    w["w_xa"] = jnp.pad(w_in[:, 2 * GLA_KDIM + 2 * GLA_VDIM:],
                        ((0, 0), (0, LANES - GLA_GATE_RANK))).astype(BF16)
    w["a2p"] = jnp.pad(a2[i], ((0, LANES - GLA_GATE_RANK), (0, 0))).astype(BF16)
    w["ab"] = row(ab)
    w["gla_norm"] = row(gla_norm)
    w["w_out"] = w_out[i].astype(BF16)
    w["router"] = jnp.pad(router[i], ((0, 0), (0, LANES - N_EXPERTS)))
    w["layer"] = i
    w["moe_gu"] = moe_gu
    w["moe_down"] = moe_down
    return w


def _rope_tables(pos, reps):
    inv = ROPE_THETA ** (-jnp.arange(0, MLA_ROPE, 2, dtype=F32) / MLA_ROPE)
    ang = pos.astype(F32)[:, None] * inv[None, :]
    cos, sin = jnp.cos(ang), jnp.sin(ang)
    pad = ((0, 0), (0, LANES - MLA_ROPE))
    cs = jnp.tile(jnp.pad(jnp.concatenate([cos, cos], axis=-1), pad), (reps, 1))
    sn = jnp.tile(jnp.pad(jnp.concatenate([-sin, sin], axis=-1), pad), (reps, 1))
    return {"cs": cs, "sn": sn, "cs8": jnp.tile(cs, (1, MLA_HEADS)), "sn8": jnp.tile(sn, (1, MLA_HEADS))}


def _even_layer(x, batch, seq, tabs, state, shift0, past, w, tm, tc, after=()):
    n = batch * seq
    rw = _rw_proj(x, w, tm)
    lat, lat_b, q_lat, q_pe = _even_in(x, w, tabs, tm)
    if past is None:
        o_lat = _mla_prompt(q_lat, q_pe, lat_b, batch, seq)
    else:
        cache, layer, page_table = past
        rows = seq * MLA_HEADS
        q_full = jnp.concatenate([q_lat.reshape(batch, rows, MLA_KV_RANK),
                                  q_pe.reshape(batch, rows, LANES)[:, :, :MLA_ROPE]], axis=-1)
        new_pad_t = jnp.pad(lat_b.reshape(batch, seq, MLA_LATB)[:, :, :MLA_LAT],
                            ((0, 0), (0, PAGE_SIZE - seq), (0, 0))).transpose(0, 2, 1)
        o_lat = _mla_decode(page_table, q_full, new_pad_t, cache.transpose(0, 1, 3, 2), layer)
        o_lat = o_lat.reshape(n, MLA_HEADS * MLA_KV_RANK)

    rw3 = rw.reshape(batch, seq, RWKV_PROJ)
    xs5, v, g, rkv = _rwkv_prep(rw, shift0, w, tm, seq)
    y_l, s_l = _rwkv_scan(_scan_vec_layout(xs5, batch, seq), _scan_val_layout(v, batch, seq),
                          _scan_state_layout(state, batch), (o_lat,) + tuple(after), tc)
    y = _scan_val_unlayout(y_l, batch, seq)
    new_state = _scan_state_unlayout(s_l, batch)

    x = _even_out(y, rkv, g, o_lat, x, w, tm)
    x = _ffn(x, w["norm_ffn"], w["ffn_gu"], w["ffn_down"], tm)
    return x, lat.reshape(batch, seq, MLA_LAT), new_state, rw3[:, -1], o_lat


def _odd_mixer_layer(x, batch, seq, state, w, tm):
    q, k, v, gate, la = _odd_in(x, w, tm)
    seq_p = -(-seq // GLA_CHUNK) * GLA_CHUNK
    if seq_p != seq:
        padr = lambda t: jnp.pad(t.reshape(batch, seq, -1), ((0, 0), (0, seq_p - seq), (0, 0))
                                 ).reshape(batch * seq_p, -1)
        qp, kp, vp, lap = padr(q), padr(k), padr(v), padr(la)
    else:
        qp, kp, vp, lap = q, k, v, la
    o, st = _gla(qp, kp, vp, lap, state.transpose(0, 1, 3, 2), batch, seq_p)
    if seq_p != seq:
        o = o.reshape(batch, seq_p, GLA_VDIM)[:, :seq].reshape(batch * seq, GLA_VDIM)
    return _odd_out(o, gate, x, w, tm), st.transpose(0, 1, 3, 2)


def _moe_all_groups(xs_groups, tms, w, final_norm, tm_moe):
    sizes = [x.shape[0] for x in xs_groups]
    n_total = sum(sizes)
    bufs, row0 = None, 0
    for x, tm in zip(xs_groups, tms):
        bufs = _router(x, w["norm_ffn"], w["router"], tm, n_total, row0, bufs)
        row0 += x.shape[0]
    xn2, idx, gates = bufs
    local, wstart, src, tile_expert, tile_valid = _route(idx[:, :2], tm_moe, MOE_TMC)
    rows = _moe_gather(src, xn2, tm_moe)
    h = _moe_up(tile_expert, tile_valid, rows, w["moe_gu"], w["layer"], tm_moe)
    ys = _moe_down(tile_expert, tile_valid, h, w["moe_down"], w["layer"], tm_moe)
    g1b = jnp.broadcast_to(gates[:, 0:1], (n_total, LANES))
    g2b = jnp.broadcast_to(gates[:, 1:2], (n_total, LANES))
    outs, row0 = [], 0
    for x in xs_groups:
        outs.append(_moe_combine(local, wstart, ys, g1b, g2b, x, final_norm, MOE_TMC, row0))
        row0 += x.shape[0]
    return outs


def kernel(x_prompt, x_sample, cache_mla, state_rwkv, state_rwkv_shift, state_gla, page_table, norm_mix_even, norm_ffn_even, w_in_even, mla_q_norm, mla_kv_norm, mla_w_uq, mla_w_uk, mla_w_uv, rwkv_mu, rwkv_w0, rwkv_w2, rwkv_a0, rwkv_a2, rwkv_g2, rwkv_k_k, rwkv_k_a, rwkv_r_k, rwkv_ln_g, rwkv_ln_b, w_out_even, ffn_w_gu_even, ffn_w_down_even, norm_mix_odd, norm_ffn_odd, w_in_odd, gla_a2, gla_ab, gla_norm, w_out_odd, moe_router, moe_w_gu, moe_w_down, final_norm):
    bp, tp, _ = x_prompt.shape
    bs, ts, _ = x_sample.shape
    past_len = page_table.shape[1] * PAGE_SIZE
    tm_p, tm_s = 512, bs * ts
    we = _prep_even(0, norm_mix_even, norm_ffn_even, w_in_even, mla_q_norm, mla_kv_norm, mla_w_uq,
                    mla_w_uk, mla_w_uv, rwkv_mu, rwkv_w0, rwkv_w2, rwkv_a0, rwkv_a2, rwkv_g2,
                    rwkv_k_k, rwkv_k_a, rwkv_r_k, rwkv_ln_g, rwkv_ln_b, w_out_even, ffn_w_gu_even,
                    ffn_w_down_even)
    wo = _prep_odd(0, norm_mix_odd, norm_ffn_odd, w_in_odd, gla_a2, gla_ab, gla_norm, w_out_odd,
                   moe_router, moe_w_gu, moe_w_down)
    fn = final_norm.reshape(1, -1)
    tabs_p = _rope_tables(jnp.arange(tp), 1)
    tabs_s = _rope_tables(past_len + jnp.arange(ts), bs)

    hp = x_prompt.reshape(bp * tp, D_MODEL)
    hs = x_sample.reshape(bs * ts, D_MODEL)
    zeros_state = jnp.zeros((bp, RWKV_HEADS, RWKV_HEAD, RWKV_HEAD), F32)
    zeros_shift = jnp.zeros((bp, RWKV_PROJ), F32)
    hp, lat_p, rs_p, sh_p, _ = _even_layer(hp, bp, tp, tabs_p, zeros_state, zeros_shift, None, we,
                                           tm_p, 64)
    hs, lat_s, rs_s, sh_s, _ = _even_layer(hs, bs, ts, tabs_s, state_rwkv[0], state_rwkv_shift[0],
                                           (cache_mla, 0, page_table), we, tm_s, ts)
    zeros_gla = jnp.zeros((bp, GLA_HEADS, GLA_DK, GLA_DV), F32)
    hp, gs_p = _odd_mixer_layer(hp, bp, tp, zeros_gla, wo, tm_p)
    hs, gs_s = _odd_mixer_layer(hs, bs, ts, state_gla[0], wo, tm_s)
    yp, ys = _moe_all_groups([hp, hs], [tm_p, tm_s], wo, fn, 512)
    return (yp.reshape(bp, tp, D_MODEL), ys.reshape(bs, ts, D_MODEL), lat_p[None], lat_s[None],
            rs_p[None], rs_s[None], sh_p[None], sh_s[None], gs_p[None], gs_s[None])
```
